```python
import math
import jax, jax.numpy as jnp
from jax import lax
import numpy as np

D_MODEL = 2048
BATCH = 2
SEQ = 4096
DEPTH = 1

N_HEADS = 8
N_KV = 2
HPG = N_HEADS // N_KV
HEAD_DIM = 128
ATTN_WIDTH = N_HEADS * HEAD_DIM
KV_WIDTH = N_KV * HEAD_DIM
CONV_WIDTH = D_MODEL - ATTN_WIDTH
MIX_WIDTH = ATTN_WIDTH + CONV_WIDTH
CONV_K = 3
N_BRANCH = 3
CMP_LEN = 32
CMP_STRIDE = 16
CMP_HIDDEN = 256
SEL_LEN = 64
SEL_TOPK = 16
WINDOW = 512
Q_BLOCK = 128
N_BUCKETS = 32
MAX_DIST = 128
D_FF = 4 * D_MODEL
EPS = 1e-6
NEG = -1e30
FORCE = 1e9
SPLIT_SIZES = [ATTN_WIDTH] + [KV_WIDTH] * 6 + [N_HEADS * N_BRANCH] + [CONV_WIDTH] * 3
IN_WIDTH = sum(SPLIT_SIZES)

kernel_name = "hymba_nsa_shortconv_sandwich_layer"


def rmsnorm(x, g):
    xf = x.astype(jnp.float32)
    y = xf * lax.rsqrt(jnp.mean(xf * xf, axis=-1, keepdims=True) + EPS)
    return (y * g.astype(jnp.float32)).astype(x.dtype)


def rel_bucket(dist):
    n = jnp.maximum(dist, 0)
    max_exact = N_BUCKETS // 2
    nf = jnp.maximum(n, 1).astype(jnp.float32)
    large = max_exact + (jnp.log(nf / max_exact) / math.log(MAX_DIST / max_exact)
                         * (N_BUCKETS - max_exact)).astype(jnp.int32)
    large = jnp.minimum(large, N_BUCKETS - 1)
    return jnp.where(n < max_exact, n, large)


def masked_softmax(s, valid):
    s = jnp.where(valid, s.astype(jnp.float32), NEG)
    p = jax.nn.softmax(s, axis=-1)
    return jnp.where(valid, p, 0.0)


def compress(raw, pe, w1, w2):
    t = raw.shape[1]
    n_cmp = (t - CMP_LEN) // CMP_STRIDE + 1
    idx = np.arange(n_cmp)[:, None] * CMP_STRIDE + np.arange(CMP_LEN)[None, :]
    blk = raw[:, idx] + pe[None, None, :, None, :]
    hid = jax.nn.silu(jnp.einsum('bclgd,ldh->bcgh', blk, w1))
    return jnp.einsum('bcgh,he->bgce', hid, w2)


def nsa(q, kc, vc, ks, vs, kw, vw, gates, pe, wk1, wk2, wv1, wv2, rel_bias):
    b, t = q.shape[0], q.shape[1]
    scale = HEAD_DIM ** -0.5
    pos = jnp.arange(t)
    qg = q.reshape(b, t, N_KV, HPG, HEAD_DIM).transpose(0, 2, 3, 1, 4)

    kcmp = compress(kc, pe, wk1, wk2)
    vcmp = compress(vc, pe, wv1, wv2)
    n_cmp = kcmp.shape[2]
    blk_end = jnp.arange(n_cmp) * CMP_STRIDE + CMP_LEN - 1
    dist_c = pos[:, None] - blk_end[None, :]
    valid_c = dist_c >= 0
    bias_c = rel_bias[:, rel_bucket(dist_c)].reshape(N_KV, HPG, t, n_cmp)
    s_c = jnp.einsum('bghtd,bgcd->bghtc', qg, kcmp).astype(jnp.float32) * scale + bias_c
    p_c = masked_softmax(s_c, valid_c)
    o_c = jnp.einsum('bghtc,bgcd->bghtd', p_c.astype(vcmp.dtype), vcmp)

    n_sel = t // SEL_LEN
    ci = np.arange(n_cmp)[:, None] * CMP_STRIDE
    sj = np.arange(n_sel)[None, :] * SEL_LEN
    overlap = ((ci < sj + SEL_LEN) & (ci + CMP_LEN > sj)).astype(np.float32)
    imp = jnp.einsum('bghtc,cj->bgtj', p_c, jnp.asarray(overlap))
    j = jnp.arange(n_sel)[None, :]
    cur = (pos // SEL_LEN)[:, None]
    forced = (j == 0) | (j == cur) | (j == cur - 1)
    imp = jnp.where(forced, FORCE, imp)
    imp = jnp.where(j * SEL_LEN <= pos[:, None], imp, NEG)
    k_top = min(SEL_TOPK, n_sel)
    _, sel_idx = lax.top_k(imp, k_top)

    ks_blk = ks.transpose(0, 2, 1, 3).reshape(b, N_KV, n_sel, SEL_LEN, HEAD_DIM)
    vs_blk = vs.transpose(0, 2, 1, 3).reshape(b, N_KV, n_sel, SEL_LEN, HEAD_DIM)
    kw_pad = jnp.pad(kw.transpose(0, 2, 1, 3), ((0, 0), (0, 0), (WINDOW, 0), (0, 0)))
    vw_pad = jnp.pad(vw.transpose(0, 2, 1, 3), ((0, 0), (0, 0), (WINDOW, 0), (0, 0)))

    rel_bias_g = rel_bias.reshape(N_KV, HPG, N_BUCKETS)
    bi = jnp.arange(b)[:, None, None, None]
    gi = jnp.arange(N_KV)[None, :, None, None]
    gi5 = jnp.arange(N_KV)[None, :, None, None, None]
    hi5 = jnp.arange(HPG)[None, None, :, None, None]
    kv_len = Q_BLOCK + WINDOW
    dist_w = WINDOW + jnp.arange(Q_BLOCK)[:, None] - jnp.arange(kv_len)[None, :]
    band = (dist_w >= 0) & (dist_w < WINDOW)
    bias_w = rel_bias[:, rel_bucket(dist_w)].reshape(N_KV, HPG, Q_BLOCK, kv_len)

    def block_fn(c):
        start = c * Q_BLOCK
        qb = lax.dynamic_slice_in_dim(qg, start, Q_BLOCK, axis=3)
        tq = start + jnp.arange(Q_BLOCK)
        idx = lax.dynamic_slice_in_dim(sel_idx, start, Q_BLOCK, axis=2)
        kb = ks_blk[bi, gi, idx].reshape(b, N_KV, Q_BLOCK, k_top * SEL_LEN, HEAD_DIM)
        vb = vs_blk[bi, gi, idx].reshape(b, N_KV, Q_BLOCK, k_top * SEL_LEN, HEAD_DIM)
        spos = (idx[..., None] * SEL_LEN + jnp.arange(SEL_LEN)).reshape(b, N_KV, Q_BLOCK, k_top * SEL_LEN)
        dist_s = tq[None, None, :, None] - spos
        valid_s = (dist_s >= 0)[:, :, None]
        bias_s = rel_bias_g[gi5, hi5, rel_bucket(dist_s)[:, :, None]]
        s_s = jnp.einsum('bghqd,bgqsd->bghqs', qb, kb).astype(jnp.float32) * scale + bias_s
        p_s = masked_softmax(s_s, valid_s)
        o_s = jnp.einsum('bghqs,bgqsd->bghqd', p_s.astype(vb.dtype), vb)
        kwb = lax.dynamic_slice_in_dim(kw_pad, start, kv_len, axis=2)
        vwb = lax.dynamic_slice_in_dim(vw_pad, start, kv_len, axis=2)
        kpos = start - WINDOW + jnp.arange(kv_len)
        valid_w = band & (kpos >= 0)[None, :]
        s_w = jnp.einsum('bghqd,bgkd->bghqk', qb, kwb).astype(jnp.float32) * scale + bias_w
        p_w = masked_softmax(s_w, valid_w)
        o_w = jnp.einsum('bghqk,bgkd->bghqd', p_w.astype(vwb.dtype), vwb)
        return o_s, o_w

    n_blocks = t // Q_BLOCK
    o_s, o_w = lax.map(block_fn, jnp.arange(n_blocks))

    def unblock(o):
        return o.transpose(1, 2, 3, 0, 4, 5).reshape(b, N_KV, HPG, t, HEAD_DIM)

    def to_bthd(o):
        return o.transpose(0, 3, 1, 2, 4).reshape(b, t, N_HEADS, HEAD_DIM)

    o_c, o_s, o_w = to_bthd(o_c), to_bthd(unblock(o_s)), to_bthd(unblock(o_w))
    g = gates.astype(o_c.dtype)
    o = g[..., 0:1] * o_c + g[..., 1:2] * o_s + g[..., 2:3] * o_w
    return o.reshape(b, t, ATTN_WIDTH)


def short_conv(h, bg, cg, w):
    u = cg * h
    t = u.shape[1]
    up = jnp.pad(u, ((0, 0), (CONV_K - 1, 0), (0, 0)))
    y = w[0] * up[:, 0:t]
    for k in range(1, CONV_K):
        y = y + w[k] * up[:, k:k + t]
    return bg * y


def setup_inputs(seed: int = 0) -> dict:
    key = jax.random.key(seed)
    ks = jax.random.split(key, 20)
    L = DEPTH
    nrm = lambda k, shape, s: jax.random.normal(k, shape, jnp.float32) * s
    gain = lambda k: 1.0 + 0.02 * jax.random.normal(k, (L, D_MODEL), jnp.float32)
    return {
        "x": nrm(ks[0], (BATCH, SEQ, D_MODEL), 1.0),
        "w_in": nrm(ks[1], (L, D_MODEL, IN_WIDTH), D_MODEL ** -0.5),
        "pe_cmp": nrm(ks[2], (L, CMP_LEN, HEAD_DIM), 0.02),
        "w_cmp_k1": nrm(ks[3], (L, CMP_LEN, HEAD_DIM, CMP_HIDDEN), (CMP_LEN * HEAD_DIM) ** -0.5),
        "w_cmp_k2": nrm(ks[4], (L, CMP_HIDDEN, HEAD_DIM), CMP_HIDDEN ** -0.5),
        "w_cmp_v1": nrm(ks[5], (L, CMP_LEN, HEAD_DIM, CMP_HIDDEN), (CMP_LEN * HEAD_DIM) ** -0.5),
        "w_cmp_v2": nrm(ks[6], (L, CMP_HIDDEN, HEAD_DIM), CMP_HIDDEN ** -0.5),
        "conv_w": nrm(ks[7], (L, CONV_K, CONV_WIDTH), CONV_K ** -0.5),
        "rel_bias": nrm(ks[8], (N_HEADS, N_BUCKETS), 0.5),
        "w_o": nrm(ks[9], (L, MIX_WIDTH, D_MODEL), MIX_WIDTH ** -0.5),
        "w_up": nrm(ks[10], (L, D_MODEL, D_FF), D_MODEL ** -0.5),
        "w_down": nrm(ks[11], (L, D_FF, D_MODEL), D_FF ** -0.5),
        "g_pre_mix": gain(ks[12]),
        "g_post_mix": gain(ks[13]),
        "g_pre_ffn": gain(ks[14]),
        "g_post_ffn": gain(ks[15]),
    }


def reference(x, w_in, pe_cmp, w_cmp_k1, w_cmp_k2, w_cmp_v1, w_cmp_v2, conv_w, rel_bias,
              w_o, w_up, w_down, g_pre_mix, g_post_mix, g_pre_ffn, g_post_ffn):
    b, t, _ = x.shape
    offsets = np.cumsum(SPLIT_SIZES)[:-1].tolist()
    for l in range(DEPTH):
        h = rmsnorm(x, g_pre_mix[l])
        proj = h @ w_in[l]
        q, kc, vc, ksl, vsl, kwn, vwn, gl, ch, cb, cc = jnp.split(proj, offsets, axis=-1)
        kv = lambda a: a.reshape(b, t, N_KV, HEAD_DIM)
        gates = jax.nn.sigmoid(gl.astype(jnp.float32)).reshape(b, t, N_HEADS, N_BRANCH)
        o_attn = nsa(q.reshape(b, t, N_HEADS, HEAD_DIM), kv(kc), kv(vc), kv(ksl), kv(vsl),
                     kv(kwn), kv(vwn), gates, pe_cmp[l], w_cmp_k1[l], w_cmp_k2[l],
                     w_cmp_v1[l], w_cmp_v2[l], rel_bias)
        o_conv = short_conv(ch, cb, cc, conv_w[l])
        mix = jnp.concatenate([o_attn, o_conv], axis=-1) @ w_o[l]
        x = x + rmsnorm(mix, g_post_mix[l])
        h = rmsnorm(x, g_pre_ffn[l])
        f = jnp.square(jax.nn.relu(h @ w_up[l])) @ w_down[l]
        x = x + rmsnorm(f, g_post_ffn[l])
    return x
```

```python
import functools
import math

import numpy as np
import jax
import jax.numpy as jnp
from jax import lax
from jax.experimental import pallas as pl
from jax.experimental.pallas import tpu as pltpu

F32 = jnp.float32
BF16 = jnp.bfloat16

D_MODEL = 2048
N_HEADS = 8
N_KV = 2
HPG = N_HEADS // N_KV
HEAD_DIM = 128
ATTN_WIDTH = N_HEADS * HEAD_DIM
KV_WIDTH = N_KV * HEAD_DIM
CONV_WIDTH = D_MODEL - ATTN_WIDTH
CONV_K = 3
N_BRANCH = 3
CMP_LEN = 32
CMP_STRIDE = 16
CMP_HIDDEN = 256
SEL_LEN = 64
SEL_TOPK = 16
WINDOW = 512
N_BUCKETS = 32
MAX_DIST = 128
EPS = 1e-6
NEG = -1e30
HALF_NEG = -5e29
FORCE = 1e9

QKV_WIDTH = ATTN_WIDTH + 6 * KV_WIDTH
GATE_OFF = QKV_WIDTH
CONV_OFF = QKV_WIDTH + N_HEADS * N_BRANCH
LANE = 128
VMEM_LIMIT = 56 * 1024 * 1024

_DN_T = (((1,), (1,)), ((), ()))


def _rms(x, g):
    ms = jnp.mean(x * x, axis=-1, keepdims=True)
    return x * lax.rsqrt(ms + EPS) * g


def _params(n_axes):
    return pltpu.CompilerParams(dimension_semantics=("arbitrary",) * n_axes, vmem_limit_bytes=VMEM_LIMIT)


def _qkv_kernel(x_ref, g_ref, w_ref, wg_ref, o_ref, gate_ref, h_ref, *, n_q_blocks, scale):
    j = pl.program_id(1)

    @pl.when(j == 0)
    def _():
        hb = _rms(x_ref[...], g_ref[...]).astype(BF16)
        h_ref[...] = hb
        gate_ref[...] = jax.nn.sigmoid(jnp.dot(hb, wg_ref[...], preferred_element_type=F32))

    acc = jnp.dot(h_ref[...], w_ref[...], preferred_element_type=F32)
    acc = acc * jnp.where(j < n_q_blocks, scale, 1.0).astype(F32)
    o_ref[...] = acc.astype(BF16)


def _qkv_proj(x2, g, w_qkv, w_gate, *, tm=512, tn=512):
    m = x2.shape[0]
    n = w_qkv.shape[1]
    gw = w_gate.shape[1]
    kern = functools.partial(_qkv_kernel, n_q_blocks=ATTN_WIDTH // tn, scale=HEAD_DIM ** -0.5)
    return pl.pallas_call(
        kern,
        grid=(m // tm, n // tn),
        in_specs=[
            pl.BlockSpec((tm, D_MODEL), lambda i, j: (i, 0)),
            pl.BlockSpec((1, D_MODEL), lambda i, j: (0, 0)),
            pl.BlockSpec((D_MODEL, tn), lambda i, j: (0, j)),
            pl.BlockSpec((D_MODEL, gw), lambda i, j: (0, 0)),
        ],
        out_specs=[
            pl.BlockSpec((tm, tn), lambda i, j: (i, j)),
            pl.BlockSpec((tm, gw), lambda i, j: (i, 0)),
        ],
        out_shape=[
            jax.ShapeDtypeStruct((m, n), BF16),
            jax.ShapeDtypeStruct((m, gw), F32),
        ],
        scratch_shapes=[pltpu.VMEM((tm, D_MODEL), BF16)],
        compiler_params=_params(2),
        name="qkv_proj",
    )(x2, g, w_qkv, w_gate)


def _conv_kernel(x_ref, g_ref, wh_ref, wb_ref, wc_ref, cw_ref, o_ref, h_ref, carry_ref, *, tiles_per_seq):
    i = pl.program_id(0)
    j = pl.program_id(1)

    @pl.when(j == 0)
    def _():
        h_ref[...] = _rms(x_ref[...], g_ref[...]).astype(BF16)

    hb = h_ref[...]
    ch = jnp.dot(hb, wh_ref[...], preferred_element_type=F32)
    cb = jnp.dot(hb, wb_ref[...], preferred_element_type=F32)
    cc = jnp.dot(hb, wc_ref[...], preferred_element_type=F32)
    u = cc * ch
    tm = u.shape[0]
    prev = carry_ref[j]
    prev = jnp.where(i % tiles_per_seq == 0, 0.0, prev)
    carry_ref[j] = u[tm - 8:tm, :]
    row = lax.broadcasted_iota(jnp.int32, u.shape, 0)
    u1 = jnp.where(row == 0, prev[7:8, :], pltpu.roll(u, 1, axis=0))
    u2 = jnp.where(row == 0, prev[6:7, :], jnp.where(row == 1, prev[7:8, :], pltpu.roll(u, 2, axis=0)))
    w = cw_ref[...]
    y = w[0:1, :] * u2
    y = y + w[1:2, :] * u1
    y = y + w[2:3, :] * u
    o_ref[...] = (cb * y).astype(BF16)


def _conv_proj(x2, g, w_h, w_b, w_c, conv_w, seq_len, *, tm=512, tc=256):
    m = x2.shape[0]
    c = w_h.shape[1]
    kern = functools.partial(_conv_kernel, tiles_per_seq=seq_len // tm)
    wspec = pl.BlockSpec((D_MODEL, tc), lambda i, j: (0, j))
    return pl.pallas_call(
        kern,
        grid=(m // tm, c // tc),
        in_specs=[
            pl.BlockSpec((tm, D_MODEL), lambda i, j: (i, 0)),
            pl.BlockSpec((1, D_MODEL), lambda i, j: (0, 0)),
            wspec, wspec, wspec,
            pl.BlockSpec((CONV_K, tc), lambda i, j: (0, j)),
        ],
        out_specs=pl.BlockSpec((tm, tc), lambda i, j: (i, j)),
        out_shape=jax.ShapeDtypeStruct((m, c), BF16),
        scratch_shapes=[pltpu.VMEM((tm, D_MODEL), BF16), pltpu.VMEM((c // tc, 8, tc), F32)],
        compiler_params=_params(2),
        name="conv_proj",
    )(x2, g, w_h, w_b, w_c, conv_w)


def _compress_kernel(x_ref, pe_ref, w1_ref, w2_ref, o_ref):
    half = CMP_STRIDE * HEAD_DIM
    x = x_ref[...].astype(F32)
    pe = pe_ref[...]
    xa = (x + pe[0:1, :]).astype(BF16)
    xb = (x + pe[1:2, :]).astype(BF16)
    a = jnp.dot(xa, w1_ref[0:half, :], preferred_element_type=F32)
    b = jnp.dot(xb, w1_ref[half:2 * half, :], preferred_element_type=F32)
    n = a.shape[0]
    pre = a + pltpu.roll(b, n - 1, axis=0)
    hid = pre * jax.nn.sigmoid(pre)
    out = jnp.dot(hid.astype(BF16), w2_ref[...], preferred_element_type=F32)
    row = lax.broadcasted_iota(jnp.int32, out.shape, 0)
    o_ref[...] = jnp.where(row < n - 1, out, 0.0).astype(BF16)


def _compress(xc, pe2, w1, w2):
    b, nkv, n_chunks, width = xc.shape
    return pl.pallas_call(
        _compress_kernel,
        grid=(b, nkv),
        in_specs=[
            pl.BlockSpec((None, None, n_chunks, width), lambda i, j: (i, j, 0, 0)),
            pl.BlockSpec((2, width), lambda i, j: (0, 0)),
            pl.BlockSpec((None, 2 * width, CMP_HIDDEN), lambda i, j: (j // N_KV, 0, 0)),
            pl.BlockSpec((None, CMP_HIDDEN, HEAD_DIM), lambda i, j: (j // N_KV, 0, 0)),
        ],
        out_specs=pl.BlockSpec((None, None, n_chunks, HEAD_DIM), lambda i, j: (i, j, 0, 0)),
        out_shape=jax.ShapeDtypeStruct((b, nkv, n_chunks, HEAD_DIM), BF16),
        compiler_params=_params(2),
        name="compress",
    )(xc, pe2, w1, w2)


TAB_DIAG, TAB_SUB, TAB_FAR, TAB_CONST = 0, 1, 2, 3


def _nsa_kernel(q_ref, kc_ref, vc_ref, ks_ref, vs_ref, kw_ref, vw_ref, gate_ref, biasc_ref, tab_ref,
                ovl_ref, et_ref, o_ref, kaug_ref, m_ref, l_ref, acc_ref, *, tq, n_sel):
    i = pl.program_id(2)
    tk = tq
    mrows = HPG * tq
    nw = WINDOW // tk
    n_cmp = kc_ref.shape[0]

    @pl.when(i == 0)
    def _():
        kaug_ref[:, 0:HEAD_DIM] = ks_ref[...]
        kaug_ref[:, HEAD_DIM:2 * HEAD_DIM] = et_ref[...]

    q = q_ref[...]
    qs = jnp.concatenate([q[:, h * HEAD_DIM:(h + 1) * HEAD_DIM] for h in range(HPG)], axis=0)

    sc = lax.dot_general(qs, kc_ref[...], _DN_T, preferred_element_type=F32)
    sc = sc + biasc_ref[...].reshape(mrows, n_cmp)
    mc = jnp.max(sc, axis=1, keepdims=True)
    pc = jnp.where(sc > HALF_NEG, jnp.exp(sc - mc), 0.0)
    lc = jnp.sum(pc, axis=1, keepdims=True)
    pc = pc * jnp.where(lc > 0.0, 1.0 / lc, 0.0)
    o_c = jnp.dot(pc.astype(BF16), vc_ref[...], preferred_element_type=F32)

    ps = pc[0:tq] + pc[tq:2 * tq] + pc[2 * tq:3 * tq] + pc[3 * tq:4 * tq]
    hi = ps.astype(BF16)
    r1 = ps - hi.astype(F32)
    mid = r1.astype(BF16)
    lo = (r1 - mid.astype(F32)).astype(BF16)
    ovl = ovl_ref[...]
    imp = (lax.dot_general(ovl, hi, _DN_T, preferred_element_type=F32)
           + lax.dot_general(ovl, mid, _DN_T, preferred_element_type=F32)
           + lax.dot_general(ovl, lo, _DN_T, preferred_element_type=F32))
    jj = lax.broadcasted_iota(jnp.int32, (n_sel, tq), 0)
    tt = i * tq + lax.broadcasted_iota(jnp.int32, (n_sel, tq), 1)
    cur = tt >> int(math.log2(SEL_LEN))
    forced = (jj == 0) | (jj == cur) | (jj == cur - 1)
    imp = jnp.where(forced, FORCE, imp)
    imp = jnp.where(jj * SEL_LEN <= tt, imp, NEG)
    rank = jnp.zeros((n_sel, tq), jnp.int32)
    for b in range(n_sel):
        row = imp[b:b + 1, :]
        tie = jnp.where(row == imp, jnp.where(jj > b, 1, 0), 0)
        rank = rank + jnp.where(row > imp, 1, tie)
    selb_t = jnp.where(rank < SEL_TOPK, 0.0, NEG)
    selb = jnp.concatenate([selb_t, jnp.zeros((LANE - n_sel, tq), F32)], axis=0).T
    selb = selb.astype(BF16)
    qa = jnp.concatenate([qs, jnp.concatenate([selb] * HPG, axis=0)], axis=1)

    def init():
        m_ref[...] = jnp.full((mrows, LANE), NEG, F32)
        l_ref[...] = jnp.zeros((mrows, LANE), F32)
        acc_ref[...] = jnp.zeros((mrows, HEAD_DIM), F32)

    def step(qmat, k_tile, v_tile, bias):
        s = lax.dot_general(qmat, k_tile, _DN_T, preferred_element_type=F32) + bias
        m_prev = m_ref[...]
        m_next = jnp.maximum(m_prev, jnp.max(s, axis=1, keepdims=True))
        alpha = jnp.exp(m_prev - m_next)
        p = jnp.exp(s - jnp.concatenate([m_next] * (tk // LANE), axis=1))
        l_ref[...] = alpha * l_ref[...] + jnp.sum(p, axis=1, keepdims=True)
        acc_ref[...] = alpha * acc_ref[...] + jnp.dot(p.astype(BF16), v_tile, preferred_element_type=F32)
        m_ref[...] = m_next

    def rows(kt):
        return pl.ds(pl.multiple_of(kt * tk, tk), tk)

    init()

    def far_body(kt, carry):
        step(qa, kaug_ref[rows(kt), :], vs_ref[rows(kt), :], tab_ref[TAB_CONST])
        return carry

    lax.fori_loop(0, jnp.maximum(i - 1, 0), far_body, 0)

    @pl.when(i >= 1)
    def _():
        step(qa, kaug_ref[rows(i - 1), :], vs_ref[rows(i - 1), :], tab_ref[TAB_SUB])

    step(qa, kaug_ref[rows(i), :], vs_ref[rows(i), :], tab_ref[TAB_DIAG])
    o_s = acc_ref[...] / l_ref[...]

    init()
    for d in range(nw, -1, -1):
        kind = TAB_DIAG if d == 0 else TAB_SUB if d == 1 else TAB_FAR if d == nw else TAB_CONST

        @pl.when(i >= d)
        def _(d=d, kind=kind):
            step(qs, kw_ref[rows(i - d), :], vw_ref[rows(i - d), :], tab_ref[kind])

    o_w = acc_ref[...] / l_ref[...]

    gt = gate_ref[...]
    for h in range(HPG):
        sl = slice(h * tq, (h + 1) * tq)
        o = gt[:, 3 * h:3 * h + 1] * o_c[sl]
        o = o + gt[:, 3 * h + 1:3 * h + 2] * o_s[sl]
        o = o + gt[:, 3 * h + 2:3 * h + 3] * o_w[sl]
        o_ref[:, h * HEAD_DIM:(h + 1) * HEAD_DIM] = o.astype(BF16)


def _nsa(qkv, cmp_kv, gates, bias_c, tabs, ovl_t, e_t, batch, seq_len, *, tq=128):
    n_t = seq_len // tq
    n_sel = seq_len // SEL_LEN
    n_chunks = cmp_kv.shape[2]
    mrows = HPG * tq
    qw = HPG * HEAD_DIM
    col0 = ATTN_WIDTH // HEAD_DIM

    def kv_spec(which):
        return pl.BlockSpec((seq_len, HEAD_DIM), lambda b, g, i, w=which: (b, col0 + N_KV * w + g))

    kern = functools.partial(_nsa_kernel, tq=tq, n_sel=n_sel)
    return pl.pallas_call(
        kern,
        grid=(batch, N_KV, n_t),
        in_specs=[
            pl.BlockSpec((tq, qw), lambda b, g, i: (b * n_t + i, g)),
            pl.BlockSpec((None, None, n_chunks, HEAD_DIM), lambda b, g, i: (b, g, 0, 0)),
            pl.BlockSpec((None, None, n_chunks, HEAD_DIM), lambda b, g, i: (b, N_KV + g, 0, 0)),
            kv_spec(2), kv_spec(3), kv_spec(4), kv_spec(5),
            pl.BlockSpec((tq, LANE), lambda b, g, i: (b * n_t + i, g)),
            pl.BlockSpec((HPG, tq, n_chunks), lambda b, g, i: (g, i, 0)),
            pl.BlockSpec((None, 4, mrows, tq), lambda b, g, i: (g, 0, 0, 0)),
            pl.BlockSpec((n_sel, n_chunks), lambda b, g, i: (0, 0)),
            pl.BlockSpec((seq_len, LANE), lambda b, g, i: (0, 0)),
        ],
        out_specs=pl.BlockSpec((tq, qw), lambda b, g, i: (b * n_t + i, g)),
        out_shape=jax.ShapeDtypeStruct((batch * seq_len, ATTN_WIDTH), BF16),
        scratch_shapes=[
            pltpu.VMEM((seq_len, 2 * HEAD_DIM), BF16),
            pltpu.VMEM((mrows, LANE), F32),
            pltpu.VMEM((mrows, LANE), F32),
            pltpu.VMEM((mrows, HEAD_DIM), F32),
        ],
        compiler_params=_params(3),
        name="nsa_attention",
    )(qkv, cmp_kv, cmp_kv, qkv, qkv, qkv, qkv, gates, bias_c, tabs, ovl_t, e_t)


def _oproj_kernel(oa_ref, ov_ref, wo_ref, x_ref, g1_ref, g2_ref, x1_ref, h2_ref):
    ka = oa_ref.shape[1]
    mix = jnp.dot(oa_ref[...], wo_ref[0:ka, :], preferred_element_type=F32)
    mix = mix + jnp.dot(ov_ref[...], wo_ref[ka:, :], preferred_element_type=F32)
    x1 = x_ref[...] + _rms(mix, g1_ref[...])
    x1_ref[...] = x1
    h2_ref[...] = _rms(x1, g2_ref[...]).astype(BF16)


def _oproj(o_attn, o_conv, w_o, x2, g_post, g_pre, *, tm=512):
    m = x2.shape[0]
    ka, kv = o_attn.shape[1], o_conv.shape[1]
    row = lambda i: (i, 0)
    fixed = lambda i: (0, 0)
    return pl.pallas_call(
        _oproj_kernel,
        grid=(m // tm,),
        in_specs=[
            pl.BlockSpec((tm, ka), row),
            pl.BlockSpec((tm, kv), row),
            pl.BlockSpec((ka + kv, D_MODEL), fixed),
            pl.BlockSpec((tm, D_MODEL), row),
            pl.BlockSpec((1, D_MODEL), fixed),
            pl.BlockSpec((1, D_MODEL), fixed),
        ],
        out_specs=[pl.BlockSpec((tm, D_MODEL), row), pl.BlockSpec((tm, D_MODEL), row)],
        out_shape=[jax.ShapeDtypeStruct((m, D_MODEL), F32), jax.ShapeDtypeStruct((m, D_MODEL), BF16)],
        compiler_params=_params(1),
        name="out_proj",
    )(o_attn, o_conv, w_o, x2, g_post, g_pre)


def _ffn_kernel(h_ref, wu_ref, wd_ref, x1_ref, g_ref, o_ref, acc_ref):
    j = pl.program_id(1)
    a = jnp.dot(h_ref[...], wu_ref[...], preferred_element_type=F32)
    a = jnp.square(jnp.maximum(a, 0.0)).astype(BF16)
    d = jnp.dot(a, wd_ref[...], preferred_element_type=F32)

    @pl.when(j == 0)
    def _():
        acc_ref[...] = d

    @pl.when(j > 0)
    def _():
        acc_ref[...] += d

    @pl.when(j == pl.num_programs(1) - 1)
    def _():
        o_ref[...] = x1_ref[...] + _rms(acc_ref[...], g_ref[...])


def _ffn(h2, w_up, w_down, x1, g_post, *, tm=512, tf=512):
    m = h2.shape[0]
    d_ff = w_up.shape[1]
    return pl.pallas_call(
        _ffn_kernel,
        grid=(m // tm, d_ff // tf),
        in_specs=[
            pl.BlockSpec((tm, D_MODEL), lambda i, j: (i, 0)),
            pl.BlockSpec((D_MODEL, tf), lambda i, j: (0, j)),
            pl.BlockSpec((tf, D_MODEL), lambda i, j: (j, 0)),
            pl.BlockSpec((tm, D_MODEL), lambda i, j: (i, 0)),
            pl.BlockSpec((1, D_MODEL), lambda i, j: (0, 0)),
        ],
        out_specs=pl.BlockSpec((tm, D_MODEL), lambda i, j: (i, 0)),
        out_shape=jax.ShapeDtypeStruct((m, D_MODEL), F32),
        scratch_shapes=[pltpu.VMEM((tm, D_MODEL), F32)],
        compiler_params=_params(2),
        name="ffn",
    )(h2, w_up, w_down, x1, g_post)


def _bucket_np(dist):
    n = np.maximum(dist, 0)
    max_exact = N_BUCKETS // 2
    nf = np.maximum(n, 1).astype(np.float32)
    large = max_exact + (np.log(nf / np.float32(max_exact)) / np.float32(math.log(MAX_DIST / max_exact))
                         * np.float32(N_BUCKETS - max_exact)).astype(np.int32)
    large = np.minimum(large, N_BUCKETS - 1)
    return np.where(n < max_exact, n, large)


def _bucket_starts():
    b = _bucket_np(np.arange(4 * MAX_DIST))
    return [int(np.argmax(b == k)) for k in range(N_BUCKETS)]


def _bias_by_distance(rel_bias, dist):
    starts = _bucket_starts()
    d = jnp.asarray(dist, jnp.int32)[None]
    col = lambda k: rel_bias[:, k].reshape((N_HEADS,) + (1,) * dist.ndim)
    out = jnp.broadcast_to(col(0), (N_HEADS,) + dist.shape)
    for k in range(1, N_BUCKETS):
        out = jnp.where(d >= starts[k], col(k), out)
    return out


def _attention_tables(rel_bias, seq_len, tq):
    r = np.arange(tq)[:, None]
    c = np.arange(tq)[None, :]
    diag = jnp.where(jnp.asarray(r >= c)[None], _bias_by_distance(rel_bias, np.maximum(r - c, 0)), NEG)
    sub = _bias_by_distance(rel_bias, tq + r - c)
    const = jnp.broadcast_to(rel_bias[:, N_BUCKETS - 1][:, None, None], (N_HEADS, tq, tq))
    far = jnp.where(jnp.asarray(r < c)[None], const, NEG)
    tabs = jnp.stack([diag, sub, far, const], axis=1)
    tabs = tabs.reshape(N_KV, HPG, 4, tq, tq).transpose(0, 2, 1, 3, 4).reshape(N_KV, 4, HPG * tq, tq)

    n_chunks = seq_len // CMP_STRIDE
    n_cmp = (seq_len - CMP_LEN) // CMP_STRIDE + 1
    t = lax.broadcasted_iota(jnp.int32, (seq_len, n_chunks), 0)
    cc = lax.broadcasted_iota(jnp.int32, (seq_len, n_chunks), 1)
    dist_c = t - (cc * CMP_STRIDE + CMP_LEN - 1)
    starts = _bucket_starts()
    bias_c = jnp.broadcast_to(rel_bias[:, 0][:, None, None], (N_HEADS, seq_len, n_chunks))
    for k in range(1, N_BUCKETS):
        bias_c = jnp.where(dist_c[None] >= starts[k], rel_bias[:, k][:, None, None], bias_c)
    bias_c = jnp.where(((dist_c >= 0) & (cc < n_cmp))[None], bias_c, NEG)

    n_sel = seq_len // SEL_LEN
    ci = np.arange(n_chunks)[None, :] * CMP_STRIDE
    sj = np.arange(n_sel)[:, None] * SEL_LEN
    ovl_t = ((ci < sj + SEL_LEN) & (ci + CMP_LEN > sj) & (np.arange(n_chunks)[None, :] < n_cmp))
    e_t = (np.arange(seq_len)[:, None] // SEL_LEN == np.arange(LANE)[None, :])
    return tabs.astype(F32), bias_c.astype(F32), jnp.asarray(ovl_t, BF16), jnp.asarray(e_t, BF16)


def kernel(x, w_in, pe_cmp, w_cmp_k1, w_cmp_k2, w_cmp_v1, w_cmp_v2, conv_w, rel_bias, w_o, w_up, w_down,
           g_pre_mix, g_post_mix, g_pre_ffn, g_post_ffn):
    batch, seq_len, _ = x.shape
    depth = w_in.shape[0]
    tq = 128
    n_chunks = seq_len // CMP_STRIDE
    half = CMP_STRIDE * HEAD_DIM
    tabs, bias_c, ovl_t, e_t = _attention_tables(rel_bias, seq_len, tq)
    x2 = x.reshape(batch * seq_len, D_MODEL)
    for l in range(depth):
        wl = w_in[l]
        w_qkv = wl[:, :QKV_WIDTH].astype(BF16)
        wg = wl[:, GATE_OFF:CONV_OFF].reshape(D_MODEL, N_KV, HPG * N_BRANCH)
        wg = jnp.pad(wg, ((0, 0), (0, 0), (0, LANE - HPG * N_BRANCH))).reshape(D_MODEL, N_KV * LANE).astype(BF16)
        w_h, w_b, w_c = (wl[:, CONV_OFF + k * CONV_WIDTH:CONV_OFF + (k + 1) * CONV_WIDTH].astype(BF16)
                         for k in range(3))
        g1 = g_pre_mix[l].reshape(1, D_MODEL)

        qkv, gates = _qkv_proj(x2, g1, w_qkv, wg)
        o_conv = _conv_proj(x2, g1, w_h, w_b, w_c, conv_w[l], seq_len)

        xc = qkv[:, ATTN_WIDTH:ATTN_WIDTH + 2 * KV_WIDTH]
        xc = xc.reshape(batch, n_chunks, CMP_STRIDE, 2 * N_KV, HEAD_DIM).transpose(0, 3, 1, 2, 4)
        xc = xc.reshape(batch, 2 * N_KV, n_chunks, half)
        w1 = jnp.stack([w_cmp_k1[l], w_cmp_v1[l]]).reshape(2, 2 * half, CMP_HIDDEN).astype(BF16)
        w2 = jnp.stack([w_cmp_k2[l], w_cmp_v2[l]]).astype(BF16)
        cmp_kv = _compress(xc, pe_cmp[l].reshape(2, half), w1, w2)

        o_attn = _nsa(qkv, cmp_kv, gates, bias_c, tabs, ovl_t, e_t, batch, seq_len, tq=tq)

        x1, h2 = _oproj(o_attn, o_conv, w_o[l].astype(BF16), x2,
                        g_post_mix[l].reshape(1, D_MODEL), g_pre_ffn[l].reshape(1, D_MODEL))
        x2 = _ffn(h2, w_up[l].astype(BF16), w_down[l].astype(BF16), x1, g_post_ffn[l].reshape(1, D_MODEL))
    return x2.reshape(batch, seq_len, D_MODEL)
```

```python
import functools
import math

import numpy as np
import jax
import jax.numpy as jnp
from jax import lax
from jax.experimental import pallas as pl
from jax.experimental.pallas import tpu as pltpu

F32 = jnp.float32
BF16 = jnp.bfloat16

D_MODEL = 2048
N_HEADS = 8
N_KV = 2
HPG = N_HEADS // N_KV
HEAD_DIM = 128
ATTN_WIDTH = N_HEADS * HEAD_DIM
KV_WIDTH = N_KV * HEAD_DIM
CONV_WIDTH = D_MODEL - ATTN_WIDTH
CONV_K = 3
N_BRANCH = 3
CMP_LEN = 32
CMP_STRIDE = 16
CMP_HIDDEN = 256
SEL_LEN = 64
SEL_TOPK = 16
WINDOW = 512
N_BUCKETS = 32
MAX_DIST = 128
EPS = 1e-6
NEG = -1e30
HALF_NEG = -5e29
FORCE = 1e9

QKV_WIDTH = ATTN_WIDTH + 6 * KV_WIDTH
GATE_OFF = QKV_WIDTH
CONV_OFF = QKV_WIDTH + N_HEADS * N_BRANCH
LANE = 128
VMEM_LIMIT = 56 * 1024 * 1024

_DN_T = (((1,), (1,)), ((), ()))


def _rms(x, g):
    ms = jnp.mean(x * x, axis=-1, keepdims=True)
    return x * lax.rsqrt(ms + EPS) * g


def _params(n_axes):
    return pltpu.CompilerParams(dimension_semantics=("arbitrary",) * n_axes, vmem_limit_bytes=VMEM_LIMIT)


def _qkv_kernel(x_ref, g_ref, w_ref, wg_ref, o_ref, gate_ref, h_ref, *, n_q_blocks, scale):
    j = pl.program_id(1)

    @pl.when(j == 0)
    def _():
        hb = _rms(x_ref[...], g_ref[...]).astype(BF16)
        h_ref[...] = hb
        gate_ref[...] = jax.nn.sigmoid(jnp.dot(hb, wg_ref[...], preferred_element_type=F32))

    acc = jnp.dot(h_ref[...], w_ref[...], preferred_element_type=F32)
    acc = acc * jnp.where(j < n_q_blocks, scale, 1.0).astype(F32)
    o_ref[...] = acc.astype(BF16)


def _qkv_proj(x2, g, w_qkv, w_gate, *, tm=512, tn=512):
    m = x2.shape[0]
    n = w_qkv.shape[1]
    gw = w_gate.shape[1]
    kern = functools.partial(_qkv_kernel, n_q_blocks=ATTN_WIDTH // tn, scale=HEAD_DIM ** -0.5)
    return pl.pallas_call(
        kern,
        grid=(m // tm, n // tn),
        in_specs=[
            pl.BlockSpec((tm, D_MODEL), lambda i, j: (i, 0)),
            pl.BlockSpec((1, D_MODEL), lambda i, j: (0, 0)),
            pl.BlockSpec((D_MODEL, tn), lambda i, j: (0, j)),
            pl.BlockSpec((D_MODEL, gw), lambda i, j: (0, 0)),
        ],
        out_specs=[
            pl.BlockSpec((tm, tn), lambda i, j: (i, j)),
            pl.BlockSpec((tm, gw), lambda i, j: (i, 0)),
        ],
        out_shape=[
            jax.ShapeDtypeStruct((m, n), BF16),
            jax.ShapeDtypeStruct((m, gw), F32),
        ],
        scratch_shapes=[pltpu.VMEM((tm, D_MODEL), BF16)],
        compiler_params=_params(2),
        name="qkv_proj",
    )(x2, g, w_qkv, w_gate)


def _conv_kernel(x_ref, g_ref, wh_ref, wb_ref, wc_ref, cw_ref, o_ref, h_ref, carry_ref, *, tiles_per_seq):
    i = pl.program_id(0)
    j = pl.program_id(1)

    @pl.when(j == 0)
    def _():
        h_ref[...] = _rms(x_ref[...], g_ref[...]).astype(BF16)

    hb = h_ref[...]
    ch = jnp.dot(hb, wh_ref[...], preferred_element_type=F32)
    cb = jnp.dot(hb, wb_ref[...], preferred_element_type=F32)
    cc = jnp.dot(hb, wc_ref[...], preferred_element_type=F32)
    u = cc * ch
    tm = u.shape[0]
    prev = carry_ref[j]
    prev = jnp.where(i % tiles_per_seq == 0, 0.0, prev)
    carry_ref[j] = u[tm - 8:tm, :]
    row = lax.broadcasted_iota(jnp.int32, u.shape, 0)
    u1 = jnp.where(row == 0, prev[7:8, :], pltpu.roll(u, 1, axis=0))
    u2 = jnp.where(row == 0, prev[6:7, :], jnp.where(row == 1, prev[7:8, :], pltpu.roll(u, 2, axis=0)))
    w = cw_ref[...]
    y = w[0:1, :] * u2
    y = y + w[1:2, :] * u1
    y = y + w[2:3, :] * u
    o_ref[...] = (cb * y).astype(BF16)


def _conv_proj(x2, g, w_h, w_b, w_c, conv_w, seq_len, *, tm=512, tc=256):
    m = x2.shape[0]
    c = w_h.shape[1]
    kern = functools.partial(_conv_kernel, tiles_per_seq=seq_len // tm)
    wspec = pl.BlockSpec((D_MODEL, tc), lambda i, j: (0, j))
    return pl.pallas_call(
        kern,
        grid=(m // tm, c // tc),
        in_specs=[
            pl.BlockSpec((tm, D_MODEL), lambda i, j: (i, 0)),
            pl.BlockSpec((1, D_MODEL), lambda i, j: (0, 0)),
            wspec, wspec, wspec,
            pl.BlockSpec((CONV_K, tc), lambda i, j: (0, j)),
        ],
        out_specs=pl.BlockSpec((tm, tc), lambda i, j: (i, j)),
        out_shape=jax.ShapeDtypeStruct((m, c), BF16),
        scratch_shapes=[pltpu.VMEM((tm, D_MODEL), BF16), pltpu.VMEM((c // tc, 8, tc), F32)],
        compiler_params=_params(2),
        name="conv_proj",
    )(x2, g, w_h, w_b, w_c, conv_w)


def _compress_kernel(x_ref, pe_ref, w1_ref, w2_ref, o_ref):
    half = CMP_STRIDE * HEAD_DIM
    x = x_ref[...].astype(F32)
    pe = pe_ref[...]
    xa = (x + pe[0:1, :]).astype(BF16)
    xb = (x + pe[1:2, :]).astype(BF16)
    a = jnp.dot(xa, w1_ref[0:half, :], preferred_element_type=F32)
    b = jnp.dot(xb, w1_ref[half:2 * half, :], preferred_element_type=F32)
    n = a.shape[0]
    pre = a + pltpu.roll(b, n - 1, axis=0)
    hid = pre * jax.nn.sigmoid(pre)
    out = jnp.dot(hid.astype(BF16), w2_ref[...], preferred_element_type=F32)
    row = lax.broadcasted_iota(jnp.int32, out.shape, 0)
    o_ref[...] = jnp.where(row < n - 1, out, 0.0).astype(BF16)


def _compress(xc, pe2, w1, w2):
    b, nkv, n_chunks, width = xc.shape
    return pl.pallas_call(
        _compress_kernel,
        grid=(b, nkv),
        in_specs=[
            pl.BlockSpec((None, None, n_chunks, width), lambda i, j: (i, j, 0, 0)),
            pl.BlockSpec((2, width), lambda i, j: (0, 0)),
            pl.BlockSpec((None, 2 * width, CMP_HIDDEN), lambda i, j: (j // N_KV, 0, 0)),
            pl.BlockSpec((None, CMP_HIDDEN, HEAD_DIM), lambda i, j: (j // N_KV, 0, 0)),
        ],
        out_specs=pl.BlockSpec((None, None, n_chunks, HEAD_DIM), lambda i, j: (i, j, 0, 0)),
        out_shape=jax.ShapeDtypeStruct((b, nkv, n_chunks, HEAD_DIM), BF16),
        compiler_params=_params(2),
        name="compress",
    )(xc, pe2, w1, w2)


TAB_DIAG, TAB_SUB, TAB_FAR, TAB_CONST = 0, 1, 2, 3


def _nsa_kernel(q_ref, kc_ref, vct_ref, ks_ref, vst_ref, kw_ref, vwt_ref, gate_ref, biasc_ref, tab_ref,
                ovl_ref, et_ref, o_ref, kaug_ref, m_ref, l_ref, acc_ref, *, tq, n_sel):
    i = pl.program_id(2)
    tk = tq
    mcols = HPG * tq
    nw = WINDOW // tk

    @pl.when(i == 0)
    def _():
        kaug_ref[:, 0:HEAD_DIM] = ks_ref[...]
        kaug_ref[:, HEAD_DIM:2 * HEAD_DIM] = et_ref[...]

    q = q_ref[...]
    qs = jnp.concatenate([q[:, h * HEAD_DIM:(h + 1) * HEAD_DIM] for h in range(HPG)], axis=0)

    sc = lax.dot_general(kc_ref[...], qs, _DN_T, preferred_element_type=F32)
    sc = sc + biasc_ref[...]
    mc = jnp.max(sc, axis=0, keepdims=True)
    pc = jnp.where(sc > HALF_NEG, jnp.exp(sc - mc), 0.0)
    lc = jnp.sum(pc, axis=0, keepdims=True)
    pc = pc * jnp.where(lc > 0.0, 1.0 / lc, 0.0)
    o_c = jnp.dot(vct_ref[...], pc.astype(BF16), preferred_element_type=F32)

    ps = pc[:, 0:tq] + pc[:, tq:2 * tq] + pc[:, 2 * tq:3 * tq] + pc[:, 3 * tq:4 * tq]
    hi = ps.astype(BF16)
    r1 = ps - hi.astype(F32)
    mid = r1.astype(BF16)
    lo = (r1 - mid.astype(F32)).astype(BF16)
    ovl = ovl_ref[...]
    imp = (jnp.dot(ovl, hi, preferred_element_type=F32) + jnp.dot(ovl, mid, preferred_element_type=F32)
           + jnp.dot(ovl, lo, preferred_element_type=F32))
    jj = lax.broadcasted_iota(jnp.int32, (n_sel, tq), 0)
    tt = i * tq + lax.broadcasted_iota(jnp.int32, (n_sel, tq), 1)
    cur = tt >> int(math.log2(SEL_LEN))
    forced = (jj == 0) | (jj == cur) | (jj == cur - 1)
    imp = jnp.where(forced, FORCE, imp)
    imp = jnp.where(jj * SEL_LEN <= tt, imp, NEG)
    rank = jnp.zeros((n_sel, tq), jnp.int32)
    for b in range(n_sel):
        row = imp[b:b + 1, :]
        tie = jnp.where(row == imp, jnp.where(jj > b, 1, 0), 0)
        rank = rank + jnp.where(row > imp, 1, tie)
    selb_t = jnp.where(rank < SEL_TOPK, 0.0, NEG)
    selb = jnp.concatenate([selb_t, jnp.zeros((LANE - n_sel, tq), F32)], axis=0).T
    selb = selb.astype(BF16)
    qa = jnp.concatenate([qs, jnp.concatenate([selb] * HPG, axis=0)], axis=1)

    def init():
        m_ref[...] = jnp.full((1, mcols), NEG, F32)
        l_ref[...] = jnp.zeros((1, mcols), F32)
        acc_ref[...] = jnp.zeros((HEAD_DIM, mcols), F32)

    def step(qmat, k_tile, vt_tile, bias):
        s = lax.dot_general(k_tile, qmat, _DN_T, preferred_element_type=F32) + bias
        m_prev = m_ref[...]
        m_next = jnp.maximum(m_prev, jnp.max(s, axis=0, keepdims=True))
        alpha = jnp.exp(m_prev - m_next)
        p = jnp.exp(s - m_next)
        l_ref[...] = alpha * l_ref[...] + jnp.sum(p, axis=0, keepdims=True)
        acc_ref[...] = alpha * acc_ref[...] + jnp.dot(vt_tile, p.astype(BF16), preferred_element_type=F32)
        m_ref[...] = m_next

    def keys(kt):
        return pl.ds(pl.multiple_of(kt * tk, tk), tk)

    init()

    def far_body(kt, carry):
        step(qa, kaug_ref[keys(kt), :], vst_ref[:, keys(kt)], tab_ref[TAB_CONST])
        return carry

    lax.fori_loop(0, jnp.maximum(i - 1, 0), far_body, 0)

    @pl.when(i >= 1)
    def _():
        step(qa, kaug_ref[keys(i - 1), :], vst_ref[:, keys(i - 1)], tab_ref[TAB_SUB])

    step(qa, kaug_ref[keys(i), :], vst_ref[:, keys(i)], tab_ref[TAB_DIAG])
    o_s = acc_ref[...] * (1.0 / l_ref[...])

    init()
    for d in range(nw, -1, -1):
        kind = TAB_DIAG if d == 0 else TAB_SUB if d == 1 else TAB_FAR if d == nw else TAB_CONST

        @pl.when(i >= d)
        def _(d=d, kind=kind):
            step(qs, kw_ref[keys(i - d), :], vwt_ref[:, keys(i - d)], tab_ref[kind])

    o_w = acc_ref[...] * (1.0 / l_ref[...])

    gt = gate_ref[...].T
    for h in range(HPG):
        sl = slice(h * tq, (h + 1) * tq)
        o = gt[3 * h:3 * h + 1, :] * o_c[:, sl]
        o = o + gt[3 * h + 1:3 * h + 2, :] * o_s[:, sl]
        o = o + gt[3 * h + 2:3 * h + 3, :] * o_w[:, sl]
        o_ref[:, h * HEAD_DIM:(h + 1) * HEAD_DIM] = o.T.astype(BF16)


def _nsa(qkv, cmp_k, cmp_vt, vs_t, vw_t, gates, bias_ct, tabs_t, ovl, e_t, batch, seq_len, *, tq):
    n_t = seq_len // tq
    n_sel = seq_len // SEL_LEN
    n_chunks = cmp_k.shape[2]
    mcols = HPG * tq
    qw = HPG * HEAD_DIM
    col_ks = (ATTN_WIDTH + 2 * KV_WIDTH) // HEAD_DIM
    col_kw = (ATTN_WIDTH + 4 * KV_WIDTH) // HEAD_DIM

    vt_spec = pl.BlockSpec((None, None, HEAD_DIM, seq_len), lambda b, g, i: (b, g, 0, 0))
    kern = functools.partial(_nsa_kernel, tq=tq, n_sel=n_sel)
    return pl.pallas_call(
        kern,
        grid=(batch, N_KV, n_t),
        in_specs=[
            pl.BlockSpec((tq, qw), lambda b, g, i: (b * n_t + i, g)),
            pl.BlockSpec((None, None, n_chunks, HEAD_DIM), lambda b, g, i: (b, g, 0, 0)),
            pl.BlockSpec((None, None, HEAD_DIM, n_chunks), lambda b, g, i: (b, g, 0, 0)),
            pl.BlockSpec((seq_len, HEAD_DIM), lambda b, g, i: (b, col_ks + g)),
            vt_spec,
            pl.BlockSpec((seq_len, HEAD_DIM), lambda b, g, i: (b, col_kw + g)),
            vt_spec,
            pl.BlockSpec((tq, LANE), lambda b, g, i: (b * n_t + i, g)),
            pl.BlockSpec((None, None, n_chunks, mcols), lambda b, g, i: (g, i, 0, 0)),
            pl.BlockSpec((None, 4, tq, mcols), lambda b, g, i: (g, 0, 0, 0)),
            pl.BlockSpec((n_sel, n_chunks), lambda b, g, i: (0, 0)),
            pl.BlockSpec((seq_len, LANE), lambda b, g, i: (0, 0)),
        ],
        out_specs=pl.BlockSpec((tq, qw), lambda b, g, i: (b * n_t + i, g)),
        out_shape=jax.ShapeDtypeStruct((batch * seq_len, ATTN_WIDTH), BF16),
        scratch_shapes=[
            pltpu.VMEM((seq_len, 2 * HEAD_DIM), BF16),
            pltpu.VMEM((1, mcols), F32),
            pltpu.VMEM((1, mcols), F32),
            pltpu.VMEM((HEAD_DIM, mcols), F32),
        ],
        compiler_params=_params(3),
        name="nsa_attention",
    )(qkv, cmp_k, cmp_vt, qkv, vs_t, qkv, vw_t, gates, bias_ct, tabs_t, ovl, e_t)


def _oproj_kernel(oa_ref, ov_ref, wo_ref, x_ref, g1_ref, g2_ref, x1_ref, h2_ref):
    ka = oa_ref.shape[1]
    mix = jnp.dot(oa_ref[...], wo_ref[0:ka, :], preferred_element_type=F32)
    mix = mix + jnp.dot(ov_ref[...], wo_ref[ka:, :], preferred_element_type=F32)
    x1 = x_ref[...] + _rms(mix, g1_ref[...])
    x1_ref[...] = x1
    h2_ref[...] = _rms(x1, g2_ref[...]).astype(BF16)


def _oproj(o_attn, o_conv, w_o, x2, g_post, g_pre, *, tm=512):
    m = x2.shape[0]
    ka, kv = o_attn.shape[1], o_conv.shape[1]
    row = lambda i: (i, 0)
    fixed = lambda i: (0, 0)
    return pl.pallas_call(
        _oproj_kernel,
        grid=(m // tm,),
        in_specs=[
            pl.BlockSpec((tm, ka), row),
            pl.BlockSpec((tm, kv), row),
            pl.BlockSpec((ka + kv, D_MODEL), fixed),
            pl.BlockSpec((tm, D_MODEL), row),
            pl.BlockSpec((1, D_MODEL), fixed),
            pl.BlockSpec((1, D_MODEL), fixed),
        ],
        out_specs=[pl.BlockSpec((tm, D_MODEL), row), pl.BlockSpec((tm, D_MODEL), row)],
        out_shape=[jax.ShapeDtypeStruct((m, D_MODEL), F32), jax.ShapeDtypeStruct((m, D_MODEL), BF16)],
        compiler_params=_params(1),
        name="out_proj",
    )(o_attn, o_conv, w_o, x2, g_post, g_pre)


def _ffn_kernel(h_ref, wu_ref, wd_ref, x1_ref, g_ref, o_ref, acc_ref):
    j = pl.program_id(1)
    a = jnp.dot(h_ref[...], wu_ref[...], preferred_element_type=F32)
    a = jnp.square(jnp.maximum(a, 0.0)).astype(BF16)
    d = jnp.dot(a, wd_ref[...], preferred_element_type=F32)

    @pl.when(j == 0)
    def _():
        acc_ref[...] = d

    @pl.when(j > 0)
    def _():
        acc_ref[...] += d

    @pl.when(j == pl.num_programs(1) - 1)
    def _():
        o_ref[...] = x1_ref[...] + _rms(acc_ref[...], g_ref[...])


def _ffn(h2, w_up, w_down, x1, g_post, *, tm=512, tf=512):
    m = h2.shape[0]
    d_ff = w_up.shape[1]
    return pl.pallas_call(
        _ffn_kernel,
        grid=(m // tm, d_ff // tf),
        in_specs=[
            pl.BlockSpec((tm, D_MODEL), lambda i, j: (i, 0)),
            pl.BlockSpec((D_MODEL, tf), lambda i, j: (0, j)),
            pl.BlockSpec((tf, D_MODEL), lambda i, j: (j, 0)),
            pl.BlockSpec((tm, D_MODEL), lambda i, j: (i, 0)),
            pl.BlockSpec((1, D_MODEL), lambda i, j: (0, 0)),
        ],
        out_specs=pl.BlockSpec((tm, D_MODEL), lambda i, j: (i, 0)),
        out_shape=jax.ShapeDtypeStruct((m, D_MODEL), F32),
        scratch_shapes=[pltpu.VMEM((tm, D_MODEL), F32)],
        compiler_params=_params(2),
        name="ffn",
    )(h2, w_up, w_down, x1, g_post)


def _bucket_np(dist):
    n = np.maximum(dist, 0)
    max_exact = N_BUCKETS // 2
    nf = np.maximum(n, 1).astype(np.float32)
    large = max_exact + (np.log(nf / np.float32(max_exact)) / np.float32(math.log(MAX_DIST / max_exact))
                         * np.float32(N_BUCKETS - max_exact)).astype(np.int32)
    large = np.minimum(large, N_BUCKETS - 1)
    return np.where(n < max_exact, n, large)


def _bucket_starts():
    b = _bucket_np(np.arange(4 * MAX_DIST))
    return [int(np.argmax(b == k)) for k in range(N_BUCKETS)]


def _bias_by_distance(rel_bias, dist):
    starts = _bucket_starts()
    d = jnp.asarray(dist, jnp.int32)[None]
    col = lambda k: rel_bias[:, k].reshape((N_HEADS,) + (1,) * dist.ndim)
    out = jnp.broadcast_to(col(0), (N_HEADS,) + dist.shape)
    for k in range(1, N_BUCKETS):
        out = jnp.where(d >= starts[k], col(k), out)
    return out


def _attention_tables(rel_bias, seq_len, tq):
    n_t = seq_len // tq
    c = np.arange(tq)[:, None]
    r = np.arange(tq)[None, :]
    diag = jnp.where(jnp.asarray(r >= c)[None], _bias_by_distance(rel_bias, np.maximum(r - c, 0)), NEG)
    sub = _bias_by_distance(rel_bias, tq + r - c)
    const = jnp.broadcast_to(rel_bias[:, N_BUCKETS - 1][:, None, None], (N_HEADS, tq, tq))
    far = jnp.where(jnp.asarray(r < c)[None], const, NEG)
    tabs = jnp.stack([diag, sub, far, const], axis=1)
    tabs = tabs.reshape(N_KV, HPG, 4, tq, tq).transpose(0, 2, 3, 1, 4).reshape(N_KV, 4, tq, HPG * tq)

    n_chunks = seq_len // CMP_STRIDE
    n_cmp = (seq_len - CMP_LEN) // CMP_STRIDE + 1
    shape = (n_t, n_chunks, tq)
    cc = lax.broadcasted_iota(jnp.int32, shape, 1)
    t = lax.broadcasted_iota(jnp.int32, shape, 0) * tq + lax.broadcasted_iota(jnp.int32, shape, 2)
    dist_c = t - (cc * CMP_STRIDE + CMP_LEN - 1)
    starts = _bucket_starts()
    col = lambda k: rel_bias[:, k][:, None, None, None]
    bias_c = jnp.broadcast_to(col(0), (N_HEADS,) + shape)
    for k in range(1, N_BUCKETS):
        bias_c = jnp.where(dist_c[None] >= starts[k], col(k), bias_c)
    bias_c = jnp.where(((dist_c >= 0) & (cc < n_cmp))[None], bias_c, NEG)
    bias_c = bias_c.reshape(N_KV, HPG, n_t, n_chunks, tq).transpose(0, 2, 3, 1, 4)
    bias_c = bias_c.reshape(N_KV, n_t, n_chunks, HPG * tq)

    n_sel = seq_len // SEL_LEN
    ci = np.arange(n_chunks)[None, :] * CMP_STRIDE
    sj = np.arange(n_sel)[:, None] * SEL_LEN
    ovl = ((ci < sj + SEL_LEN) & (ci + CMP_LEN > sj) & (np.arange(n_chunks)[None, :] < n_cmp))
    e_t = (np.arange(seq_len)[:, None] // SEL_LEN == np.arange(LANE)[None, :])
    return tabs.astype(F32), bias_c.astype(F32), jnp.asarray(ovl, BF16), jnp.asarray(e_t, BF16)


def kernel(x, w_in, pe_cmp, w_cmp_k1, w_cmp_k2, w_cmp_v1, w_cmp_v2, conv_w, rel_bias, w_o, w_up, w_down,
           g_pre_mix, g_post_mix, g_pre_ffn, g_post_ffn):
    batch, seq_len, _ = x.shape
    depth = w_in.shape[0]
    tq = 256
    n_chunks = seq_len // CMP_STRIDE
    half = CMP_STRIDE * HEAD_DIM
    tabs_t, bias_ct, ovl, e_t = _attention_tables(rel_bias, seq_len, tq)
    x2 = x.reshape(batch * seq_len, D_MODEL)
    for l in range(depth):
        wl = w_in[l]
        w_qkv = wl[:, :QKV_WIDTH].astype(BF16)
        wg = wl[:, GATE_OFF:CONV_OFF].reshape(D_MODEL, N_KV, HPG * N_BRANCH)
        wg = jnp.pad(wg, ((0, 0), (0, 0), (0, LANE - HPG * N_BRANCH))).reshape(D_MODEL, N_KV * LANE).astype(BF16)
        w_h, w_b, w_c = (wl[:, CONV_OFF + k * CONV_WIDTH:CONV_OFF + (k + 1) * CONV_WIDTH].astype(BF16)
                         for k in range(3))
        g1 = g_pre_mix[l].reshape(1, D_MODEL)

        qkv, gates = _qkv_proj(x2, g1, w_qkv, wg)
        o_conv = _conv_proj(x2, g1, w_h, w_b, w_c, conv_w[l], seq_len)

        xc = qkv[:, ATTN_WIDTH:ATTN_WIDTH + 2 * KV_WIDTH]
        xc = xc.reshape(batch, n_chunks, CMP_STRIDE, 2 * N_KV, HEAD_DIM).transpose(0, 3, 1, 2, 4)
        xc = xc.reshape(batch, 2 * N_KV, n_chunks, half)
        w1 = jnp.stack([w_cmp_k1[l], w_cmp_v1[l]]).reshape(2, 2 * half, CMP_HIDDEN).astype(BF16)
        w2 = jnp.stack([w_cmp_k2[l], w_cmp_v2[l]]).astype(BF16)
        cmp_kv = _compress(xc, pe_cmp[l].reshape(2, half), w1, w2)
        cmp_k = cmp_kv[:, :N_KV]
        cmp_vt = cmp_kv[:, N_KV:].transpose(0, 1, 3, 2)

        def v_t(which):
            off = ATTN_WIDTH + which * KV_WIDTH
            v = qkv[:, off:off + KV_WIDTH].reshape(batch, seq_len, N_KV, HEAD_DIM)
            return v.transpose(0, 2, 3, 1)

        o_attn = _nsa(qkv, cmp_k, cmp_vt, v_t(3), v_t(5), gates, bias_ct, tabs_t, ovl, e_t,
                      batch, seq_len, tq=tq)

        x1, h2 = _oproj(o_attn, o_conv, w_o[l].astype(BF16), x2,
                        g_post_mix[l].reshape(1, D_MODEL), g_pre_ffn[l].reshape(1, D_MODEL))
        x2 = _ffn(h2, w_up[l].astype(BF16), w_down[l].astype(BF16), x1, g_post_ffn[l].reshape(1, D_MODEL))
    return x2.reshape(batch, seq_len, D_MODEL)
```

```python
import functools
import math

import numpy as np
import jax
import jax.numpy as jnp
from jax import lax
from jax.experimental import pallas as pl
from jax.experimental.pallas import tpu as pltpu

F32 = jnp.float32
BF16 = jnp.bfloat16

D_MODEL = 2048
N_HEADS = 8
N_KV = 2
HPG = N_HEADS // N_KV
HEAD_DIM = 128
ATTN_WIDTH = N_HEADS * HEAD_DIM
KV_WIDTH = N_KV * HEAD_DIM
CONV_WIDTH = D_MODEL - ATTN_WIDTH
CONV_K = 3
N_BRANCH = 3
CMP_LEN = 32
CMP_STRIDE = 16
CMP_HIDDEN = 256
SEL_LEN = 64
SEL_TOPK = 16
WINDOW = 512
N_BUCKETS = 32
MAX_DIST = 128
EPS = 1e-6
NEG = -1e30
HALF_NEG = -5e29
FORCE = 1e9

QKV_WIDTH = ATTN_WIDTH + 6 * KV_WIDTH
GATE_OFF = QKV_WIDTH
CONV_OFF = QKV_WIDTH + N_HEADS * N_BRANCH
LANE = 128
VMEM_LIMIT = 56 * 1024 * 1024

_DN_T = (((1,), (1,)), ((), ()))


def _rms(x, g):
    ms = jnp.mean(x * x, axis=-1, keepdims=True)
    return x * lax.rsqrt(ms + EPS) * g


def _params(n_axes):
    return pltpu.CompilerParams(dimension_semantics=("arbitrary",) * n_axes, vmem_limit_bytes=VMEM_LIMIT)


def _qkv_kernel(x_ref, g_ref, w_ref, wg_ref, o_ref, gate_ref, h_ref, *, n_q_blocks, scale):
    j = pl.program_id(1)

    @pl.when(j == 0)
    def _():
        hb = _rms(x_ref[...], g_ref[...]).astype(BF16)
        h_ref[...] = hb
        gate_ref[...] = jax.nn.sigmoid(jnp.dot(hb, wg_ref[...], preferred_element_type=F32))

    acc = jnp.dot(h_ref[...], w_ref[...], preferred_element_type=F32)
    acc = acc * jnp.where(j < n_q_blocks, scale, 1.0).astype(F32)
    o_ref[...] = acc.astype(BF16)


def _qkv_proj(x2, g, w_qkv, w_gate, *, tm=512, tn=512):
    m = x2.shape[0]
    n = w_qkv.shape[1]
    gw = w_gate.shape[1]
    kern = functools.partial(_qkv_kernel, n_q_blocks=ATTN_WIDTH // tn, scale=HEAD_DIM ** -0.5 * math.log2(math.e))
    return pl.pallas_call(
        kern,
        grid=(m // tm, n // tn),
        in_specs=[
            pl.BlockSpec((tm, D_MODEL), lambda i, j: (i, 0)),
            pl.BlockSpec((1, D_MODEL), lambda i, j: (0, 0)),
            pl.BlockSpec((D_MODEL, tn), lambda i, j: (0, j)),
            pl.BlockSpec((D_MODEL, gw), lambda i, j: (0, 0)),
        ],
        out_specs=[
            pl.BlockSpec((tm, tn), lambda i, j: (i, j)),
            pl.BlockSpec((tm, gw), lambda i, j: (i, 0)),
        ],
        out_shape=[
            jax.ShapeDtypeStruct((m, n), BF16),
            jax.ShapeDtypeStruct((m, gw), F32),
        ],
        scratch_shapes=[pltpu.VMEM((tm, D_MODEL), BF16)],
        compiler_params=_params(2),
        name="qkv_proj",
    )(x2, g, w_qkv, w_gate)


def _conv_kernel(x_ref, g_ref, wh_ref, wb_ref, wc_ref, cw_ref, o_ref, h_ref, carry_ref, *, tiles_per_seq):
    i = pl.program_id(0)
    j = pl.program_id(1)

    @pl.when(j == 0)
    def _():
        h_ref[...] = _rms(x_ref[...], g_ref[...]).astype(BF16)

    hb = h_ref[...]
    ch = jnp.dot(hb, wh_ref[...], preferred_element_type=F32)
    cb = jnp.dot(hb, wb_ref[...], preferred_element_type=F32)
    cc = jnp.dot(hb, wc_ref[...], preferred_element_type=F32)
    u = cc * ch
    tm = u.shape[0]
    prev = carry_ref[j]
    prev = jnp.where(i % tiles_per_seq == 0, 0.0, prev)
    carry_ref[j] = u[tm - 8:tm, :]
    row = lax.broadcasted_iota(jnp.int32, u.shape, 0)
    u1 = jnp.where(row == 0, prev[7:8, :], pltpu.roll(u, 1, axis=0))
    u2 = jnp.where(row == 0, prev[6:7, :], jnp.where(row == 1, prev[7:8, :], pltpu.roll(u, 2, axis=0)))
    w = cw_ref[...]
    y = w[0:1, :] * u2
    y = y + w[1:2, :] * u1
    y = y + w[2:3, :] * u
    o_ref[...] = (cb * y).astype(BF16)


def _conv_proj(x2, g, w_h, w_b, w_c, conv_w, seq_len, *, tm=512, tc=256):
    m = x2.shape[0]
    c = w_h.shape[1]
    kern = functools.partial(_conv_kernel, tiles_per_seq=seq_len // tm)
    wspec = pl.BlockSpec((D_MODEL, tc), lambda i, j: (0, j))
    return pl.pallas_call(
        kern,
        grid=(m // tm, c // tc),
        in_specs=[
            pl.BlockSpec((tm, D_MODEL), lambda i, j: (i, 0)),
            pl.BlockSpec((1, D_MODEL), lambda i, j: (0, 0)),
            wspec, wspec, wspec,
            pl.BlockSpec((CONV_K, tc), lambda i, j: (0, j)),
        ],
        out_specs=pl.BlockSpec((tm, tc), lambda i, j: (i, j)),
        out_shape=jax.ShapeDtypeStruct((m, c), BF16),
        scratch_shapes=[pltpu.VMEM((tm, D_MODEL), BF16), pltpu.VMEM((c // tc, 8, tc), F32)],
        compiler_params=_params(2),
        name="conv_proj",
    )(x2, g, w_h, w_b, w_c, conv_w)


def _compress_kernel(x_ref, pe_ref, w1_ref, w2_ref, o_ref):
    half = CMP_STRIDE * HEAD_DIM
    x = x_ref[...].astype(F32)
    pe = pe_ref[...]
    xa = (x + pe[0:1, :]).astype(BF16)
    xb = (x + pe[1:2, :]).astype(BF16)
    a = jnp.dot(xa, w1_ref[0:half, :], preferred_element_type=F32)
    b = jnp.dot(xb, w1_ref[half:2 * half, :], preferred_element_type=F32)
    n = a.shape[0]
    pre = a + pltpu.roll(b, n - 1, axis=0)
    hid = pre * jax.nn.sigmoid(pre)
    out = jnp.dot(hid.astype(BF16), w2_ref[...], preferred_element_type=F32)
    row = lax.broadcasted_iota(jnp.int32, out.shape, 0)
    o_ref[...] = jnp.where(row < n - 1, out, 0.0).astype(BF16)


def _compress(xc, pe2, w1, w2):
    b, nkv, n_chunks, width = xc.shape
    return pl.pallas_call(
        _compress_kernel,
        grid=(b, nkv),
        in_specs=[
            pl.BlockSpec((None, None, n_chunks, width), lambda i, j: (i, j, 0, 0)),
            pl.BlockSpec((2, width), lambda i, j: (0, 0)),
            pl.BlockSpec((None, 2 * width, CMP_HIDDEN), lambda i, j: (j // N_KV, 0, 0)),
            pl.BlockSpec((None, CMP_HIDDEN, HEAD_DIM), lambda i, j: (j // N_KV, 0, 0)),
        ],
        out_specs=pl.BlockSpec((None, None, n_chunks, HEAD_DIM), lambda i, j: (i, j, 0, 0)),
        out_shape=jax.ShapeDtypeStruct((b, nkv, n_chunks, HEAD_DIM), BF16),
        compiler_params=_params(2),
        name="compress",
    )(xc, pe2, w1, w2)


TAB_DIAG, TAB_SUB, TAB_FAR = 0, 1, 2
BAND_BELOW = 8


def _nsa_kernel(q_ref, kc_ref, vct_ref, ks_ref, vst_ref, kw_ref, vwt_ref, gate_ref, band_ref, tab_ref,
                ovl_ref, et_ref, o_ref, kaug_ref, sc_ref, m_ref, l_ref, acc_ref, *, tq, n_sel):
    i = pl.program_id(2)
    tk = tq
    mcols = HPG * tq
    nw = WINDOW // tk
    cpt = tq // CMP_STRIDE
    band_rows = BAND_BELOW + cpt

    @pl.when(i == 0)
    def _():
        kaug_ref[:, 0:HEAD_DIM] = ks_ref[...]
        kaug_ref[:, HEAD_DIM:2 * HEAD_DIM] = et_ref[...]

    q = q_ref[...]
    qs = jnp.concatenate([q[:, h * HEAD_DIM:(h + 1) * HEAD_DIM] for h in range(HPG)], axis=0)

    raw = lax.dot_general(kc_ref[...], qs, _DN_T, preferred_element_type=F32)
    crow = lax.broadcasted_iota(jnp.int32, raw.shape, 0)
    sc_ref[...] = jnp.where(crow < cpt * (i + 1), raw, NEG)
    band = pl.ds(pl.multiple_of(jnp.maximum(cpt * i - BAND_BELOW, 0), 8), band_rows)
    sc_ref[band, :] += band_ref[jnp.minimum(i, 1)]
    sc = sc_ref[...]
    mc = jnp.max(sc, axis=0, keepdims=True)
    pc = jnp.where(sc > HALF_NEG, jnp.exp2(sc - mc), 0.0)
    lc = jnp.sum(pc, axis=0, keepdims=True)
    pc = pc * jnp.where(lc > 0.0, 1.0 / lc, 0.0)
    o_c = jnp.dot(vct_ref[...], pc.astype(BF16), preferred_element_type=F32)

    ps = pc[:, 0:tq] + pc[:, tq:2 * tq] + pc[:, 2 * tq:3 * tq] + pc[:, 3 * tq:4 * tq]
    hi = ps.astype(BF16)
    r1 = ps - hi.astype(F32)
    mid = r1.astype(BF16)
    lo = (r1 - mid.astype(F32)).astype(BF16)
    ovl = ovl_ref[...]
    imp = (jnp.dot(ovl, hi, preferred_element_type=F32) + jnp.dot(ovl, mid, preferred_element_type=F32)
           + jnp.dot(ovl, lo, preferred_element_type=F32))
    jj = lax.broadcasted_iota(jnp.int32, (n_sel, tq), 0)
    tt = i * tq + lax.broadcasted_iota(jnp.int32, (n_sel, tq), 1)
    cur = tt >> int(math.log2(SEL_LEN))
    forced = (jj == 0) | (jj == cur) | (jj == cur - 1)
    imp = jnp.where(forced, FORCE, imp)
    imp = jnp.where(jj * SEL_LEN <= tt, imp, NEG)
    rank = jnp.zeros((n_sel, tq), jnp.int32)
    for b in range(n_sel):
        row = imp[b:b + 1, :]
        tie = jnp.where(row == imp, jnp.where(jj > b, 1, 0), 0)
        rank = rank + jnp.where(row > imp, 1, tie)
    selb_t = jnp.where(rank < SEL_TOPK, 0.0, NEG)
    selb = jnp.concatenate([selb_t, jnp.zeros((LANE - n_sel, tq), F32)], axis=0).T
    selb = selb.astype(BF16)
    qa = jnp.concatenate([qs, jnp.concatenate([selb] * HPG, axis=0)], axis=1)

    def init():
        m_ref[...] = jnp.full((1, mcols), NEG, F32)
        l_ref[...] = jnp.zeros((1, mcols), F32)
        acc_ref[...] = jnp.zeros((HEAD_DIM, mcols), F32)

    def step(qmat, k_ref, vt_ref, kt, kind):
        keys = pl.ds(pl.multiple_of(kt * tk, tk), tk)
        s = lax.dot_general(k_ref[keys, :], qmat, _DN_T, preferred_element_type=F32)
        if kind is not None:
            s = s + tab_ref[kind]
        m_prev = m_ref[...]
        m_next = jnp.maximum(m_prev, jnp.max(s, axis=0, keepdims=True))
        alpha = jnp.exp2(m_prev - m_next)
        p = jnp.exp2(s - m_next)
        l_ref[...] = alpha * l_ref[...] + jnp.sum(p, axis=0, keepdims=True)
        acc_ref[...] = alpha * acc_ref[...] + jnp.dot(vt_ref[:, keys], p.astype(BF16),
                                                      preferred_element_type=F32)
        m_ref[...] = m_next

    init()
    n_far = jnp.maximum(i - 1, 0)
    odd = n_far % 2

    @pl.when(odd == 1)
    def _():
        step(qa, kaug_ref, vst_ref, 0, None)

    def far_pair(p, carry):
        step(qa, kaug_ref, vst_ref, odd + 2 * p, None)
        step(qa, kaug_ref, vst_ref, odd + 2 * p + 1, None)
        return carry

    lax.fori_loop(0, n_far // 2, far_pair, 0)

    @pl.when(i >= 1)
    def _():
        step(qa, kaug_ref, vst_ref, i - 1, TAB_SUB)
        step(qa, kaug_ref, vst_ref, i, TAB_DIAG)

    @pl.when(i == 0)
    def _():
        step(qa, kaug_ref, vst_ref, i, TAB_DIAG)

    o_s = acc_ref[...] * (1.0 / l_ref[...])

    init()
    for avail in range(nw + 1):

        @pl.when((i == avail) if avail < nw else (i >= nw))
        def _(avail=avail):
            for d in range(avail, -1, -1):
                kind = TAB_DIAG if d == 0 else TAB_SUB if d == 1 else TAB_FAR if d == nw else None
                step(qs, kw_ref, vwt_ref, i - d, kind)

    o_w = acc_ref[...] * (1.0 / l_ref[...])

    gt = gate_ref[...].T
    for h in range(HPG):
        sl = slice(h * tq, (h + 1) * tq)
        o = gt[3 * h:3 * h + 1, :] * o_c[:, sl]
        o = o + gt[3 * h + 1:3 * h + 2, :] * o_s[:, sl]
        o = o + gt[3 * h + 2:3 * h + 3, :] * o_w[:, sl]
        o_ref[:, h * HEAD_DIM:(h + 1) * HEAD_DIM] = o.T.astype(BF16)


def _nsa(qkv, cmp_k, cmp_vt, vs_t, vw_t, gates, band_t, tabs_t, ovl, e_t, batch, seq_len, *, tq):
    n_t = seq_len // tq
    n_sel = seq_len // SEL_LEN
    n_chunks = cmp_k.shape[2]
    mcols = HPG * tq
    band_rows = band_t.shape[2]
    qw = HPG * HEAD_DIM
    col_ks = (ATTN_WIDTH + 2 * KV_WIDTH) // HEAD_DIM
    col_kw = (ATTN_WIDTH + 4 * KV_WIDTH) // HEAD_DIM

    vt_spec = pl.BlockSpec((None, None, HEAD_DIM, seq_len), lambda b, g, i: (b, g, 0, 0))
    kern = functools.partial(_nsa_kernel, tq=tq, n_sel=n_sel)
    return pl.pallas_call(
        kern,
        grid=(batch, N_KV, n_t),
        in_specs=[
            pl.BlockSpec((tq, qw), lambda b, g, i: (b * n_t + i, g)),
            pl.BlockSpec((None, None, n_chunks, HEAD_DIM), lambda b, g, i: (b, g, 0, 0)),
            pl.BlockSpec((None, None, HEAD_DIM, n_chunks), lambda b, g, i: (b, g, 0, 0)),
            pl.BlockSpec((seq_len, HEAD_DIM), lambda b, g, i: (b, col_ks + g)),
            vt_spec,
            pl.BlockSpec((seq_len, HEAD_DIM), lambda b, g, i: (b, col_kw + g)),
            vt_spec,
            pl.BlockSpec((tq, LANE), lambda b, g, i: (b * n_t + i, g)),
            pl.BlockSpec((None, 2, band_rows, mcols), lambda b, g, i: (g, 0, 0, 0)),
            pl.BlockSpec((None, 3, tq, mcols), lambda b, g, i: (g, 0, 0, 0)),
            pl.BlockSpec((n_sel, n_chunks), lambda b, g, i: (0, 0)),
            pl.BlockSpec((seq_len, LANE), lambda b, g, i: (0, 0)),
        ],
        out_specs=pl.BlockSpec((tq, qw), lambda b, g, i: (b * n_t + i, g)),
        out_shape=jax.ShapeDtypeStruct((batch * seq_len, ATTN_WIDTH), BF16),
        scratch_shapes=[
            pltpu.VMEM((seq_len, 2 * HEAD_DIM), BF16),
            pltpu.VMEM((n_chunks, mcols), F32),
            pltpu.VMEM((1, mcols), F32),
            pltpu.VMEM((1, mcols), F32),
            pltpu.VMEM((HEAD_DIM, mcols), F32),
        ],
        compiler_params=_params(3),
        name="nsa_attention",
    )(qkv, cmp_k, cmp_vt, qkv, vs_t, qkv, vw_t, gates, band_t, tabs_t, ovl, e_t)


def _oproj_kernel(oa_ref, ov_ref, wo_ref, x_ref, g1_ref, g2_ref, x1_ref, h2_ref):
    ka = oa_ref.shape[1]
    mix = jnp.dot(oa_ref[...], wo_ref[0:ka, :], preferred_element_type=F32)
    mix = mix + jnp.dot(ov_ref[...], wo_ref[ka:, :], preferred_element_type=F32)
    x1 = x_ref[...] + _rms(mix, g1_ref[...])
    x1_ref[...] = x1
    h2_ref[...] = _rms(x1, g2_ref[...]).astype(BF16)


def _oproj(o_attn, o_conv, w_o, x2, g_post, g_pre, *, tm=512):
    m = x2.shape[0]
    ka, kv = o_attn.shape[1], o_conv.shape[1]
    row = lambda i: (i, 0)
    fixed = lambda i: (0, 0)
    return pl.pallas_call(
        _oproj_kernel,
        grid=(m // tm,),
        in_specs=[
            pl.BlockSpec((tm, ka), row),
            pl.BlockSpec((tm, kv), row),
            pl.BlockSpec((ka + kv, D_MODEL), fixed),
            pl.BlockSpec((tm, D_MODEL), row),
            pl.BlockSpec((1, D_MODEL), fixed),
            pl.BlockSpec((1, D_MODEL), fixed),
        ],
        out_specs=[pl.BlockSpec((tm, D_MODEL), row), pl.BlockSpec((tm, D_MODEL), row)],
        out_shape=[jax.ShapeDtypeStruct((m, D_MODEL), F32), jax.ShapeDtypeStruct((m, D_MODEL), BF16)],
        compiler_params=_params(1),
        name="out_proj",
    )(o_attn, o_conv, w_o, x2, g_post, g_pre)


def _ffn_kernel(h_ref, wu_ref, wd_ref, x1_ref, g_ref, o_ref, acc_ref):
    j = pl.program_id(1)
    a = jnp.dot(h_ref[...], wu_ref[...], preferred_element_type=F32)
    a = jnp.square(jnp.maximum(a, 0.0)).astype(BF16)
    d = jnp.dot(a, wd_ref[...], preferred_element_type=F32)

    @pl.when(j == 0)
    def _():
        acc_ref[...] = d

    @pl.when(j > 0)
    def _():
        acc_ref[...] += d

    @pl.when(j == pl.num_programs(1) - 1)
    def _():
        o_ref[...] = x1_ref[...] + _rms(acc_ref[...], g_ref[...])


def _ffn(h2, w_up, w_down, x1, g_post, *, tm=512, tf=512):
    m = h2.shape[0]
    d_ff = w_up.shape[1]
    return pl.pallas_call(
        _ffn_kernel,
        grid=(m // tm, d_ff // tf),
        in_specs=[
            pl.BlockSpec((tm, D_MODEL), lambda i, j: (i, 0)),
            pl.BlockSpec((D_MODEL, tf), lambda i, j: (0, j)),
            pl.BlockSpec((tf, D_MODEL), lambda i, j: (j, 0)),
            pl.BlockSpec((tm, D_MODEL), lambda i, j: (i, 0)),
            pl.BlockSpec((1, D_MODEL), lambda i, j: (0, 0)),
        ],
        out_specs=pl.BlockSpec((tm, D_MODEL), lambda i, j: (i, 0)),
        out_shape=jax.ShapeDtypeStruct((m, D_MODEL), F32),
        scratch_shapes=[pltpu.VMEM((tm, D_MODEL), F32)],
        compiler_params=_params(2),
        name="ffn",
    )(h2, w_up, w_down, x1, g_post)


def _bucket_np(dist):
    n = np.maximum(dist, 0)
    max_exact = N_BUCKETS // 2
    nf = np.maximum(n, 1).astype(np.float32)
    large = max_exact + (np.log(nf / np.float32(max_exact)) / np.float32(math.log(MAX_DIST / max_exact))
                         * np.float32(N_BUCKETS - max_exact)).astype(np.int32)
    large = np.minimum(large, N_BUCKETS - 1)
    return np.where(n < max_exact, n, large)


def _bucket_starts():
    b = _bucket_np(np.arange(4 * MAX_DIST))
    return [int(np.argmax(b == k)) for k in range(N_BUCKETS)]


def _bias_by_distance(rel_bias, dist):
    starts = _bucket_starts()
    d = jnp.asarray(dist, jnp.int32)[None]
    col = lambda k: rel_bias[:, k].reshape((N_HEADS,) + (1,) * dist.ndim)
    out = jnp.broadcast_to(col(0), (N_HEADS,) + dist.shape)
    for k in range(1, N_BUCKETS):
        out = jnp.where(d >= starts[k], col(k), out)
    return out


def _attention_tables(rel_bias, seq_len, tq):
    starts = _bucket_starts()
    assert starts[N_BUCKETS - 1] <= CMP_STRIDE * (BAND_BELOW + 1) - (CMP_LEN - 1)
    assert starts[N_BUCKETS - 1] <= tq
    rel = (rel_bias - rel_bias[:, N_BUCKETS - 1:]) * math.log2(math.e)

    def per_group(tab):
        lead = tab.shape[1:-2]
        nk, nq = tab.shape[-2:]
        tab = tab.reshape((N_KV, HPG) + lead + (nk, nq))
        tab = jnp.moveaxis(tab, 1, -2)
        return tab.reshape((N_KV,) + lead + (nk, HPG * nq)).astype(F32)

    c = np.arange(tq)[:, None]
    r = np.arange(tq)[None, :]
    diag = jnp.where(jnp.asarray(r >= c)[None], _bias_by_distance(rel, np.maximum(r - c, 0)), NEG)
    sub = _bias_by_distance(rel, tq + r - c)
    far = jnp.broadcast_to(jnp.where(jnp.asarray(r < c)[None], 0.0, NEG), (N_HEADS, tq, tq))
    tabs = per_group(jnp.stack([diag, sub, far], axis=1))

    cpt = tq // CMP_STRIDE
    cl = np.arange(BAND_BELOW + cpt)[:, None]
    bands = []
    for first in (0, -BAND_BELOW):
        dist = r - ((cl + first) * CMP_STRIDE + CMP_LEN - 1)
        bands.append(jnp.where(jnp.asarray(dist >= 0)[None], _bias_by_distance(rel, np.maximum(dist, 0)), NEG))
    band = per_group(jnp.stack(bands, axis=1))

    n_chunks = seq_len // CMP_STRIDE
    n_cmp = (seq_len - CMP_LEN) // CMP_STRIDE + 1
    n_sel = seq_len // SEL_LEN
    ci = np.arange(n_chunks)[None, :] * CMP_STRIDE
    sj = np.arange(n_sel)[:, None] * SEL_LEN
    ovl = ((ci < sj + SEL_LEN) & (ci + CMP_LEN > sj) & (np.arange(n_chunks)[None, :] < n_cmp))
    e_t = (np.arange(seq_len)[:, None] // SEL_LEN == np.arange(LANE)[None, :])
    return tabs, band, jnp.asarray(ovl, BF16), jnp.asarray(e_t, BF16)


def kernel(x, w_in, pe_cmp, w_cmp_k1, w_cmp_k2, w_cmp_v1, w_cmp_v2, conv_w, rel_bias, w_o, w_up, w_down,
           g_pre_mix, g_post_mix, g_pre_ffn, g_post_ffn):
    batch, seq_len, _ = x.shape
    depth = w_in.shape[0]
    tq = 256
    n_chunks = seq_len // CMP_STRIDE
    half = CMP_STRIDE * HEAD_DIM
    tabs_t, band_t, ovl, e_t = _attention_tables(rel_bias, seq_len, tq)
    x2 = x.reshape(batch * seq_len, D_MODEL)
    for l in range(depth):
        wl = w_in[l]
        w_qkv = wl[:, :QKV_WIDTH].astype(BF16)
        wg = wl[:, GATE_OFF:CONV_OFF].reshape(D_MODEL, N_KV, HPG * N_BRANCH)
        wg = jnp.pad(wg, ((0, 0), (0, 0), (0, LANE - HPG * N_BRANCH))).reshape(D_MODEL, N_KV * LANE).astype(BF16)
        w_h, w_b, w_c = (wl[:, CONV_OFF + k * CONV_WIDTH:CONV_OFF + (k + 1) * CONV_WIDTH].astype(BF16)
                         for k in range(3))
        g1 = g_pre_mix[l].reshape(1, D_MODEL)

        qkv, gates = _qkv_proj(x2, g1, w_qkv, wg)
        o_conv = _conv_proj(x2, g1, w_h, w_b, w_c, conv_w[l], seq_len)

        xc = qkv[:, ATTN_WIDTH:ATTN_WIDTH + 2 * KV_WIDTH]
        xc = xc.reshape(batch, n_chunks, CMP_STRIDE, 2 * N_KV, HEAD_DIM).transpose(0, 3, 1, 2, 4)
        xc = xc.reshape(batch, 2 * N_KV, n_chunks, half)
        w1 = jnp.stack([w_cmp_k1[l], w_cmp_v1[l]]).reshape(2, 2 * half, CMP_HIDDEN).astype(BF16)
        w2 = jnp.stack([w_cmp_k2[l], w_cmp_v2[l]]).astype(BF16)
        cmp_kv = _compress(xc, pe_cmp[l].reshape(2, half), w1, w2)
        cmp_k = cmp_kv[:, :N_KV]
        cmp_vt = cmp_kv[:, N_KV:].transpose(0, 1, 3, 2)

        def v_t(which):
            off = ATTN_WIDTH + which * KV_WIDTH
            v = qkv[:, off:off + KV_WIDTH].reshape(batch, seq_len, N_KV, HEAD_DIM)
            return v.transpose(0, 2, 3, 1)

        o_attn = _nsa(qkv, cmp_k, cmp_vt, v_t(3), v_t(5), gates, band_t, tabs_t, ovl, e_t,
                      batch, seq_len, tq=tq)

        x1, h2 = _oproj(o_attn, o_conv, w_o[l].astype(BF16), x2,
                        g_post_mix[l].reshape(1, D_MODEL), g_pre_ffn[l].reshape(1, D_MODEL))
        x2 = _ffn(h2, w_up[l].astype(BF16), w_down[l].astype(BF16), x1, g_post_ffn[l].reshape(1, D_MODEL))
    return x2.reshape(batch, seq_len, D_MODEL)
```

```python
import functools
import math

import numpy as np
import jax
import jax.numpy as jnp
from jax import lax
from jax.experimental import pallas as pl
from jax.experimental.pallas import tpu as pltpu

F32 = jnp.float32
BF16 = jnp.bfloat16

D_MODEL = 2048
N_HEADS = 8
N_KV = 2
HPG = N_HEADS // N_KV
HEAD_DIM = 128
ATTN_WIDTH = N_HEADS * HEAD_DIM
KV_WIDTH = N_KV * HEAD_DIM
CONV_WIDTH = D_MODEL - ATTN_WIDTH
CONV_K = 3
N_BRANCH = 3
CMP_LEN = 32
CMP_STRIDE = 16
CMP_HIDDEN = 256
SEL_LEN = 64
SEL_TOPK = 16
WINDOW = 512
N_BUCKETS = 32
MAX_DIST = 128
EPS = 1e-6
NEG = -1e30
HALF_NEG = -5e29
FORCE = 1e9

QKV_WIDTH = ATTN_WIDTH + 6 * KV_WIDTH
GATE_OFF = QKV_WIDTH
CONV_OFF = QKV_WIDTH + N_HEADS * N_BRANCH
LANE = 128
VMEM_LIMIT = 56 * 1024 * 1024

_DN_T = (((1,), (1,)), ((), ()))


def _rms(x, g):
    ms = jnp.mean(x * x, axis=-1, keepdims=True)
    return x * lax.rsqrt(ms + EPS) * g


def _params(n_axes):
    return pltpu.CompilerParams(dimension_semantics=("arbitrary",) * n_axes, vmem_limit_bytes=VMEM_LIMIT)


def _qkv_kernel(x_ref, g_ref, w_ref, wg_ref, cs_ref, o_ref, gate_ref, h_ref):
    j = pl.program_id(1)

    @pl.when(j == 0)
    def _():
        hb = _rms(x_ref[...], g_ref[...]).astype(BF16)
        h_ref[...] = hb
        gate_ref[...] = jax.nn.sigmoid(jnp.dot(hb, wg_ref[...], preferred_element_type=F32))

    acc = jnp.dot(h_ref[...], w_ref[...], preferred_element_type=F32)
    o_ref[...] = (acc * cs_ref[...]).astype(BF16)


def _qkv_proj(x2, g, w_qkv, w_gate, *, tm=1024, tn=1280):
    m = x2.shape[0]
    n = w_qkv.shape[1]
    gw = w_gate.shape[1]
    q_scale = HEAD_DIM ** -0.5 * math.log2(math.e)
    col_scale = jnp.where(jnp.arange(n) < ATTN_WIDTH, q_scale, 1.0).astype(F32).reshape(1, n)
    return pl.pallas_call(
        _qkv_kernel,
        grid=(m // tm, n // tn),
        in_specs=[
            pl.BlockSpec((tm, D_MODEL), lambda i, j: (i, 0)),
            pl.BlockSpec((1, D_MODEL), lambda i, j: (0, 0)),
            pl.BlockSpec((D_MODEL, tn), lambda i, j: (0, j)),
            pl.BlockSpec((D_MODEL, gw), lambda i, j: (0, 0)),
            pl.BlockSpec((1, tn), lambda i, j: (0, j)),
        ],
        out_specs=[
            pl.BlockSpec((tm, tn), lambda i, j: (i, j)),
            pl.BlockSpec((tm, gw), lambda i, j: (i, 0)),
        ],
        out_shape=[
            jax.ShapeDtypeStruct((m, n), BF16),
            jax.ShapeDtypeStruct((m, gw), F32),
        ],
        scratch_shapes=[pltpu.VMEM((tm, D_MODEL), BF16)],
        compiler_params=_params(2),
        name="qkv_proj",
    )(x2, g, w_qkv, w_gate, col_scale)


def _conv_kernel(x_ref, g_ref, wh_ref, wb_ref, wc_ref, cw_ref, o_ref, h_ref, carry_ref, *, tiles_per_seq):
    i = pl.program_id(0)
    j = pl.program_id(1)

    @pl.when(j == 0)
    def _():
        h_ref[...] = _rms(x_ref[...], g_ref[...]).astype(BF16)

    hb = h_ref[...]
    ch = jnp.dot(hb, wh_ref[...], preferred_element_type=F32)
    cb = jnp.dot(hb, wb_ref[...], preferred_element_type=F32)
    cc = jnp.dot(hb, wc_ref[...], preferred_element_type=F32)
    u = cc * ch
    tm = u.shape[0]
    prev = carry_ref[j]
    prev = jnp.where(i % tiles_per_seq == 0, 0.0, prev)
    carry_ref[j] = u[tm - 8:tm, :]
    row = lax.broadcasted_iota(jnp.int32, u.shape, 0)
    u1 = jnp.where(row == 0, prev[7:8, :], pltpu.roll(u, 1, axis=0))
    u2 = jnp.where(row == 0, prev[6:7, :], jnp.where(row == 1, prev[7:8, :], pltpu.roll(u, 2, axis=0)))
    w = cw_ref[...]
    y = w[0:1, :] * u2
    y = y + w[1:2, :] * u1
    y = y + w[2:3, :] * u
    o_ref[...] = (cb * y).astype(BF16)


def _conv_proj(x2, g, w_h, w_b, w_c, conv_w, seq_len, *, tm=1024, tc=512):
    m = x2.shape[0]
    c = w_h.shape[1]
    kern = functools.partial(_conv_kernel, tiles_per_seq=seq_len // tm)
    wspec = pl.BlockSpec((D_MODEL, tc), lambda i, j: (0, j))
    return pl.pallas_call(
        kern,
        grid=(m // tm, c // tc),
        in_specs=[
            pl.BlockSpec((tm, D_MODEL), lambda i, j: (i, 0)),
            pl.BlockSpec((1, D_MODEL), lambda i, j: (0, 0)),
            wspec, wspec, wspec,
            pl.BlockSpec((CONV_K, tc), lambda i, j: (0, j)),
        ],
        out_specs=pl.BlockSpec((tm, tc), lambda i, j: (i, j)),
        out_shape=jax.ShapeDtypeStruct((m, c), BF16),
        scratch_shapes=[pltpu.VMEM((tm, D_MODEL), BF16), pltpu.VMEM((c // tc, 8, tc), F32)],
        compiler_params=_params(2),
        name="conv_proj",
    )(x2, g, w_h, w_b, w_c, conv_w)


def _compress_kernel(x_ref, pe_ref, w1_ref, w2_ref, o_ref):
    half = CMP_STRIDE * HEAD_DIM
    x = x_ref[...].astype(F32)
    pe = pe_ref[...]
    xa = (x + pe[0:1, :]).astype(BF16)
    xb = (x + pe[1:2, :]).astype(BF16)
    a = jnp.dot(xa, w1_ref[0:half, :], preferred_element_type=F32)
    b = jnp.dot(xb, w1_ref[half:2 * half, :], preferred_element_type=F32)
    n = a.shape[0]
    pre = a + pltpu.roll(b, n - 1, axis=0)
    hid = pre * jax.nn.sigmoid(pre)
    out = jnp.dot(hid.astype(BF16), w2_ref[...], preferred_element_type=F32)
    row = lax.broadcasted_iota(jnp.int32, out.shape, 0)
    o_ref[...] = jnp.where(row < n - 1, out, 0.0).astype(BF16)


def _compress(xc, pe2, w1, w2):
    b, nkv, n_chunks, width = xc.shape
    return pl.pallas_call(
        _compress_kernel,
        grid=(b, nkv),
        in_specs=[
            pl.BlockSpec((None, None, n_chunks, width), lambda i, j: (i, j, 0, 0)),
            pl.BlockSpec((2, width), lambda i, j: (0, 0)),
            pl.BlockSpec((None, 2 * width, CMP_HIDDEN), lambda i, j: (j // N_KV, 0, 0)),
            pl.BlockSpec((None, CMP_HIDDEN, HEAD_DIM), lambda i, j: (j // N_KV, 0, 0)),
        ],
        out_specs=pl.BlockSpec((None, None, n_chunks, HEAD_DIM), lambda i, j: (i, j, 0, 0)),
        out_shape=jax.ShapeDtypeStruct((b, nkv, n_chunks, HEAD_DIM), BF16),
        compiler_params=_params(2),
        name="compress",
    )(xc, pe2, w1, w2)


TAB_DIAG, TAB_SUB, TAB_FAR = 0, 1, 2
BAND_BELOW = 8


def _nsa_kernel(q_ref, kc_ref, vct_ref, ks_ref, vst_ref, kw_ref, vwt_ref, gate_ref, band_ref, tab_ref,
                ovl_ref, et_ref, o_ref, kaug_ref, sc_ref, m_ref, l_ref, acc_ref, *, tq, n_sel):
    i = pl.program_id(2)
    tk = tq
    mcols = HPG * tq
    nw = WINDOW // tk
    cpt = tq // CMP_STRIDE
    band_rows = BAND_BELOW + cpt

    @pl.when(i == 0)
    def _():
        kaug_ref[:, 0:HEAD_DIM] = ks_ref[...]
        kaug_ref[:, HEAD_DIM:2 * HEAD_DIM] = et_ref[...]

    q = q_ref[...]
    qs = jnp.concatenate([q[:, h * HEAD_DIM:(h + 1) * HEAD_DIM] for h in range(HPG)], axis=0)

    raw = lax.dot_general(kc_ref[...], qs, _DN_T, preferred_element_type=F32)
    crow = lax.broadcasted_iota(jnp.int32, raw.shape, 0)
    sc_ref[...] = jnp.where(crow < cpt * (i + 1), raw, NEG)
    band = pl.ds(pl.multiple_of(jnp.maximum(cpt * i - BAND_BELOW, 0), 8), band_rows)
    sc_ref[band, :] += band_ref[jnp.minimum(i, 1)]
    sc = sc_ref[...]
    mc = jnp.max(sc, axis=0, keepdims=True)
    pc = jnp.where(sc > HALF_NEG, jnp.exp2(sc - mc), 0.0)
    lc = jnp.sum(pc, axis=0, keepdims=True)
    pc = pc * jnp.where(lc > 0.0, 1.0 / lc, 0.0)
    o_c = jnp.dot(vct_ref[...], pc.astype(BF16), preferred_element_type=F32)

    ps = pc[:, 0:tq] + pc[:, tq:2 * tq] + pc[:, 2 * tq:3 * tq] + pc[:, 3 * tq:4 * tq]
    hi = ps.astype(BF16)
    r1 = ps - hi.astype(F32)
    mid = r1.astype(BF16)
    lo = (r1 - mid.astype(F32)).astype(BF16)
    ovl = ovl_ref[...]
    imp = (jnp.dot(ovl, hi, preferred_element_type=F32) + jnp.dot(ovl, mid, preferred_element_type=F32)
           + jnp.dot(ovl, lo, preferred_element_type=F32))
    jj = lax.broadcasted_iota(jnp.int32, (n_sel, tq), 0)
    tt = i * tq + lax.broadcasted_iota(jnp.int32, (n_sel, tq), 1)
    cur = tt >> int(math.log2(SEL_LEN))
    forced = (jj == 0) | (jj == cur) | (jj == cur - 1)
    imp = jnp.where(forced, FORCE, imp)
    imp = jnp.where(jj * SEL_LEN <= tt, imp, NEG)
    rank = jnp.zeros((n_sel, tq), jnp.int32)
    for b in range(n_sel):
        row = imp[b:b + 1, :]
        tie = jnp.where(row == imp, jnp.where(jj > b, 1, 0), 0)
        rank = rank + jnp.where(row > imp, 1, tie)
    selb_t = jnp.where(rank < SEL_TOPK, 0.0, NEG)
    selb = jnp.concatenate([selb_t, jnp.zeros((LANE - n_sel, tq), F32)], axis=0).T
    selb = selb.astype(BF16)
    qa = jnp.concatenate([qs, jnp.concatenate([selb] * HPG, axis=0)], axis=1)

    def init():
        m_ref[...] = jnp.full((1, mcols), NEG, F32)
        l_ref[...] = jnp.zeros((1, mcols), F32)
        acc_ref[...] = jnp.zeros((HEAD_DIM, mcols), F32)

    def step(qmat, k_ref, vt_ref, kt, kind):
        keys = pl.ds(pl.multiple_of(kt * tk, tk), tk)
        s = lax.dot_general(k_ref[keys, :], qmat, _DN_T, preferred_element_type=F32)
        if kind is not None:
            s = s + tab_ref[kind]
        m_prev = m_ref[...]
        m_next = jnp.maximum(m_prev, jnp.max(s, axis=0, keepdims=True))
        alpha = jnp.exp2(m_prev - m_next)
        p = jnp.exp2(s - m_next)
        l_ref[...] = alpha * l_ref[...] + jnp.sum(p, axis=0, keepdims=True)
        acc_ref[...] = alpha * acc_ref[...] + jnp.dot(vt_ref[:, keys], p.astype(BF16),
                                                      preferred_element_type=F32)
        m_ref[...] = m_next

    init()
    n_far = jnp.maximum(i - 1, 0)
    odd = n_far % 2

    @pl.when(odd == 1)
    def _():
        step(qa, kaug_ref, vst_ref, 0, None)

    def far_pair(p, carry):
        step(qa, kaug_ref, vst_ref, odd + 2 * p, None)
        step(qa, kaug_ref, vst_ref, odd + 2 * p + 1, None)
        return carry

    lax.fori_loop(0, n_far // 2, far_pair, 0)

    @pl.when(i >= 1)
    def _():
        step(qa, kaug_ref, vst_ref, i - 1, TAB_SUB)
        step(qa, kaug_ref, vst_ref, i, TAB_DIAG)

    @pl.when(i == 0)
    def _():
        step(qa, kaug_ref, vst_ref, i, TAB_DIAG)

    o_s = acc_ref[...] * (1.0 / l_ref[...])

    init()
    for avail in range(nw + 1):

        @pl.when((i == avail) if avail < nw else (i >= nw))
        def _(avail=avail):
            for d in range(avail, -1, -1):
                kind = TAB_DIAG if d == 0 else TAB_SUB if d == 1 else TAB_FAR if d == nw else None
                step(qs, kw_ref, vwt_ref, i - d, kind)

    o_w = acc_ref[...] * (1.0 / l_ref[...])

    gt = gate_ref[...].T
    for h in range(HPG):
        sl = slice(h * tq, (h + 1) * tq)
        o = gt[3 * h:3 * h + 1, :] * o_c[:, sl]
        o = o + gt[3 * h + 1:3 * h + 2, :] * o_s[:, sl]
        o = o + gt[3 * h + 2:3 * h + 3, :] * o_w[:, sl]
        o_ref[:, h * HEAD_DIM:(h + 1) * HEAD_DIM] = o.T.astype(BF16)


def _nsa(qkv, cmp_k, cmp_vt, vs_t, vw_t, gates, band_t, tabs_t, ovl, e_t, batch, seq_len, *, tq):
    n_t = seq_len // tq
    n_sel = seq_len // SEL_LEN
    n_chunks = cmp_k.shape[2]
    mcols = HPG * tq
    band_rows = band_t.shape[2]
    qw = HPG * HEAD_DIM
    col_ks = (ATTN_WIDTH + 2 * KV_WIDTH) // HEAD_DIM
    col_kw = (ATTN_WIDTH + 4 * KV_WIDTH) // HEAD_DIM

    vt_spec = pl.BlockSpec((None, None, HEAD_DIM, seq_len), lambda b, g, i: (b, g, 0, 0))
    kern = functools.partial(_nsa_kernel, tq=tq, n_sel=n_sel)
    return pl.pallas_call(
        kern,
        grid=(batch, N_KV, n_t),
        in_specs=[
            pl.BlockSpec((tq, qw), lambda b, g, i: (b * n_t + i, g)),
            pl.BlockSpec((None, None, n_chunks, HEAD_DIM), lambda b, g, i: (b, g, 0, 0)),
            pl.BlockSpec((None, None, HEAD_DIM, n_chunks), lambda b, g, i: (b, g, 0, 0)),
            pl.BlockSpec((seq_len, HEAD_DIM), lambda b, g, i: (b, col_ks + g)),
            vt_spec,
            pl.BlockSpec((seq_len, HEAD_DIM), lambda b, g, i: (b, col_kw + g)),
            vt_spec,
            pl.BlockSpec((tq, LANE), lambda b, g, i: (b * n_t + i, g)),
            pl.BlockSpec((None, 2, band_rows, mcols), lambda b, g, i: (g, 0, 0, 0)),
            pl.BlockSpec((None, 3, tq, mcols), lambda b, g, i: (g, 0, 0, 0)),
            pl.BlockSpec((n_sel, n_chunks), lambda b, g, i: (0, 0)),
            pl.BlockSpec((seq_len, LANE), lambda b, g, i: (0, 0)),
        ],
        out_specs=pl.BlockSpec((tq, qw), lambda b, g, i: (b * n_t + i, g)),
        out_shape=jax.ShapeDtypeStruct((batch * seq_len, ATTN_WIDTH), BF16),
        scratch_shapes=[
            pltpu.VMEM((seq_len, 2 * HEAD_DIM), BF16),
            pltpu.VMEM((n_chunks, mcols), F32),
            pltpu.VMEM((1, mcols), F32),
            pltpu.VMEM((1, mcols), F32),
            pltpu.VMEM((HEAD_DIM, mcols), F32),
        ],
        compiler_params=_params(3),
        name="nsa_attention",
    )(qkv, cmp_k, cmp_vt, qkv, vs_t, qkv, vw_t, gates, band_t, tabs_t, ovl, e_t)


def _oproj_kernel(oa_ref, ov_ref, wo_ref, x_ref, g1_ref, g2_ref, x1_ref, h2_ref):
    ka = oa_ref.shape[1]
    mix = jnp.dot(oa_ref[...], wo_ref[0:ka, :], preferred_element_type=F32)
    mix = mix + jnp.dot(ov_ref[...], wo_ref[ka:, :], preferred_element_type=F32)
    x1 = x_ref[...] + _rms(mix, g1_ref[...])
    x1_ref[...] = x1
    h2_ref[...] = _rms(x1, g2_ref[...]).astype(BF16)


def _oproj(o_attn, o_conv, w_o, x2, g_post, g_pre, *, tm=512):
    m = x2.shape[0]
    ka, kv = o_attn.shape[1], o_conv.shape[1]
    row = lambda i: (i, 0)
    fixed = lambda i: (0, 0)
    return pl.pallas_call(
        _oproj_kernel,
        grid=(m // tm,),
        in_specs=[
            pl.BlockSpec((tm, ka), row),
            pl.BlockSpec((tm, kv), row),
            pl.BlockSpec((ka + kv, D_MODEL), fixed),
            pl.BlockSpec((tm, D_MODEL), row),
            pl.BlockSpec((1, D_MODEL), fixed),
            pl.BlockSpec((1, D_MODEL), fixed),
        ],
        out_specs=[pl.BlockSpec((tm, D_MODEL), row), pl.BlockSpec((tm, D_MODEL), row)],
        out_shape=[jax.ShapeDtypeStruct((m, D_MODEL), F32), jax.ShapeDtypeStruct((m, D_MODEL), BF16)],
        compiler_params=_params(1),
        name="out_proj",
    )(o_attn, o_conv, w_o, x2, g_post, g_pre)


def _ffn_kernel(h_ref, wu_ref, wd_ref, x1_ref, g_ref, o_ref, acc_ref):
    j = pl.program_id(1)

    @pl.when(j == 0)
    def _():
        acc_ref[...] = jnp.zeros(acc_ref.shape, F32)

    a = jnp.dot(h_ref[...], wu_ref[...], preferred_element_type=F32)
    a = jnp.square(jnp.maximum(a, 0.0)).astype(BF16)
    acc_ref[...] += jnp.dot(a, wd_ref[...], preferred_element_type=F32)

    @pl.when(j == pl.num_programs(1) - 1)
    def _():
        o_ref[...] = x1_ref[...] + _rms(acc_ref[...], g_ref[...])


def _ffn(h2, w_up, w_down, x1, g_post, *, tm=512, tf=1024):
    m = h2.shape[0]
    d_ff = w_up.shape[1]
    return pl.pallas_call(
        _ffn_kernel,
        grid=(m // tm, d_ff // tf),
        in_specs=[
            pl.BlockSpec((tm, D_MODEL), lambda i, j: (i, 0)),
            pl.BlockSpec((D_MODEL, tf), lambda i, j: (0, j)),
            pl.BlockSpec((tf, D_MODEL), lambda i, j: (j, 0)),
            pl.BlockSpec((tm, D_MODEL), lambda i, j: (i, 0)),
            pl.BlockSpec((1, D_MODEL), lambda i, j: (0, 0)),
        ],
        out_specs=pl.BlockSpec((tm, D_MODEL), lambda i, j: (i, 0)),
        out_shape=jax.ShapeDtypeStruct((m, D_MODEL), F32),
        scratch_shapes=[pltpu.VMEM((tm, D_MODEL), F32)],
        compiler_params=_params(2),
        name="ffn",
    )(h2, w_up, w_down, x1, g_post)


def _bucket_np(dist):
    n = np.maximum(dist, 0)
    max_exact = N_BUCKETS // 2
    nf = np.maximum(n, 1).astype(np.float32)
    large = max_exact + (np.log(nf / np.float32(max_exact)) / np.float32(math.log(MAX_DIST / max_exact))
                         * np.float32(N_BUCKETS - max_exact)).astype(np.int32)
    large = np.minimum(large, N_BUCKETS - 1)
    return np.where(n < max_exact, n, large)


def _bucket_starts():
    b = _bucket_np(np.arange(4 * MAX_DIST))
    return [int(np.argmax(b == k)) for k in range(N_BUCKETS)]


def _bias_by_distance(rel_bias, dist):
    starts = _bucket_starts()
    d = jnp.asarray(dist, jnp.int32)[None]
    col = lambda k: rel_bias[:, k].reshape((N_HEADS,) + (1,) * dist.ndim)
    out = jnp.broadcast_to(col(0), (N_HEADS,) + dist.shape)
    for k in range(1, N_BUCKETS):
        out = jnp.where(d >= starts[k], col(k), out)
    return out


def _attention_tables(rel_bias, seq_len, tq):
    starts = _bucket_starts()
    assert starts[N_BUCKETS - 1] <= CMP_STRIDE * (BAND_BELOW + 1) - (CMP_LEN - 1)
    assert starts[N_BUCKETS - 1] <= tq
    rel = (rel_bias - rel_bias[:, N_BUCKETS - 1:]) * math.log2(math.e)

    def per_group(tab):
        lead = tab.shape[1:-2]
        nk, nq = tab.shape[-2:]
        tab = tab.reshape((N_KV, HPG) + lead + (nk, nq))
        tab = jnp.moveaxis(tab, 1, -2)
        return tab.reshape((N_KV,) + lead + (nk, HPG * nq)).astype(F32)

    c = np.arange(tq)[:, None]
    r = np.arange(tq)[None, :]
    diag = jnp.where(jnp.asarray(r >= c)[None], _bias_by_distance(rel, np.maximum(r - c, 0)), NEG)
    sub = _bias_by_distance(rel, tq + r - c)
    far = jnp.broadcast_to(jnp.where(jnp.asarray(r < c)[None], 0.0, NEG), (N_HEADS, tq, tq))
    tabs = per_group(jnp.stack([diag, sub, far], axis=1))

    cpt = tq // CMP_STRIDE
    cl = np.arange(BAND_BELOW + cpt)[:, None]
    bands = []
    for first in (0, -BAND_BELOW):
        dist = r - ((cl + first) * CMP_STRIDE + CMP_LEN - 1)
        bands.append(jnp.where(jnp.asarray(dist >= 0)[None], _bias_by_distance(rel, np.maximum(dist, 0)), NEG))
    band = per_group(jnp.stack(bands, axis=1))

    n_chunks = seq_len // CMP_STRIDE
    n_cmp = (seq_len - CMP_LEN) // CMP_STRIDE + 1
    n_sel = seq_len // SEL_LEN
    ci = np.arange(n_chunks)[None, :] * CMP_STRIDE
    sj = np.arange(n_sel)[:, None] * SEL_LEN
    ovl = ((ci < sj + SEL_LEN) & (ci + CMP_LEN > sj) & (np.arange(n_chunks)[None, :] < n_cmp))
    e_t = (np.arange(seq_len)[:, None] // SEL_LEN == np.arange(LANE)[None, :])
    return tabs, band, jnp.asarray(ovl, BF16), jnp.asarray(e_t, BF16)


def kernel(x, w_in, pe_cmp, w_cmp_k1, w_cmp_k2, w_cmp_v1, w_cmp_v2, conv_w, rel_bias, w_o, w_up, w_down,
           g_pre_mix, g_post_mix, g_pre_ffn, g_post_ffn):
    batch, seq_len, _ = x.shape
    depth = w_in.shape[0]
    tq = 256
    n_chunks = seq_len // CMP_STRIDE
    half = CMP_STRIDE * HEAD_DIM
    tabs_t, band_t, ovl, e_t = _attention_tables(rel_bias, seq_len, tq)
    x2 = x.reshape(batch * seq_len, D_MODEL)
    for l in range(depth):
        wl = w_in[l]
        w_qkv = wl[:, :QKV_WIDTH].astype(BF16)
        wg = wl[:, GATE_OFF:CONV_OFF].reshape(D_MODEL, N_KV, HPG * N_BRANCH)
        wg = jnp.pad(wg, ((0, 0), (0, 0), (0, LANE - HPG * N_BRANCH))).reshape(D_MODEL, N_KV * LANE).astype(BF16)
        w_h, w_b, w_c = (wl[:, CONV_OFF + k * CONV_WIDTH:CONV_OFF + (k + 1) * CONV_WIDTH].astype(BF16)
                         for k in range(3))
        g1 = g_pre_mix[l].reshape(1, D_MODEL)

        qkv, gates = _qkv_proj(x2, g1, w_qkv, wg)
        o_conv = _conv_proj(x2, g1, w_h, w_b, w_c, conv_w[l], seq_len)

        xc = qkv[:, ATTN_WIDTH:ATTN_WIDTH + 2 * KV_WIDTH]
        xc = xc.reshape(batch, n_chunks, CMP_STRIDE, 2 * N_KV, HEAD_DIM).transpose(0, 3, 1, 2, 4)
        xc = xc.reshape(batch, 2 * N_KV, n_chunks, half)
        w1 = jnp.stack([w_cmp_k1[l], w_cmp_v1[l]]).reshape(2, 2 * half, CMP_HIDDEN).astype(BF16)
        w2 = jnp.stack([w_cmp_k2[l], w_cmp_v2[l]]).astype(BF16)
        cmp_kv = _compress(xc, pe_cmp[l].reshape(2, half), w1, w2)
        cmp_k = cmp_kv[:, :N_KV]
        cmp_vt = cmp_kv[:, N_KV:].transpose(0, 1, 3, 2)

        def v_t(which):
            off = ATTN_WIDTH + which * KV_WIDTH
            v = qkv[:, off:off + KV_WIDTH].reshape(batch, seq_len, N_KV, HEAD_DIM)
            return v.transpose(0, 2, 3, 1)

        o_attn = _nsa(qkv, cmp_k, cmp_vt, v_t(3), v_t(5), gates, band_t, tabs_t, ovl, e_t,
                      batch, seq_len, tq=tq)

        x1, h2 = _oproj(o_attn, o_conv, w_o[l].astype(BF16), x2,
                        g_post_mix[l].reshape(1, D_MODEL), g_pre_ffn[l].reshape(1, D_MODEL))
        x2 = _ffn(h2, w_up[l].astype(BF16), w_down[l].astype(BF16), x1, g_post_ffn[l].reshape(1, D_MODEL))
    return x2.reshape(batch, seq_len, D_MODEL)
```

```python
import functools
import math

import numpy as np
import jax
import jax.numpy as jnp
from jax import lax
from jax.experimental import pallas as pl
from jax.experimental.pallas import tpu as pltpu

F32 = jnp.float32
BF16 = jnp.bfloat16

D_MODEL = 2048
N_HEADS = 8
N_KV = 2
HPG = N_HEADS // N_KV
HEAD_DIM = 128
ATTN_WIDTH = N_HEADS * HEAD_DIM
KV_WIDTH = N_KV * HEAD_DIM
CONV_WIDTH = D_MODEL - ATTN_WIDTH
CONV_K = 3
N_BRANCH = 3
CMP_LEN = 32
CMP_STRIDE = 16
CMP_HIDDEN = 256
SEL_LEN = 64
SEL_TOPK = 16
WINDOW = 512
N_BUCKETS = 32
MAX_DIST = 128
EPS = 1e-6
NEG = -1e30
HALF_NEG = -5e29
FORCE = 1e9

QKV_WIDTH = ATTN_WIDTH + 6 * KV_WIDTH
GATE_OFF = QKV_WIDTH
CONV_OFF = QKV_WIDTH + N_HEADS * N_BRANCH
LANE = 128
VMEM_LIMIT = 56 * 1024 * 1024

_DN_T = (((1,), (1,)), ((), ()))


def _rms(x, g):
    ms = jnp.mean(x * x, axis=-1, keepdims=True)
    return x * lax.rsqrt(ms + EPS) * g


def _params(n_axes):
    return pltpu.CompilerParams(dimension_semantics=("arbitrary",) * n_axes, vmem_limit_bytes=VMEM_LIMIT)


def _qkv_kernel(x_ref, g_ref, w_ref, wg_ref, cs_ref, o_ref, gate_ref, h_ref):
    j = pl.program_id(1)

    @pl.when(j == 0)
    def _():
        hb = _rms(x_ref[...], g_ref[...]).astype(BF16)
        h_ref[...] = hb
        gate_ref[...] = jax.nn.sigmoid(jnp.dot(hb, wg_ref[...], preferred_element_type=F32))

    acc = jnp.dot(h_ref[...], w_ref[...], preferred_element_type=F32)
    o_ref[...] = (acc * cs_ref[...]).astype(BF16)


def _qkv_proj(x2, g, w_qkv, w_gate, *, tm=1024, tn=1280):
    m = x2.shape[0]
    n = w_qkv.shape[1]
    gw = w_gate.shape[1]
    q_scale = HEAD_DIM ** -0.5 * math.log2(math.e)
    col_scale = jnp.where(jnp.arange(n) < ATTN_WIDTH, q_scale, 1.0).astype(F32).reshape(1, n)
    return pl.pallas_call(
        _qkv_kernel,
        grid=(m // tm, n // tn),
        in_specs=[
            pl.BlockSpec((tm, D_MODEL), lambda i, j: (i, 0)),
            pl.BlockSpec((1, D_MODEL), lambda i, j: (0, 0)),
            pl.BlockSpec((D_MODEL, tn), lambda i, j: (0, j)),
            pl.BlockSpec((D_MODEL, gw), lambda i, j: (0, 0)),
            pl.BlockSpec((1, tn), lambda i, j: (0, j)),
        ],
        out_specs=[
            pl.BlockSpec((tm, tn), lambda i, j: (i, j)),
            pl.BlockSpec((tm, gw), lambda i, j: (i, 0)),
        ],
        out_shape=[
            jax.ShapeDtypeStruct((m, n), BF16),
            jax.ShapeDtypeStruct((m, gw), F32),
        ],
        scratch_shapes=[pltpu.VMEM((tm, D_MODEL), BF16)],
        compiler_params=_params(2),
        name="qkv_proj",
    )(x2, g, w_qkv, w_gate, col_scale)


def _conv_kernel(x_ref, g_ref, wh_ref, wb_ref, wc_ref, cw_ref, o_ref, h_ref, carry_ref, *, tiles_per_seq):
    i = pl.program_id(0)
    j = pl.program_id(1)

    @pl.when(j == 0)
    def _():
        h_ref[...] = _rms(x_ref[...], g_ref[...]).astype(BF16)

    hb = h_ref[...]
    ch = jnp.dot(hb, wh_ref[...], preferred_element_type=F32)
    cb = jnp.dot(hb, wb_ref[...], preferred_element_type=F32)
    cc = jnp.dot(hb, wc_ref[...], preferred_element_type=F32)
    u = cc * ch
    tm = u.shape[0]
    prev = carry_ref[j]
    prev = jnp.where(i % tiles_per_seq == 0, 0.0, prev)
    carry_ref[j] = u[tm - 8:tm, :]
    row = lax.broadcasted_iota(jnp.int32, u.shape, 0)
    u1 = jnp.where(row == 0, prev[7:8, :], pltpu.roll(u, 1, axis=0))
    u2 = jnp.where(row == 0, prev[6:7, :], jnp.where(row == 1, prev[7:8, :], pltpu.roll(u, 2, axis=0)))
    w = cw_ref[...]
    y = w[0:1, :] * u2
    y = y + w[1:2, :] * u1
    y = y + w[2:3, :] * u
    o_ref[...] = (cb * y).astype(BF16)


def _conv_proj(x2, g, w_h, w_b, w_c, conv_w, seq_len, *, tm=1024, tc=512):
    m = x2.shape[0]
    c = w_h.shape[1]
    kern = functools.partial(_conv_kernel, tiles_per_seq=seq_len // tm)
    wspec = pl.BlockSpec((D_MODEL, tc), lambda i, j: (0, j))
    return pl.pallas_call(
        kern,
        grid=(m // tm, c // tc),
        in_specs=[
            pl.BlockSpec((tm, D_MODEL), lambda i, j: (i, 0)),
            pl.BlockSpec((1, D_MODEL), lambda i, j: (0, 0)),
            wspec, wspec, wspec,
            pl.BlockSpec((CONV_K, tc), lambda i, j: (0, j)),
        ],
        out_specs=pl.BlockSpec((tm, tc), lambda i, j: (i, j)),
        out_shape=jax.ShapeDtypeStruct((m, c), BF16),
        scratch_shapes=[pltpu.VMEM((tm, D_MODEL), BF16), pltpu.VMEM((c // tc, 8, tc), F32)],
        compiler_params=_params(2),
        name="conv_proj",
    )(x2, g, w_h, w_b, w_c, conv_w)


def _compress_kernel(x_ref, pe_ref, w1_ref, w2_ref, o_ref):
    half = CMP_STRIDE * HEAD_DIM
    x = x_ref[...].astype(F32)
    pe = pe_ref[...]
    xa = (x + pe[0:1, :]).astype(BF16)
    xb = (x + pe[1:2, :]).astype(BF16)
    a = jnp.dot(xa, w1_ref[0:half, :], preferred_element_type=F32)
    b = jnp.dot(xb, w1_ref[half:2 * half, :], preferred_element_type=F32)
    n = a.shape[0]
    pre = a + pltpu.roll(b, n - 1, axis=0)
    hid = pre * jax.nn.sigmoid(pre)
    out = jnp.dot(hid.astype(BF16), w2_ref[...], preferred_element_type=F32)
    row = lax.broadcasted_iota(jnp.int32, out.shape, 0)
    o_ref[...] = jnp.where(row < n - 1, out, 0.0).astype(BF16)


def _compress(xc, pe2, w1, w2):
    b, nkv, n_chunks, width = xc.shape
    return pl.pallas_call(
        _compress_kernel,
        grid=(b, nkv),
        in_specs=[
            pl.BlockSpec((None, None, n_chunks, width), lambda i, j: (i, j, 0, 0)),
            pl.BlockSpec((2, width), lambda i, j: (0, 0)),
            pl.BlockSpec((None, 2 * width, CMP_HIDDEN), lambda i, j: (j // N_KV, 0, 0)),
            pl.BlockSpec((None, CMP_HIDDEN, HEAD_DIM), lambda i, j: (j // N_KV, 0, 0)),
        ],
        out_specs=pl.BlockSpec((None, None, n_chunks, HEAD_DIM), lambda i, j: (i, j, 0, 0)),
        out_shape=jax.ShapeDtypeStruct((b, nkv, n_chunks, HEAD_DIM), BF16),
        compiler_params=_params(2),
        name="compress",
    )(xc, pe2, w1, w2)


TAB_DIAG, TAB_SUB, TAB_FAR = 0, 1, 2
BAND_BELOW = 8
ONES_ROWS = 16


def _nsa_kernel(q_ref, kc_ref, vct_ref, ks_ref, vst_ref, kw_ref, vwt_ref, gate_ref, band_ref, tab_ref,
                ovl_ref, et_ref, o_ref, kaug_ref, sc_ref, s0_ref, s1_ref, m_ref, acc_ref, *, tq, n_sel):
    i = pl.program_id(2)
    tk = tq
    mcols = HPG * tq
    nw = WINDOW // tk
    cpt = tq // CMP_STRIDE
    band_rows = BAND_BELOW + cpt

    @pl.when(i == 0)
    def _():
        kaug_ref[:, 0:HEAD_DIM] = ks_ref[...]
        kaug_ref[:, HEAD_DIM:2 * HEAD_DIM] = et_ref[...]

    q = q_ref[...]
    qs = jnp.concatenate([q[:, h * HEAD_DIM:(h + 1) * HEAD_DIM] for h in range(HPG)], axis=0)

    raw = lax.dot_general(kc_ref[...], qs, _DN_T, preferred_element_type=F32)
    crow = lax.broadcasted_iota(jnp.int32, raw.shape, 0)
    sc_ref[...] = jnp.where(crow < cpt * (i + 1), raw, NEG)
    band = pl.ds(pl.multiple_of(jnp.maximum(cpt * i - BAND_BELOW, 0), 8), band_rows)
    sc_ref[band, :] += band_ref[jnp.minimum(i, 1)]
    sc = sc_ref[...]
    mc = jnp.max(sc, axis=0, keepdims=True)
    pc = jnp.where(sc > HALF_NEG, jnp.exp2(sc - mc), 0.0)
    lc = jnp.sum(pc, axis=0, keepdims=True)
    pc = pc * jnp.where(lc > 0.0, 1.0 / lc, 0.0)
    o_c = jnp.dot(vct_ref[...], pc.astype(BF16), preferred_element_type=F32)

    ps = pc[:, 0:tq] + pc[:, tq:2 * tq] + pc[:, 2 * tq:3 * tq] + pc[:, 3 * tq:4 * tq]
    hi = ps.astype(BF16)
    r1 = ps - hi.astype(F32)
    mid = r1.astype(BF16)
    lo = (r1 - mid.astype(F32)).astype(BF16)
    ovl = ovl_ref[...]
    imp = (jnp.dot(ovl, hi, preferred_element_type=F32) + jnp.dot(ovl, mid, preferred_element_type=F32)
           + jnp.dot(ovl, lo, preferred_element_type=F32))
    jj = lax.broadcasted_iota(jnp.int32, (n_sel, tq), 0)
    tt = i * tq + lax.broadcasted_iota(jnp.int32, (n_sel, tq), 1)
    cur = tt >> int(math.log2(SEL_LEN))
    forced = (jj == 0) | (jj == cur) | (jj == cur - 1)
    imp = jnp.where(forced, FORCE, imp)
    imp = jnp.where(jj * SEL_LEN <= tt, imp, NEG)
    sub = 8
    ranks = []
    for j0 in range(0, n_sel, sub):
        blk = imp[j0:j0 + sub, :]
        jl = j0 + lax.broadcasted_iota(jnp.int32, blk.shape, 0)
        cnt = jnp.zeros(blk.shape, jnp.int32)
        for b in range(n_sel):
            row = imp[b:b + 1, :]
            if b < j0:
                cnt = cnt + jnp.where(row >= blk, 1, 0)
            elif b >= j0 + sub:
                cnt = cnt + jnp.where(row > blk, 1, 0)
            else:
                cnt = cnt + jnp.where(row > blk, 1, jnp.where(row == blk, jnp.where(jl > b, 1, 0), 0))
        ranks.append(cnt)
    rank = jnp.concatenate(ranks, axis=0)
    selb_t = jnp.where(rank < SEL_TOPK, 0.0, NEG)
    selb = jnp.concatenate([selb_t, jnp.zeros((LANE - n_sel, tq), F32)], axis=0).T
    selb = selb.astype(BF16)
    qa = jnp.concatenate([qs, jnp.concatenate([selb] * HPG, axis=0)], axis=1)

    s_refs = (s0_ref, s1_ref)

    def init():
        m_ref[...] = jnp.full((1, mcols), NEG, F32)
        acc_ref[...] = jnp.zeros(acc_ref.shape, F32)

    def keys(kt):
        return pl.ds(pl.multiple_of(kt * tk, tk), tk)

    def qk(qmat, k_ref, kt, kind, buf):
        s = lax.dot_general(k_ref[keys(kt), :], qmat, _DN_T, preferred_element_type=F32)
        if kind is not None:
            s = s + tab_ref[kind]
        s_refs[buf][...] = s
        return jnp.max(s, axis=0, keepdims=True)

    def process(vt_ref, kt, buf, m_tile):
        m_prev = m_ref[...]
        m_next = jnp.maximum(m_prev, m_tile)
        alpha = jnp.exp2(m_prev - m_next)
        p = jnp.exp2(s_refs[buf][...] - m_next).astype(BF16)
        acc_ref[...] = alpha * acc_ref[...] + jnp.dot(vt_ref[:, keys(kt)], p, preferred_element_type=F32)
        m_ref[...] = m_next

    def finish():
        return acc_ref[0:HEAD_DIM, :] * (1.0 / acc_ref[HEAD_DIM:HEAD_DIM + 1, :])

    init()

    @pl.when(i == 0)
    def _():
        process(vst_ref, 0, 0, qk(qa, kaug_ref, 0, TAB_DIAG, 0))

    @pl.when(i >= 1)
    def _():
        mt0 = qk(qa, kaug_ref, i, TAB_DIAG, 0)
        mt1 = qk(qa, kaug_ref, i - 1, TAB_SUB, 1)
        process(vst_ref, i, 0, mt0)
        mt0 = qk(qa, kaug_ref, jnp.maximum(i - 2, 0), None, 0)
        process(vst_ref, i - 1, 1, mt1)
        n_far = i - 1

        def far_pair(p, mt0):
            a = i - 2 - 2 * p
            mt1 = qk(qa, kaug_ref, a - 1, None, 1)
            process(vst_ref, a, 0, mt0)
            mt0 = qk(qa, kaug_ref, jnp.maximum(a - 2, 0), None, 0)
            process(vst_ref, a - 1, 1, mt1)
            return mt0

        mt0 = lax.fori_loop(0, n_far // 2, far_pair, mt0)

        @pl.when(n_far % 2 == 1)
        def _():
            process(vst_ref, 0, 0, mt0)

    o_s = finish()

    init()
    for avail in range(nw + 1):

        @pl.when((i == avail) if avail < nw else (i >= nw))
        def _(avail=avail):
            kinds = [TAB_DIAG if d == 0 else TAB_SUB if d == 1 else TAB_FAR if d == nw else None
                     for d in range(avail + 1)]
            mt = qk(qs, kw_ref, i, kinds[0], 0)
            for d in range(avail + 1):
                mt_next = qk(qs, kw_ref, i - d - 1, kinds[d + 1], (d + 1) % 2) if d < avail else None
                process(vwt_ref, i - d, d % 2, mt)
                mt = mt_next

    o_w = finish()

    gt = gate_ref[...].T
    for h in range(HPG):
        sl = slice(h * tq, (h + 1) * tq)
        o = gt[3 * h:3 * h + 1, :] * o_c[:, sl]
        o = o + gt[3 * h + 1:3 * h + 2, :] * o_s[:, sl]
        o = o + gt[3 * h + 2:3 * h + 3, :] * o_w[:, sl]
        o_ref[:, h * HEAD_DIM:(h + 1) * HEAD_DIM] = o.T.astype(BF16)


def _nsa(qkv, cmp_k, cmp_vt, vs_t, vw_t, gates, band_t, tabs_t, ovl, e_t, batch, seq_len, *, tq):
    n_t = seq_len // tq
    n_sel = seq_len // SEL_LEN
    n_chunks = cmp_k.shape[2]
    mcols = HPG * tq
    band_rows = band_t.shape[2]
    qw = HPG * HEAD_DIM
    col_ks = (ATTN_WIDTH + 2 * KV_WIDTH) // HEAD_DIM
    col_kw = (ATTN_WIDTH + 4 * KV_WIDTH) // HEAD_DIM

    vt_spec = pl.BlockSpec((None, None, HEAD_DIM + ONES_ROWS, seq_len), lambda b, g, i: (b, g, 0, 0))
    kern = functools.partial(_nsa_kernel, tq=tq, n_sel=n_sel)
    return pl.pallas_call(
        kern,
        grid=(batch, N_KV, n_t),
        in_specs=[
            pl.BlockSpec((tq, qw), lambda b, g, i: (b * n_t + i, g)),
            pl.BlockSpec((None, None, n_chunks, HEAD_DIM), lambda b, g, i: (b, g, 0, 0)),
            pl.BlockSpec((None, None, HEAD_DIM, n_chunks), lambda b, g, i: (b, g, 0, 0)),
            pl.BlockSpec((seq_len, HEAD_DIM), lambda b, g, i: (b, col_ks + g)),
            vt_spec,
            pl.BlockSpec((seq_len, HEAD_DIM), lambda b, g, i: (b, col_kw + g)),
            vt_spec,
            pl.BlockSpec((tq, LANE), lambda b, g, i: (b * n_t + i, g)),
            pl.BlockSpec((None, 2, band_rows, mcols), lambda b, g, i: (g, 0, 0, 0)),
            pl.BlockSpec((None, 3, tq, mcols), lambda b, g, i: (g, 0, 0, 0)),
            pl.BlockSpec((n_sel, n_chunks), lambda b, g, i: (0, 0)),
            pl.BlockSpec((seq_len, LANE), lambda b, g, i: (0, 0)),
        ],
        out_specs=pl.BlockSpec((tq, qw), lambda b, g, i: (b * n_t + i, g)),
        out_shape=jax.ShapeDtypeStruct((batch * seq_len, ATTN_WIDTH), BF16),
        scratch_shapes=[
            pltpu.VMEM((seq_len, 2 * HEAD_DIM), BF16),
            pltpu.VMEM((n_chunks, mcols), F32),
            pltpu.VMEM((tq, mcols), F32),
            pltpu.VMEM((tq, mcols), F32),
            pltpu.VMEM((1, mcols), F32),
            pltpu.VMEM((HEAD_DIM + ONES_ROWS, mcols), F32),
        ],
        compiler_params=_params(3),
        name="nsa_attention",
    )(qkv, cmp_k, cmp_vt, qkv, vs_t, qkv, vw_t, gates, band_t, tabs_t, ovl, e_t)


def _oproj_kernel(oa_ref, ov_ref, wo_ref, x_ref, g1_ref, g2_ref, x1_ref, h2_ref):
    ka = oa_ref.shape[1]
    mix = jnp.dot(oa_ref[...], wo_ref[0:ka, :], preferred_element_type=F32)
    mix = mix + jnp.dot(ov_ref[...], wo_ref[ka:, :], preferred_element_type=F32)
    x1 = x_ref[...] + _rms(mix, g1_ref[...])
    x1_ref[...] = x1
    h2_ref[...] = _rms(x1, g2_ref[...]).astype(BF16)


def _oproj(o_attn, o_conv, w_o, x2, g_post, g_pre, *, tm=512):
    m = x2.shape[0]
    ka, kv = o_attn.shape[1], o_conv.shape[1]
    row = lambda i: (i, 0)
    fixed = lambda i: (0, 0)
    return pl.pallas_call(
        _oproj_kernel,
        grid=(m // tm,),
        in_specs=[
            pl.BlockSpec((tm, ka), row),
            pl.BlockSpec((tm, kv), row),
            pl.BlockSpec((ka + kv, D_MODEL), fixed),
            pl.BlockSpec((tm, D_MODEL), row),
            pl.BlockSpec((1, D_MODEL), fixed),
            pl.BlockSpec((1, D_MODEL), fixed),
        ],
        out_specs=[pl.BlockSpec((tm, D_MODEL), row), pl.BlockSpec((tm, D_MODEL), row)],
        out_shape=[jax.ShapeDtypeStruct((m, D_MODEL), F32), jax.ShapeDtypeStruct((m, D_MODEL), BF16)],
        compiler_params=_params(1),
        name="out_proj",
    )(o_attn, o_conv, w_o, x2, g_post, g_pre)


def _ffn_kernel(h_ref, wu_ref, wd_ref, x1_ref, g_ref, o_ref, acc_ref):
    j = pl.program_id(1)

    @pl.when(j == 0)
    def _():
        acc_ref[...] = jnp.zeros(acc_ref.shape, F32)

    a = jnp.dot(h_ref[...], wu_ref[...], preferred_element_type=F32)
    a = jnp.square(jnp.maximum(a, 0.0)).astype(BF16)
    acc_ref[...] += jnp.dot(a, wd_ref[...], preferred_element_type=F32)

    @pl.when(j == pl.num_programs(1) - 1)
    def _():
        o_ref[...] = x1_ref[...] + _rms(acc_ref[...], g_ref[...])


def _ffn(h2, w_up, w_down, x1, g_post, *, tm=512, tf=1024):
    m = h2.shape[0]
    d_ff = w_up.shape[1]
    return pl.pallas_call(
        _ffn_kernel,
        grid=(m // tm, d_ff // tf),
        in_specs=[
            pl.BlockSpec((tm, D_MODEL), lambda i, j: (i, 0)),
            pl.BlockSpec((D_MODEL, tf), lambda i, j: (0, j)),
            pl.BlockSpec((tf, D_MODEL), lambda i, j: (j, 0)),
            pl.BlockSpec((tm, D_MODEL), lambda i, j: (i, 0)),
            pl.BlockSpec((1, D_MODEL), lambda i, j: (0, 0)),
        ],
        out_specs=pl.BlockSpec((tm, D_MODEL), lambda i, j: (i, 0)),
        out_shape=jax.ShapeDtypeStruct((m, D_MODEL), F32),
        scratch_shapes=[pltpu.VMEM((tm, D_MODEL), F32)],
        compiler_params=_params(2),
        name="ffn",
    )(h2, w_up, w_down, x1, g_post)


def _bucket_np(dist):
    n = np.maximum(dist, 0)
    max_exact = N_BUCKETS // 2
    nf = np.maximum(n, 1).astype(np.float32)
    large = max_exact + (np.log(nf / np.float32(max_exact)) / np.float32(math.log(MAX_DIST / max_exact))
                         * np.float32(N_BUCKETS - max_exact)).astype(np.int32)
    large = np.minimum(large, N_BUCKETS - 1)
    return np.where(n < max_exact, n, large)


def _bucket_starts():
    b = _bucket_np(np.arange(4 * MAX_DIST))
    return [int(np.argmax(b == k)) for k in range(N_BUCKETS)]


def _bias_by_distance(rel_bias, dist):
    starts = _bucket_starts()
    d = jnp.asarray(dist, jnp.int32)[None]
    col = lambda k: rel_bias[:, k].reshape((N_HEADS,) + (1,) * dist.ndim)
    out = jnp.broadcast_to(col(0), (N_HEADS,) + dist.shape)
    for k in range(1, N_BUCKETS):
        out = jnp.where(d >= starts[k], col(k), out)
    return out


def _attention_tables(rel_bias, seq_len, tq):
    starts = _bucket_starts()
    assert starts[N_BUCKETS - 1] <= CMP_STRIDE * (BAND_BELOW + 1) - (CMP_LEN - 1)
    assert starts[N_BUCKETS - 1] <= tq
    rel = (rel_bias - rel_bias[:, N_BUCKETS - 1:]) * math.log2(math.e)

    def per_group(tab):
        lead = tab.shape[1:-2]
        nk, nq = tab.shape[-2:]
        tab = tab.reshape((N_KV, HPG) + lead + (nk, nq))
        tab = jnp.moveaxis(tab, 1, -2)
        return tab.reshape((N_KV,) + lead + (nk, HPG * nq)).astype(F32)

    c = np.arange(tq)[:, None]
    r = np.arange(tq)[None, :]
    diag = jnp.where(jnp.asarray(r >= c)[None], _bias_by_distance(rel, np.maximum(r - c, 0)), NEG)
    sub = _bias_by_distance(rel, tq + r - c)
    far = jnp.broadcast_to(jnp.where(jnp.asarray(r < c)[None], 0.0, NEG), (N_HEADS, tq, tq))
    tabs = per_group(jnp.stack([diag, sub, far], axis=1))

    cpt = tq // CMP_STRIDE
    cl = np.arange(BAND_BELOW + cpt)[:, None]
    bands = []
    for first in (0, -BAND_BELOW):
        dist = r - ((cl + first) * CMP_STRIDE + CMP_LEN - 1)
        bands.append(jnp.where(jnp.asarray(dist >= 0)[None], _bias_by_distance(rel, np.maximum(dist, 0)), NEG))
    band = per_group(jnp.stack(bands, axis=1))

    n_chunks = seq_len // CMP_STRIDE
    n_cmp = (seq_len - CMP_LEN) // CMP_STRIDE + 1
    n_sel = seq_len // SEL_LEN
    ci = np.arange(n_chunks)[None, :] * CMP_STRIDE
    sj = np.arange(n_sel)[:, None] * SEL_LEN
    ovl = ((ci < sj + SEL_LEN) & (ci + CMP_LEN > sj) & (np.arange(n_chunks)[None, :] < n_cmp))
    e_t = (np.arange(seq_len)[:, None] // SEL_LEN == np.arange(LANE)[None, :])
    return tabs, band, jnp.asarray(ovl, BF16), jnp.asarray(e_t, BF16)


def kernel(x, w_in, pe_cmp, w_cmp_k1, w_cmp_k2, w_cmp_v1, w_cmp_v2, conv_w, rel_bias, w_o, w_up, w_down,
           g_pre_mix, g_post_mix, g_pre_ffn, g_post_ffn):
    batch, seq_len, _ = x.shape
    depth = w_in.shape[0]
    tq = 256
    n_chunks = seq_len // CMP_STRIDE
    half = CMP_STRIDE * HEAD_DIM
    tabs_t, band_t, ovl, e_t = _attention_tables(rel_bias, seq_len, tq)
    x2 = x.reshape(batch * seq_len, D_MODEL)
    for l in range(depth):
        wl = w_in[l]
        w_qkv = wl[:, :QKV_WIDTH].astype(BF16)
        wg = wl[:, GATE_OFF:CONV_OFF].reshape(D_MODEL, N_KV, HPG * N_BRANCH)
        wg = jnp.pad(wg, ((0, 0), (0, 0), (0, LANE - HPG * N_BRANCH))).reshape(D_MODEL, N_KV * LANE).astype(BF16)
        w_h, w_b, w_c = (wl[:, CONV_OFF + k * CONV_WIDTH:CONV_OFF + (k + 1) * CONV_WIDTH].astype(BF16)
                         for k in range(3))
        g1 = g_pre_mix[l].reshape(1, D_MODEL)

        qkv, gates = _qkv_proj(x2, g1, w_qkv, wg)
        o_conv = _conv_proj(x2, g1, w_h, w_b, w_c, conv_w[l], seq_len)

        xc = qkv[:, ATTN_WIDTH:ATTN_WIDTH + 2 * KV_WIDTH]
        xc = xc.reshape(batch, n_chunks, CMP_STRIDE, 2 * N_KV, HEAD_DIM).transpose(0, 3, 1, 2, 4)
        xc = xc.reshape(batch, 2 * N_KV, n_chunks, half)
        w1 = jnp.stack([w_cmp_k1[l], w_cmp_v1[l]]).reshape(2, 2 * half, CMP_HIDDEN).astype(BF16)
        w2 = jnp.stack([w_cmp_k2[l], w_cmp_v2[l]]).astype(BF16)
        cmp_kv = _compress(xc, pe_cmp[l].reshape(2, half), w1, w2)
        cmp_k = cmp_kv[:, :N_KV]
        cmp_vt = cmp_kv[:, N_KV:].transpose(0, 1, 3, 2)

        def v_t(which):
            off = ATTN_WIDTH + which * KV_WIDTH
            v = qkv[:, off:off + KV_WIDTH].reshape(batch, seq_len, N_KV, HEAD_DIM)
            v = v.transpose(0, 2, 3, 1)
            ones = jnp.zeros((batch, N_KV, ONES_ROWS, seq_len), BF16).at[:, :, 0, :].set(1.0)
            return jnp.concatenate([v, ones], axis=2)

        o_attn = _nsa(qkv, cmp_k, cmp_vt, v_t(3), v_t(5), gates, band_t, tabs_t, ovl, e_t,
                      batch, seq_len, tq=tq)

        x1, h2 = _oproj(o_attn, o_conv, w_o[l].astype(BF16), x2,
                        g_post_mix[l].reshape(1, D_MODEL), g_pre_ffn[l].reshape(1, D_MODEL))
        x2 = _ffn(h2, w_up[l].astype(BF16), w_down[l].astype(BF16), x1, g_post_ffn[l].reshape(1, D_MODEL))
    return x2.reshape(batch, seq_len, D_MODEL)
```

```python
import functools
import math

import numpy as np
import jax
import jax.numpy as jnp
from jax import lax
from jax.experimental import pallas as pl
from jax.experimental.pallas import tpu as pltpu

F32 = jnp.float32
BF16 = jnp.bfloat16

D_MODEL = 2048
N_HEADS = 8
N_KV = 2
HPG = N_HEADS // N_KV
HEAD_DIM = 128
ATTN_WIDTH = N_HEADS * HEAD_DIM
KV_WIDTH = N_KV * HEAD_DIM
CONV_WIDTH = D_MODEL - ATTN_WIDTH
CONV_K = 3
N_BRANCH = 3
CMP_LEN = 32
CMP_STRIDE = 16
CMP_HIDDEN = 256
SEL_LEN = 64
SEL_TOPK = 16
WINDOW = 512
N_BUCKETS = 32
MAX_DIST = 128
EPS = 1e-6
NEG = -1e30
HALF_NEG = -5e29
FORCE = 1e9

QKV_WIDTH = ATTN_WIDTH + 6 * KV_WIDTH
GATE_OFF = QKV_WIDTH
CONV_OFF = QKV_WIDTH + N_HEADS * N_BRANCH
LANE = 128
VMEM_LIMIT = 56 * 1024 * 1024

_DN_T = (((1,), (1,)), ((), ()))


def _rms(x, g):
    ms = jnp.mean(x * x, axis=-1, keepdims=True)
    return x * lax.rsqrt(ms + EPS) * g


def _params(n_axes):
    return pltpu.CompilerParams(dimension_semantics=("arbitrary",) * n_axes, vmem_limit_bytes=VMEM_LIMIT)


QKV_BLOCK = 2 * KV_WIDTH
N_Q_BLOCKS = ATTN_WIDTH // QKV_BLOCK
BLK_K, BLK_C, BLK_V = N_Q_BLOCKS, N_Q_BLOCKS + 1, N_Q_BLOCKS + 2


def _qkv_kernel(x_ref, g_ref, w_ref, wg_ref, q_ref, k_ref, c_ref, vt_ref, gate_ref, h_ref, *, q_scale):
    j = pl.program_id(1)

    @pl.when(j == 0)
    def _():
        hb = _rms(x_ref[...], g_ref[...]).astype(BF16)
        h_ref[...] = hb
        gate_ref[...] = jax.nn.sigmoid(jnp.dot(hb, wg_ref[...], preferred_element_type=F32))

    acc = jnp.dot(h_ref[...], w_ref[...], preferred_element_type=F32)

    @pl.when(j < N_Q_BLOCKS)
    def _():
        q_ref[...] = (acc * q_scale).astype(BF16)

    @pl.when(j == BLK_K)
    def _():
        k_ref[...] = acc.astype(BF16)

    @pl.when(j == BLK_C)
    def _():
        c_ref[...] = acc

    @pl.when(j == BLK_V)
    def _():
        tm = acc.shape[0]
        ones = jnp.where(lax.broadcasted_iota(jnp.int32, (ONES_ROWS, tm), 0) == 0, 1.0, 0.0).astype(BF16)
        for s in range(QKV_BLOCK // HEAD_DIM):
            vt_ref[s, 0:HEAD_DIM, :] = acc[:, s * HEAD_DIM:(s + 1) * HEAD_DIM].T.astype(BF16)
            vt_ref[s, HEAD_DIM:HEAD_DIM + ONES_ROWS, :] = ones


def _qkv_proj(x2, g, w_qkv, w_gate, batch, seq_len, *, tm=1024):
    m = x2.shape[0]
    gw = w_gate.shape[1]
    tps = seq_len // tm
    n_slabs = QKV_BLOCK // HEAD_DIM
    kern = functools.partial(_qkv_kernel, q_scale=HEAD_DIM ** -0.5 * math.log2(math.e))
    return pl.pallas_call(
        kern,
        grid=(m // tm, w_qkv.shape[1] // QKV_BLOCK),
        in_specs=[
            pl.BlockSpec((tm, D_MODEL), lambda i, j: (i, 0)),
            pl.BlockSpec((1, D_MODEL), lambda i, j: (0, 0)),
            pl.BlockSpec((D_MODEL, QKV_BLOCK), lambda i, j: (0, j)),
            pl.BlockSpec((D_MODEL, gw), lambda i, j: (0, 0)),
        ],
        out_specs=[
            pl.BlockSpec((tm, QKV_BLOCK), lambda i, j: (i, jnp.minimum(j, N_Q_BLOCKS - 1))),
            pl.BlockSpec((tm, QKV_BLOCK), lambda i, j: (i, 0)),
            pl.BlockSpec((tm, QKV_BLOCK), lambda i, j: (i, 0)),
            pl.BlockSpec((None, n_slabs, HEAD_DIM + ONES_ROWS, tm), lambda i, j: (i // tps, 0, 0, i % tps)),
            pl.BlockSpec((tm, gw), lambda i, j: (i, 0)),
        ],
        out_shape=[
            jax.ShapeDtypeStruct((m, ATTN_WIDTH), BF16),
            jax.ShapeDtypeStruct((m, QKV_BLOCK), BF16),
            jax.ShapeDtypeStruct((m, QKV_BLOCK), F32),
            jax.ShapeDtypeStruct((batch, n_slabs, HEAD_DIM + ONES_ROWS, seq_len), BF16),
            jax.ShapeDtypeStruct((m, gw), F32),
        ],
        scratch_shapes=[pltpu.VMEM((tm, D_MODEL), BF16)],
        compiler_params=_params(2),
        name="qkv_proj",
    )(x2, g, w_qkv, w_gate)


def _conv_kernel(x_ref, g_ref, wh_ref, wb_ref, wc_ref, cw_ref, o_ref, h_ref, carry_ref, *, tiles_per_seq):
    i = pl.program_id(0)
    j = pl.program_id(1)

    @pl.when(j == 0)
    def _():
        h_ref[...] = _rms(x_ref[...], g_ref[...]).astype(BF16)

    hb = h_ref[...]
    ch = jnp.dot(hb, wh_ref[...], preferred_element_type=F32)
    cb = jnp.dot(hb, wb_ref[...], preferred_element_type=F32)
    cc = jnp.dot(hb, wc_ref[...], preferred_element_type=F32)
    u = cc * ch
    tm = u.shape[0]
    prev = carry_ref[j]
    prev = jnp.where(i % tiles_per_seq == 0, 0.0, prev)
    carry_ref[j] = u[tm - 8:tm, :]
    row = lax.broadcasted_iota(jnp.int32, u.shape, 0)
    u1 = jnp.where(row == 0, prev[7:8, :], pltpu.roll(u, 1, axis=0))
    u2 = jnp.where(row == 0, prev[6:7, :], jnp.where(row == 1, prev[7:8, :], pltpu.roll(u, 2, axis=0)))
    w = cw_ref[...]
    y = w[0:1, :] * u2
    y = y + w[1:2, :] * u1
    y = y + w[2:3, :] * u
    o_ref[...] = (cb * y).astype(BF16)


def _conv_proj(x2, g, w_h, w_b, w_c, conv_w, seq_len, *, tm=1024, tc=512):
    m = x2.shape[0]
    c = w_h.shape[1]
    kern = functools.partial(_conv_kernel, tiles_per_seq=seq_len // tm)
    wspec = pl.BlockSpec((D_MODEL, tc), lambda i, j: (0, j))
    return pl.pallas_call(
        kern,
        grid=(m // tm, c // tc),
        in_specs=[
            pl.BlockSpec((tm, D_MODEL), lambda i, j: (i, 0)),
            pl.BlockSpec((1, D_MODEL), lambda i, j: (0, 0)),
            wspec, wspec, wspec,
            pl.BlockSpec((CONV_K, tc), lambda i, j: (0, j)),
        ],
        out_specs=pl.BlockSpec((tm, tc), lambda i, j: (i, j)),
        out_shape=jax.ShapeDtypeStruct((m, c), BF16),
        scratch_shapes=[pltpu.VMEM((tm, D_MODEL), BF16), pltpu.VMEM((c // tc, 8, tc), F32)],
        compiler_params=_params(2),
        name="conv_proj",
    )(x2, g, w_h, w_b, w_c, conv_w)


def _compress_kernel(x_ref, pe_ref, w1_ref, w2_ref, o_ref, ot_ref):
    n = x_ref.shape[0] // CMP_STRIDE
    a = jnp.zeros((n, CMP_HIDDEN), F32)
    b = jnp.zeros((n, CMP_HIDDEN), F32)
    for l in range(CMP_STRIDE):
        xl = x_ref[pl.ds(l, n, stride=CMP_STRIDE), :]
        xa = (xl + pe_ref[l:l + 1, :]).astype(BF16)
        xb = (xl + pe_ref[CMP_STRIDE + l:CMP_STRIDE + l + 1, :]).astype(BF16)
        a = a + jnp.dot(xa, w1_ref[l], preferred_element_type=F32)
        b = b + jnp.dot(xb, w1_ref[CMP_STRIDE + l], preferred_element_type=F32)
    pre = a + pltpu.roll(b, n - 1, axis=0)
    hid = pre * jax.nn.sigmoid(pre)
    out = jnp.dot(hid.astype(BF16), w2_ref[...], preferred_element_type=F32)
    row = lax.broadcasted_iota(jnp.int32, out.shape, 0)
    out = jnp.where(row < n - 1, out, 0.0)
    o_ref[...] = out.astype(BF16)
    ot_ref[...] = out.T.astype(BF16)


def _compress(c_in, pe, w1, w2, batch, seq_len):
    n_slabs = c_in.shape[1] // HEAD_DIM
    n_chunks = seq_len // CMP_STRIDE
    return pl.pallas_call(
        _compress_kernel,
        grid=(batch, n_slabs),
        in_specs=[
            pl.BlockSpec((seq_len, HEAD_DIM), lambda i, j: (i, j)),
            pl.BlockSpec((CMP_LEN, HEAD_DIM), lambda i, j: (0, 0)),
            pl.BlockSpec((None, CMP_LEN, HEAD_DIM, CMP_HIDDEN), lambda i, j: (j // N_KV, 0, 0, 0)),
            pl.BlockSpec((None, CMP_HIDDEN, HEAD_DIM), lambda i, j: (j // N_KV, 0, 0)),
        ],
        out_specs=[
            pl.BlockSpec((None, None, n_chunks, HEAD_DIM), lambda i, j: (i, j, 0, 0)),
            pl.BlockSpec((None, None, HEAD_DIM, n_chunks), lambda i, j: (i, j, 0, 0)),
        ],
        out_shape=[
            jax.ShapeDtypeStruct((batch, n_slabs, n_chunks, HEAD_DIM), BF16),
            jax.ShapeDtypeStruct((batch, n_slabs, HEAD_DIM, n_chunks), BF16),
        ],
        compiler_params=_params(2),
        name="compress",
    )(c_in, pe, w1, w2)


TAB_DIAG, TAB_SUB, TAB_FAR = 0, 1, 2
BAND_BELOW = 8
ONES_ROWS = 16


def _nsa_kernel(q_ref, kc_ref, vct_ref, ks_ref, vst_ref, kw_ref, vwt_ref, gate_ref, band_ref, tab_ref,
                ovl_ref, et_ref, o_ref, kaug_ref, sc_ref, s0_ref, s1_ref, m_ref, acc_ref, *, tq, n_sel):
    i = pl.program_id(2)
    tk = tq
    mcols = HPG * tq
    nw = WINDOW // tk
    cpt = tq // CMP_STRIDE
    band_rows = BAND_BELOW + cpt

    @pl.when(i == 0)
    def _():
        kaug_ref[:, 0:HEAD_DIM] = ks_ref[...]
        kaug_ref[:, HEAD_DIM:2 * HEAD_DIM] = et_ref[...]

    q = q_ref[...]
    qs = jnp.concatenate([q[:, h * HEAD_DIM:(h + 1) * HEAD_DIM] for h in range(HPG)], axis=0)

    raw = lax.dot_general(kc_ref[...], qs, _DN_T, preferred_element_type=F32)
    crow = lax.broadcasted_iota(jnp.int32, raw.shape, 0)
    sc_ref[...] = jnp.where(crow < cpt * (i + 1), raw, NEG)
    band = pl.ds(pl.multiple_of(jnp.maximum(cpt * i - BAND_BELOW, 0), 8), band_rows)
    sc_ref[band, :] += band_ref[jnp.minimum(i, 1)]
    sc = sc_ref[...]
    mc = jnp.max(sc, axis=0, keepdims=True)
    pc = jnp.where(sc > HALF_NEG, jnp.exp2(sc - mc), 0.0)
    lc = jnp.sum(pc, axis=0, keepdims=True)
    pc = pc * jnp.where(lc > 0.0, 1.0 / lc, 0.0)
    o_c = jnp.dot(vct_ref[...], pc.astype(BF16), preferred_element_type=F32)

    ps = pc[:, 0:tq] + pc[:, tq:2 * tq] + pc[:, 2 * tq:3 * tq] + pc[:, 3 * tq:4 * tq]
    hi = ps.astype(BF16)
    r1 = ps - hi.astype(F32)
    mid = r1.astype(BF16)
    lo = (r1 - mid.astype(F32)).astype(BF16)
    ovl = ovl_ref[...]
    imp = (jnp.dot(ovl, hi, preferred_element_type=F32) + jnp.dot(ovl, mid, preferred_element_type=F32)
           + jnp.dot(ovl, lo, preferred_element_type=F32))
    jj = lax.broadcasted_iota(jnp.int32, (n_sel, tq), 0)
    tt = i * tq + lax.broadcasted_iota(jnp.int32, (n_sel, tq), 1)
    cur = tt >> int(math.log2(SEL_LEN))
    forced = (jj == 0) | (jj == cur) | (jj == cur - 1)
    imp = jnp.where(forced, FORCE, imp)
    imp = jnp.where(jj * SEL_LEN <= tt, imp, NEG)
    sub = 8
    ranks = []
    for j0 in range(0, n_sel, sub):
        blk = imp[j0:j0 + sub, :]
        jl = j0 + lax.broadcasted_iota(jnp.int32, blk.shape, 0)
        cnt = jnp.zeros(blk.shape, jnp.int32)
        for b in range(n_sel):
            row = imp[b:b + 1, :]
            if b < j0:
                cnt = cnt + jnp.where(row >= blk, 1, 0)
            elif b >= j0 + sub:
                cnt = cnt + jnp.where(row > blk, 1, 0)
            else:
                cnt = cnt + jnp.where(row > blk, 1, jnp.where(row == blk, jnp.where(jl > b, 1, 0), 0))
        ranks.append(cnt)
    rank = jnp.concatenate(ranks, axis=0)
    selb_t = jnp.where(rank < SEL_TOPK, 0.0, NEG)
    selb = jnp.concatenate([selb_t, jnp.zeros((LANE - n_sel, tq), F32)], axis=0).T
    selb = selb.astype(BF16)
    qa = jnp.concatenate([qs, jnp.concatenate([selb] * HPG, axis=0)], axis=1)

    s_refs = (s0_ref, s1_ref)

    def init():
        m_ref[...] = jnp.full((1, mcols), NEG, F32)
        acc_ref[...] = jnp.zeros(acc_ref.shape, F32)

    def keys(kt):
        return pl.ds(pl.multiple_of(kt * tk, tk), tk)

    def qk(qmat, k_ref, kt, kind, buf):
        s = lax.dot_general(k_ref[keys(kt), :], qmat, _DN_T, preferred_element_type=F32)
        if kind is not None:
            s = s + tab_ref[kind]
        s_refs[buf][...] = s
        return jnp.max(s, axis=0, keepdims=True)

    def process(vt_ref, kt, buf, m_tile):
        m_prev = m_ref[...]
        m_next = jnp.maximum(m_prev, m_tile)
        alpha = jnp.exp2(m_prev - m_next)
        p = jnp.exp2(s_refs[buf][...] - m_next).astype(BF16)
        acc_ref[...] = alpha * acc_ref[...] + jnp.dot(vt_ref[:, keys(kt)], p, preferred_element_type=F32)
        m_ref[...] = m_next

    def finish():
        return acc_ref[0:HEAD_DIM, :] * (1.0 / acc_ref[HEAD_DIM:HEAD_DIM + 1, :])

    init()

    @pl.when(i == 0)
    def _():
        process(vst_ref, 0, 0, qk(qa, kaug_ref, 0, TAB_DIAG, 0))

    @pl.when(i >= 1)
    def _():
        mt0 = qk(qa, kaug_ref, i, TAB_DIAG, 0)
        mt1 = qk(qa, kaug_ref, i - 1, TAB_SUB, 1)
        process(vst_ref, i, 0, mt0)
        mt0 = qk(qa, kaug_ref, jnp.maximum(i - 2, 0), None, 0)
        process(vst_ref, i - 1, 1, mt1)
        n_far = i - 1

        def far_pair(p, mt0):
            a = i - 2 - 2 * p
            mt1 = qk(qa, kaug_ref, a - 1, None, 1)
            process(vst_ref, a, 0, mt0)
            mt0 = qk(qa, kaug_ref, jnp.maximum(a - 2, 0), None, 0)
            process(vst_ref, a - 1, 1, mt1)
            return mt0

        mt0 = lax.fori_loop(0, n_far // 2, far_pair, mt0)

        @pl.when(n_far % 2 == 1)
        def _():
            process(vst_ref, 0, 0, mt0)

    o_s = finish()

    init()
    for avail in range(nw + 1):

        @pl.when((i == avail) if avail < nw else (i >= nw))
        def _(avail=avail):
            kinds = [TAB_DIAG if d == 0 else TAB_SUB if d == 1 else TAB_FAR if d == nw else None
                     for d in range(avail + 1)]
            mt = qk(qs, kw_ref, i, kinds[0], 0)
            for d in range(avail + 1):
                mt_next = qk(qs, kw_ref, i - d - 1, kinds[d + 1], (d + 1) % 2) if d < avail else None
                process(vwt_ref, i - d, d % 2, mt)
                mt = mt_next

    o_w = finish()

    gt = gate_ref[...].T
    for h in range(HPG):
        sl = slice(h * tq, (h + 1) * tq)
        o = gt[3 * h:3 * h + 1, :] * o_c[:, sl]
        o = o + gt[3 * h + 1:3 * h + 2, :] * o_s[:, sl]
        o = o + gt[3 * h + 2:3 * h + 3, :] * o_w[:, sl]
        o_ref[:, h * HEAD_DIM:(h + 1) * HEAD_DIM] = o.T.astype(BF16)


def _nsa(q, k_all, vt_all, cmp, cmp_t, gates, band_t, tabs_t, ovl, e_t, batch, seq_len, *, tq):
    n_t = seq_len // tq
    n_sel = seq_len // SEL_LEN
    n_chunks = cmp.shape[2]
    mcols = HPG * tq
    band_rows = band_t.shape[2]
    qw = HPG * HEAD_DIM

    def vt_spec(first):
        return pl.BlockSpec((None, None, HEAD_DIM + ONES_ROWS, seq_len), lambda b, g, i: (b, first + g, 0, 0))

    kern = functools.partial(_nsa_kernel, tq=tq, n_sel=n_sel)
    return pl.pallas_call(
        kern,
        grid=(batch, N_KV, n_t),
        in_specs=[
            pl.BlockSpec((tq, qw), lambda b, g, i: (b * n_t + i, g)),
            pl.BlockSpec((None, None, n_chunks, HEAD_DIM), lambda b, g, i: (b, g, 0, 0)),
            pl.BlockSpec((None, None, HEAD_DIM, n_chunks), lambda b, g, i: (b, N_KV + g, 0, 0)),
            pl.BlockSpec((seq_len, HEAD_DIM), lambda b, g, i: (b, g)),
            vt_spec(0),
            pl.BlockSpec((seq_len, HEAD_DIM), lambda b, g, i: (b, N_KV + g)),
            vt_spec(N_KV),
            pl.BlockSpec((tq, LANE), lambda b, g, i: (b * n_t + i, g)),
            pl.BlockSpec((None, 2, band_rows, mcols), lambda b, g, i: (g, 0, 0, 0)),
            pl.BlockSpec((None, 3, tq, mcols), lambda b, g, i: (g, 0, 0, 0)),
            pl.BlockSpec((n_sel, n_chunks), lambda b, g, i: (0, 0)),
            pl.BlockSpec((seq_len, LANE), lambda b, g, i: (0, 0)),
        ],
        out_specs=pl.BlockSpec((tq, qw), lambda b, g, i: (b * n_t + i, g)),
        out_shape=jax.ShapeDtypeStruct((batch * seq_len, ATTN_WIDTH), BF16),
        scratch_shapes=[
            pltpu.VMEM((seq_len, 2 * HEAD_DIM), BF16),
            pltpu.VMEM((n_chunks, mcols), F32),
            pltpu.VMEM((tq, mcols), F32),
            pltpu.VMEM((tq, mcols), F32),
            pltpu.VMEM((1, mcols), F32),
            pltpu.VMEM((HEAD_DIM + ONES_ROWS, mcols), F32),
        ],
        compiler_params=_params(3),
        name="nsa_attention",
    )(q, cmp, cmp_t, k_all, vt_all, k_all, vt_all, gates, band_t, tabs_t, ovl, e_t)


def _oproj_kernel(oa_ref, ov_ref, wo_ref, x_ref, g1_ref, g2_ref, x1_ref, h2_ref):
    ka = oa_ref.shape[1]
    mix = jnp.dot(oa_ref[...], wo_ref[0:ka, :], preferred_element_type=F32)
    mix = mix + jnp.dot(ov_ref[...], wo_ref[ka:, :], preferred_element_type=F32)
    x1 = x_ref[...] + _rms(mix, g1_ref[...])
    x1_ref[...] = x1
    h2_ref[...] = _rms(x1, g2_ref[...]).astype(BF16)


def _oproj(o_attn, o_conv, w_o, x2, g_post, g_pre, *, tm=512):
    m = x2.shape[0]
    ka, kv = o_attn.shape[1], o_conv.shape[1]
    row = lambda i: (i, 0)
    fixed = lambda i: (0, 0)
    return pl.pallas_call(
        _oproj_kernel,
        grid=(m // tm,),
        in_specs=[
            pl.BlockSpec((tm, ka), row),
            pl.BlockSpec((tm, kv), row),
            pl.BlockSpec((ka + kv, D_MODEL), fixed),
            pl.BlockSpec((tm, D_MODEL), row),
            pl.BlockSpec((1, D_MODEL), fixed),
            pl.BlockSpec((1, D_MODEL), fixed),
        ],
        out_specs=[pl.BlockSpec((tm, D_MODEL), row), pl.BlockSpec((tm, D_MODEL), row)],
        out_shape=[jax.ShapeDtypeStruct((m, D_MODEL), F32), jax.ShapeDtypeStruct((m, D_MODEL), BF16)],
        compiler_params=_params(1),
        name="out_proj",
    )(o_attn, o_conv, w_o, x2, g_post, g_pre)


def _ffn_kernel(h_ref, wu_ref, wd_ref, x1_ref, g_ref, o_ref, acc_ref):
    j = pl.program_id(1)

    @pl.when(j == 0)
    def _():
        acc_ref[...] = jnp.zeros(acc_ref.shape, F32)

    a = jnp.dot(h_ref[...], wu_ref[...], preferred_element_type=F32)
    a = jnp.square(jnp.maximum(a, 0.0)).astype(BF16)
    acc_ref[...] += jnp.dot(a, wd_ref[...], preferred_element_type=F32)

    @pl.when(j == pl.num_programs(1) - 1)
    def _():
        o_ref[...] = x1_ref[...] + _rms(acc_ref[...], g_ref[...])


def _ffn(h2, w_up, w_down, x1, g_post, *, tm=512, tf=1024):
    m = h2.shape[0]
    d_ff = w_up.shape[1]
    return pl.pallas_call(
        _ffn_kernel,
        grid=(m // tm, d_ff // tf),
        in_specs=[
            pl.BlockSpec((tm, D_MODEL), lambda i, j: (i, 0)),
            pl.BlockSpec((D_MODEL, tf), lambda i, j: (0, j)),
            pl.BlockSpec((tf, D_MODEL), lambda i, j: (j, 0)),
            pl.BlockSpec((tm, D_MODEL), lambda i, j: (i, 0)),
            pl.BlockSpec((1, D_MODEL), lambda i, j: (0, 0)),
        ],
        out_specs=pl.BlockSpec((tm, D_MODEL), lambda i, j: (i, 0)),
        out_shape=jax.ShapeDtypeStruct((m, D_MODEL), F32),
        scratch_shapes=[pltpu.VMEM((tm, D_MODEL), F32)],
        compiler_params=_params(2),
        name="ffn",
    )(h2, w_up, w_down, x1, g_post)


def _bucket_np(dist):
    n = np.maximum(dist, 0)
    max_exact = N_BUCKETS // 2
    nf = np.maximum(n, 1).astype(np.float32)
    large = max_exact + (np.log(nf / np.float32(max_exact)) / np.float32(math.log(MAX_DIST / max_exact))
                         * np.float32(N_BUCKETS - max_exact)).astype(np.int32)
    large = np.minimum(large, N_BUCKETS - 1)
    return np.where(n < max_exact, n, large)


def _bucket_starts():
    b = _bucket_np(np.arange(4 * MAX_DIST))
    return [int(np.argmax(b == k)) for k in range(N_BUCKETS)]


def _bias_by_distance(rel_bias, dist):
    starts = _bucket_starts()
    d = jnp.asarray(dist, jnp.int32)[None]
    col = lambda k: rel_bias[:, k].reshape((N_HEADS,) + (1,) * dist.ndim)
    out = jnp.broadcast_to(col(0), (N_HEADS,) + dist.shape)
    for k in range(1, N_BUCKETS):
        out = jnp.where(d >= starts[k], col(k), out)
    return out


def _attention_tables(rel_bias, seq_len, tq):
    starts = _bucket_starts()
    assert starts[N_BUCKETS - 1] <= CMP_STRIDE * (BAND_BELOW + 1) - (CMP_LEN - 1)
    assert starts[N_BUCKETS - 1] <= tq
    rel = (rel_bias - rel_bias[:, N_BUCKETS - 1:]) * math.log2(math.e)

    def per_group(tab):
        lead = tab.shape[1:-2]
        nk, nq = tab.shape[-2:]
        tab = tab.reshape((N_KV, HPG) + lead + (nk, nq))
        tab = jnp.moveaxis(tab, 1, -2)
        return tab.reshape((N_KV,) + lead + (nk, HPG * nq)).astype(F32)

    c = np.arange(tq)[:, None]
    r = np.arange(tq)[None, :]
    diag = jnp.where(jnp.asarray(r >= c)[None], _bias_by_distance(rel, np.maximum(r - c, 0)), NEG)
    sub = _bias_by_distance(rel, tq + r - c)
    far = jnp.broadcast_to(jnp.where(jnp.asarray(r < c)[None], 0.0, NEG), (N_HEADS, tq, tq))
    tabs = per_group(jnp.stack([diag, sub, far], axis=1))

    cpt = tq // CMP_STRIDE
    cl = np.arange(BAND_BELOW + cpt)[:, None]
    bands = []
    for first in (0, -BAND_BELOW):
        dist = r - ((cl + first) * CMP_STRIDE + CMP_LEN - 1)
        bands.append(jnp.where(jnp.asarray(dist >= 0)[None], _bias_by_distance(rel, np.maximum(dist, 0)), NEG))
    band = per_group(jnp.stack(bands, axis=1))

    n_chunks = seq_len // CMP_STRIDE
    n_cmp = (seq_len - CMP_LEN) // CMP_STRIDE + 1
    n_sel = seq_len // SEL_LEN
    ci = np.arange(n_chunks)[None, :] * CMP_STRIDE
    sj = np.arange(n_sel)[:, None] * SEL_LEN
    ovl = ((ci < sj + SEL_LEN) & (ci + CMP_LEN > sj) & (np.arange(n_chunks)[None, :] < n_cmp))
    e_t = (np.arange(seq_len)[:, None] // SEL_LEN == np.arange(LANE)[None, :])
    return tabs, band, jnp.asarray(ovl, BF16), jnp.asarray(e_t, BF16)


def kernel(x, w_in, pe_cmp, w_cmp_k1, w_cmp_k2, w_cmp_v1, w_cmp_v2, conv_w, rel_bias, w_o, w_up, w_down,
           g_pre_mix, g_post_mix, g_pre_ffn, g_post_ffn):
    batch, seq_len, _ = x.shape
    depth = w_in.shape[0]
    tq = 256
    tabs_t, band_t, ovl, e_t = _attention_tables(rel_bias, seq_len, tq)
    x2 = x.reshape(batch * seq_len, D_MODEL)
    for l in range(depth):
        wl = w_in[l]
        slab = lambda k: wl[:, ATTN_WIDTH + k * KV_WIDTH:ATTN_WIDTH + (k + 1) * KV_WIDTH]
        w_qkv = jnp.concatenate([wl[:, :ATTN_WIDTH], slab(2), slab(4), slab(0), slab(1), slab(3), slab(5)],
                                axis=1).astype(BF16)
        wg = wl[:, GATE_OFF:CONV_OFF].reshape(D_MODEL, N_KV, HPG * N_BRANCH)
        wg = jnp.pad(wg, ((0, 0), (0, 0), (0, LANE - HPG * N_BRANCH))).reshape(D_MODEL, N_KV * LANE).astype(BF16)
        w_h, w_b, w_c = (wl[:, CONV_OFF + k * CONV_WIDTH:CONV_OFF + (k + 1) * CONV_WIDTH].astype(BF16)
                         for k in range(3))
        g1 = g_pre_mix[l].reshape(1, D_MODEL)

        q, k_all, c_in, vt_all, gates = _qkv_proj(x2, g1, w_qkv, wg, batch, seq_len)
        o_conv = _conv_proj(x2, g1, w_h, w_b, w_c, conv_w[l], seq_len)

        w1 = jnp.stack([w_cmp_k1[l], w_cmp_v1[l]]).astype(BF16)
        w2 = jnp.stack([w_cmp_k2[l], w_cmp_v2[l]]).astype(BF16)
        cmp, cmp_t = _compress(c_in, pe_cmp[l], w1, w2, batch, seq_len)

        o_attn = _nsa(q, k_all, vt_all, cmp, cmp_t, gates, band_t, tabs_t, ovl, e_t, batch, seq_len, tq=tq)

        x1, h2 = _oproj(o_attn, o_conv, w_o[l].astype(BF16), x2,
                        g_post_mix[l].reshape(1, D_MODEL), g_pre_ffn[l].reshape(1, D_MODEL))
        x2 = _ffn(h2, w_up[l].astype(BF16), w_down[l].astype(BF16), x1, g_post_ffn[l].reshape(1, D_MODEL))
    return x2.reshape(batch, seq_len, D_MODEL)
```

```python
import functools
import math

import numpy as np
import jax
import jax.numpy as jnp
from jax import lax
from jax.experimental import pallas as pl
from jax.experimental.pallas import tpu as pltpu

F32 = jnp.float32
BF16 = jnp.bfloat16

D_MODEL = 2048
N_HEADS = 8
N_KV = 2
HPG = N_HEADS // N_KV
HEAD_DIM = 128
ATTN_WIDTH = N_HEADS * HEAD_DIM
KV_WIDTH = N_KV * HEAD_DIM
CONV_WIDTH = D_MODEL - ATTN_WIDTH
CONV_K = 3
N_BRANCH = 3
CMP_LEN = 32
CMP_STRIDE = 16
CMP_HIDDEN = 256
SEL_LEN = 64
SEL_TOPK = 16
WINDOW = 512
N_BUCKETS = 32
MAX_DIST = 128
EPS = 1e-6
NEG = -1e30
HALF_NEG = -5e29
FORCE = 1e9

QKV_WIDTH = ATTN_WIDTH + 6 * KV_WIDTH
GATE_OFF = QKV_WIDTH
CONV_OFF = QKV_WIDTH + N_HEADS * N_BRANCH
LANE = 128
VMEM_LIMIT = 56 * 1024 * 1024

_DN_T = (((1,), (1,)), ((), ()))


def _rms(x, g):
    ms = jnp.mean(x * x, axis=-1, keepdims=True)
    return x * lax.rsqrt(ms + EPS) * g


def _params(n_axes):
    return pltpu.CompilerParams(dimension_semantics=("arbitrary",) * n_axes, vmem_limit_bytes=VMEM_LIMIT)


QKV_BLOCK = 2 * KV_WIDTH
N_Q_BLOCKS = ATTN_WIDTH // QKV_BLOCK
BLK_K, BLK_C, BLK_V = N_Q_BLOCKS, N_Q_BLOCKS + 1, N_Q_BLOCKS + 2
CONV_BLOCK = 512


def _inproj_kernel(x_ref, g_ref, w_ref, wg_ref, wh_ref, wb_ref, wc_ref, cw_ref,
                   q_ref, k_ref, c_ref, vt_ref, gate_ref, ov_ref, h_ref, carry_ref, *, q_scale, tiles_per_seq):
    i = pl.program_id(0)
    tm = x_ref.shape[0]
    h_ref[...] = _rms(x_ref[...], g_ref[...]).astype(BF16)
    gate_ref[...] = jax.nn.sigmoid(jnp.dot(h_ref[...], wg_ref[...], preferred_element_type=F32))

    def proj(w, j, width):
        return jnp.dot(h_ref[...], w[:, j * width:(j + 1) * width], preferred_element_type=F32)

    for j in range(N_Q_BLOCKS):
        q_ref[:, j * QKV_BLOCK:(j + 1) * QKV_BLOCK] = (proj(w_ref, j, QKV_BLOCK) * q_scale).astype(BF16)
    k_ref[...] = proj(w_ref, BLK_K, QKV_BLOCK).astype(BF16)
    c_ref[...] = proj(w_ref, BLK_C, QKV_BLOCK)
    v = proj(w_ref, BLK_V, QKV_BLOCK)
    ones = jnp.where(lax.broadcasted_iota(jnp.int32, (ONES_ROWS, tm), 0) == 0, 1.0, 0.0).astype(BF16)
    for s in range(QKV_BLOCK // HEAD_DIM):
        vt_ref[s, 0:HEAD_DIM, :] = v[:, s * HEAD_DIM:(s + 1) * HEAD_DIM].T.astype(BF16)
        vt_ref[s, HEAD_DIM:HEAD_DIM + ONES_ROWS, :] = ones

    row = lax.broadcasted_iota(jnp.int32, (tm, CONV_BLOCK), 0)
    for j in range(CONV_WIDTH // CONV_BLOCK):
        cols = slice(j * CONV_BLOCK, (j + 1) * CONV_BLOCK)
        u = proj(wc_ref, j, CONV_BLOCK) * proj(wh_ref, j, CONV_BLOCK)
        prev = carry_ref[j]
        prev = jnp.where(i % tiles_per_seq == 0, 0.0, prev)
        carry_ref[j] = u[tm - 8:tm, :]
        u1 = jnp.where(row == 0, prev[7:8, :], pltpu.roll(u, 1, axis=0))
        u2 = jnp.where(row == 0, prev[6:7, :], jnp.where(row == 1, prev[7:8, :], pltpu.roll(u, 2, axis=0)))
        w = cw_ref[:, cols]
        y = w[0:1, :] * u2
        y = y + w[1:2, :] * u1
        y = y + w[2:3, :] * u
        ov_ref[:, cols] = (proj(wb_ref, j, CONV_BLOCK) * y).astype(BF16)


def _in_proj(x2, g, w_qkv, w_gate, w_h, w_b, w_c, conv_w, batch, seq_len, *, tm=512):
    m = x2.shape[0]
    gw = w_gate.shape[1]
    tps = seq_len // tm
    n_slabs = QKV_BLOCK // HEAD_DIM
    kern = functools.partial(_inproj_kernel, q_scale=HEAD_DIM ** -0.5 * math.log2(math.e), tiles_per_seq=tps)
    row = lambda i: (i, 0)

    def resident(shape):
        return pl.BlockSpec(shape, lambda i: (0, 0), pipeline_mode=pl.Buffered(1))

    return pl.pallas_call(
        kern,
        grid=(m // tm,),
        in_specs=[
            pl.BlockSpec((tm, D_MODEL), row),
            resident((1, D_MODEL)),
            resident(w_qkv.shape),
            resident(w_gate.shape),
            resident(w_h.shape), resident(w_b.shape), resident(w_c.shape),
            resident(conv_w.shape),
        ],
        out_specs=[
            pl.BlockSpec((tm, ATTN_WIDTH), row),
            pl.BlockSpec((tm, QKV_BLOCK), row),
            pl.BlockSpec((tm, QKV_BLOCK), row),
            pl.BlockSpec((None, n_slabs, HEAD_DIM + ONES_ROWS, tm), lambda i: (i // tps, 0, 0, i % tps)),
            pl.BlockSpec((tm, gw), row),
            pl.BlockSpec((tm, CONV_WIDTH), row),
        ],
        out_shape=[
            jax.ShapeDtypeStruct((m, ATTN_WIDTH), BF16),
            jax.ShapeDtypeStruct((m, QKV_BLOCK), BF16),
            jax.ShapeDtypeStruct((m, QKV_BLOCK), F32),
            jax.ShapeDtypeStruct((batch, n_slabs, HEAD_DIM + ONES_ROWS, seq_len), BF16),
            jax.ShapeDtypeStruct((m, gw), F32),
            jax.ShapeDtypeStruct((m, CONV_WIDTH), BF16),
        ],
        scratch_shapes=[pltpu.VMEM((tm, D_MODEL), BF16),
                        pltpu.VMEM((CONV_WIDTH // CONV_BLOCK, 8, CONV_BLOCK), F32)],
        compiler_params=_params(1),
        name="in_proj",
    )(x2, g, w_qkv, w_gate, w_h, w_b, w_c, conv_w)


def _compress_kernel(x_ref, pe_ref, w1_ref, w2_ref, o_ref, ot_ref):
    n = x_ref.shape[0] // CMP_STRIDE
    a = jnp.zeros((n, CMP_HIDDEN), F32)
    b = jnp.zeros((n, CMP_HIDDEN), F32)
    for l in range(CMP_STRIDE):
        xl = x_ref[pl.ds(l, n, stride=CMP_STRIDE), :]
        xa = (xl + pe_ref[l:l + 1, :]).astype(BF16)
        xb = (xl + pe_ref[CMP_STRIDE + l:CMP_STRIDE + l + 1, :]).astype(BF16)
        a = a + jnp.dot(xa, w1_ref[l], preferred_element_type=F32)
        b = b + jnp.dot(xb, w1_ref[CMP_STRIDE + l], preferred_element_type=F32)
    pre = a + pltpu.roll(b, n - 1, axis=0)
    hid = pre * jax.nn.sigmoid(pre)
    out = jnp.dot(hid.astype(BF16), w2_ref[...], preferred_element_type=F32)
    row = lax.broadcasted_iota(jnp.int32, out.shape, 0)
    out = jnp.where(row < n - 1, out, 0.0)
    o_ref[...] = out.astype(BF16)
    ot_ref[...] = out.T.astype(BF16)


def _compress(c_in, pe, w1, w2, batch, seq_len):
    n_slabs = c_in.shape[1] // HEAD_DIM
    n_chunks = seq_len // CMP_STRIDE
    return pl.pallas_call(
        _compress_kernel,
        grid=(batch, n_slabs),
        in_specs=[
            pl.BlockSpec((seq_len, HEAD_DIM), lambda i, j: (i, j)),
            pl.BlockSpec((CMP_LEN, HEAD_DIM), lambda i, j: (0, 0)),
            pl.BlockSpec((None, CMP_LEN, HEAD_DIM, CMP_HIDDEN), lambda i, j: (j // N_KV, 0, 0, 0)),
            pl.BlockSpec((None, CMP_HIDDEN, HEAD_DIM), lambda i, j: (j // N_KV, 0, 0)),
        ],
        out_specs=[
            pl.BlockSpec((None, None, n_chunks, HEAD_DIM), lambda i, j: (i, j, 0, 0)),
            pl.BlockSpec((None, None, HEAD_DIM, n_chunks), lambda i, j: (i, j, 0, 0)),
        ],
        out_shape=[
            jax.ShapeDtypeStruct((batch, n_slabs, n_chunks, HEAD_DIM), BF16),
            jax.ShapeDtypeStruct((batch, n_slabs, HEAD_DIM, n_chunks), BF16),
        ],
        compiler_params=_params(2),
        name="compress",
    )(c_in, pe, w1, w2)


TAB_DIAG, TAB_SUB, TAB_FAR = 0, 1, 2
BAND_BELOW = 8
ONES_ROWS = 16


def _nsa_kernel(q_ref, kc_ref, vct_ref, ks_ref, vst_ref, kw_ref, vwt_ref, gate_ref, band_ref, tab_ref,
                ovl_ref, et_ref, o_ref, kaug_ref, sc_ref, s0_ref, s1_ref, m_ref, acc_ref, *, tq, n_sel):
    i = pl.program_id(2)
    tk = tq
    mcols = HPG * tq
    nw = WINDOW // tk
    cpt = tq // CMP_STRIDE
    band_rows = BAND_BELOW + cpt

    @pl.when(i == 0)
    def _():
        kaug_ref[:, 0:HEAD_DIM] = ks_ref[...]
        kaug_ref[:, HEAD_DIM:2 * HEAD_DIM] = et_ref[...]

    q = q_ref[...]
    qs = jnp.concatenate([q[:, h * HEAD_DIM:(h + 1) * HEAD_DIM] for h in range(HPG)], axis=0)

    raw = lax.dot_general(kc_ref[...], qs, _DN_T, preferred_element_type=F32)
    crow = lax.broadcasted_iota(jnp.int32, raw.shape, 0)
    sc_ref[...] = jnp.where(crow < cpt * (i + 1), raw, NEG)
    band = pl.ds(pl.multiple_of(jnp.maximum(cpt * i - BAND_BELOW, 0), 8), band_rows)
    sc_ref[band, :] += band_ref[jnp.minimum(i, 1)]
    sc = sc_ref[...]
    mc = jnp.max(sc, axis=0, keepdims=True)
    pc = jnp.where(sc > HALF_NEG, jnp.exp2(sc - mc), 0.0)
    lc = jnp.sum(pc, axis=0, keepdims=True)
    pc = pc * jnp.where(lc > 0.0, 1.0 / lc, 0.0)
    o_c = jnp.dot(vct_ref[...], pc.astype(BF16), preferred_element_type=F32)

    ps = pc[:, 0:tq] + pc[:, tq:2 * tq] + pc[:, 2 * tq:3 * tq] + pc[:, 3 * tq:4 * tq]
    hi = ps.astype(BF16)
    r1 = ps - hi.astype(F32)
    mid = r1.astype(BF16)
    lo = (r1 - mid.astype(F32)).astype(BF16)
    ovl = ovl_ref[...]
    imp = (jnp.dot(ovl, hi, preferred_element_type=F32) + jnp.dot(ovl, mid, preferred_element_type=F32)
           + jnp.dot(ovl, lo, preferred_element_type=F32))
    jj = lax.broadcasted_iota(jnp.int32, (n_sel, tq), 0)
    tt = i * tq + lax.broadcasted_iota(jnp.int32, (n_sel, tq), 1)
    cur = tt >> int(math.log2(SEL_LEN))
    forced = (jj == 0) | (jj == cur) | (jj == cur - 1)
    imp = jnp.where(forced, FORCE, imp)
    imp = jnp.where(jj * SEL_LEN <= tt, imp, NEG)
    sub = 8
    ranks = []
    for j0 in range(0, n_sel, sub):
        blk = imp[j0:j0 + sub, :]
        jl = j0 + lax.broadcasted_iota(jnp.int32, blk.shape, 0)
        cnt = jnp.zeros(blk.shape, jnp.int32)
        for b in range(n_sel):
            row = imp[b:b + 1, :]
            if b < j0:
                cnt = cnt + jnp.where(row >= blk, 1, 0)
            elif b >= j0 + sub:
                cnt = cnt + jnp.where(row > blk, 1, 0)
            else:
                cnt = cnt + jnp.where(row > blk, 1, jnp.where(row == blk, jnp.where(jl > b, 1, 0), 0))
        ranks.append(cnt)
    rank = jnp.concatenate(ranks, axis=0)
    selb_t = jnp.where(rank < SEL_TOPK, 0.0, NEG)
    selb = jnp.concatenate([selb_t, jnp.zeros((LANE - n_sel, tq), F32)], axis=0).T
    selb = selb.astype(BF16)
    qa = jnp.concatenate([qs, jnp.concatenate([selb] * HPG, axis=0)], axis=1)

    s_refs = (s0_ref, s1_ref)

    def init():
        m_ref[...] = jnp.full((1, mcols), NEG, F32)
        acc_ref[...] = jnp.zeros(acc_ref.shape, F32)

    def keys(kt):
        return pl.ds(pl.multiple_of(kt * tk, tk), tk)

    def qk(qmat, k_ref, kt, kind, buf):
        s = lax.dot_general(k_ref[keys(kt), :], qmat, _DN_T, preferred_element_type=F32)
        if kind is not None:
            s = s + tab_ref[kind]
        s_refs[buf][...] = s
        return jnp.max(s, axis=0, keepdims=True)

    def process(vt_ref, kt, buf, m_tile):
        m_prev = m_ref[...]
        m_next = jnp.maximum(m_prev, m_tile)
        alpha = jnp.exp2(m_prev - m_next)
        p = jnp.exp2(s_refs[buf][...] - m_next).astype(BF16)
        acc_ref[...] = alpha * acc_ref[...] + jnp.dot(vt_ref[:, keys(kt)], p, preferred_element_type=F32)
        m_ref[...] = m_next

    def finish():
        return acc_ref[0:HEAD_DIM, :] * (1.0 / acc_ref[HEAD_DIM:HEAD_DIM + 1, :])

    init()

    @pl.when(i == 0)
    def _():
        process(vst_ref, 0, 0, qk(qa, kaug_ref, 0, TAB_DIAG, 0))

    @pl.when(i >= 1)
    def _():
        mt0 = qk(qa, kaug_ref, i, TAB_DIAG, 0)
        mt1 = qk(qa, kaug_ref, i - 1, TAB_SUB, 1)
        process(vst_ref, i, 0, mt0)
        mt0 = qk(qa, kaug_ref, jnp.maximum(i - 2, 0), None, 0)
        process(vst_ref, i - 1, 1, mt1)
        n_far = i - 1

        def far_pair(p, mt0):
            a = i - 2 - 2 * p
            mt1 = qk(qa, kaug_ref, a - 1, None, 1)
            process(vst_ref, a, 0, mt0)
            mt0 = qk(qa, kaug_ref, jnp.maximum(a - 2, 0), None, 0)
            process(vst_ref, a - 1, 1, mt1)
            return mt0

        mt0 = lax.fori_loop(0, n_far // 2, far_pair, mt0)

        @pl.when(n_far % 2 == 1)
        def _():
            process(vst_ref, 0, 0, mt0)

    o_s = finish()

    init()
    for avail in range(nw + 1):

        @pl.when((i == avail) if avail < nw else (i >= nw))
        def _(avail=avail):
            kinds = [TAB_DIAG if d == 0 else TAB_SUB if d == 1 else TAB_FAR if d == nw else None
                     for d in range(avail + 1)]
            mt = qk(qs, kw_ref, i, kinds[0], 0)
            for d in range(avail + 1):
                mt_next = qk(qs, kw_ref, i - d - 1, kinds[d + 1], (d + 1) % 2) if d < avail else None
                process(vwt_ref, i - d, d % 2, mt)
                mt = mt_next

    o_w = finish()

    gt = gate_ref[...].T
    for h in range(HPG):
        sl = slice(h * tq, (h + 1) * tq)
        o = gt[3 * h:3 * h + 1, :] * o_c[:, sl]
        o = o + gt[3 * h + 1:3 * h + 2, :] * o_s[:, sl]
        o = o + gt[3 * h + 2:3 * h + 3, :] * o_w[:, sl]
        o_ref[:, h * HEAD_DIM:(h + 1) * HEAD_DIM] = o.T.astype(BF16)


def _nsa(q, k_all, vt_all, cmp, cmp_t, gates, band_t, tabs_t, ovl, e_t, batch, seq_len, *, tq):
    n_t = seq_len // tq
    n_sel = seq_len // SEL_LEN
    n_chunks = cmp.shape[2]
    mcols = HPG * tq
    band_rows = band_t.shape[2]
    qw = HPG * HEAD_DIM

    def vt_spec(first):
        return pl.BlockSpec((None, None, HEAD_DIM + ONES_ROWS, seq_len), lambda b, g, i: (b, first + g, 0, 0))

    kern = functools.partial(_nsa_kernel, tq=tq, n_sel=n_sel)
    return pl.pallas_call(
        kern,
        grid=(batch, N_KV, n_t),
        in_specs=[
            pl.BlockSpec((tq, qw), lambda b, g, i: (b * n_t + i, g)),
            pl.BlockSpec((None, None, n_chunks, HEAD_DIM), lambda b, g, i: (b, g, 0, 0)),
            pl.BlockSpec((None, None, HEAD_DIM, n_chunks), lambda b, g, i: (b, N_KV + g, 0, 0)),
            pl.BlockSpec((seq_len, HEAD_DIM), lambda b, g, i: (b, g)),
            vt_spec(0),
            pl.BlockSpec((seq_len, HEAD_DIM), lambda b, g, i: (b, N_KV + g)),
            vt_spec(N_KV),
            pl.BlockSpec((tq, LANE), lambda b, g, i: (b * n_t + i, g)),
            pl.BlockSpec((None, 2, band_rows, mcols), lambda b, g, i: (g, 0, 0, 0)),
            pl.BlockSpec((None, 3, tq, mcols), lambda b, g, i: (g, 0, 0, 0)),
            pl.BlockSpec((n_sel, n_chunks), lambda b, g, i: (0, 0)),
            pl.BlockSpec((seq_len, LANE), lambda b, g, i: (0, 0)),
        ],
        out_specs=pl.BlockSpec((tq, qw), lambda b, g, i: (b * n_t + i, g)),
        out_shape=jax.ShapeDtypeStruct((batch * seq_len, ATTN_WIDTH), BF16),
        scratch_shapes=[
            pltpu.VMEM((seq_len, 2 * HEAD_DIM), BF16),
            pltpu.VMEM((n_chunks, mcols), F32),
            pltpu.VMEM((tq, mcols), F32),
            pltpu.VMEM((tq, mcols), F32),
            pltpu.VMEM((1, mcols), F32),
            pltpu.VMEM((HEAD_DIM + ONES_ROWS, mcols), F32),
        ],
        compiler_params=_params(3),
        name="nsa_attention",
    )(q, cmp, cmp_t, k_all, vt_all, k_all, vt_all, gates, band_t, tabs_t, ovl, e_t)


def _oproj_kernel(oa_ref, ov_ref, wo_ref, x_ref, g1_ref, g2_ref, x1_ref, h2_ref):
    ka = oa_ref.shape[1]
    mix = jnp.dot(oa_ref[...], wo_ref[0:ka, :], preferred_element_type=F32)
    mix = mix + jnp.dot(ov_ref[...], wo_ref[ka:, :], preferred_element_type=F32)
    x1 = x_ref[...] + _rms(mix, g1_ref[...])
    x1_ref[...] = x1
    h2_ref[...] = _rms(x1, g2_ref[...]).astype(BF16)


def _oproj(o_attn, o_conv, w_o, x2, g_post, g_pre, *, tm=512):
    m = x2.shape[0]
    ka, kv = o_attn.shape[1], o_conv.shape[1]
    row = lambda i: (i, 0)
    fixed = lambda i: (0, 0)
    return pl.pallas_call(
        _oproj_kernel,
        grid=(m // tm,),
        in_specs=[
            pl.BlockSpec((tm, ka), row),
            pl.BlockSpec((tm, kv), row),
            pl.BlockSpec((ka + kv, D_MODEL), fixed),
            pl.BlockSpec((tm, D_MODEL), row),
            pl.BlockSpec((1, D_MODEL), fixed),
            pl.BlockSpec((1, D_MODEL), fixed),
        ],
        out_specs=[pl.BlockSpec((tm, D_MODEL), row), pl.BlockSpec((tm, D_MODEL), row)],
        out_shape=[jax.ShapeDtypeStruct((m, D_MODEL), F32), jax.ShapeDtypeStruct((m, D_MODEL), BF16)],
        compiler_params=_params(1),
        name="out_proj",
    )(o_attn, o_conv, w_o, x2, g_post, g_pre)


def _ffn_kernel(h_ref, wu_ref, wd_ref, x1_ref, g_ref, o_ref, acc_ref):
    j = pl.program_id(1)

    @pl.when(j == 0)
    def _():
        acc_ref[...] = jnp.zeros(acc_ref.shape, F32)

    a = jnp.dot(h_ref[...], wu_ref[...], preferred_element_type=F32)
    a = jnp.square(jnp.maximum(a, 0.0)).astype(BF16)
    acc_ref[...] += jnp.dot(a, wd_ref[...], preferred_element_type=F32)

    @pl.when(j == pl.num_programs(1) - 1)
    def _():
        o_ref[...] = x1_ref[...] + _rms(acc_ref[...], g_ref[...])


def _ffn(h2, w_up, w_down, x1, g_post, *, tm=512, tf=1024):
    m = h2.shape[0]
    d_ff = w_up.shape[1]
    return pl.pallas_call(
        _ffn_kernel,
        grid=(m // tm, d_ff // tf),
        in_specs=[
            pl.BlockSpec((tm, D_MODEL), lambda i, j: (i, 0)),
            pl.BlockSpec((D_MODEL, tf), lambda i, j: (0, j)),
            pl.BlockSpec((tf, D_MODEL), lambda i, j: (j, 0)),
            pl.BlockSpec((tm, D_MODEL), lambda i, j: (i, 0)),
            pl.BlockSpec((1, D_MODEL), lambda i, j: (0, 0)),
        ],
        out_specs=pl.BlockSpec((tm, D_MODEL), lambda i, j: (i, 0)),
        out_shape=jax.ShapeDtypeStruct((m, D_MODEL), F32),
        scratch_shapes=[pltpu.VMEM((tm, D_MODEL), F32)],
        compiler_params=_params(2),
        name="ffn",
    )(h2, w_up, w_down, x1, g_post)


def _bucket_np(dist):
    n = np.maximum(dist, 0)
    max_exact = N_BUCKETS // 2
    nf = np.maximum(n, 1).astype(np.float32)
    large = max_exact + (np.log(nf / np.float32(max_exact)) / np.float32(math.log(MAX_DIST / max_exact))
                         * np.float32(N_BUCKETS - max_exact)).astype(np.int32)
    large = np.minimum(large, N_BUCKETS - 1)
    return np.where(n < max_exact, n, large)


def _bucket_starts():
    b = _bucket_np(np.arange(4 * MAX_DIST))
    return [int(np.argmax(b == k)) for k in range(N_BUCKETS)]


def _bias_by_distance(rel_bias, dist):
    starts = _bucket_starts()
    d = jnp.asarray(dist, jnp.int32)[None]
    col = lambda k: rel_bias[:, k].reshape((N_HEADS,) + (1,) * dist.ndim)
    out = jnp.broadcast_to(col(0), (N_HEADS,) + dist.shape)
    for k in range(1, N_BUCKETS):
        out = jnp.where(d >= starts[k], col(k), out)
    return out


def _attention_tables(rel_bias, seq_len, tq):
    starts = _bucket_starts()
    assert starts[N_BUCKETS - 1] <= CMP_STRIDE * (BAND_BELOW + 1) - (CMP_LEN - 1)
    assert starts[N_BUCKETS - 1] <= tq
    rel = (rel_bias - rel_bias[:, N_BUCKETS - 1:]) * math.log2(math.e)

    def per_group(tab):
        lead = tab.shape[1:-2]
        nk, nq = tab.shape[-2:]
        tab = tab.reshape((N_KV, HPG) + lead + (nk, nq))
        tab = jnp.moveaxis(tab, 1, -2)
        return tab.reshape((N_KV,) + lead + (nk, HPG * nq)).astype(F32)

    c = np.arange(tq)[:, None]
    r = np.arange(tq)[None, :]
    diag = jnp.where(jnp.asarray(r >= c)[None], _bias_by_distance(rel, np.maximum(r - c, 0)), NEG)
    sub = _bias_by_distance(rel, tq + r - c)
    far = jnp.broadcast_to(jnp.where(jnp.asarray(r < c)[None], 0.0, NEG), (N_HEADS, tq, tq))
    tabs = per_group(jnp.stack([diag, sub, far], axis=1))

    cpt = tq // CMP_STRIDE
    cl = np.arange(BAND_BELOW + cpt)[:, None]
    bands = []
    for first in (0, -BAND_BELOW):
        dist = r - ((cl + first) * CMP_STRIDE + CMP_LEN - 1)
        bands.append(jnp.where(jnp.asarray(dist >= 0)[None], _bias_by_distance(rel, np.maximum(dist, 0)), NEG))
    band = per_group(jnp.stack(bands, axis=1))

    n_chunks = seq_len // CMP_STRIDE
    n_cmp = (seq_len - CMP_LEN) // CMP_STRIDE + 1
    n_sel = seq_len // SEL_LEN
    ci = np.arange(n_chunks)[None, :] * CMP_STRIDE
    sj = np.arange(n_sel)[:, None] * SEL_LEN
    ovl = ((ci < sj + SEL_LEN) & (ci + CMP_LEN > sj) & (np.arange(n_chunks)[None, :] < n_cmp))
    e_t = (np.arange(seq_len)[:, None] // SEL_LEN == np.arange(LANE)[None, :])
    return tabs, band, jnp.asarray(ovl, BF16), jnp.asarray(e_t, BF16)


def kernel(x, w_in, pe_cmp, w_cmp_k1, w_cmp_k2, w_cmp_v1, w_cmp_v2, conv_w, rel_bias, w_o, w_up, w_down,
           g_pre_mix, g_post_mix, g_pre_ffn, g_post_ffn):
    batch, seq_len, _ = x.shape
    depth = w_in.shape[0]
    tq = 256
    tabs_t, band_t, ovl, e_t = _attention_tables(rel_bias, seq_len, tq)
    x2 = x.reshape(batch * seq_len, D_MODEL)
    for l in range(depth):
        wl = w_in[l]
        slab = lambda k: wl[:, ATTN_WIDTH + k * KV_WIDTH:ATTN_WIDTH + (k + 1) * KV_WIDTH]
        w_qkv = jnp.concatenate([wl[:, :ATTN_WIDTH], slab(2), slab(4), slab(0), slab(1), slab(3), slab(5)],
                                axis=1).astype(BF16)
        wg = wl[:, GATE_OFF:CONV_OFF].reshape(D_MODEL, N_KV, HPG * N_BRANCH)
        wg = jnp.pad(wg, ((0, 0), (0, 0), (0, LANE - HPG * N_BRANCH))).reshape(D_MODEL, N_KV * LANE).astype(BF16)
        w_h, w_b, w_c = (wl[:, CONV_OFF + k * CONV_WIDTH:CONV_OFF + (k + 1) * CONV_WIDTH].astype(BF16)
                         for k in range(3))
        g1 = g_pre_mix[l].reshape(1, D_MODEL)

        q, k_all, c_in, vt_all, gates, o_conv = _in_proj(x2, g1, w_qkv, wg, w_h, w_b, w_c, conv_w[l],
                                                         batch, seq_len)

        w1 = jnp.stack([w_cmp_k1[l], w_cmp_v1[l]]).astype(BF16)
        w2 = jnp.stack([w_cmp_k2[l], w_cmp_v2[l]]).astype(BF16)
        cmp, cmp_t = _compress(c_in, pe_cmp[l], w1, w2, batch, seq_len)

        o_attn = _nsa(q, k_all, vt_all, cmp, cmp_t, gates, band_t, tabs_t, ovl, e_t, batch, seq_len, tq=tq)

        x1, h2 = _oproj(o_attn, o_conv, w_o[l].astype(BF16), x2,
                        g_post_mix[l].reshape(1, D_MODEL), g_pre_ffn[l].reshape(1, D_MODEL))
        x2 = _ffn(h2, w_up[l].astype(BF16), w_down[l].astype(BF16), x1, g_post_ffn[l].reshape(1, D_MODEL))
    return x2.reshape(batch, seq_len, D_MODEL)
```

```python
import functools
import math

import numpy as np
import jax
import jax.numpy as jnp
from jax import lax
from jax.experimental import pallas as pl
from jax.experimental.pallas import tpu as pltpu

F32 = jnp.float32
BF16 = jnp.bfloat16

D_MODEL = 2048
N_HEADS = 8
N_KV = 2
HPG = N_HEADS // N_KV
HEAD_DIM = 128
ATTN_WIDTH = N_HEADS * HEAD_DIM
KV_WIDTH = N_KV * HEAD_DIM
CONV_WIDTH = D_MODEL - ATTN_WIDTH
CONV_K = 3
N_BRANCH = 3
CMP_LEN = 32
CMP_STRIDE = 16
CMP_HIDDEN = 256
SEL_LEN = 64
SEL_TOPK = 16
WINDOW = 512
N_BUCKETS = 32
MAX_DIST = 128
EPS = 1e-6
NEG = -1e30
HALF_NEG = -5e29
FORCE = 1e9

QKV_WIDTH = ATTN_WIDTH + 6 * KV_WIDTH
GATE_OFF = QKV_WIDTH
CONV_OFF = QKV_WIDTH + N_HEADS * N_BRANCH
LANE = 128
VMEM_LIMIT = 56 * 1024 * 1024

_DN_T = (((1,), (1,)), ((), ()))


def _rms(x, g):
    ms = jnp.mean(x * x, axis=-1, keepdims=True)
    return x * lax.rsqrt(ms + EPS) * g


def _params(n_axes):
    return pltpu.CompilerParams(dimension_semantics=("arbitrary",) * n_axes, vmem_limit_bytes=VMEM_LIMIT)


QKV_BLOCK = 2 * KV_WIDTH
N_Q_BLOCKS = ATTN_WIDTH // QKV_BLOCK
BLK_K, BLK_C, BLK_V = N_Q_BLOCKS, N_Q_BLOCKS + 1, N_Q_BLOCKS + 2
CONV_BLOCK = 512


def _inproj_kernel(x_ref, g_ref, w_ref, wg_ref, wh_ref, wb_ref, wc_ref, cw_ref,
                   q_ref, k_ref, c_ref, vt_ref, gate_ref, ov_ref, h_ref, carry_ref, *, q_scale, tiles_per_seq):
    i = pl.program_id(0)
    tm = x_ref.shape[0]
    h_ref[...] = _rms(x_ref[...], g_ref[...]).astype(BF16)
    gate_ref[...] = jax.nn.sigmoid(jnp.dot(h_ref[...], wg_ref[...], preferred_element_type=F32))

    def proj(w, j, width):
        return jnp.dot(h_ref[...], w[:, j * width:(j + 1) * width], preferred_element_type=F32)

    for j in range(N_Q_BLOCKS):
        q_ref[:, j * QKV_BLOCK:(j + 1) * QKV_BLOCK] = (proj(w_ref, j, QKV_BLOCK) * q_scale).astype(BF16)
    k_ref[...] = proj(w_ref, BLK_K, QKV_BLOCK).astype(BF16)
    c_ref[...] = proj(w_ref, BLK_C, QKV_BLOCK)
    v = proj(w_ref, BLK_V, QKV_BLOCK)
    ones = jnp.where(lax.broadcasted_iota(jnp.int32, (ONES_ROWS, tm), 0) == 0, 1.0, 0.0).astype(BF16)
    for s in range(QKV_BLOCK // HEAD_DIM):
        vt_ref[s, 0:HEAD_DIM, :] = v[:, s * HEAD_DIM:(s + 1) * HEAD_DIM].T.astype(BF16)
        vt_ref[s, HEAD_DIM:HEAD_DIM + ONES_ROWS, :] = ones

    row = lax.broadcasted_iota(jnp.int32, (tm, CONV_BLOCK), 0)
    for j in range(CONV_WIDTH // CONV_BLOCK):
        cols = slice(j * CONV_BLOCK, (j + 1) * CONV_BLOCK)
        u = proj(wc_ref, j, CONV_BLOCK) * proj(wh_ref, j, CONV_BLOCK)
        prev = carry_ref[j]
        prev = jnp.where(i % tiles_per_seq == 0, 0.0, prev)
        carry_ref[j] = u[tm - 8:tm, :]
        u1 = jnp.where(row == 0, prev[7:8, :], pltpu.roll(u, 1, axis=0))
        u2 = jnp.where(row == 0, prev[6:7, :], jnp.where(row == 1, prev[7:8, :], pltpu.roll(u, 2, axis=0)))
        w = cw_ref[:, cols]
        y = w[0:1, :] * u2
        y = y + w[1:2, :] * u1
        y = y + w[2:3, :] * u
        ov_ref[:, cols] = (proj(wb_ref, j, CONV_BLOCK) * y).astype(BF16)


def _in_proj(x2, g, w_qkv, w_gate, w_h, w_b, w_c, conv_w, batch, seq_len, *, tm=512):
    m = x2.shape[0]
    gw = w_gate.shape[1]
    tps = seq_len // tm
    n_slabs = QKV_BLOCK // HEAD_DIM
    kern = functools.partial(_inproj_kernel, q_scale=HEAD_DIM ** -0.5 * math.log2(math.e), tiles_per_seq=tps)
    row = lambda i: (i, 0)

    def resident(shape):
        return pl.BlockSpec(shape, lambda i: (0, 0), pipeline_mode=pl.Buffered(1))

    return pl.pallas_call(
        kern,
        grid=(m // tm,),
        in_specs=[
            pl.BlockSpec((tm, D_MODEL), row),
            resident((1, D_MODEL)),
            resident(w_qkv.shape),
            resident(w_gate.shape),
            resident(w_h.shape), resident(w_b.shape), resident(w_c.shape),
            resident(conv_w.shape),
        ],
        out_specs=[
            pl.BlockSpec((tm, ATTN_WIDTH), row),
            pl.BlockSpec((tm, QKV_BLOCK), row),
            pl.BlockSpec((tm, QKV_BLOCK), row),
            pl.BlockSpec((None, n_slabs, HEAD_DIM + ONES_ROWS, tm), lambda i: (i // tps, 0, 0, i % tps)),
            pl.BlockSpec((tm, gw), row),
            pl.BlockSpec((tm, CONV_WIDTH), row),
        ],
        out_shape=[
            jax.ShapeDtypeStruct((m, ATTN_WIDTH), BF16),
            jax.ShapeDtypeStruct((m, QKV_BLOCK), BF16),
            jax.ShapeDtypeStruct((m, QKV_BLOCK), F32),
            jax.ShapeDtypeStruct((batch, n_slabs, HEAD_DIM + ONES_ROWS, seq_len), BF16),
            jax.ShapeDtypeStruct((m, gw), F32),
            jax.ShapeDtypeStruct((m, CONV_WIDTH), BF16),
        ],
        scratch_shapes=[pltpu.VMEM((tm, D_MODEL), BF16),
                        pltpu.VMEM((CONV_WIDTH // CONV_BLOCK, 8, CONV_BLOCK), F32)],
        compiler_params=_params(1),
        name="in_proj",
    )(x2, g, w_qkv, w_gate, w_h, w_b, w_c, conv_w)


def _compress_kernel(x_ref, pe_ref, w1_ref, w2_ref, o_ref, ot_ref):
    n = x_ref.shape[0] // CMP_STRIDE
    a = jnp.zeros((n, CMP_HIDDEN), F32)
    b = jnp.zeros((n, CMP_HIDDEN), F32)
    for l in range(CMP_STRIDE):
        xl = x_ref[pl.ds(l, n, stride=CMP_STRIDE), :]
        xa = (xl + pe_ref[l:l + 1, :]).astype(BF16)
        xb = (xl + pe_ref[CMP_STRIDE + l:CMP_STRIDE + l + 1, :]).astype(BF16)
        a = a + jnp.dot(xa, w1_ref[l], preferred_element_type=F32)
        b = b + jnp.dot(xb, w1_ref[CMP_STRIDE + l], preferred_element_type=F32)
    pre = a + pltpu.roll(b, n - 1, axis=0)
    hid = pre * jax.nn.sigmoid(pre)
    out = jnp.dot(hid.astype(BF16), w2_ref[...], preferred_element_type=F32)
    row = lax.broadcasted_iota(jnp.int32, out.shape, 0)
    out = jnp.where(row < n - 1, out, 0.0)
    o_ref[...] = out.astype(BF16)
    ot_ref[...] = out.T.astype(BF16)


def _compress(c_in, pe, w1, w2, batch, seq_len):
    n_slabs = c_in.shape[1] // HEAD_DIM
    n_chunks = seq_len // CMP_STRIDE
    return pl.pallas_call(
        _compress_kernel,
        grid=(batch, n_slabs),
        in_specs=[
            pl.BlockSpec((seq_len, HEAD_DIM), lambda i, j: (i, j)),
            pl.BlockSpec((CMP_LEN, HEAD_DIM), lambda i, j: (0, 0)),
            pl.BlockSpec((None, CMP_LEN, HEAD_DIM, CMP_HIDDEN), lambda i, j: (j // N_KV, 0, 0, 0)),
            pl.BlockSpec((None, CMP_HIDDEN, HEAD_DIM), lambda i, j: (j // N_KV, 0, 0)),
        ],
        out_specs=[
            pl.BlockSpec((None, None, n_chunks, HEAD_DIM), lambda i, j: (i, j, 0, 0)),
            pl.BlockSpec((None, None, HEAD_DIM, n_chunks), lambda i, j: (i, j, 0, 0)),
        ],
        out_shape=[
            jax.ShapeDtypeStruct((batch, n_slabs, n_chunks, HEAD_DIM), BF16),
            jax.ShapeDtypeStruct((batch, n_slabs, HEAD_DIM, n_chunks), BF16),
        ],
        compiler_params=_params(2),
        name="compress",
    )(c_in, pe, w1, w2)


TAB_DIAG, TAB_SUB, TAB_FAR = 0, 1, 2
BAND_BELOW = 8
ONES_ROWS = 16


N_NSA_INPUTS = 12


def _nsa_kernel(*refs, tq, n_sel, n_cast):
    (q_ref, kc_ref, vct_ref, ks_ref, vst_ref, kw_ref, vwt_ref, gate_ref, band_ref, tab_ref,
     ovl_ref, et_ref) = refs[:N_NSA_INPUTS]
    cast_in = refs[N_NSA_INPUTS:N_NSA_INPUTS + n_cast]
    o_ref = refs[N_NSA_INPUTS + n_cast]
    cast_out = refs[N_NSA_INPUTS + n_cast + 1:N_NSA_INPUTS + 2 * n_cast + 1]
    kaug_ref, sc_ref, s0_ref, s1_ref, m_ref, acc_ref = refs[N_NSA_INPUTS + 2 * n_cast + 1:]
    i = pl.program_id(2)

    for src, dst in zip(cast_in, cast_out):
        dst[...] = src[...].astype(BF16)

    tk = tq
    mcols = HPG * tq
    nw = WINDOW // tk
    cpt = tq // CMP_STRIDE
    band_rows = BAND_BELOW + cpt

    @pl.when(i == 0)
    def _():
        kaug_ref[:, 0:HEAD_DIM] = ks_ref[...]
        kaug_ref[:, HEAD_DIM:2 * HEAD_DIM] = et_ref[...]

    q = q_ref[...]
    qs = jnp.concatenate([q[:, h * HEAD_DIM:(h + 1) * HEAD_DIM] for h in range(HPG)], axis=0)

    raw = lax.dot_general(kc_ref[...], qs, _DN_T, preferred_element_type=F32)
    crow = lax.broadcasted_iota(jnp.int32, raw.shape, 0)
    sc_ref[...] = jnp.where(crow < cpt * (i + 1), raw, NEG)
    band = pl.ds(pl.multiple_of(jnp.maximum(cpt * i - BAND_BELOW, 0), 8), band_rows)
    sc_ref[band, :] += band_ref[jnp.minimum(i, 1)]
    sc = sc_ref[...]
    mc = jnp.maximum(jnp.max(sc, axis=0, keepdims=True), HALF_NEG)
    pc = jnp.exp2(sc - mc)
    lc = jnp.sum(pc, axis=0, keepdims=True)
    pc = pc * jnp.where(lc > 0.0, 1.0 / lc, 0.0)
    o_c = jnp.dot(vct_ref[...], pc.astype(BF16), preferred_element_type=F32)

    ps = pc[:, 0:tq] + pc[:, tq:2 * tq] + pc[:, 2 * tq:3 * tq] + pc[:, 3 * tq:4 * tq]
    hi = ps.astype(BF16)
    r1 = ps - hi.astype(F32)
    mid = r1.astype(BF16)
    lo = (r1 - mid.astype(F32)).astype(BF16)
    ovl = ovl_ref[...]
    imp = (jnp.dot(ovl, hi, preferred_element_type=F32) + jnp.dot(ovl, mid, preferred_element_type=F32)
           + jnp.dot(ovl, lo, preferred_element_type=F32))
    jj = lax.broadcasted_iota(jnp.int32, (n_sel, tq), 0)
    tt = i * tq + lax.broadcasted_iota(jnp.int32, (n_sel, tq), 1)
    cur = tt >> int(math.log2(SEL_LEN))
    forced = (jj == 0) | (jj == cur) | (jj == cur - 1)
    imp = jnp.where(forced, FORCE, imp)
    imp = jnp.where(jj * SEL_LEN <= tt, imp, NEG)
    sub = 8
    ranks = []
    for j0 in range(0, n_sel, sub):
        blk = imp[j0:j0 + sub, :]
        jl = j0 + lax.broadcasted_iota(jnp.int32, blk.shape, 0)
        cnt = jnp.zeros(blk.shape, jnp.int32)
        for b in range(n_sel):
            row = imp[b:b + 1, :]
            if b < j0:
                cnt = cnt + jnp.where(row >= blk, 1, 0)
            elif b >= j0 + sub:
                cnt = cnt + jnp.where(row > blk, 1, 0)
            else:
                cnt = cnt + jnp.where(row > blk, 1, jnp.where(row == blk, jnp.where(jl > b, 1, 0), 0))
        ranks.append(cnt)
    rank = jnp.concatenate(ranks, axis=0)
    selb_t = jnp.where(rank < SEL_TOPK, 0.0, NEG)
    selb = jnp.concatenate([selb_t, jnp.zeros((LANE - n_sel, tq), F32)], axis=0).T
    selb = selb.astype(BF16)
    qa = jnp.concatenate([qs, jnp.concatenate([selb] * HPG, axis=0)], axis=1)

    s_refs = (s0_ref, s1_ref)

    def init():
        m_ref[...] = jnp.full((1, mcols), NEG, F32)
        acc_ref[...] = jnp.zeros(acc_ref.shape, F32)

    def keys(kt):
        return pl.ds(pl.multiple_of(kt * tk, tk), tk)

    def qk(qmat, k_ref, kt, kind, buf):
        s = lax.dot_general(k_ref[keys(kt), :], qmat, _DN_T, preferred_element_type=F32)
        if kind is not None:
            s = s + tab_ref[kind]
        s_refs[buf][...] = s
        return jnp.max(s, axis=0, keepdims=True)

    def process(vt_ref, kt, buf, m_tile):
        m_prev = m_ref[...]
        m_next = jnp.maximum(m_prev, m_tile)
        alpha = jnp.exp2(m_prev - m_next)
        p = jnp.exp2(s_refs[buf][...] - m_next).astype(BF16)
        acc_ref[...] = alpha * acc_ref[...] + jnp.dot(vt_ref[:, keys(kt)], p, preferred_element_type=F32)
        m_ref[...] = m_next

    def finish():
        return acc_ref[0:HEAD_DIM, :] * (1.0 / acc_ref[HEAD_DIM:HEAD_DIM + 1, :])

    init()

    @pl.when(i == 0)
    def _():
        process(vst_ref, 0, 0, qk(qa, kaug_ref, 0, TAB_DIAG, 0))

    @pl.when(i >= 1)
    def _():
        mt0 = qk(qa, kaug_ref, i, TAB_DIAG, 0)
        mt1 = qk(qa, kaug_ref, i - 1, TAB_SUB, 1)
        process(vst_ref, i, 0, mt0)
        mt0 = qk(qa, kaug_ref, jnp.maximum(i - 2, 0), None, 0)
        process(vst_ref, i - 1, 1, mt1)
        n_far = i - 1

        def far_pair(p, mt0):
            a = i - 2 - 2 * p
            mt1 = qk(qa, kaug_ref, a - 1, None, 1)
            process(vst_ref, a, 0, mt0)
            mt0 = qk(qa, kaug_ref, jnp.maximum(a - 2, 0), None, 0)
            process(vst_ref, a - 1, 1, mt1)
            return mt0

        mt0 = lax.fori_loop(0, n_far // 2, far_pair, mt0)

        @pl.when(n_far % 2 == 1)
        def _():
            process(vst_ref, 0, 0, mt0)

    o_s = finish()

    init()
    for avail in range(nw + 1):

        @pl.when((i == avail) if avail < nw else (i >= nw))
        def _(avail=avail):
            kinds = [TAB_DIAG if d == 0 else TAB_SUB if d == 1 else TAB_FAR if d == nw else None
                     for d in range(avail + 1)]
            mt = qk(qs, kw_ref, i, kinds[0], 0)
            for d in range(avail + 1):
                mt_next = qk(qs, kw_ref, i - d - 1, kinds[d + 1], (d + 1) % 2) if d < avail else None
                process(vwt_ref, i - d, d % 2, mt)
                mt = mt_next

    o_w = finish()

    gt = gate_ref[...].T
    for h in range(HPG):
        sl = slice(h * tq, (h + 1) * tq)
        o = gt[3 * h:3 * h + 1, :] * o_c[:, sl]
        o = o + gt[3 * h + 1:3 * h + 2, :] * o_s[:, sl]
        o = o + gt[3 * h + 2:3 * h + 3, :] * o_w[:, sl]
        o_ref[:, h * HEAD_DIM:(h + 1) * HEAD_DIM] = o.T.astype(BF16)


def _nsa(q, k_all, vt_all, cmp, cmp_t, gates, band_t, tabs_t, ovl, e_t, cast_weights, batch, seq_len, *, tq):
    n_t = seq_len // tq
    n_steps = batch * N_KV * n_t
    step_row = lambda b, g, i: ((b * N_KV + g) * n_t + i, 0)
    cast_specs = [pl.BlockSpec((w.shape[0] // n_steps, w.shape[1]), step_row) for w in cast_weights]
    n_sel = seq_len // SEL_LEN
    n_chunks = cmp.shape[2]
    mcols = HPG * tq
    band_rows = band_t.shape[2]
    qw = HPG * HEAD_DIM

    def vt_spec(first):
        return pl.BlockSpec((None, None, HEAD_DIM + ONES_ROWS, seq_len), lambda b, g, i: (b, first + g, 0, 0))

    kern = functools.partial(_nsa_kernel, tq=tq, n_sel=n_sel, n_cast=len(cast_weights))
    o_attn, *cast = pl.pallas_call(
        kern,
        grid=(batch, N_KV, n_t),
        in_specs=[
            pl.BlockSpec((tq, qw), lambda b, g, i: (b * n_t + i, g)),
            pl.BlockSpec((None, None, n_chunks, HEAD_DIM), lambda b, g, i: (b, g, 0, 0)),
            pl.BlockSpec((None, None, HEAD_DIM, n_chunks), lambda b, g, i: (b, N_KV + g, 0, 0)),
            pl.BlockSpec((seq_len, HEAD_DIM), lambda b, g, i: (b, g)),
            vt_spec(0),
            pl.BlockSpec((seq_len, HEAD_DIM), lambda b, g, i: (b, N_KV + g)),
            vt_spec(N_KV),
            pl.BlockSpec((tq, LANE), lambda b, g, i: (b * n_t + i, g)),
            pl.BlockSpec((None, 2, band_rows, mcols), lambda b, g, i: (g, 0, 0, 0)),
            pl.BlockSpec((None, 3, tq, mcols), lambda b, g, i: (g, 0, 0, 0)),
            pl.BlockSpec((n_sel, n_chunks), lambda b, g, i: (0, 0)),
            pl.BlockSpec((seq_len, LANE), lambda b, g, i: (0, 0)),
        ] + cast_specs,
        out_specs=[pl.BlockSpec((tq, qw), lambda b, g, i: (b * n_t + i, g))] + cast_specs,
        out_shape=[jax.ShapeDtypeStruct((batch * seq_len, ATTN_WIDTH), BF16)]
        + [jax.ShapeDtypeStruct(w.shape, BF16) for w in cast_weights],
        scratch_shapes=[
            pltpu.VMEM((seq_len, 2 * HEAD_DIM), BF16),
            pltpu.VMEM((n_chunks, mcols), F32),
            pltpu.VMEM((tq, mcols), F32),
            pltpu.VMEM((tq, mcols), F32),
            pltpu.VMEM((1, mcols), F32),
            pltpu.VMEM((HEAD_DIM + ONES_ROWS, mcols), F32),
        ],
        compiler_params=_params(3),
        name="nsa_attention",
    )(q, cmp, cmp_t, k_all, vt_all, k_all, vt_all, gates, band_t, tabs_t, ovl, e_t, *cast_weights)
    return o_attn, cast


def _oproj_kernel(oa_ref, ov_ref, wo_ref, x_ref, g1_ref, g2_ref, x1_ref, h2_ref):
    ka = oa_ref.shape[1]
    tm = oa_ref.shape[0]
    for rows in (slice(0, tm // 2), slice(tm // 2, tm)):
        mix = jnp.dot(oa_ref[rows, :], wo_ref[0:ka, :], preferred_element_type=F32)
        mix = mix + jnp.dot(ov_ref[rows, :], wo_ref[ka:, :], preferred_element_type=F32)
        x1 = x_ref[rows, :] + _rms(mix, g1_ref[...])
        x1_ref[rows, :] = x1
        h2_ref[rows, :] = _rms(x1, g2_ref[...]).astype(BF16)


def _oproj(o_attn, o_conv, w_o, x2, g_post, g_pre, *, tm=512):
    m = x2.shape[0]
    ka, kv = o_attn.shape[1], o_conv.shape[1]
    row = lambda i: (i, 0)
    fixed = lambda i: (0, 0)
    return pl.pallas_call(
        _oproj_kernel,
        grid=(m // tm,),
        in_specs=[
            pl.BlockSpec((tm, ka), row),
            pl.BlockSpec((tm, kv), row),
            pl.BlockSpec((ka + kv, D_MODEL), fixed),
            pl.BlockSpec((tm, D_MODEL), row),
            pl.BlockSpec((1, D_MODEL), fixed),
            pl.BlockSpec((1, D_MODEL), fixed),
        ],
        out_specs=[pl.BlockSpec((tm, D_MODEL), row), pl.BlockSpec((tm, D_MODEL), row)],
        out_shape=[jax.ShapeDtypeStruct((m, D_MODEL), F32), jax.ShapeDtypeStruct((m, D_MODEL), BF16)],
        compiler_params=_params(1),
        name="out_proj",
    )(o_attn, o_conv, w_o, x2, g_post, g_pre)


def _ffn_kernel(h_ref, wu_ref, wd_ref, x1_ref, g_ref, o_ref, acc_ref):
    j = pl.program_id(1)

    @pl.when(j == 0)
    def _():
        acc_ref[...] = jnp.zeros(acc_ref.shape, F32)

    a = jnp.dot(h_ref[...], wu_ref[...], preferred_element_type=F32)
    a = jnp.square(jnp.maximum(a, 0.0)).astype(BF16)
    acc_ref[...] += jnp.dot(a, wd_ref[...], preferred_element_type=F32)

    @pl.when(j == pl.num_programs(1) - 1)
    def _():
        o_ref[...] = x1_ref[...] + _rms(acc_ref[...], g_ref[...])


def _ffn(h2, w_up, w_down, x1, g_post, *, tm=512, tf=1024):
    m = h2.shape[0]
    d_ff = w_up.shape[1]
    return pl.pallas_call(
        _ffn_kernel,
        grid=(m // tm, d_ff // tf),
        in_specs=[
            pl.BlockSpec((tm, D_MODEL), lambda i, j: (i, 0)),
            pl.BlockSpec((D_MODEL, tf), lambda i, j: (0, j)),
            pl.BlockSpec((tf, D_MODEL), lambda i, j: (j, 0)),
            pl.BlockSpec((tm, D_MODEL), lambda i, j: (i, 0)),
            pl.BlockSpec((1, D_MODEL), lambda i, j: (0, 0)),
        ],
        out_specs=pl.BlockSpec((tm, D_MODEL), lambda i, j: (i, 0)),
        out_shape=jax.ShapeDtypeStruct((m, D_MODEL), F32),
        scratch_shapes=[pltpu.VMEM((tm, D_MODEL), F32)],
        compiler_params=_params(2),
        name="ffn",
    )(h2, w_up, w_down, x1, g_post)


def _bucket_np(dist):
    n = np.maximum(dist, 0)
    max_exact = N_BUCKETS // 2
    nf = np.maximum(n, 1).astype(np.float32)
    large = max_exact + (np.log(nf / np.float32(max_exact)) / np.float32(math.log(MAX_DIST / max_exact))
                         * np.float32(N_BUCKETS - max_exact)).astype(np.int32)
    large = np.minimum(large, N_BUCKETS - 1)
    return np.where(n < max_exact, n, large)


def _bucket_starts():
    b = _bucket_np(np.arange(4 * MAX_DIST))
    return [int(np.argmax(b == k)) for k in range(N_BUCKETS)]


def _bias_by_distance(rel_bias, dist):
    starts = _bucket_starts()
    d = jnp.asarray(dist, jnp.int32)[None]
    col = lambda k: rel_bias[:, k].reshape((N_HEADS,) + (1,) * dist.ndim)
    out = jnp.broadcast_to(col(0), (N_HEADS,) + dist.shape)
    for k in range(1, N_BUCKETS):
        out = jnp.where(d >= starts[k], col(k), out)
    return out


def _attention_tables(rel_bias, seq_len, tq):
    starts = _bucket_starts()
    assert starts[N_BUCKETS - 1] <= CMP_STRIDE * (BAND_BELOW + 1) - (CMP_LEN - 1)
    assert starts[N_BUCKETS - 1] <= tq
    rel = (rel_bias - rel_bias[:, N_BUCKETS - 1:]) * math.log2(math.e)

    def per_group(tab):
        lead = tab.shape[1:-2]
        nk, nq = tab.shape[-2:]
        tab = tab.reshape((N_KV, HPG) + lead + (nk, nq))
        tab = jnp.moveaxis(tab, 1, -2)
        return tab.reshape((N_KV,) + lead + (nk, HPG * nq)).astype(F32)

    c = np.arange(tq)[:, None]
    r = np.arange(tq)[None, :]
    diag = jnp.where(jnp.asarray(r >= c)[None], _bias_by_distance(rel, np.maximum(r - c, 0)), NEG)
    sub = _bias_by_distance(rel, tq + r - c)
    far = jnp.broadcast_to(jnp.where(jnp.asarray(r < c)[None], 0.0, NEG), (N_HEADS, tq, tq))
    tabs = per_group(jnp.stack([diag, sub, far], axis=1))

    cpt = tq // CMP_STRIDE
    cl = np.arange(BAND_BELOW + cpt)[:, None]
    bands = []
    for first in (0, -BAND_BELOW):
        dist = r - ((cl + first) * CMP_STRIDE + CMP_LEN - 1)
        bands.append(jnp.where(jnp.asarray(dist >= 0)[None], _bias_by_distance(rel, np.maximum(dist, 0)), NEG))
    band = per_group(jnp.stack(bands, axis=1))

    n_chunks = seq_len // CMP_STRIDE
    n_cmp = (seq_len - CMP_LEN) // CMP_STRIDE + 1
    n_sel = seq_len // SEL_LEN
    ci = np.arange(n_chunks)[None, :] * CMP_STRIDE
    sj = np.arange(n_sel)[:, None] * SEL_LEN
    ovl = ((ci < sj + SEL_LEN) & (ci + CMP_LEN > sj) & (np.arange(n_chunks)[None, :] < n_cmp))
    e_t = (np.arange(seq_len)[:, None] // SEL_LEN == np.arange(LANE)[None, :])
    return tabs, band, jnp.asarray(ovl, BF16), jnp.asarray(e_t, BF16)


def kernel(x, w_in, pe_cmp, w_cmp_k1, w_cmp_k2, w_cmp_v1, w_cmp_v2, conv_w, rel_bias, w_o, w_up, w_down,
           g_pre_mix, g_post_mix, g_pre_ffn, g_post_ffn):
    batch, seq_len, _ = x.shape
    depth = w_in.shape[0]
    tq = 256
    tabs_t, band_t, ovl, e_t = _attention_tables(rel_bias, seq_len, tq)
    x2 = x.reshape(batch * seq_len, D_MODEL)
    for l in range(depth):
        wl = w_in[l]
        slab = lambda k: wl[:, ATTN_WIDTH + k * KV_WIDTH:ATTN_WIDTH + (k + 1) * KV_WIDTH]
        w_qkv = jnp.concatenate([wl[:, :ATTN_WIDTH], slab(2), slab(4), slab(0), slab(1), slab(3), slab(5)],
                                axis=1).astype(BF16)
        wg = wl[:, GATE_OFF:CONV_OFF].reshape(D_MODEL, N_KV, HPG * N_BRANCH)
        wg = jnp.pad(wg, ((0, 0), (0, 0), (0, LANE - HPG * N_BRANCH))).reshape(D_MODEL, N_KV * LANE).astype(BF16)
        w_h, w_b, w_c = (wl[:, CONV_OFF + k * CONV_WIDTH:CONV_OFF + (k + 1) * CONV_WIDTH].astype(BF16)
                         for k in range(3))
        g1 = g_pre_mix[l].reshape(1, D_MODEL)

        q, k_all, c_in, vt_all, gates, o_conv = _in_proj(x2, g1, w_qkv, wg, w_h, w_b, w_c, conv_w[l],
                                                         batch, seq_len)

        w1 = jnp.stack([w_cmp_k1[l], w_cmp_v1[l]]).astype(BF16)
        w2 = jnp.stack([w_cmp_k2[l], w_cmp_v2[l]]).astype(BF16)
        cmp, cmp_t = _compress(c_in, pe_cmp[l], w1, w2, batch, seq_len)

        o_attn, (wo_b, wup_b, wdown_b) = _nsa(q, k_all, vt_all, cmp, cmp_t, gates, band_t, tabs_t, ovl, e_t,
                                              [w_o[l], w_up[l], w_down[l]], batch, seq_len, tq=tq)

        x1, h2 = _oproj(o_attn, o_conv, wo_b, x2,
                        g_post_mix[l].reshape(1, D_MODEL), g_pre_ffn[l].reshape(1, D_MODEL))
        x2 = _ffn(h2, wup_b, wdown_b, x1, g_post_ffn[l].reshape(1, D_MODEL))
    return x2.reshape(batch, seq_len, D_MODEL)
```

```python
import functools
import math

import numpy as np
import jax
import jax.numpy as jnp
from jax import lax
from jax.experimental import pallas as pl
from jax.experimental.pallas import tpu as pltpu

F32 = jnp.float32
BF16 = jnp.bfloat16

D_MODEL = 2048
N_HEADS = 8
N_KV = 2
HPG = N_HEADS // N_KV
HEAD_DIM = 128
ATTN_WIDTH = N_HEADS * HEAD_DIM
KV_WIDTH = N_KV * HEAD_DIM
CONV_WIDTH = D_MODEL - ATTN_WIDTH
CONV_K = 3
N_BRANCH = 3
CMP_LEN = 32
CMP_STRIDE = 16
CMP_HIDDEN = 256
SEL_LEN = 64
SEL_TOPK = 16
WINDOW = 512
N_BUCKETS = 32
MAX_DIST = 128
EPS = 1e-6
NEG = -1e30
HALF_NEG = -5e29
FORCE = 1e9

QKV_WIDTH = ATTN_WIDTH + 6 * KV_WIDTH
GATE_OFF = QKV_WIDTH
CONV_OFF = QKV_WIDTH + N_HEADS * N_BRANCH
LANE = 128
VMEM_LIMIT = 56 * 1024 * 1024

_DN_T = (((1,), (1,)), ((), ()))


def _rms(x, g):
    ms = jnp.mean(x * x, axis=-1, keepdims=True)
    return x * lax.rsqrt(ms + EPS) * g


def _params(n_axes):
    return pltpu.CompilerParams(dimension_semantics=("arbitrary",) * n_axes, vmem_limit_bytes=VMEM_LIMIT)


QKV_BLOCK = 2 * KV_WIDTH
N_Q_BLOCKS = ATTN_WIDTH // QKV_BLOCK
BLK_K, BLK_C, BLK_V = N_Q_BLOCKS, N_Q_BLOCKS + 1, N_Q_BLOCKS + 2
CONV_BLOCK = 512


def _inproj_kernel(x_ref, g_ref, w_ref, wg_ref, wh_ref, wb_ref, wc_ref, cw_ref,
                   q_ref, k_ref, c_ref, vt_ref, gate_ref, ov_ref, h_ref, carry_ref, *, q_scale, tiles_per_seq):
    i = pl.program_id(0)
    tm = x_ref.shape[0]
    h_ref[...] = _rms(x_ref[...], g_ref[...]).astype(BF16)
    gate_ref[...] = jax.nn.sigmoid(jnp.dot(h_ref[...], wg_ref[...], preferred_element_type=F32))

    def proj(w, j, width):
        return jnp.dot(h_ref[...], w[:, j * width:(j + 1) * width], preferred_element_type=F32)

    for j in range(N_Q_BLOCKS):
        q_ref[:, j * QKV_BLOCK:(j + 1) * QKV_BLOCK] = (proj(w_ref, j, QKV_BLOCK) * q_scale).astype(BF16)
    k_ref[...] = proj(w_ref, BLK_K, QKV_BLOCK).astype(BF16)
    c_ref[...] = proj(w_ref, BLK_C, QKV_BLOCK)
    v = proj(w_ref, BLK_V, QKV_BLOCK)
    ones = jnp.where(lax.broadcasted_iota(jnp.int32, (ONES_ROWS, tm), 0) == 0, 1.0, 0.0).astype(BF16)
    for s in range(QKV_BLOCK // HEAD_DIM):
        vt_ref[s, 0:HEAD_DIM, :] = v[:, s * HEAD_DIM:(s + 1) * HEAD_DIM].T.astype(BF16)
        vt_ref[s, HEAD_DIM:HEAD_DIM + ONES_ROWS, :] = ones

    row = lax.broadcasted_iota(jnp.int32, (tm, CONV_BLOCK), 0)
    for j in range(CONV_WIDTH // CONV_BLOCK):
        cols = slice(j * CONV_BLOCK, (j + 1) * CONV_BLOCK)
        u = proj(wc_ref, j, CONV_BLOCK) * proj(wh_ref, j, CONV_BLOCK)
        prev = carry_ref[j]
        prev = jnp.where(i % tiles_per_seq == 0, 0.0, prev)
        carry_ref[j] = u[tm - 8:tm, :]
        u1 = jnp.where(row == 0, prev[7:8, :], pltpu.roll(u, 1, axis=0))
        u2 = jnp.where(row == 0, prev[6:7, :], jnp.where(row == 1, prev[7:8, :], pltpu.roll(u, 2, axis=0)))
        w = cw_ref[:, cols]
        y = w[0:1, :] * u2
        y = y + w[1:2, :] * u1
        y = y + w[2:3, :] * u
        ov_ref[:, cols] = (proj(wb_ref, j, CONV_BLOCK) * y).astype(BF16)


def _in_proj(x2, g, w_qkv, w_gate, w_h, w_b, w_c, conv_w, batch, seq_len, *, tm=512):
    m = x2.shape[0]
    gw = w_gate.shape[1]
    tps = seq_len // tm
    n_slabs = QKV_BLOCK // HEAD_DIM
    kern = functools.partial(_inproj_kernel, q_scale=HEAD_DIM ** -0.5 * math.log2(math.e), tiles_per_seq=tps)
    row = lambda i: (i, 0)

    def resident(shape):
        return pl.BlockSpec(shape, lambda i: (0, 0), pipeline_mode=pl.Buffered(1))

    return pl.pallas_call(
        kern,
        grid=(m // tm,),
        in_specs=[
            pl.BlockSpec((tm, D_MODEL), row),
            resident((1, D_MODEL)),
            resident(w_qkv.shape),
            resident(w_gate.shape),
            resident(w_h.shape), resident(w_b.shape), resident(w_c.shape),
            resident(conv_w.shape),
        ],
        out_specs=[
            pl.BlockSpec((tm, ATTN_WIDTH), row),
            pl.BlockSpec((tm, QKV_BLOCK), row),
            pl.BlockSpec((tm, QKV_BLOCK), row),
            pl.BlockSpec((None, n_slabs, HEAD_DIM + ONES_ROWS, tm), lambda i: (i // tps, 0, 0, i % tps)),
            pl.BlockSpec((tm, gw), row),
            pl.BlockSpec((tm, CONV_WIDTH), row),
        ],
        out_shape=[
            jax.ShapeDtypeStruct((m, ATTN_WIDTH), BF16),
            jax.ShapeDtypeStruct((m, QKV_BLOCK), BF16),
            jax.ShapeDtypeStruct((m, QKV_BLOCK), F32),
            jax.ShapeDtypeStruct((batch, n_slabs, HEAD_DIM + ONES_ROWS, seq_len), BF16),
            jax.ShapeDtypeStruct((m, gw), F32),
            jax.ShapeDtypeStruct((m, CONV_WIDTH), BF16),
        ],
        scratch_shapes=[pltpu.VMEM((tm, D_MODEL), BF16),
                        pltpu.VMEM((CONV_WIDTH // CONV_BLOCK, 8, CONV_BLOCK), F32)],
        compiler_params=_params(1),
        name="in_proj",
    )(x2, g, w_qkv, w_gate, w_h, w_b, w_c, conv_w)


def _compress_kernel(x_ref, pe_ref, w1_ref, w2_ref, o_ref, ot_ref):
    n = x_ref.shape[0] // CMP_STRIDE
    a = jnp.zeros((n, CMP_HIDDEN), F32)
    b = jnp.zeros((n, CMP_HIDDEN), F32)
    for l in range(CMP_STRIDE):
        xl = x_ref[pl.ds(l, n, stride=CMP_STRIDE), :]
        xa = (xl + pe_ref[l:l + 1, :]).astype(BF16)
        xb = (xl + pe_ref[CMP_STRIDE + l:CMP_STRIDE + l + 1, :]).astype(BF16)
        a = a + jnp.dot(xa, w1_ref[l], preferred_element_type=F32)
        b = b + jnp.dot(xb, w1_ref[CMP_STRIDE + l], preferred_element_type=F32)
    pre = a + pltpu.roll(b, n - 1, axis=0)
    hid = pre * jax.nn.sigmoid(pre)
    out = jnp.dot(hid.astype(BF16), w2_ref[...], preferred_element_type=F32)
    row = lax.broadcasted_iota(jnp.int32, out.shape, 0)
    out = jnp.where(row < n - 1, out, 0.0)
    o_ref[...] = out.astype(BF16)
    ot_ref[...] = out.T.astype(BF16)


def _compress(c_in, pe, w1, w2, batch, seq_len):
    n_slabs = c_in.shape[1] // HEAD_DIM
    n_chunks = seq_len // CMP_STRIDE
    return pl.pallas_call(
        _compress_kernel,
        grid=(batch, n_slabs),
        in_specs=[
            pl.BlockSpec((seq_len, HEAD_DIM), lambda i, j: (i, j)),
            pl.BlockSpec((CMP_LEN, HEAD_DIM), lambda i, j: (0, 0)),
            pl.BlockSpec((None, CMP_LEN, HEAD_DIM, CMP_HIDDEN), lambda i, j: (j // N_KV, 0, 0, 0)),
            pl.BlockSpec((None, CMP_HIDDEN, HEAD_DIM), lambda i, j: (j // N_KV, 0, 0)),
        ],
        out_specs=[
            pl.BlockSpec((None, None, n_chunks, HEAD_DIM), lambda i, j: (i, j, 0, 0)),
            pl.BlockSpec((None, None, HEAD_DIM, n_chunks), lambda i, j: (i, j, 0, 0)),
        ],
        out_shape=[
            jax.ShapeDtypeStruct((batch, n_slabs, n_chunks, HEAD_DIM), BF16),
            jax.ShapeDtypeStruct((batch, n_slabs, HEAD_DIM, n_chunks), BF16),
        ],
        compiler_params=_params(2),
        name="compress",
    )(c_in, pe, w1, w2)


TAB_DIAG, TAB_SUB, TAB_FAR = 0, 1, 2
BAND_BELOW = 8
ONES_ROWS = 16


N_NSA_INPUTS = 11
BIAS_LEN = 1024


def _nsa_kernel(*refs, tq, n_sel, n_cast):
    (q_ref, kc_ref, vct_ref, ks_ref, vst_ref, kw_ref, vwt_ref, gate_ref, fvec_ref,
     ovl_ref, et_ref) = refs[:N_NSA_INPUTS]
    cast_in = refs[N_NSA_INPUTS:N_NSA_INPUTS + n_cast]
    o_ref = refs[N_NSA_INPUTS + n_cast]
    cast_out = refs[N_NSA_INPUTS + n_cast + 1:N_NSA_INPUTS + 2 * n_cast + 1]
    (kaug_ref, tab_ref, band_ref, sc_ref, s0_ref, s1_ref, m_ref,
     acc_ref) = refs[N_NSA_INPUTS + 2 * n_cast + 1:]
    i = pl.program_id(2)

    for src, dst in zip(cast_in, cast_out):
        dst[...] = src[...].astype(BF16)

    tk = tq
    mcols = HPG * tq
    nw = WINDOW // tk
    cpt = tq // CMP_STRIDE
    band_rows = BAND_BELOW + cpt

    @pl.when(i == 0)
    def _():
        kaug_ref[:, 0:HEAD_DIM] = ks_ref[...]
        kaug_ref[:, HEAD_DIM:2 * HEAD_DIM] = et_ref[...]

    @pl.when(i == 0)
    def _():
        c = lax.broadcasted_iota(jnp.int32, (tk, tq), 0)
        r = lax.broadcasted_iota(jnp.int32, (tk, tq), 1)
        far = jnp.where(r < c, 0.0, NEG)
        lane = lax.broadcasted_iota(jnp.int32, (1, BIAS_LEN), 1)
        for h in range(HPG):
            cols = slice(h * tq, (h + 1) * tq)
            f = fvec_ref[h]
            f_diag = jnp.where(lane < tq, f, NEG)
            x = pltpu.roll(jnp.broadcast_to(f_diag, (tk, BIAS_LEN)), 0, 1, stride=1, stride_axis=0)
            tab_ref[TAB_DIAG, :, cols] = x[:, 0:tq]
            x = pltpu.roll(jnp.broadcast_to(f, (tk, BIAS_LEN)), 0, 1, stride=1, stride_axis=0)
            tab_ref[TAB_SUB, :, cols] = x[:, tq:2 * tq]
            tab_ref[TAB_FAR, :, cols] = far
            for v, first in enumerate((0, -BAND_BELOW)):
                shift = (CMP_STRIDE * first + CMP_LEN - 1) % BIAS_LEN
                f_shift = pltpu.roll(f, shift, 1)
                x = pltpu.roll(jnp.broadcast_to(f_shift, (band_rows, BIAS_LEN)), 0, 1,
                               stride=CMP_STRIDE, stride_axis=0)
                band_ref[v, :, cols] = x[:, 0:tq]

    q = q_ref[...]
    qs = jnp.concatenate([q[:, h * HEAD_DIM:(h + 1) * HEAD_DIM] for h in range(HPG)], axis=0)

    raw = lax.dot_general(kc_ref[...], qs, _DN_T, preferred_element_type=F32)
    crow = lax.broadcasted_iota(jnp.int32, raw.shape, 0)
    sc_ref[...] = jnp.where(crow < cpt * (i + 1), raw, NEG)
    band = pl.ds(pl.multiple_of(jnp.maximum(cpt * i - BAND_BELOW, 0), 8), band_rows)
    sc_ref[band, :] += band_ref[jnp.minimum(i, 1)]
    sc = sc_ref[...]
    mc = jnp.maximum(jnp.max(sc, axis=0, keepdims=True), HALF_NEG)
    pc = jnp.exp2(sc - mc)
    lc = jnp.sum(pc, axis=0, keepdims=True)
    pc = pc * jnp.where(lc > 0.0, 1.0 / lc, 0.0)
    o_c = jnp.dot(vct_ref[...], pc.astype(BF16), preferred_element_type=F32)

    ps = pc[:, 0:tq] + pc[:, tq:2 * tq] + pc[:, 2 * tq:3 * tq] + pc[:, 3 * tq:4 * tq]
    hi = ps.astype(BF16)
    r1 = ps - hi.astype(F32)
    mid = r1.astype(BF16)
    lo = (r1 - mid.astype(F32)).astype(BF16)
    ovl = ovl_ref[...]
    imp = (jnp.dot(ovl, hi, preferred_element_type=F32) + jnp.dot(ovl, mid, preferred_element_type=F32)
           + jnp.dot(ovl, lo, preferred_element_type=F32))
    jj = lax.broadcasted_iota(jnp.int32, (n_sel, tq), 0)
    tt = i * tq + lax.broadcasted_iota(jnp.int32, (n_sel, tq), 1)
    cur = tt >> int(math.log2(SEL_LEN))
    forced = (jj == 0) | (jj == cur) | (jj == cur - 1)
    imp = jnp.where(forced, FORCE, imp)
    imp = jnp.where(jj * SEL_LEN <= tt, imp, NEG)
    sub = 8
    ranks = []
    for j0 in range(0, n_sel, sub):
        blk = imp[j0:j0 + sub, :]
        jl = j0 + lax.broadcasted_iota(jnp.int32, blk.shape, 0)
        cnt = jnp.zeros(blk.shape, jnp.int32)
        for b in range(n_sel):
            row = imp[b:b + 1, :]
            if b < j0:
                cnt = cnt + jnp.where(row >= blk, 1, 0)
            elif b >= j0 + sub:
                cnt = cnt + jnp.where(row > blk, 1, 0)
            else:
                cnt = cnt + jnp.where(row > blk, 1, jnp.where(row == blk, jnp.where(jl > b, 1, 0), 0))
        ranks.append(cnt)
    rank = jnp.concatenate(ranks, axis=0)
    selb_t = jnp.where(rank < SEL_TOPK, 0.0, NEG)
    selb = jnp.concatenate([selb_t, jnp.zeros((LANE - n_sel, tq), F32)], axis=0).T
    selb = selb.astype(BF16)
    qa = jnp.concatenate([qs, jnp.concatenate([selb] * HPG, axis=0)], axis=1)

    s_refs = (s0_ref, s1_ref)

    def init():
        m_ref[...] = jnp.full((1, mcols), NEG, F32)
        acc_ref[...] = jnp.zeros(acc_ref.shape, F32)

    def keys(kt):
        return pl.ds(pl.multiple_of(kt * tk, tk), tk)

    def qk(qmat, k_ref, kt, kind, buf):
        s = lax.dot_general(k_ref[keys(kt), :], qmat, _DN_T, preferred_element_type=F32)
        if kind is not None:
            s = s + tab_ref[kind]
        s_refs[buf][...] = s
        return jnp.max(s, axis=0, keepdims=True)

    def process(vt_ref, kt, buf, m_tile):
        m_prev = m_ref[...]
        m_next = jnp.maximum(m_prev, m_tile)
        alpha = jnp.exp2(m_prev - m_next)
        p = jnp.exp2(s_refs[buf][...] - m_next).astype(BF16)
        acc_ref[...] = alpha * acc_ref[...] + jnp.dot(vt_ref[:, keys(kt)], p, preferred_element_type=F32)
        m_ref[...] = m_next

    def finish():
        return acc_ref[0:HEAD_DIM, :] * (1.0 / acc_ref[HEAD_DIM:HEAD_DIM + 1, :])

    init()

    @pl.when(i == 0)
    def _():
        process(vst_ref, 0, 0, qk(qa, kaug_ref, 0, TAB_DIAG, 0))

    @pl.when(i >= 1)
    def _():
        mt0 = qk(qa, kaug_ref, i, TAB_DIAG, 0)
        mt1 = qk(qa, kaug_ref, i - 1, TAB_SUB, 1)
        process(vst_ref, i, 0, mt0)
        mt0 = qk(qa, kaug_ref, jnp.maximum(i - 2, 0), None, 0)
        process(vst_ref, i - 1, 1, mt1)
        n_far = i - 1

        def far_pair(p, mt0):
            a = i - 2 - 2 * p
            mt1 = qk(qa, kaug_ref, a - 1, None, 1)
            process(vst_ref, a, 0, mt0)
            mt0 = qk(qa, kaug_ref, jnp.maximum(a - 2, 0), None, 0)
            process(vst_ref, a - 1, 1, mt1)
            return mt0

        mt0 = lax.fori_loop(0, n_far // 2, far_pair, mt0)

        @pl.when(n_far % 2 == 1)
        def _():
            process(vst_ref, 0, 0, mt0)

    o_s = finish()

    init()
    for avail in range(nw + 1):

        @pl.when((i == avail) if avail < nw else (i >= nw))
        def _(avail=avail):
            kinds = [TAB_DIAG if d == 0 else TAB_SUB if d == 1 else TAB_FAR if d == nw else None
                     for d in range(avail + 1)]
            mt = qk(qs, kw_ref, i, kinds[0], 0)
            for d in range(avail + 1):
                mt_next = qk(qs, kw_ref, i - d - 1, kinds[d + 1], (d + 1) % 2) if d < avail else None
                process(vwt_ref, i - d, d % 2, mt)
                mt = mt_next

    o_w = finish()

    gt = gate_ref[...].T
    for h in range(HPG):
        sl = slice(h * tq, (h + 1) * tq)
        o = gt[3 * h:3 * h + 1, :] * o_c[:, sl]
        o = o + gt[3 * h + 1:3 * h + 2, :] * o_s[:, sl]
        o = o + gt[3 * h + 2:3 * h + 3, :] * o_w[:, sl]
        o_ref[:, h * HEAD_DIM:(h + 1) * HEAD_DIM] = o.T.astype(BF16)


def _nsa(q, k_all, vt_all, cmp, cmp_t, gates, fvec, ovl, e_t, cast_weights, batch, seq_len, *, tq):
    n_t = seq_len // tq
    n_steps = batch * N_KV * n_t
    step_row = lambda b, g, i: ((b * N_KV + g) * n_t + i, 0)
    cast_specs = [pl.BlockSpec((w.shape[0] // n_steps, w.shape[1]), step_row) for w in cast_weights]
    n_sel = seq_len // SEL_LEN
    n_chunks = cmp.shape[2]
    mcols = HPG * tq
    band_rows = BAND_BELOW + tq // CMP_STRIDE
    qw = HPG * HEAD_DIM

    def vt_spec(first):
        return pl.BlockSpec((None, None, HEAD_DIM + ONES_ROWS, seq_len), lambda b, g, i: (b, first + g, 0, 0))

    kern = functools.partial(_nsa_kernel, tq=tq, n_sel=n_sel, n_cast=len(cast_weights))
    o_attn, *cast = pl.pallas_call(
        kern,
        grid=(batch, N_KV, n_t),
        in_specs=[
            pl.BlockSpec((tq, qw), lambda b, g, i: (b * n_t + i, g)),
            pl.BlockSpec((None, None, n_chunks, HEAD_DIM), lambda b, g, i: (b, g, 0, 0)),
            pl.BlockSpec((None, None, HEAD_DIM, n_chunks), lambda b, g, i: (b, N_KV + g, 0, 0)),
            pl.BlockSpec((seq_len, HEAD_DIM), lambda b, g, i: (b, g)),
            vt_spec(0),
            pl.BlockSpec((seq_len, HEAD_DIM), lambda b, g, i: (b, N_KV + g)),
            vt_spec(N_KV),
            pl.BlockSpec((tq, LANE), lambda b, g, i: (b * n_t + i, g)),
            pl.BlockSpec((None, HPG, 1, BIAS_LEN), lambda b, g, i: (g, 0, 0, 0)),
            pl.BlockSpec((n_sel, n_chunks), lambda b, g, i: (0, 0)),
            pl.BlockSpec((seq_len, LANE), lambda b, g, i: (0, 0)),
        ] + cast_specs,
        out_specs=[pl.BlockSpec((tq, qw), lambda b, g, i: (b * n_t + i, g))] + cast_specs,
        out_shape=[jax.ShapeDtypeStruct((batch * seq_len, ATTN_WIDTH), BF16)]
        + [jax.ShapeDtypeStruct(w.shape, BF16) for w in cast_weights],
        scratch_shapes=[
            pltpu.VMEM((seq_len, 2 * HEAD_DIM), BF16),
            pltpu.VMEM((3, tq, mcols), F32),
            pltpu.VMEM((2, band_rows, mcols), F32),
            pltpu.VMEM((n_chunks, mcols), F32),
            pltpu.VMEM((tq, mcols), F32),
            pltpu.VMEM((tq, mcols), F32),
            pltpu.VMEM((1, mcols), F32),
            pltpu.VMEM((HEAD_DIM + ONES_ROWS, mcols), F32),
        ],
        compiler_params=_params(3),
        name="nsa_attention",
    )(q, cmp, cmp_t, k_all, vt_all, k_all, vt_all, gates, fvec, ovl, e_t, *cast_weights)
    return o_attn, cast


def _oproj_kernel(oa_ref, ov_ref, wo_ref, x_ref, g1_ref, g2_ref, x1_ref, h2_ref):
    ka = oa_ref.shape[1]
    tm = oa_ref.shape[0]
    for rows in (slice(0, tm // 2), slice(tm // 2, tm)):
        mix = jnp.dot(oa_ref[rows, :], wo_ref[0:ka, :], preferred_element_type=F32)
        mix = mix + jnp.dot(ov_ref[rows, :], wo_ref[ka:, :], preferred_element_type=F32)
        x1 = x_ref[rows, :] + _rms(mix, g1_ref[...])
        x1_ref[rows, :] = x1
        h2_ref[rows, :] = _rms(x1, g2_ref[...]).astype(BF16)


def _oproj(o_attn, o_conv, w_o, x2, g_post, g_pre, *, tm=512):
    m = x2.shape[0]
    ka, kv = o_attn.shape[1], o_conv.shape[1]
    row = lambda i: (i, 0)
    fixed = lambda i: (0, 0)
    return pl.pallas_call(
        _oproj_kernel,
        grid=(m // tm,),
        in_specs=[
            pl.BlockSpec((tm, ka), row),
            pl.BlockSpec((tm, kv), row),
            pl.BlockSpec((ka + kv, D_MODEL), fixed),
            pl.BlockSpec((tm, D_MODEL), row),
            pl.BlockSpec((1, D_MODEL), fixed),
            pl.BlockSpec((1, D_MODEL), fixed),
        ],
        out_specs=[pl.BlockSpec((tm, D_MODEL), row), pl.BlockSpec((tm, D_MODEL), row)],
        out_shape=[jax.ShapeDtypeStruct((m, D_MODEL), F32), jax.ShapeDtypeStruct((m, D_MODEL), BF16)],
        compiler_params=_params(1),
        name="out_proj",
    )(o_attn, o_conv, w_o, x2, g_post, g_pre)


def _ffn_kernel(h_ref, wu_ref, wd_ref, x1_ref, g_ref, o_ref, acc_ref):
    j = pl.program_id(1)

    @pl.when(j == 0)
    def _():
        acc_ref[...] = jnp.zeros(acc_ref.shape, F32)

    a = jnp.dot(h_ref[...], wu_ref[...], preferred_element_type=F32)
    a = jnp.square(jnp.maximum(a, 0.0)).astype(BF16)
    acc_ref[...] += jnp.dot(a, wd_ref[...], preferred_element_type=F32)

    @pl.when(j == pl.num_programs(1) - 1)
    def _():
        o_ref[...] = x1_ref[...] + _rms(acc_ref[...], g_ref[...])


def _ffn(h2, w_up, w_down, x1, g_post, *, tm=512, tf=1024):
    m = h2.shape[0]
    d_ff = w_up.shape[1]
    return pl.pallas_call(
        _ffn_kernel,
        grid=(m // tm, d_ff // tf),
        in_specs=[
            pl.BlockSpec((tm, D_MODEL), lambda i, j: (i, 0)),
            pl.BlockSpec((D_MODEL, tf), lambda i, j: (0, j)),
            pl.BlockSpec((tf, D_MODEL), lambda i, j: (j, 0)),
            pl.BlockSpec((tm, D_MODEL), lambda i, j: (i, 0)),
            pl.BlockSpec((1, D_MODEL), lambda i, j: (0, 0)),
        ],
        out_specs=pl.BlockSpec((tm, D_MODEL), lambda i, j: (i, 0)),
        out_shape=jax.ShapeDtypeStruct((m, D_MODEL), F32),
        scratch_shapes=[pltpu.VMEM((tm, D_MODEL), F32)],
        compiler_params=_params(2),
        name="ffn",
    )(h2, w_up, w_down, x1, g_post)


def _bucket_np(dist):
    n = np.maximum(dist, 0)
    max_exact = N_BUCKETS // 2
    nf = np.maximum(n, 1).astype(np.float32)
    large = max_exact + (np.log(nf / np.float32(max_exact)) / np.float32(math.log(MAX_DIST / max_exact))
                         * np.float32(N_BUCKETS - max_exact)).astype(np.int32)
    large = np.minimum(large, N_BUCKETS - 1)
    return np.where(n < max_exact, n, large)


def _bucket_starts():
    b = _bucket_np(np.arange(4 * MAX_DIST))
    return [int(np.argmax(b == k)) for k in range(N_BUCKETS)]


def _attention_tables(rel_bias, seq_len, tq):
    starts = _bucket_starts()
    assert starts[N_BUCKETS - 1] <= CMP_STRIDE * (BAND_BELOW + 1) - (CMP_LEN - 1)
    assert starts[N_BUCKETS - 1] <= tq and 2 * tq <= BIAS_LEN // 2
    rel = (rel_bias - rel_bias[:, N_BUCKETS - 1:]) * math.log2(math.e)
    d = jnp.arange(BIAS_LEN, dtype=jnp.int32)[None, :]
    fvec = jnp.broadcast_to(rel[:, 0:1], (N_HEADS, BIAS_LEN))
    for k in range(1, N_BUCKETS):
        fvec = jnp.where(d >= starts[k], rel[:, k:k + 1], fvec)
    fvec = jnp.where(d < BIAS_LEN // 2, fvec, NEG).astype(F32).reshape(N_KV, HPG, 1, BIAS_LEN)

    n_chunks = seq_len // CMP_STRIDE
    n_cmp = (seq_len - CMP_LEN) // CMP_STRIDE + 1
    n_sel = seq_len // SEL_LEN
    ci = np.arange(n_chunks)[None, :] * CMP_STRIDE
    sj = np.arange(n_sel)[:, None] * SEL_LEN
    ovl = ((ci < sj + SEL_LEN) & (ci + CMP_LEN > sj) & (np.arange(n_chunks)[None, :] < n_cmp))
    e_t = (np.arange(seq_len)[:, None] // SEL_LEN == np.arange(LANE)[None, :])
    return fvec, jnp.asarray(ovl, BF16), jnp.asarray(e_t, BF16)


def kernel(x, w_in, pe_cmp, w_cmp_k1, w_cmp_k2, w_cmp_v1, w_cmp_v2, conv_w, rel_bias, w_o, w_up, w_down,
           g_pre_mix, g_post_mix, g_pre_ffn, g_post_ffn):
    batch, seq_len, _ = x.shape
    depth = w_in.shape[0]
    tq = 256
    fvec, ovl, e_t = _attention_tables(rel_bias, seq_len, tq)
    x2 = x.reshape(batch * seq_len, D_MODEL)
    for l in range(depth):
        wl = w_in[l]
        slab = lambda k: wl[:, ATTN_WIDTH + k * KV_WIDTH:ATTN_WIDTH + (k + 1) * KV_WIDTH]
        w_qkv = jnp.concatenate([wl[:, :ATTN_WIDTH], slab(2), slab(4), slab(0), slab(1), slab(3), slab(5)],
                                axis=1).astype(BF16)
        wg = wl[:, GATE_OFF:CONV_OFF].reshape(D_MODEL, N_KV, HPG * N_BRANCH)
        wg = jnp.pad(wg, ((0, 0), (0, 0), (0, LANE - HPG * N_BRANCH))).reshape(D_MODEL, N_KV * LANE).astype(BF16)
        w_h, w_b, w_c = (wl[:, CONV_OFF + k * CONV_WIDTH:CONV_OFF + (k + 1) * CONV_WIDTH].astype(BF16)
                         for k in range(3))
        g1 = g_pre_mix[l].reshape(1, D_MODEL)

        q, k_all, c_in, vt_all, gates, o_conv = _in_proj(x2, g1, w_qkv, wg, w_h, w_b, w_c, conv_w[l],
                                                         batch, seq_len)

        w1 = jnp.stack([w_cmp_k1[l], w_cmp_v1[l]]).astype(BF16)
        w2 = jnp.stack([w_cmp_k2[l], w_cmp_v2[l]]).astype(BF16)
        cmp, cmp_t = _compress(c_in, pe_cmp[l], w1, w2, batch, seq_len)

        o_attn, (wo_b, wup_b, wdown_b) = _nsa(q, k_all, vt_all, cmp, cmp_t, gates, fvec, ovl, e_t,
                                              [w_o[l], w_up[l], w_down[l]], batch, seq_len, tq=tq)

        x1, h2 = _oproj(o_attn, o_conv, wo_b, x2,
                        g_post_mix[l].reshape(1, D_MODEL), g_pre_ffn[l].reshape(1, D_MODEL))
        x2 = _ffn(h2, wup_b, wdown_b, x1, g_post_ffn[l].reshape(1, D_MODEL))
    return x2.reshape(batch, seq_len, D_MODEL)
```

```python
import functools
import math

import numpy as np
import jax
import jax.numpy as jnp
from jax import lax
from jax.experimental import pallas as pl
from jax.experimental.pallas import tpu as pltpu

F32 = jnp.float32
BF16 = jnp.bfloat16

D_MODEL = 2048
N_HEADS = 8
N_KV = 2
HPG = N_HEADS // N_KV
HEAD_DIM = 128
ATTN_WIDTH = N_HEADS * HEAD_DIM
KV_WIDTH = N_KV * HEAD_DIM
CONV_WIDTH = D_MODEL - ATTN_WIDTH
CONV_K = 3
N_BRANCH = 3
CMP_LEN = 32
CMP_STRIDE = 16
CMP_HIDDEN = 256
SEL_LEN = 64
SEL_TOPK = 16
WINDOW = 512
N_BUCKETS = 32
MAX_DIST = 128
EPS = 1e-6
NEG = -1e30
HALF_NEG = -5e29
FORCE = 1e9

QKV_WIDTH = ATTN_WIDTH + 6 * KV_WIDTH
GATE_OFF = QKV_WIDTH
CONV_OFF = QKV_WIDTH + N_HEADS * N_BRANCH
LANE = 128
VMEM_LIMIT = 56 * 1024 * 1024

_DN_T = (((1,), (1,)), ((), ()))


def _rms(x, g):
    ms = jnp.mean(x * x, axis=-1, keepdims=True)
    return x * lax.rsqrt(ms + EPS) * g


def _params(n_axes):
    return pltpu.CompilerParams(dimension_semantics=("arbitrary",) * n_axes, vmem_limit_bytes=VMEM_LIMIT)


QKV_BLOCK = 2 * KV_WIDTH
N_Q_BLOCKS = ATTN_WIDTH // QKV_BLOCK
BLK_K, BLK_C, BLK_V = N_Q_BLOCKS, N_Q_BLOCKS + 1, N_Q_BLOCKS + 2
CONV_BLOCK = 512


def _inproj_kernel(x_ref, g_ref, w_ref, wg_ref, wh_ref, wb_ref, wc_ref, cw_ref,
                   q_ref, k_ref, c_ref, vt_ref, gate_ref, ov_ref, h_ref, carry_ref, *, q_scale, tiles_per_seq):
    i = pl.program_id(0)
    tm = x_ref.shape[0]
    h_ref[...] = _rms(x_ref[...], g_ref[...]).astype(BF16)
    gate_ref[...] = jax.nn.sigmoid(jnp.dot(h_ref[...], wg_ref[...], preferred_element_type=F32))

    def proj(w, j, width):
        return jnp.dot(h_ref[...], w[:, j * width:(j + 1) * width], preferred_element_type=F32)

    for j in range(N_Q_BLOCKS):
        q_ref[:, j * QKV_BLOCK:(j + 1) * QKV_BLOCK] = (proj(w_ref, j, QKV_BLOCK) * q_scale).astype(BF16)
    k_ref[...] = proj(w_ref, BLK_K, QKV_BLOCK).astype(BF16)
    c_ref[...] = proj(w_ref, BLK_C, QKV_BLOCK)
    v = proj(w_ref, BLK_V, QKV_BLOCK)
    ones = jnp.where(lax.broadcasted_iota(jnp.int32, (ONES_ROWS, tm), 0) == 0, 1.0, 0.0).astype(BF16)
    for s in range(QKV_BLOCK // HEAD_DIM):
        vt_ref[s, 0:HEAD_DIM, :] = v[:, s * HEAD_DIM:(s + 1) * HEAD_DIM].T.astype(BF16)
        vt_ref[s, HEAD_DIM:HEAD_DIM + ONES_ROWS, :] = ones

    row = lax.broadcasted_iota(jnp.int32, (tm, CONV_BLOCK), 0)
    for j in range(CONV_WIDTH // CONV_BLOCK):
        cols = slice(j * CONV_BLOCK, (j + 1) * CONV_BLOCK)
        u = proj(wc_ref, j, CONV_BLOCK) * proj(wh_ref, j, CONV_BLOCK)
        prev = carry_ref[j]
        prev = jnp.where(i % tiles_per_seq == 0, 0.0, prev)
        carry_ref[j] = u[tm - 8:tm, :]
        u1 = jnp.where(row == 0, prev[7:8, :], pltpu.roll(u, 1, axis=0))
        u2 = jnp.where(row == 0, prev[6:7, :], jnp.where(row == 1, prev[7:8, :], pltpu.roll(u, 2, axis=0)))
        w = cw_ref[:, cols]
        y = w[0:1, :] * u2
        y = y + w[1:2, :] * u1
        y = y + w[2:3, :] * u
        ov_ref[:, cols] = (proj(wb_ref, j, CONV_BLOCK) * y).astype(BF16)


def _in_proj(x2, g, w_qkv, w_gate, w_h, w_b, w_c, conv_w, batch, seq_len, *, tm=512):
    m = x2.shape[0]
    gw = w_gate.shape[1]
    tps = seq_len // tm
    n_slabs = QKV_BLOCK // HEAD_DIM
    kern = functools.partial(_inproj_kernel, q_scale=HEAD_DIM ** -0.5 * math.log2(math.e), tiles_per_seq=tps)
    row = lambda i: (i, 0)

    def resident(shape):
        return pl.BlockSpec(shape, lambda i: (0, 0), pipeline_mode=pl.Buffered(1))

    return pl.pallas_call(
        kern,
        grid=(m // tm,),
        in_specs=[
            pl.BlockSpec((tm, D_MODEL), row),
            resident((1, D_MODEL)),
            resident(w_qkv.shape),
            resident(w_gate.shape),
            resident(w_h.shape), resident(w_b.shape), resident(w_c.shape),
            resident(conv_w.shape),
        ],
        out_specs=[
            pl.BlockSpec((tm, ATTN_WIDTH), row),
            pl.BlockSpec((tm, QKV_BLOCK), row),
            pl.BlockSpec((tm, QKV_BLOCK), row),
            pl.BlockSpec((None, n_slabs, HEAD_DIM + ONES_ROWS, tm), lambda i: (i // tps, 0, 0, i % tps)),
            pl.BlockSpec((tm, gw), row),
            pl.BlockSpec((tm, CONV_WIDTH), row),
        ],
        out_shape=[
            jax.ShapeDtypeStruct((m, ATTN_WIDTH), BF16),
            jax.ShapeDtypeStruct((m, QKV_BLOCK), BF16),
            jax.ShapeDtypeStruct((m, QKV_BLOCK), F32),
            jax.ShapeDtypeStruct((batch, n_slabs, HEAD_DIM + ONES_ROWS, seq_len), BF16),
            jax.ShapeDtypeStruct((m, gw), F32),
            jax.ShapeDtypeStruct((m, CONV_WIDTH), BF16),
        ],
        scratch_shapes=[pltpu.VMEM((tm, D_MODEL), BF16),
                        pltpu.VMEM((CONV_WIDTH // CONV_BLOCK, 8, CONV_BLOCK), F32)],
        compiler_params=_params(1),
        name="in_proj",
    )(x2, g, w_qkv, w_gate, w_h, w_b, w_c, conv_w)


def _compress_kernel(x_ref, pe_ref, w1_ref, w2_ref, o_ref, ot_ref):
    n = x_ref.shape[0] // CMP_STRIDE
    a = jnp.zeros((n, CMP_HIDDEN), F32)
    b = jnp.zeros((n, CMP_HIDDEN), F32)
    for l in range(CMP_STRIDE):
        xl = x_ref[pl.ds(l, n, stride=CMP_STRIDE), :]
        xa = (xl + pe_ref[l:l + 1, :]).astype(BF16)
        xb = (xl + pe_ref[CMP_STRIDE + l:CMP_STRIDE + l + 1, :]).astype(BF16)
        a = a + jnp.dot(xa, w1_ref[l], preferred_element_type=F32)
        b = b + jnp.dot(xb, w1_ref[CMP_STRIDE + l], preferred_element_type=F32)
    pre = a + pltpu.roll(b, n - 1, axis=0)
    hid = pre * jax.nn.sigmoid(pre)
    out = jnp.dot(hid.astype(BF16), w2_ref[...], preferred_element_type=F32)
    row = lax.broadcasted_iota(jnp.int32, out.shape, 0)
    out = jnp.where(row < n - 1, out, 0.0)
    o_ref[...] = out.astype(BF16)
    ot_ref[...] = out.T.astype(BF16)


def _compress(c_in, pe, w1, w2, batch, seq_len):
    n_slabs = c_in.shape[1] // HEAD_DIM
    n_chunks = seq_len // CMP_STRIDE
    return pl.pallas_call(
        _compress_kernel,
        grid=(batch, n_slabs),
        in_specs=[
            pl.BlockSpec((seq_len, HEAD_DIM), lambda i, j: (i, j)),
            pl.BlockSpec((CMP_LEN, HEAD_DIM), lambda i, j: (0, 0)),
            pl.BlockSpec((None, CMP_LEN, HEAD_DIM, CMP_HIDDEN), lambda i, j: (j // N_KV, 0, 0, 0)),
            pl.BlockSpec((None, CMP_HIDDEN, HEAD_DIM), lambda i, j: (j // N_KV, 0, 0)),
        ],
        out_specs=[
            pl.BlockSpec((None, None, n_chunks, HEAD_DIM), lambda i, j: (i, j, 0, 0)),
            pl.BlockSpec((None, None, HEAD_DIM, n_chunks), lambda i, j: (i, j, 0, 0)),
        ],
        out_shape=[
            jax.ShapeDtypeStruct((batch, n_slabs, n_chunks, HEAD_DIM), BF16),
            jax.ShapeDtypeStruct((batch, n_slabs, HEAD_DIM, n_chunks), BF16),
        ],
        compiler_params=_params(2),
        name="compress",
    )(c_in, pe, w1, w2)


TAB_DIAG, TAB_SUB, TAB_FAR = 0, 1, 2
BAND_BELOW = 8
ONES_ROWS = 16


N_NSA_INPUTS = 11
BIAS_LEN = 1024
FAR_UNROLL = 4


def _nsa_kernel(*refs, tq, n_sel, n_cast):
    (q_ref, kc_ref, vct_ref, ks_ref, vst_ref, kw_ref, vwt_ref, gate_ref, fvec_ref,
     ovl_ref, et_ref) = refs[:N_NSA_INPUTS]
    cast_in = refs[N_NSA_INPUTS:N_NSA_INPUTS + n_cast]
    o_ref = refs[N_NSA_INPUTS + n_cast]
    cast_out = refs[N_NSA_INPUTS + n_cast + 1:N_NSA_INPUTS + 2 * n_cast + 1]
    (kaug_ref, tab_ref, band_ref, sc_ref, s0_ref, s1_ref, m_ref, mt_ref,
     acc_ref) = refs[N_NSA_INPUTS + 2 * n_cast + 1:]
    i = pl.program_id(2)

    for src, dst in zip(cast_in, cast_out):
        dst[...] = src[...].astype(BF16)

    tk = tq
    mcols = HPG * tq
    nw = WINDOW // tk
    cpt = tq // CMP_STRIDE
    band_rows = BAND_BELOW + cpt

    @pl.when(i == 0)
    def _():
        kaug_ref[:, 0:HEAD_DIM] = ks_ref[...]
        kaug_ref[:, HEAD_DIM:2 * HEAD_DIM] = et_ref[...]

    @pl.when(i == 0)
    def _():
        c = lax.broadcasted_iota(jnp.int32, (tk, tq), 0)
        r = lax.broadcasted_iota(jnp.int32, (tk, tq), 1)
        far = jnp.where(r < c, 0.0, NEG)
        lane = lax.broadcasted_iota(jnp.int32, (1, BIAS_LEN), 1)
        for h in range(HPG):
            cols = slice(h * tq, (h + 1) * tq)
            f = fvec_ref[h]
            f_diag = jnp.where(lane < tq, f, NEG)
            x = pltpu.roll(jnp.broadcast_to(f_diag, (tk, BIAS_LEN)), 0, 1, stride=1, stride_axis=0)
            tab_ref[TAB_DIAG, :, cols] = x[:, 0:tq]
            x = pltpu.roll(jnp.broadcast_to(f, (tk, BIAS_LEN)), 0, 1, stride=1, stride_axis=0)
            tab_ref[TAB_SUB, :, cols] = x[:, tq:2 * tq]
            tab_ref[TAB_FAR, :, cols] = far
            for v, first in enumerate((0, -BAND_BELOW)):
                shift = (CMP_STRIDE * first + CMP_LEN - 1) % BIAS_LEN
                f_shift = pltpu.roll(f, shift, 1)
                x = pltpu.roll(jnp.broadcast_to(f_shift, (band_rows, BIAS_LEN)), 0, 1,
                               stride=CMP_STRIDE, stride_axis=0)
                band_ref[v, :, cols] = x[:, 0:tq]

    q = q_ref[...]
    qs = jnp.concatenate([q[:, h * HEAD_DIM:(h + 1) * HEAD_DIM] for h in range(HPG)], axis=0)

    raw = lax.dot_general(kc_ref[...], qs, _DN_T, preferred_element_type=F32)
    crow = lax.broadcasted_iota(jnp.int32, raw.shape, 0)
    sc_ref[...] = jnp.where(crow < cpt * (i + 1), raw, NEG)
    band = pl.ds(pl.multiple_of(jnp.maximum(cpt * i - BAND_BELOW, 0), 8), band_rows)
    sc_ref[band, :] += band_ref[jnp.minimum(i, 1)]
    sc = sc_ref[...]
    mc = jnp.maximum(jnp.max(sc, axis=0, keepdims=True), HALF_NEG)
    pc = jnp.exp2(sc - mc)
    lc = jnp.sum(pc, axis=0, keepdims=True)
    pc = pc * jnp.where(lc > 0.0, 1.0 / lc, 0.0)
    o_c = jnp.dot(vct_ref[...], pc.astype(BF16), preferred_element_type=F32)

    ps = pc[:, 0:tq] + pc[:, tq:2 * tq] + pc[:, 2 * tq:3 * tq] + pc[:, 3 * tq:4 * tq]
    hi = ps.astype(BF16)
    r1 = ps - hi.astype(F32)
    mid = r1.astype(BF16)
    lo = (r1 - mid.astype(F32)).astype(BF16)
    ovl = ovl_ref[...]
    imp = (jnp.dot(ovl, hi, preferred_element_type=F32) + jnp.dot(ovl, mid, preferred_element_type=F32)
           + jnp.dot(ovl, lo, preferred_element_type=F32))
    jj = lax.broadcasted_iota(jnp.int32, (n_sel, tq), 0)
    tt = i * tq + lax.broadcasted_iota(jnp.int32, (n_sel, tq), 1)
    cur = tt >> int(math.log2(SEL_LEN))
    forced = (jj == 0) | (jj == cur) | (jj == cur - 1)
    imp = jnp.where(forced, FORCE, imp)
    imp = jnp.where(jj * SEL_LEN <= tt, imp, NEG)
    sub = 8
    ranks = []
    for j0 in range(0, n_sel, sub):
        blk = imp[j0:j0 + sub, :]
        jl = j0 + lax.broadcasted_iota(jnp.int32, blk.shape, 0)
        cnt = jnp.zeros(blk.shape, jnp.int32)
        for b in range(n_sel):
            row = imp[b:b + 1, :]
            if b < j0:
                cnt = cnt + jnp.where(row >= blk, 1, 0)
            elif b >= j0 + sub:
                cnt = cnt + jnp.where(row > blk, 1, 0)
            else:
                cnt = cnt + jnp.where(row > blk, 1, jnp.where(row == blk, jnp.where(jl > b, 1, 0), 0))
        ranks.append(cnt)
    rank = jnp.concatenate(ranks, axis=0)
    selb_t = jnp.where(rank < SEL_TOPK, 0.0, NEG)
    selb = jnp.concatenate([selb_t, jnp.zeros((LANE - n_sel, tq), F32)], axis=0).T
    selb = selb.astype(BF16)
    qa = jnp.concatenate([qs, jnp.concatenate([selb] * HPG, axis=0)], axis=1)

    s_refs = (s0_ref, s1_ref)

    def init():
        m_ref[...] = jnp.full((1, mcols), NEG, F32)
        acc_ref[...] = jnp.zeros(acc_ref.shape, F32)

    def keys(kt):
        return pl.ds(pl.multiple_of(kt * tk, tk), tk)

    def qk(qmat, k_ref, kt, kind, buf):
        s = lax.dot_general(k_ref[keys(kt), :], qmat, _DN_T, preferred_element_type=F32)
        if kind is not None:
            s = s + tab_ref[kind]
        s_refs[buf][...] = s
        return jnp.max(s, axis=0, keepdims=True)

    def process(vt_ref, kt, buf, m_tile):
        m_prev = m_ref[...]
        m_next = jnp.maximum(m_prev, m_tile)
        alpha = jnp.exp2(m_prev - m_next)
        p = jnp.exp2(s_refs[buf][...] - m_next).astype(BF16)
        acc_ref[...] = alpha * acc_ref[...] + jnp.dot(vt_ref[:, keys(kt)], p, preferred_element_type=F32)
        m_ref[...] = m_next

    def finish():
        return acc_ref[0:HEAD_DIM, :] * (1.0 / acc_ref[HEAD_DIM:HEAD_DIM + 1, :])

    init()

    @pl.when(i == 0)
    def _():
        process(vst_ref, 0, 0, qk(qa, kaug_ref, 0, TAB_DIAG, 0))

    @pl.when(i >= 1)
    def _():
        mt0 = qk(qa, kaug_ref, i, TAB_DIAG, 0)
        mt1 = qk(qa, kaug_ref, i - 1, TAB_SUB, 1)
        process(vst_ref, i, 0, mt0)
        mt0 = qk(qa, kaug_ref, jnp.maximum(i - 2, 0), None, 0)
        process(vst_ref, i - 1, 1, mt1)
        n_far = i - 1
        rem = n_far % FAR_UNROLL

        def far_tiles(a, n, mt0):
            for t in range(n):
                nxt = qk(qa, kaug_ref, jnp.maximum(a - t - 1, 0), None, (t + 1) % 2)
                process(vst_ref, a - t, t % 2, mt0)
                mt0 = nxt
            return mt0

        mt0 = lax.fori_loop(0, n_far // FAR_UNROLL,
                            lambda p, mt: far_tiles(i - 2 - FAR_UNROLL * p, FAR_UNROLL, mt), mt0)

        @pl.when(rem >= 2)
        def _():
            mt_ref[...] = far_tiles(rem - 1, 2, mt0)

        @pl.when(rem == 1)
        def _():
            process(vst_ref, 0, 0, mt0)

        @pl.when(rem == 3)
        def _():
            process(vst_ref, 0, 0, mt_ref[...])

    o_s = finish()

    init()
    for avail in range(nw + 1):

        @pl.when((i == avail) if avail < nw else (i >= nw))
        def _(avail=avail):
            kinds = [TAB_DIAG if d == 0 else TAB_SUB if d == 1 else TAB_FAR if d == nw else None
                     for d in range(avail + 1)]
            mt = qk(qs, kw_ref, i, kinds[0], 0)
            for d in range(avail + 1):
                mt_next = qk(qs, kw_ref, i - d - 1, kinds[d + 1], (d + 1) % 2) if d < avail else None
                process(vwt_ref, i - d, d % 2, mt)
                mt = mt_next

    o_w = finish()

    gt = gate_ref[...].T
    for h in range(HPG):
        sl = slice(h * tq, (h + 1) * tq)
        o = gt[3 * h:3 * h + 1, :] * o_c[:, sl]
        o = o + gt[3 * h + 1:3 * h + 2, :] * o_s[:, sl]
        o = o + gt[3 * h + 2:3 * h + 3, :] * o_w[:, sl]
        o_ref[:, h * HEAD_DIM:(h + 1) * HEAD_DIM] = o.T.astype(BF16)


def _nsa(q, k_all, vt_all, cmp, cmp_t, gates, fvec, ovl, e_t, cast_weights, batch, seq_len, *, tq):
    n_t = seq_len // tq
    n_steps = batch * N_KV * n_t
    step_row = lambda b, g, i: ((b * N_KV + g) * n_t + i, 0)
    cast_specs = [pl.BlockSpec((w.shape[0] // n_steps, w.shape[1]), step_row) for w in cast_weights]
    n_sel = seq_len // SEL_LEN
    n_chunks = cmp.shape[2]
    mcols = HPG * tq
    band_rows = BAND_BELOW + tq // CMP_STRIDE
    qw = HPG * HEAD_DIM

    def vt_spec(first):
        return pl.BlockSpec((None, None, HEAD_DIM + ONES_ROWS, seq_len), lambda b, g, i: (b, first + g, 0, 0))

    kern = functools.partial(_nsa_kernel, tq=tq, n_sel=n_sel, n_cast=len(cast_weights))
    o_attn, *cast = pl.pallas_call(
        kern,
        grid=(batch, N_KV, n_t),
        in_specs=[
            pl.BlockSpec((tq, qw), lambda b, g, i: (b * n_t + i, g)),
            pl.BlockSpec((None, None, n_chunks, HEAD_DIM), lambda b, g, i: (b, g, 0, 0)),
            pl.BlockSpec((None, None, HEAD_DIM, n_chunks), lambda b, g, i: (b, N_KV + g, 0, 0)),
            pl.BlockSpec((seq_len, HEAD_DIM), lambda b, g, i: (b, g)),
            vt_spec(0),
            pl.BlockSpec((seq_len, HEAD_DIM), lambda b, g, i: (b, N_KV + g)),
            vt_spec(N_KV),
            pl.BlockSpec((tq, LANE), lambda b, g, i: (b * n_t + i, g)),
            pl.BlockSpec((None, HPG, 1, BIAS_LEN), lambda b, g, i: (g, 0, 0, 0)),
            pl.BlockSpec((n_sel, n_chunks), lambda b, g, i: (0, 0)),
            pl.BlockSpec((seq_len, LANE), lambda b, g, i: (0, 0)),
        ] + cast_specs,
        out_specs=[pl.BlockSpec((tq, qw), lambda b, g, i: (b * n_t + i, g))] + cast_specs,
        out_shape=[jax.ShapeDtypeStruct((batch * seq_len, ATTN_WIDTH), BF16)]
        + [jax.ShapeDtypeStruct(w.shape, BF16) for w in cast_weights],
        scratch_shapes=[
            pltpu.VMEM((seq_len, 2 * HEAD_DIM), BF16),
            pltpu.VMEM((3, tq, mcols), F32),
            pltpu.VMEM((2, band_rows, mcols), F32),
            pltpu.VMEM((n_chunks, mcols), F32),
            pltpu.VMEM((tq, mcols), F32),
            pltpu.VMEM((tq, mcols), F32),
            pltpu.VMEM((1, mcols), F32),
            pltpu.VMEM((1, mcols), F32),
            pltpu.VMEM((HEAD_DIM + ONES_ROWS, mcols), F32),
        ],
        compiler_params=_params(3),
        name="nsa_attention",
    )(q, cmp, cmp_t, k_all, vt_all, k_all, vt_all, gates, fvec, ovl, e_t, *cast_weights)
    return o_attn, cast


def _oproj_kernel(oa_ref, ov_ref, wo_ref, x_ref, g1_ref, g2_ref, x1_ref, h2_ref):
    ka = oa_ref.shape[1]
    tm = oa_ref.shape[0]
    for rows in (slice(0, tm // 2), slice(tm // 2, tm)):
        mix = jnp.dot(oa_ref[rows, :], wo_ref[0:ka, :], preferred_element_type=F32)
        mix = mix + jnp.dot(ov_ref[rows, :], wo_ref[ka:, :], preferred_element_type=F32)
        x1 = x_ref[rows, :] + _rms(mix, g1_ref[...])
        x1_ref[rows, :] = x1
        h2_ref[rows, :] = _rms(x1, g2_ref[...]).astype(BF16)


def _oproj(o_attn, o_conv, w_o, x2, g_post, g_pre, *, tm=512):
    m = x2.shape[0]
    ka, kv = o_attn.shape[1], o_conv.shape[1]
    row = lambda i: (i, 0)
    fixed = lambda i: (0, 0)
    return pl.pallas_call(
        _oproj_kernel,
        grid=(m // tm,),
        in_specs=[
            pl.BlockSpec((tm, ka), row),
            pl.BlockSpec((tm, kv), row),
            pl.BlockSpec((ka + kv, D_MODEL), fixed),
            pl.BlockSpec((tm, D_MODEL), row),
            pl.BlockSpec((1, D_MODEL), fixed),
            pl.BlockSpec((1, D_MODEL), fixed),
        ],
        out_specs=[pl.BlockSpec((tm, D_MODEL), row), pl.BlockSpec((tm, D_MODEL), row)],
        out_shape=[jax.ShapeDtypeStruct((m, D_MODEL), F32), jax.ShapeDtypeStruct((m, D_MODEL), BF16)],
        compiler_params=_params(1),
        name="out_proj",
    )(o_attn, o_conv, w_o, x2, g_post, g_pre)


def _ffn_kernel(h_ref, wu_ref, wd_ref, x1_ref, g_ref, o_ref, acc_ref):
    j = pl.program_id(1)

    @pl.when(j == 0)
    def _():
        acc_ref[...] = jnp.zeros(acc_ref.shape, F32)

    a = jnp.dot(h_ref[...], wu_ref[...], preferred_element_type=F32)
    a = jnp.square(jnp.maximum(a, 0.0)).astype(BF16)
    acc_ref[...] += jnp.dot(a, wd_ref[...], preferred_element_type=F32)

    @pl.when(j == pl.num_programs(1) - 1)
    def _():
        o_ref[...] = x1_ref[...] + _rms(acc_ref[...], g_ref[...])


def _ffn(h2, w_up, w_down, x1, g_post, *, tm=512, tf=1024):
    m = h2.shape[0]
    d_ff = w_up.shape[1]
    return pl.pallas_call(
        _ffn_kernel,
        grid=(m // tm, d_ff // tf),
        in_specs=[
            pl.BlockSpec((tm, D_MODEL), lambda i, j: (i, 0)),
            pl.BlockSpec((D_MODEL, tf), lambda i, j: (0, j)),
            pl.BlockSpec((tf, D_MODEL), lambda i, j: (j, 0)),
            pl.BlockSpec((tm, D_MODEL), lambda i, j: (i, 0)),
            pl.BlockSpec((1, D_MODEL), lambda i, j: (0, 0)),
        ],
        out_specs=pl.BlockSpec((tm, D_MODEL), lambda i, j: (i, 0)),
        out_shape=jax.ShapeDtypeStruct((m, D_MODEL), F32),
        scratch_shapes=[pltpu.VMEM((tm, D_MODEL), F32)],
        compiler_params=_params(2),
        name="ffn",
    )(h2, w_up, w_down, x1, g_post)


def _bucket_np(dist):
    n = np.maximum(dist, 0)
    max_exact = N_BUCKETS // 2
    nf = np.maximum(n, 1).astype(np.float32)
    large = max_exact + (np.log(nf / np.float32(max_exact)) / np.float32(math.log(MAX_DIST / max_exact))
                         * np.float32(N_BUCKETS - max_exact)).astype(np.int32)
    large = np.minimum(large, N_BUCKETS - 1)
    return np.where(n < max_exact, n, large)


def _bucket_starts():
    b = _bucket_np(np.arange(4 * MAX_DIST))
    return [int(np.argmax(b == k)) for k in range(N_BUCKETS)]


def _attention_tables(rel_bias, seq_len, tq):
    starts = _bucket_starts()
    assert starts[N_BUCKETS - 1] <= CMP_STRIDE * (BAND_BELOW + 1) - (CMP_LEN - 1)
    assert starts[N_BUCKETS - 1] <= tq and 2 * tq <= BIAS_LEN // 2
    rel = (rel_bias - rel_bias[:, N_BUCKETS - 1:]) * math.log2(math.e)
    d = jnp.arange(BIAS_LEN, dtype=jnp.int32)[None, :]
    fvec = jnp.broadcast_to(rel[:, 0:1], (N_HEADS, BIAS_LEN))
    for k in range(1, N_BUCKETS):
        fvec = jnp.where(d >= starts[k], rel[:, k:k + 1], fvec)
    fvec = jnp.where(d < BIAS_LEN // 2, fvec, NEG).astype(F32).reshape(N_KV, HPG, 1, BIAS_LEN)

    n_chunks = seq_len // CMP_STRIDE
    n_cmp = (seq_len - CMP_LEN) // CMP_STRIDE + 1
    n_sel = seq_len // SEL_LEN
    ci = np.arange(n_chunks)[None, :] * CMP_STRIDE
    sj = np.arange(n_sel)[:, None] * SEL_LEN
    ovl = ((ci < sj + SEL_LEN) & (ci + CMP_LEN > sj) & (np.arange(n_chunks)[None, :] < n_cmp))
    e_t = (np.arange(seq_len)[:, None] // SEL_LEN == np.arange(LANE)[None, :])
    return fvec, jnp.asarray(ovl, BF16), jnp.asarray(e_t, BF16)


def kernel(x, w_in, pe_cmp, w_cmp_k1, w_cmp_k2, w_cmp_v1, w_cmp_v2, conv_w, rel_bias, w_o, w_up, w_down,
           g_pre_mix, g_post_mix, g_pre_ffn, g_post_ffn):
    batch, seq_len, _ = x.shape
    depth = w_in.shape[0]
    tq = 256
    fvec, ovl, e_t = _attention_tables(rel_bias, seq_len, tq)
    x2 = x.reshape(batch * seq_len, D_MODEL)
    for l in range(depth):
        wl = w_in[l]
        slab = lambda k: wl[:, ATTN_WIDTH + k * KV_WIDTH:ATTN_WIDTH + (k + 1) * KV_WIDTH]
        w_qkv = jnp.concatenate([wl[:, :ATTN_WIDTH], slab(2), slab(4), slab(0), slab(1), slab(3), slab(5)],
                                axis=1).astype(BF16)
        wg = wl[:, GATE_OFF:CONV_OFF].reshape(D_MODEL, N_KV, HPG * N_BRANCH)
        wg = jnp.pad(wg, ((0, 0), (0, 0), (0, LANE - HPG * N_BRANCH))).reshape(D_MODEL, N_KV * LANE).astype(BF16)
        w_h, w_b, w_c = (wl[:, CONV_OFF + k * CONV_WIDTH:CONV_OFF + (k + 1) * CONV_WIDTH].astype(BF16)
                         for k in range(3))
        g1 = g_pre_mix[l].reshape(1, D_MODEL)

        q, k_all, c_in, vt_all, gates, o_conv = _in_proj(x2, g1, w_qkv, wg, w_h, w_b, w_c, conv_w[l],
                                                         batch, seq_len)

        w1 = jnp.stack([w_cmp_k1[l], w_cmp_v1[l]]).astype(BF16)
        w2 = jnp.stack([w_cmp_k2[l], w_cmp_v2[l]]).astype(BF16)
        cmp, cmp_t = _compress(c_in, pe_cmp[l], w1, w2, batch, seq_len)

        o_attn, (wo_b, wup_b, wdown_b) = _nsa(q, k_all, vt_all, cmp, cmp_t, gates, fvec, ovl, e_t,
                                              [w_o[l], w_up[l], w_down[l]], batch, seq_len, tq=tq)

        x1, h2 = _oproj(o_attn, o_conv, wo_b, x2,
                        g_post_mix[l].reshape(1, D_MODEL), g_pre_ffn[l].reshape(1, D_MODEL))
        x2 = _ffn(h2, wup_b, wdown_b, x1, g_post_ffn[l].reshape(1, D_MODEL))
    return x2.reshape(batch, seq_len, D_MODEL)
```

```python
import functools
import math

import numpy as np
import jax
import jax.numpy as jnp
from jax import lax
from jax.experimental import pallas as pl
from jax.experimental.pallas import tpu as pltpu

F32 = jnp.float32
BF16 = jnp.bfloat16

D_MODEL = 2048
N_HEADS = 8
N_KV = 2
HPG = N_HEADS // N_KV
HEAD_DIM = 128
ATTN_WIDTH = N_HEADS * HEAD_DIM
KV_WIDTH = N_KV * HEAD_DIM
CONV_WIDTH = D_MODEL - ATTN_WIDTH
CONV_K = 3
N_BRANCH = 3
CMP_LEN = 32
CMP_STRIDE = 16
CMP_HIDDEN = 256
SEL_LEN = 64
SEL_TOPK = 16
WINDOW = 512
N_BUCKETS = 32
MAX_DIST = 128
EPS = 1e-6
NEG = -1e30
HALF_NEG = -5e29
FORCE = 1e9

QKV_WIDTH = ATTN_WIDTH + 6 * KV_WIDTH
GATE_OFF = QKV_WIDTH
CONV_OFF = QKV_WIDTH + N_HEADS * N_BRANCH
LANE = 128
VMEM_LIMIT = 56 * 1024 * 1024

_DN_T = (((1,), (1,)), ((), ()))


def _rms(x, g):
    ms = jnp.mean(x * x, axis=-1, keepdims=True)
    return x * lax.rsqrt(ms + EPS) * g


def _params(n_axes):
    return pltpu.CompilerParams(dimension_semantics=("arbitrary",) * n_axes, vmem_limit_bytes=VMEM_LIMIT)


QKV_BLOCK = 2 * KV_WIDTH
N_Q_BLOCKS = ATTN_WIDTH // QKV_BLOCK
BLK_K, BLK_C, BLK_V = N_Q_BLOCKS, N_Q_BLOCKS + 1, N_Q_BLOCKS + 2
CONV_BLOCK = 512


def _inproj_kernel(x_ref, g_ref, w_ref, wg_ref, wh_ref, wb_ref, wc_ref, cw_ref,
                   q_ref, k_ref, c_ref, vt_ref, gate_ref, ov_ref, h_ref, carry_ref, *, q_scale, tiles_per_seq):
    i = pl.program_id(0)
    tm = x_ref.shape[0]
    h_ref[...] = _rms(x_ref[...], g_ref[...]).astype(BF16)
    gate_ref[...] = jax.nn.sigmoid(jnp.dot(h_ref[...], wg_ref[...], preferred_element_type=F32))

    def proj(w, j, width):
        return jnp.dot(h_ref[...], w[:, j * width:(j + 1) * width], preferred_element_type=F32)

    for j in range(N_Q_BLOCKS):
        q_ref[:, j * QKV_BLOCK:(j + 1) * QKV_BLOCK] = (proj(w_ref, j, QKV_BLOCK) * q_scale).astype(BF16)
    k_ref[...] = proj(w_ref, BLK_K, QKV_BLOCK).astype(BF16)
    c_ref[...] = proj(w_ref, BLK_C, QKV_BLOCK)
    v = proj(w_ref, BLK_V, QKV_BLOCK)
    ones = jnp.where(lax.broadcasted_iota(jnp.int32, (ONES_ROWS, tm), 0) == 0, 1.0, 0.0).astype(BF16)
    for s in range(QKV_BLOCK // HEAD_DIM):
        vt_ref[s, 0:HEAD_DIM, :] = v[:, s * HEAD_DIM:(s + 1) * HEAD_DIM].T.astype(BF16)
        vt_ref[s, HEAD_DIM:HEAD_DIM + ONES_ROWS, :] = ones

    row = lax.broadcasted_iota(jnp.int32, (tm, CONV_BLOCK), 0)
    for j in range(CONV_WIDTH // CONV_BLOCK):
        cols = slice(j * CONV_BLOCK, (j + 1) * CONV_BLOCK)
        u = proj(wc_ref, j, CONV_BLOCK) * proj(wh_ref, j, CONV_BLOCK)
        prev = carry_ref[j]
        prev = jnp.where(i % tiles_per_seq == 0, 0.0, prev)
        carry_ref[j] = u[tm - 8:tm, :]
        u1 = jnp.where(row == 0, prev[7:8, :], pltpu.roll(u, 1, axis=0))
        u2 = jnp.where(row == 0, prev[6:7, :], jnp.where(row == 1, prev[7:8, :], pltpu.roll(u, 2, axis=0)))
        w = cw_ref[:, cols]
        y = w[0:1, :] * u2
        y = y + w[1:2, :] * u1
        y = y + w[2:3, :] * u
        ov_ref[:, cols] = (proj(wb_ref, j, CONV_BLOCK) * y).astype(BF16)


def _in_proj(x2, g, w_qkv, w_gate, w_h, w_b, w_c, conv_w, batch, seq_len, *, tm=512):
    m = x2.shape[0]
    gw = w_gate.shape[1]
    tps = seq_len // tm
    n_slabs = QKV_BLOCK // HEAD_DIM
    kern = functools.partial(_inproj_kernel, q_scale=HEAD_DIM ** -0.5 * math.log2(math.e), tiles_per_seq=tps)
    row = lambda i: (i, 0)

    def resident(shape):
        return pl.BlockSpec(shape, lambda i: (0, 0), pipeline_mode=pl.Buffered(1))

    return pl.pallas_call(
        kern,
        grid=(m // tm,),
        in_specs=[
            pl.BlockSpec((tm, D_MODEL), row),
            resident((1, D_MODEL)),
            resident(w_qkv.shape),
            resident(w_gate.shape),
            resident(w_h.shape), resident(w_b.shape), resident(w_c.shape),
            resident(conv_w.shape),
        ],
        out_specs=[
            pl.BlockSpec((tm, ATTN_WIDTH), row),
            pl.BlockSpec((tm, QKV_BLOCK), row),
            pl.BlockSpec((tm, QKV_BLOCK), row),
            pl.BlockSpec((None, n_slabs, HEAD_DIM + ONES_ROWS, tm), lambda i: (i // tps, 0, 0, i % tps)),
            pl.BlockSpec((tm, gw), row),
            pl.BlockSpec((tm, CONV_WIDTH), row),
        ],
        out_shape=[
            jax.ShapeDtypeStruct((m, ATTN_WIDTH), BF16),
            jax.ShapeDtypeStruct((m, QKV_BLOCK), BF16),
            jax.ShapeDtypeStruct((m, QKV_BLOCK), F32),
            jax.ShapeDtypeStruct((batch, n_slabs, HEAD_DIM + ONES_ROWS, seq_len), BF16),
            jax.ShapeDtypeStruct((m, gw), F32),
            jax.ShapeDtypeStruct((m, CONV_WIDTH), BF16),
        ],
        scratch_shapes=[pltpu.VMEM((tm, D_MODEL), BF16),
                        pltpu.VMEM((CONV_WIDTH // CONV_BLOCK, 8, CONV_BLOCK), F32)],
        compiler_params=_params(1),
        name="in_proj",
    )(x2, g, w_qkv, w_gate, w_h, w_b, w_c, conv_w)


def _compress_kernel(x_ref, pe_ref, w1_ref, w2_ref, o_ref, ot_ref):
    n = x_ref.shape[0] // CMP_STRIDE
    a = jnp.zeros((n, CMP_HIDDEN), F32)
    b = jnp.zeros((n, CMP_HIDDEN), F32)
    for l in range(CMP_STRIDE):
        xl = x_ref[pl.ds(l, n, stride=CMP_STRIDE), :]
        xa = (xl + pe_ref[l:l + 1, :]).astype(BF16)
        xb = (xl + pe_ref[CMP_STRIDE + l:CMP_STRIDE + l + 1, :]).astype(BF16)
        a = a + jnp.dot(xa, w1_ref[l], preferred_element_type=F32)
        b = b + jnp.dot(xb, w1_ref[CMP_STRIDE + l], preferred_element_type=F32)
    pre = a + pltpu.roll(b, n - 1, axis=0)
    hid = pre * jax.nn.sigmoid(pre)
    out = jnp.dot(hid.astype(BF16), w2_ref[...], preferred_element_type=F32)
    row = lax.broadcasted_iota(jnp.int32, out.shape, 0)
    out = jnp.where(row < n - 1, out, 0.0)
    o_ref[...] = out.astype(BF16)
    ot_ref[...] = out.T.astype(BF16)


def _compress(c_in, pe, w1, w2, batch, seq_len):
    n_slabs = c_in.shape[1] // HEAD_DIM
    n_chunks = seq_len // CMP_STRIDE
    return pl.pallas_call(
        _compress_kernel,
        grid=(batch, n_slabs),
        in_specs=[
            pl.BlockSpec((seq_len, HEAD_DIM), lambda i, j: (i, j)),
            pl.BlockSpec((CMP_LEN, HEAD_DIM), lambda i, j: (0, 0)),
            pl.BlockSpec((None, CMP_LEN, HEAD_DIM, CMP_HIDDEN), lambda i, j: (j // N_KV, 0, 0, 0)),
            pl.BlockSpec((None, CMP_HIDDEN, HEAD_DIM), lambda i, j: (j // N_KV, 0, 0)),
        ],
        out_specs=[
            pl.BlockSpec((None, None, n_chunks, HEAD_DIM), lambda i, j: (i, j, 0, 0)),
            pl.BlockSpec((None, None, HEAD_DIM, n_chunks), lambda i, j: (i, j, 0, 0)),
        ],
        out_shape=[
            jax.ShapeDtypeStruct((batch, n_slabs, n_chunks, HEAD_DIM), BF16),
            jax.ShapeDtypeStruct((batch, n_slabs, HEAD_DIM, n_chunks), BF16),
        ],
        compiler_params=_params(2),
        name="compress",
    )(c_in, pe, w1, w2)


TAB_DIAG, TAB_SUB, TAB_FAR, TAB_MASK = 0, 1, 2, 3
BAND_BELOW = 8
ONES_ROWS = 16


N_NSA_INPUTS = 11
BIAS_LEN = 1024
FAR_UNROLL = 4


def _nsa_kernel(*refs, tq, n_sel, n_cast):
    (q_ref, kc_ref, vct_ref, ks_ref, vst_ref, kw_ref, vwt_ref, gate_ref, fvec_ref,
     ovl_ref, et_ref) = refs[:N_NSA_INPUTS]
    cast_in = refs[N_NSA_INPUTS:N_NSA_INPUTS + n_cast]
    o_ref = refs[N_NSA_INPUTS + n_cast]
    cast_out = refs[N_NSA_INPUTS + n_cast + 1:N_NSA_INPUTS + 2 * n_cast + 1]
    (kaug_ref, tab_ref, band_ref, sc_ref, s0_ref, s1_ref, m_ref, mt_ref,
     acc_ref) = refs[N_NSA_INPUTS + 2 * n_cast + 1:]
    i = pl.program_id(2)

    for src, dst in zip(cast_in, cast_out):
        dst[...] = src[...].astype(BF16)

    tk = tq
    mcols = HPG * tq
    nw = WINDOW // tk
    cpt = tq // CMP_STRIDE
    band_rows = BAND_BELOW + cpt

    @pl.when(i == 0)
    def _():
        kaug_ref[:, 0:HEAD_DIM] = ks_ref[...]
        kaug_ref[:, HEAD_DIM:2 * HEAD_DIM] = et_ref[...]

    @pl.when(i == 0)
    def _():
        c = lax.broadcasted_iota(jnp.int32, (tk, tq), 0)
        r = lax.broadcasted_iota(jnp.int32, (tk, tq), 1)
        far = jnp.where(r < c, 0.0, NEG)
        lane = lax.broadcasted_iota(jnp.int32, (1, BIAS_LEN), 1)
        for h in range(HPG):
            cols = slice(h * tq, (h + 1) * tq)
            f = fvec_ref[h]
            f_diag = jnp.where(lane < tq, f, NEG)
            x = pltpu.roll(jnp.broadcast_to(f_diag, (tk, BIAS_LEN)), 0, 1, stride=1, stride_axis=0)
            tab_ref[TAB_DIAG, :, cols] = x[:, 0:tq]
            x = pltpu.roll(jnp.broadcast_to(f, (tk, BIAS_LEN)), 0, 1, stride=1, stride_axis=0)
            tab_ref[TAB_SUB, :, cols] = x[:, tq:2 * tq]
            tab_ref[TAB_FAR, :, cols] = far
            tab_ref[TAB_MASK, :, cols] = jnp.full((tk, tq), NEG, F32)
            for v, first in enumerate((0, -BAND_BELOW)):
                shift = (CMP_STRIDE * first + CMP_LEN - 1) % BIAS_LEN
                f_shift = pltpu.roll(f, shift, 1)
                x = pltpu.roll(jnp.broadcast_to(f_shift, (band_rows, BIAS_LEN)), 0, 1,
                               stride=CMP_STRIDE, stride_axis=0)
                band_ref[v, :, cols] = x[:, 0:tq]

    q = q_ref[...]
    qs = jnp.concatenate([q[:, h * HEAD_DIM:(h + 1) * HEAD_DIM] for h in range(HPG)], axis=0)

    s_refs = (s0_ref, s1_ref)

    def init():
        m_ref[...] = jnp.full((1, mcols), NEG, F32)
        acc_ref[...] = jnp.zeros(acc_ref.shape, F32)

    def keys(kt):
        return pl.ds(pl.multiple_of(kt * tk, tk), tk)

    def qk(qmat, k_ref, kt, kind, buf):
        s = lax.dot_general(k_ref[keys(kt), :], qmat, _DN_T, preferred_element_type=F32)
        if kind is not None:
            s = s + tab_ref[kind]
        s_refs[buf][...] = s
        return jnp.max(s, axis=0, keepdims=True)

    def process(vt_ref, kt, buf, m_tile):
        m_prev = m_ref[...]
        m_next = jnp.maximum(m_prev, m_tile)
        alpha = jnp.exp2(m_prev - m_next)
        p = jnp.exp2(s_refs[buf][...] - m_next).astype(BF16)
        acc_ref[...] = alpha * acc_ref[...] + jnp.dot(vt_ref[:, keys(kt)], p, preferred_element_type=F32)
        m_ref[...] = m_next

    def finish():
        return acc_ref[0:HEAD_DIM, :] * (1.0 / acc_ref[HEAD_DIM:HEAD_DIM + 1, :])

    assert nw == 2, "window tiles are i, i-1 (previous-tile table) and i-nw (window-edge table)"
    init()
    mt_w0 = qk(qs, kw_ref, i, TAB_DIAG, 0)

    raw = lax.dot_general(kc_ref[...], qs, _DN_T, preferred_element_type=F32)
    crow = lax.broadcasted_iota(jnp.int32, raw.shape, 0)
    sc_ref[...] = jnp.where(crow < cpt * (i + 1), raw, NEG)
    band = pl.ds(pl.multiple_of(jnp.maximum(cpt * i - BAND_BELOW, 0), 8), band_rows)
    sc_ref[band, :] += band_ref[jnp.minimum(i, 1)]

    mt_w1 = qk(qs, kw_ref, jnp.maximum(i - 1, 0), jnp.where(i >= 1, TAB_SUB, TAB_MASK), 1)
    process(vwt_ref, i, 0, mt_w0)

    sc = sc_ref[...]
    mc = jnp.maximum(jnp.max(sc, axis=0, keepdims=True), HALF_NEG)
    pc = jnp.exp2(sc - mc)
    lc = jnp.sum(pc, axis=0, keepdims=True)
    pc = pc * jnp.where(lc > 0.0, 1.0 / lc, 0.0)
    o_c = jnp.dot(vct_ref[...], pc.astype(BF16), preferred_element_type=F32)

    mt_w2 = qk(qs, kw_ref, jnp.maximum(i - nw, 0), jnp.where(i >= nw, TAB_FAR, TAB_MASK), 0)
    process(vwt_ref, jnp.maximum(i - 1, 0), 1, mt_w1)

    ps = pc[:, 0:tq] + pc[:, tq:2 * tq] + pc[:, 2 * tq:3 * tq] + pc[:, 3 * tq:4 * tq]
    hi = ps.astype(BF16)
    r1 = ps - hi.astype(F32)
    mid = r1.astype(BF16)
    lo = (r1 - mid.astype(F32)).astype(BF16)
    ovl = ovl_ref[...]
    imp = (jnp.dot(ovl, hi, preferred_element_type=F32) + jnp.dot(ovl, mid, preferred_element_type=F32)
           + jnp.dot(ovl, lo, preferred_element_type=F32))
    jj = lax.broadcasted_iota(jnp.int32, (n_sel, tq), 0)
    tt = i * tq + lax.broadcasted_iota(jnp.int32, (n_sel, tq), 1)
    cur = tt >> int(math.log2(SEL_LEN))
    forced = (jj == 0) | (jj == cur) | (jj == cur - 1)
    imp = jnp.where(forced, FORCE, imp)
    imp = jnp.where(jj * SEL_LEN <= tt, imp, NEG)
    sub = 8
    ranks = []
    for j0 in range(0, n_sel, sub):
        blk = imp[j0:j0 + sub, :]
        jl = j0 + lax.broadcasted_iota(jnp.int32, blk.shape, 0)
        cnt = jnp.zeros(blk.shape, jnp.int32)
        for b in range(n_sel):
            row = imp[b:b + 1, :]
            if b < j0:
                cnt = cnt + jnp.where(row >= blk, 1, 0)
            elif b >= j0 + sub:
                cnt = cnt + jnp.where(row > blk, 1, 0)
            else:
                cnt = cnt + jnp.where(row > blk, 1, jnp.where(row == blk, jnp.where(jl > b, 1, 0), 0))
        ranks.append(cnt)
    rank = jnp.concatenate(ranks, axis=0)
    selb_t = jnp.where(rank < SEL_TOPK, 0.0, NEG)

    process(vwt_ref, jnp.maximum(i - nw, 0), 0, mt_w2)
    o_w = finish()

    selb = jnp.concatenate([selb_t, jnp.zeros((LANE - n_sel, tq), F32)], axis=0).T
    selb = selb.astype(BF16)
    qa = jnp.concatenate([qs, jnp.concatenate([selb] * HPG, axis=0)], axis=1)

    init()

    @pl.when(i == 0)
    def _():
        process(vst_ref, 0, 0, qk(qa, kaug_ref, 0, TAB_DIAG, 0))

    @pl.when(i >= 1)
    def _():
        mt0 = qk(qa, kaug_ref, i, TAB_DIAG, 0)
        mt1 = qk(qa, kaug_ref, i - 1, TAB_SUB, 1)
        process(vst_ref, i, 0, mt0)
        mt0 = qk(qa, kaug_ref, jnp.maximum(i - 2, 0), None, 0)
        process(vst_ref, i - 1, 1, mt1)
        n_far = i - 1
        rem = n_far % FAR_UNROLL

        def far_tiles(a, n, mt0):
            for t in range(n):
                nxt = qk(qa, kaug_ref, jnp.maximum(a - t - 1, 0), None, (t + 1) % 2)
                process(vst_ref, a - t, t % 2, mt0)
                mt0 = nxt
            return mt0

        mt0 = lax.fori_loop(0, n_far // FAR_UNROLL,
                            lambda p, mt: far_tiles(i - 2 - FAR_UNROLL * p, FAR_UNROLL, mt), mt0)

        @pl.when(rem >= 2)
        def _():
            mt_ref[...] = far_tiles(rem - 1, 2, mt0)

        @pl.when(rem == 1)
        def _():
            process(vst_ref, 0, 0, mt0)

        @pl.when(rem == 3)
        def _():
            process(vst_ref, 0, 0, mt_ref[...])

    o_s = finish()

    gt = gate_ref[...].T
    for h in range(HPG):
        sl = slice(h * tq, (h + 1) * tq)
        o = gt[3 * h:3 * h + 1, :] * o_c[:, sl]
        o = o + gt[3 * h + 1:3 * h + 2, :] * o_s[:, sl]
        o = o + gt[3 * h + 2:3 * h + 3, :] * o_w[:, sl]
        o_ref[:, h * HEAD_DIM:(h + 1) * HEAD_DIM] = o.T.astype(BF16)


def _nsa(q, k_all, vt_all, cmp, cmp_t, gates, fvec, ovl, e_t, cast_weights, batch, seq_len, *, tq):
    n_t = seq_len // tq
    n_steps = batch * N_KV * n_t
    step_row = lambda b, g, i: ((b * N_KV + g) * n_t + i, 0)
    cast_specs = [pl.BlockSpec((w.shape[0] // n_steps, w.shape[1]), step_row) for w in cast_weights]
    n_sel = seq_len // SEL_LEN
    n_chunks = cmp.shape[2]
    mcols = HPG * tq
    band_rows = BAND_BELOW + tq // CMP_STRIDE
    qw = HPG * HEAD_DIM

    def vt_spec(first):
        return pl.BlockSpec((None, None, HEAD_DIM + ONES_ROWS, seq_len), lambda b, g, i: (b, first + g, 0, 0))

    kern = functools.partial(_nsa_kernel, tq=tq, n_sel=n_sel, n_cast=len(cast_weights))
    o_attn, *cast = pl.pallas_call(
        kern,
        grid=(batch, N_KV, n_t),
        in_specs=[
            pl.BlockSpec((tq, qw), lambda b, g, i: (b * n_t + i, g)),
            pl.BlockSpec((None, None, n_chunks, HEAD_DIM), lambda b, g, i: (b, g, 0, 0)),
            pl.BlockSpec((None, None, HEAD_DIM, n_chunks), lambda b, g, i: (b, N_KV + g, 0, 0)),
            pl.BlockSpec((seq_len, HEAD_DIM), lambda b, g, i: (b, g)),
            vt_spec(0),
            pl.BlockSpec((seq_len, HEAD_DIM), lambda b, g, i: (b, N_KV + g)),
            vt_spec(N_KV),
            pl.BlockSpec((tq, LANE), lambda b, g, i: (b * n_t + i, g)),
            pl.BlockSpec((None, HPG, 1, BIAS_LEN), lambda b, g, i: (g, 0, 0, 0)),
            pl.BlockSpec((n_sel, n_chunks), lambda b, g, i: (0, 0)),
            pl.BlockSpec((seq_len, LANE), lambda b, g, i: (0, 0)),
        ] + cast_specs,
        out_specs=[pl.BlockSpec((tq, qw), lambda b, g, i: (b * n_t + i, g))] + cast_specs,
        out_shape=[jax.ShapeDtypeStruct((batch * seq_len, ATTN_WIDTH), BF16)]
        + [jax.ShapeDtypeStruct(w.shape, BF16) for w in cast_weights],
        scratch_shapes=[
            pltpu.VMEM((seq_len, 2 * HEAD_DIM), BF16),
            pltpu.VMEM((4, tq, mcols), F32),
            pltpu.VMEM((2, band_rows, mcols), F32),
            pltpu.VMEM((n_chunks, mcols), F32),
            pltpu.VMEM((tq, mcols), F32),
            pltpu.VMEM((tq, mcols), F32),
            pltpu.VMEM((1, mcols), F32),
            pltpu.VMEM((1, mcols), F32),
            pltpu.VMEM((HEAD_DIM + ONES_ROWS, mcols), F32),
        ],
        compiler_params=_params(3),
        name="nsa_attention",
    )(q, cmp, cmp_t, k_all, vt_all, k_all, vt_all, gates, fvec, ovl, e_t, *cast_weights)
    return o_attn, cast


def _oproj_kernel(oa_ref, ov_ref, wo_ref, x_ref, g1_ref, g2_ref, x1_ref, h2_ref):
    ka = oa_ref.shape[1]
    tm = oa_ref.shape[0]
    for rows in (slice(0, tm // 2), slice(tm // 2, tm)):
        mix = jnp.dot(oa_ref[rows, :], wo_ref[0:ka, :], preferred_element_type=F32)
        mix = mix + jnp.dot(ov_ref[rows, :], wo_ref[ka:, :], preferred_element_type=F32)
        x1 = x_ref[rows, :] + _rms(mix, g1_ref[...])
        x1_ref[rows, :] = x1
        h2_ref[rows, :] = _rms(x1, g2_ref[...]).astype(BF16)


def _oproj(o_attn, o_conv, w_o, x2, g_post, g_pre, *, tm=512):
    m = x2.shape[0]
    ka, kv = o_attn.shape[1], o_conv.shape[1]
    row = lambda i: (i, 0)
    fixed = lambda i: (0, 0)
    return pl.pallas_call(
        _oproj_kernel,
        grid=(m // tm,),
        in_specs=[
            pl.BlockSpec((tm, ka), row),
            pl.BlockSpec((tm, kv), row),
            pl.BlockSpec((ka + kv, D_MODEL), fixed),
            pl.BlockSpec((tm, D_MODEL), row),
            pl.BlockSpec((1, D_MODEL), fixed),
            pl.BlockSpec((1, D_MODEL), fixed),
        ],
        out_specs=[pl.BlockSpec((tm, D_MODEL), row), pl.BlockSpec((tm, D_MODEL), row)],
        out_shape=[jax.ShapeDtypeStruct((m, D_MODEL), F32), jax.ShapeDtypeStruct((m, D_MODEL), BF16)],
        compiler_params=_params(1),
        name="out_proj",
    )(o_attn, o_conv, w_o, x2, g_post, g_pre)


def _ffn_kernel(h_ref, wu_ref, wd_ref, x1_ref, g_ref, o_ref, acc_ref):
    j = pl.program_id(1)

    @pl.when(j == 0)
    def _():
        acc_ref[...] = jnp.zeros(acc_ref.shape, F32)

    a = jnp.dot(h_ref[...], wu_ref[...], preferred_element_type=F32)
    a = jnp.square(jnp.maximum(a, 0.0)).astype(BF16)
    acc_ref[...] += jnp.dot(a, wd_ref[...], preferred_element_type=F32)

    @pl.when(j == pl.num_programs(1) - 1)
    def _():
        o_ref[...] = x1_ref[...] + _rms(acc_ref[...], g_ref[...])


def _ffn(h2, w_up, w_down, x1, g_post, *, tm=512, tf=1024):
    m = h2.shape[0]
    d_ff = w_up.shape[1]
    return pl.pallas_call(
        _ffn_kernel,
        grid=(m // tm, d_ff // tf),
        in_specs=[
            pl.BlockSpec((tm, D_MODEL), lambda i, j: (i, 0)),
            pl.BlockSpec((D_MODEL, tf), lambda i, j: (0, j)),
            pl.BlockSpec((tf, D_MODEL), lambda i, j: (j, 0)),
            pl.BlockSpec((tm, D_MODEL), lambda i, j: (i, 0)),
            pl.BlockSpec((1, D_MODEL), lambda i, j: (0, 0)),
        ],
        out_specs=pl.BlockSpec((tm, D_MODEL), lambda i, j: (i, 0)),
        out_shape=jax.ShapeDtypeStruct((m, D_MODEL), F32),
        scratch_shapes=[pltpu.VMEM((tm, D_MODEL), F32)],
        compiler_params=_params(2),
        name="ffn",
    )(h2, w_up, w_down, x1, g_post)


def _bucket_np(dist):
    n = np.maximum(dist, 0)
    max_exact = N_BUCKETS // 2
    nf = np.maximum(n, 1).astype(np.float32)
    large = max_exact + (np.log(nf / np.float32(max_exact)) / np.float32(math.log(MAX_DIST / max_exact))
                         * np.float32(N_BUCKETS - max_exact)).astype(np.int32)
    large = np.minimum(large, N_BUCKETS - 1)
    return np.where(n < max_exact, n, large)


def _bucket_starts():
    b = _bucket_np(np.arange(4 * MAX_DIST))
    return [int(np.argmax(b == k)) for k in range(N_BUCKETS)]


def _attention_tables(rel_bias, seq_len, tq):
    starts = _bucket_starts()
    assert starts[N_BUCKETS - 1] <= CMP_STRIDE * (BAND_BELOW + 1) - (CMP_LEN - 1)
    assert starts[N_BUCKETS - 1] <= tq and 2 * tq <= BIAS_LEN // 2
    rel = (rel_bias - rel_bias[:, N_BUCKETS - 1:]) * math.log2(math.e)
    d = jnp.arange(BIAS_LEN, dtype=jnp.int32)[None, :]
    fvec = jnp.broadcast_to(rel[:, 0:1], (N_HEADS, BIAS_LEN))
    for k in range(1, N_BUCKETS):
        fvec = jnp.where(d >= starts[k], rel[:, k:k + 1], fvec)
    fvec = jnp.where(d < BIAS_LEN // 2, fvec, NEG).astype(F32).reshape(N_KV, HPG, 1, BIAS_LEN)

    n_chunks = seq_len // CMP_STRIDE
    n_cmp = (seq_len - CMP_LEN) // CMP_STRIDE + 1
    n_sel = seq_len // SEL_LEN
    ci = np.arange(n_chunks)[None, :] * CMP_STRIDE
    sj = np.arange(n_sel)[:, None] * SEL_LEN
    ovl = ((ci < sj + SEL_LEN) & (ci + CMP_LEN > sj) & (np.arange(n_chunks)[None, :] < n_cmp))
    e_t = (np.arange(seq_len)[:, None] // SEL_LEN == np.arange(LANE)[None, :])
    return fvec, jnp.asarray(ovl, BF16), jnp.asarray(e_t, BF16)


def kernel(x, w_in, pe_cmp, w_cmp_k1, w_cmp_k2, w_cmp_v1, w_cmp_v2, conv_w, rel_bias, w_o, w_up, w_down,
           g_pre_mix, g_post_mix, g_pre_ffn, g_post_ffn):
    batch, seq_len, _ = x.shape
    depth = w_in.shape[0]
    tq = 256
    fvec, ovl, e_t = _attention_tables(rel_bias, seq_len, tq)
    x2 = x.reshape(batch * seq_len, D_MODEL)
    for l in range(depth):
        wl = w_in[l]
        slab = lambda k: wl[:, ATTN_WIDTH + k * KV_WIDTH:ATTN_WIDTH + (k + 1) * KV_WIDTH]
        w_qkv = jnp.concatenate([wl[:, :ATTN_WIDTH], slab(2), slab(4), slab(0), slab(1), slab(3), slab(5)],
                                axis=1).astype(BF16)
        wg = wl[:, GATE_OFF:CONV_OFF].reshape(D_MODEL, N_KV, HPG * N_BRANCH)
        wg = jnp.pad(wg, ((0, 0), (0, 0), (0, LANE - HPG * N_BRANCH))).reshape(D_MODEL, N_KV * LANE).astype(BF16)
        w_h, w_b, w_c = (wl[:, CONV_OFF + k * CONV_WIDTH:CONV_OFF + (k + 1) * CONV_WIDTH].astype(BF16)
                         for k in range(3))
        g1 = g_pre_mix[l].reshape(1, D_MODEL)

        q, k_all, c_in, vt_all, gates, o_conv = _in_proj(x2, g1, w_qkv, wg, w_h, w_b, w_c, conv_w[l],
                                                         batch, seq_len)

        w1 = jnp.stack([w_cmp_k1[l], w_cmp_v1[l]]).astype(BF16)
        w2 = jnp.stack([w_cmp_k2[l], w_cmp_v2[l]]).astype(BF16)
        cmp, cmp_t = _compress(c_in, pe_cmp[l], w1, w2, batch, seq_len)

        o_attn, (wo_b, wup_b, wdown_b) = _nsa(q, k_all, vt_all, cmp, cmp_t, gates, fvec, ovl, e_t,
                                              [w_o[l], w_up[l], w_down[l]], batch, seq_len, tq=tq)

        x1, h2 = _oproj(o_attn, o_conv, wo_b, x2,
                        g_post_mix[l].reshape(1, D_MODEL), g_pre_ffn[l].reshape(1, D_MODEL))
        x2 = _ffn(h2, wup_b, wdown_b, x1, g_post_ffn[l].reshape(1, D_MODEL))
    return x2.reshape(batch, seq_len, D_MODEL)
```

```python
import functools
import math

import numpy as np
import jax
import jax.numpy as jnp
from jax import lax
from jax.experimental import pallas as pl
from jax.experimental.pallas import tpu as pltpu

F32 = jnp.float32
BF16 = jnp.bfloat16

D_MODEL = 2048
N_HEADS = 8
N_KV = 2
HPG = N_HEADS // N_KV
HEAD_DIM = 128
ATTN_WIDTH = N_HEADS * HEAD_DIM
KV_WIDTH = N_KV * HEAD_DIM
CONV_WIDTH = D_MODEL - ATTN_WIDTH
CONV_K = 3
N_BRANCH = 3
CMP_LEN = 32
CMP_STRIDE = 16
CMP_HIDDEN = 256
SEL_LEN = 64
SEL_TOPK = 16
WINDOW = 512
N_BUCKETS = 32
MAX_DIST = 128
EPS = 1e-6
NEG = -1e30
HALF_NEG = -5e29
FORCE = 1e9

QKV_WIDTH = ATTN_WIDTH + 6 * KV_WIDTH
GATE_OFF = QKV_WIDTH
CONV_OFF = QKV_WIDTH + N_HEADS * N_BRANCH
LANE = 128
VMEM_LIMIT = 56 * 1024 * 1024

_DN_T = (((1,), (1,)), ((), ()))


def _rms(x, g):
    ms = jnp.mean(x * x, axis=-1, keepdims=True)
    return x * lax.rsqrt(ms + EPS) * g


def _params(n_axes):
    return pltpu.CompilerParams(dimension_semantics=("arbitrary",) * n_axes, vmem_limit_bytes=VMEM_LIMIT)


QKV_BLOCK = 2 * KV_WIDTH
N_Q_BLOCKS = ATTN_WIDTH // QKV_BLOCK
BLK_K, BLK_C, BLK_V = N_Q_BLOCKS, N_Q_BLOCKS + 1, N_Q_BLOCKS + 2
CONV_BLOCK = 512


def _inproj_kernel(x_ref, g_ref, w_ref, wg_ref, wh_ref, wb_ref, wc_ref, cw_ref,
                   q_ref, k_ref, c_ref, vt_ref, gate_ref, ov_ref, h_ref, carry_ref, *, q_scale, tiles_per_seq):
    i = pl.program_id(0)
    tm = x_ref.shape[0]
    h_ref[...] = _rms(x_ref[...], g_ref[...]).astype(BF16)
    gate_ref[...] = jax.nn.sigmoid(lax.dot_general(h_ref[...], wg_ref[...], _DN_T, preferred_element_type=F32))

    def proj(w, j, width):
        return lax.dot_general(h_ref[...], w[j * width:(j + 1) * width, :], _DN_T, preferred_element_type=F32)

    for j in range(N_Q_BLOCKS):
        q_ref[:, j * QKV_BLOCK:(j + 1) * QKV_BLOCK] = (proj(w_ref, j, QKV_BLOCK) * q_scale).astype(BF16)
    k_ref[...] = proj(w_ref, BLK_K, QKV_BLOCK).astype(BF16)
    c_ref[...] = proj(w_ref, BLK_C, QKV_BLOCK)
    v = proj(w_ref, BLK_V, QKV_BLOCK)
    ones = jnp.where(lax.broadcasted_iota(jnp.int32, (ONES_ROWS, tm), 0) == 0, 1.0, 0.0).astype(BF16)
    for s in range(QKV_BLOCK // HEAD_DIM):
        vt_ref[s, 0:HEAD_DIM, :] = v[:, s * HEAD_DIM:(s + 1) * HEAD_DIM].T.astype(BF16)
        vt_ref[s, HEAD_DIM:HEAD_DIM + ONES_ROWS, :] = ones

    row = lax.broadcasted_iota(jnp.int32, (tm, CONV_BLOCK), 0)
    for j in range(CONV_WIDTH // CONV_BLOCK):
        cols = slice(j * CONV_BLOCK, (j + 1) * CONV_BLOCK)
        u = proj(wc_ref, j, CONV_BLOCK) * proj(wh_ref, j, CONV_BLOCK)
        prev = carry_ref[j]
        prev = jnp.where(i % tiles_per_seq == 0, 0.0, prev)
        carry_ref[j] = u[tm - 8:tm, :]
        u1 = jnp.where(row == 0, prev[7:8, :], pltpu.roll(u, 1, axis=0))
        u2 = jnp.where(row == 0, prev[6:7, :], jnp.where(row == 1, prev[7:8, :], pltpu.roll(u, 2, axis=0)))
        w = cw_ref[:, cols]
        y = w[0:1, :] * u2
        y = y + w[1:2, :] * u1
        y = y + w[2:3, :] * u
        ov_ref[:, cols] = (proj(wb_ref, j, CONV_BLOCK) * y).astype(BF16)


def _in_proj(x2, g, w_qkv, w_gate, w_h, w_b, w_c, conv_w, batch, seq_len, *, tm=512):
    m = x2.shape[0]
    gw = w_gate.shape[0]
    tps = seq_len // tm
    n_slabs = QKV_BLOCK // HEAD_DIM
    kern = functools.partial(_inproj_kernel, q_scale=HEAD_DIM ** -0.5 * math.log2(math.e), tiles_per_seq=tps)
    row = lambda i: (i, 0)

    def resident(shape):
        return pl.BlockSpec(shape, lambda i: (0, 0), pipeline_mode=pl.Buffered(1))

    return pl.pallas_call(
        kern,
        grid=(m // tm,),
        in_specs=[
            pl.BlockSpec((tm, D_MODEL), row),
            resident((1, D_MODEL)),
            resident(w_qkv.shape),
            resident(w_gate.shape),
            resident(w_h.shape), resident(w_b.shape), resident(w_c.shape),
            resident(conv_w.shape),
        ],
        out_specs=[
            pl.BlockSpec((tm, ATTN_WIDTH), row),
            pl.BlockSpec((tm, QKV_BLOCK), row),
            pl.BlockSpec((tm, QKV_BLOCK), row),
            pl.BlockSpec((None, n_slabs, HEAD_DIM + ONES_ROWS, tm), lambda i: (i // tps, 0, 0, i % tps)),
            pl.BlockSpec((tm, gw), row),
            pl.BlockSpec((tm, CONV_WIDTH), row),
        ],
        out_shape=[
            jax.ShapeDtypeStruct((m, ATTN_WIDTH), BF16),
            jax.ShapeDtypeStruct((m, QKV_BLOCK), BF16),
            jax.ShapeDtypeStruct((m, QKV_BLOCK), F32),
            jax.ShapeDtypeStruct((batch, n_slabs, HEAD_DIM + ONES_ROWS, seq_len), BF16),
            jax.ShapeDtypeStruct((m, gw), F32),
            jax.ShapeDtypeStruct((m, CONV_WIDTH), BF16),
        ],
        scratch_shapes=[pltpu.VMEM((tm, D_MODEL), BF16),
                        pltpu.VMEM((CONV_WIDTH // CONV_BLOCK, 8, CONV_BLOCK), F32)],
        compiler_params=_params(1),
        name="in_proj",
    )(x2, g, w_qkv, w_gate, w_h, w_b, w_c, conv_w)


def _compress_kernel(x_ref, pe_ref, w1_ref, w2_ref, o_ref, ot_ref):
    n = x_ref.shape[0] // CMP_STRIDE
    a = jnp.zeros((n, CMP_HIDDEN), F32)
    b = jnp.zeros((n, CMP_HIDDEN), F32)
    for l in range(CMP_STRIDE):
        xl = x_ref[pl.ds(l, n, stride=CMP_STRIDE), :]
        xa = (xl + pe_ref[l:l + 1, :]).astype(BF16)
        xb = (xl + pe_ref[CMP_STRIDE + l:CMP_STRIDE + l + 1, :]).astype(BF16)
        a = a + jnp.dot(xa, w1_ref[l], preferred_element_type=F32)
        b = b + jnp.dot(xb, w1_ref[CMP_STRIDE + l], preferred_element_type=F32)
    pre = a + pltpu.roll(b, n - 1, axis=0)
    hid = pre * jax.nn.sigmoid(pre)
    out = jnp.dot(hid.astype(BF16), w2_ref[...], preferred_element_type=F32)
    row = lax.broadcasted_iota(jnp.int32, out.shape, 0)
    out = jnp.where(row < n - 1, out, 0.0)
    o_ref[...] = out.astype(BF16)
    ot_ref[...] = out.T.astype(BF16)


def _compress(c_in, pe, w1, w2, batch, seq_len):
    n_slabs = c_in.shape[1] // HEAD_DIM
    n_chunks = seq_len // CMP_STRIDE
    return pl.pallas_call(
        _compress_kernel,
        grid=(batch, n_slabs),
        in_specs=[
            pl.BlockSpec((seq_len, HEAD_DIM), lambda i, j: (i, j)),
            pl.BlockSpec((CMP_LEN, HEAD_DIM), lambda i, j: (0, 0)),
            pl.BlockSpec((None, CMP_LEN, HEAD_DIM, CMP_HIDDEN), lambda i, j: (j // N_KV, 0, 0, 0)),
            pl.BlockSpec((None, CMP_HIDDEN, HEAD_DIM), lambda i, j: (j // N_KV, 0, 0)),
        ],
        out_specs=[
            pl.BlockSpec((None, None, n_chunks, HEAD_DIM), lambda i, j: (i, j, 0, 0)),
            pl.BlockSpec((None, None, HEAD_DIM, n_chunks), lambda i, j: (i, j, 0, 0)),
        ],
        out_shape=[
            jax.ShapeDtypeStruct((batch, n_slabs, n_chunks, HEAD_DIM), BF16),
            jax.ShapeDtypeStruct((batch, n_slabs, HEAD_DIM, n_chunks), BF16),
        ],
        compiler_params=_params(2),
        name="compress",
    )(c_in, pe, w1, w2)


TAB_DIAG, TAB_SUB, TAB_FAR, TAB_MASK = 0, 1, 2, 3
BAND_BELOW = 8
ONES_ROWS = 16


N_NSA_INPUTS = 11
BIAS_LEN = 1024
FAR_UNROLL = 4


def _nsa_kernel(*refs, tq, n_sel, n_cast):
    (q_ref, kc_ref, vct_ref, ks_ref, vst_ref, kw_ref, vwt_ref, gate_ref, fvec_ref,
     ovl_ref, et_ref) = refs[:N_NSA_INPUTS]
    cast_in = refs[N_NSA_INPUTS:N_NSA_INPUTS + n_cast]
    o_ref = refs[N_NSA_INPUTS + n_cast]
    cast_out = refs[N_NSA_INPUTS + n_cast + 1:N_NSA_INPUTS + 2 * n_cast + 1]
    (kaug_ref, tab_ref, band_ref, sc_ref, s0_ref, s1_ref, m_ref, mt_ref,
     acc_ref) = refs[N_NSA_INPUTS + 2 * n_cast + 1:]
    i = pl.program_id(2)

    for src, dst in zip(cast_in, cast_out):
        dst[...] = src[...].astype(BF16)

    tk = tq
    mcols = HPG * tq
    nw = WINDOW // tk
    cpt = tq // CMP_STRIDE
    band_rows = BAND_BELOW + cpt

    @pl.when(i == 0)
    def _():
        kaug_ref[:, 0:HEAD_DIM] = ks_ref[...]
        kaug_ref[:, HEAD_DIM:2 * HEAD_DIM] = et_ref[...]

    @pl.when(i == 0)
    def _():
        c = lax.broadcasted_iota(jnp.int32, (tk, tq), 0)
        r = lax.broadcasted_iota(jnp.int32, (tk, tq), 1)
        far = jnp.where(r < c, 0.0, NEG)
        lane = lax.broadcasted_iota(jnp.int32, (1, BIAS_LEN), 1)
        for h in range(HPG):
            cols = slice(h * tq, (h + 1) * tq)
            f = fvec_ref[h]
            f_diag = jnp.where(lane < tq, f, NEG)
            x = pltpu.roll(jnp.broadcast_to(f_diag, (tk, BIAS_LEN)), 0, 1, stride=1, stride_axis=0)
            tab_ref[TAB_DIAG, :, cols] = x[:, 0:tq]
            x = pltpu.roll(jnp.broadcast_to(f, (tk, BIAS_LEN)), 0, 1, stride=1, stride_axis=0)
            tab_ref[TAB_SUB, :, cols] = x[:, tq:2 * tq]
            tab_ref[TAB_FAR, :, cols] = far
            tab_ref[TAB_MASK, :, cols] = jnp.full((tk, tq), NEG, F32)
            for v, first in enumerate((0, -BAND_BELOW)):
                shift = (CMP_STRIDE * first + CMP_LEN - 1) % BIAS_LEN
                f_shift = pltpu.roll(f, shift, 1)
                x = pltpu.roll(jnp.broadcast_to(f_shift, (band_rows, BIAS_LEN)), 0, 1,
                               stride=CMP_STRIDE, stride_axis=0)
                band_ref[v, :, cols] = x[:, 0:tq]

    q = q_ref[...]
    qs = jnp.concatenate([q[:, h * HEAD_DIM:(h + 1) * HEAD_DIM] for h in range(HPG)], axis=0)

    s_refs = (s0_ref, s1_ref)

    def init():
        m_ref[...] = jnp.full((1, mcols), NEG, F32)
        acc_ref[...] = jnp.zeros(acc_ref.shape, F32)

    def keys(kt):
        return pl.ds(pl.multiple_of(kt * tk, tk), tk)

    def qk(qmat, k_ref, kt, kind, buf):
        s = lax.dot_general(k_ref[keys(kt), :], qmat, _DN_T, preferred_element_type=F32)
        if kind is not None:
            s = s + tab_ref[kind]
        s_refs[buf][...] = s
        return jnp.max(s, axis=0, keepdims=True)

    def process(vt_ref, kt, buf, m_tile):
        m_prev = m_ref[...]
        m_next = jnp.maximum(m_prev, m_tile)
        alpha = jnp.exp2(m_prev - m_next)
        p = jnp.exp2(s_refs[buf][...] - m_next).astype(BF16)
        acc_ref[...] = alpha * acc_ref[...] + jnp.dot(vt_ref[:, keys(kt)], p, preferred_element_type=F32)
        m_ref[...] = m_next

    def finish():
        return acc_ref[0:HEAD_DIM, :] * (1.0 / acc_ref[HEAD_DIM:HEAD_DIM + 1, :])

    assert nw == 2, "window tiles are i, i-1 (previous-tile table) and i-nw (window-edge table)"
    init()
    mt_w0 = qk(qs, kw_ref, i, TAB_DIAG, 0)

    raw = lax.dot_general(kc_ref[...], qs, _DN_T, preferred_element_type=F32)
    crow = lax.broadcasted_iota(jnp.int32, raw.shape, 0)
    sc_ref[...] = jnp.where(crow < cpt * (i + 1), raw, NEG)
    band = pl.ds(pl.multiple_of(jnp.maximum(cpt * i - BAND_BELOW, 0), 8), band_rows)
    sc_ref[band, :] += band_ref[jnp.minimum(i, 1)]

    mt_w1 = qk(qs, kw_ref, jnp.maximum(i - 1, 0), jnp.where(i >= 1, TAB_SUB, TAB_MASK), 1)
    process(vwt_ref, i, 0, mt_w0)

    sc = sc_ref[...]
    mc = jnp.maximum(jnp.max(sc, axis=0, keepdims=True), HALF_NEG)
    pc = jnp.exp2(sc - mc)
    lc = jnp.sum(pc, axis=0, keepdims=True)
    pc = pc * jnp.where(lc > 0.0, 1.0 / lc, 0.0)
    o_c = jnp.dot(vct_ref[...], pc.astype(BF16), preferred_element_type=F32)

    mt_w2 = qk(qs, kw_ref, jnp.maximum(i - nw, 0), jnp.where(i >= nw, TAB_FAR, TAB_MASK), 0)
    process(vwt_ref, jnp.maximum(i - 1, 0), 1, mt_w1)

    ps = pc[:, 0:tq] + pc[:, tq:2 * tq] + pc[:, 2 * tq:3 * tq] + pc[:, 3 * tq:4 * tq]
    hi = ps.astype(BF16)
    r1 = ps - hi.astype(F32)
    mid = r1.astype(BF16)
    lo = (r1 - mid.astype(F32)).astype(BF16)
    ovl = ovl_ref[...]
    imp = (jnp.dot(ovl, hi, preferred_element_type=F32) + jnp.dot(ovl, mid, preferred_element_type=F32)
           + jnp.dot(ovl, lo, preferred_element_type=F32))
    jj = lax.broadcasted_iota(jnp.int32, (n_sel, tq), 0)
    tt = i * tq + lax.broadcasted_iota(jnp.int32, (n_sel, tq), 1)
    cur = tt >> int(math.log2(SEL_LEN))
    forced = (jj == 0) | (jj == cur) | (jj == cur - 1)
    imp = jnp.where(forced, FORCE, imp)
    imp = jnp.where(jj * SEL_LEN <= tt, imp, NEG)
    sub = 8
    ranks = []
    for j0 in range(0, n_sel, sub):
        blk = imp[j0:j0 + sub, :]
        jl = j0 + lax.broadcasted_iota(jnp.int32, blk.shape, 0)
        cnt = jnp.zeros(blk.shape, jnp.int32)
        for b in range(n_sel):
            row = imp[b:b + 1, :]
            if b < j0:
                cnt = cnt + jnp.where(row >= blk, 1, 0)
            elif b >= j0 + sub:
                cnt = cnt + jnp.where(row > blk, 1, 0)
            else:
                cnt = cnt + jnp.where(row > blk, 1, jnp.where(row == blk, jnp.where(jl > b, 1, 0), 0))
        ranks.append(cnt)
    rank = jnp.concatenate(ranks, axis=0)
    selb_t = jnp.where(rank < SEL_TOPK, 0.0, NEG)

    process(vwt_ref, jnp.maximum(i - nw, 0), 0, mt_w2)
    o_w = finish()

    selb = jnp.concatenate([selb_t, jnp.zeros((LANE - n_sel, tq), F32)], axis=0).T
    selb = selb.astype(BF16)
    qa = jnp.concatenate([qs, jnp.concatenate([selb] * HPG, axis=0)], axis=1)

    init()

    @pl.when(i == 0)
    def _():
        process(vst_ref, 0, 0, qk(qa, kaug_ref, 0, TAB_DIAG, 0))

    @pl.when(i >= 1)
    def _():
        mt0 = qk(qa, kaug_ref, i, TAB_DIAG, 0)
        mt1 = qk(qa, kaug_ref, i - 1, TAB_SUB, 1)
        process(vst_ref, i, 0, mt0)
        mt0 = qk(qa, kaug_ref, jnp.maximum(i - 2, 0), None, 0)
        process(vst_ref, i - 1, 1, mt1)
        n_far = i - 1
        rem = n_far % FAR_UNROLL

        def far_tiles(a, n, mt0):
            for t in range(n):
                nxt = qk(qa, kaug_ref, jnp.maximum(a - t - 1, 0), None, (t + 1) % 2)
                process(vst_ref, a - t, t % 2, mt0)
                mt0 = nxt
            return mt0

        mt0 = lax.fori_loop(0, n_far // FAR_UNROLL,
                            lambda p, mt: far_tiles(i - 2 - FAR_UNROLL * p, FAR_UNROLL, mt), mt0)

        @pl.when(rem >= 2)
        def _():
            mt_ref[...] = far_tiles(rem - 1, 2, mt0)

        @pl.when(rem == 1)
        def _():
            process(vst_ref, 0, 0, mt0)

        @pl.when(rem == 3)
        def _():
            process(vst_ref, 0, 0, mt_ref[...])

    o_s = finish()

    gt = gate_ref[...].T
    for h in range(HPG):
        sl = slice(h * tq, (h + 1) * tq)
        o = gt[3 * h:3 * h + 1, :] * o_c[:, sl]
        o = o + gt[3 * h + 1:3 * h + 2, :] * o_s[:, sl]
        o = o + gt[3 * h + 2:3 * h + 3, :] * o_w[:, sl]
        o_ref[:, h * HEAD_DIM:(h + 1) * HEAD_DIM] = o.T.astype(BF16)


def _nsa(q, k_all, vt_all, cmp, cmp_t, gates, fvec, ovl, e_t, cast_weights, batch, seq_len, *, tq):
    n_t = seq_len // tq
    n_steps = batch * N_KV * n_t
    step_row = lambda b, g, i: ((b * N_KV + g) * n_t + i, 0)
    cast_specs = [pl.BlockSpec((w.shape[0] // n_steps, w.shape[1]), step_row) for w in cast_weights]
    n_sel = seq_len // SEL_LEN
    n_chunks = cmp.shape[2]
    mcols = HPG * tq
    band_rows = BAND_BELOW + tq // CMP_STRIDE
    qw = HPG * HEAD_DIM

    def vt_spec(first):
        return pl.BlockSpec((None, None, HEAD_DIM + ONES_ROWS, seq_len), lambda b, g, i: (b, first + g, 0, 0))

    kern = functools.partial(_nsa_kernel, tq=tq, n_sel=n_sel, n_cast=len(cast_weights))
    o_attn, *cast = pl.pallas_call(
        kern,
        grid=(batch, N_KV, n_t),
        in_specs=[
            pl.BlockSpec((tq, qw), lambda b, g, i: (b * n_t + i, g)),
            pl.BlockSpec((None, None, n_chunks, HEAD_DIM), lambda b, g, i: (b, g, 0, 0)),
            pl.BlockSpec((None, None, HEAD_DIM, n_chunks), lambda b, g, i: (b, N_KV + g, 0, 0)),
            pl.BlockSpec((seq_len, HEAD_DIM), lambda b, g, i: (b, g)),
            vt_spec(0),
            pl.BlockSpec((seq_len, HEAD_DIM), lambda b, g, i: (b, N_KV + g)),
            vt_spec(N_KV),
            pl.BlockSpec((tq, LANE), lambda b, g, i: (b * n_t + i, g)),
            pl.BlockSpec((None, HPG, 1, BIAS_LEN), lambda b, g, i: (g, 0, 0, 0)),
            pl.BlockSpec((n_sel, n_chunks), lambda b, g, i: (0, 0)),
            pl.BlockSpec((seq_len, LANE), lambda b, g, i: (0, 0)),
        ] + cast_specs,
        out_specs=[pl.BlockSpec((tq, qw), lambda b, g, i: (b * n_t + i, g))] + cast_specs,
        out_shape=[jax.ShapeDtypeStruct((batch * seq_len, ATTN_WIDTH), BF16)]
        + [jax.ShapeDtypeStruct(w.shape, BF16) for w in cast_weights],
        scratch_shapes=[
            pltpu.VMEM((seq_len, 2 * HEAD_DIM), BF16),
            pltpu.VMEM((4, tq, mcols), F32),
            pltpu.VMEM((2, band_rows, mcols), F32),
            pltpu.VMEM((n_chunks, mcols), F32),
            pltpu.VMEM((tq, mcols), F32),
            pltpu.VMEM((tq, mcols), F32),
            pltpu.VMEM((1, mcols), F32),
            pltpu.VMEM((1, mcols), F32),
            pltpu.VMEM((HEAD_DIM + ONES_ROWS, mcols), F32),
        ],
        compiler_params=_params(3),
        name="nsa_attention",
    )(q, cmp, cmp_t, k_all, vt_all, k_all, vt_all, gates, fvec, ovl, e_t, *cast_weights)
    return o_attn, cast


def _oproj_kernel(oa_ref, ov_ref, wo_ref, x_ref, g1_ref, g2_ref, x1_ref, h2_ref):
    ka = oa_ref.shape[1]
    tm = oa_ref.shape[0]
    for rows in (slice(0, tm // 2), slice(tm // 2, tm)):
        mix = jnp.dot(oa_ref[rows, :], wo_ref[0:ka, :], preferred_element_type=F32)
        mix = mix + jnp.dot(ov_ref[rows, :], wo_ref[ka:, :], preferred_element_type=F32)
        x1 = x_ref[rows, :] + _rms(mix, g1_ref[...])
        x1_ref[rows, :] = x1
        h2_ref[rows, :] = _rms(x1, g2_ref[...]).astype(BF16)


def _oproj(o_attn, o_conv, w_o, x2, g_post, g_pre, *, tm=512):
    m = x2.shape[0]
    ka, kv = o_attn.shape[1], o_conv.shape[1]
    row = lambda i: (i, 0)
    fixed = lambda i: (0, 0)
    return pl.pallas_call(
        _oproj_kernel,
        grid=(m // tm,),
        in_specs=[
            pl.BlockSpec((tm, ka), row),
            pl.BlockSpec((tm, kv), row),
            pl.BlockSpec((ka + kv, D_MODEL), fixed),
            pl.BlockSpec((tm, D_MODEL), row),
            pl.BlockSpec((1, D_MODEL), fixed),
            pl.BlockSpec((1, D_MODEL), fixed),
        ],
        out_specs=[pl.BlockSpec((tm, D_MODEL), row), pl.BlockSpec((tm, D_MODEL), row)],
        out_shape=[jax.ShapeDtypeStruct((m, D_MODEL), F32), jax.ShapeDtypeStruct((m, D_MODEL), BF16)],
        compiler_params=_params(1),
        name="out_proj",
    )(o_attn, o_conv, w_o, x2, g_post, g_pre)


def _ffn_kernel(h_ref, wu_ref, wd_ref, x1_ref, g_ref, o_ref, acc_ref):
    j = pl.program_id(1)

    @pl.when(j == 0)
    def _():
        acc_ref[...] = jnp.zeros(acc_ref.shape, F32)

    a = jnp.dot(h_ref[...], wu_ref[...], preferred_element_type=F32)
    a = jnp.square(jnp.maximum(a, 0.0)).astype(BF16)
    acc_ref[...] += jnp.dot(a, wd_ref[...], preferred_element_type=F32)

    @pl.when(j == pl.num_programs(1) - 1)
    def _():
        o_ref[...] = x1_ref[...] + _rms(acc_ref[...], g_ref[...])


def _ffn(h2, w_up, w_down, x1, g_post, *, tm=512, tf=1024):
    m = h2.shape[0]
    d_ff = w_up.shape[1]
    return pl.pallas_call(
        _ffn_kernel,
        grid=(m // tm, d_ff // tf),
        in_specs=[
            pl.BlockSpec((tm, D_MODEL), lambda i, j: (i, 0)),
            pl.BlockSpec((D_MODEL, tf), lambda i, j: (0, j)),
            pl.BlockSpec((tf, D_MODEL), lambda i, j: (j, 0)),
            pl.BlockSpec((tm, D_MODEL), lambda i, j: (i, 0)),
            pl.BlockSpec((1, D_MODEL), lambda i, j: (0, 0)),
        ],
        out_specs=pl.BlockSpec((tm, D_MODEL), lambda i, j: (i, 0)),
        out_shape=jax.ShapeDtypeStruct((m, D_MODEL), F32),
        scratch_shapes=[pltpu.VMEM((tm, D_MODEL), F32)],
        compiler_params=_params(2),
        name="ffn",
    )(h2, w_up, w_down, x1, g_post)


def _bucket_np(dist):
    n = np.maximum(dist, 0)
    max_exact = N_BUCKETS // 2
    nf = np.maximum(n, 1).astype(np.float32)
    large = max_exact + (np.log(nf / np.float32(max_exact)) / np.float32(math.log(MAX_DIST / max_exact))
                         * np.float32(N_BUCKETS - max_exact)).astype(np.int32)
    large = np.minimum(large, N_BUCKETS - 1)
    return np.where(n < max_exact, n, large)


def _bucket_starts():
    b = _bucket_np(np.arange(4 * MAX_DIST))
    return [int(np.argmax(b == k)) for k in range(N_BUCKETS)]


def _attention_tables(rel_bias, seq_len, tq):
    starts = _bucket_starts()
    assert starts[N_BUCKETS - 1] <= CMP_STRIDE * (BAND_BELOW + 1) - (CMP_LEN - 1)
    assert starts[N_BUCKETS - 1] <= tq and 2 * tq <= BIAS_LEN // 2
    rel = (rel_bias - rel_bias[:, N_BUCKETS - 1:]) * math.log2(math.e)
    d = jnp.arange(BIAS_LEN, dtype=jnp.int32)[None, :]
    fvec = jnp.broadcast_to(rel[:, 0:1], (N_HEADS, BIAS_LEN))
    for k in range(1, N_BUCKETS):
        fvec = jnp.where(d >= starts[k], rel[:, k:k + 1], fvec)
    fvec = jnp.where(d < BIAS_LEN // 2, fvec, NEG).astype(F32).reshape(N_KV, HPG, 1, BIAS_LEN)

    n_chunks = seq_len // CMP_STRIDE
    n_cmp = (seq_len - CMP_LEN) // CMP_STRIDE + 1
    n_sel = seq_len // SEL_LEN
    ci = np.arange(n_chunks)[None, :] * CMP_STRIDE
    sj = np.arange(n_sel)[:, None] * SEL_LEN
    ovl = ((ci < sj + SEL_LEN) & (ci + CMP_LEN > sj) & (np.arange(n_chunks)[None, :] < n_cmp))
    e_t = (np.arange(seq_len)[:, None] // SEL_LEN == np.arange(LANE)[None, :])
    return fvec, jnp.asarray(ovl, BF16), jnp.asarray(e_t, BF16)


def kernel(x, w_in, pe_cmp, w_cmp_k1, w_cmp_k2, w_cmp_v1, w_cmp_v2, conv_w, rel_bias, w_o, w_up, w_down,
           g_pre_mix, g_post_mix, g_pre_ffn, g_post_ffn):
    batch, seq_len, _ = x.shape
    depth = w_in.shape[0]
    tq = 256
    fvec, ovl, e_t = _attention_tables(rel_bias, seq_len, tq)
    x2 = x.reshape(batch * seq_len, D_MODEL)
    for l in range(depth):
        wl = jnp.swapaxes(w_in[l], 0, 1)
        slab = lambda k: wl[ATTN_WIDTH + k * KV_WIDTH:ATTN_WIDTH + (k + 1) * KV_WIDTH]
        w_qkv = jnp.concatenate([wl[:ATTN_WIDTH], slab(2), slab(4), slab(0), slab(1), slab(3), slab(5)],
                                axis=0).astype(BF16)
        wg = wl[GATE_OFF:CONV_OFF].reshape(N_KV, HPG * N_BRANCH, D_MODEL)
        wg = jnp.pad(wg, ((0, 0), (0, LANE - HPG * N_BRANCH), (0, 0))).reshape(N_KV * LANE, D_MODEL).astype(BF16)
        w_h, w_b, w_c = (wl[CONV_OFF + k * CONV_WIDTH:CONV_OFF + (k + 1) * CONV_WIDTH].astype(BF16)
                         for k in range(3))
        g1 = g_pre_mix[l].reshape(1, D_MODEL)

        q, k_all, c_in, vt_all, gates, o_conv = _in_proj(x2, g1, w_qkv, wg, w_h, w_b, w_c, conv_w[l],
                                                         batch, seq_len)

        w1 = jnp.stack([w_cmp_k1[l], w_cmp_v1[l]]).astype(BF16)
        w2 = jnp.stack([w_cmp_k2[l], w_cmp_v2[l]]).astype(BF16)
        cmp, cmp_t = _compress(c_in, pe_cmp[l], w1, w2, batch, seq_len)

        o_attn, (wo_b, wup_b, wdown_b) = _nsa(q, k_all, vt_all, cmp, cmp_t, gates, fvec, ovl, e_t,
                                              [w_o[l], w_up[l], w_down[l]], batch, seq_len, tq=tq)

        x1, h2 = _oproj(o_attn, o_conv, wo_b, x2,
                        g_post_mix[l].reshape(1, D_MODEL), g_pre_ffn[l].reshape(1, D_MODEL))
        x2 = _ffn(h2, wup_b, wdown_b, x1, g_post_ffn[l].reshape(1, D_MODEL))
    return x2.reshape(batch, seq_len, D_MODEL)
```

```python
import functools
import math

import numpy as np
import jax
import jax.numpy as jnp
from jax import lax
from jax.experimental import pallas as pl
from jax.experimental.pallas import tpu as pltpu

F32 = jnp.float32
BF16 = jnp.bfloat16

D_MODEL = 2048
N_HEADS = 8
N_KV = 2
HPG = N_HEADS // N_KV
HEAD_DIM = 128
ATTN_WIDTH = N_HEADS * HEAD_DIM
KV_WIDTH = N_KV * HEAD_DIM
CONV_WIDTH = D_MODEL - ATTN_WIDTH
CONV_K = 3
N_BRANCH = 3
CMP_LEN = 32
CMP_STRIDE = 16
CMP_HIDDEN = 256
SEL_LEN = 64
SEL_TOPK = 16
WINDOW = 512
N_BUCKETS = 32
MAX_DIST = 128
EPS = 1e-6
NEG = -1e30
HALF_NEG = -5e29
FORCE = 1e9

QKV_WIDTH = ATTN_WIDTH + 6 * KV_WIDTH
GATE_OFF = QKV_WIDTH
CONV_OFF = QKV_WIDTH + N_HEADS * N_BRANCH
LANE = 128
VMEM_LIMIT = 56 * 1024 * 1024

_DN_T = (((1,), (1,)), ((), ()))


def _rms(x, g):
    ms = jnp.mean(x * x, axis=-1, keepdims=True)
    return x * lax.rsqrt(ms + EPS) * g


def _params(n_axes):
    return pltpu.CompilerParams(dimension_semantics=("arbitrary",) * n_axes, vmem_limit_bytes=VMEM_LIMIT)


QKV_BLOCK = 2 * KV_WIDTH
SLAB_KC, SLAB_VC, SLAB_KS, SLAB_VS, SLAB_KW, SLAB_VW = range(6)
CONV_BLOCK = 512


def _inproj_kernel(x_ref, g_ref, w_ref, wg_ref, wh_ref, wb_ref, wc_ref, cw_ref,
                   q_ref, k_ref, c_ref, vt_ref, gate_ref, ov_ref, h_ref, carry_ref, *, q_scale, tiles_per_seq):
    i = pl.program_id(0)
    tm = x_ref.shape[0]
    h_ref[...] = _rms(x_ref[...], g_ref[...]).astype(BF16)
    gate_ref[...] = jax.nn.sigmoid(lax.dot_general(h_ref[...], wg_ref[...], _DN_T, preferred_element_type=F32))

    def proj(w, j, width):
        return lax.dot_general(h_ref[...], w[j * width:(j + 1) * width, :], _DN_T, preferred_element_type=F32)

    for j in range(ATTN_WIDTH // QKV_BLOCK):
        q_ref[:, j * QKV_BLOCK:(j + 1) * QKV_BLOCK] = (proj(w_ref, j, QKV_BLOCK) * q_scale).astype(BF16)

    def slab(which):
        return proj(w_ref, ATTN_WIDTH // KV_WIDTH + which, KV_WIDTH)

    ones = jnp.where(lax.broadcasted_iota(jnp.int32, (ONES_ROWS, tm), 0) == 0, 1.0, 0.0).astype(BF16)
    for pair, (k_slab, c_slab, v_slab) in enumerate(((SLAB_KS, SLAB_KC, SLAB_VS), (SLAB_KW, SLAB_VC, SLAB_VW))):
        cols = slice(pair * KV_WIDTH, (pair + 1) * KV_WIDTH)
        k_ref[:, cols] = slab(k_slab).astype(BF16)
        c_ref[:, cols] = slab(c_slab)
        v = slab(v_slab)
        for g in range(N_KV):
            s = pair * N_KV + g
            vt_ref[s, 0:HEAD_DIM, :] = v[:, g * HEAD_DIM:(g + 1) * HEAD_DIM].T.astype(BF16)
            vt_ref[s, HEAD_DIM:HEAD_DIM + ONES_ROWS, :] = ones

    row = lax.broadcasted_iota(jnp.int32, (tm, CONV_BLOCK), 0)
    for j in range(CONV_WIDTH // CONV_BLOCK):
        cols = slice(j * CONV_BLOCK, (j + 1) * CONV_BLOCK)
        u = proj(wc_ref, j, CONV_BLOCK) * proj(wh_ref, j, CONV_BLOCK)
        prev = carry_ref[j]
        prev = jnp.where(i % tiles_per_seq == 0, 0.0, prev)
        carry_ref[j] = u[tm - 8:tm, :]
        u1 = jnp.where(row == 0, prev[7:8, :], pltpu.roll(u, 1, axis=0))
        u2 = jnp.where(row == 0, prev[6:7, :], jnp.where(row == 1, prev[7:8, :], pltpu.roll(u, 2, axis=0)))
        w = cw_ref[:, cols]
        y = w[0:1, :] * u2
        y = y + w[1:2, :] * u1
        y = y + w[2:3, :] * u
        ov_ref[:, cols] = (proj(wb_ref, j, CONV_BLOCK) * y).astype(BF16)


def _in_proj(x2, g, w_qkv, w_gate, w_h, w_b, w_c, conv_w, batch, seq_len, *, tm=512):
    m = x2.shape[0]
    gw = w_gate.shape[0]
    tps = seq_len // tm
    n_slabs = QKV_BLOCK // HEAD_DIM
    kern = functools.partial(_inproj_kernel, q_scale=HEAD_DIM ** -0.5 * math.log2(math.e), tiles_per_seq=tps)
    row = lambda i: (i, 0)

    def resident(shape):
        return pl.BlockSpec(shape, lambda i: (0, 0), pipeline_mode=pl.Buffered(1))

    return pl.pallas_call(
        kern,
        grid=(m // tm,),
        in_specs=[
            pl.BlockSpec((tm, D_MODEL), row),
            resident((1, D_MODEL)),
            resident(w_qkv.shape),
            resident(w_gate.shape),
            resident(w_h.shape), resident(w_b.shape), resident(w_c.shape),
            resident(conv_w.shape),
        ],
        out_specs=[
            pl.BlockSpec((tm, ATTN_WIDTH), row),
            pl.BlockSpec((tm, QKV_BLOCK), row),
            pl.BlockSpec((tm, QKV_BLOCK), row),
            pl.BlockSpec((None, n_slabs, HEAD_DIM + ONES_ROWS, tm), lambda i: (i // tps, 0, 0, i % tps)),
            pl.BlockSpec((tm, gw), row),
            pl.BlockSpec((tm, CONV_WIDTH), row),
        ],
        out_shape=[
            jax.ShapeDtypeStruct((m, ATTN_WIDTH), BF16),
            jax.ShapeDtypeStruct((m, QKV_BLOCK), BF16),
            jax.ShapeDtypeStruct((m, QKV_BLOCK), F32),
            jax.ShapeDtypeStruct((batch, n_slabs, HEAD_DIM + ONES_ROWS, seq_len), BF16),
            jax.ShapeDtypeStruct((m, gw), F32),
            jax.ShapeDtypeStruct((m, CONV_WIDTH), BF16),
        ],
        scratch_shapes=[pltpu.VMEM((tm, D_MODEL), BF16),
                        pltpu.VMEM((CONV_WIDTH // CONV_BLOCK, 8, CONV_BLOCK), F32)],
        compiler_params=_params(1),
        name="in_proj",
    )(x2, g, w_qkv, w_gate, w_h, w_b, w_c, conv_w)


def _compress_kernel(x_ref, pe_ref, w1_ref, w2_ref, o_ref, ot_ref):
    n = x_ref.shape[0] // CMP_STRIDE
    a = jnp.zeros((n, CMP_HIDDEN), F32)
    b = jnp.zeros((n, CMP_HIDDEN), F32)
    for l in range(CMP_STRIDE):
        xl = x_ref[pl.ds(l, n, stride=CMP_STRIDE), :]
        xa = (xl + pe_ref[l:l + 1, :]).astype(BF16)
        xb = (xl + pe_ref[CMP_STRIDE + l:CMP_STRIDE + l + 1, :]).astype(BF16)
        a = a + jnp.dot(xa, w1_ref[l], preferred_element_type=F32)
        b = b + jnp.dot(xb, w1_ref[CMP_STRIDE + l], preferred_element_type=F32)
    pre = a + pltpu.roll(b, n - 1, axis=0)
    hid = pre * jax.nn.sigmoid(pre)
    out = jnp.dot(hid.astype(BF16), w2_ref[...], preferred_element_type=F32)
    row = lax.broadcasted_iota(jnp.int32, out.shape, 0)
    out = jnp.where(row < n - 1, out, 0.0)
    o_ref[...] = out.astype(BF16)
    ot_ref[...] = out.T.astype(BF16)


def _compress(c_in, pe, w1, w2, batch, seq_len):
    n_slabs = c_in.shape[1] // HEAD_DIM
    n_chunks = seq_len // CMP_STRIDE
    return pl.pallas_call(
        _compress_kernel,
        grid=(batch, n_slabs),
        in_specs=[
            pl.BlockSpec((seq_len, HEAD_DIM), lambda i, j: (i, j)),
            pl.BlockSpec((CMP_LEN, HEAD_DIM), lambda i, j: (0, 0)),
            pl.BlockSpec((None, CMP_LEN, HEAD_DIM, CMP_HIDDEN), lambda i, j: (j // N_KV, 0, 0, 0)),
            pl.BlockSpec((None, CMP_HIDDEN, HEAD_DIM), lambda i, j: (j // N_KV, 0, 0)),
        ],
        out_specs=[
            pl.BlockSpec((None, None, n_chunks, HEAD_DIM), lambda i, j: (i, j, 0, 0)),
            pl.BlockSpec((None, None, HEAD_DIM, n_chunks), lambda i, j: (i, j, 0, 0)),
        ],
        out_shape=[
            jax.ShapeDtypeStruct((batch, n_slabs, n_chunks, HEAD_DIM), BF16),
            jax.ShapeDtypeStruct((batch, n_slabs, HEAD_DIM, n_chunks), BF16),
        ],
        compiler_params=_params(2),
        name="compress",
    )(c_in, pe, w1, w2)


TAB_DIAG, TAB_SUB, TAB_FAR, TAB_MASK = 0, 1, 2, 3
BAND_BELOW = 8
ONES_ROWS = 16


N_NSA_INPUTS = 11
BIAS_LEN = 1024
FAR_UNROLL = 4


def _nsa_kernel(*refs, tq, n_sel, n_cast):
    (q_ref, kc_ref, vct_ref, ks_ref, vst_ref, kw_ref, vwt_ref, gate_ref, fvec_ref,
     ovl_ref, et_ref) = refs[:N_NSA_INPUTS]
    cast_in = refs[N_NSA_INPUTS:N_NSA_INPUTS + n_cast]
    o_ref = refs[N_NSA_INPUTS + n_cast]
    cast_out = refs[N_NSA_INPUTS + n_cast + 1:N_NSA_INPUTS + 2 * n_cast + 1]
    (kaug_ref, tab_ref, band_ref, sc_ref, s0_ref, s1_ref, m_ref, mt_ref,
     acc_ref) = refs[N_NSA_INPUTS + 2 * n_cast + 1:]
    i = pl.program_id(2)

    for src, dst in zip(cast_in, cast_out):
        dst[...] = src[...].astype(BF16)

    tk = tq
    mcols = HPG * tq
    nw = WINDOW // tk
    cpt = tq // CMP_STRIDE
    band_rows = BAND_BELOW + cpt

    @pl.when(i == 0)
    def _():
        kaug_ref[:, 0:HEAD_DIM] = ks_ref[...]
        kaug_ref[:, HEAD_DIM:2 * HEAD_DIM] = et_ref[...]

    @pl.when(i == 0)
    def _():
        c = lax.broadcasted_iota(jnp.int32, (tk, tq), 0)
        r = lax.broadcasted_iota(jnp.int32, (tk, tq), 1)
        far = jnp.where(r < c, 0.0, NEG)
        lane = lax.broadcasted_iota(jnp.int32, (1, BIAS_LEN), 1)
        for h in range(HPG):
            cols = slice(h * tq, (h + 1) * tq)
            f = fvec_ref[h]
            f_diag = jnp.where(lane < tq, f, NEG)
            x = pltpu.roll(jnp.broadcast_to(f_diag, (tk, BIAS_LEN)), 0, 1, stride=1, stride_axis=0)
            tab_ref[TAB_DIAG, :, cols] = x[:, 0:tq]
            x = pltpu.roll(jnp.broadcast_to(f, (tk, BIAS_LEN)), 0, 1, stride=1, stride_axis=0)
            tab_ref[TAB_SUB, :, cols] = x[:, tq:2 * tq]
            tab_ref[TAB_FAR, :, cols] = far
            tab_ref[TAB_MASK, :, cols] = jnp.full((tk, tq), NEG, F32)
            for v, first in enumerate((0, -BAND_BELOW)):
                shift = (CMP_STRIDE * first + CMP_LEN - 1) % BIAS_LEN
                f_shift = pltpu.roll(f, shift, 1)
                x = pltpu.roll(jnp.broadcast_to(f_shift, (band_rows, BIAS_LEN)), 0, 1,
                               stride=CMP_STRIDE, stride_axis=0)
                band_ref[v, :, cols] = x[:, 0:tq]

    q = q_ref[...]
    qs = jnp.concatenate([q[:, h * HEAD_DIM:(h + 1) * HEAD_DIM] for h in range(HPG)], axis=0)

    s_refs = (s0_ref, s1_ref)

    def init():
        m_ref[...] = jnp.full((1, mcols), NEG, F32)
        acc_ref[...] = jnp.zeros(acc_ref.shape, F32)

    def keys(kt):
        return pl.ds(pl.multiple_of(kt * tk, tk), tk)

    def qk(qmat, k_ref, kt, kind, buf):
        s = lax.dot_general(k_ref[keys(kt), :], qmat, _DN_T, preferred_element_type=F32)
        if kind is not None:
            s = s + tab_ref[kind]
        s_refs[buf][...] = s
        return jnp.max(s, axis=0, keepdims=True)

    def process(vt_ref, kt, buf, m_tile):
        m_prev = m_ref[...]
        m_next = jnp.maximum(m_prev, m_tile)
        alpha = jnp.exp2(m_prev - m_next)
        p = jnp.exp2(s_refs[buf][...] - m_next).astype(BF16)
        acc_ref[...] = alpha * acc_ref[...] + jnp.dot(vt_ref[:, keys(kt)], p, preferred_element_type=F32)
        m_ref[...] = m_next

    def finish():
        return acc_ref[0:HEAD_DIM, :] * (1.0 / acc_ref[HEAD_DIM:HEAD_DIM + 1, :])

    assert nw == 2, "window tiles are i, i-1 (previous-tile table) and i-nw (window-edge table)"
    init()
    mt_w0 = qk(qs, kw_ref, i, TAB_DIAG, 0)

    raw = lax.dot_general(kc_ref[...], qs, _DN_T, preferred_element_type=F32)
    crow = lax.broadcasted_iota(jnp.int32, raw.shape, 0)
    sc_ref[...] = jnp.where(crow < cpt * (i + 1), raw, NEG)
    band = pl.ds(pl.multiple_of(jnp.maximum(cpt * i - BAND_BELOW, 0), 8), band_rows)
    sc_ref[band, :] += band_ref[jnp.minimum(i, 1)]

    mt_w1 = qk(qs, kw_ref, jnp.maximum(i - 1, 0), jnp.where(i >= 1, TAB_SUB, TAB_MASK), 1)
    process(vwt_ref, i, 0, mt_w0)

    sc = sc_ref[...]
    mc = jnp.maximum(jnp.max(sc, axis=0, keepdims=True), HALF_NEG)
    pc = jnp.exp2(sc - mc)
    lc = jnp.sum(pc, axis=0, keepdims=True)
    pc = pc * jnp.where(lc > 0.0, 1.0 / lc, 0.0)
    o_c = jnp.dot(vct_ref[...], pc.astype(BF16), preferred_element_type=F32)

    mt_w2 = qk(qs, kw_ref, jnp.maximum(i - nw, 0), jnp.where(i >= nw, TAB_FAR, TAB_MASK), 0)
    process(vwt_ref, jnp.maximum(i - 1, 0), 1, mt_w1)

    ps = pc[:, 0:tq] + pc[:, tq:2 * tq] + pc[:, 2 * tq:3 * tq] + pc[:, 3 * tq:4 * tq]
    hi = ps.astype(BF16)
    r1 = ps - hi.astype(F32)
    mid = r1.astype(BF16)
    lo = (r1 - mid.astype(F32)).astype(BF16)
    ovl = ovl_ref[...]
    imp = (jnp.dot(ovl, hi, preferred_element_type=F32) + jnp.dot(ovl, mid, preferred_element_type=F32)
           + jnp.dot(ovl, lo, preferred_element_type=F32))
    jj = lax.broadcasted_iota(jnp.int32, (n_sel, tq), 0)
    tt = i * tq + lax.broadcasted_iota(jnp.int32, (n_sel, tq), 1)
    cur = tt >> int(math.log2(SEL_LEN))
    forced = (jj == 0) | (jj == cur) | (jj == cur - 1)
    imp = jnp.where(forced, FORCE, imp)
    imp = jnp.where(jj * SEL_LEN <= tt, imp, NEG)
    sub = 8
    ranks = []
    for j0 in range(0, n_sel, sub):
        blk = imp[j0:j0 + sub, :]
        jl = j0 + lax.broadcasted_iota(jnp.int32, blk.shape, 0)
        cnt = jnp.zeros(blk.shape, jnp.int32)
        for b in range(n_sel):
            row = imp[b:b + 1, :]
            if b < j0:
                cnt = cnt + jnp.where(row >= blk, 1, 0)
            elif b >= j0 + sub:
                cnt = cnt + jnp.where(row > blk, 1, 0)
            else:
                cnt = cnt + jnp.where(row > blk, 1, jnp.where(row == blk, jnp.where(jl > b, 1, 0), 0))
        ranks.append(cnt)
    rank = jnp.concatenate(ranks, axis=0)
    selb_t = jnp.where(rank < SEL_TOPK, 0.0, NEG)

    process(vwt_ref, jnp.maximum(i - nw, 0), 0, mt_w2)
    o_w = finish()

    selb = jnp.concatenate([selb_t, jnp.zeros((LANE - n_sel, tq), F32)], axis=0).T
    selb = selb.astype(BF16)
    qa = jnp.concatenate([qs, jnp.concatenate([selb] * HPG, axis=0)], axis=1)

    init()

    @pl.when(i == 0)
    def _():
        process(vst_ref, 0, 0, qk(qa, kaug_ref, 0, TAB_DIAG, 0))

    @pl.when(i >= 1)
    def _():
        mt0 = qk(qa, kaug_ref, i, TAB_DIAG, 0)
        mt1 = qk(qa, kaug_ref, i - 1, TAB_SUB, 1)
        process(vst_ref, i, 0, mt0)
        mt0 = qk(qa, kaug_ref, jnp.maximum(i - 2, 0), None, 0)
        process(vst_ref, i - 1, 1, mt1)
        n_far = i - 1
        rem = n_far % FAR_UNROLL

        def far_tiles(a, n, mt0):
            for t in range(n):
                nxt = qk(qa, kaug_ref, jnp.maximum(a - t - 1, 0), None, (t + 1) % 2)
                process(vst_ref, a - t, t % 2, mt0)
                mt0 = nxt
            return mt0

        mt0 = lax.fori_loop(0, n_far // FAR_UNROLL,
                            lambda p, mt: far_tiles(i - 2 - FAR_UNROLL * p, FAR_UNROLL, mt), mt0)

        @pl.when(rem >= 2)
        def _():
            mt_ref[...] = far_tiles(rem - 1, 2, mt0)

        @pl.when(rem == 1)
        def _():
            process(vst_ref, 0, 0, mt0)

        @pl.when(rem == 3)
        def _():
            process(vst_ref, 0, 0, mt_ref[...])

    o_s = finish()

    gt = gate_ref[...].T
    for h in range(HPG):
        sl = slice(h * tq, (h + 1) * tq)
        o = gt[3 * h:3 * h + 1, :] * o_c[:, sl]
        o = o + gt[3 * h + 1:3 * h + 2, :] * o_s[:, sl]
        o = o + gt[3 * h + 2:3 * h + 3, :] * o_w[:, sl]
        o_ref[:, h * HEAD_DIM:(h + 1) * HEAD_DIM] = o.T.astype(BF16)


def _nsa(q, k_all, vt_all, cmp, cmp_t, gates, fvec, ovl, e_t, cast_weights, batch, seq_len, *, tq):
    n_t = seq_len // tq
    n_steps = batch * N_KV * n_t
    step_row = lambda b, g, i: ((b * N_KV + g) * n_t + i, 0)
    cast_specs = [pl.BlockSpec((w.shape[0] // n_steps, w.shape[1]), step_row) for w in cast_weights]
    n_sel = seq_len // SEL_LEN
    n_chunks = cmp.shape[2]
    mcols = HPG * tq
    band_rows = BAND_BELOW + tq // CMP_STRIDE
    qw = HPG * HEAD_DIM

    def vt_spec(first):
        return pl.BlockSpec((None, None, HEAD_DIM + ONES_ROWS, seq_len), lambda b, g, i: (b, first + g, 0, 0))

    kern = functools.partial(_nsa_kernel, tq=tq, n_sel=n_sel, n_cast=len(cast_weights))
    o_attn, *cast = pl.pallas_call(
        kern,
        grid=(batch, N_KV, n_t),
        in_specs=[
            pl.BlockSpec((tq, qw), lambda b, g, i: (b * n_t + i, g)),
            pl.BlockSpec((None, None, n_chunks, HEAD_DIM), lambda b, g, i: (b, g, 0, 0)),
            pl.BlockSpec((None, None, HEAD_DIM, n_chunks), lambda b, g, i: (b, N_KV + g, 0, 0)),
            pl.BlockSpec((seq_len, HEAD_DIM), lambda b, g, i: (b, g)),
            vt_spec(0),
            pl.BlockSpec((seq_len, HEAD_DIM), lambda b, g, i: (b, N_KV + g)),
            vt_spec(N_KV),
            pl.BlockSpec((tq, LANE), lambda b, g, i: (b * n_t + i, g)),
            pl.BlockSpec((None, HPG, 1, BIAS_LEN), lambda b, g, i: (g, 0, 0, 0)),
            pl.BlockSpec((n_sel, n_chunks), lambda b, g, i: (0, 0)),
            pl.BlockSpec((seq_len, LANE), lambda b, g, i: (0, 0)),
        ] + cast_specs,
        out_specs=[pl.BlockSpec((tq, qw), lambda b, g, i: (b * n_t + i, g))] + cast_specs,
        out_shape=[jax.ShapeDtypeStruct((batch * seq_len, ATTN_WIDTH), BF16)]
        + [jax.ShapeDtypeStruct(w.shape, BF16) for w in cast_weights],
        scratch_shapes=[
            pltpu.VMEM((seq_len, 2 * HEAD_DIM), BF16),
            pltpu.VMEM((4, tq, mcols), F32),
            pltpu.VMEM((2, band_rows, mcols), F32),
            pltpu.VMEM((n_chunks, mcols), F32),
            pltpu.VMEM((tq, mcols), F32),
            pltpu.VMEM((tq, mcols), F32),
            pltpu.VMEM((1, mcols), F32),
            pltpu.VMEM((1, mcols), F32),
            pltpu.VMEM((HEAD_DIM + ONES_ROWS, mcols), F32),
        ],
        compiler_params=_params(3),
        name="nsa_attention",
    )(q, cmp, cmp_t, k_all, vt_all, k_all, vt_all, gates, fvec, ovl, e_t, *cast_weights)
    return o_attn, cast


def _oproj_kernel(oa_ref, ov_ref, wo_ref, x_ref, g1_ref, g2_ref, x1_ref, h2_ref):
    ka = oa_ref.shape[1]
    tm = oa_ref.shape[0]
    for rows in (slice(k * tm // 4, (k + 1) * tm // 4) for k in range(4)):
        mix = jnp.dot(oa_ref[rows, :], wo_ref[0:ka, :], preferred_element_type=F32)
        mix = mix + jnp.dot(ov_ref[rows, :], wo_ref[ka:, :], preferred_element_type=F32)
        x1 = x_ref[rows, :] + _rms(mix, g1_ref[...])
        x1_ref[rows, :] = x1
        h2_ref[rows, :] = _rms(x1, g2_ref[...]).astype(BF16)


def _oproj(o_attn, o_conv, w_o, x2, g_post, g_pre, *, tm=512):
    m = x2.shape[0]
    ka, kv = o_attn.shape[1], o_conv.shape[1]
    row = lambda i: (i, 0)
    fixed = lambda i: (0, 0)
    return pl.pallas_call(
        _oproj_kernel,
        grid=(m // tm,),
        in_specs=[
            pl.BlockSpec((tm, ka), row),
            pl.BlockSpec((tm, kv), row),
            pl.BlockSpec((ka + kv, D_MODEL), fixed),
            pl.BlockSpec((tm, D_MODEL), row),
            pl.BlockSpec((1, D_MODEL), fixed),
            pl.BlockSpec((1, D_MODEL), fixed),
        ],
        out_specs=[pl.BlockSpec((tm, D_MODEL), row), pl.BlockSpec((tm, D_MODEL), row)],
        out_shape=[jax.ShapeDtypeStruct((m, D_MODEL), F32), jax.ShapeDtypeStruct((m, D_MODEL), BF16)],
        compiler_params=_params(1),
        name="out_proj",
    )(o_attn, o_conv, w_o, x2, g_post, g_pre)


def _ffn_kernel(h_ref, wu_ref, wd_ref, x1_ref, g_ref, o_ref, acc_ref):
    j = pl.program_id(1)

    @pl.when(j == 0)
    def _():
        acc_ref[...] = jnp.zeros(acc_ref.shape, F32)

    a = jnp.dot(h_ref[...], wu_ref[...], preferred_element_type=F32)
    a = jnp.square(jnp.maximum(a, 0.0)).astype(BF16)
    acc_ref[...] += jnp.dot(a, wd_ref[...], preferred_element_type=F32)

    @pl.when(j == pl.num_programs(1) - 1)
    def _():
        o_ref[...] = x1_ref[...] + _rms(acc_ref[...], g_ref[...])


def _ffn(h2, w_up, w_down, x1, g_post, *, tm=512, tf=1024):
    m = h2.shape[0]
    d_ff = w_up.shape[1]
    return pl.pallas_call(
        _ffn_kernel,
        grid=(m // tm, d_ff // tf),
        in_specs=[
            pl.BlockSpec((tm, D_MODEL), lambda i, j: (i, 0)),
            pl.BlockSpec((D_MODEL, tf), lambda i, j: (0, j)),
            pl.BlockSpec((tf, D_MODEL), lambda i, j: (j, 0)),
            pl.BlockSpec((tm, D_MODEL), lambda i, j: (i, 0)),
            pl.BlockSpec((1, D_MODEL), lambda i, j: (0, 0)),
        ],
        out_specs=pl.BlockSpec((tm, D_MODEL), lambda i, j: (i, 0)),
        out_shape=jax.ShapeDtypeStruct((m, D_MODEL), F32),
        scratch_shapes=[pltpu.VMEM((tm, D_MODEL), F32)],
        compiler_params=_params(2),
        name="ffn",
    )(h2, w_up, w_down, x1, g_post)


def _bucket_np(dist):
    n = np.maximum(dist, 0)
    max_exact = N_BUCKETS // 2
    nf = np.maximum(n, 1).astype(np.float32)
    large = max_exact + (np.log(nf / np.float32(max_exact)) / np.float32(math.log(MAX_DIST / max_exact))
                         * np.float32(N_BUCKETS - max_exact)).astype(np.int32)
    large = np.minimum(large, N_BUCKETS - 1)
    return np.where(n < max_exact, n, large)


def _bucket_starts():
    b = _bucket_np(np.arange(4 * MAX_DIST))
    return [int(np.argmax(b == k)) for k in range(N_BUCKETS)]


def _attention_tables(rel_bias, seq_len, tq):
    starts = _bucket_starts()
    assert starts[N_BUCKETS - 1] <= CMP_STRIDE * (BAND_BELOW + 1) - (CMP_LEN - 1)
    assert starts[N_BUCKETS - 1] <= tq and 2 * tq <= BIAS_LEN // 2
    rel = (rel_bias - rel_bias[:, N_BUCKETS - 1:]) * math.log2(math.e)
    d = jnp.arange(BIAS_LEN, dtype=jnp.int32)[None, :]
    fvec = jnp.broadcast_to(rel[:, 0:1], (N_HEADS, BIAS_LEN))
    for k in range(1, N_BUCKETS):
        fvec = jnp.where(d >= starts[k], rel[:, k:k + 1], fvec)
    fvec = jnp.where(d < BIAS_LEN // 2, fvec, NEG).astype(F32).reshape(N_KV, HPG, 1, BIAS_LEN)

    n_chunks = seq_len // CMP_STRIDE
    n_cmp = (seq_len - CMP_LEN) // CMP_STRIDE + 1
    n_sel = seq_len // SEL_LEN
    ci = np.arange(n_chunks)[None, :] * CMP_STRIDE
    sj = np.arange(n_sel)[:, None] * SEL_LEN
    ovl = ((ci < sj + SEL_LEN) & (ci + CMP_LEN > sj) & (np.arange(n_chunks)[None, :] < n_cmp))
    e_t = (np.arange(seq_len)[:, None] // SEL_LEN == np.arange(LANE)[None, :])
    return fvec, jnp.asarray(ovl, BF16), jnp.asarray(e_t, BF16)


def kernel(x, w_in, pe_cmp, w_cmp_k1, w_cmp_k2, w_cmp_v1, w_cmp_v2, conv_w, rel_bias, w_o, w_up, w_down,
           g_pre_mix, g_post_mix, g_pre_ffn, g_post_ffn):
    batch, seq_len, _ = x.shape
    depth = w_in.shape[0]
    tq = 256
    fvec, ovl, e_t = _attention_tables(rel_bias, seq_len, tq)
    x2 = x.reshape(batch * seq_len, D_MODEL)
    for l in range(depth):
        wl = jnp.swapaxes(w_in[l], 0, 1)
        w_qkv = wl[:QKV_WIDTH].astype(BF16)
        wg = wl[GATE_OFF:CONV_OFF].reshape(N_KV, HPG * N_BRANCH, D_MODEL)
        wg = jnp.pad(wg, ((0, 0), (0, LANE - HPG * N_BRANCH), (0, 0))).reshape(N_KV * LANE, D_MODEL).astype(BF16)
        w_h, w_b, w_c = (wl[CONV_OFF + k * CONV_WIDTH:CONV_OFF + (k + 1) * CONV_WIDTH].astype(BF16)
                         for k in range(3))
        g1 = g_pre_mix[l].reshape(1, D_MODEL)

        q, k_all, c_in, vt_all, gates, o_conv = _in_proj(x2, g1, w_qkv, wg, w_h, w_b, w_c, conv_w[l],
                                                         batch, seq_len)

        w1 = jnp.stack([w_cmp_k1[l], w_cmp_v1[l]]).astype(BF16)
        w2 = jnp.stack([w_cmp_k2[l], w_cmp_v2[l]]).astype(BF16)
        cmp, cmp_t = _compress(c_in, pe_cmp[l], w1, w2, batch, seq_len)

        o_attn, (wo_b, wup_b, wdown_b) = _nsa(q, k_all, vt_all, cmp, cmp_t, gates, fvec, ovl, e_t,
                                              [w_o[l], w_up[l], w_down[l]], batch, seq_len, tq=tq)

        x1, h2 = _oproj(o_attn, o_conv, wo_b, x2,
                        g_post_mix[l].reshape(1, D_MODEL), g_pre_ffn[l].reshape(1, D_MODEL))
        x2 = _ffn(h2, wup_b, wdown_b, x1, g_post_ffn[l].reshape(1, D_MODEL))
    return x2.reshape(batch, seq_len, D_MODEL)
```

```python
import functools
import math

import numpy as np
import jax
import jax.numpy as jnp
from jax import lax
from jax.experimental import pallas as pl
from jax.experimental.pallas import tpu as pltpu

F32 = jnp.float32
BF16 = jnp.bfloat16

D_MODEL = 2048
N_HEADS = 8
N_KV = 2
HPG = N_HEADS // N_KV
HEAD_DIM = 128
ATTN_WIDTH = N_HEADS * HEAD_DIM
KV_WIDTH = N_KV * HEAD_DIM
CONV_WIDTH = D_MODEL - ATTN_WIDTH
CONV_K = 3
N_BRANCH = 3
CMP_LEN = 32
CMP_STRIDE = 16
CMP_HIDDEN = 256
SEL_LEN = 64
SEL_TOPK = 16
WINDOW = 512
N_BUCKETS = 32
MAX_DIST = 128
EPS = 1e-6
NEG = -1e30
HALF_NEG = -5e29
FORCE = 1e9

QKV_WIDTH = ATTN_WIDTH + 6 * KV_WIDTH
GATE_OFF = QKV_WIDTH
CONV_OFF = QKV_WIDTH + N_HEADS * N_BRANCH
LANE = 128
VMEM_LIMIT = 56 * 1024 * 1024

_DN_T = (((1,), (1,)), ((), ()))


def _rms(x, g):
    ms = jnp.mean(x * x, axis=-1, keepdims=True)
    return x * lax.rsqrt(ms + EPS) * g


def _params(n_axes):
    return pltpu.CompilerParams(dimension_semantics=("arbitrary",) * n_axes, vmem_limit_bytes=VMEM_LIMIT)


QKV_BLOCK = 2 * KV_WIDTH
SLAB_KC, SLAB_VC, SLAB_KS, SLAB_VS, SLAB_KW, SLAB_VW = range(6)
CONV_BLOCK = 512
W_CHUNK = 256
SUBLANE = 8


def _load_weights(wt_hbm, w_ref, wg_ref, wconv_ref, stage_ref, sem):
    n_gate = N_HEADS * N_BRANCH
    per_group = HPG * N_BRANCH

    def store_rows(dst, row):
        def store(v):
            dst[row:row + v.shape[0], :] = v.astype(BF16)
        return store

    def store_gates(v):
        wg_ref[...] = jnp.zeros(wg_ref.shape, BF16)
        pad = jnp.zeros((2 * SUBLANE - per_group, v.shape[1]), F32)
        for g in range(N_KV):
            rows = jnp.concatenate([v[g * per_group:(g + 1) * per_group, :], pad], axis=0)
            wg_ref[g * LANE:g * LANE + 2 * SUBLANE, :] = rows.astype(BF16)

    chunks = [(r, W_CHUNK, store_rows(w_ref, r)) for r in range(0, QKV_WIDTH, W_CHUNK)]
    chunks.append((GATE_OFF, n_gate, store_gates))
    chunks += [(CONV_OFF + r, W_CHUNK, store_rows(wconv_ref, r)) for r in range(0, 3 * CONV_WIDTH, W_CHUNK)]

    def copy(k):
        src, n, _ = chunks[k]
        slot = k % 2
        return pltpu.make_async_copy(wt_hbm.at[pl.ds(src, n), :], stage_ref.at[slot, pl.ds(0, n), :], sem.at[slot])

    copy(0).start()
    for k, (_, n, store) in enumerate(chunks):
        if k + 1 < len(chunks):
            copy(k + 1).start()
        copy(k).wait()
        store(stage_ref[k % 2, 0:n, :])


def _inproj_kernel(x_ref, g_ref, wt_hbm, cw_ref, q_ref, k_ref, c_ref, vt_ref, gate_ref, ov_ref,
                   h_ref, carry_ref, w_ref, wg_ref, wconv_ref, stage_ref, sem, *, q_scale, tiles_per_seq):
    i = pl.program_id(0)
    tm = x_ref.shape[0]

    @pl.when(i == 0)
    def _():
        _load_weights(wt_hbm, w_ref, wg_ref, wconv_ref, stage_ref, sem)

    h_ref[...] = _rms(x_ref[...], g_ref[...]).astype(BF16)
    gate_ref[...] = jax.nn.sigmoid(lax.dot_general(h_ref[...], wg_ref[...], _DN_T, preferred_element_type=F32))

    def proj(w, j, width):
        return lax.dot_general(h_ref[...], w[j * width:(j + 1) * width, :], _DN_T, preferred_element_type=F32)

    for j in range(ATTN_WIDTH // QKV_BLOCK):
        q_ref[:, j * QKV_BLOCK:(j + 1) * QKV_BLOCK] = (proj(w_ref, j, QKV_BLOCK) * q_scale).astype(BF16)

    def slab(which):
        return proj(w_ref, ATTN_WIDTH // KV_WIDTH + which, KV_WIDTH)

    ones = jnp.where(lax.broadcasted_iota(jnp.int32, (ONES_ROWS, tm), 0) == 0, 1.0, 0.0).astype(BF16)
    for pair, (k_slab, c_slab, v_slab) in enumerate(((SLAB_KS, SLAB_KC, SLAB_VS), (SLAB_KW, SLAB_VC, SLAB_VW))):
        cols = slice(pair * KV_WIDTH, (pair + 1) * KV_WIDTH)
        k_ref[:, cols] = slab(k_slab).astype(BF16)
        c_ref[:, cols] = slab(c_slab)
        v = slab(v_slab)
        for g in range(N_KV):
            s = pair * N_KV + g
            vt_ref[s, 0:HEAD_DIM, :] = v[:, g * HEAD_DIM:(g + 1) * HEAD_DIM].T.astype(BF16)
            vt_ref[s, HEAD_DIM:HEAD_DIM + ONES_ROWS, :] = ones

    def conv_proj(which, j):
        return proj(wconv_ref, which * (CONV_WIDTH // CONV_BLOCK) + j, CONV_BLOCK)

    row = lax.broadcasted_iota(jnp.int32, (tm, CONV_BLOCK), 0)
    for j in range(CONV_WIDTH // CONV_BLOCK):
        cols = slice(j * CONV_BLOCK, (j + 1) * CONV_BLOCK)
        u = conv_proj(2, j) * conv_proj(0, j)
        prev = carry_ref[j]
        prev = jnp.where(i % tiles_per_seq == 0, 0.0, prev)
        carry_ref[j] = u[tm - 8:tm, :]
        u1 = jnp.where(row == 0, prev[7:8, :], pltpu.roll(u, 1, axis=0))
        u2 = jnp.where(row == 0, prev[6:7, :], jnp.where(row == 1, prev[7:8, :], pltpu.roll(u, 2, axis=0)))
        w = cw_ref[:, cols]
        y = w[0:1, :] * u2
        y = y + w[1:2, :] * u1
        y = y + w[2:3, :] * u
        ov_ref[:, cols] = (conv_proj(1, j) * y).astype(BF16)


def _in_proj(x2, g, w_t, conv_w, batch, seq_len, *, tm=512):
    m = x2.shape[0]
    gw = N_KV * LANE
    tps = seq_len // tm
    n_slabs = QKV_BLOCK // HEAD_DIM
    kern = functools.partial(_inproj_kernel, q_scale=HEAD_DIM ** -0.5 * math.log2(math.e), tiles_per_seq=tps)
    row = lambda i: (i, 0)

    def resident(shape):
        return pl.BlockSpec(shape, lambda i: (0, 0), pipeline_mode=pl.Buffered(1))

    return pl.pallas_call(
        kern,
        grid=(m // tm,),
        in_specs=[
            pl.BlockSpec((tm, D_MODEL), row),
            resident((1, D_MODEL)),
            pl.BlockSpec(memory_space=pl.ANY),
            resident(conv_w.shape),
        ],
        out_specs=[
            pl.BlockSpec((tm, ATTN_WIDTH), row),
            pl.BlockSpec((tm, QKV_BLOCK), row),
            pl.BlockSpec((tm, QKV_BLOCK), row),
            pl.BlockSpec((None, n_slabs, HEAD_DIM + ONES_ROWS, tm), lambda i: (i // tps, 0, 0, i % tps)),
            pl.BlockSpec((tm, gw), row),
            pl.BlockSpec((tm, CONV_WIDTH), row),
        ],
        out_shape=[
            jax.ShapeDtypeStruct((m, ATTN_WIDTH), BF16),
            jax.ShapeDtypeStruct((m, QKV_BLOCK), BF16),
            jax.ShapeDtypeStruct((m, QKV_BLOCK), F32),
            jax.ShapeDtypeStruct((batch, n_slabs, HEAD_DIM + ONES_ROWS, seq_len), BF16),
            jax.ShapeDtypeStruct((m, gw), F32),
            jax.ShapeDtypeStruct((m, CONV_WIDTH), BF16),
        ],
        scratch_shapes=[pltpu.VMEM((tm, D_MODEL), BF16),
                        pltpu.VMEM((CONV_WIDTH // CONV_BLOCK, 8, CONV_BLOCK), F32),
                        pltpu.VMEM((QKV_WIDTH, D_MODEL), BF16),
                        pltpu.VMEM((gw, D_MODEL), BF16),
                        pltpu.VMEM((3 * CONV_WIDTH, D_MODEL), BF16),
                        pltpu.VMEM((2, W_CHUNK, D_MODEL), F32),
                        pltpu.SemaphoreType.DMA((2,))],
        compiler_params=_params(1),
        name="in_proj",
    )(x2, g, w_t, conv_w)


def _compress_kernel(x_ref, pe_ref, w1_ref, w2_ref, o_ref, ot_ref):
    n = x_ref.shape[0] // CMP_STRIDE
    a = jnp.zeros((n, CMP_HIDDEN), F32)
    b = jnp.zeros((n, CMP_HIDDEN), F32)
    for l in range(CMP_STRIDE):
        xl = x_ref[pl.ds(l, n, stride=CMP_STRIDE), :]
        xa = (xl + pe_ref[l:l + 1, :]).astype(BF16)
        xb = (xl + pe_ref[CMP_STRIDE + l:CMP_STRIDE + l + 1, :]).astype(BF16)
        a = a + jnp.dot(xa, w1_ref[l], preferred_element_type=F32)
        b = b + jnp.dot(xb, w1_ref[CMP_STRIDE + l], preferred_element_type=F32)
    pre = a + pltpu.roll(b, n - 1, axis=0)
    hid = pre * jax.nn.sigmoid(pre)
    out = jnp.dot(hid.astype(BF16), w2_ref[...], preferred_element_type=F32)
    row = lax.broadcasted_iota(jnp.int32, out.shape, 0)
    out = jnp.where(row < n - 1, out, 0.0)
    o_ref[...] = out.astype(BF16)
    ot_ref[...] = out.T.astype(BF16)


def _compress(c_in, pe, w1, w2, batch, seq_len):
    n_slabs = c_in.shape[1] // HEAD_DIM
    n_chunks = seq_len // CMP_STRIDE
    return pl.pallas_call(
        _compress_kernel,
        grid=(batch, n_slabs),
        in_specs=[
            pl.BlockSpec((seq_len, HEAD_DIM), lambda i, j: (i, j)),
            pl.BlockSpec((CMP_LEN, HEAD_DIM), lambda i, j: (0, 0)),
            pl.BlockSpec((None, CMP_LEN, HEAD_DIM, CMP_HIDDEN), lambda i, j: (j // N_KV, 0, 0, 0)),
            pl.BlockSpec((None, CMP_HIDDEN, HEAD_DIM), lambda i, j: (j // N_KV, 0, 0)),
        ],
        out_specs=[
            pl.BlockSpec((None, None, n_chunks, HEAD_DIM), lambda i, j: (i, j, 0, 0)),
            pl.BlockSpec((None, None, HEAD_DIM, n_chunks), lambda i, j: (i, j, 0, 0)),
        ],
        out_shape=[
            jax.ShapeDtypeStruct((batch, n_slabs, n_chunks, HEAD_DIM), BF16),
            jax.ShapeDtypeStruct((batch, n_slabs, HEAD_DIM, n_chunks), BF16),
        ],
        compiler_params=_params(2),
        name="compress",
    )(c_in, pe, w1, w2)


TAB_DIAG, TAB_SUB, TAB_FAR, TAB_MASK = 0, 1, 2, 3
BAND_BELOW = 8
ONES_ROWS = 16


N_NSA_INPUTS = 11
BIAS_LEN = 1024
FAR_UNROLL = 4


def _nsa_kernel(*refs, tq, n_sel, n_cast):
    (q_ref, kc_ref, vct_ref, ks_ref, vst_ref, kw_ref, vwt_ref, gate_ref, fvec_ref,
     ovl_ref, et_ref) = refs[:N_NSA_INPUTS]
    cast_in = refs[N_NSA_INPUTS:N_NSA_INPUTS + n_cast]
    o_ref = refs[N_NSA_INPUTS + n_cast]
    cast_out = refs[N_NSA_INPUTS + n_cast + 1:N_NSA_INPUTS + 2 * n_cast + 1]
    (kaug_ref, tab_ref, band_ref, sc_ref, s0_ref, s1_ref, m_ref, mt_ref,
     acc_ref) = refs[N_NSA_INPUTS + 2 * n_cast + 1:]
    i = pl.program_id(2)

    for src, dst in zip(cast_in, cast_out):
        dst[...] = src[...].astype(BF16)

    tk = tq
    mcols = HPG * tq
    nw = WINDOW // tk
    cpt = tq // CMP_STRIDE
    band_rows = BAND_BELOW + cpt

    @pl.when(i == 0)
    def _():
        kaug_ref[:, 0:HEAD_DIM] = ks_ref[...]
        kaug_ref[:, HEAD_DIM:2 * HEAD_DIM] = et_ref[...]

    @pl.when(i == 0)
    def _():
        c = lax.broadcasted_iota(jnp.int32, (tk, tq), 0)
        r = lax.broadcasted_iota(jnp.int32, (tk, tq), 1)
        far = jnp.where(r < c, 0.0, NEG)
        lane = lax.broadcasted_iota(jnp.int32, (1, BIAS_LEN), 1)
        for h in range(HPG):
            cols = slice(h * tq, (h + 1) * tq)
            f = fvec_ref[h]
            f_diag = jnp.where(lane < tq, f, NEG)
            x = pltpu.roll(jnp.broadcast_to(f_diag, (tk, BIAS_LEN)), 0, 1, stride=1, stride_axis=0)
            tab_ref[TAB_DIAG, :, cols] = x[:, 0:tq]
            x = pltpu.roll(jnp.broadcast_to(f, (tk, BIAS_LEN)), 0, 1, stride=1, stride_axis=0)
            tab_ref[TAB_SUB, :, cols] = x[:, tq:2 * tq]
            tab_ref[TAB_FAR, :, cols] = far
            tab_ref[TAB_MASK, :, cols] = jnp.full((tk, tq), NEG, F32)
            for v, first in enumerate((0, -BAND_BELOW)):
                shift = (CMP_STRIDE * first + CMP_LEN - 1) % BIAS_LEN
                f_shift = pltpu.roll(f, shift, 1)
                x = pltpu.roll(jnp.broadcast_to(f_shift, (band_rows, BIAS_LEN)), 0, 1,
                               stride=CMP_STRIDE, stride_axis=0)
                band_ref[v, :, cols] = x[:, 0:tq]

    q = q_ref[...]
    qs = jnp.concatenate([q[:, h * HEAD_DIM:(h + 1) * HEAD_DIM] for h in range(HPG)], axis=0)

    s_refs = (s0_ref, s1_ref)

    def init():
        m_ref[...] = jnp.full((1, mcols), NEG, F32)
        acc_ref[...] = jnp.zeros(acc_ref.shape, F32)

    def keys(kt):
        return pl.ds(pl.multiple_of(kt * tk, tk), tk)

    def qk(qmat, k_ref, kt, kind, buf):
        s = lax.dot_general(k_ref[keys(kt), :], qmat, _DN_T, preferred_element_type=F32)
        if kind is not None:
            s = s + tab_ref[kind]
        s_refs[buf][...] = s
        return jnp.max(s, axis=0, keepdims=True)

    def process(vt_ref, kt, buf, m_tile):
        m_prev = m_ref[...]
        m_next = jnp.maximum(m_prev, m_tile)
        alpha = jnp.exp2(m_prev - m_next)
        p = jnp.exp2(s_refs[buf][...] - m_next).astype(BF16)
        acc_ref[...] = alpha * acc_ref[...] + jnp.dot(vt_ref[:, keys(kt)], p, preferred_element_type=F32)
        m_ref[...] = m_next

    def finish():
        return acc_ref[0:HEAD_DIM, :] * (1.0 / acc_ref[HEAD_DIM:HEAD_DIM + 1, :])

    assert nw == 2, "window tiles are i, i-1 (previous-tile table) and i-nw (window-edge table)"
    init()
    mt_w0 = qk(qs, kw_ref, i, TAB_DIAG, 0)

    raw = lax.dot_general(kc_ref[...], qs, _DN_T, preferred_element_type=F32)
    crow = lax.broadcasted_iota(jnp.int32, raw.shape, 0)
    sc_ref[...] = jnp.where(crow < cpt * (i + 1), raw, NEG)
    band = pl.ds(pl.multiple_of(jnp.maximum(cpt * i - BAND_BELOW, 0), 8), band_rows)
    sc_ref[band, :] += band_ref[jnp.minimum(i, 1)]

    mt_w1 = qk(qs, kw_ref, jnp.maximum(i - 1, 0), jnp.where(i >= 1, TAB_SUB, TAB_MASK), 1)
    process(vwt_ref, i, 0, mt_w0)

    sc = sc_ref[...]
    mc = jnp.maximum(jnp.max(sc, axis=0, keepdims=True), HALF_NEG)
    pc = jnp.exp2(sc - mc)
    lc = jnp.sum(pc, axis=0, keepdims=True)
    pc = pc * jnp.where(lc > 0.0, 1.0 / lc, 0.0)
    o_c = jnp.dot(vct_ref[...], pc.astype(BF16), preferred_element_type=F32)

    mt_w2 = qk(qs, kw_ref, jnp.maximum(i - nw, 0), jnp.where(i >= nw, TAB_FAR, TAB_MASK), 0)
    process(vwt_ref, jnp.maximum(i - 1, 0), 1, mt_w1)

    ps = pc[:, 0:tq] + pc[:, tq:2 * tq] + pc[:, 2 * tq:3 * tq] + pc[:, 3 * tq:4 * tq]
    hi = ps.astype(BF16)
    r1 = ps - hi.astype(F32)
    mid = r1.astype(BF16)
    lo = (r1 - mid.astype(F32)).astype(BF16)
    ovl = ovl_ref[...]
    imp = (jnp.dot(ovl, hi, preferred_element_type=F32) + jnp.dot(ovl, mid, preferred_element_type=F32)
           + jnp.dot(ovl, lo, preferred_element_type=F32))
    jj = lax.broadcasted_iota(jnp.int32, (n_sel, tq), 0)
    tt = i * tq + lax.broadcasted_iota(jnp.int32, (n_sel, tq), 1)
    cur = tt >> int(math.log2(SEL_LEN))
    forced = (jj == 0) | (jj == cur) | (jj == cur - 1)
    imp = jnp.where(forced, FORCE, imp)
    imp = jnp.where(jj * SEL_LEN <= tt, imp, NEG)
    sub = 8
    ranks = []
    for j0 in range(0, n_sel, sub):
        blk = imp[j0:j0 + sub, :]
        jl = j0 + lax.broadcasted_iota(jnp.int32, blk.shape, 0)
        cnt = jnp.zeros(blk.shape, jnp.int32)
        for b in range(n_sel):
            row = imp[b:b + 1, :]
            if b < j0:
                cnt = cnt + jnp.where(row >= blk, 1, 0)
            elif b >= j0 + sub:
                cnt = cnt + jnp.where(row > blk, 1, 0)
            else:
                cnt = cnt + jnp.where(row > blk, 1, jnp.where(row == blk, jnp.where(jl > b, 1, 0), 0))
        ranks.append(cnt)
    rank = jnp.concatenate(ranks, axis=0)
    selb_t = jnp.where(rank < SEL_TOPK, 0.0, NEG)

    process(vwt_ref, jnp.maximum(i - nw, 0), 0, mt_w2)
    o_w = finish()

    selb = jnp.concatenate([selb_t, jnp.zeros((LANE - n_sel, tq), F32)], axis=0).T
    selb = selb.astype(BF16)
    qa = jnp.concatenate([qs, jnp.concatenate([selb] * HPG, axis=0)], axis=1)

    init()

    @pl.when(i == 0)
    def _():
        process(vst_ref, 0, 0, qk(qa, kaug_ref, 0, TAB_DIAG, 0))

    @pl.when(i >= 1)
    def _():
        mt0 = qk(qa, kaug_ref, i, TAB_DIAG, 0)
        mt1 = qk(qa, kaug_ref, i - 1, TAB_SUB, 1)
        process(vst_ref, i, 0, mt0)
        mt0 = qk(qa, kaug_ref, jnp.maximum(i - 2, 0), None, 0)
        process(vst_ref, i - 1, 1, mt1)
        n_far = i - 1
        rem = n_far % FAR_UNROLL

        def far_tiles(a, n, mt0):
            for t in range(n):
                nxt = qk(qa, kaug_ref, jnp.maximum(a - t - 1, 0), None, (t + 1) % 2)
                process(vst_ref, a - t, t % 2, mt0)
                mt0 = nxt
            return mt0

        mt0 = lax.fori_loop(0, n_far // FAR_UNROLL,
                            lambda p, mt: far_tiles(i - 2 - FAR_UNROLL * p, FAR_UNROLL, mt), mt0)

        @pl.when(rem >= 2)
        def _():
            mt_ref[...] = far_tiles(rem - 1, 2, mt0)

        @pl.when(rem == 1)
        def _():
            process(vst_ref, 0, 0, mt0)

        @pl.when(rem == 3)
        def _():
            process(vst_ref, 0, 0, mt_ref[...])

    o_s = finish()

    gt = gate_ref[...].T
    for h in range(HPG):
        sl = slice(h * tq, (h + 1) * tq)
        o = gt[3 * h:3 * h + 1, :] * o_c[:, sl]
        o = o + gt[3 * h + 1:3 * h + 2, :] * o_s[:, sl]
        o = o + gt[3 * h + 2:3 * h + 3, :] * o_w[:, sl]
        o_ref[:, h * HEAD_DIM:(h + 1) * HEAD_DIM] = o.T.astype(BF16)


def _nsa(q, k_all, vt_all, cmp, cmp_t, gates, fvec, ovl, e_t, cast_weights, batch, seq_len, *, tq):
    n_t = seq_len // tq
    n_steps = batch * N_KV * n_t
    step_row = lambda b, g, i: ((b * N_KV + g) * n_t + i, 0)
    cast_specs = [pl.BlockSpec((w.shape[0] // n_steps, w.shape[1]), step_row) for w in cast_weights]
    n_sel = seq_len // SEL_LEN
    n_chunks = cmp.shape[2]
    mcols = HPG * tq
    band_rows = BAND_BELOW + tq // CMP_STRIDE
    qw = HPG * HEAD_DIM

    def vt_spec(first):
        return pl.BlockSpec((None, None, HEAD_DIM + ONES_ROWS, seq_len), lambda b, g, i: (b, first + g, 0, 0))

    kern = functools.partial(_nsa_kernel, tq=tq, n_sel=n_sel, n_cast=len(cast_weights))
    o_attn, *cast = pl.pallas_call(
        kern,
        grid=(batch, N_KV, n_t),
        in_specs=[
            pl.BlockSpec((tq, qw), lambda b, g, i: (b * n_t + i, g)),
            pl.BlockSpec((None, None, n_chunks, HEAD_DIM), lambda b, g, i: (b, g, 0, 0)),
            pl.BlockSpec((None, None, HEAD_DIM, n_chunks), lambda b, g, i: (b, N_KV + g, 0, 0)),
            pl.BlockSpec((seq_len, HEAD_DIM), lambda b, g, i: (b, g)),
            vt_spec(0),
            pl.BlockSpec((seq_len, HEAD_DIM), lambda b, g, i: (b, N_KV + g)),
            vt_spec(N_KV),
            pl.BlockSpec((tq, LANE), lambda b, g, i: (b * n_t + i, g)),
            pl.BlockSpec((None, HPG, 1, BIAS_LEN), lambda b, g, i: (g, 0, 0, 0)),
            pl.BlockSpec((n_sel, n_chunks), lambda b, g, i: (0, 0)),
            pl.BlockSpec((seq_len, LANE), lambda b, g, i: (0, 0)),
        ] + cast_specs,
        out_specs=[pl.BlockSpec((tq, qw), lambda b, g, i: (b * n_t + i, g))] + cast_specs,
        out_shape=[jax.ShapeDtypeStruct((batch * seq_len, ATTN_WIDTH), BF16)]
        + [jax.ShapeDtypeStruct(w.shape, BF16) for w in cast_weights],
        scratch_shapes=[
            pltpu.VMEM((seq_len, 2 * HEAD_DIM), BF16),
            pltpu.VMEM((4, tq, mcols), F32),
            pltpu.VMEM((2, band_rows, mcols), F32),
            pltpu.VMEM((n_chunks, mcols), F32),
            pltpu.VMEM((tq, mcols), F32),
            pltpu.VMEM((tq, mcols), F32),
            pltpu.VMEM((1, mcols), F32),
            pltpu.VMEM((1, mcols), F32),
            pltpu.VMEM((HEAD_DIM + ONES_ROWS, mcols), F32),
        ],
        compiler_params=_params(3),
        name="nsa_attention",
    )(q, cmp, cmp_t, k_all, vt_all, k_all, vt_all, gates, fvec, ovl, e_t, *cast_weights)
    return o_attn, cast


def _oproj_kernel(oa_ref, ov_ref, wo_ref, x_ref, g1_ref, g2_ref, x1_ref, h2_ref):
    ka = oa_ref.shape[1]
    tm = oa_ref.shape[0]
    for rows in (slice(k * tm // 4, (k + 1) * tm // 4) for k in range(4)):
        mix = jnp.dot(oa_ref[rows, :], wo_ref[0:ka, :], preferred_element_type=F32)
        mix = mix + jnp.dot(ov_ref[rows, :], wo_ref[ka:, :], preferred_element_type=F32)
        x1 = x_ref[rows, :] + _rms(mix, g1_ref[...])
        x1_ref[rows, :] = x1
        h2_ref[rows, :] = _rms(x1, g2_ref[...]).astype(BF16)


def _oproj(o_attn, o_conv, w_o, x2, g_post, g_pre, *, tm=512):
    m = x2.shape[0]
    ka, kv = o_attn.shape[1], o_conv.shape[1]
    row = lambda i: (i, 0)
    fixed = lambda i: (0, 0)
    return pl.pallas_call(
        _oproj_kernel,
        grid=(m // tm,),
        in_specs=[
            pl.BlockSpec((tm, ka), row),
            pl.BlockSpec((tm, kv), row),
            pl.BlockSpec((ka + kv, D_MODEL), fixed),
            pl.BlockSpec((tm, D_MODEL), row),
            pl.BlockSpec((1, D_MODEL), fixed),
            pl.BlockSpec((1, D_MODEL), fixed),
        ],
        out_specs=[pl.BlockSpec((tm, D_MODEL), row), pl.BlockSpec((tm, D_MODEL), row)],
        out_shape=[jax.ShapeDtypeStruct((m, D_MODEL), F32), jax.ShapeDtypeStruct((m, D_MODEL), BF16)],
        compiler_params=_params(1),
        name="out_proj",
    )(o_attn, o_conv, w_o, x2, g_post, g_pre)


def _ffn_kernel(h_ref, wu_ref, wd_ref, x1_ref, g_ref, o_ref, acc_ref):
    j = pl.program_id(1)

    @pl.when(j == 0)
    def _():
        acc_ref[...] = jnp.zeros(acc_ref.shape, F32)

    a = jnp.dot(h_ref[...], wu_ref[...], preferred_element_type=F32)
    a = jnp.square(jnp.maximum(a, 0.0)).astype(BF16)
    acc_ref[...] += jnp.dot(a, wd_ref[...], preferred_element_type=F32)

    @pl.when(j == pl.num_programs(1) - 1)
    def _():
        o_ref[...] = x1_ref[...] + _rms(acc_ref[...], g_ref[...])


def _ffn(h2, w_up, w_down, x1, g_post, *, tm=512, tf=1024):
    m = h2.shape[0]
    d_ff = w_up.shape[1]
    return pl.pallas_call(
        _ffn_kernel,
        grid=(m // tm, d_ff // tf),
        in_specs=[
            pl.BlockSpec((tm, D_MODEL), lambda i, j: (i, 0)),
            pl.BlockSpec((D_MODEL, tf), lambda i, j: (0, j)),
            pl.BlockSpec((tf, D_MODEL), lambda i, j: (j, 0)),
            pl.BlockSpec((tm, D_MODEL), lambda i, j: (i, 0)),
            pl.BlockSpec((1, D_MODEL), lambda i, j: (0, 0)),
        ],
        out_specs=pl.BlockSpec((tm, D_MODEL), lambda i, j: (i, 0)),
        out_shape=jax.ShapeDtypeStruct((m, D_MODEL), F32),
        scratch_shapes=[pltpu.VMEM((tm, D_MODEL), F32)],
        compiler_params=_params(2),
        name="ffn",
    )(h2, w_up, w_down, x1, g_post)


def _bucket_np(dist):
    n = np.maximum(dist, 0)
    max_exact = N_BUCKETS // 2
    nf = np.maximum(n, 1).astype(np.float32)
    large = max_exact + (np.log(nf / np.float32(max_exact)) / np.float32(math.log(MAX_DIST / max_exact))
                         * np.float32(N_BUCKETS - max_exact)).astype(np.int32)
    large = np.minimum(large, N_BUCKETS - 1)
    return np.where(n < max_exact, n, large)


def _bucket_starts():
    b = _bucket_np(np.arange(4 * MAX_DIST))
    return [int(np.argmax(b == k)) for k in range(N_BUCKETS)]


def _attention_tables(rel_bias, seq_len, tq):
    starts = _bucket_starts()
    assert starts[N_BUCKETS - 1] <= CMP_STRIDE * (BAND_BELOW + 1) - (CMP_LEN - 1)
    assert starts[N_BUCKETS - 1] <= tq and 2 * tq <= BIAS_LEN // 2
    rel = (rel_bias - rel_bias[:, N_BUCKETS - 1:]) * math.log2(math.e)
    d = jnp.arange(BIAS_LEN, dtype=jnp.int32)[None, :]
    fvec = jnp.broadcast_to(rel[:, 0:1], (N_HEADS, BIAS_LEN))
    for k in range(1, N_BUCKETS):
        fvec = jnp.where(d >= starts[k], rel[:, k:k + 1], fvec)
    fvec = jnp.where(d < BIAS_LEN // 2, fvec, NEG).astype(F32).reshape(N_KV, HPG, 1, BIAS_LEN)

    n_chunks = seq_len // CMP_STRIDE
    n_cmp = (seq_len - CMP_LEN) // CMP_STRIDE + 1
    n_sel = seq_len // SEL_LEN
    ci = np.arange(n_chunks)[None, :] * CMP_STRIDE
    sj = np.arange(n_sel)[:, None] * SEL_LEN
    ovl = ((ci < sj + SEL_LEN) & (ci + CMP_LEN > sj) & (np.arange(n_chunks)[None, :] < n_cmp))
    e_t = (np.arange(seq_len)[:, None] // SEL_LEN == np.arange(LANE)[None, :])
    return fvec, jnp.asarray(ovl, BF16), jnp.asarray(e_t, BF16)


def kernel(x, w_in, pe_cmp, w_cmp_k1, w_cmp_k2, w_cmp_v1, w_cmp_v2, conv_w, rel_bias, w_o, w_up, w_down,
           g_pre_mix, g_post_mix, g_pre_ffn, g_post_ffn):
    batch, seq_len, _ = x.shape
    depth = w_in.shape[0]
    tq = 256
    fvec, ovl, e_t = _attention_tables(rel_bias, seq_len, tq)
    x2 = x.reshape(batch * seq_len, D_MODEL)
    for l in range(depth):
        wl = jnp.swapaxes(w_in[l], 0, 1)
        g1 = g_pre_mix[l].reshape(1, D_MODEL)

        q, k_all, c_in, vt_all, gates, o_conv = _in_proj(x2, g1, wl, conv_w[l], batch, seq_len)

        w1 = jnp.stack([w_cmp_k1[l], w_cmp_v1[l]]).astype(BF16)
        w2 = jnp.stack([w_cmp_k2[l], w_cmp_v2[l]]).astype(BF16)
        cmp, cmp_t = _compress(c_in, pe_cmp[l], w1, w2, batch, seq_len)

        o_attn, (wo_b, wup_b, wdown_b) = _nsa(q, k_all, vt_all, cmp, cmp_t, gates, fvec, ovl, e_t,
                                              [w_o[l], w_up[l], w_down[l]], batch, seq_len, tq=tq)

        x1, h2 = _oproj(o_attn, o_conv, wo_b, x2,
                        g_post_mix[l].reshape(1, D_MODEL), g_pre_ffn[l].reshape(1, D_MODEL))
        x2 = _ffn(h2, wup_b, wdown_b, x1, g_post_ffn[l].reshape(1, D_MODEL))
    return x2.reshape(batch, seq_len, D_MODEL)
```

```python
import functools
import math

import numpy as np
import jax
import jax.numpy as jnp
from jax import lax
from jax.experimental import pallas as pl
from jax.experimental.pallas import tpu as pltpu

F32 = jnp.float32
BF16 = jnp.bfloat16

D_MODEL = 2048
N_HEADS = 8
N_KV = 2
HPG = N_HEADS // N_KV
HEAD_DIM = 128
ATTN_WIDTH = N_HEADS * HEAD_DIM
KV_WIDTH = N_KV * HEAD_DIM
CONV_WIDTH = D_MODEL - ATTN_WIDTH
CONV_K = 3
N_BRANCH = 3
CMP_LEN = 32
CMP_STRIDE = 16
CMP_HIDDEN = 256
SEL_LEN = 64
SEL_TOPK = 16
WINDOW = 512
N_BUCKETS = 32
MAX_DIST = 128
EPS = 1e-6
NEG = -1e30
HALF_NEG = -5e29
FORCE = 1e9

QKV_WIDTH = ATTN_WIDTH + 6 * KV_WIDTH
GATE_OFF = QKV_WIDTH
CONV_OFF = QKV_WIDTH + N_HEADS * N_BRANCH
LANE = 128
VMEM_LIMIT = 56 * 1024 * 1024

_DN_T = (((1,), (1,)), ((), ()))


def _rms(x, g):
    ms = jnp.mean(x * x, axis=-1, keepdims=True)
    return x * lax.rsqrt(ms + EPS) * g


def _params(n_axes):
    return pltpu.CompilerParams(dimension_semantics=("arbitrary",) * n_axes, vmem_limit_bytes=VMEM_LIMIT)


QKV_BLOCK = 2 * KV_WIDTH
SLAB_KC, SLAB_VC, SLAB_KS, SLAB_VS, SLAB_KW, SLAB_VW = range(6)
CONV_BLOCK = 512
W_CHUNK = 256
SUBLANE = 8


def _load_weights(wt_hbm, w_ref, wg_ref, wconv_ref, stage_ref, sem):
    n_gate = N_HEADS * N_BRANCH
    per_group = HPG * N_BRANCH

    def store_rows(dst, row):
        def store(v):
            dst[row:row + v.shape[0], :] = v.astype(BF16)
        return store

    def store_gates(v):
        wg_ref[...] = jnp.zeros(wg_ref.shape, BF16)
        pad = jnp.zeros((2 * SUBLANE - per_group, v.shape[1]), F32)
        for g in range(N_KV):
            rows = jnp.concatenate([v[g * per_group:(g + 1) * per_group, :], pad], axis=0)
            wg_ref[g * LANE:g * LANE + 2 * SUBLANE, :] = rows.astype(BF16)

    chunks = [(r, W_CHUNK, store_rows(w_ref, r)) for r in range(0, QKV_WIDTH, W_CHUNK)]
    chunks.append((GATE_OFF, n_gate, store_gates))
    chunks += [(CONV_OFF + r, W_CHUNK, store_rows(wconv_ref, r)) for r in range(0, 3 * CONV_WIDTH, W_CHUNK)]

    def copy(k):
        src, n, _ = chunks[k]
        slot = k % 2
        return pltpu.make_async_copy(wt_hbm.at[pl.ds(src, n), :], stage_ref.at[slot, pl.ds(0, n), :], sem.at[slot])

    copy(0).start()
    for k, (_, n, store) in enumerate(chunks):
        if k + 1 < len(chunks):
            copy(k + 1).start()
        copy(k).wait()
        store(stage_ref[k % 2, 0:n, :])


def _inproj_kernel(x_ref, g_ref, wt_hbm, cw_ref, q_ref, k_ref, c_ref, vt_ref, gate_ref, ov_ref,
                   h_ref, carry_ref, w_ref, wg_ref, wconv_ref, stage_ref, sem, *, q_scale, tiles_per_seq):
    i = pl.program_id(0)
    tm = x_ref.shape[0]

    @pl.when(i == 0)
    def _():
        _load_weights(wt_hbm, w_ref, wg_ref, wconv_ref, stage_ref, sem)

    h_ref[...] = _rms(x_ref[...], g_ref[...]).astype(BF16)
    gate_ref[...] = jax.nn.sigmoid(lax.dot_general(h_ref[...], wg_ref[...], _DN_T, preferred_element_type=F32))

    def proj(w, j, width):
        return lax.dot_general(h_ref[...], w[j * width:(j + 1) * width, :], _DN_T, preferred_element_type=F32)

    for j in range(ATTN_WIDTH // QKV_BLOCK):
        q_ref[:, j * QKV_BLOCK:(j + 1) * QKV_BLOCK] = (proj(w_ref, j, QKV_BLOCK) * q_scale).astype(BF16)

    def slab(which):
        return proj(w_ref, ATTN_WIDTH // KV_WIDTH + which, KV_WIDTH)

    ones = jnp.where(lax.broadcasted_iota(jnp.int32, (ONES_ROWS, tm), 0) == 0, 1.0, 0.0).astype(BF16)
    for pair, (k_slab, c_slab, v_slab) in enumerate(((SLAB_KS, SLAB_KC, SLAB_VS), (SLAB_KW, SLAB_VC, SLAB_VW))):
        cols = slice(pair * KV_WIDTH, (pair + 1) * KV_WIDTH)
        k_ref[:, cols] = slab(k_slab).astype(BF16)
        c_ref[:, cols] = slab(c_slab)
        v = slab(v_slab)
        for g in range(N_KV):
            s = pair * N_KV + g
            vt_ref[s, 0:HEAD_DIM, :] = v[:, g * HEAD_DIM:(g + 1) * HEAD_DIM].T.astype(BF16)
            vt_ref[s, HEAD_DIM:HEAD_DIM + ONES_ROWS, :] = ones

    def conv_proj(which, j):
        return proj(wconv_ref, which * (CONV_WIDTH // CONV_BLOCK) + j, CONV_BLOCK)

    row = lax.broadcasted_iota(jnp.int32, (tm, CONV_BLOCK), 0)
    for j in range(CONV_WIDTH // CONV_BLOCK):
        cols = slice(j * CONV_BLOCK, (j + 1) * CONV_BLOCK)
        u = conv_proj(2, j) * conv_proj(0, j)
        prev = carry_ref[j]
        prev = jnp.where(i % tiles_per_seq == 0, 0.0, prev)
        carry_ref[j] = u[tm - 8:tm, :]
        u1 = jnp.where(row == 0, prev[7:8, :], pltpu.roll(u, 1, axis=0))
        u2 = jnp.where(row == 0, prev[6:7, :], jnp.where(row == 1, prev[7:8, :], pltpu.roll(u, 2, axis=0)))
        w = cw_ref[:, cols]
        y = w[0:1, :] * u2
        y = y + w[1:2, :] * u1
        y = y + w[2:3, :] * u
        ov_ref[:, cols] = (conv_proj(1, j) * y).astype(BF16)


def _in_proj(x2, g, w_t, conv_w, batch, seq_len, *, tm=512):
    m = x2.shape[0]
    gw = N_KV * LANE
    tps = seq_len // tm
    n_slabs = QKV_BLOCK // HEAD_DIM
    kern = functools.partial(_inproj_kernel, q_scale=HEAD_DIM ** -0.5 * math.log2(math.e), tiles_per_seq=tps)
    row = lambda i: (i, 0)

    def resident(shape):
        return pl.BlockSpec(shape, lambda i: (0, 0), pipeline_mode=pl.Buffered(1))

    return pl.pallas_call(
        kern,
        grid=(m // tm,),
        in_specs=[
            pl.BlockSpec((tm, D_MODEL), row),
            resident((1, D_MODEL)),
            pl.BlockSpec(memory_space=pl.ANY),
            resident(conv_w.shape),
        ],
        out_specs=[
            pl.BlockSpec((tm, ATTN_WIDTH), row),
            pl.BlockSpec((tm, QKV_BLOCK), row),
            pl.BlockSpec((tm, QKV_BLOCK), row),
            pl.BlockSpec((None, n_slabs, HEAD_DIM + ONES_ROWS, tm), lambda i: (i // tps, 0, 0, i % tps)),
            pl.BlockSpec((tm, gw), row),
            pl.BlockSpec((tm, CONV_WIDTH), row),
        ],
        out_shape=[
            jax.ShapeDtypeStruct((m, ATTN_WIDTH), BF16),
            jax.ShapeDtypeStruct((m, QKV_BLOCK), BF16),
            jax.ShapeDtypeStruct((m, QKV_BLOCK), F32),
            jax.ShapeDtypeStruct((batch, n_slabs, HEAD_DIM + ONES_ROWS, seq_len), BF16),
            jax.ShapeDtypeStruct((m, gw), F32),
            jax.ShapeDtypeStruct((m, CONV_WIDTH), BF16),
        ],
        scratch_shapes=[pltpu.VMEM((tm, D_MODEL), BF16),
                        pltpu.VMEM((CONV_WIDTH // CONV_BLOCK, 8, CONV_BLOCK), F32),
                        pltpu.VMEM((QKV_WIDTH, D_MODEL), BF16),
                        pltpu.VMEM((gw, D_MODEL), BF16),
                        pltpu.VMEM((3 * CONV_WIDTH, D_MODEL), BF16),
                        pltpu.VMEM((2, W_CHUNK, D_MODEL), F32),
                        pltpu.SemaphoreType.DMA((2,))],
        compiler_params=_params(1),
        name="in_proj",
    )(x2, g, w_t, conv_w)


def _compress_kernel(x_ref, pe_ref, w1_ref, w2_ref, o_ref, ot_ref):
    n = x_ref.shape[0] // CMP_STRIDE
    a = jnp.zeros((n, CMP_HIDDEN), F32)
    b = jnp.zeros((n, CMP_HIDDEN), F32)
    for l in range(CMP_STRIDE):
        xl = x_ref[pl.ds(l, n, stride=CMP_STRIDE), :]
        xa = (xl + pe_ref[l:l + 1, :]).astype(BF16)
        xb = (xl + pe_ref[CMP_STRIDE + l:CMP_STRIDE + l + 1, :]).astype(BF16)
        a = a + jnp.dot(xa, w1_ref[l], preferred_element_type=F32)
        b = b + jnp.dot(xb, w1_ref[CMP_STRIDE + l], preferred_element_type=F32)
    pre = a + pltpu.roll(b, n - 1, axis=0)
    hid = pre * jax.nn.sigmoid(pre)
    out = jnp.dot(hid.astype(BF16), w2_ref[...], preferred_element_type=F32)
    row = lax.broadcasted_iota(jnp.int32, out.shape, 0)
    out = jnp.where(row < n - 1, out, 0.0)
    o_ref[...] = out.astype(BF16)
    ot_ref[...] = out.T.astype(BF16)


def _compress(c_in, pe, w1, w2, batch, seq_len):
    n_slabs = c_in.shape[1] // HEAD_DIM
    n_chunks = seq_len // CMP_STRIDE
    return pl.pallas_call(
        _compress_kernel,
        grid=(batch, n_slabs),
        in_specs=[
            pl.BlockSpec((seq_len, HEAD_DIM), lambda i, j: (i, j)),
            pl.BlockSpec((CMP_LEN, HEAD_DIM), lambda i, j: (0, 0)),
            pl.BlockSpec((None, CMP_LEN, HEAD_DIM, CMP_HIDDEN), lambda i, j: (j // N_KV, 0, 0, 0)),
            pl.BlockSpec((None, CMP_HIDDEN, HEAD_DIM), lambda i, j: (j // N_KV, 0, 0)),
        ],
        out_specs=[
            pl.BlockSpec((None, None, n_chunks, HEAD_DIM), lambda i, j: (i, j, 0, 0)),
            pl.BlockSpec((None, None, HEAD_DIM, n_chunks), lambda i, j: (i, j, 0, 0)),
        ],
        out_shape=[
            jax.ShapeDtypeStruct((batch, n_slabs, n_chunks, HEAD_DIM), BF16),
            jax.ShapeDtypeStruct((batch, n_slabs, HEAD_DIM, n_chunks), BF16),
        ],
        compiler_params=_params(2),
        name="compress",
    )(c_in, pe, w1, w2)


TAB_DIAG, TAB_SUB, TAB_FAR, TAB_MASK = 0, 1, 2, 3
BAND_BELOW = 8
ONES_ROWS = 16


N_NSA_INPUTS = 11
BIAS_LEN = 1024
FAR_UNROLL = 4
NSA_TILES_PER_STEP = 2


def _nsa_kernel(*refs, tiles_per_step, **static):
    for u in range(tiles_per_step):
        _nsa_tile(refs, u, tiles_per_step, **static)


def _nsa_tile(refs, u, tiles_per_step, *, tq, n_sel, n_cast):
    (q_ref, kc_ref, vct_ref, ks_ref, vst_ref, kw_ref, vwt_ref, gate_ref, fvec_ref,
     ovl_ref, et_ref) = refs[:N_NSA_INPUTS]
    cast_in = refs[N_NSA_INPUTS:N_NSA_INPUTS + n_cast]
    o_ref = refs[N_NSA_INPUTS + n_cast]
    cast_out = refs[N_NSA_INPUTS + n_cast + 1:N_NSA_INPUTS + 2 * n_cast + 1]
    (kaug_ref, tab_ref, band_ref, sc_ref, s0_ref, s1_ref, m_ref, mt_ref,
     acc_ref) = refs[N_NSA_INPUTS + 2 * n_cast + 1:]
    i = pl.program_id(2) * tiles_per_step + u
    rows = slice(u * tq, (u + 1) * tq)

    tk = tq
    mcols = HPG * tq
    nw = WINDOW // tk
    cpt = tq // CMP_STRIDE
    band_rows = BAND_BELOW + cpt

    def group_start(fn):
        if u == 0:
            pl.when(i == 0)(fn)

    @group_start
    def _():
        kaug_ref[:, 0:HEAD_DIM] = ks_ref[...]
        kaug_ref[:, HEAD_DIM:2 * HEAD_DIM] = et_ref[...]

    @group_start
    def _():
        c = lax.broadcasted_iota(jnp.int32, (tk, tq), 0)
        r = lax.broadcasted_iota(jnp.int32, (tk, tq), 1)
        far = jnp.where(r < c, 0.0, NEG)
        lane = lax.broadcasted_iota(jnp.int32, (1, BIAS_LEN), 1)
        for h in range(HPG):
            cols = slice(h * tq, (h + 1) * tq)
            f = fvec_ref[h]
            f_diag = jnp.where(lane < tq, f, NEG)
            x = pltpu.roll(jnp.broadcast_to(f_diag, (tk, BIAS_LEN)), 0, 1, stride=1, stride_axis=0)
            tab_ref[TAB_DIAG, :, cols] = x[:, 0:tq]
            x = pltpu.roll(jnp.broadcast_to(f, (tk, BIAS_LEN)), 0, 1, stride=1, stride_axis=0)
            tab_ref[TAB_SUB, :, cols] = x[:, tq:2 * tq]
            tab_ref[TAB_FAR, :, cols] = far
            tab_ref[TAB_MASK, :, cols] = jnp.full((tk, tq), NEG, F32)
            for v, first in enumerate((0, -BAND_BELOW)):
                shift = (CMP_STRIDE * first + CMP_LEN - 1) % BIAS_LEN
                f_shift = pltpu.roll(f, shift, 1)
                x = pltpu.roll(jnp.broadcast_to(f_shift, (band_rows, BIAS_LEN)), 0, 1,
                               stride=CMP_STRIDE, stride_axis=0)
                band_ref[v, :, cols] = x[:, 0:tq]

    q = q_ref[rows, :]
    qs = jnp.concatenate([q[:, h * HEAD_DIM:(h + 1) * HEAD_DIM] for h in range(HPG)], axis=0)

    s_refs = (s0_ref, s1_ref)

    def init():
        m_ref[...] = jnp.full((1, mcols), NEG, F32)
        acc_ref[...] = jnp.zeros(acc_ref.shape, F32)

    def keys(kt):
        return pl.ds(pl.multiple_of(kt * tk, tk), tk)

    def qk(qmat, k_ref, kt, kind, buf):
        s = lax.dot_general(k_ref[keys(kt), :], qmat, _DN_T, preferred_element_type=F32)
        if kind is not None:
            s = s + tab_ref[kind]
        s_refs[buf][...] = s
        return jnp.max(s, axis=0, keepdims=True)

    def process(vt_ref, kt, buf, m_tile):
        m_prev = m_ref[...]
        m_next = jnp.maximum(m_prev, m_tile)
        alpha = jnp.exp2(m_prev - m_next)
        p = jnp.exp2(s_refs[buf][...] - m_next).astype(BF16)
        acc_ref[...] = alpha * acc_ref[...] + jnp.dot(vt_ref[:, keys(kt)], p, preferred_element_type=F32)
        m_ref[...] = m_next

    def finish():
        return acc_ref[0:HEAD_DIM, :] * (1.0 / acc_ref[HEAD_DIM:HEAD_DIM + 1, :])

    assert nw == 2, "window tiles are i, i-1 (previous-tile table) and i-nw (window-edge table)"
    init()
    mt_w0 = qk(qs, kw_ref, i, TAB_DIAG, 0)

    if u == 0:
        for src, dst in zip(cast_in, cast_out):
            dst[...] = src[...].astype(BF16)

    raw = lax.dot_general(kc_ref[...], qs, _DN_T, preferred_element_type=F32)
    crow = lax.broadcasted_iota(jnp.int32, raw.shape, 0)
    sc_ref[...] = jnp.where(crow < cpt * (i + 1), raw, NEG)
    band = pl.ds(pl.multiple_of(jnp.maximum(cpt * i - BAND_BELOW, 0), 8), band_rows)
    sc_ref[band, :] += band_ref[jnp.minimum(i, 1)]

    mt_w1 = qk(qs, kw_ref, jnp.maximum(i - 1, 0), jnp.where(i >= 1, TAB_SUB, TAB_MASK), 1)
    process(vwt_ref, i, 0, mt_w0)

    sc = sc_ref[...]
    mc = jnp.maximum(jnp.max(sc, axis=0, keepdims=True), HALF_NEG)
    pc = jnp.exp2(sc - mc)
    lc = jnp.sum(pc, axis=0, keepdims=True)
    pc = pc * jnp.where(lc > 0.0, 1.0 / lc, 0.0)
    o_c = jnp.dot(vct_ref[...], pc.astype(BF16), preferred_element_type=F32)

    mt_w2 = qk(qs, kw_ref, jnp.maximum(i - nw, 0), jnp.where(i >= nw, TAB_FAR, TAB_MASK), 0)
    process(vwt_ref, jnp.maximum(i - 1, 0), 1, mt_w1)

    ps = pc[:, 0:tq] + pc[:, tq:2 * tq] + pc[:, 2 * tq:3 * tq] + pc[:, 3 * tq:4 * tq]
    hi = ps.astype(BF16)
    r1 = ps - hi.astype(F32)
    mid = r1.astype(BF16)
    lo = (r1 - mid.astype(F32)).astype(BF16)
    ovl = ovl_ref[...]
    imp = (jnp.dot(ovl, hi, preferred_element_type=F32) + jnp.dot(ovl, mid, preferred_element_type=F32)
           + jnp.dot(ovl, lo, preferred_element_type=F32))
    jj = lax.broadcasted_iota(jnp.int32, (n_sel, tq), 0)
    tt = i * tq + lax.broadcasted_iota(jnp.int32, (n_sel, tq), 1)
    cur = tt >> int(math.log2(SEL_LEN))
    forced = (jj == 0) | (jj == cur) | (jj == cur - 1)
    imp = jnp.where(forced, FORCE, imp)
    imp = jnp.where(jj * SEL_LEN <= tt, imp, NEG)
    sub = 8
    ranks = []
    for j0 in range(0, n_sel, sub):
        blk = imp[j0:j0 + sub, :]
        jl = j0 + lax.broadcasted_iota(jnp.int32, blk.shape, 0)
        cnt = jnp.zeros(blk.shape, jnp.int32)
        for b in range(n_sel):
            row = imp[b:b + 1, :]
            if b < j0:
                cnt = cnt + jnp.where(row >= blk, 1, 0)
            elif b >= j0 + sub:
                cnt = cnt + jnp.where(row > blk, 1, 0)
            else:
                cnt = cnt + jnp.where(row > blk, 1, jnp.where(row == blk, jnp.where(jl > b, 1, 0), 0))
        ranks.append(cnt)
    rank = jnp.concatenate(ranks, axis=0)
    selb_t = jnp.where(rank < SEL_TOPK, 0.0, NEG)

    process(vwt_ref, jnp.maximum(i - nw, 0), 0, mt_w2)
    o_w = finish()

    selb = jnp.concatenate([selb_t, jnp.zeros((LANE - n_sel, tq), F32)], axis=0).T
    selb = selb.astype(BF16)
    qa = jnp.concatenate([qs, jnp.concatenate([selb] * HPG, axis=0)], axis=1)

    init()

    @pl.when(i == 0)
    def _():
        process(vst_ref, 0, 0, qk(qa, kaug_ref, 0, TAB_DIAG, 0))

    @pl.when(i >= 1)
    def _():
        mt0 = qk(qa, kaug_ref, i, TAB_DIAG, 0)
        mt1 = qk(qa, kaug_ref, i - 1, TAB_SUB, 1)
        process(vst_ref, i, 0, mt0)
        mt0 = qk(qa, kaug_ref, jnp.maximum(i - 2, 0), None, 0)
        process(vst_ref, i - 1, 1, mt1)
        n_far = i - 1
        rem = n_far % FAR_UNROLL

        def far_tiles(a, n, mt0):
            for t in range(n):
                nxt = qk(qa, kaug_ref, jnp.maximum(a - t - 1, 0), None, (t + 1) % 2)
                process(vst_ref, a - t, t % 2, mt0)
                mt0 = nxt
            return mt0

        mt0 = lax.fori_loop(0, n_far // FAR_UNROLL,
                            lambda p, mt: far_tiles(i - 2 - FAR_UNROLL * p, FAR_UNROLL, mt), mt0)

        @pl.when(rem >= 2)
        def _():
            mt_ref[...] = far_tiles(rem - 1, 2, mt0)

        @pl.when(rem == 1)
        def _():
            process(vst_ref, 0, 0, mt0)

        @pl.when(rem == 3)
        def _():
            process(vst_ref, 0, 0, mt_ref[...])

    o_s = finish()

    gt = gate_ref[rows, :].T
    for h in range(HPG):
        sl = slice(h * tq, (h + 1) * tq)
        o = gt[3 * h:3 * h + 1, :] * o_c[:, sl]
        o = o + gt[3 * h + 1:3 * h + 2, :] * o_s[:, sl]
        o = o + gt[3 * h + 2:3 * h + 3, :] * o_w[:, sl]
        o_ref[rows, h * HEAD_DIM:(h + 1) * HEAD_DIM] = o.T.astype(BF16)


def _nsa(q, k_all, vt_all, cmp, cmp_t, gates, fvec, ovl, e_t, cast_weights, batch, seq_len, *, tq):
    n_t = seq_len // (tq * NSA_TILES_PER_STEP)
    n_steps = batch * N_KV * n_t
    ts = tq * NSA_TILES_PER_STEP
    step_row = lambda b, g, i: ((b * N_KV + g) * n_t + i, 0)
    cast_specs = [pl.BlockSpec((w.shape[0] // n_steps, w.shape[1]), step_row) for w in cast_weights]
    n_sel = seq_len // SEL_LEN
    n_chunks = cmp.shape[2]
    mcols = HPG * tq
    band_rows = BAND_BELOW + tq // CMP_STRIDE
    qw = HPG * HEAD_DIM

    def vt_spec(first):
        return pl.BlockSpec((None, None, HEAD_DIM + ONES_ROWS, seq_len), lambda b, g, i: (b, first + g, 0, 0))

    kern = functools.partial(_nsa_kernel, tiles_per_step=NSA_TILES_PER_STEP, tq=tq, n_sel=n_sel,
                             n_cast=len(cast_weights))
    o_attn, *cast = pl.pallas_call(
        kern,
        grid=(batch, N_KV, n_t),
        in_specs=[
            pl.BlockSpec((ts, qw), lambda b, g, i: (b * n_t + i, g)),
            pl.BlockSpec((None, None, n_chunks, HEAD_DIM), lambda b, g, i: (b, g, 0, 0)),
            pl.BlockSpec((None, None, HEAD_DIM, n_chunks), lambda b, g, i: (b, N_KV + g, 0, 0)),
            pl.BlockSpec((seq_len, HEAD_DIM), lambda b, g, i: (b, g)),
            vt_spec(0),
            pl.BlockSpec((seq_len, HEAD_DIM), lambda b, g, i: (b, N_KV + g)),
            vt_spec(N_KV),
            pl.BlockSpec((ts, LANE), lambda b, g, i: (b * n_t + i, g)),
            pl.BlockSpec((None, HPG, 1, BIAS_LEN), lambda b, g, i: (g, 0, 0, 0)),
            pl.BlockSpec((n_sel, n_chunks), lambda b, g, i: (0, 0)),
            pl.BlockSpec((seq_len, LANE), lambda b, g, i: (0, 0)),
        ] + cast_specs,
        out_specs=[pl.BlockSpec((ts, qw), lambda b, g, i: (b * n_t + i, g))] + cast_specs,
        out_shape=[jax.ShapeDtypeStruct((batch * seq_len, ATTN_WIDTH), BF16)]
        + [jax.ShapeDtypeStruct(w.shape, BF16) for w in cast_weights],
        scratch_shapes=[
            pltpu.VMEM((seq_len, 2 * HEAD_DIM), BF16),
            pltpu.VMEM((4, tq, mcols), F32),
            pltpu.VMEM((2, band_rows, mcols), F32),
            pltpu.VMEM((n_chunks, mcols), F32),
            pltpu.VMEM((tq, mcols), F32),
            pltpu.VMEM((tq, mcols), F32),
            pltpu.VMEM((1, mcols), F32),
            pltpu.VMEM((1, mcols), F32),
            pltpu.VMEM((HEAD_DIM + ONES_ROWS, mcols), F32),
        ],
        compiler_params=_params(3),
        name="nsa_attention",
    )(q, cmp, cmp_t, k_all, vt_all, k_all, vt_all, gates, fvec, ovl, e_t, *cast_weights)
    return o_attn, cast


def _oproj_kernel(oa_ref, ov_ref, wo_ref, x_ref, g1_ref, g2_ref, x1_ref, h2_ref):
    ka = oa_ref.shape[1]
    tm = oa_ref.shape[0]
    for rows in (slice(k * tm // 4, (k + 1) * tm // 4) for k in range(4)):
        mix = jnp.dot(oa_ref[rows, :], wo_ref[0:ka, :], preferred_element_type=F32)
        mix = mix + jnp.dot(ov_ref[rows, :], wo_ref[ka:, :], preferred_element_type=F32)
        x1 = x_ref[rows, :] + _rms(mix, g1_ref[...])
        x1_ref[rows, :] = x1
        h2_ref[rows, :] = _rms(x1, g2_ref[...]).astype(BF16)


def _oproj(o_attn, o_conv, w_o, x2, g_post, g_pre, *, tm=512):
    m = x2.shape[0]
    ka, kv = o_attn.shape[1], o_conv.shape[1]
    row = lambda i: (i, 0)
    fixed = lambda i: (0, 0)
    return pl.pallas_call(
        _oproj_kernel,
        grid=(m // tm,),
        in_specs=[
            pl.BlockSpec((tm, ka), row),
            pl.BlockSpec((tm, kv), row),
            pl.BlockSpec((ka + kv, D_MODEL), fixed),
            pl.BlockSpec((tm, D_MODEL), row),
            pl.BlockSpec((1, D_MODEL), fixed),
            pl.BlockSpec((1, D_MODEL), fixed),
        ],
        out_specs=[pl.BlockSpec((tm, D_MODEL), row), pl.BlockSpec((tm, D_MODEL), row)],
        out_shape=[jax.ShapeDtypeStruct((m, D_MODEL), F32), jax.ShapeDtypeStruct((m, D_MODEL), BF16)],
        compiler_params=_params(1),
        name="out_proj",
    )(o_attn, o_conv, w_o, x2, g_post, g_pre)


def _ffn_kernel(h_ref, wu_ref, wd_ref, x1_ref, g_ref, o_ref, acc_ref):
    j = pl.program_id(1)

    @pl.when(j == 0)
    def _():
        acc_ref[...] = jnp.zeros(acc_ref.shape, F32)

    a = jnp.dot(h_ref[...], wu_ref[...], preferred_element_type=F32)
    a = jnp.square(jnp.maximum(a, 0.0)).astype(BF16)
    acc_ref[...] += jnp.dot(a, wd_ref[...], preferred_element_type=F32)

    @pl.when(j == pl.num_programs(1) - 1)
    def _():
        o_ref[...] = x1_ref[...] + _rms(acc_ref[...], g_ref[...])


def _ffn(h2, w_up, w_down, x1, g_post, *, tm=512, tf=1024):
    m = h2.shape[0]
    d_ff = w_up.shape[1]
    return pl.pallas_call(
        _ffn_kernel,
        grid=(m // tm, d_ff // tf),
        in_specs=[
            pl.BlockSpec((tm, D_MODEL), lambda i, j: (i, 0)),
            pl.BlockSpec((D_MODEL, tf), lambda i, j: (0, j)),
            pl.BlockSpec((tf, D_MODEL), lambda i, j: (j, 0)),
            pl.BlockSpec((tm, D_MODEL), lambda i, j: (i, 0)),
            pl.BlockSpec((1, D_MODEL), lambda i, j: (0, 0)),
        ],
        out_specs=pl.BlockSpec((tm, D_MODEL), lambda i, j: (i, 0)),
        out_shape=jax.ShapeDtypeStruct((m, D_MODEL), F32),
        scratch_shapes=[pltpu.VMEM((tm, D_MODEL), F32)],
        compiler_params=_params(2),
        name="ffn",
    )(h2, w_up, w_down, x1, g_post)


def _bucket_np(dist):
    n = np.maximum(dist, 0)
    max_exact = N_BUCKETS // 2
    nf = np.maximum(n, 1).astype(np.float32)
    large = max_exact + (np.log(nf / np.float32(max_exact)) / np.float32(math.log(MAX_DIST / max_exact))
                         * np.float32(N_BUCKETS - max_exact)).astype(np.int32)
    large = np.minimum(large, N_BUCKETS - 1)
    return np.where(n < max_exact, n, large)


def _bucket_starts():
    b = _bucket_np(np.arange(4 * MAX_DIST))
    return [int(np.argmax(b == k)) for k in range(N_BUCKETS)]


def _attention_tables(rel_bias, seq_len, tq):
    starts = _bucket_starts()
    assert starts[N_BUCKETS - 1] <= CMP_STRIDE * (BAND_BELOW + 1) - (CMP_LEN - 1)
    assert starts[N_BUCKETS - 1] <= tq and 2 * tq <= BIAS_LEN // 2
    rel = (rel_bias - rel_bias[:, N_BUCKETS - 1:]) * math.log2(math.e)
    d = jnp.arange(BIAS_LEN, dtype=jnp.int32)[None, :]
    fvec = jnp.broadcast_to(rel[:, 0:1], (N_HEADS, BIAS_LEN))
    for k in range(1, N_BUCKETS):
        fvec = jnp.where(d >= starts[k], rel[:, k:k + 1], fvec)
    fvec = jnp.where(d < BIAS_LEN // 2, fvec, NEG).astype(F32).reshape(N_KV, HPG, 1, BIAS_LEN)

    n_chunks = seq_len // CMP_STRIDE
    n_cmp = (seq_len - CMP_LEN) // CMP_STRIDE + 1
    n_sel = seq_len // SEL_LEN
    ci = np.arange(n_chunks)[None, :] * CMP_STRIDE
    sj = np.arange(n_sel)[:, None] * SEL_LEN
    ovl = ((ci < sj + SEL_LEN) & (ci + CMP_LEN > sj) & (np.arange(n_chunks)[None, :] < n_cmp))
    e_t = (np.arange(seq_len)[:, None] // SEL_LEN == np.arange(LANE)[None, :])
    return fvec, jnp.asarray(ovl, BF16), jnp.asarray(e_t, BF16)


def kernel(x, w_in, pe_cmp, w_cmp_k1, w_cmp_k2, w_cmp_v1, w_cmp_v2, conv_w, rel_bias, w_o, w_up, w_down,
           g_pre_mix, g_post_mix, g_pre_ffn, g_post_ffn):
    batch, seq_len, _ = x.shape
    depth = w_in.shape[0]
    tq = 256
    fvec, ovl, e_t = _attention_tables(rel_bias, seq_len, tq)
    x2 = x.reshape(batch * seq_len, D_MODEL)
    for l in range(depth):
        wl = jnp.swapaxes(w_in[l], 0, 1)
        g1 = g_pre_mix[l].reshape(1, D_MODEL)

        q, k_all, c_in, vt_all, gates, o_conv = _in_proj(x2, g1, wl, conv_w[l], batch, seq_len)

        w1 = jnp.stack([w_cmp_k1[l], w_cmp_v1[l]]).astype(BF16)
        w2 = jnp.stack([w_cmp_k2[l], w_cmp_v2[l]]).astype(BF16)
        cmp, cmp_t = _compress(c_in, pe_cmp[l], w1, w2, batch, seq_len)

        o_attn, (wo_b, wup_b, wdown_b) = _nsa(q, k_all, vt_all, cmp, cmp_t, gates, fvec, ovl, e_t,
                                              [w_o[l], w_up[l], w_down[l]], batch, seq_len, tq=tq)

        x1, h2 = _oproj(o_attn, o_conv, wo_b, x2,
                        g_post_mix[l].reshape(1, D_MODEL), g_pre_ffn[l].reshape(1, D_MODEL))
        x2 = _ffn(h2, wup_b, wdown_b, x1, g_post_ffn[l].reshape(1, D_MODEL))
    return x2.reshape(batch, seq_len, D_MODEL)
```

```python
import functools
import math

import numpy as np
import jax
import jax.numpy as jnp
from jax import lax
from jax.experimental import pallas as pl
from jax.experimental.pallas import tpu as pltpu

F32 = jnp.float32
BF16 = jnp.bfloat16

D_MODEL = 2048
N_HEADS = 8
N_KV = 2
HPG = N_HEADS // N_KV
HEAD_DIM = 128
ATTN_WIDTH = N_HEADS * HEAD_DIM
KV_WIDTH = N_KV * HEAD_DIM
CONV_WIDTH = D_MODEL - ATTN_WIDTH
CONV_K = 3
N_BRANCH = 3
CMP_LEN = 32
CMP_STRIDE = 16
CMP_HIDDEN = 256
SEL_LEN = 64
SEL_TOPK = 16
WINDOW = 512
N_BUCKETS = 32
MAX_DIST = 128
EPS = 1e-6
NEG = -1e30
HALF_NEG = -5e29
FORCE = 1e9

QKV_WIDTH = ATTN_WIDTH + 6 * KV_WIDTH
GATE_OFF = QKV_WIDTH
CONV_OFF = QKV_WIDTH + N_HEADS * N_BRANCH
LANE = 128
VMEM_LIMIT = 56 * 1024 * 1024

_DN_T = (((1,), (1,)), ((), ()))


def _rms(x, g):
    ms = jnp.mean(x * x, axis=-1, keepdims=True)
    return x * lax.rsqrt(ms + EPS) * g


def _params(n_axes):
    return pltpu.CompilerParams(dimension_semantics=("arbitrary",) * n_axes, vmem_limit_bytes=VMEM_LIMIT)


QKV_BLOCK = 2 * KV_WIDTH
SLAB_KC, SLAB_VC, SLAB_KS, SLAB_VS, SLAB_KW, SLAB_VW = range(6)
CONV_BLOCK = 512
W_CHUNK = 256
SUBLANE = 8


def _load_weights(wt_hbm, w_ref, wg_ref, wconv_ref, stage_ref, sem):
    n_gate = N_HEADS * N_BRANCH
    per_group = HPG * N_BRANCH

    def store_rows(dst, row):
        def store(v):
            dst[row:row + v.shape[0], :] = v.astype(BF16)
        return store

    def store_gates(v):
        wg_ref[...] = jnp.zeros(wg_ref.shape, BF16)
        pad = jnp.zeros((2 * SUBLANE - per_group, v.shape[1]), F32)
        for g in range(N_KV):
            rows = jnp.concatenate([v[g * per_group:(g + 1) * per_group, :], pad], axis=0)
            wg_ref[g * LANE:g * LANE + 2 * SUBLANE, :] = rows.astype(BF16)

    chunks = [(r, W_CHUNK, store_rows(w_ref, r)) for r in range(0, QKV_WIDTH, W_CHUNK)]
    chunks.append((GATE_OFF, n_gate, store_gates))
    chunks += [(CONV_OFF + r, W_CHUNK, store_rows(wconv_ref, r)) for r in range(0, 3 * CONV_WIDTH, W_CHUNK)]

    def copy(k):
        src, n, _ = chunks[k]
        slot = k % 2
        return pltpu.make_async_copy(wt_hbm.at[pl.ds(src, n), :], stage_ref.at[slot, pl.ds(0, n), :], sem.at[slot])

    copy(0).start()
    for k, (_, n, store) in enumerate(chunks):
        if k + 1 < len(chunks):
            copy(k + 1).start()
        copy(k).wait()
        store(stage_ref[k % 2, 0:n, :])


def _inproj_kernel(x_ref, g_ref, wt_hbm, cw_ref, q_ref, k_ref, c_ref, vt_ref, gate_ref, ov_ref,
                   h_ref, carry_ref, w_ref, wg_ref, wconv_ref, stage_ref, sem, *, q_scale, tiles_per_seq):
    i = pl.program_id(0)
    tm = x_ref.shape[0]

    @pl.when(i == 0)
    def _():
        _load_weights(wt_hbm, w_ref, wg_ref, wconv_ref, stage_ref, sem)

    h_ref[...] = _rms(x_ref[...], g_ref[...]).astype(BF16)
    gate_ref[...] = jax.nn.sigmoid(lax.dot_general(h_ref[...], wg_ref[...], _DN_T, preferred_element_type=F32))

    def proj(w, j, width):
        return lax.dot_general(h_ref[...], w[j * width:(j + 1) * width, :], _DN_T, preferred_element_type=F32)

    for j in range(ATTN_WIDTH // QKV_BLOCK):
        q_ref[:, j * QKV_BLOCK:(j + 1) * QKV_BLOCK] = (proj(w_ref, j, QKV_BLOCK) * q_scale).astype(BF16)

    def slab(which):
        return proj(w_ref, ATTN_WIDTH // KV_WIDTH + which, KV_WIDTH)

    ones = jnp.where(lax.broadcasted_iota(jnp.int32, (ONES_ROWS, tm), 0) == 0, 1.0, 0.0).astype(BF16)
    for pair, (k_slab, c_slab, v_slab) in enumerate(((SLAB_KS, SLAB_KC, SLAB_VS), (SLAB_KW, SLAB_VC, SLAB_VW))):
        cols = slice(pair * KV_WIDTH, (pair + 1) * KV_WIDTH)
        k_ref[:, cols] = slab(k_slab).astype(BF16)
        c_ref[:, cols] = slab(c_slab)
        v = slab(v_slab)
        for g in range(N_KV):
            s = pair * N_KV + g
            vt_ref[s, 0:HEAD_DIM, :] = v[:, g * HEAD_DIM:(g + 1) * HEAD_DIM].T.astype(BF16)
            vt_ref[s, HEAD_DIM:HEAD_DIM + ONES_ROWS, :] = ones

    def conv_proj(which, j):
        return proj(wconv_ref, which * (CONV_WIDTH // CONV_BLOCK) + j, CONV_BLOCK)

    row = lax.broadcasted_iota(jnp.int32, (tm, CONV_BLOCK), 0)
    for j in range(CONV_WIDTH // CONV_BLOCK):
        cols = slice(j * CONV_BLOCK, (j + 1) * CONV_BLOCK)
        u = conv_proj(2, j) * conv_proj(0, j)
        prev = carry_ref[j]
        prev = jnp.where(i % tiles_per_seq == 0, 0.0, prev)
        carry_ref[j] = u[tm - 8:tm, :]
        u1 = jnp.where(row == 0, prev[7:8, :], pltpu.roll(u, 1, axis=0))
        u2 = jnp.where(row == 0, prev[6:7, :], jnp.where(row == 1, prev[7:8, :], pltpu.roll(u, 2, axis=0)))
        w = cw_ref[:, cols]
        y = w[0:1, :] * u2
        y = y + w[1:2, :] * u1
        y = y + w[2:3, :] * u
        ov_ref[:, cols] = (conv_proj(1, j) * y).astype(BF16)


def _in_proj(x2, g, w_t, conv_w, batch, seq_len, *, tm=512):
    m = x2.shape[0]
    gw = N_KV * LANE
    tps = seq_len // tm
    n_slabs = QKV_BLOCK // HEAD_DIM
    kern = functools.partial(_inproj_kernel, q_scale=HEAD_DIM ** -0.5 * math.log2(math.e), tiles_per_seq=tps)
    row = lambda i: (i, 0)

    def resident(shape):
        return pl.BlockSpec(shape, lambda i: (0, 0), pipeline_mode=pl.Buffered(1))

    return pl.pallas_call(
        kern,
        grid=(m // tm,),
        in_specs=[
            pl.BlockSpec((tm, D_MODEL), row),
            resident((1, D_MODEL)),
            pl.BlockSpec(memory_space=pl.ANY),
            resident(conv_w.shape),
        ],
        out_specs=[
            pl.BlockSpec((tm, ATTN_WIDTH), row),
            pl.BlockSpec((tm, QKV_BLOCK), row),
            pl.BlockSpec((tm, QKV_BLOCK), row),
            pl.BlockSpec((None, n_slabs, HEAD_DIM + ONES_ROWS, tm), lambda i: (i // tps, 0, 0, i % tps)),
            pl.BlockSpec((tm, gw), row),
            pl.BlockSpec((tm, CONV_WIDTH), row),
        ],
        out_shape=[
            jax.ShapeDtypeStruct((m, ATTN_WIDTH), BF16),
            jax.ShapeDtypeStruct((m, QKV_BLOCK), BF16),
            jax.ShapeDtypeStruct((m, QKV_BLOCK), F32),
            jax.ShapeDtypeStruct((batch, n_slabs, HEAD_DIM + ONES_ROWS, seq_len), BF16),
            jax.ShapeDtypeStruct((m, gw), F32),
            jax.ShapeDtypeStruct((m, CONV_WIDTH), BF16),
        ],
        scratch_shapes=[pltpu.VMEM((tm, D_MODEL), BF16),
                        pltpu.VMEM((CONV_WIDTH // CONV_BLOCK, 8, CONV_BLOCK), F32),
                        pltpu.VMEM((QKV_WIDTH, D_MODEL), BF16),
                        pltpu.VMEM((gw, D_MODEL), BF16),
                        pltpu.VMEM((3 * CONV_WIDTH, D_MODEL), BF16),
                        pltpu.VMEM((2, W_CHUNK, D_MODEL), F32),
                        pltpu.SemaphoreType.DMA((2,))],
        compiler_params=_params(1),
        name="in_proj",
    )(x2, g, w_t, conv_w)


def _compress_kernel(x_ref, pe_ref, w1_ref, w2_ref, o_ref, ot_ref):
    n = x_ref.shape[0] // CMP_STRIDE
    a = jnp.zeros((n, CMP_HIDDEN), F32)
    b = jnp.zeros((n, CMP_HIDDEN), F32)
    for l in range(CMP_STRIDE):
        xl = x_ref[pl.ds(l, n, stride=CMP_STRIDE), :]
        xa = (xl + pe_ref[l:l + 1, :]).astype(BF16)
        xb = (xl + pe_ref[CMP_STRIDE + l:CMP_STRIDE + l + 1, :]).astype(BF16)
        a = a + jnp.dot(xa, w1_ref[l], preferred_element_type=F32)
        b = b + jnp.dot(xb, w1_ref[CMP_STRIDE + l], preferred_element_type=F32)
    pre = a + pltpu.roll(b, n - 1, axis=0)
    hid = pre * jax.nn.sigmoid(pre)
    out = jnp.dot(hid.astype(BF16), w2_ref[...], preferred_element_type=F32)
    row = lax.broadcasted_iota(jnp.int32, out.shape, 0)
    out = jnp.where(row < n - 1, out, 0.0)
    o_ref[...] = out.astype(BF16)
    ot_ref[...] = out.T.astype(BF16)


def _compress(c_in, pe, w1, w2, batch, seq_len):
    n_slabs = c_in.shape[1] // HEAD_DIM
    n_chunks = seq_len // CMP_STRIDE
    return pl.pallas_call(
        _compress_kernel,
        grid=(batch, n_slabs),
        in_specs=[
            pl.BlockSpec((seq_len, HEAD_DIM), lambda i, j: (i, j)),
            pl.BlockSpec((CMP_LEN, HEAD_DIM), lambda i, j: (0, 0)),
            pl.BlockSpec((None, CMP_LEN, HEAD_DIM, CMP_HIDDEN), lambda i, j: (j // N_KV, 0, 0, 0)),
            pl.BlockSpec((None, CMP_HIDDEN, HEAD_DIM), lambda i, j: (j // N_KV, 0, 0)),
        ],
        out_specs=[
            pl.BlockSpec((None, None, n_chunks, HEAD_DIM), lambda i, j: (i, j, 0, 0)),
            pl.BlockSpec((None, None, HEAD_DIM, n_chunks), lambda i, j: (i, j, 0, 0)),
        ],
        out_shape=[
            jax.ShapeDtypeStruct((batch, n_slabs, n_chunks, HEAD_DIM), BF16),
            jax.ShapeDtypeStruct((batch, n_slabs, HEAD_DIM, n_chunks), BF16),
        ],
        compiler_params=_params(2),
        name="compress",
    )(c_in, pe, w1, w2)


TAB_DIAG, TAB_SUB, TAB_FAR, TAB_MASK = 0, 1, 2, 3
BAND_BELOW = 8
ONES_ROWS = 16


N_NSA_INPUTS = 11
BIAS_LEN = 1024
FAR_UNROLL = 4
NSA_TILES_PER_STEP = 2
FRONT_VARIANTS = (32, 64)


def _nsa_kernel(*refs, tiles_per_step, **static):
    for u in range(tiles_per_step):
        _nsa_tile(refs, u, tiles_per_step, **static)


def _nsa_tile(refs, u, tiles_per_step, *, tq, n_sel, n_cast):
    (q_ref, kc_ref, vct_ref, ks_ref, vst_ref, kw_ref, vwt_ref, gate_ref, fvec_ref,
     ovl_ref, et_ref) = refs[:N_NSA_INPUTS]
    cast_in = refs[N_NSA_INPUTS:N_NSA_INPUTS + n_cast]
    o_ref = refs[N_NSA_INPUTS + n_cast]
    cast_out = refs[N_NSA_INPUTS + n_cast + 1:N_NSA_INPUTS + 2 * n_cast + 1]
    (kaug_ref, tab_ref, band_ref, sc_ref, s0_ref, s1_ref, m_ref, mt_ref,
     acc_ref, qa_ref, oc_ref, ow_ref) = refs[N_NSA_INPUTS + 2 * n_cast + 1:]
    i = pl.program_id(2) * tiles_per_step + u
    rows = slice(u * tq, (u + 1) * tq)

    tk = tq
    mcols = HPG * tq
    nw = WINDOW // tk
    cpt = tq // CMP_STRIDE
    band_rows = BAND_BELOW + cpt

    def group_start(fn):
        if u == 0:
            pl.when(i == 0)(fn)

    @group_start
    def _():
        kaug_ref[:, 0:HEAD_DIM] = ks_ref[...]
        kaug_ref[:, HEAD_DIM:2 * HEAD_DIM] = et_ref[...]

    @group_start
    def _():
        c = lax.broadcasted_iota(jnp.int32, (tk, tq), 0)
        r = lax.broadcasted_iota(jnp.int32, (tk, tq), 1)
        far = jnp.where(r < c, 0.0, NEG)
        lane = lax.broadcasted_iota(jnp.int32, (1, BIAS_LEN), 1)
        for h in range(HPG):
            cols = slice(h * tq, (h + 1) * tq)
            f = fvec_ref[h]
            f_diag = jnp.where(lane < tq, f, NEG)
            x = pltpu.roll(jnp.broadcast_to(f_diag, (tk, BIAS_LEN)), 0, 1, stride=1, stride_axis=0)
            tab_ref[TAB_DIAG, :, cols] = x[:, 0:tq]
            x = pltpu.roll(jnp.broadcast_to(f, (tk, BIAS_LEN)), 0, 1, stride=1, stride_axis=0)
            tab_ref[TAB_SUB, :, cols] = x[:, tq:2 * tq]
            tab_ref[TAB_FAR, :, cols] = far
            tab_ref[TAB_MASK, :, cols] = jnp.full((tk, tq), NEG, F32)
            for v, first in enumerate((0, -BAND_BELOW)):
                shift = (CMP_STRIDE * first + CMP_LEN - 1) % BIAS_LEN
                f_shift = pltpu.roll(f, shift, 1)
                x = pltpu.roll(jnp.broadcast_to(f_shift, (band_rows, BIAS_LEN)), 0, 1,
                               stride=CMP_STRIDE, stride_axis=0)
                band_ref[v, :, cols] = x[:, 0:tq]

    q = q_ref[rows, :]
    qs = jnp.concatenate([q[:, h * HEAD_DIM:(h + 1) * HEAD_DIM] for h in range(HPG)], axis=0)

    s_refs = (s0_ref, s1_ref)

    def init():
        m_ref[...] = jnp.full((1, mcols), NEG, F32)
        acc_ref[...] = jnp.zeros(acc_ref.shape, F32)

    def keys(kt):
        return pl.ds(pl.multiple_of(kt * tk, tk), tk)

    def qk(qmat, k_ref, kt, kind, buf):
        s = lax.dot_general(k_ref[keys(kt), :], qmat, _DN_T, preferred_element_type=F32)
        if kind is not None:
            s = s + tab_ref[kind]
        s_refs[buf][...] = s
        return jnp.max(s, axis=0, keepdims=True)

    def process(vt_ref, kt, buf, m_tile):
        m_prev = m_ref[...]
        m_next = jnp.maximum(m_prev, m_tile)
        alpha = jnp.exp2(m_prev - m_next)
        p = jnp.exp2(s_refs[buf][...] - m_next).astype(BF16)
        acc_ref[...] = alpha * acc_ref[...] + jnp.dot(vt_ref[:, keys(kt)], p, preferred_element_type=F32)
        m_ref[...] = m_next

    def finish():
        return acc_ref[0:HEAD_DIM, :] * (1.0 / acc_ref[HEAD_DIM:HEAD_DIM + 1, :])

    def front(nb):
        nc = nb * (SEL_LEN // CMP_STRIDE)
        assert nw == 2, "window tiles are i, i-1 (previous-tile table) and i-nw (window-edge table)"
        init()
        mt_w0 = qk(qs, kw_ref, i, TAB_DIAG, 0)

        if u == 0:
            for src, dst in zip(cast_in, cast_out):
                dst[...] = src[...].astype(BF16)

        raw = lax.dot_general(kc_ref[0:nc, :], qs, _DN_T, preferred_element_type=F32)
        crow = lax.broadcasted_iota(jnp.int32, raw.shape, 0)
        sc_ref[0:nc, :] = jnp.where(crow < cpt * (i + 1), raw, NEG)
        band = pl.ds(pl.multiple_of(jnp.maximum(cpt * i - BAND_BELOW, 0), 8), band_rows)
        sc_ref[band, :] += band_ref[jnp.minimum(i, 1)]

        mt_w1 = qk(qs, kw_ref, jnp.maximum(i - 1, 0), jnp.where(i >= 1, TAB_SUB, TAB_MASK), 1)
        process(vwt_ref, i, 0, mt_w0)

        sc = sc_ref[0:nc, :]
        mc = jnp.maximum(jnp.max(sc, axis=0, keepdims=True), HALF_NEG)
        pc = jnp.exp2(sc - mc)
        lc = jnp.sum(pc, axis=0, keepdims=True)
        pc = pc * jnp.where(lc > 0.0, 1.0 / lc, 0.0)
        o_c = jnp.dot(vct_ref[:, 0:nc], pc.astype(BF16), preferred_element_type=F32)

        mt_w2 = qk(qs, kw_ref, jnp.maximum(i - nw, 0), jnp.where(i >= nw, TAB_FAR, TAB_MASK), 0)
        process(vwt_ref, jnp.maximum(i - 1, 0), 1, mt_w1)

        ps = pc[:, 0:tq] + pc[:, tq:2 * tq] + pc[:, 2 * tq:3 * tq] + pc[:, 3 * tq:4 * tq]
        hi = ps.astype(BF16)
        r1 = ps - hi.astype(F32)
        mid = r1.astype(BF16)
        lo = (r1 - mid.astype(F32)).astype(BF16)
        ovl = ovl_ref[0:nb, 0:nc]
        imp = (jnp.dot(ovl, hi, preferred_element_type=F32) + jnp.dot(ovl, mid, preferred_element_type=F32)
               + jnp.dot(ovl, lo, preferred_element_type=F32))
        jj = lax.broadcasted_iota(jnp.int32, (nb, tq), 0)
        tt = i * tq + lax.broadcasted_iota(jnp.int32, (nb, tq), 1)
        cur = tt >> int(math.log2(SEL_LEN))
        forced = (jj == 0) | (jj == cur) | (jj == cur - 1)
        imp = jnp.where(forced, FORCE, imp)
        imp = jnp.where(jj * SEL_LEN <= tt, imp, NEG)
        sub = 8
        ranks = []
        for j0 in range(0, nb, sub):
            blk = imp[j0:j0 + sub, :]
            jl = j0 + lax.broadcasted_iota(jnp.int32, blk.shape, 0)
            cnt = jnp.zeros(blk.shape, jnp.int32)
            for b in range(nb):
                row = imp[b:b + 1, :]
                if b < j0:
                    cnt = cnt + jnp.where(row >= blk, 1, 0)
                elif b >= j0 + sub:
                    cnt = cnt + jnp.where(row > blk, 1, 0)
                else:
                    cnt = cnt + jnp.where(row > blk, 1, jnp.where(row == blk, jnp.where(jl > b, 1, 0), 0))
            ranks.append(cnt)
        rank = jnp.concatenate(ranks, axis=0)
        selb_t = jnp.where(rank < SEL_TOPK, 0.0, NEG)

        process(vwt_ref, jnp.maximum(i - nw, 0), 0, mt_w2)
        o_w = finish()

        selb = jnp.concatenate([selb_t, jnp.zeros((LANE - nb, tq), F32)], axis=0).T
        selb = selb.astype(BF16)
        qa_ref[...] = jnp.concatenate([qs, jnp.concatenate([selb] * HPG, axis=0)], axis=1)
        oc_ref[...] = o_c
        ow_ref[...] = o_w

    lo_nb = 0
    for nb in FRONT_VARIANTS:
        need = (tq // SEL_LEN) * (i + 1)
        pl.when((need > lo_nb) & (need <= nb))(functools.partial(front, nb))
        lo_nb = nb
    qa, o_c, o_w = qa_ref[...], oc_ref[...], ow_ref[...]

    init()

    @pl.when(i == 0)
    def _():
        process(vst_ref, 0, 0, qk(qa, kaug_ref, 0, TAB_DIAG, 0))

    @pl.when(i >= 1)
    def _():
        mt0 = qk(qa, kaug_ref, i, TAB_DIAG, 0)
        mt1 = qk(qa, kaug_ref, i - 1, TAB_SUB, 1)
        process(vst_ref, i, 0, mt0)
        mt0 = qk(qa, kaug_ref, jnp.maximum(i - 2, 0), None, 0)
        process(vst_ref, i - 1, 1, mt1)
        n_far = i - 1
        rem = n_far % FAR_UNROLL

        def far_tiles(a, n, mt0):
            for t in range(n):
                nxt = qk(qa, kaug_ref, jnp.maximum(a - t - 1, 0), None, (t + 1) % 2)
                process(vst_ref, a - t, t % 2, mt0)
                mt0 = nxt
            return mt0

        mt0 = lax.fori_loop(0, n_far // FAR_UNROLL,
                            lambda p, mt: far_tiles(i - 2 - FAR_UNROLL * p, FAR_UNROLL, mt), mt0)

        @pl.when(rem >= 2)
        def _():
            mt_ref[...] = far_tiles(rem - 1, 2, mt0)

        @pl.when(rem == 1)
        def _():
            process(vst_ref, 0, 0, mt0)

        @pl.when(rem == 3)
        def _():
            process(vst_ref, 0, 0, mt_ref[...])

    o_s = finish()

    gt = gate_ref[rows, :].T
    for h in range(HPG):
        sl = slice(h * tq, (h + 1) * tq)
        o = gt[3 * h:3 * h + 1, :] * o_c[:, sl]
        o = o + gt[3 * h + 1:3 * h + 2, :] * o_s[:, sl]
        o = o + gt[3 * h + 2:3 * h + 3, :] * o_w[:, sl]
        o_ref[rows, h * HEAD_DIM:(h + 1) * HEAD_DIM] = o.T.astype(BF16)


def _nsa(q, k_all, vt_all, cmp, cmp_t, gates, fvec, ovl, e_t, cast_weights, batch, seq_len, *, tq):
    n_t = seq_len // (tq * NSA_TILES_PER_STEP)
    n_steps = batch * N_KV * n_t
    ts = tq * NSA_TILES_PER_STEP
    step_row = lambda b, g, i: ((b * N_KV + g) * n_t + i, 0)
    cast_specs = [pl.BlockSpec((w.shape[0] // n_steps, w.shape[1]), step_row) for w in cast_weights]
    n_sel = seq_len // SEL_LEN
    n_chunks = cmp.shape[2]
    mcols = HPG * tq
    band_rows = BAND_BELOW + tq // CMP_STRIDE
    qw = HPG * HEAD_DIM

    def vt_spec(first):
        return pl.BlockSpec((None, None, HEAD_DIM + ONES_ROWS, seq_len), lambda b, g, i: (b, first + g, 0, 0))

    kern = functools.partial(_nsa_kernel, tiles_per_step=NSA_TILES_PER_STEP, tq=tq, n_sel=n_sel,
                             n_cast=len(cast_weights))
    o_attn, *cast = pl.pallas_call(
        kern,
        grid=(batch, N_KV, n_t),
        in_specs=[
            pl.BlockSpec((ts, qw), lambda b, g, i: (b * n_t + i, g)),
            pl.BlockSpec((None, None, n_chunks, HEAD_DIM), lambda b, g, i: (b, g, 0, 0)),
            pl.BlockSpec((None, None, HEAD_DIM, n_chunks), lambda b, g, i: (b, N_KV + g, 0, 0)),
            pl.BlockSpec((seq_len, HEAD_DIM), lambda b, g, i: (b, g)),
            vt_spec(0),
            pl.BlockSpec((seq_len, HEAD_DIM), lambda b, g, i: (b, N_KV + g)),
            vt_spec(N_KV),
            pl.BlockSpec((ts, LANE), lambda b, g, i: (b * n_t + i, g)),
            pl.BlockSpec((None, HPG, 1, BIAS_LEN), lambda b, g, i: (g, 0, 0, 0)),
            pl.BlockSpec((n_sel, n_chunks), lambda b, g, i: (0, 0)),
            pl.BlockSpec((seq_len, LANE), lambda b, g, i: (0, 0)),
        ] + cast_specs,
        out_specs=[pl.BlockSpec((ts, qw), lambda b, g, i: (b * n_t + i, g))] + cast_specs,
        out_shape=[jax.ShapeDtypeStruct((batch * seq_len, ATTN_WIDTH), BF16)]
        + [jax.ShapeDtypeStruct(w.shape, BF16) for w in cast_weights],
        scratch_shapes=[
            pltpu.VMEM((seq_len, 2 * HEAD_DIM), BF16),
            pltpu.VMEM((4, tq, mcols), F32),
            pltpu.VMEM((2, band_rows, mcols), F32),
            pltpu.VMEM((n_chunks, mcols), F32),
            pltpu.VMEM((tq, mcols), F32),
            pltpu.VMEM((tq, mcols), F32),
            pltpu.VMEM((1, mcols), F32),
            pltpu.VMEM((1, mcols), F32),
            pltpu.VMEM((HEAD_DIM + ONES_ROWS, mcols), F32),
            pltpu.VMEM((mcols, 2 * HEAD_DIM), BF16),
            pltpu.VMEM((HEAD_DIM, mcols), F32),
            pltpu.VMEM((HEAD_DIM, mcols), F32),
        ],
        compiler_params=_params(3),
        name="nsa_attention",
    )(q, cmp, cmp_t, k_all, vt_all, k_all, vt_all, gates, fvec, ovl, e_t, *cast_weights)
    return o_attn, cast


def _oproj_kernel(oa_ref, ov_ref, wo_ref, x_ref, g1_ref, g2_ref, x1_ref, h2_ref):
    ka = oa_ref.shape[1]
    tm = oa_ref.shape[0]
    for rows in (slice(k * tm // 4, (k + 1) * tm // 4) for k in range(4)):
        mix = jnp.dot(oa_ref[rows, :], wo_ref[0:ka, :], preferred_element_type=F32)
        mix = mix + jnp.dot(ov_ref[rows, :], wo_ref[ka:, :], preferred_element_type=F32)
        x1 = x_ref[rows, :] + _rms(mix, g1_ref[...])
        x1_ref[rows, :] = x1
        h2_ref[rows, :] = _rms(x1, g2_ref[...]).astype(BF16)


def _oproj(o_attn, o_conv, w_o, x2, g_post, g_pre, *, tm=512):
    m = x2.shape[0]
    ka, kv = o_attn.shape[1], o_conv.shape[1]
    row = lambda i: (i, 0)
    fixed = lambda i: (0, 0)
    return pl.pallas_call(
        _oproj_kernel,
        grid=(m // tm,),
        in_specs=[
            pl.BlockSpec((tm, ka), row),
            pl.BlockSpec((tm, kv), row),
            pl.BlockSpec((ka + kv, D_MODEL), fixed),
            pl.BlockSpec((tm, D_MODEL), row),
            pl.BlockSpec((1, D_MODEL), fixed),
            pl.BlockSpec((1, D_MODEL), fixed),
        ],
        out_specs=[pl.BlockSpec((tm, D_MODEL), row), pl.BlockSpec((tm, D_MODEL), row)],
        out_shape=[jax.ShapeDtypeStruct((m, D_MODEL), F32), jax.ShapeDtypeStruct((m, D_MODEL), BF16)],
        compiler_params=_params(1),
        name="out_proj",
    )(o_attn, o_conv, w_o, x2, g_post, g_pre)


def _ffn_kernel(h_ref, wu_ref, wd_ref, x1_ref, g_ref, o_ref, acc_ref):
    j = pl.program_id(1)

    @pl.when(j == 0)
    def _():
        acc_ref[...] = jnp.zeros(acc_ref.shape, F32)

    a = jnp.dot(h_ref[...], wu_ref[...], preferred_element_type=F32)
    a = jnp.square(jnp.maximum(a, 0.0)).astype(BF16)
    acc_ref[...] += jnp.dot(a, wd_ref[...], preferred_element_type=F32)

    @pl.when(j == pl.num_programs(1) - 1)
    def _():
        o_ref[...] = x1_ref[...] + _rms(acc_ref[...], g_ref[...])


def _ffn(h2, w_up, w_down, x1, g_post, *, tm=512, tf=1024):
    m = h2.shape[0]
    d_ff = w_up.shape[1]
    return pl.pallas_call(
        _ffn_kernel,
        grid=(m // tm, d_ff // tf),
        in_specs=[
            pl.BlockSpec((tm, D_MODEL), lambda i, j: (i, 0)),
            pl.BlockSpec((D_MODEL, tf), lambda i, j: (0, j)),
            pl.BlockSpec((tf, D_MODEL), lambda i, j: (j, 0)),
            pl.BlockSpec((tm, D_MODEL), lambda i, j: (i, 0)),
            pl.BlockSpec((1, D_MODEL), lambda i, j: (0, 0)),
        ],
        out_specs=pl.BlockSpec((tm, D_MODEL), lambda i, j: (i, 0)),
        out_shape=jax.ShapeDtypeStruct((m, D_MODEL), F32),
        scratch_shapes=[pltpu.VMEM((tm, D_MODEL), F32)],
        compiler_params=_params(2),
        name="ffn",
    )(h2, w_up, w_down, x1, g_post)


def _bucket_np(dist):
    n = np.maximum(dist, 0)
    max_exact = N_BUCKETS // 2
    nf = np.maximum(n, 1).astype(np.float32)
    large = max_exact + (np.log(nf / np.float32(max_exact)) / np.float32(math.log(MAX_DIST / max_exact))
                         * np.float32(N_BUCKETS - max_exact)).astype(np.int32)
    large = np.minimum(large, N_BUCKETS - 1)
    return np.where(n < max_exact, n, large)


def _bucket_starts():
    b = _bucket_np(np.arange(4 * MAX_DIST))
    return [int(np.argmax(b == k)) for k in range(N_BUCKETS)]


def _attention_tables(rel_bias, seq_len, tq):
    starts = _bucket_starts()
    assert starts[N_BUCKETS - 1] <= CMP_STRIDE * (BAND_BELOW + 1) - (CMP_LEN - 1)
    assert starts[N_BUCKETS - 1] <= tq and 2 * tq <= BIAS_LEN // 2
    rel = (rel_bias - rel_bias[:, N_BUCKETS - 1:]) * math.log2(math.e)
    d = jnp.arange(BIAS_LEN, dtype=jnp.int32)[None, :]
    fvec = jnp.broadcast_to(rel[:, 0:1], (N_HEADS, BIAS_LEN))
    for k in range(1, N_BUCKETS):
        fvec = jnp.where(d >= starts[k], rel[:, k:k + 1], fvec)
    fvec = jnp.where(d < BIAS_LEN // 2, fvec, NEG).astype(F32).reshape(N_KV, HPG, 1, BIAS_LEN)

    n_chunks = seq_len // CMP_STRIDE
    n_cmp = (seq_len - CMP_LEN) // CMP_STRIDE + 1
    n_sel = seq_len // SEL_LEN
    ci = np.arange(n_chunks)[None, :] * CMP_STRIDE
    sj = np.arange(n_sel)[:, None] * SEL_LEN
    ovl = ((ci < sj + SEL_LEN) & (ci + CMP_LEN > sj) & (np.arange(n_chunks)[None, :] < n_cmp))
    e_t = (np.arange(seq_len)[:, None] // SEL_LEN == np.arange(LANE)[None, :])
    return fvec, jnp.asarray(ovl, BF16), jnp.asarray(e_t, BF16)


def kernel(x, w_in, pe_cmp, w_cmp_k1, w_cmp_k2, w_cmp_v1, w_cmp_v2, conv_w, rel_bias, w_o, w_up, w_down,
           g_pre_mix, g_post_mix, g_pre_ffn, g_post_ffn):
    batch, seq_len, _ = x.shape
    depth = w_in.shape[0]
    tq = 256
    fvec, ovl, e_t = _attention_tables(rel_bias, seq_len, tq)
    x2 = x.reshape(batch * seq_len, D_MODEL)
    for l in range(depth):
        wl = jnp.swapaxes(w_in[l], 0, 1)
        g1 = g_pre_mix[l].reshape(1, D_MODEL)

        q, k_all, c_in, vt_all, gates, o_conv = _in_proj(x2, g1, wl, conv_w[l], batch, seq_len)

        w1 = jnp.stack([w_cmp_k1[l], w_cmp_v1[l]]).astype(BF16)
        w2 = jnp.stack([w_cmp_k2[l], w_cmp_v2[l]]).astype(BF16)
        cmp, cmp_t = _compress(c_in, pe_cmp[l], w1, w2, batch, seq_len)

        o_attn, (wo_b, wup_b, wdown_b) = _nsa(q, k_all, vt_all, cmp, cmp_t, gates, fvec, ovl, e_t,
                                              [w_o[l], w_up[l], w_down[l]], batch, seq_len, tq=tq)

        x1, h2 = _oproj(o_attn, o_conv, wo_b, x2,
                        g_post_mix[l].reshape(1, D_MODEL), g_pre_ffn[l].reshape(1, D_MODEL))
        x2 = _ffn(h2, wup_b, wdown_b, x1, g_post_ffn[l].reshape(1, D_MODEL))
    return x2.reshape(batch, seq_len, D_MODEL)
```

```python
import functools
import math

import numpy as np
import jax
import jax.numpy as jnp
from jax import lax
from jax.experimental import pallas as pl
from jax.experimental.pallas import tpu as pltpu

F32 = jnp.float32
BF16 = jnp.bfloat16

D_MODEL = 2048
N_HEADS = 8
N_KV = 2
HPG = N_HEADS // N_KV
HEAD_DIM = 128
ATTN_WIDTH = N_HEADS * HEAD_DIM
KV_WIDTH = N_KV * HEAD_DIM
CONV_WIDTH = D_MODEL - ATTN_WIDTH
CONV_K = 3
N_BRANCH = 3
CMP_LEN = 32
CMP_STRIDE = 16
CMP_HIDDEN = 256
SEL_LEN = 64
SEL_TOPK = 16
WINDOW = 512
N_BUCKETS = 32
MAX_DIST = 128
EPS = 1e-6
NEG = -1e30
HALF_NEG = -5e29
FORCE = 1e9

QKV_WIDTH = ATTN_WIDTH + 6 * KV_WIDTH
GATE_OFF = QKV_WIDTH
CONV_OFF = QKV_WIDTH + N_HEADS * N_BRANCH
LANE = 128
VMEM_LIMIT = 56 * 1024 * 1024

_DN_T = (((1,), (1,)), ((), ()))


def _rms(x, g):
    ms = jnp.mean(x * x, axis=-1, keepdims=True)
    return x * lax.rsqrt(ms + EPS) * g


def _params(n_axes):
    return pltpu.CompilerParams(dimension_semantics=("arbitrary",) * n_axes, vmem_limit_bytes=VMEM_LIMIT)


QKV_BLOCK = 2 * KV_WIDTH
SLAB_KC, SLAB_VC, SLAB_KS, SLAB_VS, SLAB_KW, SLAB_VW = range(6)
CONV_BLOCK = 512
W_CHUNK = 256
SUBLANE = 8


def _load_weights(wt_hbm, w_ref, wg_ref, wconv_ref, stage_ref, sem):
    n_gate = N_HEADS * N_BRANCH
    per_group = HPG * N_BRANCH

    def store_rows(dst, row):
        def store(v):
            dst[row:row + v.shape[0], :] = v.astype(BF16)
        return store

    def store_gates(v):
        wg_ref[...] = jnp.zeros(wg_ref.shape, BF16)
        pad = jnp.zeros((2 * SUBLANE - per_group, v.shape[1]), F32)
        for g in range(N_KV):
            rows = jnp.concatenate([v[g * per_group:(g + 1) * per_group, :], pad], axis=0)
            wg_ref[g * LANE:g * LANE + 2 * SUBLANE, :] = rows.astype(BF16)

    chunks = [(r, W_CHUNK, store_rows(w_ref, r)) for r in range(0, QKV_WIDTH, W_CHUNK)]
    chunks.append((GATE_OFF, n_gate, store_gates))
    chunks += [(CONV_OFF + r, W_CHUNK, store_rows(wconv_ref, r)) for r in range(0, 3 * CONV_WIDTH, W_CHUNK)]

    def copy(k):
        src, n, _ = chunks[k]
        slot = k % 2
        return pltpu.make_async_copy(wt_hbm.at[pl.ds(src, n), :], stage_ref.at[slot, pl.ds(0, n), :], sem.at[slot])

    copy(0).start()
    for k, (_, n, store) in enumerate(chunks):
        if k + 1 < len(chunks):
            copy(k + 1).start()
        copy(k).wait()
        store(stage_ref[k % 2, 0:n, :])


def _inproj_kernel(x_ref, g_ref, wt_hbm, cw_ref, q_ref, k_ref, c_ref, vt_ref, gate_ref, ov_ref,
                   h_ref, carry_ref, w_ref, wg_ref, wconv_ref, stage_ref, sem, *, q_scale, tiles_per_seq):
    i = pl.program_id(0)
    tm = x_ref.shape[0]

    @pl.when(i == 0)
    def _():
        _load_weights(wt_hbm, w_ref, wg_ref, wconv_ref, stage_ref, sem)

    h_ref[...] = _rms(x_ref[...], g_ref[...]).astype(BF16)
    gate_ref[...] = jax.nn.sigmoid(lax.dot_general(h_ref[...], wg_ref[...], _DN_T, preferred_element_type=F32))

    def proj(w, j, width):
        return lax.dot_general(h_ref[...], w[j * width:(j + 1) * width, :], _DN_T, preferred_element_type=F32)

    for j in range(ATTN_WIDTH // QKV_BLOCK):
        q_ref[:, j * QKV_BLOCK:(j + 1) * QKV_BLOCK] = (proj(w_ref, j, QKV_BLOCK) * q_scale).astype(BF16)

    def slab(which):
        return proj(w_ref, ATTN_WIDTH // KV_WIDTH + which, KV_WIDTH)

    ones = jnp.where(lax.broadcasted_iota(jnp.int32, (ONES_ROWS, tm), 0) == 0, 1.0, 0.0).astype(BF16)
    for pair, (k_slab, c_slab, v_slab) in enumerate(((SLAB_KS, SLAB_KC, SLAB_VS), (SLAB_KW, SLAB_VC, SLAB_VW))):
        cols = slice(pair * KV_WIDTH, (pair + 1) * KV_WIDTH)
        k_ref[:, cols] = slab(k_slab).astype(BF16)
        c_ref[:, cols] = slab(c_slab)
        v = slab(v_slab)
        for g in range(N_KV):
            s = pair * N_KV + g
            vt_ref[s, 0:HEAD_DIM, :] = v[:, g * HEAD_DIM:(g + 1) * HEAD_DIM].T.astype(BF16)
            vt_ref[s, HEAD_DIM:HEAD_DIM + ONES_ROWS, :] = ones

    def conv_proj(which, j):
        return proj(wconv_ref, which * (CONV_WIDTH // CONV_BLOCK) + j, CONV_BLOCK)

    row = lax.broadcasted_iota(jnp.int32, (tm, CONV_BLOCK), 0)
    for j in range(CONV_WIDTH // CONV_BLOCK):
        cols = slice(j * CONV_BLOCK, (j + 1) * CONV_BLOCK)
        u = conv_proj(2, j) * conv_proj(0, j)
        prev = carry_ref[j]
        prev = jnp.where(i % tiles_per_seq == 0, 0.0, prev)
        carry_ref[j] = u[tm - 8:tm, :]
        u1 = jnp.where(row == 0, prev[7:8, :], pltpu.roll(u, 1, axis=0))
        u2 = jnp.where(row == 0, prev[6:7, :], jnp.where(row == 1, prev[7:8, :], pltpu.roll(u, 2, axis=0)))
        w = cw_ref[:, cols]
        y = w[0:1, :] * u2
        y = y + w[1:2, :] * u1
        y = y + w[2:3, :] * u
        ov_ref[:, cols] = (conv_proj(1, j) * y).astype(BF16)


def _in_proj(x2, g, w_t, conv_w, batch, seq_len, *, tm=512):
    m = x2.shape[0]
    gw = N_KV * LANE
    tps = seq_len // tm
    n_slabs = QKV_BLOCK // HEAD_DIM
    kern = functools.partial(_inproj_kernel, q_scale=HEAD_DIM ** -0.5 * math.log2(math.e), tiles_per_seq=tps)
    row = lambda i: (i, 0)

    def resident(shape):
        return pl.BlockSpec(shape, lambda i: (0, 0), pipeline_mode=pl.Buffered(1))

    return pl.pallas_call(
        kern,
        grid=(m // tm,),
        in_specs=[
            pl.BlockSpec((tm, D_MODEL), row),
            resident((1, D_MODEL)),
            pl.BlockSpec(memory_space=pl.ANY),
            resident(conv_w.shape),
        ],
        out_specs=[
            pl.BlockSpec((tm, ATTN_WIDTH), row),
            pl.BlockSpec((tm, QKV_BLOCK), row),
            pl.BlockSpec((tm, QKV_BLOCK), row),
            pl.BlockSpec((None, n_slabs, HEAD_DIM + ONES_ROWS, tm), lambda i: (i // tps, 0, 0, i % tps)),
            pl.BlockSpec((tm, gw), row),
            pl.BlockSpec((tm, CONV_WIDTH), row),
        ],
        out_shape=[
            jax.ShapeDtypeStruct((m, ATTN_WIDTH), BF16),
            jax.ShapeDtypeStruct((m, QKV_BLOCK), BF16),
            jax.ShapeDtypeStruct((m, QKV_BLOCK), F32),
            jax.ShapeDtypeStruct((batch, n_slabs, HEAD_DIM + ONES_ROWS, seq_len), BF16),
            jax.ShapeDtypeStruct((m, gw), F32),
            jax.ShapeDtypeStruct((m, CONV_WIDTH), BF16),
        ],
        scratch_shapes=[pltpu.VMEM((tm, D_MODEL), BF16),
                        pltpu.VMEM((CONV_WIDTH // CONV_BLOCK, 8, CONV_BLOCK), F32),
                        pltpu.VMEM((QKV_WIDTH, D_MODEL), BF16),
                        pltpu.VMEM((gw, D_MODEL), BF16),
                        pltpu.VMEM((3 * CONV_WIDTH, D_MODEL), BF16),
                        pltpu.VMEM((2, W_CHUNK, D_MODEL), F32),
                        pltpu.SemaphoreType.DMA((2,))],
        compiler_params=_params(1),
        name="in_proj",
    )(x2, g, w_t, conv_w)


def _compress_kernel(x_ref, pe_ref, w1k_ref, w1v_ref, w2k_ref, w2v_ref, o_ref, ot_ref):
    n = x_ref.shape[0] // CMP_STRIDE

    def body(w1_ref, w2_ref):
        a = jnp.zeros((n, CMP_HIDDEN), F32)
        b = jnp.zeros((n, CMP_HIDDEN), F32)
        for l in range(CMP_STRIDE):
            xl = x_ref[pl.ds(l, n, stride=CMP_STRIDE), :]
            xa = (xl + pe_ref[l:l + 1, :]).astype(BF16)
            xb = (xl + pe_ref[CMP_STRIDE + l:CMP_STRIDE + l + 1, :]).astype(BF16)
            a = a + jnp.dot(xa, w1_ref[l].astype(BF16), preferred_element_type=F32)
            b = b + jnp.dot(xb, w1_ref[CMP_STRIDE + l].astype(BF16), preferred_element_type=F32)
        pre = a + pltpu.roll(b, n - 1, axis=0)
        hid = pre * jax.nn.sigmoid(pre)
        out = jnp.dot(hid.astype(BF16), w2_ref[...].astype(BF16), preferred_element_type=F32)
        row = lax.broadcasted_iota(jnp.int32, out.shape, 0)
        out = jnp.where(row < n - 1, out, 0.0)
        o_ref[...] = out.astype(BF16)
        ot_ref[...] = out.T.astype(BF16)

    is_value = pl.program_id(1) >= N_KV
    pl.when(jnp.logical_not(is_value))(functools.partial(body, w1k_ref, w2k_ref))
    pl.when(is_value)(functools.partial(body, w1v_ref, w2v_ref))


def _compress(c_in, pe, w1_kv, w2_kv, batch, seq_len):
    n_slabs = c_in.shape[1] // HEAD_DIM
    n_chunks = seq_len // CMP_STRIDE

    def resident(shape):
        return pl.BlockSpec(shape, lambda i, j: (0,) * len(shape), pipeline_mode=pl.Buffered(1))

    return pl.pallas_call(
        _compress_kernel,
        grid=(batch, n_slabs),
        in_specs=[
            pl.BlockSpec((seq_len, HEAD_DIM), lambda i, j: (i, j)),
            resident((CMP_LEN, HEAD_DIM)),
            resident(w1_kv[0].shape), resident(w1_kv[1].shape),
            resident(w2_kv[0].shape), resident(w2_kv[1].shape),
        ],
        out_specs=[
            pl.BlockSpec((None, None, n_chunks, HEAD_DIM), lambda i, j: (i, j, 0, 0)),
            pl.BlockSpec((None, None, HEAD_DIM, n_chunks), lambda i, j: (i, j, 0, 0)),
        ],
        out_shape=[
            jax.ShapeDtypeStruct((batch, n_slabs, n_chunks, HEAD_DIM), BF16),
            jax.ShapeDtypeStruct((batch, n_slabs, HEAD_DIM, n_chunks), BF16),
        ],
        compiler_params=_params(2),
        name="compress",
    )(c_in, pe, *w1_kv, *w2_kv)


TAB_DIAG, TAB_SUB, TAB_FAR, TAB_MASK = 0, 1, 2, 3
BAND_BELOW = 8
ONES_ROWS = 16


N_NSA_INPUTS = 11
BIAS_LEN = 1024
FAR_UNROLL = 4
NSA_TILES_PER_STEP = 2


def _nsa_kernel(*refs, tiles_per_step, **static):
    for u in range(tiles_per_step):
        _nsa_tile(refs, u, tiles_per_step, **static)


def _nsa_tile(refs, u, tiles_per_step, *, tq, n_sel, n_cast):
    (q_ref, kc_ref, vct_ref, ks_ref, vst_ref, kw_ref, vwt_ref, gate_ref, fvec_ref,
     ovl_ref, et_ref) = refs[:N_NSA_INPUTS]
    cast_in = refs[N_NSA_INPUTS:N_NSA_INPUTS + n_cast]
    o_ref = refs[N_NSA_INPUTS + n_cast]
    cast_out = refs[N_NSA_INPUTS + n_cast + 1:N_NSA_INPUTS + 2 * n_cast + 1]
    (kaug_ref, tab_ref, band_ref, sc_ref, s0_ref, s1_ref, m_ref, mt_ref,
     acc_ref) = refs[N_NSA_INPUTS + 2 * n_cast + 1:]
    i = pl.program_id(2) * tiles_per_step + u
    rows = slice(u * tq, (u + 1) * tq)

    tk = tq
    mcols = HPG * tq
    nw = WINDOW // tk
    cpt = tq // CMP_STRIDE
    band_rows = BAND_BELOW + cpt

    def group_start(fn):
        if u == 0:
            pl.when(i == 0)(fn)

    @group_start
    def _():
        kaug_ref[:, 0:HEAD_DIM] = ks_ref[...]
        kaug_ref[:, HEAD_DIM:2 * HEAD_DIM] = et_ref[...]

    @group_start
    def _():
        c = lax.broadcasted_iota(jnp.int32, (tk, tq), 0)
        r = lax.broadcasted_iota(jnp.int32, (tk, tq), 1)
        far = jnp.where(r < c, 0.0, NEG)
        lane = lax.broadcasted_iota(jnp.int32, (1, BIAS_LEN), 1)
        for h in range(HPG):
            cols = slice(h * tq, (h + 1) * tq)
            f = fvec_ref[h]
            f_diag = jnp.where(lane < tq, f, NEG)
            x = pltpu.roll(jnp.broadcast_to(f_diag, (tk, BIAS_LEN)), 0, 1, stride=1, stride_axis=0)
            tab_ref[TAB_DIAG, :, cols] = x[:, 0:tq]
            x = pltpu.roll(jnp.broadcast_to(f, (tk, BIAS_LEN)), 0, 1, stride=1, stride_axis=0)
            tab_ref[TAB_SUB, :, cols] = x[:, tq:2 * tq]
            tab_ref[TAB_FAR, :, cols] = far
            tab_ref[TAB_MASK, :, cols] = jnp.full((tk, tq), NEG, F32)
            for v, first in enumerate((0, -BAND_BELOW)):
                shift = (CMP_STRIDE * first + CMP_LEN - 1) % BIAS_LEN
                f_shift = pltpu.roll(f, shift, 1)
                x = pltpu.roll(jnp.broadcast_to(f_shift, (band_rows, BIAS_LEN)), 0, 1,
                               stride=CMP_STRIDE, stride_axis=0)
                band_ref[v, :, cols] = x[:, 0:tq]

    q = q_ref[rows, :]
    qs = jnp.concatenate([q[:, h * HEAD_DIM:(h + 1) * HEAD_DIM] for h in range(HPG)], axis=0)

    s_refs = (s0_ref, s1_ref)

    def init():
        m_ref[...] = jnp.full((1, mcols), NEG, F32)
        acc_ref[...] = jnp.zeros(acc_ref.shape, F32)

    def keys(kt):
        return pl.ds(pl.multiple_of(kt * tk, tk), tk)

    def qk(qmat, k_ref, kt, kind, buf):
        s = lax.dot_general(k_ref[keys(kt), :], qmat, _DN_T, preferred_element_type=F32)
        if kind is not None:
            s = s + tab_ref[kind]
        s_refs[buf][...] = s
        return jnp.max(s, axis=0, keepdims=True)

    def process(vt_ref, kt, buf, m_tile):
        m_prev = m_ref[...]
        m_next = jnp.maximum(m_prev, m_tile)
        alpha = jnp.exp2(m_prev - m_next)
        p = jnp.exp2(s_refs[buf][...] - m_next).astype(BF16)
        acc_ref[...] = alpha * acc_ref[...] + jnp.dot(vt_ref[:, keys(kt)], p, preferred_element_type=F32)
        m_ref[...] = m_next

    def finish():
        return acc_ref[0:HEAD_DIM, :] * (1.0 / acc_ref[HEAD_DIM:HEAD_DIM + 1, :])

    assert nw == 2, "window tiles are i, i-1 (previous-tile table) and i-nw (window-edge table)"
    init()
    mt_w0 = qk(qs, kw_ref, i, TAB_DIAG, 0)

    if u == 0:
        for src, dst in zip(cast_in, cast_out):
            dst[...] = src[...].astype(BF16)

    raw = lax.dot_general(kc_ref[...], qs, _DN_T, preferred_element_type=F32)
    crow = lax.broadcasted_iota(jnp.int32, raw.shape, 0)
    sc_ref[...] = jnp.where(crow < cpt * (i + 1), raw, NEG)
    band = pl.ds(pl.multiple_of(jnp.maximum(cpt * i - BAND_BELOW, 0), 8), band_rows)
    sc_ref[band, :] += band_ref[jnp.minimum(i, 1)]

    mt_w1 = qk(qs, kw_ref, jnp.maximum(i - 1, 0), jnp.where(i >= 1, TAB_SUB, TAB_MASK), 1)
    process(vwt_ref, i, 0, mt_w0)

    sc = sc_ref[...]
    mc = jnp.maximum(jnp.max(sc, axis=0, keepdims=True), HALF_NEG)
    pc = jnp.exp2(sc - mc)
    lc = jnp.sum(pc, axis=0, keepdims=True)
    pc = pc * jnp.where(lc > 0.0, 1.0 / lc, 0.0)
    o_c = jnp.dot(vct_ref[...], pc.astype(BF16), preferred_element_type=F32)

    mt_w2 = qk(qs, kw_ref, jnp.maximum(i - nw, 0), jnp.where(i >= nw, TAB_FAR, TAB_MASK), 0)
    process(vwt_ref, jnp.maximum(i - 1, 0), 1, mt_w1)

    ps = pc[:, 0:tq] + pc[:, tq:2 * tq] + pc[:, 2 * tq:3 * tq] + pc[:, 3 * tq:4 * tq]
    hi = ps.astype(BF16)
    r1 = ps - hi.astype(F32)
    mid = r1.astype(BF16)
    lo = (r1 - mid.astype(F32)).astype(BF16)
    ovl = ovl_ref[...]
    imp = (jnp.dot(ovl, hi, preferred_element_type=F32) + jnp.dot(ovl, mid, preferred_element_type=F32)
           + jnp.dot(ovl, lo, preferred_element_type=F32))
    jj = lax.broadcasted_iota(jnp.int32, (n_sel, tq), 0)
    tt = i * tq + lax.broadcasted_iota(jnp.int32, (n_sel, tq), 1)
    cur = tt >> int(math.log2(SEL_LEN))
    forced = (jj == 0) | (jj == cur) | (jj == cur - 1)
    imp = jnp.where(forced, FORCE, imp)
    imp = jnp.where(jj * SEL_LEN <= tt, imp, NEG)
    sub = 8
    ranks = []
    for j0 in range(0, n_sel, sub):
        blk = imp[j0:j0 + sub, :]
        jl = j0 + lax.broadcasted_iota(jnp.int32, blk.shape, 0)
        cnt = jnp.zeros(blk.shape, jnp.int32)
        for b in range(n_sel):
            row = imp[b:b + 1, :]
            if b < j0:
                cnt = cnt + jnp.where(row >= blk, 1, 0)
            elif b >= j0 + sub:
                cnt = cnt + jnp.where(row > blk, 1, 0)
            else:
                cnt = cnt + jnp.where(row > blk, 1, jnp.where(row == blk, jnp.where(jl > b, 1, 0), 0))
        ranks.append(cnt)
    rank = jnp.concatenate(ranks, axis=0)
    selb_t = jnp.where(rank < SEL_TOPK, 0.0, NEG)

    process(vwt_ref, jnp.maximum(i - nw, 0), 0, mt_w2)
    o_w = finish()

    selb = jnp.concatenate([selb_t, jnp.zeros((LANE - n_sel, tq), F32)], axis=0).T
    selb = selb.astype(BF16)
    qa = jnp.concatenate([qs, jnp.concatenate([selb] * HPG, axis=0)], axis=1)

    init()

    @pl.when(i == 0)
    def _():
        process(vst_ref, 0, 0, qk(qa, kaug_ref, 0, TAB_DIAG, 0))

    @pl.when(i >= 1)
    def _():
        mt0 = qk(qa, kaug_ref, i, TAB_DIAG, 0)
        mt1 = qk(qa, kaug_ref, i - 1, TAB_SUB, 1)
        process(vst_ref, i, 0, mt0)
        mt0 = qk(qa, kaug_ref, jnp.maximum(i - 2, 0), None, 0)
        process(vst_ref, i - 1, 1, mt1)
        n_far = i - 1
        rem = n_far % FAR_UNROLL

        def far_tiles(a, n, mt0):
            for t in range(n):
                nxt = qk(qa, kaug_ref, jnp.maximum(a - t - 1, 0), None, (t + 1) % 2)
                process(vst_ref, a - t, t % 2, mt0)
                mt0 = nxt
            return mt0

        mt0 = lax.fori_loop(0, n_far // FAR_UNROLL,
                            lambda p, mt: far_tiles(i - 2 - FAR_UNROLL * p, FAR_UNROLL, mt), mt0)

        @pl.when(rem >= 2)
        def _():
            mt_ref[...] = far_tiles(rem - 1, 2, mt0)

        @pl.when(rem == 1)
        def _():
            process(vst_ref, 0, 0, mt0)

        @pl.when(rem == 3)
        def _():
            process(vst_ref, 0, 0, mt_ref[...])

    o_s = finish()

    gt = gate_ref[rows, :].T
    for h in range(HPG):
        sl = slice(h * tq, (h + 1) * tq)
        o = gt[3 * h:3 * h + 1, :] * o_c[:, sl]
        o = o + gt[3 * h + 1:3 * h + 2, :] * o_s[:, sl]
        o = o + gt[3 * h + 2:3 * h + 3, :] * o_w[:, sl]
        o_ref[rows, h * HEAD_DIM:(h + 1) * HEAD_DIM] = o.T.astype(BF16)


def _nsa(q, k_all, vt_all, cmp, cmp_t, gates, fvec, ovl, e_t, cast_weights, batch, seq_len, *, tq):
    n_t = seq_len // (tq * NSA_TILES_PER_STEP)
    n_steps = batch * N_KV * n_t
    ts = tq * NSA_TILES_PER_STEP
    step_row = lambda b, g, i: ((b * N_KV + g) * n_t + i, 0)
    cast_specs = [pl.BlockSpec((w.shape[0] // n_steps, w.shape[1]), step_row) for w in cast_weights]
    n_sel = seq_len // SEL_LEN
    n_chunks = cmp.shape[2]
    mcols = HPG * tq
    band_rows = BAND_BELOW + tq // CMP_STRIDE
    qw = HPG * HEAD_DIM

    def vt_spec(first):
        return pl.BlockSpec((None, None, HEAD_DIM + ONES_ROWS, seq_len), lambda b, g, i: (b, first + g, 0, 0))

    kern = functools.partial(_nsa_kernel, tiles_per_step=NSA_TILES_PER_STEP, tq=tq, n_sel=n_sel,
                             n_cast=len(cast_weights))
    o_attn, *cast = pl.pallas_call(
        kern,
        grid=(batch, N_KV, n_t),
        in_specs=[
            pl.BlockSpec((ts, qw), lambda b, g, i: (b * n_t + i, g)),
            pl.BlockSpec((None, None, n_chunks, HEAD_DIM), lambda b, g, i: (b, g, 0, 0)),
            pl.BlockSpec((None, None, HEAD_DIM, n_chunks), lambda b, g, i: (b, N_KV + g, 0, 0)),
            pl.BlockSpec((seq_len, HEAD_DIM), lambda b, g, i: (b, g)),
            vt_spec(0),
            pl.BlockSpec((seq_len, HEAD_DIM), lambda b, g, i: (b, N_KV + g)),
            vt_spec(N_KV),
            pl.BlockSpec((ts, LANE), lambda b, g, i: (b * n_t + i, g)),
            pl.BlockSpec((None, HPG, 1, BIAS_LEN), lambda b, g, i: (g, 0, 0, 0)),
            pl.BlockSpec((n_sel, n_chunks), lambda b, g, i: (0, 0)),
            pl.BlockSpec((seq_len, LANE), lambda b, g, i: (0, 0)),
        ] + cast_specs,
        out_specs=[pl.BlockSpec((ts, qw), lambda b, g, i: (b * n_t + i, g))] + cast_specs,
        out_shape=[jax.ShapeDtypeStruct((batch * seq_len, ATTN_WIDTH), BF16)]
        + [jax.ShapeDtypeStruct(w.shape, BF16) for w in cast_weights],
        scratch_shapes=[
            pltpu.VMEM((seq_len, 2 * HEAD_DIM), BF16),
            pltpu.VMEM((4, tq, mcols), F32),
            pltpu.VMEM((2, band_rows, mcols), F32),
            pltpu.VMEM((n_chunks, mcols), F32),
            pltpu.VMEM((tq, mcols), F32),
            pltpu.VMEM((tq, mcols), F32),
            pltpu.VMEM((1, mcols), F32),
            pltpu.VMEM((1, mcols), F32),
            pltpu.VMEM((HEAD_DIM + ONES_ROWS, mcols), F32),
        ],
        compiler_params=_params(3),
        name="nsa_attention",
    )(q, cmp, cmp_t, k_all, vt_all, k_all, vt_all, gates, fvec, ovl, e_t, *cast_weights)
    return o_attn, cast


def _oproj_kernel(oa_ref, ov_ref, wo_ref, x_ref, g1_ref, g2_ref, x1_ref, h2_ref):
    ka = oa_ref.shape[1]
    tm = oa_ref.shape[0]
    for rows in (slice(k * tm // 4, (k + 1) * tm // 4) for k in range(4)):
        mix = jnp.dot(oa_ref[rows, :], wo_ref[0:ka, :], preferred_element_type=F32)
        mix = mix + jnp.dot(ov_ref[rows, :], wo_ref[ka:, :], preferred_element_type=F32)
        x1 = x_ref[rows, :] + _rms(mix, g1_ref[...])
        x1_ref[rows, :] = x1
        h2_ref[rows, :] = _rms(x1, g2_ref[...]).astype(BF16)


def _oproj(o_attn, o_conv, w_o, x2, g_post, g_pre, *, tm=512):
    m = x2.shape[0]
    ka, kv = o_attn.shape[1], o_conv.shape[1]
    row = lambda i: (i, 0)
    fixed = lambda i: (0, 0)
    return pl.pallas_call(
        _oproj_kernel,
        grid=(m // tm,),
        in_specs=[
            pl.BlockSpec((tm, ka), row),
            pl.BlockSpec((tm, kv), row),
            pl.BlockSpec((ka + kv, D_MODEL), fixed),
            pl.BlockSpec((tm, D_MODEL), row),
            pl.BlockSpec((1, D_MODEL), fixed),
            pl.BlockSpec((1, D_MODEL), fixed),
        ],
        out_specs=[pl.BlockSpec((tm, D_MODEL), row), pl.BlockSpec((tm, D_MODEL), row)],
        out_shape=[jax.ShapeDtypeStruct((m, D_MODEL), F32), jax.ShapeDtypeStruct((m, D_MODEL), BF16)],
        compiler_params=_params(1),
        name="out_proj",
    )(o_attn, o_conv, w_o, x2, g_post, g_pre)


def _ffn_kernel(h_ref, wu_ref, wd_ref, x1_ref, g_ref, o_ref, acc_ref):
    j = pl.program_id(1)
    last = pl.num_programs(1) - 1
    tm = h_ref.shape[0]

    def step(mode):
        a = jnp.dot(h_ref[...], wu_ref[...], preferred_element_type=F32)
        a = jnp.square(jnp.maximum(a, 0.0)).astype(BF16)
        if mode == "first":
            acc_ref[...] = jnp.dot(a, wd_ref[...], preferred_element_type=F32)
        elif mode == "middle":
            acc_ref[...] += jnp.dot(a, wd_ref[...], preferred_element_type=F32)
        else:
            for rows in (slice(0, tm // 2), slice(tm // 2, tm)):
                f = acc_ref[rows, :] + jnp.dot(a[rows, :], wd_ref[...], preferred_element_type=F32)
                o_ref[rows, :] = x1_ref[rows, :] + _rms(f, g_ref[...])

    pl.when(j == 0)(functools.partial(step, "first"))
    pl.when((j > 0) & (j < last))(functools.partial(step, "middle"))
    pl.when(j == last)(functools.partial(step, "last"))


def _ffn(h2, w_up, w_down, x1, g_post, *, tm=512, tf=1024):
    m = h2.shape[0]
    d_ff = w_up.shape[1]
    return pl.pallas_call(
        _ffn_kernel,
        grid=(m // tm, d_ff // tf),
        in_specs=[
            pl.BlockSpec((tm, D_MODEL), lambda i, j: (i, 0)),
            pl.BlockSpec((D_MODEL, tf), lambda i, j: (0, j)),
            pl.BlockSpec((tf, D_MODEL), lambda i, j: (j, 0)),
            pl.BlockSpec((tm, D_MODEL), lambda i, j: (i, 0)),
            pl.BlockSpec((1, D_MODEL), lambda i, j: (0, 0)),
        ],
        out_specs=pl.BlockSpec((tm, D_MODEL), lambda i, j: (i, 0)),
        out_shape=jax.ShapeDtypeStruct((m, D_MODEL), F32),
        scratch_shapes=[pltpu.VMEM((tm, D_MODEL), F32)],
        compiler_params=_params(2),
        name="ffn",
    )(h2, w_up, w_down, x1, g_post)


def _bucket_np(dist):
    n = np.maximum(dist, 0)
    max_exact = N_BUCKETS // 2
    nf = np.maximum(n, 1).astype(np.float32)
    large = max_exact + (np.log(nf / np.float32(max_exact)) / np.float32(math.log(MAX_DIST / max_exact))
                         * np.float32(N_BUCKETS - max_exact)).astype(np.int32)
    large = np.minimum(large, N_BUCKETS - 1)
    return np.where(n < max_exact, n, large)


def _bucket_starts():
    b = _bucket_np(np.arange(4 * MAX_DIST))
    return [int(np.argmax(b == k)) for k in range(N_BUCKETS)]


def _attention_tables(rel_bias, seq_len, tq):
    starts = _bucket_starts()
    assert starts[N_BUCKETS - 1] <= CMP_STRIDE * (BAND_BELOW + 1) - (CMP_LEN - 1)
    assert starts[N_BUCKETS - 1] <= tq and 2 * tq <= BIAS_LEN // 2
    rel = (rel_bias - rel_bias[:, N_BUCKETS - 1:]) * math.log2(math.e)
    d = jnp.arange(BIAS_LEN, dtype=jnp.int32)[None, :]
    fvec = jnp.broadcast_to(rel[:, 0:1], (N_HEADS, BIAS_LEN))
    for k in range(1, N_BUCKETS):
        fvec = jnp.where(d >= starts[k], rel[:, k:k + 1], fvec)
    fvec = jnp.where(d < BIAS_LEN // 2, fvec, NEG).astype(F32).reshape(N_KV, HPG, 1, BIAS_LEN)

    n_chunks = seq_len // CMP_STRIDE
    n_cmp = (seq_len - CMP_LEN) // CMP_STRIDE + 1
    n_sel = seq_len // SEL_LEN
    ci = np.arange(n_chunks)[None, :] * CMP_STRIDE
    sj = np.arange(n_sel)[:, None] * SEL_LEN
    ovl = ((ci < sj + SEL_LEN) & (ci + CMP_LEN > sj) & (np.arange(n_chunks)[None, :] < n_cmp))
    e_t = (np.arange(seq_len)[:, None] // SEL_LEN == np.arange(LANE)[None, :])
    return fvec, jnp.asarray(ovl, BF16), jnp.asarray(e_t, BF16)


def kernel(x, w_in, pe_cmp, w_cmp_k1, w_cmp_k2, w_cmp_v1, w_cmp_v2, conv_w, rel_bias, w_o, w_up, w_down,
           g_pre_mix, g_post_mix, g_pre_ffn, g_post_ffn):
    batch, seq_len, _ = x.shape
    depth = w_in.shape[0]
    tq = 256
    fvec, ovl, e_t = _attention_tables(rel_bias, seq_len, tq)
    x2 = x.reshape(batch * seq_len, D_MODEL)
    for l in range(depth):
        wl = jnp.swapaxes(w_in[l], 0, 1)
        g1 = g_pre_mix[l].reshape(1, D_MODEL)

        q, k_all, c_in, vt_all, gates, o_conv = _in_proj(x2, g1, wl, conv_w[l], batch, seq_len)

        cmp, cmp_t = _compress(c_in, pe_cmp[l], (w_cmp_k1[l], w_cmp_v1[l]), (w_cmp_k2[l], w_cmp_v2[l]),
                               batch, seq_len)

        o_attn, (wo_b, wup_b, wdown_b) = _nsa(q, k_all, vt_all, cmp, cmp_t, gates, fvec, ovl, e_t,
                                              [w_o[l], w_up[l], w_down[l]], batch, seq_len, tq=tq)

        x1, h2 = _oproj(o_attn, o_conv, wo_b, x2,
                        g_post_mix[l].reshape(1, D_MODEL), g_pre_ffn[l].reshape(1, D_MODEL))
        x2 = _ffn(h2, wup_b, wdown_b, x1, g_post_ffn[l].reshape(1, D_MODEL))
    return x2.reshape(batch, seq_len, D_MODEL)
```

```python
import functools
import math

import numpy as np
import jax
import jax.numpy as jnp
from jax import lax
from jax.experimental import pallas as pl
from jax.experimental.pallas import tpu as pltpu

F32 = jnp.float32
BF16 = jnp.bfloat16

D_MODEL = 2048
N_HEADS = 8
N_KV = 2
HPG = N_HEADS // N_KV
HEAD_DIM = 128
ATTN_WIDTH = N_HEADS * HEAD_DIM
KV_WIDTH = N_KV * HEAD_DIM
CONV_WIDTH = D_MODEL - ATTN_WIDTH
CONV_K = 3
N_BRANCH = 3
CMP_LEN = 32
CMP_STRIDE = 16
CMP_HIDDEN = 256
SEL_LEN = 64
SEL_TOPK = 16
WINDOW = 512
N_BUCKETS = 32
MAX_DIST = 128
EPS = 1e-6
NEG = -1e30
HALF_NEG = -5e29
FORCE = 1e9

QKV_WIDTH = ATTN_WIDTH + 6 * KV_WIDTH
GATE_OFF = QKV_WIDTH
CONV_OFF = QKV_WIDTH + N_HEADS * N_BRANCH
LANE = 128
VMEM_LIMIT = 56 * 1024 * 1024

_DN_T = (((1,), (1,)), ((), ()))


def _rms(x, g):
    ms = jnp.mean(x * x, axis=-1, keepdims=True)
    return x * lax.rsqrt(ms + EPS) * g


def _params(n_axes):
    return pltpu.CompilerParams(dimension_semantics=("arbitrary",) * n_axes, vmem_limit_bytes=VMEM_LIMIT)


QKV_BLOCK = 2 * KV_WIDTH
SLAB_KC, SLAB_VC, SLAB_KS, SLAB_VS, SLAB_KW, SLAB_VW = range(6)
CONV_BLOCK = 512
W_CHUNK = 256
SUBLANE = 8


SLAB_GROUPS = ((SLAB_KS, SLAB_KC, SLAB_VS), (SLAB_KW, SLAB_VC, SLAB_VW))
CONV_H, CONV_B, CONV_C = range(3)


def _weight_loader(wt_hbm, w_ref, wg_ref, wconv_ref, stage_ref, sem):
    n_gate = N_HEADS * N_BRANCH
    per_group = HPG * N_BRANCH

    def rows_chunk(src, dst, row):
        def store(v):
            dst[row:row + W_CHUNK, :] = v.astype(BF16)
        return (src + row, W_CHUNK, dst, row, row + W_CHUNK, store)

    def store_gates(v):
        wg_ref[...] = jnp.zeros(wg_ref.shape, BF16)
        pad = jnp.zeros((2 * SUBLANE - per_group, v.shape[1]), F32)
        for g in range(N_KV):
            rows = jnp.concatenate([v[g * per_group:(g + 1) * per_group, :], pad], axis=0)
            wg_ref[g * LANE:g * LANE + 2 * SUBLANE, :] = rows.astype(BF16)

    chunks = [(GATE_OFF, n_gate, wg_ref, 0, wg_ref.shape[0], store_gates)]
    chunks += [rows_chunk(0, w_ref, r) for r in range(0, ATTN_WIDTH, W_CHUNK)]
    for group in SLAB_GROUPS:
        chunks += [rows_chunk(0, w_ref, ATTN_WIDTH + s * KV_WIDTH) for s in group]
    for j in range(CONV_WIDTH // CONV_BLOCK):
        for which in (CONV_C, CONV_H, CONV_B):
            first = which * CONV_WIDTH + j * CONV_BLOCK
            chunks += [rows_chunk(CONV_OFF, wconv_ref, r) for r in range(first, first + CONV_BLOCK, W_CHUNK)]
    assert KV_WIDTH == W_CHUNK and CONV_BLOCK % W_CHUNK == 0

    def copy(k):
        src, n = chunks[k][:2]
        slot = k % 2
        return pltpu.make_async_copy(wt_hbm.at[pl.ds(src, n), :], stage_ref.at[slot, pl.ds(0, n), :], sem.at[slot])

    done = []
    copy(0).start()

    def advance():
        k = len(done)
        if k + 1 < len(chunks):
            copy(k + 1).start()
        copy(k).wait()
        chunks[k][5](stage_ref[k % 2, 0:chunks[k][1], :])
        done.append(k)

    def need(dst, lo, hi):
        def missing():
            return any(c[2] is dst and c[3] < hi and c[4] > lo and k not in done for k, c in enumerate(chunks))
        while missing():
            advance()

    return need


def _inproj_kernel(x_ref, g_ref, wt_hbm, cw_ref, q_ref, k_ref, c_ref, vt_ref, gate_ref, ov_ref,
                   h_ref, carry_ref, w_ref, wg_ref, wconv_ref, stage_ref, sem, *, q_scale, tiles_per_seq):
    i = pl.program_id(0)
    tm = x_ref.shape[0]

    def step(first):
        need = _weight_loader(wt_hbm, w_ref, wg_ref, wconv_ref, stage_ref, sem) if first else None

        def proj(w, lo, width):
            if first:
                need(w, lo, lo + width)
            return lax.dot_general(h_ref[...], w[lo:lo + width, :], _DN_T, preferred_element_type=F32)

        h_ref[...] = _rms(x_ref[...], g_ref[...]).astype(BF16)
        gate_ref[...] = jax.nn.sigmoid(proj(wg_ref, 0, wg_ref.shape[0]))

        for j in range(ATTN_WIDTH // QKV_BLOCK):
            q_ref[:, j * QKV_BLOCK:(j + 1) * QKV_BLOCK] = (
                proj(w_ref, j * QKV_BLOCK, QKV_BLOCK) * q_scale).astype(BF16)

        def slab(which):
            return proj(w_ref, ATTN_WIDTH + which * KV_WIDTH, KV_WIDTH)

        ones = jnp.where(lax.broadcasted_iota(jnp.int32, (ONES_ROWS, tm), 0) == 0, 1.0, 0.0).astype(BF16)
        for pair, (k_slab, c_slab, v_slab) in enumerate(SLAB_GROUPS):
            cols = slice(pair * KV_WIDTH, (pair + 1) * KV_WIDTH)
            k_ref[:, cols] = slab(k_slab).astype(BF16)
            c_ref[:, cols] = slab(c_slab)
            v = slab(v_slab)
            for g in range(N_KV):
                s = pair * N_KV + g
                vt_ref[s, 0:HEAD_DIM, :] = v[:, g * HEAD_DIM:(g + 1) * HEAD_DIM].T.astype(BF16)
                vt_ref[s, HEAD_DIM:HEAD_DIM + ONES_ROWS, :] = ones

        def conv_proj(which, j):
            return proj(wconv_ref, which * CONV_WIDTH + j * CONV_BLOCK, CONV_BLOCK)

        row = lax.broadcasted_iota(jnp.int32, (tm, CONV_BLOCK), 0)
        for j in range(CONV_WIDTH // CONV_BLOCK):
            cols = slice(j * CONV_BLOCK, (j + 1) * CONV_BLOCK)
            u = conv_proj(CONV_C, j) * conv_proj(CONV_H, j)
            prev = carry_ref[j]
            prev = jnp.where(i % tiles_per_seq == 0, 0.0, prev)
            carry_ref[j] = u[tm - 8:tm, :]
            u1 = jnp.where(row == 0, prev[7:8, :], pltpu.roll(u, 1, axis=0))
            u2 = jnp.where(row == 0, prev[6:7, :], jnp.where(row == 1, prev[7:8, :], pltpu.roll(u, 2, axis=0)))
            w = cw_ref[:, cols]
            y = w[0:1, :] * u2
            y = y + w[1:2, :] * u1
            y = y + w[2:3, :] * u
            ov_ref[:, cols] = (conv_proj(CONV_B, j) * y).astype(BF16)

    pl.when(i == 0)(functools.partial(step, True))
    pl.when(i > 0)(functools.partial(step, False))


def _in_proj(x2, g, w_t, conv_w, batch, seq_len, *, tm=512):
    m = x2.shape[0]
    gw = N_KV * LANE
    tps = seq_len // tm
    n_slabs = QKV_BLOCK // HEAD_DIM
    kern = functools.partial(_inproj_kernel, q_scale=HEAD_DIM ** -0.5 * math.log2(math.e), tiles_per_seq=tps)
    row = lambda i: (i, 0)

    def resident(shape):
        return pl.BlockSpec(shape, lambda i: (0, 0), pipeline_mode=pl.Buffered(1))

    return pl.pallas_call(
        kern,
        grid=(m // tm,),
        in_specs=[
            pl.BlockSpec((tm, D_MODEL), row),
            resident((1, D_MODEL)),
            pl.BlockSpec(memory_space=pl.ANY),
            resident(conv_w.shape),
        ],
        out_specs=[
            pl.BlockSpec((tm, ATTN_WIDTH), row),
            pl.BlockSpec((tm, QKV_BLOCK), row),
            pl.BlockSpec((tm, QKV_BLOCK), row),
            pl.BlockSpec((None, n_slabs, HEAD_DIM + ONES_ROWS, tm), lambda i: (i // tps, 0, 0, i % tps)),
            pl.BlockSpec((tm, gw), row),
            pl.BlockSpec((tm, CONV_WIDTH), row),
        ],
        out_shape=[
            jax.ShapeDtypeStruct((m, ATTN_WIDTH), BF16),
            jax.ShapeDtypeStruct((m, QKV_BLOCK), BF16),
            jax.ShapeDtypeStruct((m, QKV_BLOCK), F32),
            jax.ShapeDtypeStruct((batch, n_slabs, HEAD_DIM + ONES_ROWS, seq_len), BF16),
            jax.ShapeDtypeStruct((m, gw), F32),
            jax.ShapeDtypeStruct((m, CONV_WIDTH), BF16),
        ],
        scratch_shapes=[pltpu.VMEM((tm, D_MODEL), BF16),
                        pltpu.VMEM((CONV_WIDTH // CONV_BLOCK, 8, CONV_BLOCK), F32),
                        pltpu.VMEM((QKV_WIDTH, D_MODEL), BF16),
                        pltpu.VMEM((gw, D_MODEL), BF16),
                        pltpu.VMEM((3 * CONV_WIDTH, D_MODEL), BF16),
                        pltpu.VMEM((2, W_CHUNK, D_MODEL), F32),
                        pltpu.SemaphoreType.DMA((2,))],
        compiler_params=_params(1),
        name="in_proj",
    )(x2, g, w_t, conv_w)


def _compress_kernel(x_ref, pe_ref, w1k_ref, w1v_ref, w2k_ref, w2v_ref, o_ref, ot_ref):
    n = x_ref.shape[0] // CMP_STRIDE

    def body(w1_ref, w2_ref):
        a = jnp.zeros((n, CMP_HIDDEN), F32)
        b = jnp.zeros((n, CMP_HIDDEN), F32)
        for l in range(CMP_STRIDE):
            xl = x_ref[pl.ds(l, n, stride=CMP_STRIDE), :]
            xa = (xl + pe_ref[l:l + 1, :]).astype(BF16)
            xb = (xl + pe_ref[CMP_STRIDE + l:CMP_STRIDE + l + 1, :]).astype(BF16)
            a = a + jnp.dot(xa, w1_ref[l].astype(BF16), preferred_element_type=F32)
            b = b + jnp.dot(xb, w1_ref[CMP_STRIDE + l].astype(BF16), preferred_element_type=F32)
        pre = a + pltpu.roll(b, n - 1, axis=0)
        hid = pre * jax.nn.sigmoid(pre)
        out = jnp.dot(hid.astype(BF16), w2_ref[...].astype(BF16), preferred_element_type=F32)
        row = lax.broadcasted_iota(jnp.int32, out.shape, 0)
        out = jnp.where(row < n - 1, out, 0.0)
        o_ref[...] = out.astype(BF16)
        ot_ref[...] = out.T.astype(BF16)

    is_value = pl.program_id(1) >= N_KV
    pl.when(jnp.logical_not(is_value))(functools.partial(body, w1k_ref, w2k_ref))
    pl.when(is_value)(functools.partial(body, w1v_ref, w2v_ref))


def _compress(c_in, pe, w1_kv, w2_kv, batch, seq_len):
    n_slabs = c_in.shape[1] // HEAD_DIM
    n_chunks = seq_len // CMP_STRIDE

    def resident(shape):
        return pl.BlockSpec(shape, lambda i, j: (0,) * len(shape), pipeline_mode=pl.Buffered(1))

    return pl.pallas_call(
        _compress_kernel,
        grid=(batch, n_slabs),
        in_specs=[
            pl.BlockSpec((seq_len, HEAD_DIM), lambda i, j: (i, j)),
            resident((CMP_LEN, HEAD_DIM)),
            resident(w1_kv[0].shape), resident(w1_kv[1].shape),
            resident(w2_kv[0].shape), resident(w2_kv[1].shape),
        ],
        out_specs=[
            pl.BlockSpec((None, None, n_chunks, HEAD_DIM), lambda i, j: (i, j, 0, 0)),
            pl.BlockSpec((None, None, HEAD_DIM, n_chunks), lambda i, j: (i, j, 0, 0)),
        ],
        out_shape=[
            jax.ShapeDtypeStruct((batch, n_slabs, n_chunks, HEAD_DIM), BF16),
            jax.ShapeDtypeStruct((batch, n_slabs, HEAD_DIM, n_chunks), BF16),
        ],
        compiler_params=_params(2),
        name="compress",
    )(c_in, pe, *w1_kv, *w2_kv)


TAB_DIAG, TAB_SUB, TAB_FAR, TAB_MASK = 0, 1, 2, 3
BAND_BELOW = 8
ONES_ROWS = 16


N_NSA_INPUTS = 11
BIAS_LEN = 1024
FAR_UNROLL = 4
NSA_TILES_PER_STEP = 2


def _nsa_kernel(*refs, tiles_per_step, **static):
    for u in range(tiles_per_step):
        _nsa_tile(refs, u, tiles_per_step, **static)


def _nsa_tile(refs, u, tiles_per_step, *, tq, n_sel, n_cast):
    (q_ref, kc_ref, vct_ref, ks_ref, vst_ref, kw_ref, vwt_ref, gate_ref, fvec_ref,
     ovl_ref, et_ref) = refs[:N_NSA_INPUTS]
    cast_in = refs[N_NSA_INPUTS:N_NSA_INPUTS + n_cast]
    o_ref = refs[N_NSA_INPUTS + n_cast]
    cast_out = refs[N_NSA_INPUTS + n_cast + 1:N_NSA_INPUTS + 2 * n_cast + 1]
    (kaug_ref, tab_ref, band_ref, sc_ref, s0_ref, s1_ref, m_ref, mt_ref,
     acc_ref) = refs[N_NSA_INPUTS + 2 * n_cast + 1:]
    i = pl.program_id(2) * tiles_per_step + u
    rows = slice(u * tq, (u + 1) * tq)

    tk = tq
    mcols = HPG * tq
    nw = WINDOW // tk
    cpt = tq // CMP_STRIDE
    band_rows = BAND_BELOW + cpt

    def group_start(fn):
        if u == 0:
            pl.when(i == 0)(fn)

    @group_start
    def _():
        kaug_ref[:, 0:HEAD_DIM] = ks_ref[...]
        kaug_ref[:, HEAD_DIM:2 * HEAD_DIM] = et_ref[...]

    @group_start
    def _():
        c = lax.broadcasted_iota(jnp.int32, (tk, tq), 0)
        r = lax.broadcasted_iota(jnp.int32, (tk, tq), 1)
        far = jnp.where(r < c, 0.0, NEG)
        lane = lax.broadcasted_iota(jnp.int32, (1, BIAS_LEN), 1)
        for h in range(HPG):
            cols = slice(h * tq, (h + 1) * tq)
            f = fvec_ref[h]
            f_diag = jnp.where(lane < tq, f, NEG)
            x = pltpu.roll(jnp.broadcast_to(f_diag, (tk, BIAS_LEN)), 0, 1, stride=1, stride_axis=0)
            tab_ref[TAB_DIAG, :, cols] = x[:, 0:tq]
            x = pltpu.roll(jnp.broadcast_to(f, (tk, BIAS_LEN)), 0, 1, stride=1, stride_axis=0)
            tab_ref[TAB_SUB, :, cols] = x[:, tq:2 * tq]
            tab_ref[TAB_FAR, :, cols] = far
            tab_ref[TAB_MASK, :, cols] = jnp.full((tk, tq), NEG, F32)
            for v, first in enumerate((0, -BAND_BELOW)):
                shift = (CMP_STRIDE * first + CMP_LEN - 1) % BIAS_LEN
                f_shift = pltpu.roll(f, shift, 1)
                x = pltpu.roll(jnp.broadcast_to(f_shift, (band_rows, BIAS_LEN)), 0, 1,
                               stride=CMP_STRIDE, stride_axis=0)
                band_ref[v, :, cols] = x[:, 0:tq]

    q = q_ref[rows, :]
    qs = jnp.concatenate([q[:, h * HEAD_DIM:(h + 1) * HEAD_DIM] for h in range(HPG)], axis=0)

    s_refs = (s0_ref, s1_ref)

    def init():
        m_ref[...] = jnp.full((1, mcols), NEG, F32)
        acc_ref[...] = jnp.zeros(acc_ref.shape, F32)

    def keys(kt):
        return pl.ds(pl.multiple_of(kt * tk, tk), tk)

    def qk(qmat, k_ref, kt, kind, buf):
        s = lax.dot_general(k_ref[keys(kt), :], qmat, _DN_T, preferred_element_type=F32)
        if kind is not None:
            s = s + tab_ref[kind]
        s_refs[buf][...] = s
        return jnp.max(s, axis=0, keepdims=True)

    def process(vt_ref, kt, buf, m_tile):
        m_prev = m_ref[...]
        m_next = jnp.maximum(m_prev, m_tile)
        alpha = jnp.exp2(m_prev - m_next)
        p = jnp.exp2(s_refs[buf][...] - m_next).astype(BF16)
        acc_ref[...] = alpha * acc_ref[...] + jnp.dot(vt_ref[:, keys(kt)], p, preferred_element_type=F32)
        m_ref[...] = m_next

    def finish():
        return acc_ref[0:HEAD_DIM, :] * (1.0 / acc_ref[HEAD_DIM:HEAD_DIM + 1, :])

    assert nw == 2, "window tiles are i, i-1 (previous-tile table) and i-nw (window-edge table)"
    init()
    mt_w0 = qk(qs, kw_ref, i, TAB_DIAG, 0)

    if u == 0:
        for src, dst in zip(cast_in, cast_out):
            dst[...] = src[...].astype(BF16)

    raw = lax.dot_general(kc_ref[...], qs, _DN_T, preferred_element_type=F32)
    crow = lax.broadcasted_iota(jnp.int32, raw.shape, 0)
    sc_ref[...] = jnp.where(crow < cpt * (i + 1), raw, NEG)
    band = pl.ds(pl.multiple_of(jnp.maximum(cpt * i - BAND_BELOW, 0), 8), band_rows)
    sc_ref[band, :] += band_ref[jnp.minimum(i, 1)]

    mt_w1 = qk(qs, kw_ref, jnp.maximum(i - 1, 0), jnp.where(i >= 1, TAB_SUB, TAB_MASK), 1)
    process(vwt_ref, i, 0, mt_w0)

    sc = sc_ref[...]
    mc = jnp.maximum(jnp.max(sc, axis=0, keepdims=True), HALF_NEG)
    pc = jnp.exp2(sc - mc)
    lc = jnp.sum(pc, axis=0, keepdims=True)
    pc = pc * jnp.where(lc > 0.0, 1.0 / lc, 0.0)
    o_c = jnp.dot(vct_ref[...], pc.astype(BF16), preferred_element_type=F32)

    mt_w2 = qk(qs, kw_ref, jnp.maximum(i - nw, 0), jnp.where(i >= nw, TAB_FAR, TAB_MASK), 0)
    process(vwt_ref, jnp.maximum(i - 1, 0), 1, mt_w1)

    ps = pc[:, 0:tq] + pc[:, tq:2 * tq] + pc[:, 2 * tq:3 * tq] + pc[:, 3 * tq:4 * tq]
    hi = ps.astype(BF16)
    r1 = ps - hi.astype(F32)
    mid = r1.astype(BF16)
    lo = (r1 - mid.astype(F32)).astype(BF16)
    ovl = ovl_ref[...]
    imp = (jnp.dot(ovl, hi, preferred_element_type=F32) + jnp.dot(ovl, mid, preferred_element_type=F32)
           + jnp.dot(ovl, lo, preferred_element_type=F32))
    jj = lax.broadcasted_iota(jnp.int32, (n_sel, tq), 0)
    tt = i * tq + lax.broadcasted_iota(jnp.int32, (n_sel, tq), 1)
    cur = tt >> int(math.log2(SEL_LEN))
    forced = (jj == 0) | (jj == cur) | (jj == cur - 1)
    imp = jnp.where(forced, FORCE, imp)
    imp = jnp.where(jj * SEL_LEN <= tt, imp, NEG)
    sub = 8
    ranks = []
    for j0 in range(0, n_sel, sub):
        blk = imp[j0:j0 + sub, :]
        jl = j0 + lax.broadcasted_iota(jnp.int32, blk.shape, 0)
        cnt = jnp.zeros(blk.shape, jnp.int32)
        for b in range(n_sel):
            row = imp[b:b + 1, :]
            if b < j0:
                cnt = cnt + jnp.where(row >= blk, 1, 0)
            elif b >= j0 + sub:
                cnt = cnt + jnp.where(row > blk, 1, 0)
            else:
                cnt = cnt + jnp.where(row > blk, 1, jnp.where(row == blk, jnp.where(jl > b, 1, 0), 0))
        ranks.append(cnt)
    rank = jnp.concatenate(ranks, axis=0)
    selb_t = jnp.where(rank < SEL_TOPK, 0.0, NEG)

    process(vwt_ref, jnp.maximum(i - nw, 0), 0, mt_w2)
    o_w = finish()

    selb = jnp.concatenate([selb_t, jnp.zeros((LANE - n_sel, tq), F32)], axis=0).T
    selb = selb.astype(BF16)
    qa = jnp.concatenate([qs, jnp.concatenate([selb] * HPG, axis=0)], axis=1)

    init()

    @pl.when(i == 0)
    def _():
        process(vst_ref, 0, 0, qk(qa, kaug_ref, 0, TAB_DIAG, 0))

    @pl.when(i >= 1)
    def _():
        mt0 = qk(qa, kaug_ref, i, TAB_DIAG, 0)
        mt1 = qk(qa, kaug_ref, i - 1, TAB_SUB, 1)
        process(vst_ref, i, 0, mt0)
        mt0 = qk(qa, kaug_ref, jnp.maximum(i - 2, 0), None, 0)
        process(vst_ref, i - 1, 1, mt1)
        n_far = i - 1
        rem = n_far % FAR_UNROLL

        def far_tiles(a, n, mt0):
            for t in range(n):
                nxt = qk(qa, kaug_ref, jnp.maximum(a - t - 1, 0), None, (t + 1) % 2)
                process(vst_ref, a - t, t % 2, mt0)
                mt0 = nxt
            return mt0

        mt0 = lax.fori_loop(0, n_far // FAR_UNROLL,
                            lambda p, mt: far_tiles(i - 2 - FAR_UNROLL * p, FAR_UNROLL, mt), mt0)

        @pl.when(rem >= 2)
        def _():
            mt_ref[...] = far_tiles(rem - 1, 2, mt0)

        @pl.when(rem == 1)
        def _():
            process(vst_ref, 0, 0, mt0)

        @pl.when(rem == 3)
        def _():
            process(vst_ref, 0, 0, mt_ref[...])

    o_s = finish()

    gt = gate_ref[rows, :].T
    for h in range(HPG):
        sl = slice(h * tq, (h + 1) * tq)
        o = gt[3 * h:3 * h + 1, :] * o_c[:, sl]
        o = o + gt[3 * h + 1:3 * h + 2, :] * o_s[:, sl]
        o = o + gt[3 * h + 2:3 * h + 3, :] * o_w[:, sl]
        o_ref[rows, h * HEAD_DIM:(h + 1) * HEAD_DIM] = o.T.astype(BF16)


def _nsa(q, k_all, vt_all, cmp, cmp_t, gates, fvec, ovl, e_t, cast_weights, batch, seq_len, *, tq):
    n_t = seq_len // (tq * NSA_TILES_PER_STEP)
    n_steps = batch * N_KV * n_t
    ts = tq * NSA_TILES_PER_STEP
    step_row = lambda b, g, i: ((b * N_KV + g) * n_t + i, 0)
    cast_specs = [pl.BlockSpec((w.shape[0] // n_steps, w.shape[1]), step_row) for w in cast_weights]
    n_sel = seq_len // SEL_LEN
    n_chunks = cmp.shape[2]
    mcols = HPG * tq
    band_rows = BAND_BELOW + tq // CMP_STRIDE
    qw = HPG * HEAD_DIM

    def vt_spec(first):
        return pl.BlockSpec((None, None, HEAD_DIM + ONES_ROWS, seq_len), lambda b, g, i: (b, first + g, 0, 0))

    kern = functools.partial(_nsa_kernel, tiles_per_step=NSA_TILES_PER_STEP, tq=tq, n_sel=n_sel,
                             n_cast=len(cast_weights))
    o_attn, *cast = pl.pallas_call(
        kern,
        grid=(batch, N_KV, n_t),
        in_specs=[
            pl.BlockSpec((ts, qw), lambda b, g, i: (b * n_t + i, g)),
            pl.BlockSpec((None, None, n_chunks, HEAD_DIM), lambda b, g, i: (b, g, 0, 0)),
            pl.BlockSpec((None, None, HEAD_DIM, n_chunks), lambda b, g, i: (b, N_KV + g, 0, 0)),
            pl.BlockSpec((seq_len, HEAD_DIM), lambda b, g, i: (b, g)),
            vt_spec(0),
            pl.BlockSpec((seq_len, HEAD_DIM), lambda b, g, i: (b, N_KV + g)),
            vt_spec(N_KV),
            pl.BlockSpec((ts, LANE), lambda b, g, i: (b * n_t + i, g)),
            pl.BlockSpec((None, HPG, 1, BIAS_LEN), lambda b, g, i: (g, 0, 0, 0)),
            pl.BlockSpec((n_sel, n_chunks), lambda b, g, i: (0, 0)),
            pl.BlockSpec((seq_len, LANE), lambda b, g, i: (0, 0)),
        ] + cast_specs,
        out_specs=[pl.BlockSpec((ts, qw), lambda b, g, i: (b * n_t + i, g))] + cast_specs,
        out_shape=[jax.ShapeDtypeStruct((batch * seq_len, ATTN_WIDTH), BF16)]
        + [jax.ShapeDtypeStruct(w.shape, BF16) for w in cast_weights],
        scratch_shapes=[
            pltpu.VMEM((seq_len, 2 * HEAD_DIM), BF16),
            pltpu.VMEM((4, tq, mcols), F32),
            pltpu.VMEM((2, band_rows, mcols), F32),
            pltpu.VMEM((n_chunks, mcols), F32),
            pltpu.VMEM((tq, mcols), F32),
            pltpu.VMEM((tq, mcols), F32),
            pltpu.VMEM((1, mcols), F32),
            pltpu.VMEM((1, mcols), F32),
            pltpu.VMEM((HEAD_DIM + ONES_ROWS, mcols), F32),
        ],
        compiler_params=_params(3),
        name="nsa_attention",
    )(q, cmp, cmp_t, k_all, vt_all, k_all, vt_all, gates, fvec, ovl, e_t, *cast_weights)
    return o_attn, cast


def _oproj_kernel(oa_ref, ov_ref, wo_ref, x_ref, g1_ref, g2_ref, x1_ref, h2_ref):
    ka = oa_ref.shape[1]
    tm = oa_ref.shape[0]
    for rows in (slice(k * tm // 4, (k + 1) * tm // 4) for k in range(4)):
        mix = jnp.dot(oa_ref[rows, :], wo_ref[0:ka, :], preferred_element_type=F32)
        mix = mix + jnp.dot(ov_ref[rows, :], wo_ref[ka:, :], preferred_element_type=F32)
        x1 = x_ref[rows, :] + _rms(mix, g1_ref[...])
        x1_ref[rows, :] = x1
        h2_ref[rows, :] = _rms(x1, g2_ref[...]).astype(BF16)


def _oproj(o_attn, o_conv, w_o, x2, g_post, g_pre, *, tm=512):
    m = x2.shape[0]
    ka, kv = o_attn.shape[1], o_conv.shape[1]
    row = lambda i: (i, 0)
    fixed = lambda i: (0, 0)
    return pl.pallas_call(
        _oproj_kernel,
        grid=(m // tm,),
        in_specs=[
            pl.BlockSpec((tm, ka), row),
            pl.BlockSpec((tm, kv), row),
            pl.BlockSpec((ka + kv, D_MODEL), fixed),
            pl.BlockSpec((tm, D_MODEL), row),
            pl.BlockSpec((1, D_MODEL), fixed),
            pl.BlockSpec((1, D_MODEL), fixed),
        ],
        out_specs=[pl.BlockSpec((tm, D_MODEL), row), pl.BlockSpec((tm, D_MODEL), row)],
        out_shape=[jax.ShapeDtypeStruct((m, D_MODEL), F32), jax.ShapeDtypeStruct((m, D_MODEL), BF16)],
        compiler_params=_params(1),
        name="out_proj",
    )(o_attn, o_conv, w_o, x2, g_post, g_pre)


def _ffn_kernel(h_ref, wu_ref, wd_ref, x1_ref, g_ref, o_ref, acc_ref):
    j = pl.program_id(1)
    last = pl.num_programs(1) - 1
    tm = h_ref.shape[0]

    def step(mode):
        a = jnp.dot(h_ref[...], wu_ref[...], preferred_element_type=F32)
        a = jnp.square(jnp.maximum(a, 0.0)).astype(BF16)
        if mode == "first":
            acc_ref[...] = jnp.dot(a, wd_ref[...], preferred_element_type=F32)
        elif mode == "middle":
            acc_ref[...] += jnp.dot(a, wd_ref[...], preferred_element_type=F32)
        else:
            for rows in (slice(0, tm // 2), slice(tm // 2, tm)):
                f = acc_ref[rows, :] + jnp.dot(a[rows, :], wd_ref[...], preferred_element_type=F32)
                o_ref[rows, :] = x1_ref[rows, :] + _rms(f, g_ref[...])

    pl.when(j == 0)(functools.partial(step, "first"))
    pl.when((j > 0) & (j < last))(functools.partial(step, "middle"))
    pl.when(j == last)(functools.partial(step, "last"))


def _ffn(h2, w_up, w_down, x1, g_post, *, tm=512, tf=1024):
    m = h2.shape[0]
    d_ff = w_up.shape[1]
    return pl.pallas_call(
        _ffn_kernel,
        grid=(m // tm, d_ff // tf),
        in_specs=[
            pl.BlockSpec((tm, D_MODEL), lambda i, j: (i, 0)),
            pl.BlockSpec((D_MODEL, tf), lambda i, j: (0, j)),
            pl.BlockSpec((tf, D_MODEL), lambda i, j: (j, 0)),
            pl.BlockSpec((tm, D_MODEL), lambda i, j: (i, 0)),
            pl.BlockSpec((1, D_MODEL), lambda i, j: (0, 0)),
        ],
        out_specs=pl.BlockSpec((tm, D_MODEL), lambda i, j: (i, 0)),
        out_shape=jax.ShapeDtypeStruct((m, D_MODEL), F32),
        scratch_shapes=[pltpu.VMEM((tm, D_MODEL), F32)],
        compiler_params=_params(2),
        name="ffn",
    )(h2, w_up, w_down, x1, g_post)


def _bucket_np(dist):
    n = np.maximum(dist, 0)
    max_exact = N_BUCKETS // 2
    nf = np.maximum(n, 1).astype(np.float32)
    large = max_exact + (np.log(nf / np.float32(max_exact)) / np.float32(math.log(MAX_DIST / max_exact))
                         * np.float32(N_BUCKETS - max_exact)).astype(np.int32)
    large = np.minimum(large, N_BUCKETS - 1)
    return np.where(n < max_exact, n, large)


def _bucket_starts():
    b = _bucket_np(np.arange(4 * MAX_DIST))
    return [int(np.argmax(b == k)) for k in range(N_BUCKETS)]


def _attention_tables(rel_bias, seq_len, tq):
    starts = _bucket_starts()
    assert starts[N_BUCKETS - 1] <= CMP_STRIDE * (BAND_BELOW + 1) - (CMP_LEN - 1)
    assert starts[N_BUCKETS - 1] <= tq and 2 * tq <= BIAS_LEN // 2
    rel = (rel_bias - rel_bias[:, N_BUCKETS - 1:]) * math.log2(math.e)
    d = jnp.arange(BIAS_LEN, dtype=jnp.int32)[None, :]
    fvec = jnp.broadcast_to(rel[:, 0:1], (N_HEADS, BIAS_LEN))
    for k in range(1, N_BUCKETS):
        fvec = jnp.where(d >= starts[k], rel[:, k:k + 1], fvec)
    fvec = jnp.where(d < BIAS_LEN // 2, fvec, NEG).astype(F32).reshape(N_KV, HPG, 1, BIAS_LEN)

    n_chunks = seq_len // CMP_STRIDE
    n_cmp = (seq_len - CMP_LEN) // CMP_STRIDE + 1
    n_sel = seq_len // SEL_LEN
    ci = np.arange(n_chunks)[None, :] * CMP_STRIDE
    sj = np.arange(n_sel)[:, None] * SEL_LEN
    ovl = ((ci < sj + SEL_LEN) & (ci + CMP_LEN > sj) & (np.arange(n_chunks)[None, :] < n_cmp))
    e_t = (np.arange(seq_len)[:, None] // SEL_LEN == np.arange(LANE)[None, :])
    return fvec, jnp.asarray(ovl, BF16), jnp.asarray(e_t, BF16)


def kernel(x, w_in, pe_cmp, w_cmp_k1, w_cmp_k2, w_cmp_v1, w_cmp_v2, conv_w, rel_bias, w_o, w_up, w_down,
           g_pre_mix, g_post_mix, g_pre_ffn, g_post_ffn):
    batch, seq_len, _ = x.shape
    depth = w_in.shape[0]
    tq = 256
    fvec, ovl, e_t = _attention_tables(rel_bias, seq_len, tq)
    x2 = x.reshape(batch * seq_len, D_MODEL)
    for l in range(depth):
        wl = jnp.swapaxes(w_in[l], 0, 1)
        g1 = g_pre_mix[l].reshape(1, D_MODEL)

        q, k_all, c_in, vt_all, gates, o_conv = _in_proj(x2, g1, wl, conv_w[l], batch, seq_len)

        cmp, cmp_t = _compress(c_in, pe_cmp[l], (w_cmp_k1[l], w_cmp_v1[l]), (w_cmp_k2[l], w_cmp_v2[l]),
                               batch, seq_len)

        o_attn, (wo_b, wup_b, wdown_b) = _nsa(q, k_all, vt_all, cmp, cmp_t, gates, fvec, ovl, e_t,
                                              [w_o[l], w_up[l], w_down[l]], batch, seq_len, tq=tq)

        x1, h2 = _oproj(o_attn, o_conv, wo_b, x2,
                        g_post_mix[l].reshape(1, D_MODEL), g_pre_ffn[l].reshape(1, D_MODEL))
        x2 = _ffn(h2, wup_b, wdown_b, x1, g_post_ffn[l].reshape(1, D_MODEL))
    return x2.reshape(batch, seq_len, D_MODEL)
```

```python
import functools
import math

import numpy as np
import jax
import jax.numpy as jnp
from jax import lax
from jax.experimental import pallas as pl
from jax.experimental.pallas import tpu as pltpu

F32 = jnp.float32
BF16 = jnp.bfloat16

D_MODEL = 2048
N_HEADS = 8
N_KV = 2
HPG = N_HEADS // N_KV
HEAD_DIM = 128
ATTN_WIDTH = N_HEADS * HEAD_DIM
KV_WIDTH = N_KV * HEAD_DIM
CONV_WIDTH = D_MODEL - ATTN_WIDTH
CONV_K = 3
N_BRANCH = 3
CMP_LEN = 32
CMP_STRIDE = 16
CMP_HIDDEN = 256
SEL_LEN = 64
SEL_TOPK = 16
WINDOW = 512
N_BUCKETS = 32
MAX_DIST = 128
EPS = 1e-6
NEG = -1e30
HALF_NEG = -5e29
FORCE = 1e9

QKV_WIDTH = ATTN_WIDTH + 6 * KV_WIDTH
GATE_OFF = QKV_WIDTH
CONV_OFF = QKV_WIDTH + N_HEADS * N_BRANCH
LANE = 128
VMEM_LIMIT = 56 * 1024 * 1024

_DN_T = (((1,), (1,)), ((), ()))


def _rms(x, g):
    ms = jnp.mean(x * x, axis=-1, keepdims=True)
    return x * lax.rsqrt(ms + EPS) * g


def _params(n_axes):
    return pltpu.CompilerParams(dimension_semantics=("arbitrary",) * n_axes, vmem_limit_bytes=VMEM_LIMIT)


QKV_BLOCK = 2 * KV_WIDTH
SLAB_KC, SLAB_VC, SLAB_KS, SLAB_VS, SLAB_KW, SLAB_VW = range(6)
CONV_BLOCK = 512
W_CHUNK = 256
SUBLANE = 8


def _load_weights(wt_hbm, w_ref, wg_ref, wconv_ref, stage_ref, sem):
    n_gate = N_HEADS * N_BRANCH
    per_group = HPG * N_BRANCH

    def store_rows(dst, row):
        def store(v):
            dst[row:row + v.shape[0], :] = v.astype(BF16)
        return store

    def store_gates(v):
        wg_ref[...] = jnp.zeros(wg_ref.shape, BF16)
        pad = jnp.zeros((2 * SUBLANE - per_group, v.shape[1]), F32)
        for g in range(N_KV):
            rows = jnp.concatenate([v[g * per_group:(g + 1) * per_group, :], pad], axis=0)
            wg_ref[g * LANE:g * LANE + 2 * SUBLANE, :] = rows.astype(BF16)

    chunks = [(r, W_CHUNK, store_rows(w_ref, r)) for r in range(0, QKV_WIDTH, W_CHUNK)]
    chunks.append((GATE_OFF, n_gate, store_gates))
    chunks += [(CONV_OFF + r, W_CHUNK, store_rows(wconv_ref, r)) for r in range(0, 3 * CONV_WIDTH, W_CHUNK)]

    def copy(k):
        src, n, _ = chunks[k]
        slot = k % 2
        return pltpu.make_async_copy(wt_hbm.at[pl.ds(src, n), :], stage_ref.at[slot, pl.ds(0, n), :], sem.at[slot])

    copy(0).start()
    for k, (_, n, store) in enumerate(chunks):
        if k + 1 < len(chunks):
            copy(k + 1).start()
        copy(k).wait()
        store(stage_ref[k % 2, 0:n, :])


def _inproj_kernel(x_ref, g_ref, wt_hbm, cw_ref, q_ref, k_ref, c_ref, vt_ref, gate_ref, ov_ref,
                   h_ref, carry_ref, w_ref, wg_ref, wconv_ref, stage_ref, sem, *, q_scale, tiles_per_seq):
    i = pl.program_id(0)
    tm = x_ref.shape[0]

    @pl.when(i == 0)
    def _():
        _load_weights(wt_hbm, w_ref, wg_ref, wconv_ref, stage_ref, sem)

    h_ref[...] = _rms(x_ref[...], g_ref[...]).astype(BF16)
    gate_ref[...] = jax.nn.sigmoid(lax.dot_general(h_ref[...], wg_ref[...], _DN_T, preferred_element_type=F32))

    def proj(w, j, width):
        return lax.dot_general(h_ref[...], w[j * width:(j + 1) * width, :], _DN_T, preferred_element_type=F32)

    for j in range(ATTN_WIDTH // QKV_BLOCK):
        q_ref[:, j * QKV_BLOCK:(j + 1) * QKV_BLOCK] = (proj(w_ref, j, QKV_BLOCK) * q_scale).astype(BF16)

    def slab(which):
        return proj(w_ref, ATTN_WIDTH // KV_WIDTH + which, KV_WIDTH)

    ones = jnp.where(lax.broadcasted_iota(jnp.int32, (ONES_ROWS, tm), 0) == 0, 1.0, 0.0).astype(BF16)
    for pair, (k_slab, c_slab, v_slab) in enumerate(((SLAB_KS, SLAB_KC, SLAB_VS), (SLAB_KW, SLAB_VC, SLAB_VW))):
        cols = slice(pair * KV_WIDTH, (pair + 1) * KV_WIDTH)
        k_ref[:, cols] = slab(k_slab).astype(BF16)
        c_ref[:, cols] = slab(c_slab)
        v = slab(v_slab)
        for g in range(N_KV):
            s = pair * N_KV + g
            vt_ref[s, 0:HEAD_DIM, :] = v[:, g * HEAD_DIM:(g + 1) * HEAD_DIM].T.astype(BF16)
            vt_ref[s, HEAD_DIM:HEAD_DIM + ONES_ROWS, :] = ones

    def conv_proj(which, j):
        return proj(wconv_ref, which * (CONV_WIDTH // CONV_BLOCK) + j, CONV_BLOCK)

    row = lax.broadcasted_iota(jnp.int32, (tm, CONV_BLOCK), 0)
    for j in range(CONV_WIDTH // CONV_BLOCK):
        cols = slice(j * CONV_BLOCK, (j + 1) * CONV_BLOCK)
        u = conv_proj(2, j) * conv_proj(0, j)
        prev = carry_ref[j]
        prev = jnp.where(i % tiles_per_seq == 0, 0.0, prev)
        carry_ref[j] = u[tm - SUBLANE:tm, :]
        p1 = prev[SUBLANE - 1:SUBLANE, :]
        p2 = prev[SUBLANE - 2:SUBLANE - 1, :]
        u1 = jnp.where(row == 0, p1, pltpu.roll(u, 1, axis=0))
        u2 = jnp.where(row == 0, p2, jnp.where(row == 1, p1, pltpu.roll(u, 2, axis=0)))
        w = cw_ref[:, cols]
        y = w[0:1, :] * u2
        y = y + w[1:2, :] * u1
        y = y + w[2:3, :] * u
        ov_ref[:, cols] = (conv_proj(1, j) * y).astype(BF16)


def _in_proj(x2, g, w_t, conv_w, batch, seq_len, *, tm=512):
    m = x2.shape[0]
    gw = N_KV * LANE
    tps = seq_len // tm
    n_slabs = QKV_BLOCK // HEAD_DIM
    kern = functools.partial(_inproj_kernel, q_scale=HEAD_DIM ** -0.5 * math.log2(math.e), tiles_per_seq=tps)
    row = lambda i: (i, 0)

    def resident(shape):
        return pl.BlockSpec(shape, lambda i: (0, 0), pipeline_mode=pl.Buffered(1))

    return pl.pallas_call(
        kern,
        grid=(m // tm,),
        in_specs=[
            pl.BlockSpec((tm, D_MODEL), row),
            resident((1, D_MODEL)),
            pl.BlockSpec(memory_space=pl.ANY),
            resident(conv_w.shape),
        ],
        out_specs=[
            pl.BlockSpec((tm, ATTN_WIDTH), row),
            pl.BlockSpec((tm, QKV_BLOCK), row),
            pl.BlockSpec((tm, QKV_BLOCK), row),
            pl.BlockSpec((None, n_slabs, HEAD_DIM + ONES_ROWS, tm), lambda i: (i // tps, 0, 0, i % tps)),
            pl.BlockSpec((tm, gw), row),
            pl.BlockSpec((tm, CONV_WIDTH), row),
        ],
        out_shape=[
            jax.ShapeDtypeStruct((m, ATTN_WIDTH), BF16),
            jax.ShapeDtypeStruct((m, QKV_BLOCK), BF16),
            jax.ShapeDtypeStruct((m, QKV_BLOCK), F32),
            jax.ShapeDtypeStruct((batch, n_slabs, HEAD_DIM + ONES_ROWS, seq_len), BF16),
            jax.ShapeDtypeStruct((m, gw), F32),
            jax.ShapeDtypeStruct((m, CONV_WIDTH), BF16),
        ],
        scratch_shapes=[pltpu.VMEM((tm, D_MODEL), BF16),
                        pltpu.VMEM((CONV_WIDTH // CONV_BLOCK, SUBLANE, CONV_BLOCK), F32),
                        pltpu.VMEM((QKV_WIDTH, D_MODEL), BF16),
                        pltpu.VMEM((gw, D_MODEL), BF16),
                        pltpu.VMEM((3 * CONV_WIDTH, D_MODEL), BF16),
                        pltpu.VMEM((2, W_CHUNK, D_MODEL), F32),
                        pltpu.SemaphoreType.DMA((2,))],
        compiler_params=_params(1),
        name="in_proj",
    )(x2, g, w_t, conv_w)


def _compress_kernel(x_ref, pe_ref, w1k_ref, w1v_ref, w2k_ref, w2v_ref, o_ref, ot_ref):
    n = x_ref.shape[0] // CMP_STRIDE

    def body(w1_ref, w2_ref):
        a = jnp.zeros((n, CMP_HIDDEN), F32)
        b = jnp.zeros((n, CMP_HIDDEN), F32)
        for l in range(CMP_STRIDE):
            xl = x_ref[pl.ds(l, n, stride=CMP_STRIDE), :]
            xa = (xl + pe_ref[l:l + 1, :]).astype(BF16)
            xb = (xl + pe_ref[CMP_STRIDE + l:CMP_STRIDE + l + 1, :]).astype(BF16)
            a = a + jnp.dot(xa, w1_ref[l].astype(BF16), preferred_element_type=F32)
            b = b + jnp.dot(xb, w1_ref[CMP_STRIDE + l].astype(BF16), preferred_element_type=F32)
        pre = a + pltpu.roll(b, n - 1, axis=0)
        hid = pre * jax.nn.sigmoid(pre)
        out = jnp.dot(hid.astype(BF16), w2_ref[...].astype(BF16), preferred_element_type=F32)
        row = lax.broadcasted_iota(jnp.int32, out.shape, 0)
        out = jnp.where(row < n - 1, out, 0.0)
        o_ref[...] = out.astype(BF16)
        ot_ref[...] = out.T.astype(BF16)

    is_value = pl.program_id(1) >= N_KV
    pl.when(jnp.logical_not(is_value))(functools.partial(body, w1k_ref, w2k_ref))
    pl.when(is_value)(functools.partial(body, w1v_ref, w2v_ref))


def _compress(c_in, pe, w1_kv, w2_kv, batch, seq_len):
    n_slabs = c_in.shape[1] // HEAD_DIM
    n_chunks = seq_len // CMP_STRIDE

    def resident(shape):
        return pl.BlockSpec(shape, lambda i, j: (0,) * len(shape), pipeline_mode=pl.Buffered(1))

    return pl.pallas_call(
        _compress_kernel,
        grid=(batch, n_slabs),
        in_specs=[
            pl.BlockSpec((seq_len, HEAD_DIM), lambda i, j: (i, j)),
            resident((CMP_LEN, HEAD_DIM)),
            resident(w1_kv[0].shape), resident(w1_kv[1].shape),
            resident(w2_kv[0].shape), resident(w2_kv[1].shape),
        ],
        out_specs=[
            pl.BlockSpec((None, None, n_chunks, HEAD_DIM), lambda i, j: (i, j, 0, 0)),
            pl.BlockSpec((None, None, HEAD_DIM, n_chunks), lambda i, j: (i, j, 0, 0)),
        ],
        out_shape=[
            jax.ShapeDtypeStruct((batch, n_slabs, n_chunks, HEAD_DIM), BF16),
            jax.ShapeDtypeStruct((batch, n_slabs, HEAD_DIM, n_chunks), BF16),
        ],
        compiler_params=_params(2),
        name="compress",
    )(c_in, pe, *w1_kv, *w2_kv)


TAB_DIAG, TAB_SUB, TAB_FAR, TAB_MASK = 0, 1, 2, 3
BAND_BELOW = 8
ONES_ROWS = 16


N_NSA_INPUTS = 11
BIAS_LEN = 1024
FAR_UNROLL = 8
NSA_TILES_PER_STEP = 2


def _nsa_kernel(*refs, tiles_per_step, **static):
    for u in range(tiles_per_step):
        _nsa_tile(refs, u, tiles_per_step, **static)


def _nsa_tile(refs, u, tiles_per_step, *, tq, n_sel, n_cast):
    (q_ref, kc_ref, vct_ref, ks_ref, vst_ref, kw_ref, vwt_ref, gate_ref, fvec_ref,
     ovl_ref, et_ref) = refs[:N_NSA_INPUTS]
    cast_in = refs[N_NSA_INPUTS:N_NSA_INPUTS + n_cast]
    o_ref = refs[N_NSA_INPUTS + n_cast]
    cast_out = refs[N_NSA_INPUTS + n_cast + 1:N_NSA_INPUTS + 2 * n_cast + 1]
    (kaug_ref, tab_ref, band_ref, sc_ref, s0_ref, s1_ref, m_ref, mt_ref,
     acc_ref) = refs[N_NSA_INPUTS + 2 * n_cast + 1:]
    i = pl.program_id(2) * tiles_per_step + u
    rows = slice(u * tq, (u + 1) * tq)

    tk = tq
    mcols = HPG * tq
    nw = WINDOW // tk
    cpt = tq // CMP_STRIDE
    band_rows = BAND_BELOW + cpt

    def group_start(fn):
        if u == 0:
            pl.when(i == 0)(fn)

    @group_start
    def _():
        kaug_ref[:, 0:HEAD_DIM] = ks_ref[...]
        kaug_ref[:, HEAD_DIM:2 * HEAD_DIM] = et_ref[...]

    @group_start
    def _():
        c = lax.broadcasted_iota(jnp.int32, (tk, tq), 0)
        r = lax.broadcasted_iota(jnp.int32, (tk, tq), 1)
        far = jnp.where(r < c, 0.0, NEG)
        lane = lax.broadcasted_iota(jnp.int32, (1, BIAS_LEN), 1)
        for h in range(HPG):
            cols = slice(h * tq, (h + 1) * tq)
            f = fvec_ref[h]
            f_diag = jnp.where(lane < tq, f, NEG)
            x = pltpu.roll(jnp.broadcast_to(f_diag, (tk, BIAS_LEN)), 0, 1, stride=1, stride_axis=0)
            tab_ref[TAB_DIAG, :, cols] = x[:, 0:tq]
            x = pltpu.roll(jnp.broadcast_to(f, (tk, BIAS_LEN)), 0, 1, stride=1, stride_axis=0)
            tab_ref[TAB_SUB, :, cols] = x[:, tq:2 * tq]
            tab_ref[TAB_FAR, :, cols] = far
            tab_ref[TAB_MASK, :, cols] = jnp.full((tk, tq), NEG, F32)
            for v, first in enumerate((0, -BAND_BELOW)):
                shift = (CMP_STRIDE * first + CMP_LEN - 1) % BIAS_LEN
                f_shift = pltpu.roll(f, shift, 1)
                x = pltpu.roll(jnp.broadcast_to(f_shift, (band_rows, BIAS_LEN)), 0, 1,
                               stride=CMP_STRIDE, stride_axis=0)
                band_ref[v, :, cols] = x[:, 0:tq]

    q = q_ref[rows, :]
    qs = jnp.concatenate([q[:, h * HEAD_DIM:(h + 1) * HEAD_DIM] for h in range(HPG)], axis=0)

    s_refs = (s0_ref, s1_ref)

    def init():
        m_ref[...] = jnp.full((1, mcols), NEG, F32)
        acc_ref[...] = jnp.zeros(acc_ref.shape, F32)

    def keys(kt):
        return pl.ds(pl.multiple_of(kt * tk, tk), tk)

    def qk(qmat, k_ref, kt, kind, buf):
        s = lax.dot_general(k_ref[keys(kt), :], qmat, _DN_T, preferred_element_type=F32)
        if kind is not None:
            s = s + tab_ref[kind]
        s_refs[buf][...] = s
        return jnp.max(s, axis=0, keepdims=True)

    def process(vt_ref, kt, buf, m_tile):
        m_prev = m_ref[...]
        m_next = jnp.maximum(m_prev, m_tile)
        alpha = jnp.exp2(m_prev - m_next)
        p = jnp.exp2(s_refs[buf][...] - m_next).astype(BF16)
        acc_ref[...] = alpha * acc_ref[...] + jnp.dot(vt_ref[:, keys(kt)], p, preferred_element_type=F32)
        m_ref[...] = m_next

    def finish():
        return acc_ref[0:HEAD_DIM, :] * (1.0 / acc_ref[HEAD_DIM:HEAD_DIM + 1, :])

    assert nw == 2, "window tiles are i, i-1 (previous-tile table) and i-nw (window-edge table)"
    init()
    mt_w0 = qk(qs, kw_ref, i, TAB_DIAG, 0)

    if u == 0:
        for src, dst in zip(cast_in, cast_out):
            dst[...] = src[...].astype(BF16)

    raw = lax.dot_general(kc_ref[...], qs, _DN_T, preferred_element_type=F32)
    crow = lax.broadcasted_iota(jnp.int32, raw.shape, 0)
    sc_ref[...] = jnp.where(crow < cpt * (i + 1), raw, NEG)
    band = pl.ds(pl.multiple_of(jnp.maximum(cpt * i - BAND_BELOW, 0), SUBLANE), band_rows)
    sc_ref[band, :] += band_ref[jnp.minimum(i, 1)]

    mt_w1 = qk(qs, kw_ref, jnp.maximum(i - 1, 0), jnp.where(i >= 1, TAB_SUB, TAB_MASK), 1)
    process(vwt_ref, i, 0, mt_w0)

    sc = sc_ref[...]
    mc = jnp.maximum(jnp.max(sc, axis=0, keepdims=True), HALF_NEG)
    pc = jnp.exp2(sc - mc)
    lc = jnp.sum(pc, axis=0, keepdims=True)
    pc = pc * jnp.where(lc > 0.0, 1.0 / lc, 0.0)
    o_c = jnp.dot(vct_ref[...], pc.astype(BF16), preferred_element_type=F32)

    mt_w2 = qk(qs, kw_ref, jnp.maximum(i - nw, 0), jnp.where(i >= nw, TAB_FAR, TAB_MASK), 0)
    process(vwt_ref, jnp.maximum(i - 1, 0), 1, mt_w1)

    ps = pc[:, 0:tq] + pc[:, tq:2 * tq] + pc[:, 2 * tq:3 * tq] + pc[:, 3 * tq:4 * tq]
    hi = ps.astype(BF16)
    r1 = ps - hi.astype(F32)
    mid = r1.astype(BF16)
    lo = (r1 - mid.astype(F32)).astype(BF16)
    ovl = ovl_ref[...]
    imp = (jnp.dot(ovl, hi, preferred_element_type=F32) + jnp.dot(ovl, mid, preferred_element_type=F32)
           + jnp.dot(ovl, lo, preferred_element_type=F32))
    jj = lax.broadcasted_iota(jnp.int32, (n_sel, tq), 0)
    tt = i * tq + lax.broadcasted_iota(jnp.int32, (n_sel, tq), 1)
    cur = tt >> int(math.log2(SEL_LEN))
    forced = (jj == 0) | (jj == cur) | (jj == cur - 1)
    imp = jnp.where(forced, FORCE, imp)
    imp = jnp.where(jj * SEL_LEN <= tt, imp, NEG)
    sub = SUBLANE
    ranks = []
    for j0 in range(0, n_sel, sub):
        blk = imp[j0:j0 + sub, :]
        jl = j0 + lax.broadcasted_iota(jnp.int32, blk.shape, 0)
        cnt = jnp.zeros(blk.shape, jnp.int32)
        for b in range(n_sel):
            row = imp[b:b + 1, :]
            if b < j0:
                cnt = cnt + jnp.where(row >= blk, 1, 0)
            elif b >= j0 + sub:
                cnt = cnt + jnp.where(row > blk, 1, 0)
            else:
                cnt = cnt + jnp.where(row > blk, 1, jnp.where(row == blk, jnp.where(jl > b, 1, 0), 0))
        ranks.append(cnt)
    rank = jnp.concatenate(ranks, axis=0)
    selb_t = jnp.where(rank < SEL_TOPK, 0.0, NEG)

    process(vwt_ref, jnp.maximum(i - nw, 0), 0, mt_w2)
    o_w = finish()

    selb = jnp.concatenate([selb_t, jnp.zeros((LANE - n_sel, tq), F32)], axis=0).T
    selb = selb.astype(BF16)
    qa = jnp.concatenate([qs, jnp.concatenate([selb] * HPG, axis=0)], axis=1)

    init()
    mt0 = qk(qa, kaug_ref, i, TAB_DIAG, 0)
    mt1 = qk(qa, kaug_ref, jnp.maximum(i - 1, 0), jnp.where(i >= 1, TAB_SUB, TAB_MASK), 1)
    process(vst_ref, i, 0, mt0)
    mt0 = qk(qa, kaug_ref, jnp.maximum(i - 2, 0), None, 0)
    process(vst_ref, jnp.maximum(i - 1, 0), 1, mt1)
    n_far = jnp.maximum(i - 1, 0)
    rem = n_far % FAR_UNROLL

    def far_tiles(a, n, mt0):
        for t in range(n):
            nxt = qk(qa, kaug_ref, jnp.maximum(a - t - 1, 0), None, (t + 1) % 2)
            process(vst_ref, a - t, t % 2, mt0)
            mt0 = nxt
        return mt0

    mt0 = lax.fori_loop(0, n_far // FAR_UNROLL,
                        lambda p, mt: far_tiles(i - 2 - FAR_UNROLL * p, FAR_UNROLL, mt), mt0)

    mt_ref[...] = mt0
    size = FAR_UNROLL // 2
    while size >= 2:
        @pl.when((rem & size) != 0)
        def _(size=size):
            mt_ref[...] = far_tiles((rem & (2 * size - 1)) - 1, size, mt_ref[...])
        size //= 2

    @pl.when((rem & 1) != 0)
    def _():
        process(vst_ref, 0, 0, mt_ref[...])

    o_s = finish()

    gt = gate_ref[rows, :].T
    for h in range(HPG):
        sl = slice(h * tq, (h + 1) * tq)
        o = gt[3 * h:3 * h + 1, :] * o_c[:, sl]
        o = o + gt[3 * h + 1:3 * h + 2, :] * o_s[:, sl]
        o = o + gt[3 * h + 2:3 * h + 3, :] * o_w[:, sl]
        o_ref[rows, h * HEAD_DIM:(h + 1) * HEAD_DIM] = o.T.astype(BF16)


def _nsa(q, k_all, vt_all, cmp, cmp_t, gates, fvec, ovl, e_t, cast_weights, batch, seq_len, *, tq):
    n_t = seq_len // (tq * NSA_TILES_PER_STEP)
    n_steps = batch * N_KV * n_t
    ts = tq * NSA_TILES_PER_STEP
    step_row = lambda b, g, i: ((b * N_KV + g) * n_t + i, 0)
    cast_specs = [pl.BlockSpec((w.shape[0] // n_steps, w.shape[1]), step_row) for w in cast_weights]
    n_sel = seq_len // SEL_LEN
    n_chunks = cmp.shape[2]
    mcols = HPG * tq
    band_rows = BAND_BELOW + tq // CMP_STRIDE
    qw = HPG * HEAD_DIM

    def vt_spec(first):
        return pl.BlockSpec((None, None, HEAD_DIM + ONES_ROWS, seq_len), lambda b, g, i: (b, first + g, 0, 0))

    kern = functools.partial(_nsa_kernel, tiles_per_step=NSA_TILES_PER_STEP, tq=tq, n_sel=n_sel,
                             n_cast=len(cast_weights))
    o_attn, *cast = pl.pallas_call(
        kern,
        grid=(batch, N_KV, n_t),
        in_specs=[
            pl.BlockSpec((ts, qw), lambda b, g, i: (b * n_t + i, g)),
            pl.BlockSpec((None, None, n_chunks, HEAD_DIM), lambda b, g, i: (b, g, 0, 0)),
            pl.BlockSpec((None, None, HEAD_DIM, n_chunks), lambda b, g, i: (b, N_KV + g, 0, 0)),
            pl.BlockSpec((seq_len, HEAD_DIM), lambda b, g, i: (b, g)),
            vt_spec(0),
            pl.BlockSpec((seq_len, HEAD_DIM), lambda b, g, i: (b, N_KV + g)),
            vt_spec(N_KV),
            pl.BlockSpec((ts, LANE), lambda b, g, i: (b * n_t + i, g)),
            pl.BlockSpec((None, HPG, 1, BIAS_LEN), lambda b, g, i: (g, 0, 0, 0)),
            pl.BlockSpec((n_sel, n_chunks), lambda b, g, i: (0, 0)),
            pl.BlockSpec((seq_len, LANE), lambda b, g, i: (0, 0)),
        ] + cast_specs,
        out_specs=[pl.BlockSpec((ts, qw), lambda b, g, i: (b * n_t + i, g))] + cast_specs,
        out_shape=[jax.ShapeDtypeStruct((batch * seq_len, ATTN_WIDTH), BF16)]
        + [jax.ShapeDtypeStruct(w.shape, BF16) for w in cast_weights],
        scratch_shapes=[
            pltpu.VMEM((seq_len, 2 * HEAD_DIM), BF16),
            pltpu.VMEM((4, tq, mcols), F32),
            pltpu.VMEM((2, band_rows, mcols), F32),
            pltpu.VMEM((n_chunks, mcols), F32),
            pltpu.VMEM((tq, mcols), F32),
            pltpu.VMEM((tq, mcols), F32),
            pltpu.VMEM((1, mcols), F32),
            pltpu.VMEM((1, mcols), F32),
            pltpu.VMEM((HEAD_DIM + ONES_ROWS, mcols), F32),
        ],
        compiler_params=_params(3),
        name="nsa_attention",
    )(q, cmp, cmp_t, k_all, vt_all, k_all, vt_all, gates, fvec, ovl, e_t, *cast_weights)
    return o_attn, cast


OPROJ_CHUNKS = 4


def _oproj_kernel(oa_ref, ov_ref, wo_ref, x_ref, g1_ref, g2_ref, x1_ref, h2_ref):
    ka = oa_ref.shape[1]
    tm = oa_ref.shape[0]
    chunk = tm // OPROJ_CHUNKS
    for rows in (slice(k * chunk, (k + 1) * chunk) for k in range(OPROJ_CHUNKS)):
        mix = jnp.dot(oa_ref[rows, :], wo_ref[0:ka, :], preferred_element_type=F32)
        mix = mix + jnp.dot(ov_ref[rows, :], wo_ref[ka:, :], preferred_element_type=F32)
        x1 = x_ref[rows, :] + _rms(mix, g1_ref[...])
        x1_ref[rows, :] = x1
        h2_ref[rows, :] = _rms(x1, g2_ref[...]).astype(BF16)


def _oproj(o_attn, o_conv, w_o, x2, g_post, g_pre, *, tm=512):
    m = x2.shape[0]
    ka, kv = o_attn.shape[1], o_conv.shape[1]
    row = lambda i: (i, 0)
    fixed = lambda i: (0, 0)
    return pl.pallas_call(
        _oproj_kernel,
        grid=(m // tm,),
        in_specs=[
            pl.BlockSpec((tm, ka), row),
            pl.BlockSpec((tm, kv), row),
            pl.BlockSpec((ka + kv, D_MODEL), fixed),
            pl.BlockSpec((tm, D_MODEL), row),
            pl.BlockSpec((1, D_MODEL), fixed),
            pl.BlockSpec((1, D_MODEL), fixed),
        ],
        out_specs=[pl.BlockSpec((tm, D_MODEL), row), pl.BlockSpec((tm, D_MODEL), row)],
        out_shape=[jax.ShapeDtypeStruct((m, D_MODEL), F32), jax.ShapeDtypeStruct((m, D_MODEL), BF16)],
        compiler_params=_params(1),
        name="out_proj",
    )(o_attn, o_conv, w_o, x2, g_post, g_pre)


def _ffn_kernel(h_ref, wu_ref, wd_ref, x1_ref, g_ref, o_ref, acc_ref):
    j = pl.program_id(1)
    last = pl.num_programs(1) - 1
    tm = h_ref.shape[0]

    def step(mode):
        a = jnp.dot(h_ref[...], wu_ref[...], preferred_element_type=F32)
        a = jnp.square(jnp.maximum(a, 0.0)).astype(BF16)
        if mode == "first":
            acc_ref[...] = jnp.dot(a, wd_ref[...], preferred_element_type=F32)
        elif mode == "middle":
            acc_ref[...] += jnp.dot(a, wd_ref[...], preferred_element_type=F32)
        else:
            for rows in (slice(0, tm // 2), slice(tm // 2, tm)):
                f = acc_ref[rows, :] + jnp.dot(a[rows, :], wd_ref[...], preferred_element_type=F32)
                o_ref[rows, :] = x1_ref[rows, :] + _rms(f, g_ref[...])

    pl.when(j == 0)(functools.partial(step, "first"))
    pl.when((j > 0) & (j < last))(functools.partial(step, "middle"))
    pl.when(j == last)(functools.partial(step, "last"))


def _ffn(h2, w_up, w_down, x1, g_post, *, tm=512, tf=1024):
    m = h2.shape[0]
    d_ff = w_up.shape[1]
    return pl.pallas_call(
        _ffn_kernel,
        grid=(m // tm, d_ff // tf),
        in_specs=[
            pl.BlockSpec((tm, D_MODEL), lambda i, j: (i, 0)),
            pl.BlockSpec((D_MODEL, tf), lambda i, j: (0, j)),
            pl.BlockSpec((tf, D_MODEL), lambda i, j: (j, 0)),
            pl.BlockSpec((tm, D_MODEL), lambda i, j: (i, 0)),
            pl.BlockSpec((1, D_MODEL), lambda i, j: (0, 0)),
        ],
        out_specs=pl.BlockSpec((tm, D_MODEL), lambda i, j: (i, 0)),
        out_shape=jax.ShapeDtypeStruct((m, D_MODEL), F32),
        scratch_shapes=[pltpu.VMEM((tm, D_MODEL), F32)],
        compiler_params=_params(2),
        name="ffn",
    )(h2, w_up, w_down, x1, g_post)


def _bucket_np(dist):
    n = np.maximum(dist, 0)
    max_exact = N_BUCKETS // 2
    nf = np.maximum(n, 1).astype(np.float32)
    large = max_exact + (np.log(nf / np.float32(max_exact)) / np.float32(math.log(MAX_DIST / max_exact))
                         * np.float32(N_BUCKETS - max_exact)).astype(np.int32)
    large = np.minimum(large, N_BUCKETS - 1)
    return np.where(n < max_exact, n, large)


def _bucket_starts():
    b = _bucket_np(np.arange(4 * MAX_DIST))
    return [int(np.argmax(b == k)) for k in range(N_BUCKETS)]


def _attention_tables(rel_bias, seq_len, tq):
    starts = _bucket_starts()
    assert starts[N_BUCKETS - 1] <= CMP_STRIDE * (BAND_BELOW + 1) - (CMP_LEN - 1)
    assert starts[N_BUCKETS - 1] <= tq and 2 * tq <= BIAS_LEN // 2
    rel = (rel_bias - rel_bias[:, N_BUCKETS - 1:]) * math.log2(math.e)
    d = jnp.arange(BIAS_LEN, dtype=jnp.int32)[None, :]
    fvec = jnp.broadcast_to(rel[:, 0:1], (N_HEADS, BIAS_LEN))
    for k in range(1, N_BUCKETS):
        fvec = jnp.where(d >= starts[k], rel[:, k:k + 1], fvec)
    fvec = jnp.where(d < BIAS_LEN // 2, fvec, NEG).astype(F32).reshape(N_KV, HPG, 1, BIAS_LEN)

    n_chunks = seq_len // CMP_STRIDE
    n_cmp = (seq_len - CMP_LEN) // CMP_STRIDE + 1
    n_sel = seq_len // SEL_LEN
    ci = np.arange(n_chunks)[None, :] * CMP_STRIDE
    sj = np.arange(n_sel)[:, None] * SEL_LEN
    ovl = ((ci < sj + SEL_LEN) & (ci + CMP_LEN > sj) & (np.arange(n_chunks)[None, :] < n_cmp))
    e_t = (np.arange(seq_len)[:, None] // SEL_LEN == np.arange(LANE)[None, :])
    return fvec, jnp.asarray(ovl, BF16), jnp.asarray(e_t, BF16)


def kernel(x, w_in, pe_cmp, w_cmp_k1, w_cmp_k2, w_cmp_v1, w_cmp_v2, conv_w, rel_bias, w_o, w_up, w_down,
           g_pre_mix, g_post_mix, g_pre_ffn, g_post_ffn):
    batch, seq_len, _ = x.shape
    depth = w_in.shape[0]
    tq = 256
    fvec, ovl, e_t = _attention_tables(rel_bias, seq_len, tq)
    x2 = x.reshape(batch * seq_len, D_MODEL)
    for l in range(depth):
        wl = jnp.swapaxes(w_in[l], 0, 1)
        g1 = g_pre_mix[l].reshape(1, D_MODEL)

        q, k_all, c_in, vt_all, gates, o_conv = _in_proj(x2, g1, wl, conv_w[l], batch, seq_len)

        cmp, cmp_t = _compress(c_in, pe_cmp[l], (w_cmp_k1[l], w_cmp_v1[l]), (w_cmp_k2[l], w_cmp_v2[l]),
                               batch, seq_len)

        o_attn, (wo_b, wup_b, wdown_b) = _nsa(q, k_all, vt_all, cmp, cmp_t, gates, fvec, ovl, e_t,
                                              [w_o[l], w_up[l], w_down[l]], batch, seq_len, tq=tq)

        x1, h2 = _oproj(o_attn, o_conv, wo_b, x2,
                        g_post_mix[l].reshape(1, D_MODEL), g_pre_ffn[l].reshape(1, D_MODEL))
        x2 = _ffn(h2, wup_b, wdown_b, x1, g_post_ffn[l].reshape(1, D_MODEL))
    return x2.reshape(batch, seq_len, D_MODEL)
```

```python
import functools
import math

import numpy as np
import jax
import jax.numpy as jnp
from jax import lax
from jax.experimental import pallas as pl
from jax.experimental.pallas import tpu as pltpu

F32 = jnp.float32
BF16 = jnp.bfloat16

D_MODEL = 2048
N_HEADS = 8
N_KV = 2
HPG = N_HEADS // N_KV
HEAD_DIM = 128
ATTN_WIDTH = N_HEADS * HEAD_DIM
KV_WIDTH = N_KV * HEAD_DIM
CONV_WIDTH = D_MODEL - ATTN_WIDTH
CONV_K = 3
N_BRANCH = 3
CMP_LEN = 32
CMP_STRIDE = 16
CMP_HIDDEN = 256
SEL_LEN = 64
SEL_TOPK = 16
WINDOW = 512
N_BUCKETS = 32
MAX_DIST = 128
EPS = 1e-6
NEG = -1e30
HALF_NEG = -5e29
FORCE = 1e9

QKV_WIDTH = ATTN_WIDTH + 6 * KV_WIDTH
GATE_OFF = QKV_WIDTH
CONV_OFF = QKV_WIDTH + N_HEADS * N_BRANCH
LANE = 128
VMEM_LIMIT = 56 * 1024 * 1024

_DN_T = (((1,), (1,)), ((), ()))


def _rms(x, g):
    ms = jnp.mean(x * x, axis=-1, keepdims=True)
    return x * lax.rsqrt(ms + EPS) * g


def _params(n_axes):
    return pltpu.CompilerParams(dimension_semantics=("arbitrary",) * n_axes, vmem_limit_bytes=VMEM_LIMIT)


QKV_BLOCK = 2 * KV_WIDTH
SLAB_KC, SLAB_VC, SLAB_KS, SLAB_VS, SLAB_KW, SLAB_VW = range(6)
CONV_BLOCK = 512
W_CHUNK = 256
SUBLANE = 8


def _load_weights(wt_hbm, w_ref, wg_ref, wconv_ref, stage_ref, sem):
    n_gate = N_HEADS * N_BRANCH
    per_group = HPG * N_BRANCH

    def store_rows(dst, row):
        def store(v):
            dst[row:row + v.shape[0], :] = v.astype(BF16)
        return store

    def store_gates(v):
        wg_ref[...] = jnp.zeros(wg_ref.shape, BF16)
        pad = jnp.zeros((2 * SUBLANE - per_group, v.shape[1]), F32)
        for g in range(N_KV):
            rows = jnp.concatenate([v[g * per_group:(g + 1) * per_group, :], pad], axis=0)
            wg_ref[g * LANE:g * LANE + 2 * SUBLANE, :] = rows.astype(BF16)

    chunks = [(r, W_CHUNK, store_rows(w_ref, r)) for r in range(0, QKV_WIDTH, W_CHUNK)]
    chunks.append((GATE_OFF, n_gate, store_gates))
    chunks += [(CONV_OFF + r, W_CHUNK, store_rows(wconv_ref, r)) for r in range(0, 3 * CONV_WIDTH, W_CHUNK)]

    def copy(k):
        src, n, _ = chunks[k]
        slot = k % 2
        return pltpu.make_async_copy(wt_hbm.at[pl.ds(src, n), :], stage_ref.at[slot, pl.ds(0, n), :], sem.at[slot])

    copy(0).start()
    for k, (_, n, store) in enumerate(chunks):
        if k + 1 < len(chunks):
            copy(k + 1).start()
        copy(k).wait()
        store(stage_ref[k % 2, 0:n, :])


def _inproj_kernel(x_ref, g_ref, wt_hbm, cw_ref, q_ref, k_ref, c_ref, vt_ref, gate_ref, ov_ref,
                   h_ref, carry_ref, w_ref, wg_ref, wconv_ref, stage_ref, sem, *, q_scale, tiles_per_seq):
    i = pl.program_id(0)
    tm = x_ref.shape[0]

    @pl.when(i == 0)
    def _():
        _load_weights(wt_hbm, w_ref, wg_ref, wconv_ref, stage_ref, sem)

    h_ref[...] = _rms(x_ref[...], g_ref[...]).astype(BF16)
    gate_ref[...] = jax.nn.sigmoid(lax.dot_general(h_ref[...], wg_ref[...], _DN_T, preferred_element_type=F32))

    def proj(w, j, width):
        return lax.dot_general(h_ref[...], w[j * width:(j + 1) * width, :], _DN_T, preferred_element_type=F32)

    for j in range(ATTN_WIDTH // QKV_BLOCK):
        q_ref[:, j * QKV_BLOCK:(j + 1) * QKV_BLOCK] = (proj(w_ref, j, QKV_BLOCK) * q_scale).astype(BF16)

    def slab(which):
        return proj(w_ref, ATTN_WIDTH // KV_WIDTH + which, KV_WIDTH)

    ones = jnp.where(lax.broadcasted_iota(jnp.int32, (ONES_ROWS, tm), 0) == 0, 1.0, 0.0).astype(BF16)
    for pair, (k_slab, c_slab, v_slab) in enumerate(((SLAB_KS, SLAB_KC, SLAB_VS), (SLAB_KW, SLAB_VC, SLAB_VW))):
        cols = slice(pair * KV_WIDTH, (pair + 1) * KV_WIDTH)
        k_ref[:, cols] = slab(k_slab).astype(BF16)
        c_ref[:, cols] = slab(c_slab)
        v = slab(v_slab)
        for g in range(N_KV):
            s = pair * N_KV + g
            vt_ref[s, 0:HEAD_DIM, :] = v[:, g * HEAD_DIM:(g + 1) * HEAD_DIM].T.astype(BF16)
            vt_ref[s, HEAD_DIM:HEAD_DIM + ONES_ROWS, :] = ones

    def conv_proj(which, j):
        return proj(wconv_ref, which * (CONV_WIDTH // CONV_BLOCK) + j, CONV_BLOCK)

    row = lax.broadcasted_iota(jnp.int32, (tm, CONV_BLOCK), 0)
    for j in range(CONV_WIDTH // CONV_BLOCK):
        cols = slice(j * CONV_BLOCK, (j + 1) * CONV_BLOCK)
        u = conv_proj(2, j) * conv_proj(0, j)
        prev = carry_ref[j]
        prev = jnp.where(i % tiles_per_seq == 0, 0.0, prev)
        carry_ref[j] = u[tm - SUBLANE:tm, :]
        p1 = prev[SUBLANE - 1:SUBLANE, :]
        p2 = prev[SUBLANE - 2:SUBLANE - 1, :]
        u1 = jnp.where(row == 0, p1, pltpu.roll(u, 1, axis=0))
        u2 = jnp.where(row == 0, p2, jnp.where(row == 1, p1, pltpu.roll(u, 2, axis=0)))
        w = cw_ref[:, cols]
        y = w[0:1, :] * u2
        y = y + w[1:2, :] * u1
        y = y + w[2:3, :] * u
        ov_ref[:, cols] = (conv_proj(1, j) * y).astype(BF16)


def _in_proj(x2, g, w_t, conv_w, batch, seq_len, *, tm=512):
    m = x2.shape[0]
    gw = N_KV * LANE
    tps = seq_len // tm
    n_slabs = QKV_BLOCK // HEAD_DIM
    kern = functools.partial(_inproj_kernel, q_scale=HEAD_DIM ** -0.5 * math.log2(math.e), tiles_per_seq=tps)
    row = lambda i: (i, 0)

    def resident(shape):
        return pl.BlockSpec(shape, lambda i: (0, 0), pipeline_mode=pl.Buffered(1))

    return pl.pallas_call(
        kern,
        grid=(m // tm,),
        in_specs=[
            pl.BlockSpec((tm, D_MODEL), row),
            resident((1, D_MODEL)),
            pl.BlockSpec(memory_space=pl.ANY),
            resident(conv_w.shape),
        ],
        out_specs=[
            pl.BlockSpec((tm, ATTN_WIDTH), row),
            pl.BlockSpec((tm, QKV_BLOCK), row),
            pl.BlockSpec((tm, QKV_BLOCK), row),
            pl.BlockSpec((None, n_slabs, HEAD_DIM + ONES_ROWS, tm), lambda i: (i // tps, 0, 0, i % tps)),
            pl.BlockSpec((tm, gw), row),
            pl.BlockSpec((tm, CONV_WIDTH), row),
        ],
        out_shape=[
            jax.ShapeDtypeStruct((m, ATTN_WIDTH), BF16),
            jax.ShapeDtypeStruct((m, QKV_BLOCK), BF16),
            jax.ShapeDtypeStruct((m, QKV_BLOCK), F32),
            jax.ShapeDtypeStruct((batch, n_slabs, HEAD_DIM + ONES_ROWS, seq_len), BF16),
            jax.ShapeDtypeStruct((m, gw), F32),
            jax.ShapeDtypeStruct((m, CONV_WIDTH), BF16),
        ],
        scratch_shapes=[pltpu.VMEM((tm, D_MODEL), BF16),
                        pltpu.VMEM((CONV_WIDTH // CONV_BLOCK, SUBLANE, CONV_BLOCK), F32),
                        pltpu.VMEM((QKV_WIDTH, D_MODEL), BF16),
                        pltpu.VMEM((gw, D_MODEL), BF16),
                        pltpu.VMEM((3 * CONV_WIDTH, D_MODEL), BF16),
                        pltpu.VMEM((2, W_CHUNK, D_MODEL), F32),
                        pltpu.SemaphoreType.DMA((2,))],
        compiler_params=_params(1),
        name="in_proj",
    )(x2, g, w_t, conv_w)


def _compress_kernel(x_ref, pe_ref, w1k_ref, w1v_ref, w2k_ref, w2v_ref, o_ref, ot_ref):
    n = x_ref.shape[0] // CMP_STRIDE

    def body(w1_ref, w2_ref):
        a = jnp.zeros((n, CMP_HIDDEN), F32)
        b = jnp.zeros((n, CMP_HIDDEN), F32)
        for l in range(CMP_STRIDE):
            xl = x_ref[pl.ds(l, n, stride=CMP_STRIDE), :]
            xa = (xl + pe_ref[l:l + 1, :]).astype(BF16)
            xb = (xl + pe_ref[CMP_STRIDE + l:CMP_STRIDE + l + 1, :]).astype(BF16)
            a = a + jnp.dot(xa, w1_ref[l].astype(BF16), preferred_element_type=F32)
            b = b + jnp.dot(xb, w1_ref[CMP_STRIDE + l].astype(BF16), preferred_element_type=F32)
        pre = a + pltpu.roll(b, n - 1, axis=0)
        hid = pre * jax.nn.sigmoid(pre)
        out = jnp.dot(hid.astype(BF16), w2_ref[...].astype(BF16), preferred_element_type=F32)
        row = lax.broadcasted_iota(jnp.int32, out.shape, 0)
        out = jnp.where(row < n - 1, out, 0.0)
        o_ref[...] = out.astype(BF16)
        ot_ref[...] = out.T.astype(BF16)

    is_value = pl.program_id(1) >= N_KV
    pl.when(jnp.logical_not(is_value))(functools.partial(body, w1k_ref, w2k_ref))
    pl.when(is_value)(functools.partial(body, w1v_ref, w2v_ref))


def _compress(c_in, pe, w1_kv, w2_kv, batch, seq_len):
    n_slabs = c_in.shape[1] // HEAD_DIM
    n_chunks = seq_len // CMP_STRIDE

    def resident(shape):
        return pl.BlockSpec(shape, lambda i, j: (0,) * len(shape), pipeline_mode=pl.Buffered(1))

    return pl.pallas_call(
        _compress_kernel,
        grid=(batch, n_slabs),
        in_specs=[
            pl.BlockSpec((seq_len, HEAD_DIM), lambda i, j: (i, j)),
            resident((CMP_LEN, HEAD_DIM)),
            resident(w1_kv[0].shape), resident(w1_kv[1].shape),
            resident(w2_kv[0].shape), resident(w2_kv[1].shape),
        ],
        out_specs=[
            pl.BlockSpec((None, None, n_chunks, HEAD_DIM), lambda i, j: (i, j, 0, 0)),
            pl.BlockSpec((None, None, HEAD_DIM, n_chunks), lambda i, j: (i, j, 0, 0)),
        ],
        out_shape=[
            jax.ShapeDtypeStruct((batch, n_slabs, n_chunks, HEAD_DIM), BF16),
            jax.ShapeDtypeStruct((batch, n_slabs, HEAD_DIM, n_chunks), BF16),
        ],
        compiler_params=_params(2),
        name="compress",
    )(c_in, pe, *w1_kv, *w2_kv)


TAB_DIAG, TAB_SUB, TAB_FAR, TAB_MASK = 0, 1, 2, 3
BAND_BELOW = 8
ONES_ROWS = 16


N_NSA_INPUTS = 11
BIAS_LEN = 1024
FAR_UNROLL = 8
NSA_TILES_PER_STEP = 2


def _nsa_kernel(*refs, tiles_per_step, **static):
    for u in range(tiles_per_step):
        _nsa_tile(refs, u, tiles_per_step, **static)


def _nsa_tile(refs, u, tiles_per_step, *, tq, n_sel, n_cast):
    (q_ref, kc_ref, vct_ref, ks_ref, vst_ref, kw_ref, vwt_ref, gate_ref, fvec_ref,
     ovl_ref, et_ref) = refs[:N_NSA_INPUTS]
    cast_in = refs[N_NSA_INPUTS:N_NSA_INPUTS + n_cast]
    o_ref = refs[N_NSA_INPUTS + n_cast]
    cast_out = refs[N_NSA_INPUTS + n_cast + 1:N_NSA_INPUTS + 2 * n_cast + 1]
    (kaug_ref, tab_ref, band_ref, sc_ref, s0_ref, s1_ref, m_ref, mt_ref,
     acc_ref) = refs[N_NSA_INPUTS + 2 * n_cast + 1:]
    i = pl.program_id(2) * tiles_per_step + u
    rows = slice(u * tq, (u + 1) * tq)

    tk = tq
    mcols = HPG * tq
    nw = WINDOW // tk
    cpt = tq // CMP_STRIDE
    band_rows = BAND_BELOW + cpt

    def group_start(fn):
        if u == 0:
            pl.when(i == 0)(fn)

    @group_start
    def _():
        kaug_ref[:, 0:HEAD_DIM] = ks_ref[...]
        kaug_ref[:, HEAD_DIM:2 * HEAD_DIM] = et_ref[...]

    @group_start
    def _():
        c = lax.broadcasted_iota(jnp.int32, (tk, tq), 0)
        r = lax.broadcasted_iota(jnp.int32, (tk, tq), 1)
        far = jnp.where(r < c, 0.0, NEG)
        lane = lax.broadcasted_iota(jnp.int32, (1, BIAS_LEN), 1)
        for h in range(HPG):
            cols = slice(h * tq, (h + 1) * tq)
            f = fvec_ref[h]
            f_diag = jnp.where(lane < tq, f, NEG)
            x = pltpu.roll(jnp.broadcast_to(f_diag, (tk, BIAS_LEN)), 0, 1, stride=1, stride_axis=0)
            tab_ref[TAB_DIAG, :, cols] = x[:, 0:tq]
            x = pltpu.roll(jnp.broadcast_to(f, (tk, BIAS_LEN)), 0, 1, stride=1, stride_axis=0)
            tab_ref[TAB_SUB, :, cols] = x[:, tq:2 * tq]
            tab_ref[TAB_FAR, :, cols] = far
            tab_ref[TAB_MASK, :, cols] = jnp.full((tk, tq), NEG, F32)
            for v, first in enumerate((0, -BAND_BELOW)):
                shift = (CMP_STRIDE * first + CMP_LEN - 1) % BIAS_LEN
                f_shift = pltpu.roll(f, shift, 1)
                x = pltpu.roll(jnp.broadcast_to(f_shift, (band_rows, BIAS_LEN)), 0, 1,
                               stride=CMP_STRIDE, stride_axis=0)
                band_ref[v, :, cols] = x[:, 0:tq]

    q = q_ref[rows, :]
    qs = jnp.concatenate([q[:, h * HEAD_DIM:(h + 1) * HEAD_DIM] for h in range(HPG)], axis=0)

    s_refs = (s0_ref, s1_ref)

    def init():
        m_ref[...] = jnp.full((1, mcols), NEG, F32)
        acc_ref[...] = jnp.zeros(acc_ref.shape, F32)

    def keys(kt):
        return pl.ds(pl.multiple_of(kt * tk, tk), tk)

    def qk(qmat, k_ref, kt, kind, buf):
        s = lax.dot_general(k_ref[keys(kt), :], qmat, _DN_T, preferred_element_type=F32)
        if kind is not None:
            s = s + tab_ref[kind]
        s_refs[buf][...] = s
        return jnp.max(s, axis=0, keepdims=True)

    def process(vt_ref, kt, buf, m_tile):
        m_prev = m_ref[...]
        m_next = jnp.maximum(m_prev, m_tile)
        alpha = jnp.exp2(m_prev - m_next)
        p = jnp.exp2(s_refs[buf][...] - m_next).astype(BF16)
        acc_ref[...] = alpha * acc_ref[...] + jnp.dot(vt_ref[:, keys(kt)], p, preferred_element_type=F32)
        m_ref[...] = m_next

    def finish():
        return acc_ref[0:HEAD_DIM, :] * (1.0 / acc_ref[HEAD_DIM:HEAD_DIM + 1, :])

    assert nw == 2, "window tiles are i, i-1 (previous-tile table) and i-nw (window-edge table)"
    init()
    mt_w0 = qk(qs, kw_ref, i, TAB_DIAG, 0)

    if u == 0:
        for src, dst in zip(cast_in, cast_out):
            dst[...] = src[...].astype(BF16)

    raw = lax.dot_general(kc_ref[...], qs, _DN_T, preferred_element_type=F32)
    crow = lax.broadcasted_iota(jnp.int32, raw.shape, 0)
    sc_ref[...] = jnp.where(crow < cpt * (i + 1), raw, NEG)
    band = pl.ds(pl.multiple_of(jnp.maximum(cpt * i - BAND_BELOW, 0), SUBLANE), band_rows)
    sc_ref[band, :] += band_ref[jnp.minimum(i, 1)]

    mt_w1 = qk(qs, kw_ref, jnp.maximum(i - 1, 0), jnp.where(i >= 1, TAB_SUB, TAB_MASK), 1)
    process(vwt_ref, i, 0, mt_w0)

    sc = sc_ref[...]
    mc = jnp.maximum(jnp.max(sc, axis=0, keepdims=True), HALF_NEG)
    pc = jnp.exp2(sc - mc)
    lc = jnp.sum(pc, axis=0, keepdims=True)
    pc = pc * jnp.where(lc > 0.0, 1.0 / lc, 0.0)
    o_c = jnp.dot(vct_ref[...], pc.astype(BF16), preferred_element_type=F32)

    mt_w2 = qk(qs, kw_ref, jnp.maximum(i - nw, 0), jnp.where(i >= nw, TAB_FAR, TAB_MASK), 0)
    process(vwt_ref, jnp.maximum(i - 1, 0), 1, mt_w1)

    ps = pc[:, 0:tq] + pc[:, tq:2 * tq] + pc[:, 2 * tq:3 * tq] + pc[:, 3 * tq:4 * tq]
    hi = ps.astype(BF16)
    r1 = ps - hi.astype(F32)
    mid = r1.astype(BF16)
    lo = (r1 - mid.astype(F32)).astype(BF16)
    ovl = ovl_ref[...]
    imp = (jnp.dot(ovl, hi, preferred_element_type=F32) + jnp.dot(ovl, mid, preferred_element_type=F32)
           + jnp.dot(ovl, lo, preferred_element_type=F32))
    jj = lax.broadcasted_iota(jnp.int32, (n_sel, tq), 0)
    tt = i * tq + lax.broadcasted_iota(jnp.int32, (n_sel, tq), 1)
    cur = tt >> int(math.log2(SEL_LEN))
    forced = (jj == 0) | (jj == cur) | (jj == cur - 1)
    imp = jnp.where(forced, FORCE, imp)
    imp = jnp.where(jj * SEL_LEN <= tt, imp, NEG)
    sub = SUBLANE
    ranks = []
    for j0 in range(0, n_sel, sub):
        blk = imp[j0:j0 + sub, :]
        jl = j0 + lax.broadcasted_iota(jnp.int32, blk.shape, 0)
        cnt = jnp.zeros(blk.shape, jnp.int32)
        for b in range(n_sel):
            row = imp[b:b + 1, :]
            if b < j0:
                cnt = cnt + jnp.where(row >= blk, 1, 0)
            elif b >= j0 + sub:
                cnt = cnt + jnp.where(row > blk, 1, 0)
            else:
                cnt = cnt + jnp.where(row > blk, 1, jnp.where(row == blk, jnp.where(jl > b, 1, 0), 0))
        ranks.append(cnt)
    rank = jnp.concatenate(ranks, axis=0)
    selb_t = jnp.where(rank < SEL_TOPK, 0.0, NEG)

    process(vwt_ref, jnp.maximum(i - nw, 0), 0, mt_w2)
    o_w = finish()

    selb = jnp.concatenate([selb_t, jnp.zeros((LANE - n_sel, tq), F32)], axis=0).T
    selb = selb.astype(BF16)
    qa = jnp.concatenate([qs, jnp.concatenate([selb] * HPG, axis=0)], axis=1)

    init()

    @pl.when(i == 0)
    def _():
        process(vst_ref, 0, 0, qk(qa, kaug_ref, 0, TAB_DIAG, 0))

    @pl.when(i >= 1)
    def _():
        mt0 = qk(qa, kaug_ref, i, TAB_DIAG, 0)
        mt1 = qk(qa, kaug_ref, i - 1, TAB_SUB, 1)
        process(vst_ref, i, 0, mt0)
        mt0 = qk(qa, kaug_ref, jnp.maximum(i - 2, 0), None, 0)
        process(vst_ref, i - 1, 1, mt1)
        n_far = i - 1
        rem = n_far % FAR_UNROLL

        def far_tiles(a, n, mt0):
            for t in range(n):
                nxt = qk(qa, kaug_ref, jnp.maximum(a - t - 1, 0), None, (t + 1) % 2)
                process(vst_ref, a - t, t % 2, mt0)
                mt0 = nxt
            return mt0

        mt0 = lax.fori_loop(0, n_far // FAR_UNROLL,
                            lambda p, mt: far_tiles(i - 2 - FAR_UNROLL * p, FAR_UNROLL, mt), mt0)

        mt_ref[...] = mt0
        size = FAR_UNROLL // 2
        while size >= 2:
            @pl.when((rem & size) != 0)
            def _(size=size):
                mt_ref[...] = far_tiles((rem & (2 * size - 1)) - 1, size, mt_ref[...])
            size //= 2

        @pl.when((rem & 1) != 0)
        def _():
            process(vst_ref, 0, 0, mt_ref[...])

    o_s = finish()

    gt = gate_ref[rows, :].T
    for h in range(HPG):
        sl = slice(h * tq, (h + 1) * tq)
        o = gt[3 * h:3 * h + 1, :] * o_c[:, sl]
        o = o + gt[3 * h + 1:3 * h + 2, :] * o_s[:, sl]
        o = o + gt[3 * h + 2:3 * h + 3, :] * o_w[:, sl]
        o_ref[rows, h * HEAD_DIM:(h + 1) * HEAD_DIM] = o.T.astype(BF16)


def _nsa(q, k_all, vt_all, cmp, cmp_t, gates, fvec, ovl, e_t, cast_weights, batch, seq_len, *, tq):
    n_t = seq_len // (tq * NSA_TILES_PER_STEP)
    n_steps = batch * N_KV * n_t
    ts = tq * NSA_TILES_PER_STEP
    step_row = lambda b, g, i: ((b * N_KV + g) * n_t + i, 0)
    cast_specs = [pl.BlockSpec((w.shape[0] // n_steps, w.shape[1]), step_row) for w in cast_weights]
    n_sel = seq_len // SEL_LEN
    n_chunks = cmp.shape[2]
    mcols = HPG * tq
    band_rows = BAND_BELOW + tq // CMP_STRIDE
    qw = HPG * HEAD_DIM

    def vt_spec(first):
        return pl.BlockSpec((None, None, HEAD_DIM + ONES_ROWS, seq_len), lambda b, g, i: (b, first + g, 0, 0))

    kern = functools.partial(_nsa_kernel, tiles_per_step=NSA_TILES_PER_STEP, tq=tq, n_sel=n_sel,
                             n_cast=len(cast_weights))
    o_attn, *cast = pl.pallas_call(
        kern,
        grid=(batch, N_KV, n_t),
        in_specs=[
            pl.BlockSpec((ts, qw), lambda b, g, i: (b * n_t + i, g)),
            pl.BlockSpec((None, None, n_chunks, HEAD_DIM), lambda b, g, i: (b, g, 0, 0)),
            pl.BlockSpec((None, None, HEAD_DIM, n_chunks), lambda b, g, i: (b, N_KV + g, 0, 0)),
            pl.BlockSpec((seq_len, HEAD_DIM), lambda b, g, i: (b, g)),
            vt_spec(0),
            pl.BlockSpec((seq_len, HEAD_DIM), lambda b, g, i: (b, N_KV + g)),
            vt_spec(N_KV),
            pl.BlockSpec((ts, LANE), lambda b, g, i: (b * n_t + i, g)),
            pl.BlockSpec((None, HPG, 1, BIAS_LEN), lambda b, g, i: (g, 0, 0, 0)),
            pl.BlockSpec((n_sel, n_chunks), lambda b, g, i: (0, 0)),
            pl.BlockSpec((seq_len, LANE), lambda b, g, i: (0, 0)),
        ] + cast_specs,
        out_specs=[pl.BlockSpec((ts, qw), lambda b, g, i: (b * n_t + i, g))] + cast_specs,
        out_shape=[jax.ShapeDtypeStruct((batch * seq_len, ATTN_WIDTH), BF16)]
        + [jax.ShapeDtypeStruct(w.shape, BF16) for w in cast_weights],
        scratch_shapes=[
            pltpu.VMEM((seq_len, 2 * HEAD_DIM), BF16),
            pltpu.VMEM((4, tq, mcols), F32),
            pltpu.VMEM((2, band_rows, mcols), F32),
            pltpu.VMEM((n_chunks, mcols), F32),
            pltpu.VMEM((tq, mcols), F32),
            pltpu.VMEM((tq, mcols), F32),
            pltpu.VMEM((1, mcols), F32),
            pltpu.VMEM((1, mcols), F32),
            pltpu.VMEM((HEAD_DIM + ONES_ROWS, mcols), F32),
        ],
        compiler_params=_params(3),
        name="nsa_attention",
    )(q, cmp, cmp_t, k_all, vt_all, k_all, vt_all, gates, fvec, ovl, e_t, *cast_weights)
    return o_attn, cast


OPROJ_CHUNKS = 4


def _oproj_kernel(oa_ref, ov_ref, wo_ref, x_ref, g1_ref, g2_ref, x1_ref, h2_ref):
    ka = oa_ref.shape[1]
    tm = oa_ref.shape[0]
    chunk = tm // OPROJ_CHUNKS
    for rows in (slice(k * chunk, (k + 1) * chunk) for k in range(OPROJ_CHUNKS)):
        mix = jnp.dot(oa_ref[rows, :], wo_ref[0:ka, :], preferred_element_type=F32)
        mix = mix + jnp.dot(ov_ref[rows, :], wo_ref[ka:, :], preferred_element_type=F32)
        x1 = x_ref[rows, :] + _rms(mix, g1_ref[...])
        x1_ref[rows, :] = x1
        h2_ref[rows, :] = _rms(x1, g2_ref[...]).astype(BF16)


def _oproj(o_attn, o_conv, w_o, x2, g_post, g_pre, *, tm=512):
    m = x2.shape[0]
    ka, kv = o_attn.shape[1], o_conv.shape[1]
    row = lambda i: (i, 0)
    fixed = lambda i: (0, 0)
    return pl.pallas_call(
        _oproj_kernel,
        grid=(m // tm,),
        in_specs=[
            pl.BlockSpec((tm, ka), row),
            pl.BlockSpec((tm, kv), row),
            pl.BlockSpec((ka + kv, D_MODEL), fixed),
            pl.BlockSpec((tm, D_MODEL), row),
            pl.BlockSpec((1, D_MODEL), fixed),
            pl.BlockSpec((1, D_MODEL), fixed),
        ],
        out_specs=[pl.BlockSpec((tm, D_MODEL), row), pl.BlockSpec((tm, D_MODEL), row)],
        out_shape=[jax.ShapeDtypeStruct((m, D_MODEL), F32), jax.ShapeDtypeStruct((m, D_MODEL), BF16)],
        compiler_params=_params(1),
        name="out_proj",
    )(o_attn, o_conv, w_o, x2, g_post, g_pre)


FFN_SPLIT = 2


def _ffn_kernel(h_ref, wu_ref, wd_ref, x1_hbm, g_ref, o_ref, x1_ref, sem):
    i = pl.program_id(0)
    j = pl.program_id(1)
    last = pl.num_programs(1) - 1
    tm = h_ref.shape[0]
    width = wu_ref.shape[1] // FFN_SPLIT
    x1_copy = pltpu.make_async_copy(x1_hbm.at[pl.ds(pl.multiple_of(i * tm, tm), tm), :], x1_ref, sem)

    def step(mode):
        if mode == "first":
            x1_copy.start()
        for k in range(FFN_SPLIT):
            cols = slice(k * width, (k + 1) * width)
            a = jnp.dot(h_ref[...], wu_ref[:, cols], preferred_element_type=F32)
            a = jnp.square(jnp.maximum(a, 0.0)).astype(BF16)
            if mode == "first" and k == 0:
                o_ref[...] = jnp.dot(a, wd_ref[cols, :], preferred_element_type=F32)
            elif mode != "last" or k < FFN_SPLIT - 1:
                o_ref[...] += jnp.dot(a, wd_ref[cols, :], preferred_element_type=F32)
            else:
                x1_copy.wait()
                for rows in (slice(0, tm // 2), slice(tm // 2, tm)):
                    f = o_ref[rows, :] + jnp.dot(a[rows, :], wd_ref[cols, :], preferred_element_type=F32)
                    o_ref[rows, :] = x1_ref[rows, :] + _rms(f, g_ref[...])

    pl.when(j == 0)(functools.partial(step, "first"))
    pl.when((j > 0) & (j < last))(functools.partial(step, "middle"))
    pl.when(j == last)(functools.partial(step, "last"))


def _ffn(h2, w_up, w_down, x1, g_post, *, tm=512, tf=2048):
    m = h2.shape[0]
    d_ff = w_up.shape[1]
    assert d_ff // tf >= 2
    return pl.pallas_call(
        _ffn_kernel,
        grid=(m // tm, d_ff // tf),
        in_specs=[
            pl.BlockSpec((tm, D_MODEL), lambda i, j: (i, 0)),
            pl.BlockSpec((D_MODEL, tf), lambda i, j: (0, j)),
            pl.BlockSpec((tf, D_MODEL), lambda i, j: (j, 0)),
            pl.BlockSpec(memory_space=pl.ANY),
            pl.BlockSpec((1, D_MODEL), lambda i, j: (0, 0)),
        ],
        out_specs=pl.BlockSpec((tm, D_MODEL), lambda i, j: (i, 0)),
        out_shape=jax.ShapeDtypeStruct((m, D_MODEL), F32),
        scratch_shapes=[pltpu.VMEM((tm, D_MODEL), F32), pltpu.SemaphoreType.DMA(())],
        compiler_params=_params(2),
        name="ffn",
    )(h2, w_up, w_down, x1, g_post)


def _bucket_np(dist):
    n = np.maximum(dist, 0)
    max_exact = N_BUCKETS // 2
    nf = np.maximum(n, 1).astype(np.float32)
    large = max_exact + (np.log(nf / np.float32(max_exact)) / np.float32(math.log(MAX_DIST / max_exact))
                         * np.float32(N_BUCKETS - max_exact)).astype(np.int32)
    large = np.minimum(large, N_BUCKETS - 1)
    return np.where(n < max_exact, n, large)


def _bucket_starts():
    b = _bucket_np(np.arange(4 * MAX_DIST))
    return [int(np.argmax(b == k)) for k in range(N_BUCKETS)]


def _attention_tables(rel_bias, seq_len, tq):
    starts = _bucket_starts()
    assert starts[N_BUCKETS - 1] <= CMP_STRIDE * (BAND_BELOW + 1) - (CMP_LEN - 1)
    assert starts[N_BUCKETS - 1] <= tq and 2 * tq <= BIAS_LEN // 2
    rel = (rel_bias - rel_bias[:, N_BUCKETS - 1:]) * math.log2(math.e)
    d = jnp.arange(BIAS_LEN, dtype=jnp.int32)[None, :]
    fvec = jnp.broadcast_to(rel[:, 0:1], (N_HEADS, BIAS_LEN))
    for k in range(1, N_BUCKETS):
        fvec = jnp.where(d >= starts[k], rel[:, k:k + 1], fvec)
    fvec = jnp.where(d < BIAS_LEN // 2, fvec, NEG).astype(F32).reshape(N_KV, HPG, 1, BIAS_LEN)

    n_chunks = seq_len // CMP_STRIDE
    n_cmp = (seq_len - CMP_LEN) // CMP_STRIDE + 1
    n_sel = seq_len // SEL_LEN
    ci = np.arange(n_chunks)[None, :] * CMP_STRIDE
    sj = np.arange(n_sel)[:, None] * SEL_LEN
    ovl = ((ci < sj + SEL_LEN) & (ci + CMP_LEN > sj) & (np.arange(n_chunks)[None, :] < n_cmp))
    e_t = (np.arange(seq_len)[:, None] // SEL_LEN == np.arange(LANE)[None, :])
    return fvec, jnp.asarray(ovl, BF16), jnp.asarray(e_t, BF16)


def kernel(x, w_in, pe_cmp, w_cmp_k1, w_cmp_k2, w_cmp_v1, w_cmp_v2, conv_w, rel_bias, w_o, w_up, w_down,
           g_pre_mix, g_post_mix, g_pre_ffn, g_post_ffn):
    batch, seq_len, _ = x.shape
    depth = w_in.shape[0]
    tq = 256
    fvec, ovl, e_t = _attention_tables(rel_bias, seq_len, tq)
    x2 = x.reshape(batch * seq_len, D_MODEL)
    for l in range(depth):
        wl = jnp.swapaxes(w_in[l], 0, 1)
        g1 = g_pre_mix[l].reshape(1, D_MODEL)

        q, k_all, c_in, vt_all, gates, o_conv = _in_proj(x2, g1, wl, conv_w[l], batch, seq_len)

        cmp, cmp_t = _compress(c_in, pe_cmp[l], (w_cmp_k1[l], w_cmp_v1[l]), (w_cmp_k2[l], w_cmp_v2[l]),
                               batch, seq_len)

        o_attn, (wo_b, wup_b, wdown_b) = _nsa(q, k_all, vt_all, cmp, cmp_t, gates, fvec, ovl, e_t,
                                              [w_o[l], w_up[l], w_down[l]], batch, seq_len, tq=tq)

        x1, h2 = _oproj(o_attn, o_conv, wo_b, x2,
                        g_post_mix[l].reshape(1, D_MODEL), g_pre_ffn[l].reshape(1, D_MODEL))
        x2 = _ffn(h2, wup_b, wdown_b, x1, g_post_ffn[l].reshape(1, D_MODEL))
    return x2.reshape(batch, seq_len, D_MODEL)
```

```python
import functools
import math

import numpy as np
import jax
import jax.numpy as jnp
from jax import lax
from jax.experimental import pallas as pl
from jax.experimental.pallas import tpu as pltpu

F32 = jnp.float32
BF16 = jnp.bfloat16

D_MODEL = 2048
N_HEADS = 8
N_KV = 2
HPG = N_HEADS // N_KV
HEAD_DIM = 128
ATTN_WIDTH = N_HEADS * HEAD_DIM
KV_WIDTH = N_KV * HEAD_DIM
CONV_WIDTH = D_MODEL - ATTN_WIDTH
CONV_K = 3
N_BRANCH = 3
CMP_LEN = 32
CMP_STRIDE = 16
CMP_HIDDEN = 256
SEL_LEN = 64
SEL_TOPK = 16
WINDOW = 512
N_BUCKETS = 32
MAX_DIST = 128
EPS = 1e-6
NEG = -1e30
HALF_NEG = -5e29
FORCE = 1e9

QKV_WIDTH = ATTN_WIDTH + 6 * KV_WIDTH
GATE_OFF = QKV_WIDTH
CONV_OFF = QKV_WIDTH + N_HEADS * N_BRANCH
LANE = 128
VMEM_LIMIT = 56 * 1024 * 1024

_DN_T = (((1,), (1,)), ((), ()))


def _rms(x, g):
    ms = jnp.mean(x * x, axis=-1, keepdims=True)
    return x * lax.rsqrt(ms + EPS) * g


def _params(n_axes):
    return pltpu.CompilerParams(dimension_semantics=("arbitrary",) * n_axes, vmem_limit_bytes=VMEM_LIMIT)


QKV_BLOCK = 2 * KV_WIDTH
SLAB_KC, SLAB_VC, SLAB_KS, SLAB_VS, SLAB_KW, SLAB_VW = range(6)
CONV_BLOCK = 512
W_CHUNK = 256
SUBLANE = 8


def _load_weights(wt_hbm, w_ref, wg_ref, wconv_ref, stage_ref, sem):
    n_gate = N_HEADS * N_BRANCH
    per_group = HPG * N_BRANCH

    def store_rows(dst, row):
        def store(v):
            dst[row:row + v.shape[0], :] = v.astype(BF16)
        return store

    def store_gates(v):
        wg_ref[...] = jnp.zeros(wg_ref.shape, BF16)
        pad = jnp.zeros((2 * SUBLANE - per_group, v.shape[1]), F32)
        for g in range(N_KV):
            rows = jnp.concatenate([v[g * per_group:(g + 1) * per_group, :], pad], axis=0)
            wg_ref[g * LANE:g * LANE + 2 * SUBLANE, :] = rows.astype(BF16)

    chunks = [(r, W_CHUNK, store_rows(w_ref, r)) for r in range(0, QKV_WIDTH, W_CHUNK)]
    chunks.append((GATE_OFF, n_gate, store_gates))
    chunks += [(CONV_OFF + r, W_CHUNK, store_rows(wconv_ref, r)) for r in range(0, 3 * CONV_WIDTH, W_CHUNK)]

    def copy(k):
        src, n, _ = chunks[k]
        slot = k % 2
        return pltpu.make_async_copy(wt_hbm.at[pl.ds(src, n), :], stage_ref.at[slot, pl.ds(0, n), :], sem.at[slot])

    copy(0).start()
    for k, (_, n, store) in enumerate(chunks):
        if k + 1 < len(chunks):
            copy(k + 1).start()
        copy(k).wait()
        store(stage_ref[k % 2, 0:n, :])


def _inproj_kernel(x_ref, g_ref, wt_hbm, cw_ref, q_ref, k_ref, c_ref, vt_ref, gate_ref, ov_ref,
                   h_ref, carry_ref, w_ref, wg_ref, wconv_ref, stage_ref, sem, *, q_scale, tiles_per_seq):
    i = pl.program_id(0)
    tm = x_ref.shape[0]

    @pl.when(i == 0)
    def _():
        _load_weights(wt_hbm, w_ref, wg_ref, wconv_ref, stage_ref, sem)

    h_ref[...] = _rms(x_ref[...], g_ref[...]).astype(BF16)
    gate_ref[...] = jax.nn.sigmoid(lax.dot_general(h_ref[...], wg_ref[...], _DN_T, preferred_element_type=F32))

    def proj(w, j, width):
        return lax.dot_general(h_ref[...], w[j * width:(j + 1) * width, :], _DN_T, preferred_element_type=F32)

    for j in range(ATTN_WIDTH // QKV_BLOCK):
        q_ref[:, j * QKV_BLOCK:(j + 1) * QKV_BLOCK] = (proj(w_ref, j, QKV_BLOCK) * q_scale).astype(BF16)

    def slab(which):
        return proj(w_ref, ATTN_WIDTH // KV_WIDTH + which, KV_WIDTH)

    ones = jnp.where(lax.broadcasted_iota(jnp.int32, (ONES_ROWS, tm), 0) == 0, 1.0, 0.0).astype(BF16)
    for pair, (k_slab, c_slab, v_slab) in enumerate(((SLAB_KS, SLAB_KC, SLAB_VS), (SLAB_KW, SLAB_VC, SLAB_VW))):
        cols = slice(pair * KV_WIDTH, (pair + 1) * KV_WIDTH)
        k_ref[:, cols] = slab(k_slab).astype(BF16)
        c_ref[:, cols] = slab(c_slab)
        v = slab(v_slab)
        for g in range(N_KV):
            s = pair * N_KV + g
            vt_ref[s, 0:HEAD_DIM, :] = v[:, g * HEAD_DIM:(g + 1) * HEAD_DIM].T.astype(BF16)
            vt_ref[s, HEAD_DIM:HEAD_DIM + ONES_ROWS, :] = ones

    def conv_proj(which, j):
        return proj(wconv_ref, which * (CONV_WIDTH // CONV_BLOCK) + j, CONV_BLOCK)

    row = lax.broadcasted_iota(jnp.int32, (tm, CONV_BLOCK), 0)
    for j in range(CONV_WIDTH // CONV_BLOCK):
        cols = slice(j * CONV_BLOCK, (j + 1) * CONV_BLOCK)
        u = conv_proj(2, j) * conv_proj(0, j)
        prev = carry_ref[j]
        prev = jnp.where(i % tiles_per_seq == 0, 0.0, prev)
        carry_ref[j] = u[tm - SUBLANE:tm, :]
        p1 = prev[SUBLANE - 1:SUBLANE, :]
        p2 = prev[SUBLANE - 2:SUBLANE - 1, :]
        u1 = jnp.where(row == 0, p1, pltpu.roll(u, 1, axis=0))
        u2 = jnp.where(row == 0, p2, jnp.where(row == 1, p1, pltpu.roll(u, 2, axis=0)))
        w = cw_ref[:, cols]
        y = w[0:1, :] * u2
        y = y + w[1:2, :] * u1
        y = y + w[2:3, :] * u
        ov_ref[:, cols] = (conv_proj(1, j) * y).astype(BF16)


def _in_proj(x2, g, w_t, conv_w, batch, seq_len, *, tm=512):
    m = x2.shape[0]
    gw = N_KV * LANE
    tps = seq_len // tm
    n_slabs = QKV_BLOCK // HEAD_DIM
    kern = functools.partial(_inproj_kernel, q_scale=HEAD_DIM ** -0.5 * math.log2(math.e), tiles_per_seq=tps)
    row = lambda i: (i, 0)

    def resident(shape):
        return pl.BlockSpec(shape, lambda i: (0, 0), pipeline_mode=pl.Buffered(1))

    return pl.pallas_call(
        kern,
        grid=(m // tm,),
        in_specs=[
            pl.BlockSpec((tm, D_MODEL), row),
            resident((1, D_MODEL)),
            pl.BlockSpec(memory_space=pl.ANY),
            resident(conv_w.shape),
        ],
        out_specs=[
            pl.BlockSpec((tm, ATTN_WIDTH), row),
            pl.BlockSpec((tm, QKV_BLOCK), row),
            pl.BlockSpec((tm, QKV_BLOCK), row),
            pl.BlockSpec((None, n_slabs, HEAD_DIM + ONES_ROWS, tm), lambda i: (i // tps, 0, 0, i % tps)),
            pl.BlockSpec((tm, gw), row),
            pl.BlockSpec((tm, CONV_WIDTH), row),
        ],
        out_shape=[
            jax.ShapeDtypeStruct((m, ATTN_WIDTH), BF16),
            jax.ShapeDtypeStruct((m, QKV_BLOCK), BF16),
            jax.ShapeDtypeStruct((m, QKV_BLOCK), F32),
            jax.ShapeDtypeStruct((batch, n_slabs, HEAD_DIM + ONES_ROWS, seq_len), BF16),
            jax.ShapeDtypeStruct((m, gw), F32),
            jax.ShapeDtypeStruct((m, CONV_WIDTH), BF16),
        ],
        scratch_shapes=[pltpu.VMEM((tm, D_MODEL), BF16),
                        pltpu.VMEM((CONV_WIDTH // CONV_BLOCK, SUBLANE, CONV_BLOCK), F32),
                        pltpu.VMEM((QKV_WIDTH, D_MODEL), BF16),
                        pltpu.VMEM((gw, D_MODEL), BF16),
                        pltpu.VMEM((3 * CONV_WIDTH, D_MODEL), BF16),
                        pltpu.VMEM((2, W_CHUNK, D_MODEL), F32),
                        pltpu.SemaphoreType.DMA((2,))],
        compiler_params=_params(1),
        name="in_proj",
    )(x2, g, w_t, conv_w)


def _compress_kernel(x_ref, pe_ref, w1k_ref, w1v_ref, w2k_ref, w2v_ref, o_ref, ot_ref):
    n = x_ref.shape[0] // CMP_STRIDE

    def body(w1_ref, w2_ref):
        a = jnp.zeros((n, CMP_HIDDEN), F32)
        b = jnp.zeros((n, CMP_HIDDEN), F32)
        for l in range(CMP_STRIDE):
            xl = x_ref[pl.ds(l, n, stride=CMP_STRIDE), :]
            xa = (xl + pe_ref[l:l + 1, :]).astype(BF16)
            xb = (xl + pe_ref[CMP_STRIDE + l:CMP_STRIDE + l + 1, :]).astype(BF16)
            a = a + jnp.dot(xa, w1_ref[l].astype(BF16), preferred_element_type=F32)
            b = b + jnp.dot(xb, w1_ref[CMP_STRIDE + l].astype(BF16), preferred_element_type=F32)
        pre = a + pltpu.roll(b, n - 1, axis=0)
        hid = pre * jax.nn.sigmoid(pre)
        out = jnp.dot(hid.astype(BF16), w2_ref[...].astype(BF16), preferred_element_type=F32)
        row = lax.broadcasted_iota(jnp.int32, out.shape, 0)
        out = jnp.where(row < n - 1, out, 0.0)
        o_ref[...] = out.astype(BF16)
        ot_ref[...] = out.T.astype(BF16)

    is_value = pl.program_id(1) >= N_KV
    pl.when(jnp.logical_not(is_value))(functools.partial(body, w1k_ref, w2k_ref))
    pl.when(is_value)(functools.partial(body, w1v_ref, w2v_ref))


def _compress(c_in, pe, w1_kv, w2_kv, batch, seq_len):
    n_slabs = c_in.shape[1] // HEAD_DIM
    n_chunks = seq_len // CMP_STRIDE

    def resident(shape):
        return pl.BlockSpec(shape, lambda i, j: (0,) * len(shape), pipeline_mode=pl.Buffered(1))

    return pl.pallas_call(
        _compress_kernel,
        grid=(batch, n_slabs),
        in_specs=[
            pl.BlockSpec((seq_len, HEAD_DIM), lambda i, j: (i, j)),
            resident((CMP_LEN, HEAD_DIM)),
            resident(w1_kv[0].shape), resident(w1_kv[1].shape),
            resident(w2_kv[0].shape), resident(w2_kv[1].shape),
        ],
        out_specs=[
            pl.BlockSpec((None, None, n_chunks, HEAD_DIM), lambda i, j: (i, j, 0, 0)),
            pl.BlockSpec((None, None, HEAD_DIM, n_chunks), lambda i, j: (i, j, 0, 0)),
        ],
        out_shape=[
            jax.ShapeDtypeStruct((batch, n_slabs, n_chunks, HEAD_DIM), BF16),
            jax.ShapeDtypeStruct((batch, n_slabs, HEAD_DIM, n_chunks), BF16),
        ],
        compiler_params=_params(2),
        name="compress",
    )(c_in, pe, *w1_kv, *w2_kv)


TAB_DIAG, TAB_SUB, TAB_FAR, TAB_MASK = 0, 1, 2, 3
BAND_BELOW = 8
ONES_ROWS = 16


N_NSA_INPUTS = 11
BIAS_LEN = 1024
FAR_UNROLL = 8
NSA_TILES_PER_STEP = 2


def _nsa_kernel(*refs, tiles_per_step, **static):
    for u in range(tiles_per_step):
        _nsa_tile(refs, u, tiles_per_step, **static)


def _nsa_tile(refs, u, tiles_per_step, *, tq, n_sel, n_cast):
    (q_ref, kc_ref, vct_ref, ks_ref, vst_ref, kw_ref, vwt_ref, gate_ref, fvec_ref,
     ovl_ref, et_ref) = refs[:N_NSA_INPUTS]
    cast_in = refs[N_NSA_INPUTS:N_NSA_INPUTS + n_cast]
    o_ref = refs[N_NSA_INPUTS + n_cast]
    cast_out = refs[N_NSA_INPUTS + n_cast + 1:N_NSA_INPUTS + 2 * n_cast + 1]
    (kaug_ref, tab_ref, band_ref, sc_ref, s0_ref, s1_ref, m_ref, mt_ref,
     acc_ref) = refs[N_NSA_INPUTS + 2 * n_cast + 1:]
    i = pl.program_id(2) * tiles_per_step + u
    rows = slice(u * tq, (u + 1) * tq)

    tk = tq
    mcols = HPG * tq
    nw = WINDOW // tk
    cpt = tq // CMP_STRIDE
    band_rows = BAND_BELOW + cpt

    def group_start(fn):
        if u == 0:
            pl.when(i == 0)(fn)

    @group_start
    def _():
        kaug_ref[:, 0:HEAD_DIM] = ks_ref[...]
        kaug_ref[:, HEAD_DIM:2 * HEAD_DIM] = et_ref[...]

    @group_start
    def _():
        c = lax.broadcasted_iota(jnp.int32, (tk, tq), 0)
        r = lax.broadcasted_iota(jnp.int32, (tk, tq), 1)
        far = jnp.where(r < c, 0.0, NEG)
        lane = lax.broadcasted_iota(jnp.int32, (1, BIAS_LEN), 1)
        for h in range(HPG):
            cols = slice(h * tq, (h + 1) * tq)
            f = fvec_ref[h]
            f_diag = jnp.where(lane < tq, f, NEG)
            x = pltpu.roll(jnp.broadcast_to(f_diag, (tk, BIAS_LEN)), 0, 1, stride=1, stride_axis=0)
            tab_ref[TAB_DIAG, :, cols] = x[:, 0:tq]
            x = pltpu.roll(jnp.broadcast_to(f, (tk, BIAS_LEN)), 0, 1, stride=1, stride_axis=0)
            tab_ref[TAB_SUB, :, cols] = x[:, tq:2 * tq]
            tab_ref[TAB_FAR, :, cols] = far
            tab_ref[TAB_MASK, :, cols] = jnp.full((tk, tq), NEG, F32)
            for v, first in enumerate((0, -BAND_BELOW)):
                shift = (CMP_STRIDE * first + CMP_LEN - 1) % BIAS_LEN
                f_shift = pltpu.roll(f, shift, 1)
                x = pltpu.roll(jnp.broadcast_to(f_shift, (band_rows, BIAS_LEN)), 0, 1,
                               stride=CMP_STRIDE, stride_axis=0)
                band_ref[v, :, cols] = x[:, 0:tq]

    q = q_ref[rows, :]
    qs = jnp.concatenate([q[:, h * HEAD_DIM:(h + 1) * HEAD_DIM] for h in range(HPG)], axis=0)

    s_refs = (s0_ref, s1_ref)

    def init():
        m_ref[...] = jnp.full((1, mcols), NEG, F32)
        acc_ref[...] = jnp.zeros(acc_ref.shape, F32)

    def keys(kt):
        return pl.ds(pl.multiple_of(kt * tk, tk), tk)

    def qk(qmat, k_ref, kt, kind, buf):
        s = lax.dot_general(k_ref[keys(kt), :], qmat, _DN_T, preferred_element_type=F32)
        if kind is not None:
            s = s + tab_ref[kind]
        s_refs[buf][...] = s
        return jnp.max(s, axis=0, keepdims=True)

    def process(vt_ref, kt, buf, m_tile):
        m_prev = m_ref[...]
        m_next = jnp.maximum(m_prev, m_tile)
        alpha = jnp.exp2(m_prev - m_next)
        p = jnp.exp2(s_refs[buf][...] - m_next).astype(BF16)
        acc_ref[...] = alpha * acc_ref[...] + jnp.dot(vt_ref[:, keys(kt)], p, preferred_element_type=F32)
        m_ref[...] = m_next

    def finish():
        return acc_ref[0:HEAD_DIM, :] * (1.0 / acc_ref[HEAD_DIM:HEAD_DIM + 1, :])

    assert nw == 2, "window tiles are i, i-1 (previous-tile table) and i-nw (window-edge table)"
    init()
    mt_w0 = qk(qs, kw_ref, i, TAB_DIAG, 0)

    if u == 0:
        for src, dst in zip(cast_in, cast_out):
            dst[...] = src[...].astype(BF16)

    raw = lax.dot_general(kc_ref[...], qs, _DN_T, preferred_element_type=F32)
    crow = lax.broadcasted_iota(jnp.int32, raw.shape, 0)
    sc_ref[...] = jnp.where(crow < cpt * (i + 1), raw, NEG)
    band = pl.ds(pl.multiple_of(jnp.maximum(cpt * i - BAND_BELOW, 0), SUBLANE), band_rows)
    sc_ref[band, :] += band_ref[jnp.minimum(i, 1)]

    mt_w1 = qk(qs, kw_ref, jnp.maximum(i - 1, 0), jnp.where(i >= 1, TAB_SUB, TAB_MASK), 1)
    process(vwt_ref, i, 0, mt_w0)

    sc = sc_ref[...]
    mc = jnp.maximum(jnp.max(sc, axis=0, keepdims=True), HALF_NEG)
    pc = jnp.exp2(sc - mc)
    lc = jnp.sum(pc, axis=0, keepdims=True)
    pc = pc * jnp.where(lc > 0.0, 1.0 / lc, 0.0)
    o_c = jnp.dot(vct_ref[...], pc.astype(BF16), preferred_element_type=F32)

    mt_w2 = qk(qs, kw_ref, jnp.maximum(i - nw, 0), jnp.where(i >= nw, TAB_FAR, TAB_MASK), 0)
    process(vwt_ref, jnp.maximum(i - 1, 0), 1, mt_w1)

    ps = pc[:, 0:tq] + pc[:, tq:2 * tq] + pc[:, 2 * tq:3 * tq] + pc[:, 3 * tq:4 * tq]
    hi = ps.astype(BF16)
    r1 = ps - hi.astype(F32)
    mid = r1.astype(BF16)
    lo = (r1 - mid.astype(F32)).astype(BF16)
    ovl = ovl_ref[...]
    imp = (jnp.dot(ovl, hi, preferred_element_type=F32) + jnp.dot(ovl, mid, preferred_element_type=F32)
           + jnp.dot(ovl, lo, preferred_element_type=F32))
    jj = lax.broadcasted_iota(jnp.int32, (n_sel, tq), 0)
    tt = i * tq + lax.broadcasted_iota(jnp.int32, (n_sel, tq), 1)
    cur = tt >> int(math.log2(SEL_LEN))
    forced = (jj == 0) | (jj == cur) | (jj == cur - 1)
    imp = jnp.where(forced, FORCE, imp)
    imp = jnp.where(jj * SEL_LEN <= tt, imp, NEG)
    sub = SUBLANE
    ranks = []
    for j0 in range(0, n_sel, sub):
        blk = imp[j0:j0 + sub, :]
        jl = j0 + lax.broadcasted_iota(jnp.int32, blk.shape, 0)
        cnt = jnp.zeros(blk.shape, jnp.int32)
        for b in range(n_sel):
            row = imp[b:b + 1, :]
            if b < j0:
                cnt = cnt + jnp.where(row >= blk, 1, 0)
            elif b >= j0 + sub:
                cnt = cnt + jnp.where(row > blk, 1, 0)
            else:
                cnt = cnt + jnp.where(row > blk, 1, jnp.where(row == blk, jnp.where(jl > b, 1, 0), 0))
        ranks.append(cnt)
    rank = jnp.concatenate(ranks, axis=0)
    selb_t = jnp.where(rank < SEL_TOPK, 0.0, NEG)

    process(vwt_ref, jnp.maximum(i - nw, 0), 0, mt_w2)
    o_w = finish()

    selb = jnp.concatenate([selb_t, jnp.zeros((LANE - n_sel, tq), F32)], axis=0).T
    selb = selb.astype(BF16)
    qa = jnp.concatenate([qs, jnp.concatenate([selb] * HPG, axis=0)], axis=1)

    init()

    @pl.when(i == 0)
    def _():
        process(vst_ref, 0, 0, qk(qa, kaug_ref, 0, TAB_DIAG, 0))

    @pl.when(i >= 1)
    def _():
        mt0 = qk(qa, kaug_ref, i, TAB_DIAG, 0)
        mt1 = qk(qa, kaug_ref, i - 1, TAB_SUB, 1)
        process(vst_ref, i, 0, mt0)
        mt0 = qk(qa, kaug_ref, jnp.maximum(i - 2, 0), None, 0)
        process(vst_ref, i - 1, 1, mt1)
        n_far = i - 1
        rem = n_far % FAR_UNROLL

        def far_tiles(a, n, mt0):
            for t in range(n):
                nxt = qk(qa, kaug_ref, jnp.maximum(a - t - 1, 0), None, (t + 1) % 2)
                process(vst_ref, a - t, t % 2, mt0)
                mt0 = nxt
            return mt0

        mt0 = lax.fori_loop(0, n_far // FAR_UNROLL,
                            lambda p, mt: far_tiles(i - 2 - FAR_UNROLL * p, FAR_UNROLL, mt), mt0)

        mt_ref[...] = mt0
        size = FAR_UNROLL // 2
        while size >= 2:
            @pl.when((rem & size) != 0)
            def _(size=size):
                mt_ref[...] = far_tiles((rem & (2 * size - 1)) - 1, size, mt_ref[...])
            size //= 2

        @pl.when((rem & 1) != 0)
        def _():
            process(vst_ref, 0, 0, mt_ref[...])

    o_s = finish()

    gt = gate_ref[rows, :].T
    for h in range(HPG):
        sl = slice(h * tq, (h + 1) * tq)
        o = gt[3 * h:3 * h + 1, :] * o_c[:, sl]
        o = o + gt[3 * h + 1:3 * h + 2, :] * o_s[:, sl]
        o = o + gt[3 * h + 2:3 * h + 3, :] * o_w[:, sl]
        o_ref[rows, h * HEAD_DIM:(h + 1) * HEAD_DIM] = o.T.astype(BF16)


def _nsa(q, k_all, vt_all, cmp, cmp_t, gates, fvec, ovl, e_t, cast_weights, batch, seq_len, *, tq):
    n_t = seq_len // (tq * NSA_TILES_PER_STEP)
    n_steps = batch * N_KV * n_t
    ts = tq * NSA_TILES_PER_STEP
    step_row = lambda b, g, i: ((b * N_KV + g) * n_t + i, 0)
    cast_specs = [pl.BlockSpec((w.shape[0] // n_steps, w.shape[1]), step_row) for w in cast_weights]
    n_sel = seq_len // SEL_LEN
    n_chunks = cmp.shape[2]
    mcols = HPG * tq
    band_rows = BAND_BELOW + tq // CMP_STRIDE
    qw = HPG * HEAD_DIM

    def vt_spec(first):
        return pl.BlockSpec((None, None, HEAD_DIM + ONES_ROWS, seq_len), lambda b, g, i: (b, first + g, 0, 0))

    kern = functools.partial(_nsa_kernel, tiles_per_step=NSA_TILES_PER_STEP, tq=tq, n_sel=n_sel,
                             n_cast=len(cast_weights))
    o_attn, *cast = pl.pallas_call(
        kern,
        grid=(batch, N_KV, n_t),
        in_specs=[
            pl.BlockSpec((ts, qw), lambda b, g, i: (b * n_t + i, g)),
            pl.BlockSpec((None, None, n_chunks, HEAD_DIM), lambda b, g, i: (b, g, 0, 0)),
            pl.BlockSpec((None, None, HEAD_DIM, n_chunks), lambda b, g, i: (b, N_KV + g, 0, 0)),
            pl.BlockSpec((seq_len, HEAD_DIM), lambda b, g, i: (b, g)),
            vt_spec(0),
            pl.BlockSpec((seq_len, HEAD_DIM), lambda b, g, i: (b, N_KV + g)),
            vt_spec(N_KV),
            pl.BlockSpec((ts, LANE), lambda b, g, i: (b * n_t + i, g)),
            pl.BlockSpec((None, HPG, 1, BIAS_LEN), lambda b, g, i: (g, 0, 0, 0)),
            pl.BlockSpec((n_sel, n_chunks), lambda b, g, i: (0, 0)),
            pl.BlockSpec((seq_len, LANE), lambda b, g, i: (0, 0)),
        ] + cast_specs,
        out_specs=[pl.BlockSpec((ts, qw), lambda b, g, i: (b * n_t + i, g))] + cast_specs,
        out_shape=[jax.ShapeDtypeStruct((batch * seq_len, ATTN_WIDTH), BF16)]
        + [jax.ShapeDtypeStruct(w.shape, BF16) for w in cast_weights],
        scratch_shapes=[
            pltpu.VMEM((seq_len, 2 * HEAD_DIM), BF16),
            pltpu.VMEM((4, tq, mcols), F32),
            pltpu.VMEM((2, band_rows, mcols), F32),
            pltpu.VMEM((n_chunks, mcols), F32),
            pltpu.VMEM((tq, mcols), F32),
            pltpu.VMEM((tq, mcols), F32),
            pltpu.VMEM((1, mcols), F32),
            pltpu.VMEM((1, mcols), F32),
            pltpu.VMEM((HEAD_DIM + ONES_ROWS, mcols), F32),
        ],
        compiler_params=_params(3),
        name="nsa_attention",
    )(q, cmp, cmp_t, k_all, vt_all, k_all, vt_all, gates, fvec, ovl, e_t, *cast_weights)
    return o_attn, cast


OPROJ_CHUNKS = 4


def _oproj_kernel(oa_ref, ov_ref, wo_ref, x_ref, g1_ref, g2_ref, x1_ref, h2_ref):
    ka = oa_ref.shape[1]
    tm = oa_ref.shape[0]
    chunk = tm // OPROJ_CHUNKS
    for rows in (slice(k * chunk, (k + 1) * chunk) for k in range(OPROJ_CHUNKS)):
        mix = jnp.dot(oa_ref[rows, :], wo_ref[0:ka, :], preferred_element_type=F32)
        mix = mix + jnp.dot(ov_ref[rows, :], wo_ref[ka:, :], preferred_element_type=F32)
        x1 = x_ref[rows, :] + _rms(mix, g1_ref[...])
        x1_ref[rows, :] = x1
        h2_ref[rows, :] = _rms(x1, g2_ref[...]).astype(BF16)


def _oproj(o_attn, o_conv, w_o, x2, g_post, g_pre, *, tm=512):
    m = x2.shape[0]
    ka, kv = o_attn.shape[1], o_conv.shape[1]
    row = lambda i: (i, 0)
    fixed = lambda i: (0, 0)
    return pl.pallas_call(
        _oproj_kernel,
        grid=(m // tm,),
        in_specs=[
            pl.BlockSpec((tm, ka), row),
            pl.BlockSpec((tm, kv), row),
            pl.BlockSpec((ka + kv, D_MODEL), fixed),
            pl.BlockSpec((tm, D_MODEL), row),
            pl.BlockSpec((1, D_MODEL), fixed),
            pl.BlockSpec((1, D_MODEL), fixed),
        ],
        out_specs=[pl.BlockSpec((tm, D_MODEL), row), pl.BlockSpec((tm, D_MODEL), row)],
        out_shape=[jax.ShapeDtypeStruct((m, D_MODEL), F32), jax.ShapeDtypeStruct((m, D_MODEL), BF16)],
        compiler_params=_params(1),
        name="out_proj",
    )(o_attn, o_conv, w_o, x2, g_post, g_pre)


FFN_SPLIT = 2


def _ffn_kernel(h_ref, wu_ref, wd_ref, x1_hbm, g_ref, o_ref, x1_ref, sem):
    i = pl.program_id(0)
    j = pl.program_id(1)
    last = pl.num_programs(1) - 1
    tm = h_ref.shape[0]
    width = wu_ref.shape[1] // FFN_SPLIT
    x1_copy = pltpu.make_async_copy(x1_hbm.at[pl.ds(pl.multiple_of(i * tm, tm), tm), :], x1_ref, sem)

    def step(mode):
        if mode == "first":
            x1_copy.start()
        for k in range(FFN_SPLIT):
            cols = slice(k * width, (k + 1) * width)
            a = jnp.dot(h_ref[...], wu_ref[:, cols], preferred_element_type=F32)
            a = jnp.square(jnp.maximum(a, 0.0)).astype(BF16)
            if mode == "first" and k == 0:
                o_ref[...] = jnp.dot(a, wd_ref[cols, :], preferred_element_type=F32)
            elif mode != "last" or k < FFN_SPLIT - 1:
                o_ref[...] += jnp.dot(a, wd_ref[cols, :], preferred_element_type=F32)
            else:
                x1_copy.wait()
                for rows in (slice(0, tm // 2), slice(tm // 2, tm)):
                    f = o_ref[rows, :] + jnp.dot(a[rows, :], wd_ref[cols, :], preferred_element_type=F32)
                    o_ref[rows, :] = x1_ref[rows, :] + _rms(f, g_ref[...])

    pl.when(j == 0)(functools.partial(step, "first"))
    pl.when((j > 0) & (j < last))(functools.partial(step, "middle"))
    pl.when(j == last)(functools.partial(step, "last"))


def _ffn(h2, w_up, w_down, x1, g_post, *, tm=1024, tf=1024):
    m = h2.shape[0]
    d_ff = w_up.shape[1]
    assert d_ff // tf >= 2
    return pl.pallas_call(
        _ffn_kernel,
        grid=(m // tm, d_ff // tf),
        in_specs=[
            pl.BlockSpec((tm, D_MODEL), lambda i, j: (i, 0)),
            pl.BlockSpec((D_MODEL, tf), lambda i, j: (0, j)),
            pl.BlockSpec((tf, D_MODEL), lambda i, j: (j, 0)),
            pl.BlockSpec(memory_space=pl.ANY),
            pl.BlockSpec((1, D_MODEL), lambda i, j: (0, 0)),
        ],
        out_specs=pl.BlockSpec((tm, D_MODEL), lambda i, j: (i, 0)),
        out_shape=jax.ShapeDtypeStruct((m, D_MODEL), F32),
        scratch_shapes=[pltpu.VMEM((tm, D_MODEL), F32), pltpu.SemaphoreType.DMA(())],
        compiler_params=_params(2),
        name="ffn",
    )(h2, w_up, w_down, x1, g_post)


def _bucket_np(dist):
    n = np.maximum(dist, 0)
    max_exact = N_BUCKETS // 2
    nf = np.maximum(n, 1).astype(np.float32)
    large = max_exact + (np.log(nf / np.float32(max_exact)) / np.float32(math.log(MAX_DIST / max_exact))
                         * np.float32(N_BUCKETS - max_exact)).astype(np.int32)
    large = np.minimum(large, N_BUCKETS - 1)
    return np.where(n < max_exact, n, large)


def _bucket_starts():
    b = _bucket_np(np.arange(4 * MAX_DIST))
    return [int(np.argmax(b == k)) for k in range(N_BUCKETS)]


def _attention_tables(rel_bias, seq_len, tq):
    starts = _bucket_starts()
    assert starts[N_BUCKETS - 1] <= CMP_STRIDE * (BAND_BELOW + 1) - (CMP_LEN - 1)
    assert starts[N_BUCKETS - 1] <= tq and 2 * tq <= BIAS_LEN // 2
    rel = (rel_bias - rel_bias[:, N_BUCKETS - 1:]) * math.log2(math.e)
    d = jnp.arange(BIAS_LEN, dtype=jnp.int32)[None, :]
    fvec = jnp.broadcast_to(rel[:, 0:1], (N_HEADS, BIAS_LEN))
    for k in range(1, N_BUCKETS):
        fvec = jnp.where(d >= starts[k], rel[:, k:k + 1], fvec)
    fvec = jnp.where(d < BIAS_LEN // 2, fvec, NEG).astype(F32).reshape(N_KV, HPG, 1, BIAS_LEN)

    n_chunks = seq_len // CMP_STRIDE
    n_cmp = (seq_len - CMP_LEN) // CMP_STRIDE + 1
    n_sel = seq_len // SEL_LEN
    ci = np.arange(n_chunks)[None, :] * CMP_STRIDE
    sj = np.arange(n_sel)[:, None] * SEL_LEN
    ovl = ((ci < sj + SEL_LEN) & (ci + CMP_LEN > sj) & (np.arange(n_chunks)[None, :] < n_cmp))
    e_t = (np.arange(seq_len)[:, None] // SEL_LEN == np.arange(LANE)[None, :])
    return fvec, jnp.asarray(ovl, BF16), jnp.asarray(e_t, BF16)


def kernel(x, w_in, pe_cmp, w_cmp_k1, w_cmp_k2, w_cmp_v1, w_cmp_v2, conv_w, rel_bias, w_o, w_up, w_down,
           g_pre_mix, g_post_mix, g_pre_ffn, g_post_ffn):
    batch, seq_len, _ = x.shape
    depth = w_in.shape[0]
    tq = 256
    fvec, ovl, e_t = _attention_tables(rel_bias, seq_len, tq)
    x2 = x.reshape(batch * seq_len, D_MODEL)
    for l in range(depth):
        wl = jnp.swapaxes(w_in[l], 0, 1)
        g1 = g_pre_mix[l].reshape(1, D_MODEL)

        q, k_all, c_in, vt_all, gates, o_conv = _in_proj(x2, g1, wl, conv_w[l], batch, seq_len)

        cmp, cmp_t = _compress(c_in, pe_cmp[l], (w_cmp_k1[l], w_cmp_v1[l]), (w_cmp_k2[l], w_cmp_v2[l]),
                               batch, seq_len)

        o_attn, (wo_b, wup_b, wdown_b) = _nsa(q, k_all, vt_all, cmp, cmp_t, gates, fvec, ovl, e_t,
                                              [w_o[l], w_up[l], w_down[l]], batch, seq_len, tq=tq)

        x1, h2 = _oproj(o_attn, o_conv, wo_b, x2,
                        g_post_mix[l].reshape(1, D_MODEL), g_pre_ffn[l].reshape(1, D_MODEL))
        x2 = _ffn(h2, wup_b, wdown_b, x1, g_post_ffn[l].reshape(1, D_MODEL))
    return x2.reshape(batch, seq_len, D_MODEL)
```

```python
import functools
import math

import numpy as np
import jax
import jax.numpy as jnp
from jax import lax
from jax.experimental import pallas as pl
from jax.experimental.pallas import tpu as pltpu

F32 = jnp.float32
BF16 = jnp.bfloat16

D_MODEL = 2048
N_HEADS = 8
N_KV = 2
HPG = N_HEADS // N_KV
HEAD_DIM = 128
ATTN_WIDTH = N_HEADS * HEAD_DIM
KV_WIDTH = N_KV * HEAD_DIM
CONV_WIDTH = D_MODEL - ATTN_WIDTH
CONV_K = 3
N_BRANCH = 3
CMP_LEN = 32
CMP_STRIDE = 16
CMP_HIDDEN = 256
SEL_LEN = 64
SEL_TOPK = 16
WINDOW = 512
N_BUCKETS = 32
MAX_DIST = 128
EPS = 1e-6
NEG = -1e30
HALF_NEG = -5e29
FORCE = 1e9

QKV_WIDTH = ATTN_WIDTH + 6 * KV_WIDTH
GATE_OFF = QKV_WIDTH
CONV_OFF = QKV_WIDTH + N_HEADS * N_BRANCH
LANE = 128
VMEM_LIMIT = 56 * 1024 * 1024

_DN_T = (((1,), (1,)), ((), ()))


def _rms(x, g):
    ms = jnp.mean(x * x, axis=-1, keepdims=True)
    return x * lax.rsqrt(ms + EPS) * g


def _params(n_axes):
    return pltpu.CompilerParams(dimension_semantics=("arbitrary",) * n_axes, vmem_limit_bytes=VMEM_LIMIT)


QKV_BLOCK = 2 * KV_WIDTH
SLAB_KC, SLAB_VC, SLAB_KS, SLAB_VS, SLAB_KW, SLAB_VW = range(6)
CONV_BLOCK = 512
W_CHUNK = 256
SUBLANE = 8


def _load_weights(wt_hbm, w_ref, wg_ref, wconv_ref, stage_ref, sem):
    n_gate = N_HEADS * N_BRANCH
    per_group = HPG * N_BRANCH

    def store_rows(dst, row):
        def store(v):
            dst[row:row + v.shape[0], :] = v.astype(BF16)
        return store

    def store_gates(v):
        wg_ref[...] = jnp.zeros(wg_ref.shape, BF16)
        pad = jnp.zeros((2 * SUBLANE - per_group, v.shape[1]), F32)
        for g in range(N_KV):
            rows = jnp.concatenate([v[g * per_group:(g + 1) * per_group, :], pad], axis=0)
            wg_ref[g * LANE:g * LANE + 2 * SUBLANE, :] = rows.astype(BF16)

    chunks = [(r, W_CHUNK, store_rows(w_ref, r)) for r in range(0, QKV_WIDTH, W_CHUNK)]
    chunks.append((GATE_OFF, n_gate, store_gates))
    chunks += [(CONV_OFF + r, W_CHUNK, store_rows(wconv_ref, r)) for r in range(0, 3 * CONV_WIDTH, W_CHUNK)]

    def copy(k):
        src, n, _ = chunks[k]
        slot = k % 2
        return pltpu.make_async_copy(wt_hbm.at[pl.ds(src, n), :], stage_ref.at[slot, pl.ds(0, n), :], sem.at[slot])

    copy(0).start()
    for k, (_, n, store) in enumerate(chunks):
        if k + 1 < len(chunks):
            copy(k + 1).start()
        copy(k).wait()
        store(stage_ref[k % 2, 0:n, :])


def _inproj_kernel(x_ref, g_ref, wt_hbm, cw_ref, q_ref, k_ref, c_ref, vt_ref, gate_ref, ov_ref,
                   h_ref, carry_ref, w_ref, wg_ref, wconv_ref, stage_ref, sem, *, q_scale, tiles_per_seq):
    i = pl.program_id(0)
    tm = x_ref.shape[0]

    @pl.when(i == 0)
    def _():
        _load_weights(wt_hbm, w_ref, wg_ref, wconv_ref, stage_ref, sem)

    h_ref[...] = _rms(x_ref[...], g_ref[...]).astype(BF16)
    gate_ref[...] = jax.nn.sigmoid(lax.dot_general(h_ref[...], wg_ref[...], _DN_T, preferred_element_type=F32))

    def proj(w, j, width):
        return lax.dot_general(h_ref[...], w[j * width:(j + 1) * width, :], _DN_T, preferred_element_type=F32)

    for j in range(ATTN_WIDTH // QKV_BLOCK):
        q_ref[:, j * QKV_BLOCK:(j + 1) * QKV_BLOCK] = (proj(w_ref, j, QKV_BLOCK) * q_scale).astype(BF16)

    def slab(which):
        return proj(w_ref, ATTN_WIDTH // KV_WIDTH + which, KV_WIDTH)

    ones = jnp.where(lax.broadcasted_iota(jnp.int32, (ONES_ROWS, tm), 0) == 0, 1.0, 0.0).astype(BF16)
    for pair, (k_slab, c_slab, v_slab) in enumerate(((SLAB_KS, SLAB_KC, SLAB_VS), (SLAB_KW, SLAB_VC, SLAB_VW))):
        cols = slice(pair * KV_WIDTH, (pair + 1) * KV_WIDTH)
        k_ref[:, cols] = slab(k_slab).astype(BF16)
        c_ref[:, cols] = slab(c_slab)
        v = slab(v_slab)
        for g in range(N_KV):
            s = pair * N_KV + g
            vt_ref[s, 0:HEAD_DIM, :] = v[:, g * HEAD_DIM:(g + 1) * HEAD_DIM].T.astype(BF16)
            vt_ref[s, HEAD_DIM:HEAD_DIM + ONES_ROWS, :] = ones

    def conv_proj(which, j):
        return proj(wconv_ref, which * (CONV_WIDTH // CONV_BLOCK) + j, CONV_BLOCK)

    row = lax.broadcasted_iota(jnp.int32, (tm, CONV_BLOCK), 0)
    for j in range(CONV_WIDTH // CONV_BLOCK):
        cols = slice(j * CONV_BLOCK, (j + 1) * CONV_BLOCK)
        u = conv_proj(2, j) * conv_proj(0, j)
        prev = carry_ref[j]
        prev = jnp.where(i % tiles_per_seq == 0, 0.0, prev)
        carry_ref[j] = u[tm - SUBLANE:tm, :]
        p1 = prev[SUBLANE - 1:SUBLANE, :]
        p2 = prev[SUBLANE - 2:SUBLANE - 1, :]
        u1 = jnp.where(row == 0, p1, pltpu.roll(u, 1, axis=0))
        u2 = jnp.where(row == 0, p2, jnp.where(row == 1, p1, pltpu.roll(u, 2, axis=0)))
        w = cw_ref[:, cols]
        y = w[0:1, :] * u2
        y = y + w[1:2, :] * u1
        y = y + w[2:3, :] * u
        ov_ref[:, cols] = (conv_proj(1, j) * y).astype(BF16)


def _in_proj(x2, g, w_t, conv_w, batch, seq_len, *, tm=512):
    m = x2.shape[0]
    gw = N_KV * LANE
    tps = seq_len // tm
    n_slabs = QKV_BLOCK // HEAD_DIM
    kern = functools.partial(_inproj_kernel, q_scale=HEAD_DIM ** -0.5 * math.log2(math.e), tiles_per_seq=tps)
    row = lambda i: (i, 0)

    def resident(shape):
        return pl.BlockSpec(shape, lambda i: (0, 0), pipeline_mode=pl.Buffered(1))

    return pl.pallas_call(
        kern,
        grid=(m // tm,),
        in_specs=[
            pl.BlockSpec((tm, D_MODEL), row),
            resident((1, D_MODEL)),
            pl.BlockSpec(memory_space=pl.ANY),
            resident(conv_w.shape),
        ],
        out_specs=[
            pl.BlockSpec((tm, ATTN_WIDTH), row),
            pl.BlockSpec((tm, QKV_BLOCK), row),
            pl.BlockSpec((tm, QKV_BLOCK), row),
            pl.BlockSpec((None, n_slabs, HEAD_DIM + ONES_ROWS, tm), lambda i: (i // tps, 0, 0, i % tps)),
            pl.BlockSpec((tm, gw), row),
            pl.BlockSpec((tm, CONV_WIDTH), row),
        ],
        out_shape=[
            jax.ShapeDtypeStruct((m, ATTN_WIDTH), BF16),
            jax.ShapeDtypeStruct((m, QKV_BLOCK), BF16),
            jax.ShapeDtypeStruct((m, QKV_BLOCK), F32),
            jax.ShapeDtypeStruct((batch, n_slabs, HEAD_DIM + ONES_ROWS, seq_len), BF16),
            jax.ShapeDtypeStruct((m, gw), F32),
            jax.ShapeDtypeStruct((m, CONV_WIDTH), BF16),
        ],
        scratch_shapes=[pltpu.VMEM((tm, D_MODEL), BF16),
                        pltpu.VMEM((CONV_WIDTH // CONV_BLOCK, SUBLANE, CONV_BLOCK), F32),
                        pltpu.VMEM((QKV_WIDTH, D_MODEL), BF16),
                        pltpu.VMEM((gw, D_MODEL), BF16),
                        pltpu.VMEM((3 * CONV_WIDTH, D_MODEL), BF16),
                        pltpu.VMEM((2, W_CHUNK, D_MODEL), F32),
                        pltpu.SemaphoreType.DMA((2,))],
        compiler_params=_params(1),
        name="in_proj",
    )(x2, g, w_t, conv_w)


def _compress_kernel(x_ref, pe_ref, w1k_ref, w1v_ref, w2k_ref, w2v_ref, o_ref, ot_ref):
    n = x_ref.shape[0] // CMP_STRIDE

    def body(w1_ref, w2_ref):
        a = jnp.zeros((n, CMP_HIDDEN), F32)
        b = jnp.zeros((n, CMP_HIDDEN), F32)
        for l in range(CMP_STRIDE):
            xl = x_ref[pl.ds(l, n, stride=CMP_STRIDE), :]
            xa = (xl + pe_ref[l:l + 1, :]).astype(BF16)
            xb = (xl + pe_ref[CMP_STRIDE + l:CMP_STRIDE + l + 1, :]).astype(BF16)
            a = a + jnp.dot(xa, w1_ref[l].astype(BF16), preferred_element_type=F32)
            b = b + jnp.dot(xb, w1_ref[CMP_STRIDE + l].astype(BF16), preferred_element_type=F32)
        pre = a + pltpu.roll(b, n - 1, axis=0)
        hid = pre * jax.nn.sigmoid(pre)
        out = jnp.dot(hid.astype(BF16), w2_ref[...].astype(BF16), preferred_element_type=F32)
        row = lax.broadcasted_iota(jnp.int32, out.shape, 0)
        out = jnp.where(row < n - 1, out, 0.0)
        o_ref[...] = out.astype(BF16)
        ot_ref[...] = out.T.astype(BF16)

    is_value = pl.program_id(1) >= N_KV
    pl.when(jnp.logical_not(is_value))(functools.partial(body, w1k_ref, w2k_ref))
    pl.when(is_value)(functools.partial(body, w1v_ref, w2v_ref))


def _compress(c_in, pe, w1_kv, w2_kv, batch, seq_len):
    n_slabs = c_in.shape[1] // HEAD_DIM
    n_chunks = seq_len // CMP_STRIDE

    def resident(shape):
        return pl.BlockSpec(shape, lambda i, j: (0,) * len(shape), pipeline_mode=pl.Buffered(1))

    return pl.pallas_call(
        _compress_kernel,
        grid=(batch, n_slabs),
        in_specs=[
            pl.BlockSpec((seq_len, HEAD_DIM), lambda i, j: (i, j)),
            resident((CMP_LEN, HEAD_DIM)),
            resident(w1_kv[0].shape), resident(w1_kv[1].shape),
            resident(w2_kv[0].shape), resident(w2_kv[1].shape),
        ],
        out_specs=[
            pl.BlockSpec((None, None, n_chunks, HEAD_DIM), lambda i, j: (i, j, 0, 0)),
            pl.BlockSpec((None, None, HEAD_DIM, n_chunks), lambda i, j: (i, j, 0, 0)),
        ],
        out_shape=[
            jax.ShapeDtypeStruct((batch, n_slabs, n_chunks, HEAD_DIM), BF16),
            jax.ShapeDtypeStruct((batch, n_slabs, HEAD_DIM, n_chunks), BF16),
        ],
        compiler_params=_params(2),
        name="compress",
    )(c_in, pe, *w1_kv, *w2_kv)


TAB_DIAG, TAB_SUB, TAB_FAR, TAB_MASK = 0, 1, 2, 3
BAND_BELOW = 8
ONES_ROWS = 16


N_NSA_INPUTS = 11
BIAS_LEN = 1024
FAR_UNROLL = 8
NSA_TILES_PER_STEP = 2
RANK_PREFIXES = 4


def _nsa_kernel(*refs, tiles_per_step, **static):
    for u in range(tiles_per_step):
        _nsa_tile(refs, u, tiles_per_step, **static)


def _nsa_tile(refs, u, tiles_per_step, *, tq, n_sel, n_cast):
    (q_ref, kc_ref, vct_ref, ks_ref, vst_ref, kw_ref, vwt_ref, gate_ref, fvec_ref,
     ovl_ref, et_ref) = refs[:N_NSA_INPUTS]
    cast_in = refs[N_NSA_INPUTS:N_NSA_INPUTS + n_cast]
    o_ref = refs[N_NSA_INPUTS + n_cast]
    cast_out = refs[N_NSA_INPUTS + n_cast + 1:N_NSA_INPUTS + 2 * n_cast + 1]
    (kaug_ref, tab_ref, band_ref, sc_ref, s0_ref, s1_ref, m_ref, mt_ref,
     acc_ref, imp_ref, selbt_ref) = refs[N_NSA_INPUTS + 2 * n_cast + 1:]
    i = pl.program_id(2) * tiles_per_step + u
    rows = slice(u * tq, (u + 1) * tq)

    tk = tq
    mcols = HPG * tq
    nw = WINDOW // tk
    cpt = tq // CMP_STRIDE
    band_rows = BAND_BELOW + cpt

    def group_start(fn):
        if u == 0:
            pl.when(i == 0)(fn)

    @group_start
    def _():
        kaug_ref[:, 0:HEAD_DIM] = ks_ref[...]
        kaug_ref[:, HEAD_DIM:2 * HEAD_DIM] = et_ref[...]

    @group_start
    def _():
        c = lax.broadcasted_iota(jnp.int32, (tk, tq), 0)
        r = lax.broadcasted_iota(jnp.int32, (tk, tq), 1)
        far = jnp.where(r < c, 0.0, NEG)
        lane = lax.broadcasted_iota(jnp.int32, (1, BIAS_LEN), 1)
        for h in range(HPG):
            cols = slice(h * tq, (h + 1) * tq)
            f = fvec_ref[h]
            f_diag = jnp.where(lane < tq, f, NEG)
            x = pltpu.roll(jnp.broadcast_to(f_diag, (tk, BIAS_LEN)), 0, 1, stride=1, stride_axis=0)
            tab_ref[TAB_DIAG, :, cols] = x[:, 0:tq]
            x = pltpu.roll(jnp.broadcast_to(f, (tk, BIAS_LEN)), 0, 1, stride=1, stride_axis=0)
            tab_ref[TAB_SUB, :, cols] = x[:, tq:2 * tq]
            tab_ref[TAB_FAR, :, cols] = far
            tab_ref[TAB_MASK, :, cols] = jnp.full((tk, tq), NEG, F32)
            for v, first in enumerate((0, -BAND_BELOW)):
                shift = (CMP_STRIDE * first + CMP_LEN - 1) % BIAS_LEN
                f_shift = pltpu.roll(f, shift, 1)
                x = pltpu.roll(jnp.broadcast_to(f_shift, (band_rows, BIAS_LEN)), 0, 1,
                               stride=CMP_STRIDE, stride_axis=0)
                band_ref[v, :, cols] = x[:, 0:tq]

    q = q_ref[rows, :]
    qs = jnp.concatenate([q[:, h * HEAD_DIM:(h + 1) * HEAD_DIM] for h in range(HPG)], axis=0)

    s_refs = (s0_ref, s1_ref)

    def init():
        m_ref[...] = jnp.full((1, mcols), NEG, F32)
        acc_ref[...] = jnp.zeros(acc_ref.shape, F32)

    def keys(kt):
        return pl.ds(pl.multiple_of(kt * tk, tk), tk)

    def qk(qmat, k_ref, kt, kind, buf):
        s = lax.dot_general(k_ref[keys(kt), :], qmat, _DN_T, preferred_element_type=F32)
        if kind is not None:
            s = s + tab_ref[kind]
        s_refs[buf][...] = s
        return jnp.max(s, axis=0, keepdims=True)

    def process(vt_ref, kt, buf, m_tile):
        m_prev = m_ref[...]
        m_next = jnp.maximum(m_prev, m_tile)
        alpha = jnp.exp2(m_prev - m_next)
        p = jnp.exp2(s_refs[buf][...] - m_next).astype(BF16)
        acc_ref[...] = alpha * acc_ref[...] + jnp.dot(vt_ref[:, keys(kt)], p, preferred_element_type=F32)
        m_ref[...] = m_next

    def finish():
        return acc_ref[0:HEAD_DIM, :] * (1.0 / acc_ref[HEAD_DIM:HEAD_DIM + 1, :])

    assert nw == 2, "window tiles are i, i-1 (previous-tile table) and i-nw (window-edge table)"
    init()
    mt_w0 = qk(qs, kw_ref, i, TAB_DIAG, 0)

    if u == 0:
        for src, dst in zip(cast_in, cast_out):
            dst[...] = src[...].astype(BF16)

    raw = lax.dot_general(kc_ref[...], qs, _DN_T, preferred_element_type=F32)
    crow = lax.broadcasted_iota(jnp.int32, raw.shape, 0)
    sc_ref[...] = jnp.where(crow < cpt * (i + 1), raw, NEG)
    band = pl.ds(pl.multiple_of(jnp.maximum(cpt * i - BAND_BELOW, 0), SUBLANE), band_rows)
    sc_ref[band, :] += band_ref[jnp.minimum(i, 1)]

    mt_w1 = qk(qs, kw_ref, jnp.maximum(i - 1, 0), jnp.where(i >= 1, TAB_SUB, TAB_MASK), 1)
    process(vwt_ref, i, 0, mt_w0)

    sc = sc_ref[...]
    mc = jnp.maximum(jnp.max(sc, axis=0, keepdims=True), HALF_NEG)
    pc = jnp.exp2(sc - mc)
    lc = jnp.sum(pc, axis=0, keepdims=True)
    pc = pc * jnp.where(lc > 0.0, 1.0 / lc, 0.0)
    o_c = jnp.dot(vct_ref[...], pc.astype(BF16), preferred_element_type=F32)

    mt_w2 = qk(qs, kw_ref, jnp.maximum(i - nw, 0), jnp.where(i >= nw, TAB_FAR, TAB_MASK), 0)
    process(vwt_ref, jnp.maximum(i - 1, 0), 1, mt_w1)

    ps = pc[:, 0:tq] + pc[:, tq:2 * tq] + pc[:, 2 * tq:3 * tq] + pc[:, 3 * tq:4 * tq]
    hi = ps.astype(BF16)
    r1 = ps - hi.astype(F32)
    mid = r1.astype(BF16)
    lo = (r1 - mid.astype(F32)).astype(BF16)
    ovl = ovl_ref[...]
    imp = (jnp.dot(ovl, hi, preferred_element_type=F32) + jnp.dot(ovl, mid, preferred_element_type=F32)
           + jnp.dot(ovl, lo, preferred_element_type=F32))
    jj = lax.broadcasted_iota(jnp.int32, (n_sel, tq), 0)
    tt = i * tq + lax.broadcasted_iota(jnp.int32, (n_sel, tq), 1)
    cur = tt >> int(math.log2(SEL_LEN))
    forced = (jj == 0) | (jj == cur) | (jj == cur - 1)
    imp = jnp.where(forced, FORCE, imp)
    imp = jnp.where(jj * SEL_LEN <= tt, imp, NEG)
    imp_ref[...] = imp

    def rank_prefix(nb):
        sub = SUBLANE
        for j0 in range(0, nb, sub):
            blk = imp_ref[j0:j0 + sub, :]
            jl = j0 + lax.broadcasted_iota(jnp.int32, blk.shape, 0)
            cnt = jnp.zeros(blk.shape, jnp.int32)
            for b in range(nb):
                row = imp_ref[b:b + 1, :]
                if b < j0:
                    cnt = cnt + jnp.where(row >= blk, 1, 0)
                elif b >= j0 + sub:
                    cnt = cnt + jnp.where(row > blk, 1, 0)
                else:
                    cnt = cnt + jnp.where(row > blk, 1, jnp.where(row == blk, jnp.where(jl > b, 1, 0), 0))
            selbt_ref[j0:j0 + sub, :] = jnp.where(cnt < SEL_TOPK, 0.0, NEG)
        if nb < n_sel:
            selbt_ref[nb:n_sel, :] = jnp.full((n_sel - nb, tq), NEG, F32)

    need = (tq // SEL_LEN) * (i + 1)
    lo_nb = 0
    for nb in range(n_sel // RANK_PREFIXES, n_sel + 1, n_sel // RANK_PREFIXES):
        pl.when((need > lo_nb) & (need <= nb))(functools.partial(rank_prefix, nb))
        lo_nb = nb
    selb_t = selbt_ref[...]

    process(vwt_ref, jnp.maximum(i - nw, 0), 0, mt_w2)
    o_w = finish()

    selb = jnp.concatenate([selb_t, jnp.zeros((LANE - n_sel, tq), F32)], axis=0).T
    selb = selb.astype(BF16)
    qa = jnp.concatenate([qs, jnp.concatenate([selb] * HPG, axis=0)], axis=1)

    init()

    @pl.when(i == 0)
    def _():
        process(vst_ref, 0, 0, qk(qa, kaug_ref, 0, TAB_DIAG, 0))

    @pl.when(i >= 1)
    def _():
        mt0 = qk(qa, kaug_ref, i, TAB_DIAG, 0)
        mt1 = qk(qa, kaug_ref, i - 1, TAB_SUB, 1)
        process(vst_ref, i, 0, mt0)
        mt0 = qk(qa, kaug_ref, jnp.maximum(i - 2, 0), None, 0)
        process(vst_ref, i - 1, 1, mt1)
        n_far = i - 1
        rem = n_far % FAR_UNROLL

        def far_tiles(a, n, mt0):
            for t in range(n):
                nxt = qk(qa, kaug_ref, jnp.maximum(a - t - 1, 0), None, (t + 1) % 2)
                process(vst_ref, a - t, t % 2, mt0)
                mt0 = nxt
            return mt0

        mt0 = lax.fori_loop(0, n_far // FAR_UNROLL,
                            lambda p, mt: far_tiles(i - 2 - FAR_UNROLL * p, FAR_UNROLL, mt), mt0)

        mt_ref[...] = mt0
        size = FAR_UNROLL // 2
        while size >= 2:
            @pl.when((rem & size) != 0)
            def _(size=size):
                mt_ref[...] = far_tiles((rem & (2 * size - 1)) - 1, size, mt_ref[...])
            size //= 2

        @pl.when((rem & 1) != 0)
        def _():
            process(vst_ref, 0, 0, mt_ref[...])

    o_s = finish()

    gt = gate_ref[rows, :].T
    for h in range(HPG):
        sl = slice(h * tq, (h + 1) * tq)
        o = gt[3 * h:3 * h + 1, :] * o_c[:, sl]
        o = o + gt[3 * h + 1:3 * h + 2, :] * o_s[:, sl]
        o = o + gt[3 * h + 2:3 * h + 3, :] * o_w[:, sl]
        o_ref[rows, h * HEAD_DIM:(h + 1) * HEAD_DIM] = o.T.astype(BF16)


def _nsa(q, k_all, vt_all, cmp, cmp_t, gates, fvec, ovl, e_t, cast_weights, batch, seq_len, *, tq):
    n_t = seq_len // (tq * NSA_TILES_PER_STEP)
    n_steps = batch * N_KV * n_t
    ts = tq * NSA_TILES_PER_STEP
    step_row = lambda b, g, i: ((b * N_KV + g) * n_t + i, 0)
    cast_specs = [pl.BlockSpec((w.shape[0] // n_steps, w.shape[1]), step_row) for w in cast_weights]
    n_sel = seq_len // SEL_LEN
    n_chunks = cmp.shape[2]
    mcols = HPG * tq
    band_rows = BAND_BELOW + tq // CMP_STRIDE
    qw = HPG * HEAD_DIM

    def vt_spec(first):
        return pl.BlockSpec((None, None, HEAD_DIM + ONES_ROWS, seq_len), lambda b, g, i: (b, first + g, 0, 0))

    kern = functools.partial(_nsa_kernel, tiles_per_step=NSA_TILES_PER_STEP, tq=tq, n_sel=n_sel,
                             n_cast=len(cast_weights))
    o_attn, *cast = pl.pallas_call(
        kern,
        grid=(batch, N_KV, n_t),
        in_specs=[
            pl.BlockSpec((ts, qw), lambda b, g, i: (b * n_t + i, g)),
            pl.BlockSpec((None, None, n_chunks, HEAD_DIM), lambda b, g, i: (b, g, 0, 0)),
            pl.BlockSpec((None, None, HEAD_DIM, n_chunks), lambda b, g, i: (b, N_KV + g, 0, 0)),
            pl.BlockSpec((seq_len, HEAD_DIM), lambda b, g, i: (b, g)),
            vt_spec(0),
            pl.BlockSpec((seq_len, HEAD_DIM), lambda b, g, i: (b, N_KV + g)),
            vt_spec(N_KV),
            pl.BlockSpec((ts, LANE), lambda b, g, i: (b * n_t + i, g)),
            pl.BlockSpec((None, HPG, 1, BIAS_LEN), lambda b, g, i: (g, 0, 0, 0)),
            pl.BlockSpec((n_sel, n_chunks), lambda b, g, i: (0, 0)),
            pl.BlockSpec((seq_len, LANE), lambda b, g, i: (0, 0)),
        ] + cast_specs,
        out_specs=[pl.BlockSpec((ts, qw), lambda b, g, i: (b * n_t + i, g))] + cast_specs,
        out_shape=[jax.ShapeDtypeStruct((batch * seq_len, ATTN_WIDTH), BF16)]
        + [jax.ShapeDtypeStruct(w.shape, BF16) for w in cast_weights],
        scratch_shapes=[
            pltpu.VMEM((seq_len, 2 * HEAD_DIM), BF16),
            pltpu.VMEM((4, tq, mcols), F32),
            pltpu.VMEM((2, band_rows, mcols), F32),
            pltpu.VMEM((n_chunks, mcols), F32),
            pltpu.VMEM((tq, mcols), F32),
            pltpu.VMEM((tq, mcols), F32),
            pltpu.VMEM((1, mcols), F32),
            pltpu.VMEM((1, mcols), F32),
            pltpu.VMEM((HEAD_DIM + ONES_ROWS, mcols), F32),
            pltpu.VMEM((n_sel, tq), F32),
            pltpu.VMEM((n_sel, tq), F32),
        ],
        compiler_params=_params(3),
        name="nsa_attention",
    )(q, cmp, cmp_t, k_all, vt_all, k_all, vt_all, gates, fvec, ovl, e_t, *cast_weights)
    return o_attn, cast


OPROJ_CHUNKS = 4


def _oproj_kernel(oa_ref, ov_ref, wo_ref, x_ref, g1_ref, g2_ref, x1_ref, h2_ref):
    ka = oa_ref.shape[1]
    tm = oa_ref.shape[0]
    chunk = tm // OPROJ_CHUNKS
    for rows in (slice(k * chunk, (k + 1) * chunk) for k in range(OPROJ_CHUNKS)):
        mix = jnp.dot(oa_ref[rows, :], wo_ref[0:ka, :], preferred_element_type=F32)
        mix = mix + jnp.dot(ov_ref[rows, :], wo_ref[ka:, :], preferred_element_type=F32)
        x1 = x_ref[rows, :] + _rms(mix, g1_ref[...])
        x1_ref[rows, :] = x1
        h2_ref[rows, :] = _rms(x1, g2_ref[...]).astype(BF16)


def _oproj(o_attn, o_conv, w_o, x2, g_post, g_pre, *, tm=512):
    m = x2.shape[0]
    ka, kv = o_attn.shape[1], o_conv.shape[1]
    row = lambda i: (i, 0)
    fixed = lambda i: (0, 0)
    return pl.pallas_call(
        _oproj_kernel,
        grid=(m // tm,),
        in_specs=[
            pl.BlockSpec((tm, ka), row),
            pl.BlockSpec((tm, kv), row),
            pl.BlockSpec((ka + kv, D_MODEL), fixed),
            pl.BlockSpec((tm, D_MODEL), row),
            pl.BlockSpec((1, D_MODEL), fixed),
            pl.BlockSpec((1, D_MODEL), fixed),
        ],
        out_specs=[pl.BlockSpec((tm, D_MODEL), row), pl.BlockSpec((tm, D_MODEL), row)],
        out_shape=[jax.ShapeDtypeStruct((m, D_MODEL), F32), jax.ShapeDtypeStruct((m, D_MODEL), BF16)],
        compiler_params=_params(1),
        name="out_proj",
    )(o_attn, o_conv, w_o, x2, g_post, g_pre)


FFN_SPLIT = 2


def _ffn_kernel(h_ref, wu_ref, wd_ref, x1_hbm, g_ref, o_ref, x1_ref, sem):
    i = pl.program_id(0)
    j = pl.program_id(1)
    last = pl.num_programs(1) - 1
    tm = h_ref.shape[0]
    width = wu_ref.shape[1] // FFN_SPLIT
    x1_copy = pltpu.make_async_copy(x1_hbm.at[pl.ds(pl.multiple_of(i * tm, tm), tm), :], x1_ref, sem)

    def step(mode):
        if mode == "first":
            x1_copy.start()
        for k in range(FFN_SPLIT):
            cols = slice(k * width, (k + 1) * width)
            a = jnp.dot(h_ref[...], wu_ref[:, cols], preferred_element_type=F32)
            a = jnp.square(jnp.maximum(a, 0.0)).astype(BF16)
            if mode == "first" and k == 0:
                o_ref[...] = jnp.dot(a, wd_ref[cols, :], preferred_element_type=F32)
            elif mode != "last" or k < FFN_SPLIT - 1:
                o_ref[...] += jnp.dot(a, wd_ref[cols, :], preferred_element_type=F32)
            else:
                x1_copy.wait()
                for rows in (slice(0, tm // 2), slice(tm // 2, tm)):
                    f = o_ref[rows, :] + jnp.dot(a[rows, :], wd_ref[cols, :], preferred_element_type=F32)
                    o_ref[rows, :] = x1_ref[rows, :] + _rms(f, g_ref[...])

    pl.when(j == 0)(functools.partial(step, "first"))
    pl.when((j > 0) & (j < last))(functools.partial(step, "middle"))
    pl.when(j == last)(functools.partial(step, "last"))


def _ffn(h2, w_up, w_down, x1, g_post, *, tm=512, tf=2048):
    m = h2.shape[0]
    d_ff = w_up.shape[1]
    assert d_ff // tf >= 2
    return pl.pallas_call(
        _ffn_kernel,
        grid=(m // tm, d_ff // tf),
        in_specs=[
            pl.BlockSpec((tm, D_MODEL), lambda i, j: (i, 0)),
            pl.BlockSpec((D_MODEL, tf), lambda i, j: (0, j)),
            pl.BlockSpec((tf, D_MODEL), lambda i, j: (j, 0)),
            pl.BlockSpec(memory_space=pl.ANY),
            pl.BlockSpec((1, D_MODEL), lambda i, j: (0, 0)),
        ],
        out_specs=pl.BlockSpec((tm, D_MODEL), lambda i, j: (i, 0)),
        out_shape=jax.ShapeDtypeStruct((m, D_MODEL), F32),
        scratch_shapes=[pltpu.VMEM((tm, D_MODEL), F32), pltpu.SemaphoreType.DMA(())],
        compiler_params=_params(2),
        name="ffn",
    )(h2, w_up, w_down, x1, g_post)


def _bucket_np(dist):
    n = np.maximum(dist, 0)
    max_exact = N_BUCKETS // 2
    nf = np.maximum(n, 1).astype(np.float32)
    large = max_exact + (np.log(nf / np.float32(max_exact)) / np.float32(math.log(MAX_DIST / max_exact))
                         * np.float32(N_BUCKETS - max_exact)).astype(np.int32)
    large = np.minimum(large, N_BUCKETS - 1)
    return np.where(n < max_exact, n, large)


def _bucket_starts():
    b = _bucket_np(np.arange(4 * MAX_DIST))
    return [int(np.argmax(b == k)) for k in range(N_BUCKETS)]


def _attention_tables(rel_bias, seq_len, tq):
    starts = _bucket_starts()
    assert starts[N_BUCKETS - 1] <= CMP_STRIDE * (BAND_BELOW + 1) - (CMP_LEN - 1)
    assert starts[N_BUCKETS - 1] <= tq and 2 * tq <= BIAS_LEN // 2
    rel = (rel_bias - rel_bias[:, N_BUCKETS - 1:]) * math.log2(math.e)
    d = jnp.arange(BIAS_LEN, dtype=jnp.int32)[None, :]
    fvec = jnp.broadcast_to(rel[:, 0:1], (N_HEADS, BIAS_LEN))
    for k in range(1, N_BUCKETS):
        fvec = jnp.where(d >= starts[k], rel[:, k:k + 1], fvec)
    fvec = jnp.where(d < BIAS_LEN // 2, fvec, NEG).astype(F32).reshape(N_KV, HPG, 1, BIAS_LEN)

    n_chunks = seq_len // CMP_STRIDE
    n_cmp = (seq_len - CMP_LEN) // CMP_STRIDE + 1
    n_sel = seq_len // SEL_LEN
    ci = np.arange(n_chunks)[None, :] * CMP_STRIDE
    sj = np.arange(n_sel)[:, None] * SEL_LEN
    ovl = ((ci < sj + SEL_LEN) & (ci + CMP_LEN > sj) & (np.arange(n_chunks)[None, :] < n_cmp))
    e_t = (np.arange(seq_len)[:, None] // SEL_LEN == np.arange(LANE)[None, :])
    return fvec, jnp.asarray(ovl, BF16), jnp.asarray(e_t, BF16)


def kernel(x, w_in, pe_cmp, w_cmp_k1, w_cmp_k2, w_cmp_v1, w_cmp_v2, conv_w, rel_bias, w_o, w_up, w_down,
           g_pre_mix, g_post_mix, g_pre_ffn, g_post_ffn):
    batch, seq_len, _ = x.shape
    depth = w_in.shape[0]
    tq = 256
    fvec, ovl, e_t = _attention_tables(rel_bias, seq_len, tq)
    x2 = x.reshape(batch * seq_len, D_MODEL)
    for l in range(depth):
        wl = jnp.swapaxes(w_in[l], 0, 1)
        g1 = g_pre_mix[l].reshape(1, D_MODEL)

        q, k_all, c_in, vt_all, gates, o_conv = _in_proj(x2, g1, wl, conv_w[l], batch, seq_len)

        cmp, cmp_t = _compress(c_in, pe_cmp[l], (w_cmp_k1[l], w_cmp_v1[l]), (w_cmp_k2[l], w_cmp_v2[l]),
                               batch, seq_len)

        o_attn, (wo_b, wup_b, wdown_b) = _nsa(q, k_all, vt_all, cmp, cmp_t, gates, fvec, ovl, e_t,
                                              [w_o[l], w_up[l], w_down[l]], batch, seq_len, tq=tq)

        x1, h2 = _oproj(o_attn, o_conv, wo_b, x2,
                        g_post_mix[l].reshape(1, D_MODEL), g_pre_ffn[l].reshape(1, D_MODEL))
        x2 = _ffn(h2, wup_b, wdown_b, x1, g_post_ffn[l].reshape(1, D_MODEL))
    return x2.reshape(batch, seq_len, D_MODEL)
```

```python
import functools
import math

import numpy as np
import jax
import jax.numpy as jnp
from jax import lax
from jax.experimental import pallas as pl
from jax.experimental.pallas import tpu as pltpu

F32 = jnp.float32
BF16 = jnp.bfloat16

D_MODEL = 2048
N_HEADS = 8
N_KV = 2
HPG = N_HEADS // N_KV
HEAD_DIM = 128
ATTN_WIDTH = N_HEADS * HEAD_DIM
KV_WIDTH = N_KV * HEAD_DIM
CONV_WIDTH = D_MODEL - ATTN_WIDTH
CONV_K = 3
N_BRANCH = 3
CMP_LEN = 32
CMP_STRIDE = 16
CMP_HIDDEN = 256
SEL_LEN = 64
SEL_TOPK = 16
WINDOW = 512
N_BUCKETS = 32
MAX_DIST = 128
EPS = 1e-6
NEG = -1e30
HALF_NEG = -5e29
FORCE = 1e9

QKV_WIDTH = ATTN_WIDTH + 6 * KV_WIDTH
GATE_OFF = QKV_WIDTH
CONV_OFF = QKV_WIDTH + N_HEADS * N_BRANCH
LANE = 128
VMEM_LIMIT = 56 * 1024 * 1024

_DN_T = (((1,), (1,)), ((), ()))


def _rms(x, g):
    ms = jnp.mean(x * x, axis=-1, keepdims=True)
    return x * lax.rsqrt(ms + EPS) * g


def _params(n_axes):
    return pltpu.CompilerParams(dimension_semantics=("arbitrary",) * n_axes, vmem_limit_bytes=VMEM_LIMIT)


QKV_BLOCK = 2 * KV_WIDTH
SLAB_KC, SLAB_VC, SLAB_KS, SLAB_VS, SLAB_KW, SLAB_VW = range(6)
CONV_BLOCK = 512
W_CHUNK = 256
SUBLANE = 8


def _load_weights(wt_hbm, w_ref, wg_ref, wconv_ref, stage_ref, sem):
    n_gate = N_HEADS * N_BRANCH
    per_group = HPG * N_BRANCH

    def store_rows(dst, row):
        def store(v):
            dst[row:row + v.shape[0], :] = v.astype(BF16)
        return store

    def store_gates(v):
        wg_ref[...] = jnp.zeros(wg_ref.shape, BF16)
        pad = jnp.zeros((2 * SUBLANE - per_group, v.shape[1]), F32)
        for g in range(N_KV):
            rows = jnp.concatenate([v[g * per_group:(g + 1) * per_group, :], pad], axis=0)
            wg_ref[g * LANE:g * LANE + 2 * SUBLANE, :] = rows.astype(BF16)

    chunks = [(r, W_CHUNK, store_rows(w_ref, r)) for r in range(0, QKV_WIDTH, W_CHUNK)]
    chunks.append((GATE_OFF, n_gate, store_gates))
    chunks += [(CONV_OFF + r, W_CHUNK, store_rows(wconv_ref, r)) for r in range(0, 3 * CONV_WIDTH, W_CHUNK)]

    def copy(k):
        src, n, _ = chunks[k]
        slot = k % 2
        return pltpu.make_async_copy(wt_hbm.at[pl.ds(src, n), :], stage_ref.at[slot, pl.ds(0, n), :], sem.at[slot])

    copy(0).start()
    for k, (_, n, store) in enumerate(chunks):
        if k + 1 < len(chunks):
            copy(k + 1).start()
        copy(k).wait()
        store(stage_ref[k % 2, 0:n, :])


def _inproj_kernel(x_ref, g_ref, wt_hbm, cw_ref, q_ref, k_ref, c_ref, vt_ref, gate_ref, ov_ref,
                   h_ref, carry_ref, w_ref, wg_ref, wconv_ref, stage_ref, sem, *, q_scale, tiles_per_seq):
    i = pl.program_id(0)
    tm = x_ref.shape[0]

    @pl.when(i == 0)
    def _():
        _load_weights(wt_hbm, w_ref, wg_ref, wconv_ref, stage_ref, sem)

    h_ref[...] = _rms(x_ref[...], g_ref[...]).astype(BF16)
    gate_ref[...] = jax.nn.sigmoid(lax.dot_general(h_ref[...], wg_ref[...], _DN_T, preferred_element_type=F32))

    def proj(w, j, width):
        return lax.dot_general(h_ref[...], w[j * width:(j + 1) * width, :], _DN_T, preferred_element_type=F32)

    for j in range(ATTN_WIDTH // QKV_BLOCK):
        q_ref[:, j * QKV_BLOCK:(j + 1) * QKV_BLOCK] = (proj(w_ref, j, QKV_BLOCK) * q_scale).astype(BF16)

    def slab(which):
        return proj(w_ref, ATTN_WIDTH // KV_WIDTH + which, KV_WIDTH)

    ones = jnp.where(lax.broadcasted_iota(jnp.int32, (ONES_ROWS, tm), 0) == 0, 1.0, 0.0).astype(BF16)
    for pair, (k_slab, c_slab, v_slab) in enumerate(((SLAB_KS, SLAB_KC, SLAB_VS), (SLAB_KW, SLAB_VC, SLAB_VW))):
        cols = slice(pair * KV_WIDTH, (pair + 1) * KV_WIDTH)
        k_ref[:, cols] = slab(k_slab).astype(BF16)
        c_ref[:, cols] = slab(c_slab)
        v = slab(v_slab)
        for g in range(N_KV):
            s = pair * N_KV + g
            vt_ref[s, 0:HEAD_DIM, :] = v[:, g * HEAD_DIM:(g + 1) * HEAD_DIM].T.astype(BF16)
            vt_ref[s, HEAD_DIM:HEAD_DIM + ONES_ROWS, :] = ones

    def conv_proj(which, j):
        return proj(wconv_ref, which * (CONV_WIDTH // CONV_BLOCK) + j, CONV_BLOCK)

    row = lax.broadcasted_iota(jnp.int32, (tm, CONV_BLOCK), 0)
    for j in range(CONV_WIDTH // CONV_BLOCK):
        cols = slice(j * CONV_BLOCK, (j + 1) * CONV_BLOCK)
        u = conv_proj(2, j) * conv_proj(0, j)
        prev = carry_ref[j]
        prev = jnp.where(i % tiles_per_seq == 0, 0.0, prev)
        carry_ref[j] = u[tm - SUBLANE:tm, :]
        p1 = prev[SUBLANE - 1:SUBLANE, :]
        p2 = prev[SUBLANE - 2:SUBLANE - 1, :]
        u1 = jnp.where(row == 0, p1, pltpu.roll(u, 1, axis=0))
        u2 = jnp.where(row == 0, p2, jnp.where(row == 1, p1, pltpu.roll(u, 2, axis=0)))
        w = cw_ref[:, cols]
        y = w[0:1, :] * u2
        y = y + w[1:2, :] * u1
        y = y + w[2:3, :] * u
        ov_ref[:, cols] = (conv_proj(1, j) * y).astype(BF16)


def _in_proj(x2, g, w_t, conv_w, batch, seq_len, *, tm=512):
    m = x2.shape[0]
    gw = N_KV * LANE
    tps = seq_len // tm
    n_slabs = QKV_BLOCK // HEAD_DIM
    kern = functools.partial(_inproj_kernel, q_scale=HEAD_DIM ** -0.5 * math.log2(math.e), tiles_per_seq=tps)
    row = lambda i: (i, 0)

    def resident(shape):
        return pl.BlockSpec(shape, lambda i: (0, 0), pipeline_mode=pl.Buffered(1))

    return pl.pallas_call(
        kern,
        grid=(m // tm,),
        in_specs=[
            pl.BlockSpec((tm, D_MODEL), row),
            resident((1, D_MODEL)),
            pl.BlockSpec(memory_space=pl.ANY),
            resident(conv_w.shape),
        ],
        out_specs=[
            pl.BlockSpec((tm, ATTN_WIDTH), row),
            pl.BlockSpec((tm, QKV_BLOCK), row),
            pl.BlockSpec((tm, QKV_BLOCK), row),
            pl.BlockSpec((None, n_slabs, HEAD_DIM + ONES_ROWS, tm), lambda i: (i // tps, 0, 0, i % tps)),
            pl.BlockSpec((tm, gw), row),
            pl.BlockSpec((tm, CONV_WIDTH), row),
        ],
        out_shape=[
            jax.ShapeDtypeStruct((m, ATTN_WIDTH), BF16),
            jax.ShapeDtypeStruct((m, QKV_BLOCK), BF16),
            jax.ShapeDtypeStruct((m, QKV_BLOCK), F32),
            jax.ShapeDtypeStruct((batch, n_slabs, HEAD_DIM + ONES_ROWS, seq_len), BF16),
            jax.ShapeDtypeStruct((m, gw), F32),
            jax.ShapeDtypeStruct((m, CONV_WIDTH), BF16),
        ],
        scratch_shapes=[pltpu.VMEM((tm, D_MODEL), BF16),
                        pltpu.VMEM((CONV_WIDTH // CONV_BLOCK, SUBLANE, CONV_BLOCK), F32),
                        pltpu.VMEM((QKV_WIDTH, D_MODEL), BF16),
                        pltpu.VMEM((gw, D_MODEL), BF16),
                        pltpu.VMEM((3 * CONV_WIDTH, D_MODEL), BF16),
                        pltpu.VMEM((2, W_CHUNK, D_MODEL), F32),
                        pltpu.SemaphoreType.DMA((2,))],
        compiler_params=_params(1),
        name="in_proj",
    )(x2, g, w_t, conv_w)


def _compress_kernel(x_ref, pe_ref, w1k_ref, w1v_ref, w2k_ref, w2v_ref, o_ref, ot_ref):
    n = x_ref.shape[0] // CMP_STRIDE

    def body(w1_ref, w2_ref):
        a = jnp.zeros((n, CMP_HIDDEN), F32)
        b = jnp.zeros((n, CMP_HIDDEN), F32)
        for l in range(CMP_STRIDE):
            xl = x_ref[pl.ds(l, n, stride=CMP_STRIDE), :]
            xa = (xl + pe_ref[l:l + 1, :]).astype(BF16)
            xb = (xl + pe_ref[CMP_STRIDE + l:CMP_STRIDE + l + 1, :]).astype(BF16)
            a = a + jnp.dot(xa, w1_ref[l].astype(BF16), preferred_element_type=F32)
            b = b + jnp.dot(xb, w1_ref[CMP_STRIDE + l].astype(BF16), preferred_element_type=F32)
        pre = a + pltpu.roll(b, n - 1, axis=0)
        hid = pre * jax.nn.sigmoid(pre)
        out = jnp.dot(hid.astype(BF16), w2_ref[...].astype(BF16), preferred_element_type=F32)
        row = lax.broadcasted_iota(jnp.int32, out.shape, 0)
        out = jnp.where(row < n - 1, out, 0.0)
        o_ref[...] = out.astype(BF16)
        ot_ref[...] = out.T.astype(BF16)

    is_value = pl.program_id(1) >= N_KV
    pl.when(jnp.logical_not(is_value))(functools.partial(body, w1k_ref, w2k_ref))
    pl.when(is_value)(functools.partial(body, w1v_ref, w2v_ref))


def _compress(c_in, pe, w1_kv, w2_kv, batch, seq_len):
    n_slabs = c_in.shape[1] // HEAD_DIM
    n_chunks = seq_len // CMP_STRIDE

    def resident(shape):
        return pl.BlockSpec(shape, lambda i, j: (0,) * len(shape), pipeline_mode=pl.Buffered(1))

    return pl.pallas_call(
        _compress_kernel,
        grid=(batch, n_slabs),
        in_specs=[
            pl.BlockSpec((seq_len, HEAD_DIM), lambda i, j: (i, j)),
            resident((CMP_LEN, HEAD_DIM)),
            resident(w1_kv[0].shape), resident(w1_kv[1].shape),
            resident(w2_kv[0].shape), resident(w2_kv[1].shape),
        ],
        out_specs=[
            pl.BlockSpec((None, None, n_chunks, HEAD_DIM), lambda i, j: (i, j, 0, 0)),
            pl.BlockSpec((None, None, HEAD_DIM, n_chunks), lambda i, j: (i, j, 0, 0)),
        ],
        out_shape=[
            jax.ShapeDtypeStruct((batch, n_slabs, n_chunks, HEAD_DIM), BF16),
            jax.ShapeDtypeStruct((batch, n_slabs, HEAD_DIM, n_chunks), BF16),
        ],
        compiler_params=_params(2),
        name="compress",
    )(c_in, pe, *w1_kv, *w2_kv)


TAB_DIAG, TAB_SUB, TAB_FAR = 0, 1, 2
BAND_BELOW = 8
ONES_ROWS = 16


N_NSA_INPUTS = 12
BIAS_LEN = 1024
_END = object()


def _nsa_kernel(*refs, n_tiles, **static):
    i = pl.program_id(2)
    for k in range(n_tiles):
        pl.when(i == k)(functools.partial(_nsa_step, refs, k, n_tiles, **static))


def _spread(main, side, n_main, n_side):
    done = 0
    for m, _ in enumerate(main, 1):
        want = (m * n_side) // n_main
        while done < want and next(side, _END) is not _END:
            done += 1
    for _ in side:
        pass


def _nsa_step(refs, k, n_tiles, *, tq, n_sel, n_cast):
    (q_ref, qn_ref, kc_ref, vct_ref, ks_ref, vst_ref, kw_ref, vwt_ref, gate_ref, fvec_ref,
     ovl_ref, et_ref) = refs[:N_NSA_INPUTS]
    cast_in = refs[N_NSA_INPUTS:N_NSA_INPUTS + n_cast]
    o_ref = refs[N_NSA_INPUTS + n_cast]
    cast_out = refs[N_NSA_INPUTS + n_cast + 1:N_NSA_INPUTS + 2 * n_cast + 1]
    (kaug_ref, tab_ref, band_ref, sc_ref, s0_ref, s1_ref, sw0_ref, sw1_ref, m_ref, mw_ref,
     acc_ref, accw_ref, qa_ref, oc_ref, ow_ref) = refs[N_NSA_INPUTS + 2 * n_cast + 1:]

    tk = tq
    mcols = HPG * tq
    nw = WINDOW // tk
    cpt = tq // CMP_STRIDE
    band_rows = BAND_BELOW + cpt

    def group_start(fn):
        if k == 0:
            fn()

    @group_start
    def _():
        kaug_ref[:, 0:HEAD_DIM] = ks_ref[...]
        kaug_ref[:, HEAD_DIM:2 * HEAD_DIM] = et_ref[...]

    @group_start
    def _():
        c = lax.broadcasted_iota(jnp.int32, (tk, tq), 0)
        r = lax.broadcasted_iota(jnp.int32, (tk, tq), 1)
        far = jnp.where(r < c, 0.0, NEG)
        lane = lax.broadcasted_iota(jnp.int32, (1, BIAS_LEN), 1)
        for h in range(HPG):
            cols = slice(h * tq, (h + 1) * tq)
            f = fvec_ref[h]
            f_diag = jnp.where(lane < tq, f, NEG)
            x = pltpu.roll(jnp.broadcast_to(f_diag, (tk, BIAS_LEN)), 0, 1, stride=1, stride_axis=0)
            tab_ref[TAB_DIAG, :, cols] = x[:, 0:tq]
            x = pltpu.roll(jnp.broadcast_to(f, (tk, BIAS_LEN)), 0, 1, stride=1, stride_axis=0)
            tab_ref[TAB_SUB, :, cols] = x[:, tq:2 * tq]
            tab_ref[TAB_FAR, :, cols] = far
            for v, first in enumerate((0, -BAND_BELOW)):
                shift = (CMP_STRIDE * first + CMP_LEN - 1) % BIAS_LEN
                f_shift = pltpu.roll(f, shift, 1)
                x = pltpu.roll(jnp.broadcast_to(f_shift, (band_rows, BIAS_LEN)), 0, 1,
                               stride=CMP_STRIDE, stride_axis=0)
                band_ref[v, :, cols] = x[:, 0:tq]

    for src, dst in zip(cast_in, cast_out):
        dst[...] = src[...].astype(BF16)

    class Flash:
        def __init__(self, s_refs, m_ref, acc_ref):
            self.s_refs, self.m_ref, self.acc_ref = s_refs, m_ref, acc_ref

        def init(self):
            self.m_ref[...] = jnp.full((1, mcols), NEG, F32)
            self.acc_ref[...] = jnp.zeros(self.acc_ref.shape, F32)

        def qk(self, qmat, k_ref, kt, kind, buf):
            s = lax.dot_general(k_ref[kt * tk:(kt + 1) * tk, :], qmat, _DN_T,
                                preferred_element_type=F32)
            if kind is not None:
                s = s + tab_ref[kind]
            self.s_refs[buf][...] = s
            return jnp.max(s, axis=0, keepdims=True)

        def process(self, vt_ref, kt, buf, m_tile):
            m_prev = self.m_ref[...]
            m_next = jnp.maximum(m_prev, m_tile)
            alpha = jnp.exp2(m_prev - m_next)
            p = jnp.exp2(self.s_refs[buf][...] - m_next).astype(BF16)
            self.acc_ref[...] = alpha * self.acc_ref[...] + jnp.dot(
                vt_ref[:, kt * tk:(kt + 1) * tk], p, preferred_element_type=F32)
            self.m_ref[...] = m_next

        def finish(self):
            return self.acc_ref[0:HEAD_DIM, :] * (1.0 / self.acc_ref[HEAD_DIM:HEAD_DIM + 1, :])

    win = Flash((sw0_ref, sw1_ref), mw_ref, accw_ref)
    sel = Flash((s0_ref, s1_ref), m_ref, acc_ref)

    assert nw == 2, "window tiles are t, t-1 (previous-tile table) and t-nw (window-edge table)"

    def front_pieces(t):
        n_valid = (t + 1) * tq // SEL_LEN
        return 6 + -(-n_valid // SUBLANE)

    def front(t, qsrc_ref):
        par = t % 2
        n_valid = (t + 1) * tq // SEL_LEN
        nb8 = -(-n_valid // SUBLANE) * SUBLANE
        nb16 = -(-n_valid // 16) * 16
        c_valid = cpt * (t + 1)
        nc = -(-c_valid // LANE) * LANE
        n_win = min(t, nw) + 1
        q = qsrc_ref[...]
        qs = jnp.concatenate([q[:, h * HEAD_DIM:(h + 1) * HEAD_DIM] for h in range(HPG)], axis=0)

        win.init()
        mt_w0 = win.qk(qs, kw_ref, t, TAB_DIAG, 0)
        raw = lax.dot_general(kc_ref[0:nc, :], qs, _DN_T, preferred_element_type=F32)
        crow = lax.broadcasted_iota(jnp.int32, raw.shape, 0)
        sc_ref[0:nc, :] = jnp.where(crow < c_valid, raw, NEG)
        b0 = max(cpt * t - BAND_BELOW, 0)
        sc_ref[b0:b0 + band_rows, :] += band_ref[min(t, 1)]
        yield

        if n_win > 1:
            mt_w1 = win.qk(qs, kw_ref, t - 1, TAB_SUB, 1)
        win.process(vwt_ref, t, 0, mt_w0)
        yield

        sc = sc_ref[0:nc, :]
        mc = jnp.maximum(jnp.max(sc, axis=0, keepdims=True), HALF_NEG)
        pc = jnp.exp2(sc - mc)
        lc = jnp.sum(pc, axis=0, keepdims=True)
        pc = pc * jnp.where(lc > 0.0, 1.0 / lc, 0.0)
        oc_ref[par] = jnp.dot(vct_ref[:, 0:nc], pc.astype(BF16), preferred_element_type=F32)
        yield

        if n_win > 2:
            mt_w2 = win.qk(qs, kw_ref, t - nw, TAB_FAR, 0)
        if n_win > 1:
            win.process(vwt_ref, t - 1, 1, mt_w1)
        yield

        ps = pc[:, 0:tq] + pc[:, tq:2 * tq] + pc[:, 2 * tq:3 * tq] + pc[:, 3 * tq:4 * tq]
        hi = ps.astype(BF16)
        r1 = ps - hi.astype(F32)
        mid = r1.astype(BF16)
        lo = (r1 - mid.astype(F32)).astype(BF16)
        ovl = ovl_ref[0:nb16, 0:nc]
        imp = (jnp.dot(ovl, hi, preferred_element_type=F32) + jnp.dot(ovl, mid, preferred_element_type=F32)
               + jnp.dot(ovl, lo, preferred_element_type=F32))
        jj = lax.broadcasted_iota(jnp.int32, (nb16, tq), 0)
        tt = t * tq + lax.broadcasted_iota(jnp.int32, (nb16, tq), 1)
        cur = tt >> int(math.log2(SEL_LEN))
        forced = (jj == 0) | (jj == cur) | (jj == cur - 1)
        imp = jnp.where(forced, FORCE, imp)
        imp = jnp.where(jj * SEL_LEN <= tt, imp, NEG)
        yield

        sub = SUBLANE
        ranks = []
        for j0 in range(0, nb8, sub):
            blk = imp[j0:j0 + sub, :]
            jl = j0 + lax.broadcasted_iota(jnp.int32, blk.shape, 0)
            cnt = jnp.zeros(blk.shape, jnp.int32)
            for b in range(n_valid):
                row = imp[b:b + 1, :]
                if b < j0:
                    cnt = cnt + jnp.where(row >= blk, 1, 0)
                elif b >= j0 + sub:
                    cnt = cnt + jnp.where(row > blk, 1, 0)
                else:
                    cnt = cnt + jnp.where(row > blk, 1, jnp.where(row == blk, jnp.where(jl > b, 1, 0), 0))
            ranks.append(cnt)
            yield
        rank = jnp.concatenate(ranks, axis=0)
        selb_t = jnp.where(rank < SEL_TOPK, 0.0, NEG)

        if n_win > 2:
            win.process(vwt_ref, t - nw, 0, mt_w2)
        ow_ref[par] = win.finish()

        selb = jnp.concatenate([selb_t, jnp.zeros((LANE - nb8, tq), F32)], axis=0).T
        selb = selb.astype(BF16)
        qa_ref[par, :, 0:HEAD_DIM] = qs
        qa_ref[par, :, HEAD_DIM:2 * HEAD_DIM] = jnp.concatenate([selb] * HPG, axis=0)
        yield

    def selected(t):
        par = t % 2
        tiles = [(t, TAB_DIAG)] + ([(t - 1, TAB_SUB)] if t >= 1 else [])
        tiles += [(kt, None) for kt in range(t - 2, -1, -1)]
        sel.init()
        mt = sel.qk(qa_ref[par], kaug_ref, tiles[0][0], tiles[0][1], 0)
        for n, (kt, _) in enumerate(tiles):
            nxt = None
            if n + 1 < len(tiles):
                nxt = sel.qk(qa_ref[par], kaug_ref, tiles[n + 1][0], tiles[n + 1][1], (n + 1) % 2)
            sel.process(vst_ref, kt, n % 2, mt)
            mt = nxt
            yield
        o_s = sel.finish()
        gt = gate_ref[...].T
        for h in range(HPG):
            sl = slice(h * tq, (h + 1) * tq)
            o = gt[3 * h:3 * h + 1, :] * oc_ref[par, :, sl]
            o = o + gt[3 * h + 1:3 * h + 2, :] * o_s[:, sl]
            o = o + gt[3 * h + 2:3 * h + 3, :] * ow_ref[par, :, sl]
            o_ref[:, h * HEAD_DIM:(h + 1) * HEAD_DIM] = o.T.astype(BF16)
        yield

    if k == 0:
        for _ in front(0, q_ref):
            pass
    if k + 1 < n_tiles:
        _spread(selected(k), front(k + 1, qn_ref), k + 2, front_pieces(k + 1))
    else:
        for _ in selected(k):
            pass


def _nsa(q, k_all, vt_all, cmp, cmp_t, gates, fvec, ovl, e_t, cast_weights, batch, seq_len, *, tq):
    n_t = seq_len // tq
    n_steps = batch * N_KV * n_t
    ts = tq
    step_row = lambda b, g, i: ((b * N_KV + g) * n_t + i, 0)
    cast_specs = [pl.BlockSpec((w.shape[0] // n_steps, w.shape[1]), step_row) for w in cast_weights]
    n_sel = seq_len // SEL_LEN
    n_chunks = cmp.shape[2]
    mcols = HPG * tq
    band_rows = BAND_BELOW + tq // CMP_STRIDE
    qw = HPG * HEAD_DIM

    def vt_spec(first):
        return pl.BlockSpec((None, None, HEAD_DIM + ONES_ROWS, seq_len), lambda b, g, i: (b, first + g, 0, 0))

    kern = functools.partial(_nsa_kernel, n_tiles=n_t, tq=tq, n_sel=n_sel, n_cast=len(cast_weights))
    o_attn, *cast = pl.pallas_call(
        kern,
        grid=(batch, N_KV, n_t),
        in_specs=[
            pl.BlockSpec((ts, qw), lambda b, g, i: (b * n_t + i, g)),
            pl.BlockSpec((ts, qw), lambda b, g, i: (b * n_t + jnp.minimum(i + 1, n_t - 1), g)),
            pl.BlockSpec((None, None, n_chunks, HEAD_DIM), lambda b, g, i: (b, g, 0, 0)),
            pl.BlockSpec((None, None, HEAD_DIM, n_chunks), lambda b, g, i: (b, N_KV + g, 0, 0)),
            pl.BlockSpec((seq_len, HEAD_DIM), lambda b, g, i: (b, g)),
            vt_spec(0),
            pl.BlockSpec((seq_len, HEAD_DIM), lambda b, g, i: (b, N_KV + g)),
            vt_spec(N_KV),
            pl.BlockSpec((ts, LANE), lambda b, g, i: (b * n_t + i, g)),
            pl.BlockSpec((None, HPG, 1, BIAS_LEN), lambda b, g, i: (g, 0, 0, 0)),
            pl.BlockSpec((n_sel, n_chunks), lambda b, g, i: (0, 0)),
            pl.BlockSpec((seq_len, LANE), lambda b, g, i: (0, 0)),
        ] + cast_specs,
        out_specs=[pl.BlockSpec((ts, qw), lambda b, g, i: (b * n_t + i, g))] + cast_specs,
        out_shape=[jax.ShapeDtypeStruct((batch * seq_len, ATTN_WIDTH), BF16)]
        + [jax.ShapeDtypeStruct(w.shape, BF16) for w in cast_weights],
        scratch_shapes=[
            pltpu.VMEM((seq_len, 2 * HEAD_DIM), BF16),
            pltpu.VMEM((3, tq, mcols), F32),
            pltpu.VMEM((2, band_rows, mcols), F32),
            pltpu.VMEM((n_chunks, mcols), F32),
            pltpu.VMEM((tq, mcols), F32),
            pltpu.VMEM((tq, mcols), F32),
            pltpu.VMEM((tq, mcols), F32),
            pltpu.VMEM((tq, mcols), F32),
            pltpu.VMEM((1, mcols), F32),
            pltpu.VMEM((1, mcols), F32),
            pltpu.VMEM((HEAD_DIM + ONES_ROWS, mcols), F32),
            pltpu.VMEM((HEAD_DIM + ONES_ROWS, mcols), F32),
            pltpu.VMEM((2, mcols, 2 * HEAD_DIM), BF16),
            pltpu.VMEM((2, HEAD_DIM, mcols), F32),
            pltpu.VMEM((2, HEAD_DIM, mcols), F32),
        ],
        compiler_params=_params(3),
        name="nsa_attention",
    )(q, q, cmp, cmp_t, k_all, vt_all, k_all, vt_all, gates, fvec, ovl, e_t, *cast_weights)
    return o_attn, cast


OPROJ_CHUNKS = 4


def _oproj_kernel(oa_ref, ov_ref, wo_ref, x_ref, g1_ref, g2_ref, x1_ref, h2_ref):
    ka = oa_ref.shape[1]
    tm = oa_ref.shape[0]
    chunk = tm // OPROJ_CHUNKS
    for rows in (slice(k * chunk, (k + 1) * chunk) for k in range(OPROJ_CHUNKS)):
        mix = jnp.dot(oa_ref[rows, :], wo_ref[0:ka, :], preferred_element_type=F32)
        mix = mix + jnp.dot(ov_ref[rows, :], wo_ref[ka:, :], preferred_element_type=F32)
        x1 = x_ref[rows, :] + _rms(mix, g1_ref[...])
        x1_ref[rows, :] = x1
        h2_ref[rows, :] = _rms(x1, g2_ref[...]).astype(BF16)


def _oproj(o_attn, o_conv, w_o, x2, g_post, g_pre, *, tm=512):
    m = x2.shape[0]
    ka, kv = o_attn.shape[1], o_conv.shape[1]
    row = lambda i: (i, 0)
    fixed = lambda i: (0, 0)
    return pl.pallas_call(
        _oproj_kernel,
        grid=(m // tm,),
        in_specs=[
            pl.BlockSpec((tm, ka), row),
            pl.BlockSpec((tm, kv), row),
            pl.BlockSpec((ka + kv, D_MODEL), fixed),
            pl.BlockSpec((tm, D_MODEL), row),
            pl.BlockSpec((1, D_MODEL), fixed),
            pl.BlockSpec((1, D_MODEL), fixed),
        ],
        out_specs=[pl.BlockSpec((tm, D_MODEL), row), pl.BlockSpec((tm, D_MODEL), row)],
        out_shape=[jax.ShapeDtypeStruct((m, D_MODEL), F32), jax.ShapeDtypeStruct((m, D_MODEL), BF16)],
        compiler_params=_params(1),
        name="out_proj",
    )(o_attn, o_conv, w_o, x2, g_post, g_pre)


FFN_SPLIT = 2


def _ffn_kernel(h_ref, wu_ref, wd_ref, x1_hbm, g_ref, o_ref, x1_ref, sem):
    i = pl.program_id(0)
    j = pl.program_id(1)
    last = pl.num_programs(1) - 1
    tm = h_ref.shape[0]
    width = wu_ref.shape[1] // FFN_SPLIT
    x1_copy = pltpu.make_async_copy(x1_hbm.at[pl.ds(pl.multiple_of(i * tm, tm), tm), :], x1_ref, sem)

    def step(mode):
        if mode == "first":
            x1_copy.start()
        for k in range(FFN_SPLIT):
            cols = slice(k * width, (k + 1) * width)
            a = jnp.dot(h_ref[...], wu_ref[:, cols], preferred_element_type=F32)
            a = jnp.square(jnp.maximum(a, 0.0)).astype(BF16)
            if mode == "first" and k == 0:
                o_ref[...] = jnp.dot(a, wd_ref[cols, :], preferred_element_type=F32)
            elif mode != "last" or k < FFN_SPLIT - 1:
                o_ref[...] += jnp.dot(a, wd_ref[cols, :], preferred_element_type=F32)
            else:
                x1_copy.wait()
                for rows in (slice(0, tm // 2), slice(tm // 2, tm)):
                    f = o_ref[rows, :] + jnp.dot(a[rows, :], wd_ref[cols, :], preferred_element_type=F32)
                    o_ref[rows, :] = x1_ref[rows, :] + _rms(f, g_ref[...])

    pl.when(j == 0)(functools.partial(step, "first"))
    pl.when((j > 0) & (j < last))(functools.partial(step, "middle"))
    pl.when(j == last)(functools.partial(step, "last"))


def _ffn(h2, w_up, w_down, x1, g_post, *, tm=512, tf=2048):
    m = h2.shape[0]
    d_ff = w_up.shape[1]
    assert d_ff // tf >= 2
    return pl.pallas_call(
        _ffn_kernel,
        grid=(m // tm, d_ff // tf),
        in_specs=[
            pl.BlockSpec((tm, D_MODEL), lambda i, j: (i, 0)),
            pl.BlockSpec((D_MODEL, tf), lambda i, j: (0, j)),
            pl.BlockSpec((tf, D_MODEL), lambda i, j: (j, 0)),
            pl.BlockSpec(memory_space=pl.ANY),
            pl.BlockSpec((1, D_MODEL), lambda i, j: (0, 0)),
        ],
        out_specs=pl.BlockSpec((tm, D_MODEL), lambda i, j: (i, 0)),
        out_shape=jax.ShapeDtypeStruct((m, D_MODEL), F32),
        scratch_shapes=[pltpu.VMEM((tm, D_MODEL), F32), pltpu.SemaphoreType.DMA(())],
        compiler_params=_params(2),
        name="ffn",
    )(h2, w_up, w_down, x1, g_post)


def _bucket_np(dist):
    n = np.maximum(dist, 0)
    max_exact = N_BUCKETS // 2
    nf = np.maximum(n, 1).astype(np.float32)
    large = max_exact + (np.log(nf / np.float32(max_exact)) / np.float32(math.log(MAX_DIST / max_exact))
                         * np.float32(N_BUCKETS - max_exact)).astype(np.int32)
    large = np.minimum(large, N_BUCKETS - 1)
    return np.where(n < max_exact, n, large)


def _bucket_starts():
    b = _bucket_np(np.arange(4 * MAX_DIST))
    return [int(np.argmax(b == k)) for k in range(N_BUCKETS)]


def _attention_tables(rel_bias, seq_len, tq):
    starts = _bucket_starts()
    assert starts[N_BUCKETS - 1] <= CMP_STRIDE * (BAND_BELOW + 1) - (CMP_LEN - 1)
    assert starts[N_BUCKETS - 1] <= tq and 2 * tq <= BIAS_LEN // 2
    rel = (rel_bias - rel_bias[:, N_BUCKETS - 1:]) * math.log2(math.e)
    d = jnp.arange(BIAS_LEN, dtype=jnp.int32)[None, :]
    fvec = jnp.broadcast_to(rel[:, 0:1], (N_HEADS, BIAS_LEN))
    for k in range(1, N_BUCKETS):
        fvec = jnp.where(d >= starts[k], rel[:, k:k + 1], fvec)
    fvec = jnp.where(d < BIAS_LEN // 2, fvec, NEG).astype(F32).reshape(N_KV, HPG, 1, BIAS_LEN)

    n_chunks = seq_len // CMP_STRIDE
    n_cmp = (seq_len - CMP_LEN) // CMP_STRIDE + 1
    n_sel = seq_len // SEL_LEN
    ci = np.arange(n_chunks)[None, :] * CMP_STRIDE
    sj = np.arange(n_sel)[:, None] * SEL_LEN
    ovl = ((ci < sj + SEL_LEN) & (ci + CMP_LEN > sj) & (np.arange(n_chunks)[None, :] < n_cmp))
    e_t = (np.arange(seq_len)[:, None] // SEL_LEN == np.arange(LANE)[None, :])
    return fvec, jnp.asarray(ovl, BF16), jnp.asarray(e_t, BF16)


def kernel(x, w_in, pe_cmp, w_cmp_k1, w_cmp_k2, w_cmp_v1, w_cmp_v2, conv_w, rel_bias, w_o, w_up, w_down,
           g_pre_mix, g_post_mix, g_pre_ffn, g_post_ffn):
    batch, seq_len, _ = x.shape
    depth = w_in.shape[0]
    tq = 256
    fvec, ovl, e_t = _attention_tables(rel_bias, seq_len, tq)
    x2 = x.reshape(batch * seq_len, D_MODEL)
    for l in range(depth):
        wl = jnp.swapaxes(w_in[l], 0, 1)
        g1 = g_pre_mix[l].reshape(1, D_MODEL)

        q, k_all, c_in, vt_all, gates, o_conv = _in_proj(x2, g1, wl, conv_w[l], batch, seq_len)

        cmp, cmp_t = _compress(c_in, pe_cmp[l], (w_cmp_k1[l], w_cmp_v1[l]), (w_cmp_k2[l], w_cmp_v2[l]),
                               batch, seq_len)

        o_attn, (wo_b, wup_b, wdown_b) = _nsa(q, k_all, vt_all, cmp, cmp_t, gates, fvec, ovl, e_t,
                                              [w_o[l], w_up[l], w_down[l]], batch, seq_len, tq=tq)

        x1, h2 = _oproj(o_attn, o_conv, wo_b, x2,
                        g_post_mix[l].reshape(1, D_MODEL), g_pre_ffn[l].reshape(1, D_MODEL))
        x2 = _ffn(h2, wup_b, wdown_b, x1, g_post_ffn[l].reshape(1, D_MODEL))
    return x2.reshape(batch, seq_len, D_MODEL)
```

```python
import functools
import math

import numpy as np
import jax
import jax.numpy as jnp
from jax import lax
from jax.experimental import pallas as pl
from jax.experimental.pallas import tpu as pltpu

F32 = jnp.float32
BF16 = jnp.bfloat16

D_MODEL = 2048
N_HEADS = 8
N_KV = 2
HPG = N_HEADS // N_KV
HEAD_DIM = 128
ATTN_WIDTH = N_HEADS * HEAD_DIM
KV_WIDTH = N_KV * HEAD_DIM
CONV_WIDTH = D_MODEL - ATTN_WIDTH
CONV_K = 3
N_BRANCH = 3
CMP_LEN = 32
CMP_STRIDE = 16
CMP_HIDDEN = 256
SEL_LEN = 64
SEL_TOPK = 16
WINDOW = 512
N_BUCKETS = 32
MAX_DIST = 128
EPS = 1e-6
NEG = -1e30
HALF_NEG = -5e29
FORCE = 1e9

QKV_WIDTH = ATTN_WIDTH + 6 * KV_WIDTH
GATE_OFF = QKV_WIDTH
CONV_OFF = QKV_WIDTH + N_HEADS * N_BRANCH
LANE = 128
VMEM_LIMIT = 56 * 1024 * 1024

_DN_T = (((1,), (1,)), ((), ()))


def _rms(x, g):
    ms = jnp.mean(x * x, axis=-1, keepdims=True)
    return x * lax.rsqrt(ms + EPS) * g


def _params(n_axes):
    return pltpu.CompilerParams(dimension_semantics=("arbitrary",) * n_axes, vmem_limit_bytes=VMEM_LIMIT)


QKV_BLOCK = 2 * KV_WIDTH
SLAB_KC, SLAB_VC, SLAB_KS, SLAB_VS, SLAB_KW, SLAB_VW = range(6)
CONV_BLOCK = 512
W_CHUNK = 256
SUBLANE = 8


def _load_weights(wt_hbm, w_ref, wg_ref, wconv_ref, stage_ref, sem):
    n_gate = N_HEADS * N_BRANCH
    per_group = HPG * N_BRANCH

    def store_rows(dst, row):
        def store(v):
            dst[row:row + v.shape[0], :] = v.astype(BF16)
        return store

    def store_gates(v):
        wg_ref[...] = jnp.zeros(wg_ref.shape, BF16)
        pad = jnp.zeros((2 * SUBLANE - per_group, v.shape[1]), F32)
        for g in range(N_KV):
            rows = jnp.concatenate([v[g * per_group:(g + 1) * per_group, :], pad], axis=0)
            wg_ref[g * LANE:g * LANE + 2 * SUBLANE, :] = rows.astype(BF16)

    chunks = [(r, W_CHUNK, store_rows(w_ref, r)) for r in range(0, QKV_WIDTH, W_CHUNK)]
    chunks.append((GATE_OFF, n_gate, store_gates))
    chunks += [(CONV_OFF + r, W_CHUNK, store_rows(wconv_ref, r)) for r in range(0, 3 * CONV_WIDTH, W_CHUNK)]

    def copy(k):
        src, n, _ = chunks[k]
        slot = k % 2
        return pltpu.make_async_copy(wt_hbm.at[pl.ds(src, n), :], stage_ref.at[slot, pl.ds(0, n), :], sem.at[slot])

    copy(0).start()
    for k, (_, n, store) in enumerate(chunks):
        if k + 1 < len(chunks):
            copy(k + 1).start()
        copy(k).wait()
        store(stage_ref[k % 2, 0:n, :])


def _inproj_kernel(x_ref, g_ref, wt_hbm, cw_ref, q_ref, k_ref, c_ref, vt_ref, gate_ref, ov_ref,
                   h_ref, carry_ref, w_ref, wg_ref, wconv_ref, stage_ref, sem, *, q_scale, tiles_per_seq):
    i = pl.program_id(0)
    tm = x_ref.shape[0]

    @pl.when(i == 0)
    def _():
        _load_weights(wt_hbm, w_ref, wg_ref, wconv_ref, stage_ref, sem)

    h_ref[...] = _rms(x_ref[...], g_ref[...]).astype(BF16)
    gate_ref[...] = jax.nn.sigmoid(lax.dot_general(h_ref[...], wg_ref[...], _DN_T, preferred_element_type=F32))

    def proj(w, j, width):
        return lax.dot_general(h_ref[...], w[j * width:(j + 1) * width, :], _DN_T, preferred_element_type=F32)

    for j in range(ATTN_WIDTH // QKV_BLOCK):
        q_ref[:, j * QKV_BLOCK:(j + 1) * QKV_BLOCK] = (proj(w_ref, j, QKV_BLOCK) * q_scale).astype(BF16)

    def slab(which):
        return proj(w_ref, ATTN_WIDTH // KV_WIDTH + which, KV_WIDTH)

    ones = jnp.where(lax.broadcasted_iota(jnp.int32, (ONES_ROWS, tm), 0) == 0, 1.0, 0.0).astype(BF16)
    for pair, (k_slab, c_slab, v_slab) in enumerate(((SLAB_KS, SLAB_KC, SLAB_VS), (SLAB_KW, SLAB_VC, SLAB_VW))):
        cols = slice(pair * KV_WIDTH, (pair + 1) * KV_WIDTH)
        k_ref[:, cols] = slab(k_slab).astype(BF16)
        c_ref[:, cols] = slab(c_slab)
        v = slab(v_slab)
        for g in range(N_KV):
            s = pair * N_KV + g
            vt_ref[s, 0:HEAD_DIM, :] = v[:, g * HEAD_DIM:(g + 1) * HEAD_DIM].T.astype(BF16)
            vt_ref[s, HEAD_DIM:HEAD_DIM + ONES_ROWS, :] = ones

    def conv_proj(which, j):
        return proj(wconv_ref, which * (CONV_WIDTH // CONV_BLOCK) + j, CONV_BLOCK)

    row = lax.broadcasted_iota(jnp.int32, (tm, CONV_BLOCK), 0)
    for j in range(CONV_WIDTH // CONV_BLOCK):
        cols = slice(j * CONV_BLOCK, (j + 1) * CONV_BLOCK)
        u = conv_proj(2, j) * conv_proj(0, j)
        prev = carry_ref[j]
        prev = jnp.where(i % tiles_per_seq == 0, 0.0, prev)
        carry_ref[j] = u[tm - SUBLANE:tm, :]
        p1 = prev[SUBLANE - 1:SUBLANE, :]
        p2 = prev[SUBLANE - 2:SUBLANE - 1, :]
        u1 = jnp.where(row == 0, p1, pltpu.roll(u, 1, axis=0))
        u2 = jnp.where(row == 0, p2, jnp.where(row == 1, p1, pltpu.roll(u, 2, axis=0)))
        w = cw_ref[:, cols]
        y = w[0:1, :] * u2
        y = y + w[1:2, :] * u1
        y = y + w[2:3, :] * u
        ov_ref[:, cols] = (conv_proj(1, j) * y).astype(BF16)


def _in_proj(x2, g, w_t, conv_w, batch, seq_len, *, tm=512):
    m = x2.shape[0]
    gw = N_KV * LANE
    tps = seq_len // tm
    n_slabs = QKV_BLOCK // HEAD_DIM
    kern = functools.partial(_inproj_kernel, q_scale=HEAD_DIM ** -0.5 * math.log2(math.e), tiles_per_seq=tps)
    row = lambda i: (i, 0)

    def resident(shape):
        return pl.BlockSpec(shape, lambda i: (0, 0), pipeline_mode=pl.Buffered(1))

    return pl.pallas_call(
        kern,
        grid=(m // tm,),
        in_specs=[
            pl.BlockSpec((tm, D_MODEL), row),
            resident((1, D_MODEL)),
            pl.BlockSpec(memory_space=pl.ANY),
            resident(conv_w.shape),
        ],
        out_specs=[
            pl.BlockSpec((tm, ATTN_WIDTH), row),
            pl.BlockSpec((tm, QKV_BLOCK), row),
            pl.BlockSpec((tm, QKV_BLOCK), row),
            pl.BlockSpec((None, n_slabs, HEAD_DIM + ONES_ROWS, tm), lambda i: (i // tps, 0, 0, i % tps)),
            pl.BlockSpec((tm, gw), row),
            pl.BlockSpec((tm, CONV_WIDTH), row),
        ],
        out_shape=[
            jax.ShapeDtypeStruct((m, ATTN_WIDTH), BF16),
            jax.ShapeDtypeStruct((m, QKV_BLOCK), BF16),
            jax.ShapeDtypeStruct((m, QKV_BLOCK), F32),
            jax.ShapeDtypeStruct((batch, n_slabs, HEAD_DIM + ONES_ROWS, seq_len), BF16),
            jax.ShapeDtypeStruct((m, gw), F32),
            jax.ShapeDtypeStruct((m, CONV_WIDTH), BF16),
        ],
        scratch_shapes=[pltpu.VMEM((tm, D_MODEL), BF16),
                        pltpu.VMEM((CONV_WIDTH // CONV_BLOCK, SUBLANE, CONV_BLOCK), F32),
                        pltpu.VMEM((QKV_WIDTH, D_MODEL), BF16),
                        pltpu.VMEM((gw, D_MODEL), BF16),
                        pltpu.VMEM((3 * CONV_WIDTH, D_MODEL), BF16),
                        pltpu.VMEM((2, W_CHUNK, D_MODEL), F32),
                        pltpu.SemaphoreType.DMA((2,))],
        compiler_params=_params(1),
        name="in_proj",
    )(x2, g, w_t, conv_w)


def _compress_kernel(x_ref, pe_ref, w1k_ref, w1v_ref, w2k_ref, w2v_ref, o_ref, ot_ref):
    n = x_ref.shape[0] // CMP_STRIDE

    def body(w1_ref, w2_ref):
        a = jnp.zeros((n, CMP_HIDDEN), F32)
        b = jnp.zeros((n, CMP_HIDDEN), F32)
        for l in range(CMP_STRIDE):
            xl = x_ref[pl.ds(l, n, stride=CMP_STRIDE), :]
            xa = (xl + pe_ref[l:l + 1, :]).astype(BF16)
            xb = (xl + pe_ref[CMP_STRIDE + l:CMP_STRIDE + l + 1, :]).astype(BF16)
            a = a + jnp.dot(xa, w1_ref[l].astype(BF16), preferred_element_type=F32)
            b = b + jnp.dot(xb, w1_ref[CMP_STRIDE + l].astype(BF16), preferred_element_type=F32)
        pre = a + pltpu.roll(b, n - 1, axis=0)
        hid = pre * jax.nn.sigmoid(pre)
        out = jnp.dot(hid.astype(BF16), w2_ref[...].astype(BF16), preferred_element_type=F32)
        row = lax.broadcasted_iota(jnp.int32, out.shape, 0)
        out = jnp.where(row < n - 1, out, 0.0)
        o_ref[...] = out.astype(BF16)
        ot_ref[...] = out.T.astype(BF16)

    is_value = pl.program_id(1) >= N_KV
    pl.when(jnp.logical_not(is_value))(functools.partial(body, w1k_ref, w2k_ref))
    pl.when(is_value)(functools.partial(body, w1v_ref, w2v_ref))


def _compress(c_in, pe, w1_kv, w2_kv, batch, seq_len):
    n_slabs = c_in.shape[1] // HEAD_DIM
    n_chunks = seq_len // CMP_STRIDE

    def resident(shape):
        return pl.BlockSpec(shape, lambda i, j: (0,) * len(shape), pipeline_mode=pl.Buffered(1))

    return pl.pallas_call(
        _compress_kernel,
        grid=(batch, n_slabs),
        in_specs=[
            pl.BlockSpec((seq_len, HEAD_DIM), lambda i, j: (i, j)),
            resident((CMP_LEN, HEAD_DIM)),
            resident(w1_kv[0].shape), resident(w1_kv[1].shape),
            resident(w2_kv[0].shape), resident(w2_kv[1].shape),
        ],
        out_specs=[
            pl.BlockSpec((None, None, n_chunks, HEAD_DIM), lambda i, j: (i, j, 0, 0)),
            pl.BlockSpec((None, None, HEAD_DIM, n_chunks), lambda i, j: (i, j, 0, 0)),
        ],
        out_shape=[
            jax.ShapeDtypeStruct((batch, n_slabs, n_chunks, HEAD_DIM), BF16),
            jax.ShapeDtypeStruct((batch, n_slabs, HEAD_DIM, n_chunks), BF16),
        ],
        compiler_params=_params(2),
        name="compress",
    )(c_in, pe, *w1_kv, *w2_kv)


TAB_DIAG, TAB_SUB, TAB_FAR = 0, 1, 2
BAND_BELOW = 8
ONES_ROWS = 16


N_NSA_INPUTS = 12
BIAS_LEN = 1024
_END = object()


NSA_CLASS = 4


def _nsa_kernel(*refs, n_tiles, **static):
    i = pl.program_id(2)
    for lo in range(0, n_tiles, NSA_CLASS):
        hi = min(lo + NSA_CLASS, n_tiles) - 1
        pl.when((i >= lo) & (i <= hi))(functools.partial(_nsa_step, refs, lo, hi, n_tiles, **static))


def _spread(main, side, n_main, n_side):
    done = 0
    for m, _ in enumerate(main, 1):
        want = (m * n_side) // n_main
        while done < want and next(side, _END) is not _END:
            done += 1
    for _ in side:
        pass


def _nsa_step(refs, i_lo, i_hi, n_tiles, *, tq, n_sel, n_cast):
    i = pl.program_id(2)
    (q_ref, qn_ref, kc_ref, vct_ref, ks_ref, vst_ref, kw_ref, vwt_ref, gate_ref, fvec_ref,
     ovl_ref, et_ref) = refs[:N_NSA_INPUTS]
    cast_in = refs[N_NSA_INPUTS:N_NSA_INPUTS + n_cast]
    o_ref = refs[N_NSA_INPUTS + n_cast]
    cast_out = refs[N_NSA_INPUTS + n_cast + 1:N_NSA_INPUTS + 2 * n_cast + 1]
    (kaug_ref, tab_ref, band_ref, sc_ref, s0_ref, s1_ref, sw0_ref, sw1_ref, m_ref, mw_ref,
     acc_ref, accw_ref, qa_ref, oc_ref, ow_ref) = refs[N_NSA_INPUTS + 2 * n_cast + 1:]

    tk = tq
    mcols = HPG * tq
    nw = WINDOW // tk
    cpt = tq // CMP_STRIDE
    band_rows = BAND_BELOW + cpt

    def group_start(fn):
        if i_lo == 0:
            pl.when(i == 0)(fn)

    @group_start
    def _():
        kaug_ref[:, 0:HEAD_DIM] = ks_ref[...]
        kaug_ref[:, HEAD_DIM:2 * HEAD_DIM] = et_ref[...]

    @group_start
    def _():
        c = lax.broadcasted_iota(jnp.int32, (tk, tq), 0)
        r = lax.broadcasted_iota(jnp.int32, (tk, tq), 1)
        far = jnp.where(r < c, 0.0, NEG)
        lane = lax.broadcasted_iota(jnp.int32, (1, BIAS_LEN), 1)
        for h in range(HPG):
            cols = slice(h * tq, (h + 1) * tq)
            f = fvec_ref[h]
            f_diag = jnp.where(lane < tq, f, NEG)
            x = pltpu.roll(jnp.broadcast_to(f_diag, (tk, BIAS_LEN)), 0, 1, stride=1, stride_axis=0)
            tab_ref[TAB_DIAG, :, cols] = x[:, 0:tq]
            x = pltpu.roll(jnp.broadcast_to(f, (tk, BIAS_LEN)), 0, 1, stride=1, stride_axis=0)
            tab_ref[TAB_SUB, :, cols] = x[:, tq:2 * tq]
            tab_ref[TAB_FAR, :, cols] = far
            for v, first in enumerate((0, -BAND_BELOW)):
                shift = (CMP_STRIDE * first + CMP_LEN - 1) % BIAS_LEN
                f_shift = pltpu.roll(f, shift, 1)
                x = pltpu.roll(jnp.broadcast_to(f_shift, (band_rows, BIAS_LEN)), 0, 1,
                               stride=CMP_STRIDE, stride_axis=0)
                band_ref[v, :, cols] = x[:, 0:tq]

    for src, dst in zip(cast_in, cast_out):
        dst[...] = src[...].astype(BF16)

    def keys(kt):
        if isinstance(kt, int):
            return slice(kt * tk, (kt + 1) * tk)
        return pl.ds(pl.multiple_of(kt * tk, tk), tk)

    class Flash:
        def __init__(self, s_refs, m_ref, acc_ref):
            self.s_refs, self.m_ref, self.acc_ref = s_refs, m_ref, acc_ref

        def init(self):
            self.m_ref[...] = jnp.full((1, mcols), NEG, F32)
            self.acc_ref[...] = jnp.zeros(self.acc_ref.shape, F32)

        def qk(self, qmat, k_ref, kt, kind, buf, exists=None):
            s = lax.dot_general(k_ref[keys(kt), :], qmat, _DN_T, preferred_element_type=F32)
            if kind is not None:
                s = s + tab_ref[kind]
            if exists is not None:
                s = s + jnp.where(exists, 0.0, NEG)
            self.s_refs[buf][...] = s
            return jnp.max(s, axis=0, keepdims=True)

        def process(self, vt_ref, kt, buf, m_tile):
            m_prev = self.m_ref[...]
            m_next = jnp.maximum(m_prev, m_tile)
            alpha = jnp.exp2(m_prev - m_next)
            p = jnp.exp2(self.s_refs[buf][...] - m_next).astype(BF16)
            self.acc_ref[...] = alpha * self.acc_ref[...] + jnp.dot(
                vt_ref[:, keys(kt)], p, preferred_element_type=F32)
            self.m_ref[...] = m_next

        def finish(self):
            return self.acc_ref[0:HEAD_DIM, :] * (1.0 / self.acc_ref[HEAD_DIM:HEAD_DIM + 1, :])

    win = Flash((sw0_ref, sw1_ref), mw_ref, accw_ref)
    sel = Flash((s0_ref, s1_ref), m_ref, acc_ref)

    assert nw == 2, "window tiles are t, t-1 (previous-tile table) and t-nw (window-edge table)"

    def front_pieces(t_hi):
        n_valid = (t_hi + 1) * tq // SEL_LEN
        return 6 + -(-n_valid // SUBLANE)

    def front(t, t_lo, t_hi, qsrc_ref, par):
        n_valid = (t_hi + 1) * tq // SEL_LEN
        nb8 = -(-n_valid // SUBLANE) * SUBLANE
        nb16 = -(-n_valid // 16) * 16
        nc = -(-(cpt * (t_hi + 1)) // LANE) * LANE
        n_win = min(t_hi, nw) + 1
        q = qsrc_ref[...]
        qs = jnp.concatenate([q[:, h * HEAD_DIM:(h + 1) * HEAD_DIM] for h in range(HPG)], axis=0)

        def win_tile(d):
            if t_lo >= d:
                return t - d, None
            return jnp.maximum(t - d, 0), t >= d

        win.init()
        mt_w0 = win.qk(qs, kw_ref, t, TAB_DIAG, 0)
        raw = lax.dot_general(kc_ref[0:nc, :], qs, _DN_T, preferred_element_type=F32)
        crow = lax.broadcasted_iota(jnp.int32, raw.shape, 0)
        sc_ref[0:nc, :] = jnp.where(crow < cpt * (t + 1), raw, NEG)
        if isinstance(t, int):
            b0 = max(cpt * t - BAND_BELOW, 0)
            sc_ref[b0:b0 + band_rows, :] += band_ref[min(t, 1)]
        else:
            band = pl.ds(pl.multiple_of(jnp.maximum(cpt * t - BAND_BELOW, 0), SUBLANE), band_rows)
            sc_ref[band, :] += band_ref[jnp.minimum(t, 1)]
        yield

        if n_win > 1:
            kt1, ex1 = win_tile(1)
            mt_w1 = win.qk(qs, kw_ref, kt1, TAB_SUB, 1, ex1)
        win.process(vwt_ref, t, 0, mt_w0)
        yield

        sc = sc_ref[0:nc, :]
        mc = jnp.maximum(jnp.max(sc, axis=0, keepdims=True), HALF_NEG)
        pc = jnp.exp2(sc - mc)
        lc = jnp.sum(pc, axis=0, keepdims=True)
        pc = pc * jnp.where(lc > 0.0, 1.0 / lc, 0.0)
        oc_ref[par] = jnp.dot(vct_ref[:, 0:nc], pc.astype(BF16), preferred_element_type=F32)
        yield

        if n_win > 2:
            kt2, ex2 = win_tile(nw)
            mt_w2 = win.qk(qs, kw_ref, kt2, TAB_FAR, 0, ex2)
        if n_win > 1:
            win.process(vwt_ref, kt1, 1, mt_w1)
        yield

        ps = pc[:, 0:tq] + pc[:, tq:2 * tq] + pc[:, 2 * tq:3 * tq] + pc[:, 3 * tq:4 * tq]
        hi = ps.astype(BF16)
        r1 = ps - hi.astype(F32)
        mid = r1.astype(BF16)
        lo = (r1 - mid.astype(F32)).astype(BF16)
        ovl = ovl_ref[0:nb16, 0:nc]
        imp = (jnp.dot(ovl, hi, preferred_element_type=F32) + jnp.dot(ovl, mid, preferred_element_type=F32)
               + jnp.dot(ovl, lo, preferred_element_type=F32))
        jj = lax.broadcasted_iota(jnp.int32, (nb16, tq), 0)
        tt = t * tq + lax.broadcasted_iota(jnp.int32, (nb16, tq), 1)
        cur = tt >> int(math.log2(SEL_LEN))
        forced = (jj == 0) | (jj == cur) | (jj == cur - 1)
        imp = jnp.where(forced, FORCE, imp)
        imp = jnp.where(jj * SEL_LEN <= tt, imp, NEG)
        yield

        sub = SUBLANE
        ranks = []
        for j0 in range(0, nb8, sub):
            blk = imp[j0:j0 + sub, :]
            jl = j0 + lax.broadcasted_iota(jnp.int32, blk.shape, 0)
            cnt = jnp.zeros(blk.shape, jnp.int32)
            for b in range(n_valid):
                row = imp[b:b + 1, :]
                if b < j0:
                    cnt = cnt + jnp.where(row >= blk, 1, 0)
                elif b >= j0 + sub:
                    cnt = cnt + jnp.where(row > blk, 1, 0)
                else:
                    cnt = cnt + jnp.where(row > blk, 1, jnp.where(row == blk, jnp.where(jl > b, 1, 0), 0))
            ranks.append(cnt)
            yield
        rank = jnp.concatenate(ranks, axis=0)
        selb_t = jnp.where(rank < SEL_TOPK, 0.0, NEG)

        if n_win > 2:
            win.process(vwt_ref, kt2, 0, mt_w2)
        ow_ref[par] = win.finish()

        selb = jnp.concatenate([selb_t, jnp.zeros((LANE - nb8, tq), F32)], axis=0).T
        selb = selb.astype(BF16)
        qa_ref[par, :, 0:HEAD_DIM] = qs
        qa_ref[par, :, HEAD_DIM:2 * HEAD_DIM] = jnp.concatenate([selb] * HPG, axis=0)
        yield

    def selected(t, t_lo, t_hi):
        par = t % 2
        tiles = []
        for d in range(t_hi + 1):
            kind = (TAB_DIAG, TAB_SUB)[d] if d < 2 else None
            tiles.append((t - d, kind, None) if t_lo >= d else (jnp.maximum(t - d, 0), kind, t >= d))
        sel.init()
        mt = sel.qk(qa_ref[par], kaug_ref, tiles[0][0], tiles[0][1], 0)
        for n, (kt, _, _) in enumerate(tiles):
            nxt = None
            if n + 1 < len(tiles):
                kt_n, kind_n, ex_n = tiles[n + 1]
                nxt = sel.qk(qa_ref[par], kaug_ref, kt_n, kind_n, (n + 1) % 2, ex_n)
            sel.process(vst_ref, kt, n % 2, mt)
            mt = nxt
            yield
        o_s = sel.finish()
        gt = gate_ref[...].T
        for h in range(HPG):
            sl = slice(h * tq, (h + 1) * tq)
            o = gt[3 * h:3 * h + 1, :] * oc_ref[par, :, sl]
            o = o + gt[3 * h + 1:3 * h + 2, :] * o_s[:, sl]
            o = o + gt[3 * h + 2:3 * h + 3, :] * ow_ref[par, :, sl]
            o_ref[:, h * HEAD_DIM:(h + 1) * HEAD_DIM] = o.T.astype(BF16)
        yield

    if i_lo == 0:
        @pl.when(i == 0)
        def _():
            for _ in front(0, 0, 0, q_ref, 0):
                pass
    t_hi = min(i_hi + 1, n_tiles - 1)
    t_next = jnp.minimum(i + 1, n_tiles - 1)
    _spread(selected(i, i_lo, i_hi), front(t_next, i_lo + 1, t_hi, qn_ref, (i + 1) % 2),
            i_hi + 2, front_pieces(t_hi))


def _nsa(q, k_all, vt_all, cmp, cmp_t, gates, fvec, ovl, e_t, cast_weights, batch, seq_len, *, tq):
    n_t = seq_len // tq
    n_steps = batch * N_KV * n_t
    ts = tq
    step_row = lambda b, g, i: ((b * N_KV + g) * n_t + i, 0)
    cast_specs = [pl.BlockSpec((w.shape[0] // n_steps, w.shape[1]), step_row) for w in cast_weights]
    n_sel = seq_len // SEL_LEN
    n_chunks = cmp.shape[2]
    mcols = HPG * tq
    band_rows = BAND_BELOW + tq // CMP_STRIDE
    qw = HPG * HEAD_DIM

    def vt_spec(first):
        return pl.BlockSpec((None, None, HEAD_DIM + ONES_ROWS, seq_len), lambda b, g, i: (b, first + g, 0, 0))

    kern = functools.partial(_nsa_kernel, n_tiles=n_t, tq=tq, n_sel=n_sel, n_cast=len(cast_weights))
    o_attn, *cast = pl.pallas_call(
        kern,
        grid=(batch, N_KV, n_t),
        in_specs=[
            pl.BlockSpec((ts, qw), lambda b, g, i: (b * n_t + i, g)),
            pl.BlockSpec((ts, qw), lambda b, g, i: (b * n_t + jnp.minimum(i + 1, n_t - 1), g)),
            pl.BlockSpec((None, None, n_chunks, HEAD_DIM), lambda b, g, i: (b, g, 0, 0)),
            pl.BlockSpec((None, None, HEAD_DIM, n_chunks), lambda b, g, i: (b, N_KV + g, 0, 0)),
            pl.BlockSpec((seq_len, HEAD_DIM), lambda b, g, i: (b, g)),
            vt_spec(0),
            pl.BlockSpec((seq_len, HEAD_DIM), lambda b, g, i: (b, N_KV + g)),
            vt_spec(N_KV),
            pl.BlockSpec((ts, LANE), lambda b, g, i: (b * n_t + i, g)),
            pl.BlockSpec((None, HPG, 1, BIAS_LEN), lambda b, g, i: (g, 0, 0, 0)),
            pl.BlockSpec((n_sel, n_chunks), lambda b, g, i: (0, 0)),
            pl.BlockSpec((seq_len, LANE), lambda b, g, i: (0, 0)),
        ] + cast_specs,
        out_specs=[pl.BlockSpec((ts, qw), lambda b, g, i: (b * n_t + i, g))] + cast_specs,
        out_shape=[jax.ShapeDtypeStruct((batch * seq_len, ATTN_WIDTH), BF16)]
        + [jax.ShapeDtypeStruct(w.shape, BF16) for w in cast_weights],
        scratch_shapes=[
            pltpu.VMEM((seq_len, 2 * HEAD_DIM), BF16),
            pltpu.VMEM((3, tq, mcols), F32),
            pltpu.VMEM((2, band_rows, mcols), F32),
            pltpu.VMEM((n_chunks, mcols), F32),
            pltpu.VMEM((tq, mcols), F32),
            pltpu.VMEM((tq, mcols), F32),
            pltpu.VMEM((tq, mcols), F32),
            pltpu.VMEM((tq, mcols), F32),
            pltpu.VMEM((1, mcols), F32),
            pltpu.VMEM((1, mcols), F32),
            pltpu.VMEM((HEAD_DIM + ONES_ROWS, mcols), F32),
            pltpu.VMEM((HEAD_DIM + ONES_ROWS, mcols), F32),
            pltpu.VMEM((2, mcols, 2 * HEAD_DIM), BF16),
            pltpu.VMEM((2, HEAD_DIM, mcols), F32),
            pltpu.VMEM((2, HEAD_DIM, mcols), F32),
        ],
        compiler_params=_params(3),
        name="nsa_attention",
    )(q, q, cmp, cmp_t, k_all, vt_all, k_all, vt_all, gates, fvec, ovl, e_t, *cast_weights)
    return o_attn, cast


OPROJ_CHUNKS = 4


def _oproj_kernel(oa_ref, ov_ref, wo_ref, x_ref, g1_ref, g2_ref, x1_ref, h2_ref):
    ka = oa_ref.shape[1]
    tm = oa_ref.shape[0]
    chunk = tm // OPROJ_CHUNKS
    for rows in (slice(k * chunk, (k + 1) * chunk) for k in range(OPROJ_CHUNKS)):
        mix = jnp.dot(oa_ref[rows, :], wo_ref[0:ka, :], preferred_element_type=F32)
        mix = mix + jnp.dot(ov_ref[rows, :], wo_ref[ka:, :], preferred_element_type=F32)
        x1 = x_ref[rows, :] + _rms(mix, g1_ref[...])
        x1_ref[rows, :] = x1
        h2_ref[rows, :] = _rms(x1, g2_ref[...]).astype(BF16)


def _oproj(o_attn, o_conv, w_o, x2, g_post, g_pre, *, tm=512):
    m = x2.shape[0]
    ka, kv = o_attn.shape[1], o_conv.shape[1]
    row = lambda i: (i, 0)
    fixed = lambda i: (0, 0)
    return pl.pallas_call(
        _oproj_kernel,
        grid=(m // tm,),
        in_specs=[
            pl.BlockSpec((tm, ka), row),
            pl.BlockSpec((tm, kv), row),
            pl.BlockSpec((ka + kv, D_MODEL), fixed),
            pl.BlockSpec((tm, D_MODEL), row),
            pl.BlockSpec((1, D_MODEL), fixed),
            pl.BlockSpec((1, D_MODEL), fixed),
        ],
        out_specs=[pl.BlockSpec((tm, D_MODEL), row), pl.BlockSpec((tm, D_MODEL), row)],
        out_shape=[jax.ShapeDtypeStruct((m, D_MODEL), F32), jax.ShapeDtypeStruct((m, D_MODEL), BF16)],
        compiler_params=_params(1),
        name="out_proj",
    )(o_attn, o_conv, w_o, x2, g_post, g_pre)


FFN_SPLIT = 2


def _ffn_kernel(h_ref, wu_ref, wd_ref, x1_hbm, g_ref, o_ref, x1_ref, sem):
    i = pl.program_id(0)
    j = pl.program_id(1)
    last = pl.num_programs(1) - 1
    tm = h_ref.shape[0]
    width = wu_ref.shape[1] // FFN_SPLIT
    x1_copy = pltpu.make_async_copy(x1_hbm.at[pl.ds(pl.multiple_of(i * tm, tm), tm), :], x1_ref, sem)

    def step(mode):
        if mode == "first":
            x1_copy.start()
        for k in range(FFN_SPLIT):
            cols = slice(k * width, (k + 1) * width)
            a = jnp.dot(h_ref[...], wu_ref[:, cols], preferred_element_type=F32)
            a = jnp.square(jnp.maximum(a, 0.0)).astype(BF16)
            if mode == "first" and k == 0:
                o_ref[...] = jnp.dot(a, wd_ref[cols, :], preferred_element_type=F32)
            elif mode != "last" or k < FFN_SPLIT - 1:
                o_ref[...] += jnp.dot(a, wd_ref[cols, :], preferred_element_type=F32)
            else:
                x1_copy.wait()
                for rows in (slice(0, tm // 2), slice(tm // 2, tm)):
                    f = o_ref[rows, :] + jnp.dot(a[rows, :], wd_ref[cols, :], preferred_element_type=F32)
                    o_ref[rows, :] = x1_ref[rows, :] + _rms(f, g_ref[...])

    pl.when(j == 0)(functools.partial(step, "first"))
    pl.when((j > 0) & (j < last))(functools.partial(step, "middle"))
    pl.when(j == last)(functools.partial(step, "last"))


def _ffn(h2, w_up, w_down, x1, g_post, *, tm=512, tf=2048):
    m = h2.shape[0]
    d_ff = w_up.shape[1]
    assert d_ff // tf >= 2
    return pl.pallas_call(
        _ffn_kernel,
        grid=(m // tm, d_ff // tf),
        in_specs=[
            pl.BlockSpec((tm, D_MODEL), lambda i, j: (i, 0)),
            pl.BlockSpec((D_MODEL, tf), lambda i, j: (0, j)),
            pl.BlockSpec((tf, D_MODEL), lambda i, j: (j, 0)),
            pl.BlockSpec(memory_space=pl.ANY),
            pl.BlockSpec((1, D_MODEL), lambda i, j: (0, 0)),
        ],
        out_specs=pl.BlockSpec((tm, D_MODEL), lambda i, j: (i, 0)),
        out_shape=jax.ShapeDtypeStruct((m, D_MODEL), F32),
        scratch_shapes=[pltpu.VMEM((tm, D_MODEL), F32), pltpu.SemaphoreType.DMA(())],
        compiler_params=_params(2),
        name="ffn",
    )(h2, w_up, w_down, x1, g_post)


def _bucket_np(dist):
    n = np.maximum(dist, 0)
    max_exact = N_BUCKETS // 2
    nf = np.maximum(n, 1).astype(np.float32)
    large = max_exact + (np.log(nf / np.float32(max_exact)) / np.float32(math.log(MAX_DIST / max_exact))
                         * np.float32(N_BUCKETS - max_exact)).astype(np.int32)
    large = np.minimum(large, N_BUCKETS - 1)
    return np.where(n < max_exact, n, large)


def _bucket_starts():
    b = _bucket_np(np.arange(4 * MAX_DIST))
    return [int(np.argmax(b == k)) for k in range(N_BUCKETS)]


def _attention_tables(rel_bias, seq_len, tq):
    starts = _bucket_starts()
    assert starts[N_BUCKETS - 1] <= CMP_STRIDE * (BAND_BELOW + 1) - (CMP_LEN - 1)
    assert starts[N_BUCKETS - 1] <= tq and 2 * tq <= BIAS_LEN // 2
    rel = (rel_bias - rel_bias[:, N_BUCKETS - 1:]) * math.log2(math.e)
    d = jnp.arange(BIAS_LEN, dtype=jnp.int32)[None, :]
    fvec = jnp.broadcast_to(rel[:, 0:1], (N_HEADS, BIAS_LEN))
    for k in range(1, N_BUCKETS):
        fvec = jnp.where(d >= starts[k], rel[:, k:k + 1], fvec)
    fvec = jnp.where(d < BIAS_LEN // 2, fvec, NEG).astype(F32).reshape(N_KV, HPG, 1, BIAS_LEN)

    n_chunks = seq_len // CMP_STRIDE
    n_cmp = (seq_len - CMP_LEN) // CMP_STRIDE + 1
    n_sel = seq_len // SEL_LEN
    ci = np.arange(n_chunks)[None, :] * CMP_STRIDE
    sj = np.arange(n_sel)[:, None] * SEL_LEN
    ovl = ((ci < sj + SEL_LEN) & (ci + CMP_LEN > sj) & (np.arange(n_chunks)[None, :] < n_cmp))
    e_t = (np.arange(seq_len)[:, None] // SEL_LEN == np.arange(LANE)[None, :])
    return fvec, jnp.asarray(ovl, BF16), jnp.asarray(e_t, BF16)


def kernel(x, w_in, pe_cmp, w_cmp_k1, w_cmp_k2, w_cmp_v1, w_cmp_v2, conv_w, rel_bias, w_o, w_up, w_down,
           g_pre_mix, g_post_mix, g_pre_ffn, g_post_ffn):
    batch, seq_len, _ = x.shape
    depth = w_in.shape[0]
    tq = 256
    fvec, ovl, e_t = _attention_tables(rel_bias, seq_len, tq)
    x2 = x.reshape(batch * seq_len, D_MODEL)
    for l in range(depth):
        wl = jnp.swapaxes(w_in[l], 0, 1)
        g1 = g_pre_mix[l].reshape(1, D_MODEL)

        q, k_all, c_in, vt_all, gates, o_conv = _in_proj(x2, g1, wl, conv_w[l], batch, seq_len)

        cmp, cmp_t = _compress(c_in, pe_cmp[l], (w_cmp_k1[l], w_cmp_v1[l]), (w_cmp_k2[l], w_cmp_v2[l]),
                               batch, seq_len)

        o_attn, (wo_b, wup_b, wdown_b) = _nsa(q, k_all, vt_all, cmp, cmp_t, gates, fvec, ovl, e_t,
                                              [w_o[l], w_up[l], w_down[l]], batch, seq_len, tq=tq)

        x1, h2 = _oproj(o_attn, o_conv, wo_b, x2,
                        g_post_mix[l].reshape(1, D_MODEL), g_pre_ffn[l].reshape(1, D_MODEL))
        x2 = _ffn(h2, wup_b, wdown_b, x1, g_post_ffn[l].reshape(1, D_MODEL))
    return x2.reshape(batch, seq_len, D_MODEL)
```

```python
import functools
import math

import numpy as np
import jax
import jax.numpy as jnp
from jax import lax
from jax.experimental import pallas as pl
from jax.experimental.pallas import tpu as pltpu

F32 = jnp.float32
BF16 = jnp.bfloat16

D_MODEL = 2048
N_HEADS = 8
N_KV = 2
HPG = N_HEADS // N_KV
HEAD_DIM = 128
ATTN_WIDTH = N_HEADS * HEAD_DIM
KV_WIDTH = N_KV * HEAD_DIM
CONV_WIDTH = D_MODEL - ATTN_WIDTH
CONV_K = 3
N_BRANCH = 3
CMP_LEN = 32
CMP_STRIDE = 16
CMP_HIDDEN = 256
SEL_LEN = 64
SEL_TOPK = 16
WINDOW = 512
N_BUCKETS = 32
MAX_DIST = 128
EPS = 1e-6
NEG = -1e30
HALF_NEG = -5e29
FORCE = 1e9

QKV_WIDTH = ATTN_WIDTH + 6 * KV_WIDTH
GATE_OFF = QKV_WIDTH
CONV_OFF = QKV_WIDTH + N_HEADS * N_BRANCH
LANE = 128
VMEM_LIMIT = 56 * 1024 * 1024

_DN_T = (((1,), (1,)), ((), ()))


def _rms(x, g):
    ms = jnp.mean(x * x, axis=-1, keepdims=True)
    return x * lax.rsqrt(ms + EPS) * g


def _params(n_axes):
    return pltpu.CompilerParams(dimension_semantics=("arbitrary",) * n_axes, vmem_limit_bytes=VMEM_LIMIT)


QKV_BLOCK = 2 * KV_WIDTH
SLAB_KC, SLAB_VC, SLAB_KS, SLAB_VS, SLAB_KW, SLAB_VW = range(6)
CONV_BLOCK = 512
W_CHUNK = 256
SUBLANE = 8


def _load_weights(wt_hbm, w_ref, wg_ref, wconv_ref, stage_ref, sem):
    n_gate = N_HEADS * N_BRANCH
    per_group = HPG * N_BRANCH

    def store_rows(dst, row):
        def store(v):
            dst[row:row + v.shape[0], :] = v.astype(BF16)
        return store

    def store_gates(v):
        wg_ref[...] = jnp.zeros(wg_ref.shape, BF16)
        pad = jnp.zeros((2 * SUBLANE - per_group, v.shape[1]), F32)
        for g in range(N_KV):
            rows = jnp.concatenate([v[g * per_group:(g + 1) * per_group, :], pad], axis=0)
            wg_ref[g * LANE:g * LANE + 2 * SUBLANE, :] = rows.astype(BF16)

    chunks = [(r, W_CHUNK, store_rows(w_ref, r)) for r in range(0, QKV_WIDTH, W_CHUNK)]
    chunks.append((GATE_OFF, n_gate, store_gates))
    chunks += [(CONV_OFF + r, W_CHUNK, store_rows(wconv_ref, r)) for r in range(0, 3 * CONV_WIDTH, W_CHUNK)]

    def copy(k):
        src, n, _ = chunks[k]
        slot = k % 2
        return pltpu.make_async_copy(wt_hbm.at[pl.ds(src, n), :], stage_ref.at[slot, pl.ds(0, n), :], sem.at[slot])

    copy(0).start()
    for k, (_, n, store) in enumerate(chunks):
        if k + 1 < len(chunks):
            copy(k + 1).start()
        copy(k).wait()
        store(stage_ref[k % 2, 0:n, :])


def _inproj_kernel(x_ref, g_ref, wt_hbm, cw_ref, q_ref, k_ref, c_ref, vt_ref, gate_ref, ov_ref,
                   h_ref, carry_ref, w_ref, wg_ref, wconv_ref, stage_ref, sem, *, q_scale, tiles_per_seq):
    i = pl.program_id(0)
    tm = x_ref.shape[0]

    @pl.when(i == 0)
    def _():
        _load_weights(wt_hbm, w_ref, wg_ref, wconv_ref, stage_ref, sem)

    h_ref[...] = _rms(x_ref[...], g_ref[...]).astype(BF16)
    gate_ref[...] = jax.nn.sigmoid(lax.dot_general(h_ref[...], wg_ref[...], _DN_T, preferred_element_type=F32))

    def proj(w, j, width):
        return lax.dot_general(h_ref[...], w[j * width:(j + 1) * width, :], _DN_T, preferred_element_type=F32)

    for j in range(ATTN_WIDTH // QKV_BLOCK):
        q_ref[:, j * QKV_BLOCK:(j + 1) * QKV_BLOCK] = (proj(w_ref, j, QKV_BLOCK) * q_scale).astype(BF16)

    def slab(which):
        return proj(w_ref, ATTN_WIDTH // KV_WIDTH + which, KV_WIDTH)

    ones = jnp.where(lax.broadcasted_iota(jnp.int32, (ONES_ROWS, tm), 0) == 0, 1.0, 0.0).astype(BF16)
    for pair, (k_slab, c_slab, v_slab) in enumerate(((SLAB_KS, SLAB_KC, SLAB_VS), (SLAB_KW, SLAB_VC, SLAB_VW))):
        cols = slice(pair * KV_WIDTH, (pair + 1) * KV_WIDTH)
        k_ref[:, cols] = slab(k_slab).astype(BF16)
        c_ref[:, cols] = slab(c_slab)
        v = slab(v_slab)
        for g in range(N_KV):
            s = pair * N_KV + g
            vt_ref[s, 0:HEAD_DIM, :] = v[:, g * HEAD_DIM:(g + 1) * HEAD_DIM].T.astype(BF16)
            vt_ref[s, HEAD_DIM:HEAD_DIM + ONES_ROWS, :] = ones

    def conv_proj(which, j):
        return proj(wconv_ref, which * (CONV_WIDTH // CONV_BLOCK) + j, CONV_BLOCK)

    row = lax.broadcasted_iota(jnp.int32, (tm, CONV_BLOCK), 0)
    for j in range(CONV_WIDTH // CONV_BLOCK):
        cols = slice(j * CONV_BLOCK, (j + 1) * CONV_BLOCK)
        u = conv_proj(2, j) * conv_proj(0, j)
        prev = carry_ref[j]
        prev = jnp.where(i % tiles_per_seq == 0, 0.0, prev)
        carry_ref[j] = u[tm - SUBLANE:tm, :]
        p1 = prev[SUBLANE - 1:SUBLANE, :]
        p2 = prev[SUBLANE - 2:SUBLANE - 1, :]
        u1 = jnp.where(row == 0, p1, pltpu.roll(u, 1, axis=0))
        u2 = jnp.where(row == 0, p2, jnp.where(row == 1, p1, pltpu.roll(u, 2, axis=0)))
        w = cw_ref[:, cols]
        y = w[0:1, :] * u2
        y = y + w[1:2, :] * u1
        y = y + w[2:3, :] * u
        ov_ref[:, cols] = (conv_proj(1, j) * y).astype(BF16)


def _in_proj(x2, g, w_t, conv_w, batch, seq_len, *, tm=512):
    m = x2.shape[0]
    gw = N_KV * LANE
    tps = seq_len // tm
    n_slabs = QKV_BLOCK // HEAD_DIM
    kern = functools.partial(_inproj_kernel, q_scale=HEAD_DIM ** -0.5 * math.log2(math.e), tiles_per_seq=tps)
    row = lambda i: (i, 0)

    def resident(shape):
        return pl.BlockSpec(shape, lambda i: (0, 0), pipeline_mode=pl.Buffered(1))

    return pl.pallas_call(
        kern,
        grid=(m // tm,),
        in_specs=[
            pl.BlockSpec((tm, D_MODEL), row),
            resident((1, D_MODEL)),
            pl.BlockSpec(memory_space=pl.ANY),
            resident(conv_w.shape),
        ],
        out_specs=[
            pl.BlockSpec((tm, ATTN_WIDTH), row),
            pl.BlockSpec((tm, QKV_BLOCK), row),
            pl.BlockSpec((tm, QKV_BLOCK), row),
            pl.BlockSpec((None, n_slabs, HEAD_DIM + ONES_ROWS, tm), lambda i: (i // tps, 0, 0, i % tps)),
            pl.BlockSpec((tm, gw), row),
            pl.BlockSpec((tm, CONV_WIDTH), row),
        ],
        out_shape=[
            jax.ShapeDtypeStruct((m, ATTN_WIDTH), BF16),
            jax.ShapeDtypeStruct((m, QKV_BLOCK), BF16),
            jax.ShapeDtypeStruct((m, QKV_BLOCK), F32),
            jax.ShapeDtypeStruct((batch, n_slabs, HEAD_DIM + ONES_ROWS, seq_len), BF16),
            jax.ShapeDtypeStruct((m, gw), F32),
            jax.ShapeDtypeStruct((m, CONV_WIDTH), BF16),
        ],
        scratch_shapes=[pltpu.VMEM((tm, D_MODEL), BF16),
                        pltpu.VMEM((CONV_WIDTH // CONV_BLOCK, SUBLANE, CONV_BLOCK), F32),
                        pltpu.VMEM((QKV_WIDTH, D_MODEL), BF16),
                        pltpu.VMEM((gw, D_MODEL), BF16),
                        pltpu.VMEM((3 * CONV_WIDTH, D_MODEL), BF16),
                        pltpu.VMEM((2, W_CHUNK, D_MODEL), F32),
                        pltpu.SemaphoreType.DMA((2,))],
        compiler_params=_params(1),
        name="in_proj",
    )(x2, g, w_t, conv_w)


def _compress_kernel(x_ref, pe_ref, w1k_ref, w1v_ref, w2k_ref, w2v_ref, o_ref, ot_ref):
    n = x_ref.shape[0] // CMP_STRIDE

    def body(w1_ref, w2_ref):
        a = jnp.zeros((n, CMP_HIDDEN), F32)
        b = jnp.zeros((n, CMP_HIDDEN), F32)
        for l in range(CMP_STRIDE):
            xl = x_ref[pl.ds(l, n, stride=CMP_STRIDE), :]
            xa = (xl + pe_ref[l:l + 1, :]).astype(BF16)
            xb = (xl + pe_ref[CMP_STRIDE + l:CMP_STRIDE + l + 1, :]).astype(BF16)
            a = a + jnp.dot(xa, w1_ref[l].astype(BF16), preferred_element_type=F32)
            b = b + jnp.dot(xb, w1_ref[CMP_STRIDE + l].astype(BF16), preferred_element_type=F32)
        pre = a + pltpu.roll(b, n - 1, axis=0)
        hid = pre * jax.nn.sigmoid(pre)
        out = jnp.dot(hid.astype(BF16), w2_ref[...].astype(BF16), preferred_element_type=F32)
        row = lax.broadcasted_iota(jnp.int32, out.shape, 0)
        out = jnp.where(row < n - 1, out, 0.0)
        o_ref[...] = out.astype(BF16)
        ot_ref[...] = out.T.astype(BF16)

    is_value = pl.program_id(1) >= N_KV
    pl.when(jnp.logical_not(is_value))(functools.partial(body, w1k_ref, w2k_ref))
    pl.when(is_value)(functools.partial(body, w1v_ref, w2v_ref))


def _compress(c_in, pe, w1_kv, w2_kv, batch, seq_len):
    n_slabs = c_in.shape[1] // HEAD_DIM
    n_chunks = seq_len // CMP_STRIDE

    def resident(shape):
        return pl.BlockSpec(shape, lambda i, j: (0,) * len(shape), pipeline_mode=pl.Buffered(1))

    return pl.pallas_call(
        _compress_kernel,
        grid=(batch, n_slabs),
        in_specs=[
            pl.BlockSpec((seq_len, HEAD_DIM), lambda i, j: (i, j)),
            resident((CMP_LEN, HEAD_DIM)),
            resident(w1_kv[0].shape), resident(w1_kv[1].shape),
            resident(w2_kv[0].shape), resident(w2_kv[1].shape),
        ],
        out_specs=[
            pl.BlockSpec((None, None, n_chunks, HEAD_DIM), lambda i, j: (i, j, 0, 0)),
            pl.BlockSpec((None, None, HEAD_DIM, n_chunks), lambda i, j: (i, j, 0, 0)),
        ],
        out_shape=[
            jax.ShapeDtypeStruct((batch, n_slabs, n_chunks, HEAD_DIM), BF16),
            jax.ShapeDtypeStruct((batch, n_slabs, HEAD_DIM, n_chunks), BF16),
        ],
        compiler_params=_params(2),
        name="compress",
    )(c_in, pe, *w1_kv, *w2_kv)


TAB_DIAG, TAB_SUB, TAB_FAR = 0, 1, 2
BAND_BELOW = 8
ONES_ROWS = 16


N_NSA_INPUTS = 12
BIAS_LEN = 1024
_END = object()


NSA_CLASS = 2


def _nsa_kernel(*refs, n_tiles, **static):
    i = pl.program_id(2)
    for lo in range(0, n_tiles, NSA_CLASS):
        hi = min(lo + NSA_CLASS, n_tiles) - 1
        pl.when((i >= lo) & (i <= hi))(functools.partial(_nsa_step, refs, lo, hi, n_tiles, **static))


def _spread(main, side, n_main, n_side):
    done = 0
    for m, _ in enumerate(main, 1):
        want = (m * n_side) // n_main
        while done < want and next(side, _END) is not _END:
            done += 1
    for _ in side:
        pass


def _nsa_step(refs, i_lo, i_hi, n_tiles, *, tq, n_sel, n_cast):
    i = pl.program_id(2)
    (q_ref, qn_ref, kc_ref, vct_ref, ks_ref, vst_ref, kw_ref, vwt_ref, gate_ref, fvec_ref,
     ovl_ref, et_ref) = refs[:N_NSA_INPUTS]
    cast_in = refs[N_NSA_INPUTS:N_NSA_INPUTS + n_cast]
    o_ref = refs[N_NSA_INPUTS + n_cast]
    cast_out = refs[N_NSA_INPUTS + n_cast + 1:N_NSA_INPUTS + 2 * n_cast + 1]
    (kaug_ref, tab_ref, band_ref, sc_ref, s0_ref, s1_ref, sw0_ref, sw1_ref, m_ref, mw_ref,
     acc_ref, accw_ref, qa_ref, oc_ref, ow_ref) = refs[N_NSA_INPUTS + 2 * n_cast + 1:]

    tk = tq
    mcols = HPG * tq
    nw = WINDOW // tk
    cpt = tq // CMP_STRIDE
    band_rows = BAND_BELOW + cpt

    def group_start(fn):
        if i_lo == 0:
            pl.when(i == 0)(fn)

    @group_start
    def _():
        kaug_ref[:, 0:HEAD_DIM] = ks_ref[...]
        kaug_ref[:, HEAD_DIM:2 * HEAD_DIM] = et_ref[...]

    @group_start
    def _():
        c = lax.broadcasted_iota(jnp.int32, (tk, tq), 0)
        r = lax.broadcasted_iota(jnp.int32, (tk, tq), 1)
        far = jnp.where(r < c, 0.0, NEG)
        lane = lax.broadcasted_iota(jnp.int32, (1, BIAS_LEN), 1)
        for h in range(HPG):
            cols = slice(h * tq, (h + 1) * tq)
            f = fvec_ref[h]
            f_diag = jnp.where(lane < tq, f, NEG)
            x = pltpu.roll(jnp.broadcast_to(f_diag, (tk, BIAS_LEN)), 0, 1, stride=1, stride_axis=0)
            tab_ref[TAB_DIAG, :, cols] = x[:, 0:tq]
            x = pltpu.roll(jnp.broadcast_to(f, (tk, BIAS_LEN)), 0, 1, stride=1, stride_axis=0)
            tab_ref[TAB_SUB, :, cols] = x[:, tq:2 * tq]
            tab_ref[TAB_FAR, :, cols] = far
            for v, first in enumerate((0, -BAND_BELOW)):
                shift = (CMP_STRIDE * first + CMP_LEN - 1) % BIAS_LEN
                f_shift = pltpu.roll(f, shift, 1)
                x = pltpu.roll(jnp.broadcast_to(f_shift, (band_rows, BIAS_LEN)), 0, 1,
                               stride=CMP_STRIDE, stride_axis=0)
                band_ref[v, :, cols] = x[:, 0:tq]

    for src, dst in zip(cast_in, cast_out):
        dst[...] = src[...].astype(BF16)

    def keys(kt):
        if isinstance(kt, int):
            return slice(kt * tk, (kt + 1) * tk)
        return pl.ds(pl.multiple_of(kt * tk, tk), tk)

    class Flash:
        def __init__(self, s_refs, m_ref, acc_ref):
            self.s_refs, self.m_ref, self.acc_ref = s_refs, m_ref, acc_ref

        def init(self):
            self.m_ref[...] = jnp.full((1, mcols), NEG, F32)
            self.acc_ref[...] = jnp.zeros(self.acc_ref.shape, F32)

        def qk(self, qmat, k_ref, kt, kind, buf, exists=None):
            s = lax.dot_general(k_ref[keys(kt), :], qmat, _DN_T, preferred_element_type=F32)
            if kind is not None:
                s = s + tab_ref[kind]
            if exists is not None:
                s = s + jnp.where(exists, 0.0, NEG)
            self.s_refs[buf][...] = s
            return jnp.max(s, axis=0, keepdims=True)

        def process(self, vt_ref, kt, buf, m_tile):
            m_prev = self.m_ref[...]
            m_next = jnp.maximum(m_prev, m_tile)
            alpha = jnp.exp2(m_prev - m_next)
            p = jnp.exp2(self.s_refs[buf][...] - m_next).astype(BF16)
            self.acc_ref[...] = alpha * self.acc_ref[...] + jnp.dot(
                vt_ref[:, keys(kt)], p, preferred_element_type=F32)
            self.m_ref[...] = m_next

        def finish(self):
            return self.acc_ref[0:HEAD_DIM, :] * (1.0 / self.acc_ref[HEAD_DIM:HEAD_DIM + 1, :])

    win = Flash((sw0_ref, sw1_ref), mw_ref, accw_ref)
    sel = Flash((s0_ref, s1_ref), m_ref, acc_ref)

    assert nw == 2, "window tiles are t, t-1 (previous-tile table) and t-nw (window-edge table)"

    def front_pieces(t_hi):
        n_valid = (t_hi + 1) * tq // SEL_LEN
        return 6 + -(-n_valid // SUBLANE)

    def front(t, t_lo, t_hi, qsrc_ref, par):
        n_valid = (t_hi + 1) * tq // SEL_LEN
        nb8 = -(-n_valid // SUBLANE) * SUBLANE
        nb16 = -(-n_valid // 16) * 16
        nc = -(-(cpt * (t_hi + 1)) // LANE) * LANE
        n_win = min(t_hi, nw) + 1
        q = qsrc_ref[...]
        qs = jnp.concatenate([q[:, h * HEAD_DIM:(h + 1) * HEAD_DIM] for h in range(HPG)], axis=0)

        def win_tile(d):
            if t_lo >= d:
                return t - d, None
            return jnp.maximum(t - d, 0), t >= d

        win.init()
        mt_w0 = win.qk(qs, kw_ref, t, TAB_DIAG, 0)
        raw = lax.dot_general(kc_ref[0:nc, :], qs, _DN_T, preferred_element_type=F32)
        crow = lax.broadcasted_iota(jnp.int32, raw.shape, 0)
        sc_ref[0:nc, :] = jnp.where(crow < cpt * (t + 1), raw, NEG)
        if isinstance(t, int):
            b0 = max(cpt * t - BAND_BELOW, 0)
            sc_ref[b0:b0 + band_rows, :] += band_ref[min(t, 1)]
        else:
            band = pl.ds(pl.multiple_of(jnp.maximum(cpt * t - BAND_BELOW, 0), SUBLANE), band_rows)
            sc_ref[band, :] += band_ref[jnp.minimum(t, 1)]
        yield

        if n_win > 1:
            kt1, ex1 = win_tile(1)
            mt_w1 = win.qk(qs, kw_ref, kt1, TAB_SUB, 1, ex1)
        win.process(vwt_ref, t, 0, mt_w0)
        yield

        sc = sc_ref[0:nc, :]
        mc = jnp.maximum(jnp.max(sc, axis=0, keepdims=True), HALF_NEG)
        pc = jnp.exp2(sc - mc)
        lc = jnp.sum(pc, axis=0, keepdims=True)
        pc = pc * jnp.where(lc > 0.0, 1.0 / lc, 0.0)
        oc_ref[par] = jnp.dot(vct_ref[:, 0:nc], pc.astype(BF16), preferred_element_type=F32)
        yield

        if n_win > 2:
            kt2, ex2 = win_tile(nw)
            mt_w2 = win.qk(qs, kw_ref, kt2, TAB_FAR, 0, ex2)
        if n_win > 1:
            win.process(vwt_ref, kt1, 1, mt_w1)
        yield

        ps = pc[:, 0:tq] + pc[:, tq:2 * tq] + pc[:, 2 * tq:3 * tq] + pc[:, 3 * tq:4 * tq]
        hi = ps.astype(BF16)
        r1 = ps - hi.astype(F32)
        mid = r1.astype(BF16)
        lo = (r1 - mid.astype(F32)).astype(BF16)
        ovl = ovl_ref[0:nb16, 0:nc]
        imp = (jnp.dot(ovl, hi, preferred_element_type=F32) + jnp.dot(ovl, mid, preferred_element_type=F32)
               + jnp.dot(ovl, lo, preferred_element_type=F32))
        jj = lax.broadcasted_iota(jnp.int32, (nb16, tq), 0)
        tt = t * tq + lax.broadcasted_iota(jnp.int32, (nb16, tq), 1)
        cur = tt >> int(math.log2(SEL_LEN))
        forced = (jj == 0) | (jj == cur) | (jj == cur - 1)
        imp = jnp.where(forced, FORCE, imp)
        imp = jnp.where(jj * SEL_LEN <= tt, imp, NEG)
        yield

        sub = SUBLANE
        ranks = []
        for j0 in range(0, nb8, sub):
            blk = imp[j0:j0 + sub, :]
            jl = j0 + lax.broadcasted_iota(jnp.int32, blk.shape, 0)
            cnt = jnp.zeros(blk.shape, jnp.int32)
            for b in range(n_valid):
                row = imp[b:b + 1, :]
                if b < j0:
                    cnt = cnt + jnp.where(row >= blk, 1, 0)
                elif b >= j0 + sub:
                    cnt = cnt + jnp.where(row > blk, 1, 0)
                else:
                    cnt = cnt + jnp.where(row > blk, 1, jnp.where(row == blk, jnp.where(jl > b, 1, 0), 0))
            ranks.append(cnt)
            yield
        rank = jnp.concatenate(ranks, axis=0)
        selb_t = jnp.where(rank < SEL_TOPK, 0.0, NEG)

        if n_win > 2:
            win.process(vwt_ref, kt2, 0, mt_w2)
        ow_ref[par] = win.finish()

        selb = jnp.concatenate([selb_t, jnp.zeros((LANE - nb8, tq), F32)], axis=0).T
        selb = selb.astype(BF16)
        qa_ref[par, :, 0:HEAD_DIM] = qs
        qa_ref[par, :, HEAD_DIM:2 * HEAD_DIM] = jnp.concatenate([selb] * HPG, axis=0)
        yield

    def selected(t, t_lo, t_hi):
        par = t % 2
        tiles = []
        for d in range(t_hi + 1):
            kind = (TAB_DIAG, TAB_SUB)[d] if d < 2 else None
            tiles.append((t - d, kind, None) if t_lo >= d else (jnp.maximum(t - d, 0), kind, t >= d))
        sel.init()
        mt = sel.qk(qa_ref[par], kaug_ref, tiles[0][0], tiles[0][1], 0)
        for n, (kt, _, _) in enumerate(tiles):
            nxt = None
            if n + 1 < len(tiles):
                kt_n, kind_n, ex_n = tiles[n + 1]
                nxt = sel.qk(qa_ref[par], kaug_ref, kt_n, kind_n, (n + 1) % 2, ex_n)
            sel.process(vst_ref, kt, n % 2, mt)
            mt = nxt
            yield
        o_s = sel.finish()
        gt = gate_ref[...].T
        for h in range(HPG):
            sl = slice(h * tq, (h + 1) * tq)
            o = gt[3 * h:3 * h + 1, :] * oc_ref[par, :, sl]
            o = o + gt[3 * h + 1:3 * h + 2, :] * o_s[:, sl]
            o = o + gt[3 * h + 2:3 * h + 3, :] * ow_ref[par, :, sl]
            o_ref[:, h * HEAD_DIM:(h + 1) * HEAD_DIM] = o.T.astype(BF16)
        yield

    if i_lo == 0:
        @pl.when(i == 0)
        def _():
            for _ in front(0, 0, 0, q_ref, 0):
                pass
    t_hi = min(i_hi + 1, n_tiles - 1)
    t_next = jnp.minimum(i + 1, n_tiles - 1)
    _spread(selected(i, i_lo, i_hi), front(t_next, i_lo + 1, t_hi, qn_ref, (i + 1) % 2),
            i_hi + 2, front_pieces(t_hi))


def _nsa(q, k_all, vt_all, cmp, cmp_t, gates, fvec, ovl, e_t, cast_weights, batch, seq_len, *, tq):
    n_t = seq_len // tq
    n_steps = batch * N_KV * n_t
    ts = tq
    step_row = lambda b, g, i: ((b * N_KV + g) * n_t + i, 0)
    cast_specs = [pl.BlockSpec((w.shape[0] // n_steps, w.shape[1]), step_row) for w in cast_weights]
    n_sel = seq_len // SEL_LEN
    n_chunks = cmp.shape[2]
    mcols = HPG * tq
    band_rows = BAND_BELOW + tq // CMP_STRIDE
    qw = HPG * HEAD_DIM

    def vt_spec(first):
        return pl.BlockSpec((None, None, HEAD_DIM + ONES_ROWS, seq_len), lambda b, g, i: (b, first + g, 0, 0))

    kern = functools.partial(_nsa_kernel, n_tiles=n_t, tq=tq, n_sel=n_sel, n_cast=len(cast_weights))
    o_attn, *cast = pl.pallas_call(
        kern,
        grid=(batch, N_KV, n_t),
        in_specs=[
            pl.BlockSpec((ts, qw), lambda b, g, i: (b * n_t + i, g)),
            pl.BlockSpec((ts, qw), lambda b, g, i: (b * n_t + jnp.minimum(i + 1, n_t - 1), g)),
            pl.BlockSpec((None, None, n_chunks, HEAD_DIM), lambda b, g, i: (b, g, 0, 0)),
            pl.BlockSpec((None, None, HEAD_DIM, n_chunks), lambda b, g, i: (b, N_KV + g, 0, 0)),
            pl.BlockSpec((seq_len, HEAD_DIM), lambda b, g, i: (b, g)),
            vt_spec(0),
            pl.BlockSpec((seq_len, HEAD_DIM), lambda b, g, i: (b, N_KV + g)),
            vt_spec(N_KV),
            pl.BlockSpec((ts, LANE), lambda b, g, i: (b * n_t + i, g)),
            pl.BlockSpec((None, HPG, 1, BIAS_LEN), lambda b, g, i: (g, 0, 0, 0)),
            pl.BlockSpec((n_sel, n_chunks), lambda b, g, i: (0, 0)),
            pl.BlockSpec((seq_len, LANE), lambda b, g, i: (0, 0)),
        ] + cast_specs,
        out_specs=[pl.BlockSpec((ts, qw), lambda b, g, i: (b * n_t + i, g))] + cast_specs,
        out_shape=[jax.ShapeDtypeStruct((batch * seq_len, ATTN_WIDTH), BF16)]
        + [jax.ShapeDtypeStruct(w.shape, BF16) for w in cast_weights],
        scratch_shapes=[
            pltpu.VMEM((seq_len, 2 * HEAD_DIM), BF16),
            pltpu.VMEM((3, tq, mcols), F32),
            pltpu.VMEM((2, band_rows, mcols), F32),
            pltpu.VMEM((n_chunks, mcols), F32),
            pltpu.VMEM((tq, mcols), F32),
            pltpu.VMEM((tq, mcols), F32),
            pltpu.VMEM((tq, mcols), F32),
            pltpu.VMEM((tq, mcols), F32),
            pltpu.VMEM((1, mcols), F32),
            pltpu.VMEM((1, mcols), F32),
            pltpu.VMEM((HEAD_DIM + ONES_ROWS, mcols), F32),
            pltpu.VMEM((HEAD_DIM + ONES_ROWS, mcols), F32),
            pltpu.VMEM((2, mcols, 2 * HEAD_DIM), BF16),
            pltpu.VMEM((2, HEAD_DIM, mcols), F32),
            pltpu.VMEM((2, HEAD_DIM, mcols), F32),
        ],
        compiler_params=_params(3),
        name="nsa_attention",
    )(q, q, cmp, cmp_t, k_all, vt_all, k_all, vt_all, gates, fvec, ovl, e_t, *cast_weights)
    return o_attn, cast


OPROJ_CHUNKS = 4


def _oproj_kernel(oa_ref, ov_ref, wo_ref, x_ref, g1_ref, g2_ref, x1_ref, h2_ref):
    ka = oa_ref.shape[1]
    tm = oa_ref.shape[0]
    chunk = tm // OPROJ_CHUNKS
    for rows in (slice(k * chunk, (k + 1) * chunk) for k in range(OPROJ_CHUNKS)):
        mix = jnp.dot(oa_ref[rows, :], wo_ref[0:ka, :], preferred_element_type=F32)
        mix = mix + jnp.dot(ov_ref[rows, :], wo_ref[ka:, :], preferred_element_type=F32)
        x1 = x_ref[rows, :] + _rms(mix, g1_ref[...])
        x1_ref[rows, :] = x1
        h2_ref[rows, :] = _rms(x1, g2_ref[...]).astype(BF16)


def _oproj(o_attn, o_conv, w_o, x2, g_post, g_pre, *, tm=512):
    m = x2.shape[0]
    ka, kv = o_attn.shape[1], o_conv.shape[1]
    row = lambda i: (i, 0)
    fixed = lambda i: (0, 0)
    return pl.pallas_call(
        _oproj_kernel,
        grid=(m // tm,),
        in_specs=[
            pl.BlockSpec((tm, ka), row),
            pl.BlockSpec((tm, kv), row),
            pl.BlockSpec((ka + kv, D_MODEL), fixed),
            pl.BlockSpec((tm, D_MODEL), row),
            pl.BlockSpec((1, D_MODEL), fixed),
            pl.BlockSpec((1, D_MODEL), fixed),
        ],
        out_specs=[pl.BlockSpec((tm, D_MODEL), row), pl.BlockSpec((tm, D_MODEL), row)],
        out_shape=[jax.ShapeDtypeStruct((m, D_MODEL), F32), jax.ShapeDtypeStruct((m, D_MODEL), BF16)],
        compiler_params=_params(1),
        name="out_proj",
    )(o_attn, o_conv, w_o, x2, g_post, g_pre)


FFN_SPLIT = 2


def _ffn_kernel(h_ref, wu_ref, wd_ref, x1_hbm, g_ref, o_ref, x1_ref, sem):
    i = pl.program_id(0)
    j = pl.program_id(1)
    last = pl.num_programs(1) - 1
    tm = h_ref.shape[0]
    width = wu_ref.shape[1] // FFN_SPLIT
    x1_copy = pltpu.make_async_copy(x1_hbm.at[pl.ds(pl.multiple_of(i * tm, tm), tm), :], x1_ref, sem)

    def step(mode):
        if mode == "first":
            x1_copy.start()
        for k in range(FFN_SPLIT):
            cols = slice(k * width, (k + 1) * width)
            a = jnp.dot(h_ref[...], wu_ref[:, cols], preferred_element_type=F32)
            a = jnp.square(jnp.maximum(a, 0.0)).astype(BF16)
            if mode == "first" and k == 0:
                o_ref[...] = jnp.dot(a, wd_ref[cols, :], preferred_element_type=F32)
            elif mode != "last" or k < FFN_SPLIT - 1:
                o_ref[...] += jnp.dot(a, wd_ref[cols, :], preferred_element_type=F32)
            else:
                x1_copy.wait()
                for rows in (slice(0, tm // 2), slice(tm // 2, tm)):
                    f = o_ref[rows, :] + jnp.dot(a[rows, :], wd_ref[cols, :], preferred_element_type=F32)
                    o_ref[rows, :] = x1_ref[rows, :] + _rms(f, g_ref[...])

    pl.when(j == 0)(functools.partial(step, "first"))
    pl.when((j > 0) & (j < last))(functools.partial(step, "middle"))
    pl.when(j == last)(functools.partial(step, "last"))


def _ffn(h2, w_up, w_down, x1, g_post, *, tm=512, tf=2048):
    m = h2.shape[0]
    d_ff = w_up.shape[1]
    assert d_ff // tf >= 2
    return pl.pallas_call(
        _ffn_kernel,
        grid=(m // tm, d_ff // tf),
        in_specs=[
            pl.BlockSpec((tm, D_MODEL), lambda i, j: (i, 0)),
            pl.BlockSpec((D_MODEL, tf), lambda i, j: (0, j)),
            pl.BlockSpec((tf, D_MODEL), lambda i, j: (j, 0)),
            pl.BlockSpec(memory_space=pl.ANY),
            pl.BlockSpec((1, D_MODEL), lambda i, j: (0, 0)),
        ],
        out_specs=pl.BlockSpec((tm, D_MODEL), lambda i, j: (i, 0)),
        out_shape=jax.ShapeDtypeStruct((m, D_MODEL), F32),
        scratch_shapes=[pltpu.VMEM((tm, D_MODEL), F32), pltpu.SemaphoreType.DMA(())],
        compiler_params=_params(2),
        name="ffn",
    )(h2, w_up, w_down, x1, g_post)


def _bucket_np(dist):
    n = np.maximum(dist, 0)
    max_exact = N_BUCKETS // 2
    nf = np.maximum(n, 1).astype(np.float32)
    large = max_exact + (np.log(nf / np.float32(max_exact)) / np.float32(math.log(MAX_DIST / max_exact))
                         * np.float32(N_BUCKETS - max_exact)).astype(np.int32)
    large = np.minimum(large, N_BUCKETS - 1)
    return np.where(n < max_exact, n, large)


def _bucket_starts():
    b = _bucket_np(np.arange(4 * MAX_DIST))
    return [int(np.argmax(b == k)) for k in range(N_BUCKETS)]


def _attention_tables(rel_bias, seq_len, tq):
    starts = _bucket_starts()
    assert starts[N_BUCKETS - 1] <= CMP_STRIDE * (BAND_BELOW + 1) - (CMP_LEN - 1)
    assert starts[N_BUCKETS - 1] <= tq and 2 * tq <= BIAS_LEN // 2
    rel = (rel_bias - rel_bias[:, N_BUCKETS - 1:]) * math.log2(math.e)
    d = jnp.arange(BIAS_LEN, dtype=jnp.int32)[None, :]
    fvec = jnp.broadcast_to(rel[:, 0:1], (N_HEADS, BIAS_LEN))
    for k in range(1, N_BUCKETS):
        fvec = jnp.where(d >= starts[k], rel[:, k:k + 1], fvec)
    fvec = jnp.where(d < BIAS_LEN // 2, fvec, NEG).astype(F32).reshape(N_KV, HPG, 1, BIAS_LEN)

    n_chunks = seq_len // CMP_STRIDE
    n_cmp = (seq_len - CMP_LEN) // CMP_STRIDE + 1
    n_sel = seq_len // SEL_LEN
    ci = np.arange(n_chunks)[None, :] * CMP_STRIDE
    sj = np.arange(n_sel)[:, None] * SEL_LEN
    ovl = ((ci < sj + SEL_LEN) & (ci + CMP_LEN > sj) & (np.arange(n_chunks)[None, :] < n_cmp))
    e_t = (np.arange(seq_len)[:, None] // SEL_LEN == np.arange(LANE)[None, :])
    return fvec, jnp.asarray(ovl, BF16), jnp.asarray(e_t, BF16)


def kernel(x, w_in, pe_cmp, w_cmp_k1, w_cmp_k2, w_cmp_v1, w_cmp_v2, conv_w, rel_bias, w_o, w_up, w_down,
           g_pre_mix, g_post_mix, g_pre_ffn, g_post_ffn):
    batch, seq_len, _ = x.shape
    depth = w_in.shape[0]
    tq = 256
    fvec, ovl, e_t = _attention_tables(rel_bias, seq_len, tq)
    x2 = x.reshape(batch * seq_len, D_MODEL)
    for l in range(depth):
        wl = jnp.swapaxes(w_in[l], 0, 1)
        g1 = g_pre_mix[l].reshape(1, D_MODEL)

        q, k_all, c_in, vt_all, gates, o_conv = _in_proj(x2, g1, wl, conv_w[l], batch, seq_len)

        cmp, cmp_t = _compress(c_in, pe_cmp[l], (w_cmp_k1[l], w_cmp_v1[l]), (w_cmp_k2[l], w_cmp_v2[l]),
                               batch, seq_len)

        o_attn, (wo_b, wup_b, wdown_b) = _nsa(q, k_all, vt_all, cmp, cmp_t, gates, fvec, ovl, e_t,
                                              [w_o[l], w_up[l], w_down[l]], batch, seq_len, tq=tq)

        x1, h2 = _oproj(o_attn, o_conv, wo_b, x2,
                        g_post_mix[l].reshape(1, D_MODEL), g_pre_ffn[l].reshape(1, D_MODEL))
        x2 = _ffn(h2, wup_b, wdown_b, x1, g_post_ffn[l].reshape(1, D_MODEL))
    return x2.reshape(batch, seq_len, D_MODEL)
```

```python
import functools
import math

import numpy as np
import jax
import jax.numpy as jnp
from jax import lax
from jax.experimental import pallas as pl
from jax.experimental.pallas import tpu as pltpu

F32 = jnp.float32
BF16 = jnp.bfloat16

D_MODEL = 2048
N_HEADS = 8
N_KV = 2
HPG = N_HEADS // N_KV
HEAD_DIM = 128
ATTN_WIDTH = N_HEADS * HEAD_DIM
KV_WIDTH = N_KV * HEAD_DIM
CONV_WIDTH = D_MODEL - ATTN_WIDTH
CONV_K = 3
N_BRANCH = 3
CMP_LEN = 32
CMP_STRIDE = 16
CMP_HIDDEN = 256
SEL_LEN = 64
SEL_TOPK = 16
WINDOW = 512
N_BUCKETS = 32
MAX_DIST = 128
EPS = 1e-6
NEG = -1e30
HALF_NEG = -5e29
FORCE = 1e9

QKV_WIDTH = ATTN_WIDTH + 6 * KV_WIDTH
GATE_OFF = QKV_WIDTH
CONV_OFF = QKV_WIDTH + N_HEADS * N_BRANCH
LANE = 128
VMEM_LIMIT = 56 * 1024 * 1024

_DN_T = (((1,), (1,)), ((), ()))


def _rms(x, g):
    ms = jnp.mean(x * x, axis=-1, keepdims=True)
    return x * lax.rsqrt(ms + EPS) * g


def _params(n_axes):
    return pltpu.CompilerParams(dimension_semantics=("arbitrary",) * n_axes, vmem_limit_bytes=VMEM_LIMIT)


QKV_BLOCK = 2 * KV_WIDTH
SLAB_KC, SLAB_VC, SLAB_KS, SLAB_VS, SLAB_KW, SLAB_VW = range(6)
CONV_BLOCK = 512
W_CHUNK = 256
SUBLANE = 8


def _load_weights(wt_hbm, w_ref, wg_ref, wconv_ref, stage_ref, sem):
    n_gate = N_HEADS * N_BRANCH
    per_group = HPG * N_BRANCH

    def store_rows(dst, row):
        def store(v):
            dst[row:row + v.shape[0], :] = v.astype(BF16)
        return store

    def store_gates(v):
        wg_ref[...] = jnp.zeros(wg_ref.shape, BF16)
        pad = jnp.zeros((2 * SUBLANE - per_group, v.shape[1]), F32)
        for g in range(N_KV):
            rows = jnp.concatenate([v[g * per_group:(g + 1) * per_group, :], pad], axis=0)
            wg_ref[g * LANE:g * LANE + 2 * SUBLANE, :] = rows.astype(BF16)

    chunks = [(r, W_CHUNK, store_rows(w_ref, r)) for r in range(0, QKV_WIDTH, W_CHUNK)]
    chunks.append((GATE_OFF, n_gate, store_gates))
    chunks += [(CONV_OFF + r, W_CHUNK, store_rows(wconv_ref, r)) for r in range(0, 3 * CONV_WIDTH, W_CHUNK)]

    def copy(k):
        src, n, _ = chunks[k]
        slot = k % 2
        return pltpu.make_async_copy(wt_hbm.at[pl.ds(src, n), :], stage_ref.at[slot, pl.ds(0, n), :], sem.at[slot])

    copy(0).start()
    for k, (_, n, store) in enumerate(chunks):
        if k + 1 < len(chunks):
            copy(k + 1).start()
        copy(k).wait()
        store(stage_ref[k % 2, 0:n, :])


def _inproj_kernel(x_ref, g_ref, wt_hbm, cw_ref, q_ref, k_ref, c_ref, vt_ref, gate_ref, ov_ref,
                   h_ref, carry_ref, w_ref, wg_ref, wconv_ref, stage_ref, sem, *, q_scale, tiles_per_seq):
    i = pl.program_id(0)
    tm = x_ref.shape[0]

    @pl.when(i == 0)
    def _():
        _load_weights(wt_hbm, w_ref, wg_ref, wconv_ref, stage_ref, sem)

    h_ref[...] = _rms(x_ref[...], g_ref[...]).astype(BF16)
    gate_ref[...] = jax.nn.sigmoid(lax.dot_general(h_ref[...], wg_ref[...], _DN_T, preferred_element_type=F32))

    def proj(w, j, width):
        return lax.dot_general(h_ref[...], w[j * width:(j + 1) * width, :], _DN_T, preferred_element_type=F32)

    for j in range(ATTN_WIDTH // QKV_BLOCK):
        q_ref[:, j * QKV_BLOCK:(j + 1) * QKV_BLOCK] = (proj(w_ref, j, QKV_BLOCK) * q_scale).astype(BF16)

    def slab(which):
        return proj(w_ref, ATTN_WIDTH // KV_WIDTH + which, KV_WIDTH)

    ones = jnp.where(lax.broadcasted_iota(jnp.int32, (ONES_ROWS, tm), 0) == 0, 1.0, 0.0).astype(BF16)
    for pair, (k_slab, c_slab, v_slab) in enumerate(((SLAB_KS, SLAB_KC, SLAB_VS), (SLAB_KW, SLAB_VC, SLAB_VW))):
        cols = slice(pair * KV_WIDTH, (pair + 1) * KV_WIDTH)
        k_ref[:, cols] = slab(k_slab).astype(BF16)
        c_ref[:, cols] = slab(c_slab)
        v = slab(v_slab)
        for g in range(N_KV):
            s = pair * N_KV + g
            vt_ref[s, 0:HEAD_DIM, :] = v[:, g * HEAD_DIM:(g + 1) * HEAD_DIM].T.astype(BF16)
            vt_ref[s, HEAD_DIM:HEAD_DIM + ONES_ROWS, :] = ones

    def conv_proj(which, j):
        return proj(wconv_ref, which * (CONV_WIDTH // CONV_BLOCK) + j, CONV_BLOCK)

    row = lax.broadcasted_iota(jnp.int32, (tm, CONV_BLOCK), 0)
    for j in range(CONV_WIDTH // CONV_BLOCK):
        cols = slice(j * CONV_BLOCK, (j + 1) * CONV_BLOCK)
        u = conv_proj(2, j) * conv_proj(0, j)
        prev = carry_ref[j]
        prev = jnp.where(i % tiles_per_seq == 0, 0.0, prev)
        carry_ref[j] = u[tm - SUBLANE:tm, :]
        p1 = prev[SUBLANE - 1:SUBLANE, :]
        p2 = prev[SUBLANE - 2:SUBLANE - 1, :]
        u1 = jnp.where(row == 0, p1, pltpu.roll(u, 1, axis=0))
        u2 = jnp.where(row == 0, p2, jnp.where(row == 1, p1, pltpu.roll(u, 2, axis=0)))
        w = cw_ref[:, cols]
        y = w[0:1, :] * u2
        y = y + w[1:2, :] * u1
        y = y + w[2:3, :] * u
        ov_ref[:, cols] = (conv_proj(1, j) * y).astype(BF16)


def _in_proj(x2, g, w_t, conv_w, batch, seq_len, *, tm=512):
    m = x2.shape[0]
    gw = N_KV * LANE
    tps = seq_len // tm
    n_slabs = QKV_BLOCK // HEAD_DIM
    kern = functools.partial(_inproj_kernel, q_scale=HEAD_DIM ** -0.5 * math.log2(math.e), tiles_per_seq=tps)
    row = lambda i: (i, 0)

    def resident(shape):
        return pl.BlockSpec(shape, lambda i: (0, 0), pipeline_mode=pl.Buffered(1))

    return pl.pallas_call(
        kern,
        grid=(m // tm,),
        in_specs=[
            pl.BlockSpec((tm, D_MODEL), row),
            resident((1, D_MODEL)),
            pl.BlockSpec(memory_space=pl.ANY),
            resident(conv_w.shape),
        ],
        out_specs=[
            pl.BlockSpec((tm, ATTN_WIDTH), row),
            pl.BlockSpec((tm, QKV_BLOCK), row),
            pl.BlockSpec((tm, QKV_BLOCK), row),
            pl.BlockSpec((None, n_slabs, HEAD_DIM + ONES_ROWS, tm), lambda i: (i // tps, 0, 0, i % tps)),
            pl.BlockSpec((tm, gw), row),
            pl.BlockSpec((tm, CONV_WIDTH), row),
        ],
        out_shape=[
            jax.ShapeDtypeStruct((m, ATTN_WIDTH), BF16),
            jax.ShapeDtypeStruct((m, QKV_BLOCK), BF16),
            jax.ShapeDtypeStruct((m, QKV_BLOCK), F32),
            jax.ShapeDtypeStruct((batch, n_slabs, HEAD_DIM + ONES_ROWS, seq_len), BF16),
            jax.ShapeDtypeStruct((m, gw), F32),
            jax.ShapeDtypeStruct((m, CONV_WIDTH), BF16),
        ],
        scratch_shapes=[pltpu.VMEM((tm, D_MODEL), BF16),
                        pltpu.VMEM((CONV_WIDTH // CONV_BLOCK, SUBLANE, CONV_BLOCK), F32),
                        pltpu.VMEM((QKV_WIDTH, D_MODEL), BF16),
                        pltpu.VMEM((gw, D_MODEL), BF16),
                        pltpu.VMEM((3 * CONV_WIDTH, D_MODEL), BF16),
                        pltpu.VMEM((2, W_CHUNK, D_MODEL), F32),
                        pltpu.SemaphoreType.DMA((2,))],
        compiler_params=_params(1),
        name="in_proj",
    )(x2, g, w_t, conv_w)


def _compress_kernel(x_ref, pe_ref, w1k_ref, w1v_ref, w2k_ref, w2v_ref, o_ref, ot_ref):
    n = x_ref.shape[0] // CMP_STRIDE

    def body(w1_ref, w2_ref):
        a = jnp.zeros((n, CMP_HIDDEN), F32)
        b = jnp.zeros((n, CMP_HIDDEN), F32)
        for l in range(CMP_STRIDE):
            xl = x_ref[pl.ds(l, n, stride=CMP_STRIDE), :]
            xa = (xl + pe_ref[l:l + 1, :]).astype(BF16)
            xb = (xl + pe_ref[CMP_STRIDE + l:CMP_STRIDE + l + 1, :]).astype(BF16)
            a = a + jnp.dot(xa, w1_ref[l].astype(BF16), preferred_element_type=F32)
            b = b + jnp.dot(xb, w1_ref[CMP_STRIDE + l].astype(BF16), preferred_element_type=F32)
        pre = a + pltpu.roll(b, n - 1, axis=0)
        hid = pre * jax.nn.sigmoid(pre)
        out = jnp.dot(hid.astype(BF16), w2_ref[...].astype(BF16), preferred_element_type=F32)
        row = lax.broadcasted_iota(jnp.int32, out.shape, 0)
        out = jnp.where(row < n - 1, out, 0.0)
        o_ref[...] = out.astype(BF16)
        ot_ref[...] = out.T.astype(BF16)

    is_value = pl.program_id(1) >= N_KV
    pl.when(jnp.logical_not(is_value))(functools.partial(body, w1k_ref, w2k_ref))
    pl.when(is_value)(functools.partial(body, w1v_ref, w2v_ref))


def _compress(c_in, pe, w1_kv, w2_kv, batch, seq_len):
    n_slabs = c_in.shape[1] // HEAD_DIM
    n_chunks = seq_len // CMP_STRIDE

    def resident(shape):
        return pl.BlockSpec(shape, lambda i, j: (0,) * len(shape), pipeline_mode=pl.Buffered(1))

    return pl.pallas_call(
        _compress_kernel,
        grid=(batch, n_slabs),
        in_specs=[
            pl.BlockSpec((seq_len, HEAD_DIM), lambda i, j: (i, j)),
            resident((CMP_LEN, HEAD_DIM)),
            resident(w1_kv[0].shape), resident(w1_kv[1].shape),
            resident(w2_kv[0].shape), resident(w2_kv[1].shape),
        ],
        out_specs=[
            pl.BlockSpec((None, None, n_chunks, HEAD_DIM), lambda i, j: (i, j, 0, 0)),
            pl.BlockSpec((None, None, HEAD_DIM, n_chunks), lambda i, j: (i, j, 0, 0)),
        ],
        out_shape=[
            jax.ShapeDtypeStruct((batch, n_slabs, n_chunks, HEAD_DIM), BF16),
            jax.ShapeDtypeStruct((batch, n_slabs, HEAD_DIM, n_chunks), BF16),
        ],
        compiler_params=_params(2),
        name="compress",
    )(c_in, pe, *w1_kv, *w2_kv)


TAB_DIAG, TAB_SUB, TAB_FAR = 0, 1, 2
BAND_BELOW = 8
ONES_ROWS = 16


N_NSA_INPUTS = 12
BIAS_LEN = 1024
_END = object()


NSA_CLASSES = 3


def _nsa_kernel(*refs, n_tiles, **static):
    i = pl.program_id(2)
    n_cls = min(NSA_CLASSES, n_tiles)
    lo = 0
    for c in range(n_cls):
        hi = lo + n_tiles // n_cls + (1 if c < n_tiles % n_cls else 0) - 1
        pl.when((i >= lo) & (i <= hi))(functools.partial(_nsa_step, refs, lo, hi, n_tiles, **static))
        lo = hi + 1


def _spread(main, side, n_main, n_side):
    done = 0
    for m, _ in enumerate(main, 1):
        want = (m * n_side) // n_main
        while done < want and next(side, _END) is not _END:
            done += 1
    for _ in side:
        pass


def _nsa_step(refs, i_lo, i_hi, n_tiles, *, tq, n_sel, n_cast):
    i = pl.program_id(2)
    (q_ref, qn_ref, kc_ref, vct_ref, ks_ref, vst_ref, kw_ref, vwt_ref, gate_ref, fvec_ref,
     ovl_ref, et_ref) = refs[:N_NSA_INPUTS]
    cast_in = refs[N_NSA_INPUTS:N_NSA_INPUTS + n_cast]
    o_ref = refs[N_NSA_INPUTS + n_cast]
    cast_out = refs[N_NSA_INPUTS + n_cast + 1:N_NSA_INPUTS + 2 * n_cast + 1]
    (kaug_ref, tab_ref, band_ref, sc_ref, s0_ref, s1_ref, sw0_ref, sw1_ref, m_ref, mw_ref,
     acc_ref, accw_ref, qa_ref, oc_ref, ow_ref) = refs[N_NSA_INPUTS + 2 * n_cast + 1:]

    tk = tq
    mcols = HPG * tq
    nw = WINDOW // tk
    cpt = tq // CMP_STRIDE
    band_rows = BAND_BELOW + cpt

    def group_start(fn):
        if i_lo == 0:
            pl.when(i == 0)(fn)

    @group_start
    def _():
        kaug_ref[:, 0:HEAD_DIM] = ks_ref[...]
        kaug_ref[:, HEAD_DIM:2 * HEAD_DIM] = et_ref[...]

    @group_start
    def _():
        c = lax.broadcasted_iota(jnp.int32, (tk, tq), 0)
        r = lax.broadcasted_iota(jnp.int32, (tk, tq), 1)
        far = jnp.where(r < c, 0.0, NEG)
        lane = lax.broadcasted_iota(jnp.int32, (1, BIAS_LEN), 1)
        for h in range(HPG):
            cols = slice(h * tq, (h + 1) * tq)
            f = fvec_ref[h]
            f_diag = jnp.where(lane < tq, f, NEG)
            x = pltpu.roll(jnp.broadcast_to(f_diag, (tk, BIAS_LEN)), 0, 1, stride=1, stride_axis=0)
            tab_ref[TAB_DIAG, :, cols] = x[:, 0:tq]
            x = pltpu.roll(jnp.broadcast_to(f, (tk, BIAS_LEN)), 0, 1, stride=1, stride_axis=0)
            tab_ref[TAB_SUB, :, cols] = x[:, tq:2 * tq]
            tab_ref[TAB_FAR, :, cols] = far
            for v, first in enumerate((0, -BAND_BELOW)):
                shift = (CMP_STRIDE * first + CMP_LEN - 1) % BIAS_LEN
                f_shift = pltpu.roll(f, shift, 1)
                x = pltpu.roll(jnp.broadcast_to(f_shift, (band_rows, BIAS_LEN)), 0, 1,
                               stride=CMP_STRIDE, stride_axis=0)
                band_ref[v, :, cols] = x[:, 0:tq]

    for src, dst in zip(cast_in, cast_out):
        dst[...] = src[...].astype(BF16)

    def keys(kt):
        if isinstance(kt, int):
            return slice(kt * tk, (kt + 1) * tk)
        return pl.ds(pl.multiple_of(kt * tk, tk), tk)

    class Flash:
        def __init__(self, s_refs, m_ref, acc_ref):
            self.s_refs, self.m_ref, self.acc_ref = s_refs, m_ref, acc_ref

        def init(self):
            self.m_ref[...] = jnp.full((1, mcols), NEG, F32)
            self.acc_ref[...] = jnp.zeros(self.acc_ref.shape, F32)

        def qk(self, qmat, k_ref, kt, kind, buf, exists=None):
            s = lax.dot_general(k_ref[keys(kt), :], qmat, _DN_T, preferred_element_type=F32)
            if kind is not None:
                s = s + tab_ref[kind]
            if exists is not None:
                s = s + jnp.where(exists, 0.0, NEG)
            self.s_refs[buf][...] = s
            return jnp.max(s, axis=0, keepdims=True)

        def process(self, vt_ref, kt, buf, m_tile):
            m_prev = self.m_ref[...]
            m_next = jnp.maximum(m_prev, m_tile)
            alpha = jnp.exp2(m_prev - m_next)
            p = jnp.exp2(self.s_refs[buf][...] - m_next).astype(BF16)
            self.acc_ref[...] = alpha * self.acc_ref[...] + jnp.dot(
                vt_ref[:, keys(kt)], p, preferred_element_type=F32)
            self.m_ref[...] = m_next

        def finish(self):
            return self.acc_ref[0:HEAD_DIM, :] * (1.0 / self.acc_ref[HEAD_DIM:HEAD_DIM + 1, :])

    win = Flash((sw0_ref, sw1_ref), mw_ref, accw_ref)
    sel = Flash((s0_ref, s1_ref), m_ref, acc_ref)

    assert nw == 2, "window tiles are t, t-1 (previous-tile table) and t-nw (window-edge table)"

    def front_pieces(t_hi):
        n_valid = (t_hi + 1) * tq // SEL_LEN
        return 6 + -(-n_valid // SUBLANE)

    def front(t, t_lo, t_hi, qsrc_ref, par):
        n_valid = (t_hi + 1) * tq // SEL_LEN
        nb8 = -(-n_valid // SUBLANE) * SUBLANE
        nb16 = -(-n_valid // 16) * 16
        nc = -(-(cpt * (t_hi + 1)) // LANE) * LANE
        n_win = min(t_hi, nw) + 1
        q = qsrc_ref[...]
        qs = jnp.concatenate([q[:, h * HEAD_DIM:(h + 1) * HEAD_DIM] for h in range(HPG)], axis=0)

        def win_tile(d):
            if t_lo >= d:
                return t - d, None
            return jnp.maximum(t - d, 0), t >= d

        win.init()
        mt_w0 = win.qk(qs, kw_ref, t, TAB_DIAG, 0)
        raw = lax.dot_general(kc_ref[0:nc, :], qs, _DN_T, preferred_element_type=F32)
        crow = lax.broadcasted_iota(jnp.int32, raw.shape, 0)
        sc_ref[0:nc, :] = jnp.where(crow < cpt * (t + 1), raw, NEG)
        if isinstance(t, int):
            b0 = max(cpt * t - BAND_BELOW, 0)
            sc_ref[b0:b0 + band_rows, :] += band_ref[min(t, 1)]
        else:
            band = pl.ds(pl.multiple_of(jnp.maximum(cpt * t - BAND_BELOW, 0), SUBLANE), band_rows)
            sc_ref[band, :] += band_ref[jnp.minimum(t, 1)]
        yield

        if n_win > 1:
            kt1, ex1 = win_tile(1)
            mt_w1 = win.qk(qs, kw_ref, kt1, TAB_SUB, 1, ex1)
        win.process(vwt_ref, t, 0, mt_w0)
        yield

        sc = sc_ref[0:nc, :]
        mc = jnp.maximum(jnp.max(sc, axis=0, keepdims=True), HALF_NEG)
        pc = jnp.exp2(sc - mc)
        lc = jnp.sum(pc, axis=0, keepdims=True)
        pc = pc * jnp.where(lc > 0.0, 1.0 / lc, 0.0)
        oc_ref[par] = jnp.dot(vct_ref[:, 0:nc], pc.astype(BF16), preferred_element_type=F32)
        yield

        if n_win > 2:
            kt2, ex2 = win_tile(nw)
            mt_w2 = win.qk(qs, kw_ref, kt2, TAB_FAR, 0, ex2)
        if n_win > 1:
            win.process(vwt_ref, kt1, 1, mt_w1)
        yield

        ps = pc[:, 0:tq] + pc[:, tq:2 * tq] + pc[:, 2 * tq:3 * tq] + pc[:, 3 * tq:4 * tq]
        hi = ps.astype(BF16)
        r1 = ps - hi.astype(F32)
        mid = r1.astype(BF16)
        lo = (r1 - mid.astype(F32)).astype(BF16)
        ovl = ovl_ref[0:nb16, 0:nc]
        imp = (jnp.dot(ovl, hi, preferred_element_type=F32) + jnp.dot(ovl, mid, preferred_element_type=F32)
               + jnp.dot(ovl, lo, preferred_element_type=F32))
        jj = lax.broadcasted_iota(jnp.int32, (nb16, tq), 0)
        tt = t * tq + lax.broadcasted_iota(jnp.int32, (nb16, tq), 1)
        cur = tt >> int(math.log2(SEL_LEN))
        forced = (jj == 0) | (jj == cur) | (jj == cur - 1)
        imp = jnp.where(forced, FORCE, imp)
        imp = jnp.where(jj * SEL_LEN <= tt, imp, NEG)
        yield

        sub = SUBLANE
        ranks = []
        for j0 in range(0, nb8, sub):
            blk = imp[j0:j0 + sub, :]
            jl = j0 + lax.broadcasted_iota(jnp.int32, blk.shape, 0)
            cnt = jnp.zeros(blk.shape, jnp.int32)
            for b in range(n_valid):
                row = imp[b:b + 1, :]
                if b < j0:
                    cnt = cnt + jnp.where(row >= blk, 1, 0)
                elif b >= j0 + sub:
                    cnt = cnt + jnp.where(row > blk, 1, 0)
                else:
                    cnt = cnt + jnp.where(row > blk, 1, jnp.where(row == blk, jnp.where(jl > b, 1, 0), 0))
            ranks.append(cnt)
            yield
        rank = jnp.concatenate(ranks, axis=0)
        selb_t = jnp.where(rank < SEL_TOPK, 0.0, NEG)

        if n_win > 2:
            win.process(vwt_ref, kt2, 0, mt_w2)
        ow_ref[par] = win.finish()

        selb = jnp.concatenate([selb_t, jnp.zeros((LANE - nb8, tq), F32)], axis=0).T
        selb = selb.astype(BF16)
        qa_ref[par, :, 0:HEAD_DIM] = qs
        qa_ref[par, :, HEAD_DIM:2 * HEAD_DIM] = jnp.concatenate([selb] * HPG, axis=0)
        yield

    def selected(t, t_lo, t_hi):
        par = t % 2
        tiles = []
        for d in range(t_hi + 1):
            kind = (TAB_DIAG, TAB_SUB)[d] if d < 2 else None
            tiles.append((t - d, kind, None) if t_lo >= d else (jnp.maximum(t - d, 0), kind, t >= d))
        sel.init()
        mt = sel.qk(qa_ref[par], kaug_ref, tiles[0][0], tiles[0][1], 0)
        for n, (kt, _, _) in enumerate(tiles):
            nxt = None
            if n + 1 < len(tiles):
                kt_n, kind_n, ex_n = tiles[n + 1]
                nxt = sel.qk(qa_ref[par], kaug_ref, kt_n, kind_n, (n + 1) % 2, ex_n)
            sel.process(vst_ref, kt, n % 2, mt)
            mt = nxt
            yield
        o_s = sel.finish()
        gt = gate_ref[...].T
        for h in range(HPG):
            sl = slice(h * tq, (h + 1) * tq)
            o = gt[3 * h:3 * h + 1, :] * oc_ref[par, :, sl]
            o = o + gt[3 * h + 1:3 * h + 2, :] * o_s[:, sl]
            o = o + gt[3 * h + 2:3 * h + 3, :] * ow_ref[par, :, sl]
            o_ref[:, h * HEAD_DIM:(h + 1) * HEAD_DIM] = o.T.astype(BF16)
        yield

    if i_lo == 0:
        @pl.when(i == 0)
        def _():
            for _ in front(0, 0, 0, q_ref, 0):
                pass
    t_hi = min(i_hi + 1, n_tiles - 1)
    t_next = jnp.minimum(i + 1, n_tiles - 1)
    _spread(selected(i, i_lo, i_hi), front(t_next, i_lo + 1, t_hi, qn_ref, (i + 1) % 2),
            i_hi + 2, front_pieces(t_hi))


def _nsa(q, k_all, vt_all, cmp, cmp_t, gates, fvec, ovl, e_t, cast_weights, batch, seq_len, *, tq):
    n_t = seq_len // tq
    n_steps = batch * N_KV * n_t
    ts = tq
    step_row = lambda b, g, i: ((b * N_KV + g) * n_t + i, 0)
    cast_specs = [pl.BlockSpec((w.shape[0] // n_steps, w.shape[1]), step_row) for w in cast_weights]
    n_sel = seq_len // SEL_LEN
    n_chunks = cmp.shape[2]
    mcols = HPG * tq
    band_rows = BAND_BELOW + tq // CMP_STRIDE
    qw = HPG * HEAD_DIM

    def vt_spec(first):
        return pl.BlockSpec((None, None, HEAD_DIM + ONES_ROWS, seq_len), lambda b, g, i: (b, first + g, 0, 0))

    kern = functools.partial(_nsa_kernel, n_tiles=n_t, tq=tq, n_sel=n_sel, n_cast=len(cast_weights))
    o_attn, *cast = pl.pallas_call(
        kern,
        grid=(batch, N_KV, n_t),
        in_specs=[
            pl.BlockSpec((ts, qw), lambda b, g, i: (b * n_t + i, g)),
            pl.BlockSpec((ts, qw), lambda b, g, i: (b * n_t + jnp.minimum(i + 1, n_t - 1), g)),
            pl.BlockSpec((None, None, n_chunks, HEAD_DIM), lambda b, g, i: (b, g, 0, 0)),
            pl.BlockSpec((None, None, HEAD_DIM, n_chunks), lambda b, g, i: (b, N_KV + g, 0, 0)),
            pl.BlockSpec((seq_len, HEAD_DIM), lambda b, g, i: (b, g)),
            vt_spec(0),
            pl.BlockSpec((seq_len, HEAD_DIM), lambda b, g, i: (b, N_KV + g)),
            vt_spec(N_KV),
            pl.BlockSpec((ts, LANE), lambda b, g, i: (b * n_t + i, g)),
            pl.BlockSpec((None, HPG, 1, BIAS_LEN), lambda b, g, i: (g, 0, 0, 0)),
            pl.BlockSpec((n_sel, n_chunks), lambda b, g, i: (0, 0)),
            pl.BlockSpec((seq_len, LANE), lambda b, g, i: (0, 0)),
        ] + cast_specs,
        out_specs=[pl.BlockSpec((ts, qw), lambda b, g, i: (b * n_t + i, g))] + cast_specs,
        out_shape=[jax.ShapeDtypeStruct((batch * seq_len, ATTN_WIDTH), BF16)]
        + [jax.ShapeDtypeStruct(w.shape, BF16) for w in cast_weights],
        scratch_shapes=[
            pltpu.VMEM((seq_len, 2 * HEAD_DIM), BF16),
            pltpu.VMEM((3, tq, mcols), F32),
            pltpu.VMEM((2, band_rows, mcols), F32),
            pltpu.VMEM((n_chunks, mcols), F32),
            pltpu.VMEM((tq, mcols), F32),
            pltpu.VMEM((tq, mcols), F32),
            pltpu.VMEM((tq, mcols), F32),
            pltpu.VMEM((tq, mcols), F32),
            pltpu.VMEM((1, mcols), F32),
            pltpu.VMEM((1, mcols), F32),
            pltpu.VMEM((HEAD_DIM + ONES_ROWS, mcols), F32),
            pltpu.VMEM((HEAD_DIM + ONES_ROWS, mcols), F32),
            pltpu.VMEM((2, mcols, 2 * HEAD_DIM), BF16),
            pltpu.VMEM((2, HEAD_DIM, mcols), F32),
            pltpu.VMEM((2, HEAD_DIM, mcols), F32),
        ],
        compiler_params=_params(3),
        name="nsa_attention",
    )(q, q, cmp, cmp_t, k_all, vt_all, k_all, vt_all, gates, fvec, ovl, e_t, *cast_weights)
    return o_attn, cast


OPROJ_CHUNKS = 4


def _oproj_kernel(oa_ref, ov_ref, wo_ref, x_ref, g1_ref, g2_ref, x1_ref, h2_ref):
    ka = oa_ref.shape[1]
    tm = oa_ref.shape[0]
    chunk = tm // OPROJ_CHUNKS
    for rows in (slice(k * chunk, (k + 1) * chunk) for k in range(OPROJ_CHUNKS)):
        mix = jnp.dot(oa_ref[rows, :], wo_ref[0:ka, :], preferred_element_type=F32)
        mix = mix + jnp.dot(ov_ref[rows, :], wo_ref[ka:, :], preferred_element_type=F32)
        x1 = x_ref[rows, :] + _rms(mix, g1_ref[...])
        x1_ref[rows, :] = x1
        h2_ref[rows, :] = _rms(x1, g2_ref[...]).astype(BF16)


def _oproj(o_attn, o_conv, w_o, x2, g_post, g_pre, *, tm=512):
    m = x2.shape[0]
    ka, kv = o_attn.shape[1], o_conv.shape[1]
    row = lambda i: (i, 0)
    fixed = lambda i: (0, 0)
    return pl.pallas_call(
        _oproj_kernel,
        grid=(m // tm,),
        in_specs=[
            pl.BlockSpec((tm, ka), row),
            pl.BlockSpec((tm, kv), row),
            pl.BlockSpec((ka + kv, D_MODEL), fixed),
            pl.BlockSpec((tm, D_MODEL), row),
            pl.BlockSpec((1, D_MODEL), fixed),
            pl.BlockSpec((1, D_MODEL), fixed),
        ],
        out_specs=[pl.BlockSpec((tm, D_MODEL), row), pl.BlockSpec((tm, D_MODEL), row)],
        out_shape=[jax.ShapeDtypeStruct((m, D_MODEL), F32), jax.ShapeDtypeStruct((m, D_MODEL), BF16)],
        compiler_params=_params(1),
        name="out_proj",
    )(o_attn, o_conv, w_o, x2, g_post, g_pre)


FFN_SPLIT = 2


def _ffn_kernel(h_ref, wu_ref, wd_ref, x1_hbm, g_ref, o_ref, x1_ref, sem):
    i = pl.program_id(0)
    j = pl.program_id(1)
    last = pl.num_programs(1) - 1
    tm = h_ref.shape[0]
    width = wu_ref.shape[1] // FFN_SPLIT
    x1_copy = pltpu.make_async_copy(x1_hbm.at[pl.ds(pl.multiple_of(i * tm, tm), tm), :], x1_ref, sem)

    def step(mode):
        if mode == "first":
            x1_copy.start()
        for k in range(FFN_SPLIT):
            cols = slice(k * width, (k + 1) * width)
            a = jnp.dot(h_ref[...], wu_ref[:, cols], preferred_element_type=F32)
            a = jnp.square(jnp.maximum(a, 0.0)).astype(BF16)
            if mode == "first" and k == 0:
                o_ref[...] = jnp.dot(a, wd_ref[cols, :], preferred_element_type=F32)
            elif mode != "last" or k < FFN_SPLIT - 1:
                o_ref[...] += jnp.dot(a, wd_ref[cols, :], preferred_element_type=F32)
            else:
                x1_copy.wait()
                for rows in (slice(0, tm // 2), slice(tm // 2, tm)):
                    f = o_ref[rows, :] + jnp.dot(a[rows, :], wd_ref[cols, :], preferred_element_type=F32)
                    o_ref[rows, :] = x1_ref[rows, :] + _rms(f, g_ref[...])

    pl.when(j == 0)(functools.partial(step, "first"))
    pl.when((j > 0) & (j < last))(functools.partial(step, "middle"))
    pl.when(j == last)(functools.partial(step, "last"))


def _ffn(h2, w_up, w_down, x1, g_post, *, tm=512, tf=2048):
    m = h2.shape[0]
    d_ff = w_up.shape[1]
    assert d_ff // tf >= 2
    return pl.pallas_call(
        _ffn_kernel,
        grid=(m // tm, d_ff // tf),
        in_specs=[
            pl.BlockSpec((tm, D_MODEL), lambda i, j: (i, 0)),
            pl.BlockSpec((D_MODEL, tf), lambda i, j: (0, j)),
            pl.BlockSpec((tf, D_MODEL), lambda i, j: (j, 0)),
            pl.BlockSpec(memory_space=pl.ANY),
            pl.BlockSpec((1, D_MODEL), lambda i, j: (0, 0)),
        ],
        out_specs=pl.BlockSpec((tm, D_MODEL), lambda i, j: (i, 0)),
        out_shape=jax.ShapeDtypeStruct((m, D_MODEL), F32),
        scratch_shapes=[pltpu.VMEM((tm, D_MODEL), F32), pltpu.SemaphoreType.DMA(())],
        compiler_params=_params(2),
        name="ffn",
    )(h2, w_up, w_down, x1, g_post)


def _bucket_np(dist):
    n = np.maximum(dist, 0)
    max_exact = N_BUCKETS // 2
    nf = np.maximum(n, 1).astype(np.float32)
    large = max_exact + (np.log(nf / np.float32(max_exact)) / np.float32(math.log(MAX_DIST / max_exact))
                         * np.float32(N_BUCKETS - max_exact)).astype(np.int32)
    large = np.minimum(large, N_BUCKETS - 1)
    return np.where(n < max_exact, n, large)


def _bucket_starts():
    b = _bucket_np(np.arange(4 * MAX_DIST))
    return [int(np.argmax(b == k)) for k in range(N_BUCKETS)]


def _attention_tables(rel_bias, seq_len, tq):
    starts = _bucket_starts()
    assert starts[N_BUCKETS - 1] <= CMP_STRIDE * (BAND_BELOW + 1) - (CMP_LEN - 1)
    assert starts[N_BUCKETS - 1] <= tq and 2 * tq <= BIAS_LEN // 2
    rel = (rel_bias - rel_bias[:, N_BUCKETS - 1:]) * math.log2(math.e)
    d = jnp.arange(BIAS_LEN, dtype=jnp.int32)[None, :]
    fvec = jnp.broadcast_to(rel[:, 0:1], (N_HEADS, BIAS_LEN))
    for k in range(1, N_BUCKETS):
        fvec = jnp.where(d >= starts[k], rel[:, k:k + 1], fvec)
    fvec = jnp.where(d < BIAS_LEN // 2, fvec, NEG).astype(F32).reshape(N_KV, HPG, 1, BIAS_LEN)

    n_chunks = seq_len // CMP_STRIDE
    n_cmp = (seq_len - CMP_LEN) // CMP_STRIDE + 1
    n_sel = seq_len // SEL_LEN
    ci = np.arange(n_chunks)[None, :] * CMP_STRIDE
    sj = np.arange(n_sel)[:, None] * SEL_LEN
    ovl = ((ci < sj + SEL_LEN) & (ci + CMP_LEN > sj) & (np.arange(n_chunks)[None, :] < n_cmp))
    e_t = (np.arange(seq_len)[:, None] // SEL_LEN == np.arange(LANE)[None, :])
    return fvec, jnp.asarray(ovl, BF16), jnp.asarray(e_t, BF16)


def kernel(x, w_in, pe_cmp, w_cmp_k1, w_cmp_k2, w_cmp_v1, w_cmp_v2, conv_w, rel_bias, w_o, w_up, w_down,
           g_pre_mix, g_post_mix, g_pre_ffn, g_post_ffn):
    batch, seq_len, _ = x.shape
    depth = w_in.shape[0]
    tq = 256
    fvec, ovl, e_t = _attention_tables(rel_bias, seq_len, tq)
    x2 = x.reshape(batch * seq_len, D_MODEL)
    for l in range(depth):
        wl = jnp.swapaxes(w_in[l], 0, 1)
        g1 = g_pre_mix[l].reshape(1, D_MODEL)

        q, k_all, c_in, vt_all, gates, o_conv = _in_proj(x2, g1, wl, conv_w[l], batch, seq_len)

        cmp, cmp_t = _compress(c_in, pe_cmp[l], (w_cmp_k1[l], w_cmp_v1[l]), (w_cmp_k2[l], w_cmp_v2[l]),
                               batch, seq_len)

        o_attn, (wo_b, wup_b, wdown_b) = _nsa(q, k_all, vt_all, cmp, cmp_t, gates, fvec, ovl, e_t,
                                              [w_o[l], w_up[l], w_down[l]], batch, seq_len, tq=tq)

        x1, h2 = _oproj(o_attn, o_conv, wo_b, x2,
                        g_post_mix[l].reshape(1, D_MODEL), g_pre_ffn[l].reshape(1, D_MODEL))
        x2 = _ffn(h2, wup_b, wdown_b, x1, g_post_ffn[l].reshape(1, D_MODEL))
    return x2.reshape(batch, seq_len, D_MODEL)
```

```python
import functools
import math

import numpy as np
import jax
import jax.numpy as jnp
from jax import lax
from jax.experimental import pallas as pl
from jax.experimental.pallas import tpu as pltpu

F32 = jnp.float32
BF16 = jnp.bfloat16

D_MODEL = 2048
N_HEADS = 8
N_KV = 2
HPG = N_HEADS // N_KV
HEAD_DIM = 128
ATTN_WIDTH = N_HEADS * HEAD_DIM
KV_WIDTH = N_KV * HEAD_DIM
CONV_WIDTH = D_MODEL - ATTN_WIDTH
CONV_K = 3
N_BRANCH = 3
CMP_LEN = 32
CMP_STRIDE = 16
CMP_HIDDEN = 256
SEL_LEN = 64
SEL_TOPK = 16
WINDOW = 512
N_BUCKETS = 32
MAX_DIST = 128
EPS = 1e-6
NEG = -1e30
HALF_NEG = -5e29
FORCE = 1e9

QKV_WIDTH = ATTN_WIDTH + 6 * KV_WIDTH
GATE_OFF = QKV_WIDTH
CONV_OFF = QKV_WIDTH + N_HEADS * N_BRANCH
LANE = 128
VMEM_LIMIT = 56 * 1024 * 1024

_DN_T = (((1,), (1,)), ((), ()))


def _rms(x, g):
    ms = jnp.mean(x * x, axis=-1, keepdims=True)
    return x * lax.rsqrt(ms + EPS) * g


def _params(n_axes):
    return pltpu.CompilerParams(dimension_semantics=("arbitrary",) * n_axes, vmem_limit_bytes=VMEM_LIMIT)


QKV_BLOCK = 2 * KV_WIDTH
SLAB_KC, SLAB_VC, SLAB_KS, SLAB_VS, SLAB_KW, SLAB_VW = range(6)
CONV_BLOCK = 512
W_CHUNK = 256
SUBLANE = 8


def _load_weights(wt_hbm, w_ref, wg_ref, wconv_ref, stage_ref, sem):
    n_gate = N_HEADS * N_BRANCH
    per_group = HPG * N_BRANCH

    def store_rows(dst, row):
        def store(v):
            dst[row:row + v.shape[0], :] = v.astype(BF16)
        return store

    def store_gates(v):
        wg_ref[...] = jnp.zeros(wg_ref.shape, BF16)
        pad = jnp.zeros((2 * SUBLANE - per_group, v.shape[1]), F32)
        for g in range(N_KV):
            rows = jnp.concatenate([v[g * per_group:(g + 1) * per_group, :], pad], axis=0)
            wg_ref[g * LANE:g * LANE + 2 * SUBLANE, :] = rows.astype(BF16)

    chunks = [(r, W_CHUNK, store_rows(w_ref, r)) for r in range(0, QKV_WIDTH, W_CHUNK)]
    chunks.append((GATE_OFF, n_gate, store_gates))
    chunks += [(CONV_OFF + r, W_CHUNK, store_rows(wconv_ref, r)) for r in range(0, 3 * CONV_WIDTH, W_CHUNK)]

    def copy(k):
        src, n, _ = chunks[k]
        slot = k % 2
        return pltpu.make_async_copy(wt_hbm.at[pl.ds(src, n), :], stage_ref.at[slot, pl.ds(0, n), :], sem.at[slot])

    copy(0).start()
    for k, (_, n, store) in enumerate(chunks):
        if k + 1 < len(chunks):
            copy(k + 1).start()
        copy(k).wait()
        store(stage_ref[k % 2, 0:n, :])


def _inproj_kernel(x_ref, g_ref, wt_hbm, cw_ref, q_ref, k_ref, c_ref, vt_ref, gate_ref, ov_ref,
                   h_ref, carry_ref, w_ref, wg_ref, wconv_ref, stage_ref, sem, *, q_scale, tiles_per_seq):
    i = pl.program_id(0)
    tm = x_ref.shape[0]

    @pl.when(i == 0)
    def _():
        _load_weights(wt_hbm, w_ref, wg_ref, wconv_ref, stage_ref, sem)

    h_ref[...] = _rms(x_ref[...], g_ref[...]).astype(BF16)
    gate_ref[...] = jax.nn.sigmoid(lax.dot_general(h_ref[...], wg_ref[...], _DN_T, preferred_element_type=F32))

    def proj(w, j, width):
        return lax.dot_general(h_ref[...], w[j * width:(j + 1) * width, :], _DN_T, preferred_element_type=F32)

    for j in range(ATTN_WIDTH // QKV_BLOCK):
        q_ref[:, j * QKV_BLOCK:(j + 1) * QKV_BLOCK] = (proj(w_ref, j, QKV_BLOCK) * q_scale).astype(BF16)

    def slab(which):
        return proj(w_ref, ATTN_WIDTH // KV_WIDTH + which, KV_WIDTH)

    ones = jnp.where(lax.broadcasted_iota(jnp.int32, (ONES_ROWS, tm), 0) == 0, 1.0, 0.0).astype(BF16)
    for pair, (k_slab, c_slab, v_slab) in enumerate(((SLAB_KS, SLAB_KC, SLAB_VS), (SLAB_KW, SLAB_VC, SLAB_VW))):
        cols = slice(pair * KV_WIDTH, (pair + 1) * KV_WIDTH)
        k_ref[:, cols] = slab(k_slab).astype(BF16)
        c_ref[:, cols] = slab(c_slab)
        v = slab(v_slab)
        for g in range(N_KV):
            s = pair * N_KV + g
            vt_ref[s, 0:HEAD_DIM, :] = v[:, g * HEAD_DIM:(g + 1) * HEAD_DIM].T.astype(BF16)
            vt_ref[s, HEAD_DIM:HEAD_DIM + ONES_ROWS, :] = ones

    def conv_proj(which, j):
        return proj(wconv_ref, which * (CONV_WIDTH // CONV_BLOCK) + j, CONV_BLOCK)

    row = lax.broadcasted_iota(jnp.int32, (tm, CONV_BLOCK), 0)
    for j in range(CONV_WIDTH // CONV_BLOCK):
        cols = slice(j * CONV_BLOCK, (j + 1) * CONV_BLOCK)
        u = conv_proj(2, j) * conv_proj(0, j)
        prev = carry_ref[j]
        prev = jnp.where(i % tiles_per_seq == 0, 0.0, prev)
        carry_ref[j] = u[tm - SUBLANE:tm, :]
        p1 = prev[SUBLANE - 1:SUBLANE, :]
        p2 = prev[SUBLANE - 2:SUBLANE - 1, :]
        u1 = jnp.where(row == 0, p1, pltpu.roll(u, 1, axis=0))
        u2 = jnp.where(row == 0, p2, jnp.where(row == 1, p1, pltpu.roll(u, 2, axis=0)))
        w = cw_ref[:, cols]
        y = w[0:1, :] * u2
        y = y + w[1:2, :] * u1
        y = y + w[2:3, :] * u
        ov_ref[:, cols] = (conv_proj(1, j) * y).astype(BF16)


def _in_proj(x2, g, w_t, conv_w, batch, seq_len, *, tm=512):
    m = x2.shape[0]
    gw = N_KV * LANE
    tps = seq_len // tm
    n_slabs = QKV_BLOCK // HEAD_DIM
    kern = functools.partial(_inproj_kernel, q_scale=HEAD_DIM ** -0.5 * math.log2(math.e), tiles_per_seq=tps)
    row = lambda i: (i, 0)

    def resident(shape):
        return pl.BlockSpec(shape, lambda i: (0, 0), pipeline_mode=pl.Buffered(1))

    return pl.pallas_call(
        kern,
        grid=(m // tm,),
        in_specs=[
            pl.BlockSpec((tm, D_MODEL), row),
            resident((1, D_MODEL)),
            pl.BlockSpec(memory_space=pl.ANY),
            resident(conv_w.shape),
        ],
        out_specs=[
            pl.BlockSpec((tm, ATTN_WIDTH), row),
            pl.BlockSpec((tm, QKV_BLOCK), row),
            pl.BlockSpec((tm, QKV_BLOCK), row),
            pl.BlockSpec((None, n_slabs, HEAD_DIM + ONES_ROWS, tm), lambda i: (i // tps, 0, 0, i % tps)),
            pl.BlockSpec((tm, gw), row),
            pl.BlockSpec((tm, CONV_WIDTH), row),
        ],
        out_shape=[
            jax.ShapeDtypeStruct((m, ATTN_WIDTH), BF16),
            jax.ShapeDtypeStruct((m, QKV_BLOCK), BF16),
            jax.ShapeDtypeStruct((m, QKV_BLOCK), F32),
            jax.ShapeDtypeStruct((batch, n_slabs, HEAD_DIM + ONES_ROWS, seq_len), BF16),
            jax.ShapeDtypeStruct((m, gw), F32),
            jax.ShapeDtypeStruct((m, CONV_WIDTH), BF16),
        ],
        scratch_shapes=[pltpu.VMEM((tm, D_MODEL), BF16),
                        pltpu.VMEM((CONV_WIDTH // CONV_BLOCK, SUBLANE, CONV_BLOCK), F32),
                        pltpu.VMEM((QKV_WIDTH, D_MODEL), BF16),
                        pltpu.VMEM((gw, D_MODEL), BF16),
                        pltpu.VMEM((3 * CONV_WIDTH, D_MODEL), BF16),
                        pltpu.VMEM((2, W_CHUNK, D_MODEL), F32),
                        pltpu.SemaphoreType.DMA((2,))],
        compiler_params=_params(1),
        name="in_proj",
    )(x2, g, w_t, conv_w)


def _compress_kernel(x_ref, pe_ref, w1k_ref, w1v_ref, w2k_ref, w2v_ref, o_ref, ot_ref):
    n = x_ref.shape[0] // CMP_STRIDE

    def body(w1_ref, w2_ref):
        a = jnp.zeros((n, CMP_HIDDEN), F32)
        b = jnp.zeros((n, CMP_HIDDEN), F32)
        for l in range(CMP_STRIDE):
            xl = x_ref[pl.ds(l, n, stride=CMP_STRIDE), :]
            xa = (xl + pe_ref[l:l + 1, :]).astype(BF16)
            xb = (xl + pe_ref[CMP_STRIDE + l:CMP_STRIDE + l + 1, :]).astype(BF16)
            a = a + jnp.dot(xa, w1_ref[l].astype(BF16), preferred_element_type=F32)
            b = b + jnp.dot(xb, w1_ref[CMP_STRIDE + l].astype(BF16), preferred_element_type=F32)
        pre = a + pltpu.roll(b, n - 1, axis=0)
        hid = pre * jax.nn.sigmoid(pre)
        out = jnp.dot(hid.astype(BF16), w2_ref[...].astype(BF16), preferred_element_type=F32)
        row = lax.broadcasted_iota(jnp.int32, out.shape, 0)
        out = jnp.where(row < n - 1, out, 0.0)
        o_ref[...] = out.astype(BF16)
        ot_ref[...] = out.T.astype(BF16)

    is_value = pl.program_id(1) >= N_KV
    pl.when(jnp.logical_not(is_value))(functools.partial(body, w1k_ref, w2k_ref))
    pl.when(is_value)(functools.partial(body, w1v_ref, w2v_ref))


def _compress(c_in, pe, w1_kv, w2_kv, batch, seq_len):
    n_slabs = c_in.shape[1] // HEAD_DIM
    n_chunks = seq_len // CMP_STRIDE

    def resident(shape):
        return pl.BlockSpec(shape, lambda i, j: (0,) * len(shape), pipeline_mode=pl.Buffered(1))

    return pl.pallas_call(
        _compress_kernel,
        grid=(batch, n_slabs),
        in_specs=[
            pl.BlockSpec((seq_len, HEAD_DIM), lambda i, j: (i, j)),
            resident((CMP_LEN, HEAD_DIM)),
            resident(w1_kv[0].shape), resident(w1_kv[1].shape),
            resident(w2_kv[0].shape), resident(w2_kv[1].shape),
        ],
        out_specs=[
            pl.BlockSpec((None, None, n_chunks, HEAD_DIM), lambda i, j: (i, j, 0, 0)),
            pl.BlockSpec((None, None, HEAD_DIM, n_chunks), lambda i, j: (i, j, 0, 0)),
        ],
        out_shape=[
            jax.ShapeDtypeStruct((batch, n_slabs, n_chunks, HEAD_DIM), BF16),
            jax.ShapeDtypeStruct((batch, n_slabs, HEAD_DIM, n_chunks), BF16),
        ],
        compiler_params=_params(2),
        name="compress",
    )(c_in, pe, *w1_kv, *w2_kv)


TAB_DIAG, TAB_SUB, TAB_FAR = 0, 1, 2
BAND_BELOW = 8
ONES_ROWS = 16


N_NSA_INPUTS = 12
BIAS_LEN = 1024
_END = object()


NSA_CLASSES = 5


def _nsa_kernel(*refs, n_tiles, **static):
    i = pl.program_id(2)
    n_cls = min(NSA_CLASSES, n_tiles)
    lo = 0
    for c in range(n_cls):
        hi = lo + n_tiles // n_cls + (1 if c >= n_cls - n_tiles % n_cls else 0) - 1
        pl.when((i >= lo) & (i <= hi))(functools.partial(_nsa_step, refs, lo, hi, n_tiles, **static))
        lo = hi + 1


def _spread(main, side, n_main, n_side):
    done = 0
    for m, _ in enumerate(main, 1):
        want = (m * n_side) // n_main
        while done < want and next(side, _END) is not _END:
            done += 1
    for _ in side:
        pass


def _nsa_step(refs, i_lo, i_hi, n_tiles, *, tq, n_sel, n_cast):
    i = pl.program_id(2)
    (q_ref, qn_ref, kc_ref, vct_ref, ks_ref, vst_ref, kw_ref, vwt_ref, gate_ref, fvec_ref,
     ovl_ref, et_ref) = refs[:N_NSA_INPUTS]
    cast_in = refs[N_NSA_INPUTS:N_NSA_INPUTS + n_cast]
    o_ref = refs[N_NSA_INPUTS + n_cast]
    cast_out = refs[N_NSA_INPUTS + n_cast + 1:N_NSA_INPUTS + 2 * n_cast + 1]
    (kaug_ref, tab_ref, band_ref, sc_ref, s0_ref, s1_ref, sw0_ref, sw1_ref, m_ref, mw_ref,
     acc_ref, accw_ref, qa_ref, oc_ref, ow_ref) = refs[N_NSA_INPUTS + 2 * n_cast + 1:]

    tk = tq
    mcols = HPG * tq
    nw = WINDOW // tk
    cpt = tq // CMP_STRIDE
    band_rows = BAND_BELOW + cpt

    def group_start(fn):
        if i_lo == 0:
            pl.when(i == 0)(fn)

    @group_start
    def _():
        kaug_ref[:, 0:HEAD_DIM] = ks_ref[...]
        kaug_ref[:, HEAD_DIM:2 * HEAD_DIM] = et_ref[...]

    @group_start
    def _():
        c = lax.broadcasted_iota(jnp.int32, (tk, tq), 0)
        r = lax.broadcasted_iota(jnp.int32, (tk, tq), 1)
        far = jnp.where(r < c, 0.0, NEG)
        lane = lax.broadcasted_iota(jnp.int32, (1, BIAS_LEN), 1)
        for h in range(HPG):
            cols = slice(h * tq, (h + 1) * tq)
            f = fvec_ref[h]
            f_diag = jnp.where(lane < tq, f, NEG)
            x = pltpu.roll(jnp.broadcast_to(f_diag, (tk, BIAS_LEN)), 0, 1, stride=1, stride_axis=0)
            tab_ref[TAB_DIAG, :, cols] = x[:, 0:tq]
            x = pltpu.roll(jnp.broadcast_to(f, (tk, BIAS_LEN)), 0, 1, stride=1, stride_axis=0)
            tab_ref[TAB_SUB, :, cols] = x[:, tq:2 * tq]
            tab_ref[TAB_FAR, :, cols] = far
            for v, first in enumerate((0, -BAND_BELOW)):
                shift = (CMP_STRIDE * first + CMP_LEN - 1) % BIAS_LEN
                f_shift = pltpu.roll(f, shift, 1)
                x = pltpu.roll(jnp.broadcast_to(f_shift, (band_rows, BIAS_LEN)), 0, 1,
                               stride=CMP_STRIDE, stride_axis=0)
                band_ref[v, :, cols] = x[:, 0:tq]

    for src, dst in zip(cast_in, cast_out):
        dst[...] = src[...].astype(BF16)

    def keys(kt):
        if isinstance(kt, int):
            return slice(kt * tk, (kt + 1) * tk)
        return pl.ds(pl.multiple_of(kt * tk, tk), tk)

    class Flash:
        def __init__(self, s_refs, m_ref, acc_ref):
            self.s_refs, self.m_ref, self.acc_ref = s_refs, m_ref, acc_ref

        def init(self):
            self.m_ref[...] = jnp.full((1, mcols), NEG, F32)
            self.acc_ref[...] = jnp.zeros(self.acc_ref.shape, F32)

        def qk(self, qmat, k_ref, kt, kind, buf, exists=None):
            s = lax.dot_general(k_ref[keys(kt), :], qmat, _DN_T, preferred_element_type=F32)
            if kind is not None:
                s = s + tab_ref[kind]
            if exists is not None:
                s = s + jnp.where(exists, 0.0, NEG)
            self.s_refs[buf][...] = s
            return jnp.max(s, axis=0, keepdims=True)

        def process(self, vt_ref, kt, buf, m_tile):
            m_prev = self.m_ref[...]
            m_next = jnp.maximum(m_prev, m_tile)
            alpha = jnp.exp2(m_prev - m_next)
            p = jnp.exp2(self.s_refs[buf][...] - m_next).astype(BF16)
            self.acc_ref[...] = alpha * self.acc_ref[...] + jnp.dot(
                vt_ref[:, keys(kt)], p, preferred_element_type=F32)
            self.m_ref[...] = m_next

        def finish(self):
            return self.acc_ref[0:HEAD_DIM, :] * (1.0 / self.acc_ref[HEAD_DIM:HEAD_DIM + 1, :])

    win = Flash((sw0_ref, sw1_ref), mw_ref, accw_ref)
    sel = Flash((s0_ref, s1_ref), m_ref, acc_ref)

    assert nw == 2, "window tiles are t, t-1 (previous-tile table) and t-nw (window-edge table)"

    def front_pieces(t_hi):
        n_valid = (t_hi + 1) * tq // SEL_LEN
        return 6 + -(-n_valid // SUBLANE)

    def front(t, t_lo, t_hi, qsrc_ref, par):
        n_valid = (t_hi + 1) * tq // SEL_LEN
        nb8 = -(-n_valid // SUBLANE) * SUBLANE
        nb16 = -(-n_valid // 16) * 16
        nc = -(-(cpt * (t_hi + 1)) // LANE) * LANE
        n_win = min(t_hi, nw) + 1
        q = qsrc_ref[...]
        qs = jnp.concatenate([q[:, h * HEAD_DIM:(h + 1) * HEAD_DIM] for h in range(HPG)], axis=0)

        def win_tile(d):
            if t_lo >= d:
                return t - d, None
            return jnp.maximum(t - d, 0), t >= d

        win.init()
        mt_w0 = win.qk(qs, kw_ref, t, TAB_DIAG, 0)
        raw = lax.dot_general(kc_ref[0:nc, :], qs, _DN_T, preferred_element_type=F32)
        crow = lax.broadcasted_iota(jnp.int32, raw.shape, 0)
        sc_ref[0:nc, :] = jnp.where(crow < cpt * (t + 1), raw, NEG)
        if isinstance(t, int):
            b0 = max(cpt * t - BAND_BELOW, 0)
            sc_ref[b0:b0 + band_rows, :] += band_ref[min(t, 1)]
        else:
            band = pl.ds(pl.multiple_of(jnp.maximum(cpt * t - BAND_BELOW, 0), SUBLANE), band_rows)
            sc_ref[band, :] += band_ref[jnp.minimum(t, 1)]
        yield

        if n_win > 1:
            kt1, ex1 = win_tile(1)
            mt_w1 = win.qk(qs, kw_ref, kt1, TAB_SUB, 1, ex1)
        win.process(vwt_ref, t, 0, mt_w0)
        yield

        sc = sc_ref[0:nc, :]
        mc = jnp.maximum(jnp.max(sc, axis=0, keepdims=True), HALF_NEG)
        pc = jnp.exp2(sc - mc)
        lc = jnp.sum(pc, axis=0, keepdims=True)
        pc = pc * jnp.where(lc > 0.0, 1.0 / lc, 0.0)
        oc_ref[par] = jnp.dot(vct_ref[:, 0:nc], pc.astype(BF16), preferred_element_type=F32)
        yield

        if n_win > 2:
            kt2, ex2 = win_tile(nw)
            mt_w2 = win.qk(qs, kw_ref, kt2, TAB_FAR, 0, ex2)
        if n_win > 1:
            win.process(vwt_ref, kt1, 1, mt_w1)
        yield

        ps = pc[:, 0:tq] + pc[:, tq:2 * tq] + pc[:, 2 * tq:3 * tq] + pc[:, 3 * tq:4 * tq]
        hi = ps.astype(BF16)
        r1 = ps - hi.astype(F32)
        mid = r1.astype(BF16)
        lo = (r1 - mid.astype(F32)).astype(BF16)
        ovl = ovl_ref[0:nb16, 0:nc]
        imp = (jnp.dot(ovl, hi, preferred_element_type=F32) + jnp.dot(ovl, mid, preferred_element_type=F32)
               + jnp.dot(ovl, lo, preferred_element_type=F32))
        jj = lax.broadcasted_iota(jnp.int32, (nb16, tq), 0)
        tt = t * tq + lax.broadcasted_iota(jnp.int32, (nb16, tq), 1)
        cur = tt >> int(math.log2(SEL_LEN))
        forced = (jj == 0) | (jj == cur) | (jj == cur - 1)
        imp = jnp.where(forced, FORCE, imp)
        imp = jnp.where(jj * SEL_LEN <= tt, imp, NEG)
        yield

        sub = SUBLANE
        ranks = []
        for j0 in range(0, nb8, sub):
            blk = imp[j0:j0 + sub, :]
            jl = j0 + lax.broadcasted_iota(jnp.int32, blk.shape, 0)
            cnt = jnp.zeros(blk.shape, jnp.int32)
            for b in range(n_valid):
                row = imp[b:b + 1, :]
                if b < j0:
                    cnt = cnt + jnp.where(row >= blk, 1, 0)
                elif b >= j0 + sub:
                    cnt = cnt + jnp.where(row > blk, 1, 0)
                else:
                    cnt = cnt + jnp.where(row > blk, 1, jnp.where(row == blk, jnp.where(jl > b, 1, 0), 0))
            ranks.append(cnt)
            yield
        rank = jnp.concatenate(ranks, axis=0)
        selb_t = jnp.where(rank < SEL_TOPK, 0.0, NEG)

        if n_win > 2:
            win.process(vwt_ref, kt2, 0, mt_w2)
        ow_ref[par] = win.finish()

        selb = jnp.concatenate([selb_t, jnp.zeros((LANE - nb8, tq), F32)], axis=0).T
        selb = selb.astype(BF16)
        qa_ref[par, :, 0:HEAD_DIM] = qs
        qa_ref[par, :, HEAD_DIM:2 * HEAD_DIM] = jnp.concatenate([selb] * HPG, axis=0)
        yield

    def selected(t, t_lo, t_hi):
        par = t % 2
        tiles = []
        for d in range(t_hi + 1):
            kind = (TAB_DIAG, TAB_SUB)[d] if d < 2 else None
            tiles.append((t - d, kind, None) if t_lo >= d else (jnp.maximum(t - d, 0), kind, t >= d))
        sel.init()
        mt = sel.qk(qa_ref[par], kaug_ref, tiles[0][0], tiles[0][1], 0)
        for n, (kt, _, _) in enumerate(tiles):
            nxt = None
            if n + 1 < len(tiles):
                kt_n, kind_n, ex_n = tiles[n + 1]
                nxt = sel.qk(qa_ref[par], kaug_ref, kt_n, kind_n, (n + 1) % 2, ex_n)
            sel.process(vst_ref, kt, n % 2, mt)
            mt = nxt
            yield
        o_s = sel.finish()
        gt = gate_ref[...].T
        for h in range(HPG):
            sl = slice(h * tq, (h + 1) * tq)
            o = gt[3 * h:3 * h + 1, :] * oc_ref[par, :, sl]
            o = o + gt[3 * h + 1:3 * h + 2, :] * o_s[:, sl]
            o = o + gt[3 * h + 2:3 * h + 3, :] * ow_ref[par, :, sl]
            o_ref[:, h * HEAD_DIM:(h + 1) * HEAD_DIM] = o.T.astype(BF16)
        yield

    if i_lo == 0:
        @pl.when(i == 0)
        def _():
            for _ in front(0, 0, 0, q_ref, 0):
                pass
    t_hi = min(i_hi + 1, n_tiles - 1)
    t_next = jnp.minimum(i + 1, n_tiles - 1)
    _spread(selected(i, i_lo, i_hi), front(t_next, i_lo + 1, t_hi, qn_ref, (i + 1) % 2),
            i_hi + 2, front_pieces(t_hi))


def _nsa(q, k_all, vt_all, cmp, cmp_t, gates, fvec, ovl, e_t, cast_weights, batch, seq_len, *, tq):
    n_t = seq_len // tq
    n_steps = batch * N_KV * n_t
    ts = tq
    step_row = lambda b, g, i: ((b * N_KV + g) * n_t + i, 0)
    cast_specs = [pl.BlockSpec((w.shape[0] // n_steps, w.shape[1]), step_row) for w in cast_weights]
    n_sel = seq_len // SEL_LEN
    n_chunks = cmp.shape[2]
    mcols = HPG * tq
    band_rows = BAND_BELOW + tq // CMP_STRIDE
    qw = HPG * HEAD_DIM

    def vt_spec(first):
        return pl.BlockSpec((None, None, HEAD_DIM + ONES_ROWS, seq_len), lambda b, g, i: (b, first + g, 0, 0))

    kern = functools.partial(_nsa_kernel, n_tiles=n_t, tq=tq, n_sel=n_sel, n_cast=len(cast_weights))
    o_attn, *cast = pl.pallas_call(
        kern,
        grid=(batch, N_KV, n_t),
        in_specs=[
            pl.BlockSpec((ts, qw), lambda b, g, i: (b * n_t + i, g)),
            pl.BlockSpec((ts, qw), lambda b, g, i: (b * n_t + jnp.minimum(i + 1, n_t - 1), g)),
            pl.BlockSpec((None, None, n_chunks, HEAD_DIM), lambda b, g, i: (b, g, 0, 0)),
            pl.BlockSpec((None, None, HEAD_DIM, n_chunks), lambda b, g, i: (b, N_KV + g, 0, 0)),
            pl.BlockSpec((seq_len, HEAD_DIM), lambda b, g, i: (b, g)),
            vt_spec(0),
            pl.BlockSpec((seq_len, HEAD_DIM), lambda b, g, i: (b, N_KV + g)),
            vt_spec(N_KV),
            pl.BlockSpec((ts, LANE), lambda b, g, i: (b * n_t + i, g)),
            pl.BlockSpec((None, HPG, 1, BIAS_LEN), lambda b, g, i: (g, 0, 0, 0)),
            pl.BlockSpec((n_sel, n_chunks), lambda b, g, i: (0, 0)),
            pl.BlockSpec((seq_len, LANE), lambda b, g, i: (0, 0)),
        ] + cast_specs,
        out_specs=[pl.BlockSpec((ts, qw), lambda b, g, i: (b * n_t + i, g))] + cast_specs,
        out_shape=[jax.ShapeDtypeStruct((batch * seq_len, ATTN_WIDTH), BF16)]
        + [jax.ShapeDtypeStruct(w.shape, BF16) for w in cast_weights],
        scratch_shapes=[
            pltpu.VMEM((seq_len, 2 * HEAD_DIM), BF16),
            pltpu.VMEM((3, tq, mcols), F32),
            pltpu.VMEM((2, band_rows, mcols), F32),
            pltpu.VMEM((n_chunks, mcols), F32),
            pltpu.VMEM((tq, mcols), F32),
            pltpu.VMEM((tq, mcols), F32),
            pltpu.VMEM((tq, mcols), F32),
            pltpu.VMEM((tq, mcols), F32),
            pltpu.VMEM((1, mcols), F32),
            pltpu.VMEM((1, mcols), F32),
            pltpu.VMEM((HEAD_DIM + ONES_ROWS, mcols), F32),
            pltpu.VMEM((HEAD_DIM + ONES_ROWS, mcols), F32),
            pltpu.VMEM((2, mcols, 2 * HEAD_DIM), BF16),
            pltpu.VMEM((2, HEAD_DIM, mcols), F32),
            pltpu.VMEM((2, HEAD_DIM, mcols), F32),
        ],
        compiler_params=_params(3),
        name="nsa_attention",
    )(q, q, cmp, cmp_t, k_all, vt_all, k_all, vt_all, gates, fvec, ovl, e_t, *cast_weights)
    return o_attn, cast


OPROJ_CHUNKS = 4


def _oproj_kernel(oa_ref, ov_ref, wo_ref, x_ref, g1_ref, g2_ref, x1_ref, h2_ref):
    ka = oa_ref.shape[1]
    tm = oa_ref.shape[0]
    chunk = tm // OPROJ_CHUNKS
    for rows in (slice(k * chunk, (k + 1) * chunk) for k in range(OPROJ_CHUNKS)):
        mix = jnp.dot(oa_ref[rows, :], wo_ref[0:ka, :], preferred_element_type=F32)
        mix = mix + jnp.dot(ov_ref[rows, :], wo_ref[ka:, :], preferred_element_type=F32)
        x1 = x_ref[rows, :] + _rms(mix, g1_ref[...])
        x1_ref[rows, :] = x1
        h2_ref[rows, :] = _rms(x1, g2_ref[...]).astype(BF16)


def _oproj(o_attn, o_conv, w_o, x2, g_post, g_pre, *, tm=512):
    m = x2.shape[0]
    ka, kv = o_attn.shape[1], o_conv.shape[1]
    row = lambda i: (i, 0)
    fixed = lambda i: (0, 0)
    return pl.pallas_call(
        _oproj_kernel,
        grid=(m // tm,),
        in_specs=[
            pl.BlockSpec((tm, ka), row),
            pl.BlockSpec((tm, kv), row),
            pl.BlockSpec((ka + kv, D_MODEL), fixed),
            pl.BlockSpec((tm, D_MODEL), row),
            pl.BlockSpec((1, D_MODEL), fixed),
            pl.BlockSpec((1, D_MODEL), fixed),
        ],
        out_specs=[pl.BlockSpec((tm, D_MODEL), row), pl.BlockSpec((tm, D_MODEL), row)],
        out_shape=[jax.ShapeDtypeStruct((m, D_MODEL), F32), jax.ShapeDtypeStruct((m, D_MODEL), BF16)],
        compiler_params=_params(1),
        name="out_proj",
    )(o_attn, o_conv, w_o, x2, g_post, g_pre)


FFN_SPLIT = 2


def _ffn_kernel(h_ref, wu_ref, wd_ref, x1_hbm, g_ref, o_ref, x1_ref, sem):
    i = pl.program_id(0)
    j = pl.program_id(1)
    last = pl.num_programs(1) - 1
    tm = h_ref.shape[0]
    width = wu_ref.shape[1] // FFN_SPLIT
    x1_copy = pltpu.make_async_copy(x1_hbm.at[pl.ds(pl.multiple_of(i * tm, tm), tm), :], x1_ref, sem)

    def step(mode):
        if mode == "first":
            x1_copy.start()
        for k in range(FFN_SPLIT):
            cols = slice(k * width, (k + 1) * width)
            a = jnp.dot(h_ref[...], wu_ref[:, cols], preferred_element_type=F32)
            a = jnp.square(jnp.maximum(a, 0.0)).astype(BF16)
            if mode == "first" and k == 0:
                o_ref[...] = jnp.dot(a, wd_ref[cols, :], preferred_element_type=F32)
            elif mode != "last" or k < FFN_SPLIT - 1:
                o_ref[...] += jnp.dot(a, wd_ref[cols, :], preferred_element_type=F32)
            else:
                x1_copy.wait()
                for rows in (slice(0, tm // 2), slice(tm // 2, tm)):
                    f = o_ref[rows, :] + jnp.dot(a[rows, :], wd_ref[cols, :], preferred_element_type=F32)
                    o_ref[rows, :] = x1_ref[rows, :] + _rms(f, g_ref[...])

    pl.when(j == 0)(functools.partial(step, "first"))
    pl.when((j > 0) & (j < last))(functools.partial(step, "middle"))
    pl.when(j == last)(functools.partial(step, "last"))


def _ffn(h2, w_up, w_down, x1, g_post, *, tm=512, tf=2048):
    m = h2.shape[0]
    d_ff = w_up.shape[1]
    assert d_ff // tf >= 2
    return pl.pallas_call(
        _ffn_kernel,
        grid=(m // tm, d_ff // tf),
        in_specs=[
            pl.BlockSpec((tm, D_MODEL), lambda i, j: (i, 0)),
            pl.BlockSpec((D_MODEL, tf), lambda i, j: (0, j)),
            pl.BlockSpec((tf, D_MODEL), lambda i, j: (j, 0)),
            pl.BlockSpec(memory_space=pl.ANY),
            pl.BlockSpec((1, D_MODEL), lambda i, j: (0, 0)),
        ],
        out_specs=pl.BlockSpec((tm, D_MODEL), lambda i, j: (i, 0)),
        out_shape=jax.ShapeDtypeStruct((m, D_MODEL), F32),
        scratch_shapes=[pltpu.VMEM((tm, D_MODEL), F32), pltpu.SemaphoreType.DMA(())],
        compiler_params=_params(2),
        name="ffn",
    )(h2, w_up, w_down, x1, g_post)


def _bucket_np(dist):
    n = np.maximum(dist, 0)
    max_exact = N_BUCKETS // 2
    nf = np.maximum(n, 1).astype(np.float32)
    large = max_exact + (np.log(nf / np.float32(max_exact)) / np.float32(math.log(MAX_DIST / max_exact))
                         * np.float32(N_BUCKETS - max_exact)).astype(np.int32)
    large = np.minimum(large, N_BUCKETS - 1)
    return np.where(n < max_exact, n, large)


def _bucket_starts():
    b = _bucket_np(np.arange(4 * MAX_DIST))
    return [int(np.argmax(b == k)) for k in range(N_BUCKETS)]


def _attention_tables(rel_bias, seq_len, tq):
    starts = _bucket_starts()
    assert starts[N_BUCKETS - 1] <= CMP_STRIDE * (BAND_BELOW + 1) - (CMP_LEN - 1)
    assert starts[N_BUCKETS - 1] <= tq and 2 * tq <= BIAS_LEN // 2
    rel = (rel_bias - rel_bias[:, N_BUCKETS - 1:]) * math.log2(math.e)
    d = jnp.arange(BIAS_LEN, dtype=jnp.int32)[None, :]
    fvec = jnp.broadcast_to(rel[:, 0:1], (N_HEADS, BIAS_LEN))
    for k in range(1, N_BUCKETS):
        fvec = jnp.where(d >= starts[k], rel[:, k:k + 1], fvec)
    fvec = jnp.where(d < BIAS_LEN // 2, fvec, NEG).astype(F32).reshape(N_KV, HPG, 1, BIAS_LEN)

    n_chunks = seq_len // CMP_STRIDE
    n_cmp = (seq_len - CMP_LEN) // CMP_STRIDE + 1
    n_sel = seq_len // SEL_LEN
    ci = np.arange(n_chunks)[None, :] * CMP_STRIDE
    sj = np.arange(n_sel)[:, None] * SEL_LEN
    ovl = ((ci < sj + SEL_LEN) & (ci + CMP_LEN > sj) & (np.arange(n_chunks)[None, :] < n_cmp))
    e_t = (np.arange(seq_len)[:, None] // SEL_LEN == np.arange(LANE)[None, :])
    return fvec, jnp.asarray(ovl, BF16), jnp.asarray(e_t, BF16)


def kernel(x, w_in, pe_cmp, w_cmp_k1, w_cmp_k2, w_cmp_v1, w_cmp_v2, conv_w, rel_bias, w_o, w_up, w_down,
           g_pre_mix, g_post_mix, g_pre_ffn, g_post_ffn):
    batch, seq_len, _ = x.shape
    depth = w_in.shape[0]
    tq = 256
    fvec, ovl, e_t = _attention_tables(rel_bias, seq_len, tq)
    x2 = x.reshape(batch * seq_len, D_MODEL)
    for l in range(depth):
        wl = jnp.swapaxes(w_in[l], 0, 1)
        g1 = g_pre_mix[l].reshape(1, D_MODEL)

        q, k_all, c_in, vt_all, gates, o_conv = _in_proj(x2, g1, wl, conv_w[l], batch, seq_len)

        cmp, cmp_t = _compress(c_in, pe_cmp[l], (w_cmp_k1[l], w_cmp_v1[l]), (w_cmp_k2[l], w_cmp_v2[l]),
                               batch, seq_len)

        o_attn, (wo_b, wup_b, wdown_b) = _nsa(q, k_all, vt_all, cmp, cmp_t, gates, fvec, ovl, e_t,
                                              [w_o[l], w_up[l], w_down[l]], batch, seq_len, tq=tq)

        x1, h2 = _oproj(o_attn, o_conv, wo_b, x2,
                        g_post_mix[l].reshape(1, D_MODEL), g_pre_ffn[l].reshape(1, D_MODEL))
        x2 = _ffn(h2, wup_b, wdown_b, x1, g_post_ffn[l].reshape(1, D_MODEL))
    return x2.reshape(batch, seq_len, D_MODEL)
```

```python
import functools
import math

import numpy as np
import jax
import jax.numpy as jnp
from jax import lax
from jax.experimental import pallas as pl
from jax.experimental.pallas import tpu as pltpu

F32 = jnp.float32
BF16 = jnp.bfloat16

D_MODEL = 2048
N_HEADS = 8
N_KV = 2
HPG = N_HEADS // N_KV
HEAD_DIM = 128
ATTN_WIDTH = N_HEADS * HEAD_DIM
KV_WIDTH = N_KV * HEAD_DIM
CONV_WIDTH = D_MODEL - ATTN_WIDTH
CONV_K = 3
N_BRANCH = 3
CMP_LEN = 32
CMP_STRIDE = 16
CMP_HIDDEN = 256
SEL_LEN = 64
SEL_TOPK = 16
WINDOW = 512
N_BUCKETS = 32
MAX_DIST = 128
EPS = 1e-6
NEG = -1e30
HALF_NEG = -5e29
FORCE = 1e9

QKV_WIDTH = ATTN_WIDTH + 6 * KV_WIDTH
GATE_OFF = QKV_WIDTH
CONV_OFF = QKV_WIDTH + N_HEADS * N_BRANCH
LANE = 128
VMEM_LIMIT = 56 * 1024 * 1024

_DN_T = (((1,), (1,)), ((), ()))


def _rms(x, g):
    ms = jnp.mean(x * x, axis=-1, keepdims=True)
    return x * lax.rsqrt(ms + EPS) * g


def _params(n_axes):
    return pltpu.CompilerParams(dimension_semantics=("arbitrary",) * n_axes, vmem_limit_bytes=VMEM_LIMIT)


QKV_BLOCK = 2 * KV_WIDTH
SLAB_KC, SLAB_VC, SLAB_KS, SLAB_VS, SLAB_KW, SLAB_VW = range(6)
CONV_BLOCK = 512
W_CHUNK = 256
SUBLANE = 8


def _load_weights(wt_hbm, w_ref, wg_ref, wconv_ref, stage_ref, sem):
    n_gate = N_HEADS * N_BRANCH
    per_group = HPG * N_BRANCH

    def store_rows(dst, row):
        def store(v):
            dst[row:row + v.shape[0], :] = v.astype(BF16)
        return store

    def store_gates(v):
        wg_ref[...] = jnp.zeros(wg_ref.shape, BF16)
        pad = jnp.zeros((2 * SUBLANE - per_group, v.shape[1]), F32)
        for g in range(N_KV):
            rows = jnp.concatenate([v[g * per_group:(g + 1) * per_group, :], pad], axis=0)
            wg_ref[g * LANE:g * LANE + 2 * SUBLANE, :] = rows.astype(BF16)

    chunks = [(r, W_CHUNK, store_rows(w_ref, r)) for r in range(0, QKV_WIDTH, W_CHUNK)]
    chunks.append((GATE_OFF, n_gate, store_gates))
    chunks += [(CONV_OFF + r, W_CHUNK, store_rows(wconv_ref, r)) for r in range(0, 3 * CONV_WIDTH, W_CHUNK)]

    def copy(k):
        src, n, _ = chunks[k]
        slot = k % 2
        return pltpu.make_async_copy(wt_hbm.at[pl.ds(src, n), :], stage_ref.at[slot, pl.ds(0, n), :], sem.at[slot])

    copy(0).start()
    for k, (_, n, store) in enumerate(chunks):
        if k + 1 < len(chunks):
            copy(k + 1).start()
        copy(k).wait()
        store(stage_ref[k % 2, 0:n, :])


def _inproj_kernel(x_ref, g_ref, wt_hbm, cw_ref, q_ref, k_ref, c_ref, vt_ref, gate_ref, ov_ref,
                   h_ref, carry_ref, w_ref, wg_ref, wconv_ref, stage_ref, sem, *, q_scale, tiles_per_seq):
    i = pl.program_id(0)
    tm = x_ref.shape[0]

    @pl.when(i == 0)
    def _():
        _load_weights(wt_hbm, w_ref, wg_ref, wconv_ref, stage_ref, sem)

    h_ref[...] = _rms(x_ref[...], g_ref[...]).astype(BF16)
    gate_ref[...] = jax.nn.sigmoid(lax.dot_general(h_ref[...], wg_ref[...], _DN_T, preferred_element_type=F32))

    def proj(w, j, width):
        return lax.dot_general(h_ref[...], w[j * width:(j + 1) * width, :], _DN_T, preferred_element_type=F32)

    for j in range(ATTN_WIDTH // QKV_BLOCK):
        q_ref[:, j * QKV_BLOCK:(j + 1) * QKV_BLOCK] = (proj(w_ref, j, QKV_BLOCK) * q_scale).astype(BF16)

    def slab(which):
        return proj(w_ref, ATTN_WIDTH // KV_WIDTH + which, KV_WIDTH)

    ones = jnp.where(lax.broadcasted_iota(jnp.int32, (ONES_ROWS, tm), 0) == 0, 1.0, 0.0).astype(BF16)
    for pair, (k_slab, c_slab, v_slab) in enumerate(((SLAB_KS, SLAB_KC, SLAB_VS), (SLAB_KW, SLAB_VC, SLAB_VW))):
        cols = slice(pair * KV_WIDTH, (pair + 1) * KV_WIDTH)
        k_ref[:, cols] = slab(k_slab).astype(BF16)
        c_ref[:, cols] = slab(c_slab)
        v = slab(v_slab)
        for g in range(N_KV):
            s = pair * N_KV + g
            vt_ref[s, 0:HEAD_DIM, :] = v[:, g * HEAD_DIM:(g + 1) * HEAD_DIM].T.astype(BF16)
            vt_ref[s, HEAD_DIM:HEAD_DIM + ONES_ROWS, :] = ones

    def conv_proj(which, j):
        return proj(wconv_ref, which * (CONV_WIDTH // CONV_BLOCK) + j, CONV_BLOCK)

    row = lax.broadcasted_iota(jnp.int32, (tm, CONV_BLOCK), 0)
    for j in range(CONV_WIDTH // CONV_BLOCK):
        cols = slice(j * CONV_BLOCK, (j + 1) * CONV_BLOCK)
        u = conv_proj(2, j) * conv_proj(0, j)
        prev = carry_ref[j]
        prev = jnp.where(i % tiles_per_seq == 0, 0.0, prev)
        carry_ref[j] = u[tm - SUBLANE:tm, :]
        p1 = prev[SUBLANE - 1:SUBLANE, :]
        p2 = prev[SUBLANE - 2:SUBLANE - 1, :]
        u1 = jnp.where(row == 0, p1, pltpu.roll(u, 1, axis=0))
        u2 = jnp.where(row == 0, p2, jnp.where(row == 1, p1, pltpu.roll(u, 2, axis=0)))
        w = cw_ref[:, cols]
        y = w[0:1, :] * u2
        y = y + w[1:2, :] * u1
        y = y + w[2:3, :] * u
        ov_ref[:, cols] = (conv_proj(1, j) * y).astype(BF16)


def _in_proj(x2, g, w_t, conv_w, batch, seq_len, *, tm=512):
    m = x2.shape[0]
    gw = N_KV * LANE
    tps = seq_len // tm
    n_slabs = QKV_BLOCK // HEAD_DIM
    kern = functools.partial(_inproj_kernel, q_scale=HEAD_DIM ** -0.5 * math.log2(math.e), tiles_per_seq=tps)
    row = lambda i: (i, 0)

    def resident(shape):
        return pl.BlockSpec(shape, lambda i: (0, 0), pipeline_mode=pl.Buffered(1))

    return pl.pallas_call(
        kern,
        grid=(m // tm,),
        in_specs=[
            pl.BlockSpec((tm, D_MODEL), row),
            resident((1, D_MODEL)),
            pl.BlockSpec(memory_space=pl.ANY),
            resident(conv_w.shape),
        ],
        out_specs=[
            pl.BlockSpec((tm, ATTN_WIDTH), row),
            pl.BlockSpec((tm, QKV_BLOCK), row),
            pl.BlockSpec((tm, QKV_BLOCK), row),
            pl.BlockSpec((None, n_slabs, HEAD_DIM + ONES_ROWS, tm), lambda i: (i // tps, 0, 0, i % tps)),
            pl.BlockSpec((tm, gw), row),
            pl.BlockSpec((tm, CONV_WIDTH), row),
        ],
        out_shape=[
            jax.ShapeDtypeStruct((m, ATTN_WIDTH), BF16),
            jax.ShapeDtypeStruct((m, QKV_BLOCK), BF16),
            jax.ShapeDtypeStruct((m, QKV_BLOCK), F32),
            jax.ShapeDtypeStruct((batch, n_slabs, HEAD_DIM + ONES_ROWS, seq_len), BF16),
            jax.ShapeDtypeStruct((m, gw), F32),
            jax.ShapeDtypeStruct((m, CONV_WIDTH), BF16),
        ],
        scratch_shapes=[pltpu.VMEM((tm, D_MODEL), BF16),
                        pltpu.VMEM((CONV_WIDTH // CONV_BLOCK, SUBLANE, CONV_BLOCK), F32),
                        pltpu.VMEM((QKV_WIDTH, D_MODEL), BF16),
                        pltpu.VMEM((gw, D_MODEL), BF16),
                        pltpu.VMEM((3 * CONV_WIDTH, D_MODEL), BF16),
                        pltpu.VMEM((2, W_CHUNK, D_MODEL), F32),
                        pltpu.SemaphoreType.DMA((2,))],
        compiler_params=_params(1),
        name="in_proj",
    )(x2, g, w_t, conv_w)


def _compress_kernel(x_ref, pe_ref, w1k_ref, w1v_ref, w2k_ref, w2v_ref, o_ref, ot_ref):
    n = x_ref.shape[0] // CMP_STRIDE

    def body(w1_ref, w2_ref):
        a = jnp.zeros((n, CMP_HIDDEN), F32)
        b = jnp.zeros((n, CMP_HIDDEN), F32)
        for l in range(CMP_STRIDE):
            xl = x_ref[pl.ds(l, n, stride=CMP_STRIDE), :]
            xa = (xl + pe_ref[l:l + 1, :]).astype(BF16)
            xb = (xl + pe_ref[CMP_STRIDE + l:CMP_STRIDE + l + 1, :]).astype(BF16)
            a = a + jnp.dot(xa, w1_ref[l].astype(BF16), preferred_element_type=F32)
            b = b + jnp.dot(xb, w1_ref[CMP_STRIDE + l].astype(BF16), preferred_element_type=F32)
        pre = a + pltpu.roll(b, n - 1, axis=0)
        hid = pre * jax.nn.sigmoid(pre)
        out = jnp.dot(hid.astype(BF16), w2_ref[...].astype(BF16), preferred_element_type=F32)
        row = lax.broadcasted_iota(jnp.int32, out.shape, 0)
        out = jnp.where(row < n - 1, out, 0.0)
        o_ref[...] = out.astype(BF16)
        ot_ref[...] = out.T.astype(BF16)

    is_value = pl.program_id(1) >= N_KV
    pl.when(jnp.logical_not(is_value))(functools.partial(body, w1k_ref, w2k_ref))
    pl.when(is_value)(functools.partial(body, w1v_ref, w2v_ref))


def _compress(c_in, pe, w1_kv, w2_kv, batch, seq_len):
    n_slabs = c_in.shape[1] // HEAD_DIM
    n_chunks = seq_len // CMP_STRIDE

    def resident(shape):
        return pl.BlockSpec(shape, lambda i, j: (0,) * len(shape), pipeline_mode=pl.Buffered(1))

    return pl.pallas_call(
        _compress_kernel,
        grid=(batch, n_slabs),
        in_specs=[
            pl.BlockSpec((seq_len, HEAD_DIM), lambda i, j: (i, j)),
            resident((CMP_LEN, HEAD_DIM)),
            resident(w1_kv[0].shape), resident(w1_kv[1].shape),
            resident(w2_kv[0].shape), resident(w2_kv[1].shape),
        ],
        out_specs=[
            pl.BlockSpec((None, None, n_chunks, HEAD_DIM), lambda i, j: (i, j, 0, 0)),
            pl.BlockSpec((None, None, HEAD_DIM, n_chunks), lambda i, j: (i, j, 0, 0)),
        ],
        out_shape=[
            jax.ShapeDtypeStruct((batch, n_slabs, n_chunks, HEAD_DIM), BF16),
            jax.ShapeDtypeStruct((batch, n_slabs, HEAD_DIM, n_chunks), BF16),
        ],
        compiler_params=_params(2),
        name="compress",
    )(c_in, pe, *w1_kv, *w2_kv)


TAB_DIAG, TAB_SUB, TAB_FAR = 0, 1, 2
BAND_BELOW = 8
ONES_ROWS = 16


N_NSA_INPUTS = 12
BIAS_LEN = 1024
_END = object()


NSA_CLASSES = 6


def _nsa_kernel(*refs, n_tiles, **static):
    i = pl.program_id(2)
    n_cls = min(NSA_CLASSES, n_tiles)
    lo = 0
    for c in range(n_cls):
        hi = lo + n_tiles // n_cls + (1 if c >= n_cls - n_tiles % n_cls else 0) - 1
        pl.when((i >= lo) & (i <= hi))(functools.partial(_nsa_step, refs, lo, hi, n_tiles, **static))
        lo = hi + 1


def _spread(main, side, n_main, n_side):
    done = 0
    for m, _ in enumerate(main, 1):
        want = (m * n_side) // n_main
        while done < want and next(side, _END) is not _END:
            done += 1
    for _ in side:
        pass


def _nsa_step(refs, i_lo, i_hi, n_tiles, *, tq, n_sel, n_cast):
    i = pl.program_id(2)
    (q_ref, qn_ref, kc_ref, vct_ref, ks_ref, vst_ref, kw_ref, vwt_ref, gate_ref, fvec_ref,
     ovl_ref, et_ref) = refs[:N_NSA_INPUTS]
    cast_in = refs[N_NSA_INPUTS:N_NSA_INPUTS + n_cast]
    o_ref = refs[N_NSA_INPUTS + n_cast]
    cast_out = refs[N_NSA_INPUTS + n_cast + 1:N_NSA_INPUTS + 2 * n_cast + 1]
    (kaug_ref, tab_ref, band_ref, sc_ref, s0_ref, s1_ref, sw0_ref, sw1_ref, m_ref, mw_ref,
     acc_ref, accw_ref, qa_ref, oc_ref, ow_ref) = refs[N_NSA_INPUTS + 2 * n_cast + 1:]

    tk = tq
    mcols = HPG * tq
    nw = WINDOW // tk
    cpt = tq // CMP_STRIDE
    band_rows = BAND_BELOW + cpt

    def group_start(fn):
        if i_lo == 0:
            pl.when(i == 0)(fn)

    @group_start
    def _():
        kaug_ref[:, 0:HEAD_DIM] = ks_ref[...]
        kaug_ref[:, HEAD_DIM:2 * HEAD_DIM] = et_ref[...]

    @group_start
    def _():
        c = lax.broadcasted_iota(jnp.int32, (tk, tq), 0)
        r = lax.broadcasted_iota(jnp.int32, (tk, tq), 1)
        far = jnp.where(r < c, 0.0, NEG)
        lane = lax.broadcasted_iota(jnp.int32, (1, BIAS_LEN), 1)
        for h in range(HPG):
            cols = slice(h * tq, (h + 1) * tq)
            f = fvec_ref[h]
            f_diag = jnp.where(lane < tq, f, NEG)
            x = pltpu.roll(jnp.broadcast_to(f_diag, (tk, BIAS_LEN)), 0, 1, stride=1, stride_axis=0)
            tab_ref[TAB_DIAG, :, cols] = x[:, 0:tq]
            x = pltpu.roll(jnp.broadcast_to(f, (tk, BIAS_LEN)), 0, 1, stride=1, stride_axis=0)
            tab_ref[TAB_SUB, :, cols] = x[:, tq:2 * tq]
            tab_ref[TAB_FAR, :, cols] = far
            for v, first in enumerate((0, -BAND_BELOW)):
                shift = (CMP_STRIDE * first + CMP_LEN - 1) % BIAS_LEN
                f_shift = pltpu.roll(f, shift, 1)
                x = pltpu.roll(jnp.broadcast_to(f_shift, (band_rows, BIAS_LEN)), 0, 1,
                               stride=CMP_STRIDE, stride_axis=0)
                band_ref[v, :, cols] = x[:, 0:tq]

    for src, dst in zip(cast_in, cast_out):
        dst[...] = src[...].astype(BF16)

    def keys(kt):
        if isinstance(kt, int):
            return slice(kt * tk, (kt + 1) * tk)
        return pl.ds(pl.multiple_of(kt * tk, tk), tk)

    class Flash:
        def __init__(self, s_refs, m_ref, acc_ref):
            self.s_refs, self.m_ref, self.acc_ref = s_refs, m_ref, acc_ref

        def init(self):
            self.m_ref[...] = jnp.full((1, mcols), NEG, F32)
            self.acc_ref[...] = jnp.zeros(self.acc_ref.shape, F32)

        def qk(self, qmat, k_ref, kt, kind, buf, exists=None):
            s = lax.dot_general(k_ref[keys(kt), :], qmat, _DN_T, preferred_element_type=F32)
            if kind is not None:
                s = s + tab_ref[kind]
            if exists is not None:
                s = s + jnp.where(exists, 0.0, NEG)
            self.s_refs[buf][...] = s
            return jnp.max(s, axis=0, keepdims=True)

        def process(self, vt_ref, kt, buf, m_tile):
            m_prev = self.m_ref[...]
            m_next = jnp.maximum(m_prev, m_tile)
            alpha = jnp.exp2(m_prev - m_next)
            p = jnp.exp2(self.s_refs[buf][...] - m_next).astype(BF16)
            self.acc_ref[...] = alpha * self.acc_ref[...] + jnp.dot(
                vt_ref[:, keys(kt)], p, preferred_element_type=F32)
            self.m_ref[...] = m_next

        def finish(self):
            return self.acc_ref[0:HEAD_DIM, :] * (1.0 / self.acc_ref[HEAD_DIM:HEAD_DIM + 1, :])

    win = Flash((sw0_ref, sw1_ref), mw_ref, accw_ref)
    sel = Flash((s0_ref, s1_ref), m_ref, acc_ref)

    assert nw == 2, "window tiles are t, t-1 (previous-tile table) and t-nw (window-edge table)"

    def front_pieces(t_hi):
        n_valid = (t_hi + 1) * tq // SEL_LEN
        return 6 + -(-n_valid // SUBLANE)

    def front(t, t_lo, t_hi, qsrc_ref, par):
        n_valid = (t_hi + 1) * tq // SEL_LEN
        nb8 = -(-n_valid // SUBLANE) * SUBLANE
        nb16 = -(-n_valid // 16) * 16
        nc = -(-(cpt * (t_hi + 1)) // LANE) * LANE
        n_win = min(t_hi, nw) + 1
        q = qsrc_ref[...]
        qs = jnp.concatenate([q[:, h * HEAD_DIM:(h + 1) * HEAD_DIM] for h in range(HPG)], axis=0)

        def win_tile(d):
            if t_lo >= d:
                return t - d, None
            return jnp.maximum(t - d, 0), t >= d

        win.init()
        mt_w0 = win.qk(qs, kw_ref, t, TAB_DIAG, 0)
        raw = lax.dot_general(kc_ref[0:nc, :], qs, _DN_T, preferred_element_type=F32)
        crow = lax.broadcasted_iota(jnp.int32, raw.shape, 0)
        sc_ref[0:nc, :] = jnp.where(crow < cpt * (t + 1), raw, NEG)
        if isinstance(t, int):
            b0 = max(cpt * t - BAND_BELOW, 0)
            sc_ref[b0:b0 + band_rows, :] += band_ref[min(t, 1)]
        else:
            band = pl.ds(pl.multiple_of(jnp.maximum(cpt * t - BAND_BELOW, 0), SUBLANE), band_rows)
            sc_ref[band, :] += band_ref[jnp.minimum(t, 1)]
        yield

        if n_win > 1:
            kt1, ex1 = win_tile(1)
            mt_w1 = win.qk(qs, kw_ref, kt1, TAB_SUB, 1, ex1)
        win.process(vwt_ref, t, 0, mt_w0)
        yield

        sc = sc_ref[0:nc, :]
        mc = jnp.maximum(jnp.max(sc, axis=0, keepdims=True), HALF_NEG)
        pc = jnp.exp2(sc - mc)
        lc = jnp.sum(pc, axis=0, keepdims=True)
        pc = pc * jnp.where(lc > 0.0, 1.0 / lc, 0.0)
        oc_ref[par] = jnp.dot(vct_ref[:, 0:nc], pc.astype(BF16), preferred_element_type=F32)
        yield

        if n_win > 2:
            kt2, ex2 = win_tile(nw)
            mt_w2 = win.qk(qs, kw_ref, kt2, TAB_FAR, 0, ex2)
        if n_win > 1:
            win.process(vwt_ref, kt1, 1, mt_w1)
        yield

        ps = pc[:, 0:tq] + pc[:, tq:2 * tq] + pc[:, 2 * tq:3 * tq] + pc[:, 3 * tq:4 * tq]
        hi = ps.astype(BF16)
        r1 = ps - hi.astype(F32)
        mid = r1.astype(BF16)
        lo = (r1 - mid.astype(F32)).astype(BF16)
        ovl = ovl_ref[0:nb16, 0:nc]
        imp = (jnp.dot(ovl, hi, preferred_element_type=F32) + jnp.dot(ovl, mid, preferred_element_type=F32)
               + jnp.dot(ovl, lo, preferred_element_type=F32))
        jj = lax.broadcasted_iota(jnp.int32, (nb16, tq), 0)
        tt = t * tq + lax.broadcasted_iota(jnp.int32, (nb16, tq), 1)
        cur = tt >> int(math.log2(SEL_LEN))
        forced = (jj == 0) | (jj == cur) | (jj == cur - 1)
        imp = jnp.where(forced, FORCE, imp)
        imp = jnp.where(jj * SEL_LEN <= tt, imp, NEG)
        yield

        sub = SUBLANE
        ranks = []
        for j0 in range(0, nb8, sub):
            blk = imp[j0:j0 + sub, :]
            jl = j0 + lax.broadcasted_iota(jnp.int32, blk.shape, 0)
            cnt = jnp.zeros(blk.shape, jnp.int32)
            for b in range(n_valid):
                row = imp[b:b + 1, :]
                if b < j0:
                    cnt = cnt + jnp.where(row >= blk, 1, 0)
                elif b >= j0 + sub:
                    cnt = cnt + jnp.where(row > blk, 1, 0)
                else:
                    cnt = cnt + jnp.where(row > blk, 1, jnp.where(row == blk, jnp.where(jl > b, 1, 0), 0))
            ranks.append(cnt)
            yield
        rank = jnp.concatenate(ranks, axis=0)
        selb_t = jnp.where(rank < SEL_TOPK, 0.0, NEG)

        if n_win > 2:
            win.process(vwt_ref, kt2, 0, mt_w2)
        ow_ref[par] = win.finish()

        selb = jnp.concatenate([selb_t, jnp.zeros((LANE - nb8, tq), F32)], axis=0).T
        selb = selb.astype(BF16)
        qa_ref[par, :, 0:HEAD_DIM] = qs
        qa_ref[par, :, HEAD_DIM:2 * HEAD_DIM] = jnp.concatenate([selb] * HPG, axis=0)
        yield

    def selected(t, t_lo, t_hi):
        par = t % 2
        tiles = []
        for d in range(t_hi + 1):
            kind = (TAB_DIAG, TAB_SUB)[d] if d < 2 else None
            tiles.append((t - d, kind, None) if t_lo >= d else (jnp.maximum(t - d, 0), kind, t >= d))
        sel.init()
        mt = sel.qk(qa_ref[par], kaug_ref, tiles[0][0], tiles[0][1], 0)
        for n, (kt, _, _) in enumerate(tiles):
            nxt = None
            if n + 1 < len(tiles):
                kt_n, kind_n, ex_n = tiles[n + 1]
                nxt = sel.qk(qa_ref[par], kaug_ref, kt_n, kind_n, (n + 1) % 2, ex_n)
            sel.process(vst_ref, kt, n % 2, mt)
            mt = nxt
            yield
        o_s = sel.finish()
        gt = gate_ref[...].T
        for h in range(HPG):
            sl = slice(h * tq, (h + 1) * tq)
            o = gt[3 * h:3 * h + 1, :] * oc_ref[par, :, sl]
            o = o + gt[3 * h + 1:3 * h + 2, :] * o_s[:, sl]
            o = o + gt[3 * h + 2:3 * h + 3, :] * ow_ref[par, :, sl]
            o_ref[:, h * HEAD_DIM:(h + 1) * HEAD_DIM] = o.T.astype(BF16)
        yield

    if i_lo == 0:
        @pl.when(i == 0)
        def _():
            for _ in front(0, 0, 0, q_ref, 0):
                pass
    t_hi = min(i_hi + 1, n_tiles - 1)
    t_next = jnp.minimum(i + 1, n_tiles - 1)
    _spread(selected(i, i_lo, i_hi), front(t_next, i_lo + 1, t_hi, qn_ref, (i + 1) % 2),
            i_hi + 2, front_pieces(t_hi))


def _nsa(q, k_all, vt_all, cmp, cmp_t, gates, fvec, ovl, e_t, cast_weights, batch, seq_len, *, tq):
    n_t = seq_len // tq
    n_steps = batch * N_KV * n_t
    ts = tq
    step_row = lambda b, g, i: ((b * N_KV + g) * n_t + i, 0)
    cast_specs = [pl.BlockSpec((w.shape[0] // n_steps, w.shape[1]), step_row) for w in cast_weights]
    n_sel = seq_len // SEL_LEN
    n_chunks = cmp.shape[2]
    mcols = HPG * tq
    band_rows = BAND_BELOW + tq // CMP_STRIDE
    qw = HPG * HEAD_DIM

    def vt_spec(first):
        return pl.BlockSpec((None, None, HEAD_DIM + ONES_ROWS, seq_len), lambda b, g, i: (b, first + g, 0, 0))

    kern = functools.partial(_nsa_kernel, n_tiles=n_t, tq=tq, n_sel=n_sel, n_cast=len(cast_weights))
    o_attn, *cast = pl.pallas_call(
        kern,
        grid=(batch, N_KV, n_t),
        in_specs=[
            pl.BlockSpec((ts, qw), lambda b, g, i: (b * n_t + i, g)),
            pl.BlockSpec((ts, qw), lambda b, g, i: (b * n_t + jnp.minimum(i + 1, n_t - 1), g)),
            pl.BlockSpec((None, None, n_chunks, HEAD_DIM), lambda b, g, i: (b, g, 0, 0)),
            pl.BlockSpec((None, None, HEAD_DIM, n_chunks), lambda b, g, i: (b, N_KV + g, 0, 0)),
            pl.BlockSpec((seq_len, HEAD_DIM), lambda b, g, i: (b, g)),
            vt_spec(0),
            pl.BlockSpec((seq_len, HEAD_DIM), lambda b, g, i: (b, N_KV + g)),
            vt_spec(N_KV),
            pl.BlockSpec((ts, LANE), lambda b, g, i: (b * n_t + i, g)),
            pl.BlockSpec((None, HPG, 1, BIAS_LEN), lambda b, g, i: (g, 0, 0, 0)),
            pl.BlockSpec((n_sel, n_chunks), lambda b, g, i: (0, 0)),
            pl.BlockSpec((seq_len, LANE), lambda b, g, i: (0, 0)),
        ] + cast_specs,
        out_specs=[pl.BlockSpec((ts, qw), lambda b, g, i: (b * n_t + i, g))] + cast_specs,
        out_shape=[jax.ShapeDtypeStruct((batch * seq_len, ATTN_WIDTH), BF16)]
        + [jax.ShapeDtypeStruct(w.shape, BF16) for w in cast_weights],
        scratch_shapes=[
            pltpu.VMEM((seq_len, 2 * HEAD_DIM), BF16),
            pltpu.VMEM((3, tq, mcols), F32),
            pltpu.VMEM((2, band_rows, mcols), F32),
            pltpu.VMEM((n_chunks, mcols), F32),
            pltpu.VMEM((tq, mcols), F32),
            pltpu.VMEM((tq, mcols), F32),
            pltpu.VMEM((tq, mcols), F32),
            pltpu.VMEM((tq, mcols), F32),
            pltpu.VMEM((1, mcols), F32),
            pltpu.VMEM((1, mcols), F32),
            pltpu.VMEM((HEAD_DIM + ONES_ROWS, mcols), F32),
            pltpu.VMEM((HEAD_DIM + ONES_ROWS, mcols), F32),
            pltpu.VMEM((2, mcols, 2 * HEAD_DIM), BF16),
            pltpu.VMEM((2, HEAD_DIM, mcols), F32),
            pltpu.VMEM((2, HEAD_DIM, mcols), F32),
        ],
        compiler_params=_params(3),
        name="nsa_attention",
    )(q, q, cmp, cmp_t, k_all, vt_all, k_all, vt_all, gates, fvec, ovl, e_t, *cast_weights)
    return o_attn, cast


OPROJ_CHUNKS = 4


def _oproj_kernel(oa_ref, ov_ref, wo_ref, x_ref, g1_ref, g2_ref, x1_ref, h2_ref):
    ka = oa_ref.shape[1]
    tm = oa_ref.shape[0]
    chunk = tm // OPROJ_CHUNKS
    for rows in (slice(k * chunk, (k + 1) * chunk) for k in range(OPROJ_CHUNKS)):
        mix = jnp.dot(oa_ref[rows, :], wo_ref[0:ka, :], preferred_element_type=F32)
        mix = mix + jnp.dot(ov_ref[rows, :], wo_ref[ka:, :], preferred_element_type=F32)
        x1 = x_ref[rows, :] + _rms(mix, g1_ref[...])
        x1_ref[rows, :] = x1
        h2_ref[rows, :] = _rms(x1, g2_ref[...]).astype(BF16)


def _oproj(o_attn, o_conv, w_o, x2, g_post, g_pre, *, tm=512):
    m = x2.shape[0]
    ka, kv = o_attn.shape[1], o_conv.shape[1]
    row = lambda i: (i, 0)
    fixed = lambda i: (0, 0)
    return pl.pallas_call(
        _oproj_kernel,
        grid=(m // tm,),
        in_specs=[
            pl.BlockSpec((tm, ka), row),
            pl.BlockSpec((tm, kv), row),
            pl.BlockSpec((ka + kv, D_MODEL), fixed),
            pl.BlockSpec((tm, D_MODEL), row),
            pl.BlockSpec((1, D_MODEL), fixed),
            pl.BlockSpec((1, D_MODEL), fixed),
        ],
        out_specs=[pl.BlockSpec((tm, D_MODEL), row), pl.BlockSpec((tm, D_MODEL), row)],
        out_shape=[jax.ShapeDtypeStruct((m, D_MODEL), F32), jax.ShapeDtypeStruct((m, D_MODEL), BF16)],
        compiler_params=_params(1),
        name="out_proj",
    )(o_attn, o_conv, w_o, x2, g_post, g_pre)


FFN_SPLIT = 2


def _ffn_kernel(h_ref, wu_ref, wd_ref, x1_hbm, g_ref, o_ref, x1_ref, sem):
    i = pl.program_id(0)
    j = pl.program_id(1)
    last = pl.num_programs(1) - 1
    tm = h_ref.shape[0]
    width = wu_ref.shape[1] // FFN_SPLIT
    x1_copy = pltpu.make_async_copy(x1_hbm.at[pl.ds(pl.multiple_of(i * tm, tm), tm), :], x1_ref, sem)

    def step(mode):
        if mode == "first":
            x1_copy.start()
        for k in range(FFN_SPLIT):
            cols = slice(k * width, (k + 1) * width)
            a = jnp.dot(h_ref[...], wu_ref[:, cols], preferred_element_type=F32)
            a = jnp.square(jnp.maximum(a, 0.0)).astype(BF16)
            if mode == "first" and k == 0:
                o_ref[...] = jnp.dot(a, wd_ref[cols, :], preferred_element_type=F32)
            elif mode != "last" or k < FFN_SPLIT - 1:
                o_ref[...] += jnp.dot(a, wd_ref[cols, :], preferred_element_type=F32)
            else:
                x1_copy.wait()
                for rows in (slice(0, tm // 2), slice(tm // 2, tm)):
                    f = o_ref[rows, :] + jnp.dot(a[rows, :], wd_ref[cols, :], preferred_element_type=F32)
                    o_ref[rows, :] = x1_ref[rows, :] + _rms(f, g_ref[...])

    pl.when(j == 0)(functools.partial(step, "first"))
    pl.when((j > 0) & (j < last))(functools.partial(step, "middle"))
    pl.when(j == last)(functools.partial(step, "last"))


def _ffn(h2, w_up, w_down, x1, g_post, *, tm=512, tf=2048):
    m = h2.shape[0]
    d_ff = w_up.shape[1]
    assert d_ff // tf >= 2
    return pl.pallas_call(
        _ffn_kernel,
        grid=(m // tm, d_ff // tf),
        in_specs=[
            pl.BlockSpec((tm, D_MODEL), lambda i, j: (i, 0)),
            pl.BlockSpec((D_MODEL, tf), lambda i, j: (0, j)),
            pl.BlockSpec((tf, D_MODEL), lambda i, j: (j, 0)),
            pl.BlockSpec(memory_space=pl.ANY),
            pl.BlockSpec((1, D_MODEL), lambda i, j: (0, 0)),
        ],
        out_specs=pl.BlockSpec((tm, D_MODEL), lambda i, j: (i, 0)),
        out_shape=jax.ShapeDtypeStruct((m, D_MODEL), F32),
        scratch_shapes=[pltpu.VMEM((tm, D_MODEL), F32), pltpu.SemaphoreType.DMA(())],
        compiler_params=_params(2),
        name="ffn",
    )(h2, w_up, w_down, x1, g_post)


def _bucket_np(dist):
    n = np.maximum(dist, 0)
    max_exact = N_BUCKETS // 2
    nf = np.maximum(n, 1).astype(np.float32)
    large = max_exact + (np.log(nf / np.float32(max_exact)) / np.float32(math.log(MAX_DIST / max_exact))
                         * np.float32(N_BUCKETS - max_exact)).astype(np.int32)
    large = np.minimum(large, N_BUCKETS - 1)
    return np.where(n < max_exact, n, large)


def _bucket_starts():
    b = _bucket_np(np.arange(4 * MAX_DIST))
    return [int(np.argmax(b == k)) for k in range(N_BUCKETS)]


def _attention_tables(rel_bias, seq_len, tq):
    starts = _bucket_starts()
    assert starts[N_BUCKETS - 1] <= CMP_STRIDE * (BAND_BELOW + 1) - (CMP_LEN - 1)
    assert starts[N_BUCKETS - 1] <= tq and 2 * tq <= BIAS_LEN // 2
    rel = (rel_bias - rel_bias[:, N_BUCKETS - 1:]) * math.log2(math.e)
    d = jnp.arange(BIAS_LEN, dtype=jnp.int32)[None, :]
    fvec = jnp.broadcast_to(rel[:, 0:1], (N_HEADS, BIAS_LEN))
    for k in range(1, N_BUCKETS):
        fvec = jnp.where(d >= starts[k], rel[:, k:k + 1], fvec)
    fvec = jnp.where(d < BIAS_LEN // 2, fvec, NEG).astype(F32).reshape(N_KV, HPG, 1, BIAS_LEN)

    n_chunks = seq_len // CMP_STRIDE
    n_cmp = (seq_len - CMP_LEN) // CMP_STRIDE + 1
    n_sel = seq_len // SEL_LEN
    ci = np.arange(n_chunks)[None, :] * CMP_STRIDE
    sj = np.arange(n_sel)[:, None] * SEL_LEN
    ovl = ((ci < sj + SEL_LEN) & (ci + CMP_LEN > sj) & (np.arange(n_chunks)[None, :] < n_cmp))
    e_t = (np.arange(seq_len)[:, None] // SEL_LEN == np.arange(LANE)[None, :])
    return fvec, jnp.asarray(ovl, BF16), jnp.asarray(e_t, BF16)


def kernel(x, w_in, pe_cmp, w_cmp_k1, w_cmp_k2, w_cmp_v1, w_cmp_v2, conv_w, rel_bias, w_o, w_up, w_down,
           g_pre_mix, g_post_mix, g_pre_ffn, g_post_ffn):
    batch, seq_len, _ = x.shape
    depth = w_in.shape[0]
    tq = 256
    fvec, ovl, e_t = _attention_tables(rel_bias, seq_len, tq)
    x2 = x.reshape(batch * seq_len, D_MODEL)
    for l in range(depth):
        wl = jnp.swapaxes(w_in[l], 0, 1)
        g1 = g_pre_mix[l].reshape(1, D_MODEL)

        q, k_all, c_in, vt_all, gates, o_conv = _in_proj(x2, g1, wl, conv_w[l], batch, seq_len)

        cmp, cmp_t = _compress(c_in, pe_cmp[l], (w_cmp_k1[l], w_cmp_v1[l]), (w_cmp_k2[l], w_cmp_v2[l]),
                               batch, seq_len)

        o_attn, (wo_b, wup_b, wdown_b) = _nsa(q, k_all, vt_all, cmp, cmp_t, gates, fvec, ovl, e_t,
                                              [w_o[l], w_up[l], w_down[l]], batch, seq_len, tq=tq)

        x1, h2 = _oproj(o_attn, o_conv, wo_b, x2,
                        g_post_mix[l].reshape(1, D_MODEL), g_pre_ffn[l].reshape(1, D_MODEL))
        x2 = _ffn(h2, wup_b, wdown_b, x1, g_post_ffn[l].reshape(1, D_MODEL))
    return x2.reshape(batch, seq_len, D_MODEL)
```

```python
import functools
import math

import numpy as np
import jax
import jax.numpy as jnp
from jax import lax
from jax.experimental import pallas as pl
from jax.experimental.pallas import tpu as pltpu

F32 = jnp.float32
BF16 = jnp.bfloat16

D_MODEL = 2048
N_HEADS = 8
N_KV = 2
HPG = N_HEADS // N_KV
HEAD_DIM = 128
ATTN_WIDTH = N_HEADS * HEAD_DIM
KV_WIDTH = N_KV * HEAD_DIM
CONV_WIDTH = D_MODEL - ATTN_WIDTH
CONV_K = 3
N_BRANCH = 3
CMP_LEN = 32
CMP_STRIDE = 16
CMP_HIDDEN = 256
SEL_LEN = 64
SEL_TOPK = 16
WINDOW = 512
N_BUCKETS = 32
MAX_DIST = 128
EPS = 1e-6
NEG = -1e30
HALF_NEG = -5e29
FORCE = 1e9

QKV_WIDTH = ATTN_WIDTH + 6 * KV_WIDTH
GATE_OFF = QKV_WIDTH
CONV_OFF = QKV_WIDTH + N_HEADS * N_BRANCH
LANE = 128
VMEM_LIMIT = 56 * 1024 * 1024

_DN_T = (((1,), (1,)), ((), ()))


def _rms(x, g):
    ms = jnp.mean(x * x, axis=-1, keepdims=True)
    return x * lax.rsqrt(ms + EPS) * g


def _params(n_axes):
    return pltpu.CompilerParams(dimension_semantics=("arbitrary",) * n_axes, vmem_limit_bytes=VMEM_LIMIT)


QKV_BLOCK = 2 * KV_WIDTH
SLAB_KC, SLAB_VC, SLAB_KS, SLAB_VS, SLAB_KW, SLAB_VW = range(6)
CONV_BLOCK = 512
W_CHUNK = 256
SUBLANE = 8


def _load_weights(wt_hbm, w_ref, wg_ref, wconv_ref, stage_ref, sem):
    n_gate = N_HEADS * N_BRANCH
    per_group = HPG * N_BRANCH

    def store_rows(dst, row):
        def store(v):
            dst[row:row + v.shape[0], :] = v.astype(BF16)
        return store

    def store_gates(v):
        wg_ref[...] = jnp.zeros(wg_ref.shape, BF16)
        pad = jnp.zeros((2 * SUBLANE - per_group, v.shape[1]), F32)
        for g in range(N_KV):
            rows = jnp.concatenate([v[g * per_group:(g + 1) * per_group, :], pad], axis=0)
            wg_ref[g * LANE:g * LANE + 2 * SUBLANE, :] = rows.astype(BF16)

    chunks = [(r, W_CHUNK, store_rows(w_ref, r)) for r in range(0, QKV_WIDTH, W_CHUNK)]
    chunks.append((GATE_OFF, n_gate, store_gates))
    chunks += [(CONV_OFF + r, W_CHUNK, store_rows(wconv_ref, r)) for r in range(0, 3 * CONV_WIDTH, W_CHUNK)]

    def copy(k):
        src, n, _ = chunks[k]
        slot = k % 2
        return pltpu.make_async_copy(wt_hbm.at[pl.ds(src, n), :], stage_ref.at[slot, pl.ds(0, n), :], sem.at[slot])

    copy(0).start()
    for k, (_, n, store) in enumerate(chunks):
        if k + 1 < len(chunks):
            copy(k + 1).start()
        copy(k).wait()
        store(stage_ref[k % 2, 0:n, :])


def _inproj_kernel(x_ref, g_ref, wt_hbm, cw_ref, q_ref, k_ref, c_ref, vt_ref, gate_ref, ov_ref,
                   h_ref, carry_ref, w_ref, wg_ref, wconv_ref, stage_ref, sem, *, q_scale, tiles_per_seq):
    i = pl.program_id(0)
    tm = x_ref.shape[0]

    @pl.when(i == 0)
    def _():
        _load_weights(wt_hbm, w_ref, wg_ref, wconv_ref, stage_ref, sem)

    h_ref[...] = _rms(x_ref[...], g_ref[...]).astype(BF16)
    gate_ref[...] = jax.nn.sigmoid(lax.dot_general(h_ref[...], wg_ref[...], _DN_T, preferred_element_type=F32))

    def proj(w, j, width):
        return lax.dot_general(h_ref[...], w[j * width:(j + 1) * width, :], _DN_T, preferred_element_type=F32)

    for j in range(ATTN_WIDTH // QKV_BLOCK):
        q_ref[:, j * QKV_BLOCK:(j + 1) * QKV_BLOCK] = (proj(w_ref, j, QKV_BLOCK) * q_scale).astype(BF16)

    def slab(which):
        return proj(w_ref, ATTN_WIDTH // KV_WIDTH + which, KV_WIDTH)

    ones = jnp.where(lax.broadcasted_iota(jnp.int32, (ONES_ROWS, tm), 0) == 0, 1.0, 0.0).astype(BF16)
    for pair, (k_slab, c_slab, v_slab) in enumerate(((SLAB_KS, SLAB_KC, SLAB_VS), (SLAB_KW, SLAB_VC, SLAB_VW))):
        cols = slice(pair * KV_WIDTH, (pair + 1) * KV_WIDTH)
        k_ref[:, cols] = slab(k_slab).astype(BF16)
        c_ref[:, cols] = slab(c_slab)
        v = slab(v_slab)
        for g in range(N_KV):
            s = pair * N_KV + g
            vt_ref[s, 0:HEAD_DIM, :] = v[:, g * HEAD_DIM:(g + 1) * HEAD_DIM].T.astype(BF16)
            vt_ref[s, HEAD_DIM:HEAD_DIM + ONES_ROWS, :] = ones

    def conv_proj(which, j):
        return proj(wconv_ref, which * (CONV_WIDTH // CONV_BLOCK) + j, CONV_BLOCK)

    row = lax.broadcasted_iota(jnp.int32, (tm, CONV_BLOCK), 0)
    for j in range(CONV_WIDTH // CONV_BLOCK):
        cols = slice(j * CONV_BLOCK, (j + 1) * CONV_BLOCK)
        u = conv_proj(2, j) * conv_proj(0, j)
        prev = carry_ref[j]
        prev = jnp.where(i % tiles_per_seq == 0, 0.0, prev)
        carry_ref[j] = u[tm - SUBLANE:tm, :]
        p1 = prev[SUBLANE - 1:SUBLANE, :]
        p2 = prev[SUBLANE - 2:SUBLANE - 1, :]
        u1 = jnp.where(row == 0, p1, pltpu.roll(u, 1, axis=0))
        u2 = jnp.where(row == 0, p2, jnp.where(row == 1, p1, pltpu.roll(u, 2, axis=0)))
        w = cw_ref[:, cols]
        y = w[0:1, :] * u2
        y = y + w[1:2, :] * u1
        y = y + w[2:3, :] * u
        ov_ref[:, cols] = (conv_proj(1, j) * y).astype(BF16)


def _in_proj(x2, g, w_t, conv_w, batch, seq_len, *, tm=512):
    m = x2.shape[0]
    gw = N_KV * LANE
    tps = seq_len // tm
    n_slabs = QKV_BLOCK // HEAD_DIM
    kern = functools.partial(_inproj_kernel, q_scale=HEAD_DIM ** -0.5 * math.log2(math.e), tiles_per_seq=tps)
    row = lambda i: (i, 0)

    def resident(shape):
        return pl.BlockSpec(shape, lambda i: (0, 0), pipeline_mode=pl.Buffered(1))

    return pl.pallas_call(
        kern,
        grid=(m // tm,),
        in_specs=[
            pl.BlockSpec((tm, D_MODEL), row),
            resident((1, D_MODEL)),
            pl.BlockSpec(memory_space=pl.ANY),
            resident(conv_w.shape),
        ],
        out_specs=[
            pl.BlockSpec((tm, ATTN_WIDTH), row),
            pl.BlockSpec((tm, QKV_BLOCK), row),
            pl.BlockSpec((tm, QKV_BLOCK), row),
            pl.BlockSpec((None, n_slabs, HEAD_DIM + ONES_ROWS, tm), lambda i: (i // tps, 0, 0, i % tps)),
            pl.BlockSpec((tm, gw), row),
            pl.BlockSpec((tm, CONV_WIDTH), row),
        ],
        out_shape=[
            jax.ShapeDtypeStruct((m, ATTN_WIDTH), BF16),
            jax.ShapeDtypeStruct((m, QKV_BLOCK), BF16),
            jax.ShapeDtypeStruct((m, QKV_BLOCK), F32),
            jax.ShapeDtypeStruct((batch, n_slabs, HEAD_DIM + ONES_ROWS, seq_len), BF16),
            jax.ShapeDtypeStruct((m, gw), F32),
            jax.ShapeDtypeStruct((m, CONV_WIDTH), BF16),
        ],
        scratch_shapes=[pltpu.VMEM((tm, D_MODEL), BF16),
                        pltpu.VMEM((CONV_WIDTH // CONV_BLOCK, SUBLANE, CONV_BLOCK), F32),
                        pltpu.VMEM((QKV_WIDTH, D_MODEL), BF16),
                        pltpu.VMEM((gw, D_MODEL), BF16),
                        pltpu.VMEM((3 * CONV_WIDTH, D_MODEL), BF16),
                        pltpu.VMEM((2, W_CHUNK, D_MODEL), F32),
                        pltpu.SemaphoreType.DMA((2,))],
        compiler_params=_params(1),
        name="in_proj",
    )(x2, g, w_t, conv_w)


def _compress_kernel(x_ref, pe_ref, w1k_ref, w1v_ref, w2k_ref, w2v_ref, o_ref, ot_ref):
    n = x_ref.shape[0] // CMP_STRIDE

    def body(w1_ref, w2_ref):
        a = jnp.zeros((n, CMP_HIDDEN), F32)
        b = jnp.zeros((n, CMP_HIDDEN), F32)
        for l in range(CMP_STRIDE):
            xl = x_ref[pl.ds(l, n, stride=CMP_STRIDE), :]
            xa = (xl + pe_ref[l:l + 1, :]).astype(BF16)
            xb = (xl + pe_ref[CMP_STRIDE + l:CMP_STRIDE + l + 1, :]).astype(BF16)
            a = a + jnp.dot(xa, w1_ref[l].astype(BF16), preferred_element_type=F32)
            b = b + jnp.dot(xb, w1_ref[CMP_STRIDE + l].astype(BF16), preferred_element_type=F32)
        pre = a + pltpu.roll(b, n - 1, axis=0)
        hid = pre * jax.nn.sigmoid(pre)
        out = jnp.dot(hid.astype(BF16), w2_ref[...].astype(BF16), preferred_element_type=F32)
        row = lax.broadcasted_iota(jnp.int32, out.shape, 0)
        out = jnp.where(row < n - 1, out, 0.0)
        o_ref[...] = out.astype(BF16)
        ot_ref[...] = out.T.astype(BF16)

    is_value = pl.program_id(1) >= N_KV
    pl.when(jnp.logical_not(is_value))(functools.partial(body, w1k_ref, w2k_ref))
    pl.when(is_value)(functools.partial(body, w1v_ref, w2v_ref))


def _compress(c_in, pe, w1_kv, w2_kv, batch, seq_len):
    n_slabs = c_in.shape[1] // HEAD_DIM
    n_chunks = seq_len // CMP_STRIDE

    def resident(shape):
        return pl.BlockSpec(shape, lambda i, j: (0,) * len(shape), pipeline_mode=pl.Buffered(1))

    return pl.pallas_call(
        _compress_kernel,
        grid=(batch, n_slabs),
        in_specs=[
            pl.BlockSpec((seq_len, HEAD_DIM), lambda i, j: (i, j)),
            resident((CMP_LEN, HEAD_DIM)),
            resident(w1_kv[0].shape), resident(w1_kv[1].shape),
            resident(w2_kv[0].shape), resident(w2_kv[1].shape),
        ],
        out_specs=[
            pl.BlockSpec((None, None, n_chunks, HEAD_DIM), lambda i, j: (i, j, 0, 0)),
            pl.BlockSpec((None, None, HEAD_DIM, n_chunks), lambda i, j: (i, j, 0, 0)),
        ],
        out_shape=[
            jax.ShapeDtypeStruct((batch, n_slabs, n_chunks, HEAD_DIM), BF16),
            jax.ShapeDtypeStruct((batch, n_slabs, HEAD_DIM, n_chunks), BF16),
        ],
        compiler_params=_params(2),
        name="compress",
    )(c_in, pe, *w1_kv, *w2_kv)


TAB_DIAG, TAB_SUB, TAB_FAR = 0, 1, 2
BAND_BELOW = 8
ONES_ROWS = 16


N_NSA_INPUTS = 12
BIAS_LEN = 1024
_END = object()


NSA_CLASSES = 5


def _nsa_kernel(*refs, n_tiles, **static):
    i = pl.program_id(2)
    n_cls = min(NSA_CLASSES, n_tiles)
    lo = 0
    for c in range(n_cls):
        hi = lo + n_tiles // n_cls + (1 if c >= n_cls - n_tiles % n_cls else 0) - 1
        pl.when((i >= lo) & (i <= hi))(functools.partial(_nsa_step, refs, lo, hi, n_tiles, **static))
        lo = hi + 1


def _spread(main, side, n_main, n_side):
    done = 0
    for m, _ in enumerate(main, 1):
        want = (m * n_side) // n_main
        while done < want and next(side, _END) is not _END:
            done += 1
    for _ in side:
        pass


def _nsa_step(refs, i_lo, i_hi, n_tiles, *, tq, n_sel, n_cast):
    i = pl.program_id(2)
    (q_ref, qn_ref, kc_ref, vct_ref, ks_ref, vst_ref, kw_ref, vwt_ref, gate_ref, fvec_ref,
     ovl_ref, et_ref) = refs[:N_NSA_INPUTS]
    cast_in = refs[N_NSA_INPUTS:N_NSA_INPUTS + n_cast]
    o_ref = refs[N_NSA_INPUTS + n_cast]
    cast_out = refs[N_NSA_INPUTS + n_cast + 1:N_NSA_INPUTS + 2 * n_cast + 1]
    (kaug_ref, tab_ref, band_ref, sc_ref, s0_ref, s1_ref, sw0_ref, sw1_ref, m_ref, mw_ref,
     acc_ref, accw_ref, qa_ref, oc_ref, ow_ref) = refs[N_NSA_INPUTS + 2 * n_cast + 1:]

    tk = tq
    mcols = HPG * tq
    nw = WINDOW // tk
    cpt = tq // CMP_STRIDE
    band_rows = BAND_BELOW + cpt

    def group_start(fn):
        if i_lo == 0:
            pl.when(i == 0)(fn)

    @group_start
    def _():
        kaug_ref[:, 0:HEAD_DIM] = ks_ref[...]
        kaug_ref[:, HEAD_DIM:2 * HEAD_DIM] = et_ref[...]

    @group_start
    def _():
        c = lax.broadcasted_iota(jnp.int32, (tk, tq), 0)
        r = lax.broadcasted_iota(jnp.int32, (tk, tq), 1)
        far = jnp.where(r < c, 0.0, NEG)
        lane = lax.broadcasted_iota(jnp.int32, (1, BIAS_LEN), 1)
        for h in range(HPG):
            cols = slice(h * tq, (h + 1) * tq)
            f = fvec_ref[h]
            f_diag = jnp.where(lane < tq, f, NEG)
            x = pltpu.roll(jnp.broadcast_to(f_diag, (tk, BIAS_LEN)), 0, 1, stride=1, stride_axis=0)
            tab_ref[TAB_DIAG, :, cols] = x[:, 0:tq]
            x = pltpu.roll(jnp.broadcast_to(f, (tk, BIAS_LEN)), 0, 1, stride=1, stride_axis=0)
            tab_ref[TAB_SUB, :, cols] = x[:, tq:2 * tq]
            tab_ref[TAB_FAR, :, cols] = far
            for v, first in enumerate((0, -BAND_BELOW)):
                shift = (CMP_STRIDE * first + CMP_LEN - 1) % BIAS_LEN
                f_shift = pltpu.roll(f, shift, 1)
                x = pltpu.roll(jnp.broadcast_to(f_shift, (band_rows, BIAS_LEN)), 0, 1,
                               stride=CMP_STRIDE, stride_axis=0)
                band_ref[v, :, cols] = x[:, 0:tq]

    for src, dst in zip(cast_in, cast_out):
        dst[...] = src[...].astype(BF16)

    def keys(kt):
        if isinstance(kt, int):
            return slice(kt * tk, (kt + 1) * tk)
        return pl.ds(pl.multiple_of(kt * tk, tk), tk)

    class Flash:
        def __init__(self, s_refs, m_ref, acc_ref):
            self.s_refs, self.m_ref, self.acc_ref = s_refs, m_ref, acc_ref

        def init(self):
            self.m_ref[...] = jnp.full((1, mcols), NEG, F32)
            self.acc_ref[...] = jnp.zeros(self.acc_ref.shape, F32)

        def qk(self, qmat, k_ref, kt, kind, buf, exists=None):
            s = lax.dot_general(k_ref[keys(kt), :], qmat, _DN_T, preferred_element_type=F32)
            if kind is not None:
                s = s + tab_ref[kind]
            if exists is not None:
                s = s + jnp.where(exists, 0.0, NEG)
            self.s_refs[buf][...] = s
            return jnp.max(s, axis=0, keepdims=True)

        def process(self, vt_ref, kt, buf, m_tile):
            m_prev = self.m_ref[...]
            m_next = jnp.maximum(m_prev, m_tile)
            alpha = jnp.exp2(m_prev - m_next)
            p = jnp.exp2(self.s_refs[buf][...] - m_next).astype(BF16)
            self.acc_ref[...] = alpha * self.acc_ref[...] + jnp.dot(
                vt_ref[:, keys(kt)], p, preferred_element_type=F32)
            self.m_ref[...] = m_next

        def finish(self):
            return self.acc_ref[0:HEAD_DIM, :] * (1.0 / self.acc_ref[HEAD_DIM:HEAD_DIM + 1, :])

    win = Flash((sw0_ref, sw1_ref), mw_ref, accw_ref)
    sel = Flash((s0_ref, s1_ref), m_ref, acc_ref)

    assert nw == 2, "window tiles are t, t-1 (previous-tile table) and t-nw (window-edge table)"

    def front_pieces(t_hi):
        n_valid = (t_hi + 1) * tq // SEL_LEN
        return 6 + -(-n_valid // SUBLANE)

    def front(t, t_lo, t_hi, qsrc_ref, par):
        n_valid = (t_hi + 1) * tq // SEL_LEN
        nb8 = -(-n_valid // SUBLANE) * SUBLANE
        nb16 = -(-n_valid // 16) * 16
        nc = -(-(cpt * (t_hi + 1)) // LANE) * LANE
        n_win = min(t_hi, nw) + 1
        q = qsrc_ref[...]
        qs = jnp.concatenate([q[:, h * HEAD_DIM:(h + 1) * HEAD_DIM] for h in range(HPG)], axis=0)

        def win_tile(d):
            if t_lo >= d:
                return t - d, None
            return jnp.maximum(t - d, 0), t >= d

        win.init()
        mt_w0 = win.qk(qs, kw_ref, t, TAB_DIAG, 0)
        raw = lax.dot_general(kc_ref[0:nc, :], qs, _DN_T, preferred_element_type=F32)
        crow = lax.broadcasted_iota(jnp.int32, raw.shape, 0)
        sc_ref[0:nc, :] = jnp.where(crow < cpt * (t + 1), raw, NEG)
        if isinstance(t, int):
            b0 = max(cpt * t - BAND_BELOW, 0)
            sc_ref[b0:b0 + band_rows, :] += band_ref[min(t, 1)]
        else:
            band = pl.ds(pl.multiple_of(jnp.maximum(cpt * t - BAND_BELOW, 0), SUBLANE), band_rows)
            sc_ref[band, :] += band_ref[jnp.minimum(t, 1)]
        yield

        if n_win > 1:
            kt1, ex1 = win_tile(1)
            mt_w1 = win.qk(qs, kw_ref, kt1, TAB_SUB, 1, ex1)
        win.process(vwt_ref, t, 0, mt_w0)
        yield

        sc = sc_ref[0:nc, :]
        mc = jnp.maximum(jnp.max(sc, axis=0, keepdims=True), HALF_NEG)
        pc = jnp.exp2(sc - mc)
        lc = jnp.sum(pc, axis=0, keepdims=True)
        pc = pc * jnp.where(lc > 0.0, 1.0 / lc, 0.0)
        oc_ref[par] = jnp.dot(vct_ref[:, 0:nc], pc.astype(BF16), preferred_element_type=F32)
        yield

        if n_win > 2:
            kt2, ex2 = win_tile(nw)
            mt_w2 = win.qk(qs, kw_ref, kt2, TAB_FAR, 0, ex2)
        if n_win > 1:
            win.process(vwt_ref, kt1, 1, mt_w1)
        yield

        ps = pc[:, 0:tq] + pc[:, tq:2 * tq] + pc[:, 2 * tq:3 * tq] + pc[:, 3 * tq:4 * tq]
        hi = ps.astype(BF16)
        r1 = ps - hi.astype(F32)
        mid = r1.astype(BF16)
        lo = (r1 - mid.astype(F32)).astype(BF16)
        ovl = ovl_ref[0:nb16, 0:nc]
        imp = (jnp.dot(ovl, hi, preferred_element_type=F32) + jnp.dot(ovl, mid, preferred_element_type=F32)
               + jnp.dot(ovl, lo, preferred_element_type=F32))
        jj = lax.broadcasted_iota(jnp.int32, (nb16, tq), 0)
        tt = t * tq + lax.broadcasted_iota(jnp.int32, (nb16, tq), 1)
        cur = tt >> int(math.log2(SEL_LEN))
        forced = (jj == 0) | (jj == cur) | (jj == cur - 1)
        imp = jnp.where(forced, FORCE, imp)
        imp = jnp.where(jj * SEL_LEN <= tt, imp, NEG)
        yield

        sub = SUBLANE
        ranks = []
        for j0 in range(0, nb8, sub):
            blk = imp[j0:j0 + sub, :]
            jl = j0 + lax.broadcasted_iota(jnp.int32, blk.shape, 0)
            cnt = jnp.zeros(blk.shape, jnp.int32)
            for b in range(n_valid):
                row = imp[b:b + 1, :]
                if b < j0:
                    cnt = cnt + jnp.where(row >= blk, 1, 0)
                elif b >= j0 + sub:
                    cnt = cnt + jnp.where(row > blk, 1, 0)
                else:
                    cnt = cnt + jnp.where(row > blk, 1, jnp.where(row == blk, jnp.where(jl > b, 1, 0), 0))
            ranks.append(cnt)
            yield
        rank = jnp.concatenate(ranks, axis=0)
        selb_t = jnp.where(rank < SEL_TOPK, 0.0, NEG)

        if n_win > 2:
            win.process(vwt_ref, kt2, 0, mt_w2)
        ow_ref[par] = win.finish()

        selb = jnp.concatenate([selb_t, jnp.zeros((LANE - nb8, tq), F32)], axis=0).T
        selb = selb.astype(BF16)
        qa_ref[par, :, 0:HEAD_DIM] = qs
        qa_ref[par, :, HEAD_DIM:2 * HEAD_DIM] = jnp.concatenate([selb] * HPG, axis=0)
        yield

    def selected(t, t_lo, t_hi):
        tiles = []
        for d in range(t_hi + 1):
            kind = (TAB_DIAG, TAB_SUB)[d] if d < 2 else None
            tiles.append((t - d, kind, None) if t_lo >= d else (jnp.maximum(t - d, 0), kind, t >= d))
        qa_ref[0] = qa_ref[1]
        gt = gate_ref[...].T
        for h in range(HPG):
            sl = slice(h * tq, (h + 1) * tq)
            oc_ref[0, :, sl] = (gt[3 * h:3 * h + 1, :] * oc_ref[1, :, sl]
                                + gt[3 * h + 2:3 * h + 3, :] * ow_ref[1, :, sl])
        sel.init()
        mt = sel.qk(qa_ref[0], kaug_ref, tiles[0][0], tiles[0][1], 0)
        for n, (kt, _, _) in enumerate(tiles):
            nxt = None
            if n + 1 < len(tiles):
                kt_n, kind_n, ex_n = tiles[n + 1]
                nxt = sel.qk(qa_ref[0], kaug_ref, kt_n, kind_n, (n + 1) % 2, ex_n)
            sel.process(vst_ref, kt, n % 2, mt)
            mt = nxt
            yield
        o_s = sel.finish()
        gt = gate_ref[...].T
        for h in range(HPG):
            sl = slice(h * tq, (h + 1) * tq)
            o = oc_ref[0, :, sl] + gt[3 * h + 1:3 * h + 2, :] * o_s[:, sl]
            o_ref[:, h * HEAD_DIM:(h + 1) * HEAD_DIM] = o.T.astype(BF16)
        yield

    if i_lo == 0:
        @pl.when(i == 0)
        def _():
            for _ in front(0, 0, 0, q_ref, 1):
                pass
    t_hi = min(i_hi + 1, n_tiles - 1)
    t_next = jnp.minimum(i + 1, n_tiles - 1)
    _spread(selected(i, i_lo, i_hi), front(t_next, i_lo + 1, t_hi, qn_ref, 1),
            i_hi + 2, front_pieces(t_hi))


def _nsa(q, k_all, vt_all, cmp, cmp_t, gates, fvec, ovl, e_t, cast_weights, batch, seq_len, *, tq):
    n_t = seq_len // tq
    n_steps = batch * N_KV * n_t
    ts = tq
    step_row = lambda b, g, i: ((b * N_KV + g) * n_t + i, 0)
    cast_specs = [pl.BlockSpec((w.shape[0] // n_steps, w.shape[1]), step_row) for w in cast_weights]
    n_sel = seq_len // SEL_LEN
    n_chunks = cmp.shape[2]
    mcols = HPG * tq
    band_rows = BAND_BELOW + tq // CMP_STRIDE
    qw = HPG * HEAD_DIM

    def vt_spec(first):
        return pl.BlockSpec((None, None, HEAD_DIM + ONES_ROWS, seq_len), lambda b, g, i: (b, first + g, 0, 0))

    kern = functools.partial(_nsa_kernel, n_tiles=n_t, tq=tq, n_sel=n_sel, n_cast=len(cast_weights))
    o_attn, *cast = pl.pallas_call(
        kern,
        grid=(batch, N_KV, n_t),
        in_specs=[
            pl.BlockSpec((ts, qw), lambda b, g, i: (b * n_t + i, g)),
            pl.BlockSpec((ts, qw), lambda b, g, i: (b * n_t + jnp.minimum(i + 1, n_t - 1), g)),
            pl.BlockSpec((None, None, n_chunks, HEAD_DIM), lambda b, g, i: (b, g, 0, 0)),
            pl.BlockSpec((None, None, HEAD_DIM, n_chunks), lambda b, g, i: (b, N_KV + g, 0, 0)),
            pl.BlockSpec((seq_len, HEAD_DIM), lambda b, g, i: (b, g)),
            vt_spec(0),
            pl.BlockSpec((seq_len, HEAD_DIM), lambda b, g, i: (b, N_KV + g)),
            vt_spec(N_KV),
            pl.BlockSpec((ts, LANE), lambda b, g, i: (b * n_t + i, g)),
            pl.BlockSpec((None, HPG, 1, BIAS_LEN), lambda b, g, i: (g, 0, 0, 0)),
            pl.BlockSpec((n_sel, n_chunks), lambda b, g, i: (0, 0)),
            pl.BlockSpec((seq_len, LANE), lambda b, g, i: (0, 0)),
        ] + cast_specs,
        out_specs=[pl.BlockSpec((ts, qw), lambda b, g, i: (b * n_t + i, g))] + cast_specs,
        out_shape=[jax.ShapeDtypeStruct((batch * seq_len, ATTN_WIDTH), BF16)]
        + [jax.ShapeDtypeStruct(w.shape, BF16) for w in cast_weights],
        scratch_shapes=[
            pltpu.VMEM((seq_len, 2 * HEAD_DIM), BF16),
            pltpu.VMEM((3, tq, mcols), F32),
            pltpu.VMEM((2, band_rows, mcols), F32),
            pltpu.VMEM((n_chunks, mcols), F32),
            pltpu.VMEM((tq, mcols), F32),
            pltpu.VMEM((tq, mcols), F32),
            pltpu.VMEM((tq, mcols), F32),
            pltpu.VMEM((tq, mcols), F32),
            pltpu.VMEM((1, mcols), F32),
            pltpu.VMEM((1, mcols), F32),
            pltpu.VMEM((HEAD_DIM + ONES_ROWS, mcols), F32),
            pltpu.VMEM((HEAD_DIM + ONES_ROWS, mcols), F32),
            pltpu.VMEM((2, mcols, 2 * HEAD_DIM), BF16),
            pltpu.VMEM((2, HEAD_DIM, mcols), F32),
            pltpu.VMEM((2, HEAD_DIM, mcols), F32),
        ],
        compiler_params=_params(3),
        name="nsa_attention",
    )(q, q, cmp, cmp_t, k_all, vt_all, k_all, vt_all, gates, fvec, ovl, e_t, *cast_weights)
    return o_attn, cast


OPROJ_CHUNKS = 4


def _oproj_kernel(oa_ref, ov_ref, wo_ref, x_ref, g1_ref, g2_ref, x1_ref, h2_ref):
    ka = oa_ref.shape[1]
    tm = oa_ref.shape[0]
    chunk = tm // OPROJ_CHUNKS
    for rows in (slice(k * chunk, (k + 1) * chunk) for k in range(OPROJ_CHUNKS)):
        mix = jnp.dot(oa_ref[rows, :], wo_ref[0:ka, :], preferred_element_type=F32)
        mix = mix + jnp.dot(ov_ref[rows, :], wo_ref[ka:, :], preferred_element_type=F32)
        x1 = x_ref[rows, :] + _rms(mix, g1_ref[...])
        x1_ref[rows, :] = x1
        h2_ref[rows, :] = _rms(x1, g2_ref[...]).astype(BF16)


def _oproj(o_attn, o_conv, w_o, x2, g_post, g_pre, *, tm=512):
    m = x2.shape[0]
    ka, kv = o_attn.shape[1], o_conv.shape[1]
    row = lambda i: (i, 0)
    fixed = lambda i: (0, 0)
    return pl.pallas_call(
        _oproj_kernel,
        grid=(m // tm,),
        in_specs=[
            pl.BlockSpec((tm, ka), row),
            pl.BlockSpec((tm, kv), row),
            pl.BlockSpec((ka + kv, D_MODEL), fixed),
            pl.BlockSpec((tm, D_MODEL), row),
            pl.BlockSpec((1, D_MODEL), fixed),
            pl.BlockSpec((1, D_MODEL), fixed),
        ],
        out_specs=[pl.BlockSpec((tm, D_MODEL), row), pl.BlockSpec((tm, D_MODEL), row)],
        out_shape=[jax.ShapeDtypeStruct((m, D_MODEL), F32), jax.ShapeDtypeStruct((m, D_MODEL), BF16)],
        compiler_params=_params(1),
        name="out_proj",
    )(o_attn, o_conv, w_o, x2, g_post, g_pre)


FFN_SPLIT = 2


def _ffn_kernel(h_ref, wu_ref, wd_ref, x1_hbm, g_ref, o_ref, x1_ref, sem):
    i = pl.program_id(0)
    j = pl.program_id(1)
    last = pl.num_programs(1) - 1
    tm = h_ref.shape[0]
    width = wu_ref.shape[1] // FFN_SPLIT
    x1_copy = pltpu.make_async_copy(x1_hbm.at[pl.ds(pl.multiple_of(i * tm, tm), tm), :], x1_ref, sem)

    def step(mode):
        if mode == "first":
            x1_copy.start()
        for k in range(FFN_SPLIT):
            cols = slice(k * width, (k + 1) * width)
            a = jnp.dot(h_ref[...], wu_ref[:, cols], preferred_element_type=F32)
            a = jnp.square(jnp.maximum(a, 0.0)).astype(BF16)
            if mode == "first" and k == 0:
                o_ref[...] = jnp.dot(a, wd_ref[cols, :], preferred_element_type=F32)
            elif mode != "last" or k < FFN_SPLIT - 1:
                o_ref[...] += jnp.dot(a, wd_ref[cols, :], preferred_element_type=F32)
            else:
                x1_copy.wait()
                for rows in (slice(0, tm // 2), slice(tm // 2, tm)):
                    f = o_ref[rows, :] + jnp.dot(a[rows, :], wd_ref[cols, :], preferred_element_type=F32)
                    o_ref[rows, :] = x1_ref[rows, :] + _rms(f, g_ref[...])

    pl.when(j == 0)(functools.partial(step, "first"))
    pl.when((j > 0) & (j < last))(functools.partial(step, "middle"))
    pl.when(j == last)(functools.partial(step, "last"))


def _ffn(h2, w_up, w_down, x1, g_post, *, tm=512, tf=2048):
    m = h2.shape[0]
    d_ff = w_up.shape[1]
    assert d_ff // tf >= 2
    return pl.pallas_call(
        _ffn_kernel,
        grid=(m // tm, d_ff // tf),
        in_specs=[
            pl.BlockSpec((tm, D_MODEL), lambda i, j: (i, 0)),
            pl.BlockSpec((D_MODEL, tf), lambda i, j: (0, j)),
            pl.BlockSpec((tf, D_MODEL), lambda i, j: (j, 0)),
            pl.BlockSpec(memory_space=pl.ANY),
            pl.BlockSpec((1, D_MODEL), lambda i, j: (0, 0)),
        ],
        out_specs=pl.BlockSpec((tm, D_MODEL), lambda i, j: (i, 0)),
        out_shape=jax.ShapeDtypeStruct((m, D_MODEL), F32),
        scratch_shapes=[pltpu.VMEM((tm, D_MODEL), F32), pltpu.SemaphoreType.DMA(())],
        compiler_params=_params(2),
        name="ffn",
    )(h2, w_up, w_down, x1, g_post)


def _bucket_np(dist):
    n = np.maximum(dist, 0)
    max_exact = N_BUCKETS // 2
    nf = np.maximum(n, 1).astype(np.float32)
    large = max_exact + (np.log(nf / np.float32(max_exact)) / np.float32(math.log(MAX_DIST / max_exact))
                         * np.float32(N_BUCKETS - max_exact)).astype(np.int32)
    large = np.minimum(large, N_BUCKETS - 1)
    return np.where(n < max_exact, n, large)


def _bucket_starts():
    b = _bucket_np(np.arange(4 * MAX_DIST))
    return [int(np.argmax(b == k)) for k in range(N_BUCKETS)]


def _attention_tables(rel_bias, seq_len, tq):
    starts = _bucket_starts()
    assert starts[N_BUCKETS - 1] <= CMP_STRIDE * (BAND_BELOW + 1) - (CMP_LEN - 1)
    assert starts[N_BUCKETS - 1] <= tq and 2 * tq <= BIAS_LEN // 2
    rel = (rel_bias - rel_bias[:, N_BUCKETS - 1:]) * math.log2(math.e)
    d = jnp.arange(BIAS_LEN, dtype=jnp.int32)[None, :]
    fvec = jnp.broadcast_to(rel[:, 0:1], (N_HEADS, BIAS_LEN))
    for k in range(1, N_BUCKETS):
        fvec = jnp.where(d >= starts[k], rel[:, k:k + 1], fvec)
    fvec = jnp.where(d < BIAS_LEN // 2, fvec, NEG).astype(F32).reshape(N_KV, HPG, 1, BIAS_LEN)

    n_chunks = seq_len // CMP_STRIDE
    n_cmp = (seq_len - CMP_LEN) // CMP_STRIDE + 1
    n_sel = seq_len // SEL_LEN
    ci = np.arange(n_chunks)[None, :] * CMP_STRIDE
    sj = np.arange(n_sel)[:, None] * SEL_LEN
    ovl = ((ci < sj + SEL_LEN) & (ci + CMP_LEN > sj) & (np.arange(n_chunks)[None, :] < n_cmp))
    e_t = (np.arange(seq_len)[:, None] // SEL_LEN == np.arange(LANE)[None, :])
    return fvec, jnp.asarray(ovl, BF16), jnp.asarray(e_t, BF16)


def kernel(x, w_in, pe_cmp, w_cmp_k1, w_cmp_k2, w_cmp_v1, w_cmp_v2, conv_w, rel_bias, w_o, w_up, w_down,
           g_pre_mix, g_post_mix, g_pre_ffn, g_post_ffn):
    batch, seq_len, _ = x.shape
    depth = w_in.shape[0]
    tq = 256
    fvec, ovl, e_t = _attention_tables(rel_bias, seq_len, tq)
    x2 = x.reshape(batch * seq_len, D_MODEL)
    for l in range(depth):
        wl = jnp.swapaxes(w_in[l], 0, 1)
        g1 = g_pre_mix[l].reshape(1, D_MODEL)

        q, k_all, c_in, vt_all, gates, o_conv = _in_proj(x2, g1, wl, conv_w[l], batch, seq_len)

        cmp, cmp_t = _compress(c_in, pe_cmp[l], (w_cmp_k1[l], w_cmp_v1[l]), (w_cmp_k2[l], w_cmp_v2[l]),
                               batch, seq_len)

        o_attn, (wo_b, wup_b, wdown_b) = _nsa(q, k_all, vt_all, cmp, cmp_t, gates, fvec, ovl, e_t,
                                              [w_o[l], w_up[l], w_down[l]], batch, seq_len, tq=tq)

        x1, h2 = _oproj(o_attn, o_conv, wo_b, x2,
                        g_post_mix[l].reshape(1, D_MODEL), g_pre_ffn[l].reshape(1, D_MODEL))
        x2 = _ffn(h2, wup_b, wdown_b, x1, g_post_ffn[l].reshape(1, D_MODEL))
    return x2.reshape(batch, seq_len, D_MODEL)
```

```python
import functools
import math

import numpy as np
import jax
import jax.numpy as jnp
from jax import lax
from jax.experimental import pallas as pl
from jax.experimental.pallas import tpu as pltpu

F32 = jnp.float32
BF16 = jnp.bfloat16

D_MODEL = 2048
N_HEADS = 8
N_KV = 2
HPG = N_HEADS // N_KV
HEAD_DIM = 128
ATTN_WIDTH = N_HEADS * HEAD_DIM
KV_WIDTH = N_KV * HEAD_DIM
CONV_WIDTH = D_MODEL - ATTN_WIDTH
CONV_K = 3
N_BRANCH = 3
CMP_LEN = 32
CMP_STRIDE = 16
CMP_HIDDEN = 256
SEL_LEN = 64
SEL_TOPK = 16
WINDOW = 512
N_BUCKETS = 32
MAX_DIST = 128
EPS = 1e-6
NEG = -1e30
HALF_NEG = -5e29
FORCE = 1e9

QKV_WIDTH = ATTN_WIDTH + 6 * KV_WIDTH
GATE_OFF = QKV_WIDTH
CONV_OFF = QKV_WIDTH + N_HEADS * N_BRANCH
LANE = 128
VMEM_LIMIT = 56 * 1024 * 1024

_DN_T = (((1,), (1,)), ((), ()))


def _rms(x, g):
    ms = jnp.mean(x * x, axis=-1, keepdims=True)
    return x * lax.rsqrt(ms + EPS) * g


def _params(n_axes):
    return pltpu.CompilerParams(dimension_semantics=("arbitrary",) * n_axes, vmem_limit_bytes=VMEM_LIMIT)


QKV_BLOCK = 2 * KV_WIDTH
SLAB_KC, SLAB_VC, SLAB_KS, SLAB_VS, SLAB_KW, SLAB_VW = range(6)
CONV_BLOCK = 512
W_CHUNK = 256
SUBLANE = 8


def _load_weights(wt_hbm, w_ref, wg_ref, wconv_ref, stage_ref, sem):
    n_gate = N_HEADS * N_BRANCH
    per_group = HPG * N_BRANCH

    def store_rows(dst, row):
        def store(v):
            dst[row:row + v.shape[0], :] = v.astype(BF16)
        return store

    def store_gates(v):
        wg_ref[...] = jnp.zeros(wg_ref.shape, BF16)
        pad = jnp.zeros((2 * SUBLANE - per_group, v.shape[1]), F32)
        for g in range(N_KV):
            rows = jnp.concatenate([v[g * per_group:(g + 1) * per_group, :], pad], axis=0)
            wg_ref[g * LANE:g * LANE + 2 * SUBLANE, :] = rows.astype(BF16)

    chunks = [(r, W_CHUNK, store_rows(w_ref, r)) for r in range(0, QKV_WIDTH, W_CHUNK)]
    chunks.append((GATE_OFF, n_gate, store_gates))
    chunks += [(CONV_OFF + r, W_CHUNK, store_rows(wconv_ref, r)) for r in range(0, 3 * CONV_WIDTH, W_CHUNK)]

    def copy(k):
        src, n, _ = chunks[k]
        slot = k % 2
        return pltpu.make_async_copy(wt_hbm.at[pl.ds(src, n), :], stage_ref.at[slot, pl.ds(0, n), :], sem.at[slot])

    copy(0).start()
    for k, (_, n, store) in enumerate(chunks):
        if k + 1 < len(chunks):
            copy(k + 1).start()
        copy(k).wait()
        store(stage_ref[k % 2, 0:n, :])


def _inproj_kernel(x_ref, g_ref, wt_hbm, cw_ref, q_ref, k_ref, c_ref, vt_ref, gate_ref, ov_ref,
                   h_ref, carry_ref, w_ref, wg_ref, wconv_ref, stage_ref, sem, *, q_scale, tiles_per_seq):
    i = pl.program_id(0)
    tm = x_ref.shape[0]

    @pl.when(i == 0)
    def _():
        _load_weights(wt_hbm, w_ref, wg_ref, wconv_ref, stage_ref, sem)

    h_ref[...] = _rms(x_ref[...], g_ref[...]).astype(BF16)
    gate_ref[...] = jax.nn.sigmoid(lax.dot_general(h_ref[...], wg_ref[...], _DN_T, preferred_element_type=F32))

    def proj(w, j, width):
        return lax.dot_general(h_ref[...], w[j * width:(j + 1) * width, :], _DN_T, preferred_element_type=F32)

    for j in range(ATTN_WIDTH // QKV_BLOCK):
        q_ref[:, j * QKV_BLOCK:(j + 1) * QKV_BLOCK] = (proj(w_ref, j, QKV_BLOCK) * q_scale).astype(BF16)

    def slab(which):
        return proj(w_ref, ATTN_WIDTH // KV_WIDTH + which, KV_WIDTH)

    ones = jnp.where(lax.broadcasted_iota(jnp.int32, (ONES_ROWS, tm), 0) == 0, 1.0, 0.0).astype(BF16)
    for pair, (k_slab, c_slab, v_slab) in enumerate(((SLAB_KS, SLAB_KC, SLAB_VS), (SLAB_KW, SLAB_VC, SLAB_VW))):
        cols = slice(pair * KV_WIDTH, (pair + 1) * KV_WIDTH)
        k_ref[:, cols] = slab(k_slab).astype(BF16)
        c_ref[:, cols] = slab(c_slab)
        v = slab(v_slab)
        for g in range(N_KV):
            s = pair * N_KV + g
            vt_ref[s, 0:HEAD_DIM, :] = v[:, g * HEAD_DIM:(g + 1) * HEAD_DIM].T.astype(BF16)
            vt_ref[s, HEAD_DIM:HEAD_DIM + ONES_ROWS, :] = ones

    def conv_proj(which, j):
        return proj(wconv_ref, which * (CONV_WIDTH // CONV_BLOCK) + j, CONV_BLOCK)

    row = lax.broadcasted_iota(jnp.int32, (tm, CONV_BLOCK), 0)
    for j in range(CONV_WIDTH // CONV_BLOCK):
        cols = slice(j * CONV_BLOCK, (j + 1) * CONV_BLOCK)
        u = conv_proj(2, j) * conv_proj(0, j)
        prev = carry_ref[j]
        prev = jnp.where(i % tiles_per_seq == 0, 0.0, prev)
        carry_ref[j] = u[tm - SUBLANE:tm, :]
        p1 = prev[SUBLANE - 1:SUBLANE, :]
        p2 = prev[SUBLANE - 2:SUBLANE - 1, :]
        u1 = jnp.where(row == 0, p1, pltpu.roll(u, 1, axis=0))
        u2 = jnp.where(row == 0, p2, jnp.where(row == 1, p1, pltpu.roll(u, 2, axis=0)))
        w = cw_ref[:, cols]
        y = w[0:1, :] * u2
        y = y + w[1:2, :] * u1
        y = y + w[2:3, :] * u
        ov_ref[:, cols] = (conv_proj(1, j) * y).astype(BF16)


def _in_proj(x2, g, w_t, conv_w, batch, seq_len, *, tm=512):
    m = x2.shape[0]
    gw = N_KV * LANE
    tps = seq_len // tm
    n_slabs = QKV_BLOCK // HEAD_DIM
    kern = functools.partial(_inproj_kernel, q_scale=HEAD_DIM ** -0.5 * math.log2(math.e), tiles_per_seq=tps)
    row = lambda i: (i, 0)

    def resident(shape):
        return pl.BlockSpec(shape, lambda i: (0, 0), pipeline_mode=pl.Buffered(1))

    return pl.pallas_call(
        kern,
        grid=(m // tm,),
        in_specs=[
            pl.BlockSpec((tm, D_MODEL), row),
            resident((1, D_MODEL)),
            pl.BlockSpec(memory_space=pl.ANY),
            resident(conv_w.shape),
        ],
        out_specs=[
            pl.BlockSpec((tm, ATTN_WIDTH), row),
            pl.BlockSpec((tm, QKV_BLOCK), row),
            pl.BlockSpec((tm, QKV_BLOCK), row),
            pl.BlockSpec((None, n_slabs, HEAD_DIM + ONES_ROWS, tm), lambda i: (i // tps, 0, 0, i % tps)),
            pl.BlockSpec((tm, gw), row),
            pl.BlockSpec((tm, CONV_WIDTH), row),
        ],
        out_shape=[
            jax.ShapeDtypeStruct((m, ATTN_WIDTH), BF16),
            jax.ShapeDtypeStruct((m, QKV_BLOCK), BF16),
            jax.ShapeDtypeStruct((m, QKV_BLOCK), F32),
            jax.ShapeDtypeStruct((batch, n_slabs, HEAD_DIM + ONES_ROWS, seq_len), BF16),
            jax.ShapeDtypeStruct((m, gw), F32),
            jax.ShapeDtypeStruct((m, CONV_WIDTH), BF16),
        ],
        scratch_shapes=[pltpu.VMEM((tm, D_MODEL), BF16),
                        pltpu.VMEM((CONV_WIDTH // CONV_BLOCK, SUBLANE, CONV_BLOCK), F32),
                        pltpu.VMEM((QKV_WIDTH, D_MODEL), BF16),
                        pltpu.VMEM((gw, D_MODEL), BF16),
                        pltpu.VMEM((3 * CONV_WIDTH, D_MODEL), BF16),
                        pltpu.VMEM((2, W_CHUNK, D_MODEL), F32),
                        pltpu.SemaphoreType.DMA((2,))],
        compiler_params=_params(1),
        name="in_proj",
    )(x2, g, w_t, conv_w)


def _compress_kernel(x_ref, pe_ref, w1k_ref, w1v_ref, w2k_ref, w2v_ref, o_ref, ot_ref):
    n = x_ref.shape[0] // CMP_STRIDE

    def body(w1_ref, w2_ref):
        a = jnp.zeros((n, CMP_HIDDEN), F32)
        b = jnp.zeros((n, CMP_HIDDEN), F32)
        for l in range(CMP_STRIDE):
            xl = x_ref[pl.ds(l, n, stride=CMP_STRIDE), :]
            xa = (xl + pe_ref[l:l + 1, :]).astype(BF16)
            xb = (xl + pe_ref[CMP_STRIDE + l:CMP_STRIDE + l + 1, :]).astype(BF16)
            a = a + jnp.dot(xa, w1_ref[l].astype(BF16), preferred_element_type=F32)
            b = b + jnp.dot(xb, w1_ref[CMP_STRIDE + l].astype(BF16), preferred_element_type=F32)
        pre = a + pltpu.roll(b, n - 1, axis=0)
        hid = pre * jax.nn.sigmoid(pre)
        out = jnp.dot(hid.astype(BF16), w2_ref[...].astype(BF16), preferred_element_type=F32)
        row = lax.broadcasted_iota(jnp.int32, out.shape, 0)
        out = jnp.where(row < n - 1, out, 0.0)
        o_ref[...] = out.astype(BF16)
        ot_ref[...] = out.T.astype(BF16)

    is_value = pl.program_id(1) >= N_KV
    pl.when(jnp.logical_not(is_value))(functools.partial(body, w1k_ref, w2k_ref))
    pl.when(is_value)(functools.partial(body, w1v_ref, w2v_ref))


def _compress(c_in, pe, w1_kv, w2_kv, batch, seq_len):
    n_slabs = c_in.shape[1] // HEAD_DIM
    n_chunks = seq_len // CMP_STRIDE

    def resident(shape):
        return pl.BlockSpec(shape, lambda i, j: (0,) * len(shape), pipeline_mode=pl.Buffered(1))

    return pl.pallas_call(
        _compress_kernel,
        grid=(batch, n_slabs),
        in_specs=[
            pl.BlockSpec((seq_len, HEAD_DIM), lambda i, j: (i, j)),
            resident((CMP_LEN, HEAD_DIM)),
            resident(w1_kv[0].shape), resident(w1_kv[1].shape),
            resident(w2_kv[0].shape), resident(w2_kv[1].shape),
        ],
        out_specs=[
            pl.BlockSpec((None, None, n_chunks, HEAD_DIM), lambda i, j: (i, j, 0, 0)),
            pl.BlockSpec((None, None, HEAD_DIM, n_chunks), lambda i, j: (i, j, 0, 0)),
        ],
        out_shape=[
            jax.ShapeDtypeStruct((batch, n_slabs, n_chunks, HEAD_DIM), BF16),
            jax.ShapeDtypeStruct((batch, n_slabs, HEAD_DIM, n_chunks), BF16),
        ],
        compiler_params=_params(2),
        name="compress",
    )(c_in, pe, *w1_kv, *w2_kv)


TAB_DIAG, TAB_SUB, TAB_FAR = 0, 1, 2
BAND_BELOW = 8
ONES_ROWS = 16


N_NSA_INPUTS = 12
BIAS_LEN = 1024
_END = object()


NSA_CLASSES = 5


def _nsa_kernel(*refs, n_tiles, **static):
    i = pl.program_id(2)
    n_cls = min(NSA_CLASSES, n_tiles)
    lo = 0
    for c in range(n_cls):
        hi = lo + n_tiles // n_cls + (1 if c >= n_cls - n_tiles % n_cls else 0) - 1
        pl.when((i >= lo) & (i <= hi))(functools.partial(_nsa_step, refs, lo, hi, n_tiles, **static))
        lo = hi + 1


def _spread(main, side, n_main, n_side):
    done = 0
    for m, _ in enumerate(main, 1):
        want = (m * n_side) // n_main
        while done < want and next(side, _END) is not _END:
            done += 1
    for _ in side:
        pass


def _nsa_step(refs, i_lo, i_hi, n_tiles, *, tq, n_sel, n_cast):
    i = pl.program_id(2)
    (q_ref, qn_ref, kc_ref, vct_ref, ks_ref, vst_ref, kw_ref, vwt_ref, gate_ref, fvec_ref,
     ovl_ref, et_ref) = refs[:N_NSA_INPUTS]
    cast_in = refs[N_NSA_INPUTS:N_NSA_INPUTS + n_cast]
    o_ref = refs[N_NSA_INPUTS + n_cast]
    cast_out = refs[N_NSA_INPUTS + n_cast + 1:N_NSA_INPUTS + 2 * n_cast + 1]
    (kaug_ref, tab_ref, band_ref, sc_ref, s0_ref, s1_ref, sw0_ref, sw1_ref, m_ref, mw_ref,
     acc_ref, accw_ref, qa_ref, oc_ref, ow_ref) = refs[N_NSA_INPUTS + 2 * n_cast + 1:]

    tk = tq
    mcols = HPG * tq
    nw = WINDOW // tk
    cpt = tq // CMP_STRIDE
    band_rows = BAND_BELOW + cpt

    def group_start(fn):
        if i_lo == 0:
            pl.when(i == 0)(fn)

    @group_start
    def _():
        kaug_ref[:, 0:HEAD_DIM] = ks_ref[...]
        kaug_ref[:, HEAD_DIM:2 * HEAD_DIM] = et_ref[...]

    @group_start
    def _():
        c = lax.broadcasted_iota(jnp.int32, (tk, tq), 0)
        r = lax.broadcasted_iota(jnp.int32, (tk, tq), 1)
        far = jnp.where(r < c, 0.0, NEG)
        lane = lax.broadcasted_iota(jnp.int32, (1, BIAS_LEN), 1)
        for h in range(HPG):
            cols = slice(h * tq, (h + 1) * tq)
            f = fvec_ref[h]
            f_diag = jnp.where(lane < tq, f, NEG)
            x = pltpu.roll(jnp.broadcast_to(f_diag, (tk, BIAS_LEN)), 0, 1, stride=1, stride_axis=0)
            tab_ref[TAB_DIAG, :, cols] = x[:, 0:tq]
            x = pltpu.roll(jnp.broadcast_to(f, (tk, BIAS_LEN)), 0, 1, stride=1, stride_axis=0)
            tab_ref[TAB_SUB, :, cols] = x[:, tq:2 * tq]
            tab_ref[TAB_FAR, :, cols] = far
            for v, first in enumerate((0, -BAND_BELOW)):
                shift = (CMP_STRIDE * first + CMP_LEN - 1) % BIAS_LEN
                f_shift = pltpu.roll(f, shift, 1)
                x = pltpu.roll(jnp.broadcast_to(f_shift, (band_rows, BIAS_LEN)), 0, 1,
                               stride=CMP_STRIDE, stride_axis=0)
                band_ref[v, :, cols] = x[:, 0:tq]

    for src, dst in zip(cast_in, cast_out):
        dst[...] = src[...].astype(BF16)

    def keys(kt):
        if isinstance(kt, int):
            return slice(kt * tk, (kt + 1) * tk)
        return pl.ds(pl.multiple_of(kt * tk, tk), tk)

    class Flash:
        def __init__(self, s_refs, m_ref, acc_ref):
            self.s_refs, self.m_ref, self.acc_ref = s_refs, m_ref, acc_ref

        def init(self):
            self.m_ref[...] = jnp.full((1, mcols), NEG, F32)
            self.acc_ref[...] = jnp.zeros(self.acc_ref.shape, F32)

        def qk(self, qmat, k_ref, kt, kind, buf, exists=None):
            s = lax.dot_general(k_ref[keys(kt), :], qmat, _DN_T, preferred_element_type=F32)
            if kind is not None:
                s = s + tab_ref[kind]
            if exists is not None:
                s = s + jnp.where(exists, 0.0, NEG)
            self.s_refs[buf][...] = s
            return jnp.max(s, axis=0, keepdims=True)

        def process(self, vt_ref, kt, buf, m_tile):
            m_prev = self.m_ref[...]
            m_next = jnp.maximum(m_prev, m_tile)
            alpha = jnp.exp2(m_prev - m_next)
            p = jnp.exp2(self.s_refs[buf][...] - m_next).astype(BF16)
            self.acc_ref[...] = alpha * self.acc_ref[...] + jnp.dot(
                vt_ref[:, keys(kt)], p, preferred_element_type=F32)
            self.m_ref[...] = m_next

        def finish(self):
            return self.acc_ref[0:HEAD_DIM, :] * (1.0 / self.acc_ref[HEAD_DIM:HEAD_DIM + 1, :])

    win = Flash((sw0_ref, sw1_ref), mw_ref, accw_ref)
    sel = Flash((s0_ref, s1_ref), m_ref, acc_ref)

    assert nw == 2, "window tiles are t, t-1 (previous-tile table) and t-nw (window-edge table)"

    def front_pieces(t_hi):
        n_valid = (t_hi + 1) * tq // SEL_LEN
        return 6 + -(-n_valid // SUBLANE)

    def front(t, t_lo, t_hi, qsrc_ref, par):
        n_valid = (t_hi + 1) * tq // SEL_LEN
        nb8 = -(-n_valid // SUBLANE) * SUBLANE
        nb16 = -(-n_valid // 16) * 16
        nc = -(-(cpt * (t_hi + 1)) // LANE) * LANE
        n_win = min(t_hi, nw) + 1
        q = qsrc_ref[...]
        qs = jnp.concatenate([q[:, h * HEAD_DIM:(h + 1) * HEAD_DIM] for h in range(HPG)], axis=0)

        def win_tile(d):
            if t_lo >= d:
                return t - d, None
            return jnp.maximum(t - d, 0), t >= d

        win.init()
        mt_w0 = win.qk(qs, kw_ref, t, TAB_DIAG, 0)
        raw = lax.dot_general(kc_ref[0:nc, :], qs, _DN_T, preferred_element_type=F32)
        crow = lax.broadcasted_iota(jnp.int32, raw.shape, 0)
        sc_ref[0:nc, :] = jnp.where(crow < cpt * (t + 1), raw, NEG)
        if isinstance(t, int):
            b0 = max(cpt * t - BAND_BELOW, 0)
            sc_ref[b0:b0 + band_rows, :] += band_ref[min(t, 1)]
        else:
            assert t_lo >= 1
            band = pl.ds(pl.multiple_of(cpt * t - BAND_BELOW, SUBLANE), band_rows)
            sc_ref[band, :] += band_ref[1]
        yield

        if n_win > 1:
            kt1, ex1 = win_tile(1)
            mt_w1 = win.qk(qs, kw_ref, kt1, TAB_SUB, 1, ex1)
        win.process(vwt_ref, t, 0, mt_w0)
        yield

        sc = sc_ref[0:nc, :]
        mc = jnp.maximum(jnp.max(sc, axis=0, keepdims=True), HALF_NEG)
        pc = jnp.exp2(sc - mc)
        lc = jnp.sum(pc, axis=0, keepdims=True)
        pc = pc * jnp.where(lc > 0.0, 1.0 / lc, 0.0)
        oc_ref[par] = jnp.dot(vct_ref[:, 0:nc], pc.astype(BF16), preferred_element_type=F32)
        yield

        if n_win > 2:
            kt2, ex2 = win_tile(nw)
            mt_w2 = win.qk(qs, kw_ref, kt2, TAB_FAR, 0, ex2)
        if n_win > 1:
            win.process(vwt_ref, kt1, 1, mt_w1)
        yield

        ps = pc[:, 0:tq] + pc[:, tq:2 * tq] + pc[:, 2 * tq:3 * tq] + pc[:, 3 * tq:4 * tq]
        hi = ps.astype(BF16)
        r1 = ps - hi.astype(F32)
        mid = r1.astype(BF16)
        lo = (r1 - mid.astype(F32)).astype(BF16)
        ovl = ovl_ref[0:nb16, 0:nc]
        imp = (jnp.dot(ovl, hi, preferred_element_type=F32) + jnp.dot(ovl, mid, preferred_element_type=F32)
               + jnp.dot(ovl, lo, preferred_element_type=F32))
        jj = lax.broadcasted_iota(jnp.int32, (nb16, tq), 0)
        tt = t * tq + lax.broadcasted_iota(jnp.int32, (nb16, tq), 1)
        cur = tt >> int(math.log2(SEL_LEN))
        forced = (jj == 0) | (jj == cur) | (jj == cur - 1)
        imp = jnp.where(forced, FORCE, imp)
        imp = jnp.where(jj * SEL_LEN <= tt, imp, NEG)
        yield

        sub = SUBLANE
        ranks = []
        for j0 in range(0, nb8, sub):
            blk = imp[j0:j0 + sub, :]
            jl = j0 + lax.broadcasted_iota(jnp.int32, blk.shape, 0)
            cnt = jnp.zeros(blk.shape, jnp.int32)
            for b in range(n_valid):
                row = imp[b:b + 1, :]
                if b < j0:
                    cnt = cnt + jnp.where(row >= blk, 1, 0)
                elif b >= j0 + sub:
                    cnt = cnt + jnp.where(row > blk, 1, 0)
                else:
                    cnt = cnt + jnp.where(row > blk, 1, jnp.where(row == blk, jnp.where(jl > b, 1, 0), 0))
            ranks.append(cnt)
            yield
        rank = jnp.concatenate(ranks, axis=0)
        selb_t = jnp.where(rank < SEL_TOPK, 0.0, NEG)

        if n_win > 2:
            win.process(vwt_ref, kt2, 0, mt_w2)
        ow_ref[par] = win.finish()

        selb = jnp.concatenate([selb_t, jnp.zeros((LANE - nb8, tq), F32)], axis=0).T
        selb = selb.astype(BF16)
        qa_ref[par, :, 0:HEAD_DIM] = qs
        qa_ref[par, :, HEAD_DIM:2 * HEAD_DIM] = jnp.concatenate([selb] * HPG, axis=0)
        yield

    def selected(t, t_lo, t_hi):
        tiles = []
        for d in range(t_hi + 1):
            kind = (TAB_DIAG, TAB_SUB)[d] if d < 2 else None
            tiles.append((t - d, kind, None) if t_lo >= d else (jnp.maximum(t - d, 0), kind, t >= d))
        qa_ref[0] = qa_ref[1]
        gt = gate_ref[...].T
        for h in range(HPG):
            sl = slice(h * tq, (h + 1) * tq)
            oc_ref[0, :, sl] = (gt[3 * h:3 * h + 1, :] * oc_ref[1, :, sl]
                                + gt[3 * h + 2:3 * h + 3, :] * ow_ref[1, :, sl])
        sel.init()
        mt = sel.qk(qa_ref[0], kaug_ref, tiles[0][0], tiles[0][1], 0)
        for n, (kt, _, _) in enumerate(tiles):
            nxt = None
            if n + 1 < len(tiles):
                kt_n, kind_n, ex_n = tiles[n + 1]
                nxt = sel.qk(qa_ref[0], kaug_ref, kt_n, kind_n, (n + 1) % 2, ex_n)
            sel.process(vst_ref, kt, n % 2, mt)
            mt = nxt
            yield
        o_s = sel.finish()
        gt = gate_ref[...].T
        for h in range(HPG):
            sl = slice(h * tq, (h + 1) * tq)
            o = oc_ref[0, :, sl] + gt[3 * h + 1:3 * h + 2, :] * o_s[:, sl]
            o_ref[:, h * HEAD_DIM:(h + 1) * HEAD_DIM] = o.T.astype(BF16)
        yield

    if i_lo == 0:
        @pl.when(i == 0)
        def _():
            for _ in front(0, 0, 0, q_ref, 1):
                pass
    t_hi = min(i_hi + 1, n_tiles - 1)
    t_next = jnp.minimum(i + 1, n_tiles - 1)
    _spread(selected(i, i_lo, i_hi), front(t_next, i_lo + 1, t_hi, qn_ref, 1),
            i_hi + 2, front_pieces(t_hi))


def _nsa(q, k_all, vt_all, cmp, cmp_t, gates, fvec, ovl, e_t, cast_weights, batch, seq_len, *, tq):
    n_t = seq_len // tq
    n_steps = batch * N_KV * n_t
    ts = tq
    step_row = lambda b, g, i: ((b * N_KV + g) * n_t + i, 0)
    cast_specs = [pl.BlockSpec((w.shape[0] // n_steps, w.shape[1]), step_row) for w in cast_weights]
    n_sel = seq_len // SEL_LEN
    n_chunks = cmp.shape[2]
    mcols = HPG * tq
    band_rows = BAND_BELOW + tq // CMP_STRIDE
    qw = HPG * HEAD_DIM

    def vt_spec(first):
        return pl.BlockSpec((None, None, HEAD_DIM + ONES_ROWS, seq_len), lambda b, g, i: (b, first + g, 0, 0))

    kern = functools.partial(_nsa_kernel, n_tiles=n_t, tq=tq, n_sel=n_sel, n_cast=len(cast_weights))
    o_attn, *cast = pl.pallas_call(
        kern,
        grid=(batch, N_KV, n_t),
        in_specs=[
            pl.BlockSpec((ts, qw), lambda b, g, i: (b * n_t + i, g)),
            pl.BlockSpec((ts, qw), lambda b, g, i: (b * n_t + jnp.minimum(i + 1, n_t - 1), g)),
            pl.BlockSpec((None, None, n_chunks, HEAD_DIM), lambda b, g, i: (b, g, 0, 0)),
            pl.BlockSpec((None, None, HEAD_DIM, n_chunks), lambda b, g, i: (b, N_KV + g, 0, 0)),
            pl.BlockSpec((seq_len, HEAD_DIM), lambda b, g, i: (b, g)),
            vt_spec(0),
            pl.BlockSpec((seq_len, HEAD_DIM), lambda b, g, i: (b, N_KV + g)),
            vt_spec(N_KV),
            pl.BlockSpec((ts, LANE), lambda b, g, i: (b * n_t + i, g)),
            pl.BlockSpec((None, HPG, 1, BIAS_LEN), lambda b, g, i: (g, 0, 0, 0)),
            pl.BlockSpec((n_sel, n_chunks), lambda b, g, i: (0, 0)),
            pl.BlockSpec((seq_len, LANE), lambda b, g, i: (0, 0)),
        ] + cast_specs,
        out_specs=[pl.BlockSpec((ts, qw), lambda b, g, i: (b * n_t + i, g))] + cast_specs,
        out_shape=[jax.ShapeDtypeStruct((batch * seq_len, ATTN_WIDTH), BF16)]
        + [jax.ShapeDtypeStruct(w.shape, BF16) for w in cast_weights],
        scratch_shapes=[
            pltpu.VMEM((seq_len, 2 * HEAD_DIM), BF16),
            pltpu.VMEM((3, tq, mcols), F32),
            pltpu.VMEM((2, band_rows, mcols), F32),
            pltpu.VMEM((n_chunks, mcols), F32),
            pltpu.VMEM((tq, mcols), F32),
            pltpu.VMEM((tq, mcols), F32),
            pltpu.VMEM((tq, mcols), F32),
            pltpu.VMEM((tq, mcols), F32),
            pltpu.VMEM((1, mcols), F32),
            pltpu.VMEM((1, mcols), F32),
            pltpu.VMEM((HEAD_DIM + ONES_ROWS, mcols), F32),
            pltpu.VMEM((HEAD_DIM + ONES_ROWS, mcols), F32),
            pltpu.VMEM((2, mcols, 2 * HEAD_DIM), BF16),
            pltpu.VMEM((2, HEAD_DIM, mcols), F32),
            pltpu.VMEM((2, HEAD_DIM, mcols), F32),
        ],
        compiler_params=_params(3),
        name="nsa_attention",
    )(q, q, cmp, cmp_t, k_all, vt_all, k_all, vt_all, gates, fvec, ovl, e_t, *cast_weights)
    return o_attn, cast


OPROJ_CHUNKS = 4


def _oproj_kernel(oa_ref, ov_ref, wo_ref, x_ref, g1_ref, g2_ref, x1_ref, h2_ref):
    ka = oa_ref.shape[1]
    tm = oa_ref.shape[0]
    chunk = tm // OPROJ_CHUNKS
    for rows in (slice(k * chunk, (k + 1) * chunk) for k in range(OPROJ_CHUNKS)):
        mix = jnp.dot(oa_ref[rows, :], wo_ref[0:ka, :], preferred_element_type=F32)
        mix = mix + jnp.dot(ov_ref[rows, :], wo_ref[ka:, :], preferred_element_type=F32)
        x1 = x_ref[rows, :] + _rms(mix, g1_ref[...])
        x1_ref[rows, :] = x1
        h2_ref[rows, :] = _rms(x1, g2_ref[...]).astype(BF16)


def _oproj(o_attn, o_conv, w_o, x2, g_post, g_pre, *, tm=512):
    m = x2.shape[0]
    ka, kv = o_attn.shape[1], o_conv.shape[1]
    row = lambda i: (i, 0)
    fixed = lambda i: (0, 0)
    return pl.pallas_call(
        _oproj_kernel,
        grid=(m // tm,),
        in_specs=[
            pl.BlockSpec((tm, ka), row),
            pl.BlockSpec((tm, kv), row),
            pl.BlockSpec((ka + kv, D_MODEL), fixed),
            pl.BlockSpec((tm, D_MODEL), row),
            pl.BlockSpec((1, D_MODEL), fixed),
            pl.BlockSpec((1, D_MODEL), fixed),
        ],
        out_specs=[pl.BlockSpec((tm, D_MODEL), row), pl.BlockSpec((tm, D_MODEL), row)],
        out_shape=[jax.ShapeDtypeStruct((m, D_MODEL), F32), jax.ShapeDtypeStruct((m, D_MODEL), BF16)],
        compiler_params=_params(1),
        name="out_proj",
    )(o_attn, o_conv, w_o, x2, g_post, g_pre)


FFN_SPLIT = 2


def _ffn_kernel(h_ref, wu_ref, wd_ref, x1_hbm, g_ref, o_ref, x1_ref, sem):
    i = pl.program_id(0)
    j = pl.program_id(1)
    last = pl.num_programs(1) - 1
    tm = h_ref.shape[0]
    width = wu_ref.shape[1] // FFN_SPLIT
    x1_copy = pltpu.make_async_copy(x1_hbm.at[pl.ds(pl.multiple_of(i * tm, tm), tm), :], x1_ref, sem)

    def step(mode):
        if mode == "first":
            x1_copy.start()
        for k in range(FFN_SPLIT):
            cols = slice(k * width, (k + 1) * width)
            a = jnp.dot(h_ref[...], wu_ref[:, cols], preferred_element_type=F32)
            a = jnp.square(jnp.maximum(a, 0.0)).astype(BF16)
            if mode == "first" and k == 0:
                o_ref[...] = jnp.dot(a, wd_ref[cols, :], preferred_element_type=F32)
            elif mode != "last" or k < FFN_SPLIT - 1:
                o_ref[...] += jnp.dot(a, wd_ref[cols, :], preferred_element_type=F32)
            else:
                x1_copy.wait()
                for rows in (slice(0, tm // 2), slice(tm // 2, tm)):
                    f = o_ref[rows, :] + jnp.dot(a[rows, :], wd_ref[cols, :], preferred_element_type=F32)
                    o_ref[rows, :] = x1_ref[rows, :] + _rms(f, g_ref[...])

    pl.when(j == 0)(functools.partial(step, "first"))
    pl.when((j > 0) & (j < last))(functools.partial(step, "middle"))
    pl.when(j == last)(functools.partial(step, "last"))


def _ffn(h2, w_up, w_down, x1, g_post, *, tm=512, tf=2048):
    m = h2.shape[0]
    d_ff = w_up.shape[1]
    assert d_ff // tf >= 2
    return pl.pallas_call(
        _ffn_kernel,
        grid=(m // tm, d_ff // tf),
        in_specs=[
            pl.BlockSpec((tm, D_MODEL), lambda i, j: (i, 0)),
            pl.BlockSpec((D_MODEL, tf), lambda i, j: (0, j)),
            pl.BlockSpec((tf, D_MODEL), lambda i, j: (j, 0)),
            pl.BlockSpec(memory_space=pl.ANY),
            pl.BlockSpec((1, D_MODEL), lambda i, j: (0, 0)),
        ],
        out_specs=pl.BlockSpec((tm, D_MODEL), lambda i, j: (i, 0)),
        out_shape=jax.ShapeDtypeStruct((m, D_MODEL), F32),
        scratch_shapes=[pltpu.VMEM((tm, D_MODEL), F32), pltpu.SemaphoreType.DMA(())],
        compiler_params=_params(2),
        name="ffn",
    )(h2, w_up, w_down, x1, g_post)


def _bucket_np(dist):
    n = np.maximum(dist, 0)
    max_exact = N_BUCKETS // 2
    nf = np.maximum(n, 1).astype(np.float32)
    large = max_exact + (np.log(nf / np.float32(max_exact)) / np.float32(math.log(MAX_DIST / max_exact))
                         * np.float32(N_BUCKETS - max_exact)).astype(np.int32)
    large = np.minimum(large, N_BUCKETS - 1)
    return np.where(n < max_exact, n, large)


def _bucket_starts():
    b = _bucket_np(np.arange(4 * MAX_DIST))
    return [int(np.argmax(b == k)) for k in range(N_BUCKETS)]


def _attention_tables(rel_bias, seq_len, tq):
    starts = _bucket_starts()
    assert starts[N_BUCKETS - 1] <= CMP_STRIDE * (BAND_BELOW + 1) - (CMP_LEN - 1)
    assert starts[N_BUCKETS - 1] <= tq and 2 * tq <= BIAS_LEN // 2
    rel = (rel_bias - rel_bias[:, N_BUCKETS - 1:]) * math.log2(math.e)
    d = jnp.arange(BIAS_LEN, dtype=jnp.int32)[None, :]
    fvec = jnp.broadcast_to(rel[:, 0:1], (N_HEADS, BIAS_LEN))
    for k in range(1, N_BUCKETS):
        fvec = jnp.where(d >= starts[k], rel[:, k:k + 1], fvec)
    fvec = jnp.where(d < BIAS_LEN // 2, fvec, NEG).astype(F32).reshape(N_KV, HPG, 1, BIAS_LEN)

    n_chunks = seq_len // CMP_STRIDE
    n_cmp = (seq_len - CMP_LEN) // CMP_STRIDE + 1
    n_sel = seq_len // SEL_LEN
    ci = np.arange(n_chunks)[None, :] * CMP_STRIDE
    sj = np.arange(n_sel)[:, None] * SEL_LEN
    ovl = ((ci < sj + SEL_LEN) & (ci + CMP_LEN > sj) & (np.arange(n_chunks)[None, :] < n_cmp))
    e_t = (np.arange(seq_len)[:, None] // SEL_LEN == np.arange(LANE)[None, :])
    return fvec, jnp.asarray(ovl, BF16), jnp.asarray(e_t, BF16)


def kernel(x, w_in, pe_cmp, w_cmp_k1, w_cmp_k2, w_cmp_v1, w_cmp_v2, conv_w, rel_bias, w_o, w_up, w_down,
           g_pre_mix, g_post_mix, g_pre_ffn, g_post_ffn):
    batch, seq_len, _ = x.shape
    depth = w_in.shape[0]
    tq = 256
    fvec, ovl, e_t = _attention_tables(rel_bias, seq_len, tq)
    x2 = x.reshape(batch * seq_len, D_MODEL)
    for l in range(depth):
        wl = jnp.swapaxes(w_in[l], 0, 1)
        g1 = g_pre_mix[l].reshape(1, D_MODEL)

        q, k_all, c_in, vt_all, gates, o_conv = _in_proj(x2, g1, wl, conv_w[l], batch, seq_len)

        cmp, cmp_t = _compress(c_in, pe_cmp[l], (w_cmp_k1[l], w_cmp_v1[l]), (w_cmp_k2[l], w_cmp_v2[l]),
                               batch, seq_len)

        o_attn, (wo_b, wup_b, wdown_b) = _nsa(q, k_all, vt_all, cmp, cmp_t, gates, fvec, ovl, e_t,
                                              [w_o[l], w_up[l], w_down[l]], batch, seq_len, tq=tq)

        x1, h2 = _oproj(o_attn, o_conv, wo_b, x2,
                        g_post_mix[l].reshape(1, D_MODEL), g_pre_ffn[l].reshape(1, D_MODEL))
        x2 = _ffn(h2, wup_b, wdown_b, x1, g_post_ffn[l].reshape(1, D_MODEL))
    return x2.reshape(batch, seq_len, D_MODEL)
```
